```python
import jax
import jax.numpy as jnp
from jax import lax
import numpy as np

D_MODEL = 2048
BATCH = 8
SEQ = 2048
DEPTH = 1

N_META = 16
POOL_GROUPS = 4
POOL_WINDOWS = (2, 4, 8, 16)
POOL_WIDTH = D_MODEL // 2
POOL_GROUP_DIM = POOL_WIDTH // POOL_GROUPS
DN_HEADS = 16
DN_HEAD_DIM = 128
DN_WIDTH = DN_HEADS * DN_HEAD_DIM
CONV_WIDTH = 4
CHUNK = 64
NORM_EPS = 1e-6
IN_SPLIT_SIZES = (POOL_WIDTH, POOL_WIDTH, DN_WIDTH, DN_WIDTH, DN_WIDTH, DN_WIDTH, DN_HEADS, DN_HEADS, D_MODEL, D_MODEL)
IN_COLS = 2 * POOL_WIDTH + 4 * DN_WIDTH + 2 * DN_HEADS + 2 * D_MODEL

kernel_name = "hybrid_pool_gated_deltanet_block"


def rmsnorm(x, w):
    xf = x.astype(jnp.float32)
    y = xf * lax.rsqrt(jnp.mean(xf * xf, axis=-1, keepdims=True) + NORM_EPS)
    return (y * w.astype(jnp.float32)).astype(x.dtype)


def l2norm(x):
    return x * lax.rsqrt(jnp.sum(x * x, axis=-1, keepdims=True) + NORM_EPS)


def causal_multiscale_pool(u, mix_w, scale):
    Bsz, L, _ = u.shape
    uf = u.astype(jnp.float32)
    csum = jnp.concatenate([jnp.zeros((Bsz, 1, POOL_WIDTH), jnp.float32), jnp.cumsum(uf, axis=1)], axis=1)
    t = jnp.arange(1, L + 1)
    pooled = []
    for gi, w in enumerate(POOL_WINDOWS):
        c = csum[..., gi * POOL_GROUP_DIM:(gi + 1) * POOL_GROUP_DIM]
        lag = jnp.pad(c, ((0, 0), (w, 0), (0, 0)))[:, :L + 1]
        cnt = jnp.minimum(t, w).astype(jnp.float32)[None, :, None]
        pooled.append((c[:, 1:] - lag[:, 1:]) / cnt)
    pooled = jnp.concatenate(pooled, axis=-1) - uf
    pooled = pooled.reshape(Bsz, L, POOL_GROUPS, POOL_GROUP_DIM)
    mixed = jnp.einsum('blgc,gcd->blgd', pooled, mix_w.astype(jnp.float32)).reshape(Bsz, L, POOL_WIDTH)
    return mixed * scale.astype(jnp.float32)


def causal_depthwise_conv_silu(x, w):
    K = w.shape[0]
    L = x.shape[1]
    xp = jnp.pad(x, ((0, 0), (K - 1, 0), (0, 0)))
    y = xp[:, K - 1:K - 1 + L] * w[K - 1]
    for kk in range(K - 1):
        y = y + xp[:, kk:kk + L] * w[kk]
    return jax.nn.silu(y)


def chunk_gated_delta_rule(q, k, v, beta, g):
    Bsz, Lp, H, Dk = q.shape
    Dv = v.shape[-1]
    N = Lp // CHUNK

    def to_chunks(t):
        return jnp.moveaxis(t.reshape((Bsz, N, CHUNK) + t.shape[2:]), 3, 1)

    q, k, v, beta, g = to_chunks(q), to_chunks(k), to_chunks(v), to_chunks(beta), to_chunks(g)
    gcum = jnp.cumsum(g, axis=-1)
    causal = jnp.tril(jnp.ones((CHUNK, CHUNK), bool))
    strict = jnp.tril(jnp.ones((CHUNK, CHUNK), bool), -1)
    diff = gcum[..., :, None] - gcum[..., None, :]
    decay = jnp.where(causal, jnp.exp(jnp.where(causal, diff, 0.0)), 0.0)
    k_beta = k * beta[..., None]
    v_beta = v * beta[..., None]
    lmat = jnp.where(strict, jnp.einsum('bhncd,bhnsd->bhncs', k_beta, k) * decay, 0.0)
    eye = jnp.eye(CHUNK, dtype=jnp.float32)
    tmat = lax.linalg.triangular_solve(eye + lmat, jnp.broadcast_to(eye, lmat.shape), left_side=True, lower=True)
    u_c = jnp.einsum('bhncs,bhnsd->bhncd', tmat, v_beta)
    w_c = jnp.einsum('bhncs,bhnsd->bhncd', tmat, k_beta * jnp.exp(gcum)[..., None])
    qk = jnp.where(causal, jnp.einsum('bhncd,bhnsd->bhncs', q, k) * decay, 0.0)
    q_dec = q * jnp.exp(gcum)[..., None]
    k_dec = k * jnp.exp(gcum[..., -1:] - gcum)[..., None]
    g_last = jnp.exp(gcum[..., -1])

    def step(S, xs):
        u_i, w_i, q_i, k_i, qk_i, gl_i = xs
        v_new = u_i - jnp.einsum('bhcd,bhde->bhce', w_i, S)
        o_i = jnp.einsum('bhcd,bhde->bhce', q_i, S) + jnp.einsum('bhcs,bhse->bhce', qk_i, v_new)
        S = S * gl_i[..., None, None] + jnp.einsum('bhcd,bhce->bhde', k_i, v_new)
        return S, o_i

    xs = tuple(jnp.moveaxis(t, 2, 0) for t in (u_c, w_c, q_dec, k_dec, qk, g_last))
    S0 = jnp.zeros((Bsz, H, Dk, Dv), jnp.float32)
    _, o = lax.scan(step, S0, xs)
    o = jnp.transpose(o, (1, 0, 3, 2, 4))
    return o.reshape(Bsz, Lp, H, Dv)


def gated_deltanet_branch(q, k, v, z, b, a, conv_w, A_log, dt_bias, norm_w):
    Bsz, L, _ = q.shape
    qkv = causal_depthwise_conv_silu(jnp.concatenate([q, k, v], axis=-1), conv_w)
    q, k, v = jnp.split(qkv.astype(jnp.float32), 3, axis=-1)
    heads = lambda t: t.reshape(Bsz, L, DN_HEADS, DN_HEAD_DIM)
    q = l2norm(heads(q)) * (DN_HEAD_DIM ** -0.5)
    k = l2norm(heads(k))
    v = heads(v)
    beta = jax.nn.sigmoid(b.astype(jnp.float32))
    g = -jnp.exp(A_log.astype(jnp.float32)) * jax.nn.softplus(a.astype(jnp.float32) + dt_bias.astype(jnp.float32))
    pad = (-N_META) % CHUNK
    front = lambda t: jnp.pad(t, ((0, 0), (pad, 0)) + ((0, 0),) * (t.ndim - 2))
    o = chunk_gated_delta_rule(front(q), front(k), front(v), front(beta), front(g))[:, pad:]
    o = rmsnorm(o, norm_w) * jax.nn.silu(heads(z.astype(jnp.float32)))
    return o.reshape(Bsz, L, DN_WIDTH)


def hybrid_layer(h, norm_w, w_in, conv_w, A_log, dt_bias, pool_mix, pool_scale, dn_norm_w, w_pool_out, w_dn_out, w_o):
    xn = rmsnorm(h, norm_w)
    proj = xn @ w_in
    splits = np.cumsum(IN_SPLIT_SIZES[:-1]).tolist()
    u_pool, z_pool, q, k, v, z_dn, b, a, gate_pool, gate_dn = jnp.split(proj, splits, axis=-1)
    y_pool = (causal_multiscale_pool(u_pool, pool_mix, pool_scale) * jax.nn.silu(z_pool.astype(jnp.float32))).astype(h.dtype)
    y_dn = gated_deltanet_branch(q, k, v, z_dn, b, a, conv_w, A_log, dt_bias, dn_norm_w).astype(h.dtype)
    merged = jax.nn.sigmoid(gate_pool) * (y_pool @ w_pool_out) + jax.nn.sigmoid(gate_dn) * (y_dn @ w_dn_out)
    return h + merged @ w_o


def _fwd_setup_inputs(seed: int = 0) -> dict:
    key = jax.random.key(seed)
    ks = jax.random.split(key, 14)
    nrm = jax.random.normal
    x = nrm(ks[0], (BATCH, SEQ, D_MODEL), jnp.float32)
    meta_tokens = nrm(ks[1], (N_META, D_MODEL), jnp.float32)
    norm_w = 1.0 + 0.02 * nrm(ks[2], (DEPTH, D_MODEL), jnp.float32)
    w_in = nrm(ks[3], (DEPTH, D_MODEL, IN_COLS), jnp.float32) * D_MODEL ** -0.5
    conv_w = nrm(ks[4], (DEPTH, CONV_WIDTH, 3 * DN_WIDTH), jnp.float32) * CONV_WIDTH ** -0.5
    A_log = jnp.log(jax.random.uniform(ks[5], (DEPTH, DN_HEADS), jnp.float32, minval=1.0, maxval=16.0))
    dt = jnp.exp(jax.random.uniform(ks[6], (DEPTH, DN_HEADS), jnp.float32, minval=float(np.log(1e-3)), maxval=float(np.log(1e-1))))
    dt_bias = dt + jnp.log(-jnp.expm1(-dt))
    pool_mix = nrm(ks[7], (DEPTH, POOL_GROUPS, POOL_GROUP_DIM, POOL_GROUP_DIM), jnp.float32) * POOL_GROUP_DIM ** -0.5
    pool_scale = 1.0 + 0.02 * nrm(ks[8], (DEPTH, POOL_WIDTH), jnp.float32)
    dn_norm_w = 1.0 + 0.02 * nrm(ks[9], (DEPTH, DN_HEAD_DIM), jnp.float32)
    w_pool_out = nrm(ks[10], (DEPTH, POOL_WIDTH, D_MODEL), jnp.float32) * POOL_WIDTH ** -0.5
    w_dn_out = nrm(ks[11], (DEPTH, DN_WIDTH, D_MODEL), jnp.float32) * DN_WIDTH ** -0.5
    w_o = nrm(ks[12], (DEPTH, D_MODEL, D_MODEL), jnp.float32) * D_MODEL ** -0.5
    final_norm_w = 1.0 + 0.02 * nrm(ks[13], (D_MODEL,), jnp.float32)
    return {"x": x, "meta_tokens": meta_tokens, "norm_w": norm_w, "w_in": w_in, "conv_w": conv_w,
            "A_log": A_log, "dt_bias": dt_bias, "pool_mix": pool_mix, "pool_scale": pool_scale,
            "dn_norm_w": dn_norm_w, "w_pool_out": w_pool_out, "w_dn_out": w_dn_out, "w_o": w_o,
            "final_norm_w": final_norm_w}


def _fwd_reference(x, meta_tokens, norm_w, w_in, conv_w, A_log, dt_bias, pool_mix, pool_scale, dn_norm_w, w_pool_out, w_dn_out, w_o, final_norm_w):
    Bsz = x.shape[0]
    meta = jnp.broadcast_to(meta_tokens.astype(x.dtype)[None], (Bsz, N_META, D_MODEL))
    h = jnp.concatenate([meta, x], axis=1)
    for i in range(DEPTH):
        h = hybrid_layer(h, norm_w[i], w_in[i], conv_w[i], A_log[i], dt_bias[i], pool_mix[i], pool_scale[i],
                         dn_norm_w[i], w_pool_out[i], w_dn_out[i], w_o[i])
    return rmsnorm(h[:, N_META:], final_norm_w)


import jax as _jax
import jax.numpy as _jnp

TWIN_FORMAT = 'train_step'
FWD_PARAMS = ['x', 'meta_tokens', 'norm_w', 'w_in', 'conv_w', 'A_log', 'dt_bias', 'pool_mix', 'pool_scale', 'dn_norm_w', 'w_pool_out', 'w_dn_out', 'w_o', 'final_norm_w']
TWIN_WEIGHTS = ['meta_tokens', 'norm_w', 'w_in', 'conv_w', 'A_log', 'dt_bias', 'pool_mix', 'pool_scale', 'dn_norm_w', 'w_pool_out', 'w_dn_out', 'w_o', 'final_norm_w']
TWIN_DIFF_INPUT = 'x'
TWIN_INPUTS = ['x', 'meta_tokens', 'norm_w', 'w_in', 'conv_w', 'A_log', 'dt_bias', 'pool_mix', 'pool_scale', 'dn_norm_w', 'w_pool_out', 'w_dn_out', 'w_o', 'final_norm_w', 'loss_target', 'm_meta_tokens', 'm_norm_w', 'm_w_in', 'm_conv_w', 'm_A_log', 'm_dt_bias', 'm_pool_mix', 'm_pool_scale', 'm_dn_norm_w', 'm_w_pool_out', 'm_w_dn_out', 'm_w_o', 'm_final_norm_w', 'v_meta_tokens', 'v_norm_w', 'v_w_in', 'v_conv_w', 'v_A_log', 'v_dt_bias', 'v_pool_mix', 'v_pool_scale', 'v_dn_norm_w', 'v_w_pool_out', 'v_w_dn_out', 'v_w_o', 'v_final_norm_w']
TWIN_OUTPUTS = ['loss', 'grad_x', 'grad_meta_tokens', 'grad_norm_w', 'grad_w_in', 'grad_conv_w', 'grad_A_log', 'grad_dt_bias', 'grad_pool_mix', 'grad_pool_scale', 'grad_dn_norm_w', 'grad_w_pool_out', 'grad_w_dn_out', 'grad_w_o', 'grad_final_norm_w', 'delta_meta_tokens', 'delta_norm_w', 'delta_w_in', 'delta_conv_w', 'delta_A_log', 'delta_dt_bias', 'delta_pool_mix', 'delta_pool_scale', 'delta_dn_norm_w', 'delta_w_pool_out', 'delta_w_dn_out', 'delta_w_o', 'delta_final_norm_w', 'new_m_meta_tokens', 'new_m_norm_w', 'new_m_w_in', 'new_m_conv_w', 'new_m_A_log', 'new_m_dt_bias', 'new_m_pool_mix', 'new_m_pool_scale', 'new_m_dn_norm_w', 'new_m_w_pool_out', 'new_m_w_dn_out', 'new_m_w_o', 'new_m_final_norm_w', 'new_v_meta_tokens', 'new_v_norm_w', 'new_v_w_in', 'new_v_conv_w', 'new_v_A_log', 'new_v_dt_bias', 'new_v_pool_mix', 'new_v_pool_scale', 'new_v_dn_norm_w', 'new_v_w_pool_out', 'new_v_w_dn_out', 'new_v_w_o', 'new_v_final_norm_w']
TWIN_LEAF_KINDS = {'loss': 'loss', 'grad_x': 'grad_x', 'grad_meta_tokens': 'grad_w', 'grad_norm_w': 'grad_w', 'grad_w_in': 'grad_w', 'grad_conv_w': 'grad_w', 'grad_A_log': 'grad_w', 'grad_dt_bias': 'grad_w', 'grad_pool_mix': 'grad_w', 'grad_pool_scale': 'grad_w', 'grad_dn_norm_w': 'grad_w', 'grad_w_pool_out': 'grad_w', 'grad_w_dn_out': 'grad_w', 'grad_w_o': 'grad_w', 'grad_final_norm_w': 'grad_w', 'delta_meta_tokens': 'delta_w', 'delta_norm_w': 'delta_w', 'delta_w_in': 'delta_w', 'delta_conv_w': 'delta_w', 'delta_A_log': 'delta_w', 'delta_dt_bias': 'delta_w', 'delta_pool_mix': 'delta_w', 'delta_pool_scale': 'delta_w', 'delta_dn_norm_w': 'delta_w', 'delta_w_pool_out': 'delta_w', 'delta_w_dn_out': 'delta_w', 'delta_w_o': 'delta_w', 'delta_final_norm_w': 'delta_w', 'new_m_meta_tokens': 'new_m', 'new_m_norm_w': 'new_m', 'new_m_w_in': 'new_m', 'new_m_conv_w': 'new_m', 'new_m_A_log': 'new_m', 'new_m_dt_bias': 'new_m', 'new_m_pool_mix': 'new_m', 'new_m_pool_scale': 'new_m', 'new_m_dn_norm_w': 'new_m', 'new_m_w_pool_out': 'new_m', 'new_m_w_dn_out': 'new_m', 'new_m_w_o': 'new_m', 'new_m_final_norm_w': 'new_m', 'new_v_meta_tokens': 'new_v', 'new_v_norm_w': 'new_v', 'new_v_w_in': 'new_v', 'new_v_conv_w': 'new_v', 'new_v_A_log': 'new_v', 'new_v_dt_bias': 'new_v', 'new_v_pool_mix': 'new_v', 'new_v_pool_scale': 'new_v', 'new_v_dn_norm_w': 'new_v', 'new_v_w_pool_out': 'new_v', 'new_v_w_dn_out': 'new_v', 'new_v_w_o': 'new_v', 'new_v_final_norm_w': 'new_v'}


def _forward(args):
    return _fwd_reference(*[args[k] for k in FWD_PARAMS])


def _output_shape():
    out = _jax.eval_shape(lambda: _forward(_fwd_setup_inputs(0)))
    return out.shape, out.dtype

N_MICROBATCH = 1
ADAM_LR = 0.001
ADAM_B1 = 0.9
ADAM_B2 = 0.999
ADAM_EPS = 1e-08
ADAM_WD = 0.01
ADAM_STEP = 10
PER_EXAMPLE_BATCH_AXIS = {'x': 0, 'loss_target': 0}
SHARED_INPUTS = []
_WEIGHT_DTYPES = {'meta_tokens': _jnp.float32, 'norm_w': _jnp.float32, 'w_in': _jnp.float32, 'conv_w': _jnp.float32, 'A_log': _jnp.float32, 'dt_bias': _jnp.float32, 'pool_mix': _jnp.float32, 'pool_scale': _jnp.float32, 'dn_norm_w': _jnp.float32, 'w_pool_out': _jnp.float32, 'w_dn_out': _jnp.float32, 'w_o': _jnp.float32, 'final_norm_w': _jnp.float32}
MOMENT_SCALE = {'meta_tokens': 8.103213e-04, 'norm_w': 4.026005e-02, 'w_in': 1.506255e-02, 'conv_w': 1.392781e-02, 'A_log': 5.446929e-02, 'dt_bias': 5.238225e-02, 'pool_mix': 2.308760e-02, 'pool_scale': 2.384473e-02, 'dn_norm_w': 7.011184e-02, 'w_pool_out': 1.636036e-02, 'w_dn_out': 1.792424e-02, 'w_o': 2.427982e-02, 'final_norm_w': 7.998048e+00}


def _to_microbatches(a, axis):
    t = _jnp.moveaxis(a, axis, 0)
    t = t.reshape((N_MICROBATCH, t.shape[0] // N_MICROBATCH) + t.shape[1:])
    return _jnp.moveaxis(t, 1, axis + 1)


def setup_inputs(seed: int = 0) -> dict:
    inp = _fwd_setup_inputs(seed)
    key = _jax.random.fold_in(_jax.random.key(seed), 7919)
    shape, _ = _output_shape()
    out = dict(inp)
    out["loss_target"] = _jax.random.normal(_jax.random.fold_in(key, 0), shape, _jnp.float32)
    for i, name in enumerate(TWIN_WEIGHTS):
        w = inp[name].astype(_jnp.float32)
        if MOMENT_SCALE is None:
            s = _jnp.sqrt(_jnp.mean(_jnp.square(w)) + 1e-30)
        else:
            s = MOMENT_SCALE[name]
        km, kv = _jax.random.split(_jax.random.fold_in(key, i + 1))
        out[name] = w
        out["m_" + name] = s * _jax.random.normal(km, w.shape, _jnp.float32)
        out["v_" + name] = (s * s) * _jax.random.uniform(kv, w.shape, _jnp.float32, 0.5, 1.5)
    if N_MICROBATCH > 1:
        for name, axis in PER_EXAMPLE_BATCH_AXIS.items():
            out[name] = _to_microbatches(out[name], axis)
    return {'x': out['x'], 'meta_tokens': out['meta_tokens'], 'norm_w': out['norm_w'], 'w_in': out['w_in'], 'conv_w': out['conv_w'], 'A_log': out['A_log'], 'dt_bias': out['dt_bias'], 'pool_mix': out['pool_mix'], 'pool_scale': out['pool_scale'], 'dn_norm_w': out['dn_norm_w'], 'w_pool_out': out['w_pool_out'], 'w_dn_out': out['w_dn_out'], 'w_o': out['w_o'], 'final_norm_w': out['final_norm_w'], 'loss_target': out['loss_target'], 'm_meta_tokens': out['m_meta_tokens'], 'm_norm_w': out['m_norm_w'], 'm_w_in': out['m_w_in'], 'm_conv_w': out['m_conv_w'], 'm_A_log': out['m_A_log'], 'm_dt_bias': out['m_dt_bias'], 'm_pool_mix': out['m_pool_mix'], 'm_pool_scale': out['m_pool_scale'], 'm_dn_norm_w': out['m_dn_norm_w'], 'm_w_pool_out': out['m_w_pool_out'], 'm_w_dn_out': out['m_w_dn_out'], 'm_w_o': out['m_w_o'], 'm_final_norm_w': out['m_final_norm_w'], 'v_meta_tokens': out['v_meta_tokens'], 'v_norm_w': out['v_norm_w'], 'v_w_in': out['v_w_in'], 'v_conv_w': out['v_conv_w'], 'v_A_log': out['v_A_log'], 'v_dt_bias': out['v_dt_bias'], 'v_pool_mix': out['v_pool_mix'], 'v_pool_scale': out['v_pool_scale'], 'v_dn_norm_w': out['v_dn_norm_w'], 'v_w_pool_out': out['v_w_pool_out'], 'v_w_dn_out': out['v_w_dn_out'], 'v_w_o': out['v_w_o'], 'v_final_norm_w': out['v_final_norm_w']}


def _loss(weights, diff, rest, loss_target):
    with _jax.named_scope("forward"):
        args = {**rest, TWIN_DIFF_INPUT: diff, **{k: w.astype(_WEIGHT_DTYPES[k]) for k, w in weights.items()}}
        y = _forward(args)
    with _jax.named_scope("loss_head"):
        err = _jnp.square(y.astype(_jnp.float32) - loss_target)
        return 0.5 * _jnp.sum(_jnp.mean(err, axis=-1)) if err.ndim else 0.5 * err


def _adamw(w, g, m, v):
    m = ADAM_B1 * m + (1.0 - ADAM_B1) * g
    v = ADAM_B2 * v + (1.0 - ADAM_B2) * _jnp.square(g)
    m_hat = m / (1.0 - ADAM_B1 ** ADAM_STEP)
    v_hat = v / (1.0 - ADAM_B2 ** ADAM_STEP)
    delta = -ADAM_LR * (m_hat / (_jnp.sqrt(v_hat) + ADAM_EPS) + ADAM_WD * w)
    return delta, m, v


def reference(x, meta_tokens, norm_w, w_in, conv_w, A_log, dt_bias, pool_mix, pool_scale, dn_norm_w, w_pool_out, w_dn_out, w_o, final_norm_w, loss_target, m_meta_tokens, m_norm_w, m_w_in, m_conv_w, m_A_log, m_dt_bias, m_pool_mix, m_pool_scale, m_dn_norm_w, m_w_pool_out, m_w_dn_out, m_w_o, m_final_norm_w, v_meta_tokens, v_norm_w, v_w_in, v_conv_w, v_A_log, v_dt_bias, v_pool_mix, v_pool_scale, v_dn_norm_w, v_w_pool_out, v_w_dn_out, v_w_o, v_final_norm_w):
    given = dict(x=x, meta_tokens=meta_tokens, norm_w=norm_w, w_in=w_in, conv_w=conv_w, A_log=A_log, dt_bias=dt_bias, pool_mix=pool_mix, pool_scale=pool_scale, dn_norm_w=dn_norm_w, w_pool_out=w_pool_out, w_dn_out=w_dn_out, w_o=w_o, final_norm_w=final_norm_w, loss_target=loss_target, m_meta_tokens=m_meta_tokens, m_norm_w=m_norm_w, m_w_in=m_w_in, m_conv_w=m_conv_w, m_A_log=m_A_log, m_dt_bias=m_dt_bias, m_pool_mix=m_pool_mix, m_pool_scale=m_pool_scale, m_dn_norm_w=m_dn_norm_w, m_w_pool_out=m_w_pool_out, m_w_dn_out=m_w_dn_out, m_w_o=m_w_o, m_final_norm_w=m_final_norm_w, v_meta_tokens=v_meta_tokens, v_norm_w=v_norm_w, v_w_in=v_w_in, v_conv_w=v_conv_w, v_A_log=v_A_log, v_dt_bias=v_dt_bias, v_pool_mix=v_pool_mix, v_pool_scale=v_pool_scale, v_dn_norm_w=v_dn_norm_w, v_w_pool_out=v_w_pool_out, v_w_dn_out=v_w_dn_out, v_w_o=v_w_o, v_final_norm_w=v_final_norm_w)
    weights = {n: given[n] for n in TWIN_WEIGHTS}
    shared = {n: given[n] for n in SHARED_INPUTS}
    per_example = {n: given[n] for n in ['x']}
    grad_fn = _jax.value_and_grad(_loss, argnums=(0, 1))

    def one_microbatch(ex, loss_target):
        ex = dict(ex)
        diff = ex.pop(TWIN_DIFF_INPUT)
        return grad_fn(weights, diff, {**shared, **ex}, loss_target)

    if N_MICROBATCH == 1:
        loss, (grad_w, grad_x) = one_microbatch(per_example, given["loss_target"])
    else:
        def body(carry, xs):
            loss_sum, grad_sum = carry
            l_k, (gw_k, gx_k) = one_microbatch(xs[0], xs[1])
            with _jax.named_scope("update"):
                return (loss_sum + l_k, _jax.tree.map(_jnp.add, grad_sum, gw_k)), gx_k

        init = (_jnp.zeros((), _jnp.float32), _jax.tree.map(_jnp.zeros_like, weights))
        (loss, grad_w), grad_x = _jax.lax.scan(body, init, (per_example, given["loss_target"]))
    with _jax.named_scope("update"):
        delta_w, new_m, new_v = {}, {}, {}
        for n in TWIN_WEIGHTS:
            delta_w[n], new_m[n], new_v[n] = _adamw(weights[n], grad_w[n], given["m_" + n], given["v_" + n])
    return (loss, grad_x, *[grad_w[n] for n in TWIN_WEIGHTS], *[delta_w[n] for n in TWIN_WEIGHTS],
            *[new_m[n] for n in TWIN_WEIGHTS], *[new_v[n] for n in TWIN_WEIGHTS])
```

```python
import functools

import jax
import jax.numpy as jnp
from jax import lax
from jax.experimental import pallas as pl
from jax.experimental.pallas import tpu as pltpu

F32 = jnp.float32
BF16 = jnp.bfloat16
HIGHEST = lax.Precision.HIGHEST
MESH = pl.DeviceIdType.MESH

CHUNK = 64
NORM_EPS = 1e-6
POOL_WINDOWS = (2, 4, 8, 16)
ADAM_LR, ADAM_B1, ADAM_B2, ADAM_EPS, ADAM_WD, ADAM_STEP = 0.001, 0.9, 0.999, 1e-08, 0.01, 10
N_DEV = 8
LANES = 128
PACK_COLS = 1024
VMEM_LIMIT = 48 * 1024 * 1024

NN = (((1,), (0,)), ((), ()))
NT = (((1,), (1,)), ((), ()))
TN = (((0,), (0,)), ((), ()))


def _call(body, **kw):
    return pl.pallas_call(body, **kw)


def _params(sem=None):
    return pltpu.CompilerParams(dimension_semantics=sem, vmem_limit_bytes=VMEM_LIMIT)


def _tile(n, pref, align):
    for d in range(min(pref, n), 0, -1):
        if n % d == 0 and d % align == 0:
            return d
    return n


def _dot(a, b, dims=NN, precision=None):
    return lax.dot_general(a, b, dims, precision=precision, preferred_element_type=F32)


def _sigmoid(x):
    return 1.0 / (1.0 + jnp.exp(-x))


def _silu(x):
    return x * _sigmoid(x)


def _softplus(x):
    return jnp.maximum(x, 0.0) + jnp.log(1.0 + jnp.exp(-jnp.abs(x)))


def _rmsnorm(x, w):
    return x * lax.rsqrt(jnp.mean(x * x, axis=-1, keepdims=True) + NORM_EPS) * w


def _shift_down(x, j, row):
    if j == 0:
        return x
    return jnp.where(row >= j, pltpu.roll(x, j, 0), 0.0)


def _shift_up(x, j, row):
    if j == 0:
        return x
    n = x.shape[0]
    return jnp.where(row < n - j, pltpu.roll(x, n - j, 0), 0.0)


def _matmul(a, b, dims, out_dtype, tm, tn, tk, name):
    ta = dims == TN
    tb = dims == NT
    m, kdim = (a.shape[1], a.shape[0]) if ta else a.shape
    n = b.shape[0] if tb else b.shape[1]
    tm, tn, tk = _tile(m, tm, 8), _tile(n, tn, LANES), _tile(kdim, tk, LANES if not ta else 16)
    nk = kdim // tk

    def body(a_ref, b_ref, o_ref, *scratch):
        part = _dot(a_ref[...].astype(BF16), b_ref[...].astype(BF16), dims)
        if nk == 1:
            o_ref[...] = part.astype(o_ref.dtype)
            return
        acc_ref, = scratch
        k = pl.program_id(2)

        @pl.when(k == 0)
        def _():
            acc_ref[...] = part

        @pl.when(k > 0)
        def _():
            acc_ref[...] += part

        @pl.when(k == nk - 1)
        def _():
            o_ref[...] = acc_ref[...].astype(o_ref.dtype)

    a_spec = pl.BlockSpec((tk, tm), lambda i, j, k: (k, i)) if ta else pl.BlockSpec((tm, tk), lambda i, j, k: (i, k))
    b_spec = pl.BlockSpec((tn, tk), lambda i, j, k: (j, k)) if tb else pl.BlockSpec((tk, tn), lambda i, j, k: (k, j))
    return _call(
        body, name=name, grid=(m // tm, n // tn, nk),
        in_specs=[a_spec, b_spec], out_specs=pl.BlockSpec((tm, tn), lambda i, j, k: (i, j)),
        out_shape=jax.ShapeDtypeStruct((m, n), out_dtype),
        scratch_shapes=[] if nk == 1 else [pltpu.VMEM((tm, tn), F32)],
        compiler_params=_params(("parallel", "parallel", "arbitrary")),
    )(a, b)


def _norm_in_fwd(h0, w):
    lp, d = h0.shape
    tr = _tile(lp, 264, 16)

    def body(h_ref, w_ref, o_ref):
        o_ref[...] = _rmsnorm(h_ref[...], w_ref[...]).astype(BF16)

    return _call(
        body, name="norm_in_fwd", grid=(lp // tr,),
        in_specs=[pl.BlockSpec((tr, d), lambda i: (i, 0)), pl.BlockSpec((1, d), lambda i: (0, 0))],
        out_specs=pl.BlockSpec((tr, d), lambda i: (i, 0)),
        out_shape=jax.ShapeDtypeStruct((lp, d), BF16), compiler_params=_params(("parallel",)),
    )(h0, w)


def _norm_in_bwd(h0, w, dxn_a, dxn_b, dh1):
    lp, d = h0.shape
    tr = _tile(lp, 264, 8)

    def body(h_ref, w_ref, da_ref, db_ref, dh1_ref, dh_ref, dw_ref):
        _, vjp = jax.vjp(_rmsnorm, h_ref[...], w_ref[...])
        dh, dw = vjp(da_ref[...] + db_ref[...])
        dh_ref[...] = dh + dh1_ref[...]

        @pl.when(pl.program_id(0) == 0)
        def _():
            dw_ref[...] = jnp.zeros_like(dw_ref)

        dw_ref[...] += dw

    row = pl.BlockSpec((tr, d), lambda i: (i, 0))
    vec = pl.BlockSpec((1, d), lambda i: (0, 0))
    return _call(
        body, name="norm_in_bwd", grid=(lp // tr,),
        in_specs=[row, vec, row, row, row], out_specs=[row, vec],
        out_shape=[jax.ShapeDtypeStruct((lp, d), F32), jax.ShapeDtypeStruct((1, d), F32)],
        compiler_params=_params(("arbitrary",)),
    )(h0, w, dxn_a, dxn_b, dh1)


def _final_loss(h0, mo, fw, tgt, x0):
    lp, d = h0.shape
    tr = _tile(lp, 264, 8)

    def body(h_ref, mo_ref, fw_ref, t_ref, dh_ref, dw_ref, loss_ref):
        i = pl.program_id(0)
        row = i * tr + lax.broadcasted_iota(jnp.int32, (tr, 1), 0)
        mask = jnp.where(row >= x0, 1.0, 0.0).astype(F32)
        tgt_v = t_ref[...]

        def loss_fn(h1, w):
            err = _rmsnorm(h1, w) - tgt_v
            return 0.5 * jnp.sum(jnp.mean(err * err, axis=-1, keepdims=True) * mask, axis=0, keepdims=True)

        loss, vjp = jax.vjp(loss_fn, h_ref[...] + mo_ref[...], fw_ref[...])
        dh, dw = vjp(jnp.ones((1, 1), F32))
        dh_ref[...] = dh

        @pl.when(i == 0)
        def _():
            dw_ref[...] = jnp.zeros_like(dw_ref)
            loss_ref[...] = jnp.zeros_like(loss_ref)

        dw_ref[...] += dw
        loss_ref[...] += jnp.broadcast_to(loss, loss_ref.shape)

    row_spec = pl.BlockSpec((tr, d), lambda i: (i, 0))
    vec = pl.BlockSpec((1, d), lambda i: (0, 0))
    return _call(
        body, name="final_loss", grid=(lp // tr,),
        in_specs=[row_spec, row_spec, vec, row_spec],
        out_specs=[row_spec, vec, pl.BlockSpec((8, LANES), lambda i: (0, 0))],
        out_shape=[jax.ShapeDtypeStruct((lp, d), F32), jax.ShapeDtypeStruct((1, d), F32), jax.ShapeDtypeStruct((8, LANES), F32)],
        compiler_params=_params(("arbitrary",)),
    )(h0, mo, fw, tgt)


def _pool_select(parts, g):
    out = parts[-1]
    for gi in range(len(parts) - 2, -1, -1):
        out = jnp.where(g == gi, parts[gi], out)
    return out


def _pool_count(row, g, pad):
    win = _pool_select([jnp.full(row.shape, float(w), F32) for w in POOL_WINDOWS], g)
    return jnp.maximum(jnp.minimum((row - pad + 1).astype(F32), win), 1.0)


def _pooled(u, g, row, pad):
    sums, s, span = [], u, 1
    for w in POOL_WINDOWS:
        while span < w:
            s = s + _shift_down(s, span, row)
            span *= 2
        sums.append(s)
    return _pool_select(sums, g) / _pool_count(row, g, pad) - u


def _pooled_adjoint(dp, g, row, pad):
    e = dp / _pool_count(row, g, pad)
    sums, s, span = [], e, 1
    for w in POOL_WINDOWS:
        while span < w:
            s = s + _shift_up(s, span, row)
            span *= 2
        sums.append(s)
    return _pool_select(sums, g) - dp


def _pool_specs(lp, pg, ng, z_off):
    u_spec = pl.BlockSpec((lp, pg), lambda g: (0, g))
    z_spec = pl.BlockSpec((lp, pg), lambda g: (0, z_off + g))
    mix_spec = pl.BlockSpec((1, pg, pg), lambda g: (g, 0, 0))
    vec_spec = pl.BlockSpec((1, pg), lambda g: (0, g))
    return u_spec, z_spec, mix_spec, vec_spec


def _pool_fwd(proj, mix, scale, pad):
    lp = proj.shape[0]
    ng, pg, _ = mix.shape
    pw = ng * pg

    def body(u_ref, z_ref, mix_ref, sc_ref, y_ref):
        g = pl.program_id(0)
        row = lax.broadcasted_iota(jnp.int32, (lp, 1), 0)
        pooled = _pooled(u_ref[...], g, row, pad)
        mixed = _dot(pooled.astype(BF16), mix_ref[0])
        y_ref[...] = (mixed * sc_ref[...] * _silu(z_ref[...])).astype(BF16)

    u_spec, z_spec, mix_spec, vec_spec = _pool_specs(lp, pg, ng, pw // pg)
    return _call(
        body, name="pool_fwd", grid=(ng,), in_specs=[u_spec, z_spec, mix_spec, vec_spec], out_specs=u_spec,
        out_shape=jax.ShapeDtypeStruct((lp, pw), BF16), compiler_params=_params(("parallel",)),
    )(proj, proj, mix, scale)


def _pool_bwd(proj, mix, scale, dy, pad):
    lp = proj.shape[0]
    ng, pg, _ = mix.shape
    pw = ng * pg

    def body(u_ref, z_ref, mix_ref, sc_ref, dy_ref, du_ref, dz_ref, dmix_ref, dsc_ref):
        g = pl.program_id(0)
        row = lax.broadcasted_iota(jnp.int32, (lp, 1), 0)
        real = row >= pad
        z = z_ref[...]
        pooled = _pooled(u_ref[...], g, row, pad).astype(BF16)
        mixed = _dot(pooled, mix_ref[0])
        sig = _sigmoid(z)
        sz = z * sig
        dyv = dy_ref[...]
        dsc_ref[...] = jnp.sum(dyv * mixed * sz, axis=0, keepdims=True)
        d_sz = dyv * mixed * sc_ref[...]
        dz_ref[...] = jnp.where(real, d_sz * (sig + sz * (1.0 - sig)), 0.0).astype(BF16)
        d_mixed = (dyv * sc_ref[...] * sz).astype(BF16)
        dmix_ref[0] = _dot(pooled, d_mixed, TN)
        d_pooled = _dot(d_mixed, mix_ref[0], NT)
        du_ref[...] = jnp.where(real, _pooled_adjoint(d_pooled, g, row, pad), 0.0).astype(BF16)

    u_spec, z_spec, mix_spec, vec_spec = _pool_specs(lp, pg, ng, pw // pg)
    return _call(
        body, name="pool_bwd", grid=(ng,),
        in_specs=[u_spec, z_spec, mix_spec, vec_spec, u_spec], out_specs=[u_spec, u_spec, mix_spec, vec_spec],
        out_shape=[jax.ShapeDtypeStruct((lp, pw), BF16), jax.ShapeDtypeStruct((lp, pw), BF16),
                   jax.ShapeDtypeStruct((ng, pg, pg), F32), jax.ShapeDtypeStruct((1, pw), F32)],
        compiler_params=_params(("parallel",)),
    )(proj, proj, mix, scale, dy)


def _conv_pre(x, w, row):
    kw = w.shape[0]
    y = w[kw - 1:kw, :] * x
    for kk in range(kw - 1):
        y = y + w[kk:kk + 1, :] * _shift_down(x, kw - 1 - kk, row)
    return y


def _conv_post(y, out_scale):
    s = _silu(y)
    if out_scale is None:
        return s
    return s * lax.rsqrt(jnp.sum(s * s, axis=-1, keepdims=True) + NORM_EPS) * out_scale


def _conv_fwd(proj, col_off, w, hd, out_scale, name):
    lp = proj.shape[0]
    kw, width = w.shape
    blk0 = col_off // hd

    def body(x_ref, w_ref, o_ref):
        row = lax.broadcasted_iota(jnp.int32, (lp, 1), 0)
        o_ref[...] = _conv_post(_conv_pre(x_ref[...], w_ref[...], row), out_scale)

    return _call(
        body, name=name, grid=(width // hd,),
        in_specs=[pl.BlockSpec((lp, hd), lambda j: (0, blk0 + j)), pl.BlockSpec((kw, hd), lambda j: (0, j))],
        out_specs=pl.BlockSpec((lp, hd), lambda j: (0, j)),
        out_shape=jax.ShapeDtypeStruct((lp, width), F32), compiler_params=_params(("parallel",)),
    )(proj, w)


def _conv_bwd(proj, col_off, w, d_out, hd, out_scale, pad, name):
    lp = proj.shape[0]
    kw, width = w.shape
    blk0 = col_off // hd

    def body(x_ref, w_ref, do_ref, dx_ref, dw_ref):
        row = lax.broadcasted_iota(jnp.int32, (lp, 1), 0)
        real = row >= pad
        x, wv = x_ref[...], w_ref[...]
        _, vjp = jax.vjp(functools.partial(_conv_post, out_scale=out_scale), _conv_pre(x, wv, row))
        dy = jnp.where(real, vjp(do_ref[...])[0], 0.0)
        dx = wv[kw - 1:kw, :] * dy
        dw_ref[kw - 1:kw, :] = jnp.sum(dy * x, axis=0, keepdims=True)
        for kk in range(kw - 1):
            j = kw - 1 - kk
            dx = dx + wv[kk:kk + 1, :] * _shift_up(dy, j, row)
            dw_ref[kk:kk + 1, :] = jnp.sum(dy * _shift_down(x, j, row), axis=0, keepdims=True)
        dx_ref[...] = jnp.where(real, dx, 0.0).astype(BF16)

    col = pl.BlockSpec((lp, hd), lambda j: (0, j))
    wspec = pl.BlockSpec((kw, hd), lambda j: (0, j))
    return _call(
        body, name=name, grid=(width // hd,),
        in_specs=[pl.BlockSpec((lp, hd), lambda j: (0, blk0 + j)), wspec, col], out_specs=[col, wspec],
        out_shape=[jax.ShapeDtypeStruct((lp, width), BF16), jax.ShapeDtypeStruct((kw, width), F32)],
        compiler_params=_params(("parallel",)),
    )(proj, w, d_out)


def _chunk_math(state, q, k, v, ba, z, prm, nw, head, rowmask, n_heads):
    c = q.shape[0]
    lane = lax.broadcasted_iota(jnp.int32, ba.shape, 1)
    g_full = -jnp.exp(prm[0:1, :]) * _softplus(ba + prm[1:2, :])
    beta = jnp.sum(jnp.where(lane == head, _sigmoid(ba), 0.0), axis=1, keepdims=True) * rowmask
    g = jnp.sum(jnp.where(lane == n_heads + head, g_full, 0.0), axis=1, keepdims=True) * rowmask
    ri = lax.broadcasted_iota(jnp.int32, (c, c), 0)
    ci = lax.broadcasted_iota(jnp.int32, (c, c), 1)
    causal, strict, eye = ri >= ci, ri > ci, ri == ci
    gcum_b = _dot(jnp.where(causal, 1.0, 0.0).astype(F32), jnp.broadcast_to(g, (c, q.shape[1])), precision=HIGHEST)
    gcum = gcum_b[:, :1]
    grow = _dot(jnp.ones((c, c), F32), jnp.where(eye, gcum_b[:, :c], 0.0), precision=HIGHEST)
    decay = jnp.where(causal, jnp.exp(jnp.where(causal, gcum - grow, 0.0)), 0.0)
    eg = jnp.exp(gcum)
    k_beta = k * beta
    v_beta = v * beta
    lmat = jnp.where(strict, _dot(k_beta, k, NT) * decay, 0.0)
    a = -lmat
    tmat = jnp.where(eye, 1.0, 0.0).astype(F32) + a
    span = 2
    while span < c:
        a = _dot(a, a, precision=HIGHEST)
        tmat = tmat + _dot(tmat, a, precision=HIGHEST)
        span *= 2
    u_c = _dot(tmat, v_beta)
    w_c = _dot(tmat, k_beta * eg)
    qk = jnp.where(causal, _dot(q, k, NT) * decay, 0.0)
    glast = jnp.sum(g, axis=0, keepdims=True)
    v_new = u_c - _dot(w_c, state)
    o = _dot(q * eg, state) + _dot(qk, v_new)
    new_state = state * jnp.exp(glast) + _dot(k * jnp.exp(glast - gcum), v_new, TN)
    return _rmsnorm(o, nw) * _silu(z), new_state


def _chunk_specs(nc, hd, n_heads, z_blk, rev):
    cidx = (lambda c: nc - 1 - c) if rev else (lambda c: c)
    blk = lambda off: pl.BlockSpec((CHUNK, hd), lambda c, h: (cidx(c), off + h))
    ba_spec = pl.BlockSpec((CHUNK, LANES), lambda c, h: (cidx(c), 0))
    prm_spec = pl.BlockSpec((8, LANES), lambda c, h: (0, 0))
    nw_spec = pl.BlockSpec((1, hd), lambda c, h: (0, 0))
    st_spec = pl.BlockSpec((1, 1, hd, hd), lambda c, h: (cidx(c), h, 0, 0))
    return blk, ba_spec, prm_spec, nw_spec, st_spec, blk(z_blk)


def _rowmask(chunk_idx, pad):
    row = chunk_idx * CHUNK + lax.broadcasted_iota(jnp.int32, (CHUNK, 1), 0)
    return jnp.where(row >= pad, 1.0, 0.0).astype(F32)


def _chunk_fwd(qn, kn, vv, ba, proj, z_off, prm, nw, n_heads, pad):
    lp, dn = qn.shape
    hd = dn // n_heads
    nc = lp // CHUNK

    def body(q_ref, k_ref, v_ref, ba_ref, z_ref, prm_ref, nw_ref, y_ref, hist_ref, st_ref):
        c, h = pl.program_id(0), pl.program_id(1)

        @pl.when(c == 0)
        def _():
            st_ref[h] = jnp.zeros((hd, hd), F32)

        state = st_ref[h]
        hist_ref[0, 0] = state
        y, new_state = _chunk_math(state, q_ref[...], k_ref[...], v_ref[...], ba_ref[...], z_ref[...], prm_ref[...],
                                   nw_ref[...], h, _rowmask(c, pad), n_heads)
        y_ref[...] = y.astype(BF16)
        st_ref[h] = new_state

    blk, ba_spec, prm_spec, nw_spec, st_spec, z_spec = _chunk_specs(nc, hd, n_heads, z_off // hd, False)
    return _call(
        body, name="chunk_fwd", grid=(nc, n_heads),
        in_specs=[blk(0), blk(0), blk(0), ba_spec, z_spec, prm_spec, nw_spec], out_specs=[blk(0), st_spec],
        out_shape=[jax.ShapeDtypeStruct((lp, dn), BF16), jax.ShapeDtypeStruct((nc, n_heads, hd, hd), F32)],
        scratch_shapes=[pltpu.VMEM((n_heads, hd, hd), F32)],
        compiler_params=_params(("arbitrary", "arbitrary")),
    )(qn, kn, vv, ba, proj, prm, nw)


def _chunk_bwd(qn, kn, vv, ba, proj, z_off, prm, nw, hist, dy, n_heads, pad):
    lp, dn = qn.shape
    hd = dn // n_heads
    nc = lp // CHUNK

    def body(q_ref, k_ref, v_ref, ba_ref, z_ref, prm_ref, nw_ref, hist_ref, dy_ref,
             dq_ref, dk_ref, dv_ref, dba_ref, dz_ref, dprm_ref, dnw_ref, dst_ref):
        step, h = pl.program_id(0), pl.program_id(1)

        @pl.when(step == 0)
        def _():
            dst_ref[h] = jnp.zeros((hd, hd), F32)

        @pl.when((step == 0) & (h == 0))
        def _():
            dprm_ref[...] = jnp.zeros_like(dprm_ref)
            dnw_ref[...] = jnp.zeros_like(dnw_ref)

        @pl.when(h == 0)
        def _():
            dba_ref[...] = jnp.zeros_like(dba_ref)

        fn = functools.partial(_chunk_math, head=h, rowmask=_rowmask(nc - 1 - step, pad), n_heads=n_heads)
        _, vjp = jax.vjp(fn, hist_ref[0, 0], q_ref[...], k_ref[...], v_ref[...], ba_ref[...], z_ref[...], prm_ref[...], nw_ref[...])
        dst, dq, dk, dv, dba, dz, dprm, dnw = vjp((dy_ref[...], dst_ref[h]))
        dst_ref[h] = dst
        dq_ref[...] = dq
        dk_ref[...] = dk
        dv_ref[...] = dv
        dz_ref[...] = dz.astype(BF16)
        dba_ref[...] += dba
        dprm_ref[...] += dprm
        dnw_ref[...] += dnw

    blk, ba_spec, prm_spec, nw_spec, st_spec, z_spec = _chunk_specs(nc, hd, n_heads, z_off // hd, True)
    f32_full = jax.ShapeDtypeStruct((lp, dn), F32)
    return _call(
        body, name="chunk_bwd", grid=(nc, n_heads),
        in_specs=[blk(0), blk(0), blk(0), ba_spec, z_spec, prm_spec, nw_spec, st_spec, blk(0)],
        out_specs=[blk(0), blk(0), blk(0), ba_spec, blk(0), prm_spec, nw_spec],
        out_shape=[f32_full, f32_full, f32_full, jax.ShapeDtypeStruct((lp, LANES), F32), jax.ShapeDtypeStruct((lp, dn), BF16),
                   jax.ShapeDtypeStruct((8, LANES), F32), jax.ShapeDtypeStruct((1, hd), F32)],
        scratch_shapes=[pltpu.VMEM((n_heads, hd, hd), F32)],
        compiler_params=_params(("arbitrary", "arbitrary")),
    )(qn, kn, vv, ba, proj, prm, nw, hist, dy)


def _merge_math(p, q, gp, gd):
    return _sigmoid(gp) * p + _sigmoid(gd) * q


def _merge_specs(lp, d, gp_off, gd_off):
    tr, tc = _tile(lp, 264, 16), _tile(d, 1024, LANES)
    blk = pl.BlockSpec((tr, tc), lambda i, j: (i, j))
    gp_spec = pl.BlockSpec((tr, tc), lambda i, j: (i, gp_off // tc + j))
    gd_spec = pl.BlockSpec((tr, tc), lambda i, j: (i, gd_off // tc + j))
    return (lp // tr, d // tc), blk, gp_spec, gd_spec


def _merge_fwd(p, q, proj, gp_off, gd_off):
    lp, d = p.shape
    grid, blk, gp_spec, gd_spec = _merge_specs(lp, d, gp_off, gd_off)

    def body(p_ref, q_ref, gp_ref, gd_ref, o_ref):
        o_ref[...] = _merge_math(p_ref[...], q_ref[...], gp_ref[...], gd_ref[...]).astype(BF16)

    return _call(
        body, name="merge_fwd", grid=grid, in_specs=[blk, blk, gp_spec, gd_spec], out_specs=blk,
        out_shape=jax.ShapeDtypeStruct((lp, d), BF16), compiler_params=_params(("parallel", "parallel")),
    )(p, q, proj, proj)


def _merge_bwd(p, q, proj, gp_off, gd_off, dm):
    lp, d = p.shape
    grid, blk, gp_spec, gd_spec = _merge_specs(lp, d, gp_off, gd_off)

    def body(p_ref, q_ref, gp_ref, gd_ref, dm_ref, dp_ref, dq_ref, dgp_ref, dgd_ref):
        _, vjp = jax.vjp(_merge_math, p_ref[...], q_ref[...], gp_ref[...], gd_ref[...])
        for ref, val in zip((dp_ref, dq_ref, dgp_ref, dgd_ref), vjp(dm_ref[...])):
            ref[...] = val.astype(BF16)

    out = jax.ShapeDtypeStruct((lp, d), BF16)
    return _call(
        body, name="merge_bwd", grid=grid, in_specs=[blk, blk, gp_spec, gd_spec, blk], out_specs=[blk] * 4,
        out_shape=[out] * 4, compiler_params=_params(("parallel", "parallel")),
    )(p, q, proj, proj, dm)


def _adamw(w, g, m, v, name):
    shape = w.shape
    w2, g2, m2, v2 = (t.reshape((-1, shape[-1])) for t in (w, g, m, v))
    rows, cols = w2.shape
    tr = _tile(rows, 128, 8)

    def body(w_ref, g_ref, m_ref, v_ref, d_ref, nm_ref, nv_ref):
        gv = g_ref[...]
        nm = ADAM_B1 * m_ref[...] + (1.0 - ADAM_B1) * gv
        nv = ADAM_B2 * v_ref[...] + (1.0 - ADAM_B2) * (gv * gv)
        m_hat = nm / (1.0 - ADAM_B1 ** ADAM_STEP)
        v_hat = nv / (1.0 - ADAM_B2 ** ADAM_STEP)
        d_ref[...] = -ADAM_LR * (m_hat / (jnp.sqrt(v_hat) + ADAM_EPS) + ADAM_WD * w_ref[...])
        nm_ref[...] = nm
        nv_ref[...] = nv

    blk = pl.BlockSpec((tr, cols), lambda i: (i, 0))
    out = jax.ShapeDtypeStruct((rows, cols), F32)
    res = _call(
        body, name=name, grid=(rows // tr,), in_specs=[blk] * 4, out_specs=[blk] * 3, out_shape=[out] * 3,
        compiler_params=_params(("parallel",)),
    )(w2, g2, m2, v2)
    return tuple(t.reshape(shape) for t in res)


def _coords():
    return lax.axis_index("x"), lax.axis_index("y"), lax.axis_index("c")


def _flip(v, bit):
    return 1 - v if bit else v


CHIP_FLIPS = ((1, 0), (0, 1), (1, 1))
ANY = pl.BlockSpec(memory_space=pl.ANY)


def _all_gather(shard):
    rows, cols = shard.shape

    def body(x_ref, out_ref, send_sems, recv_sems, local_sem):
        x, y, c = _coords()
        sibling = (x, y, 1 - c)
        chips = [(_flip(x, fx), _flip(y, fy)) for fx, fy in CHIP_FLIPS]

        def slot(px, py, pc):
            return out_ref.at[4 * px + 2 * py + pc]

        def copy(k, block, to, src=None):
            return pltpu.make_async_remote_copy(
                src_ref=slot(*block) if src is None else src, dst_ref=slot(*block),
                send_sem=send_sems.at[k], recv_sem=recv_sems.at[k], device_id=to, device_id_type=MESH)

        mine = pltpu.make_async_copy(x_ref, slot(x, y, c), local_sem)
        mine.start()
        first = [copy(0, (x, y, c), sibling, src=x_ref)]
        first += [copy(1 + j, (x, y, c), (*chip, c), src=x_ref) for j, chip in enumerate(chips)]
        for cp in first:
            cp.start()
        passed = [copy(4 + j, (*chip, c), sibling) for j, chip in enumerate(chips)]
        for j, chip in enumerate(chips):
            copy(1 + j, (*chip, c), (x, y, c)).wait_recv()
            passed[j].start()
        copy(0, (x, y, 1 - c), (x, y, c)).wait_recv()
        for j, chip in enumerate(chips):
            copy(4 + j, (*chip, 1 - c), (x, y, c)).wait_recv()
        for cp in first + passed:
            cp.wait_send()
        mine.wait()

    return _call(
        body, name="all_gather", in_specs=[ANY], out_specs=ANY,
        out_shape=jax.ShapeDtypeStruct((N_DEV, rows, cols), shard.dtype),
        scratch_shapes=[pltpu.SemaphoreType.DMA((7,)), pltpu.SemaphoreType.DMA((7,)), pltpu.SemaphoreType.DMA],
    )(shard)


def _rs_to_sibling(g):
    _, rows, cols = g.shape

    def body(g_ref, got_ref, send_sems, recv_sems):
        x, y, c = _coords()
        copies = []
        for p in range(4):
            cp = pltpu.make_async_remote_copy(
                src_ref=g_ref.at[2 * p + (1 - c)], dst_ref=got_ref.at[p], send_sem=send_sems.at[p],
                recv_sem=recv_sems.at[p], device_id=(x, y, 1 - c), device_id_type=MESH)
            cp.start()
            copies.append(cp)
        for cp in copies:
            cp.wait()

    return _call(
        body, name="rs_to_sibling", in_specs=[ANY], out_specs=ANY,
        out_shape=jax.ShapeDtypeStruct((4, rows, cols), g.dtype),
        scratch_shapes=[pltpu.SemaphoreType.DMA((4,)), pltpu.SemaphoreType.DMA((4,))],
    )(g)


def _rs_pair_sum(g, got, c_idx):
    _, rows, cols = g.shape
    tr = _tile(rows, 512, 8)

    def body(c_ref, g_ref, got_ref, o_ref):
        o_ref[...] = g_ref[...] + got_ref[...]

    grid_spec = pltpu.PrefetchScalarGridSpec(
        num_scalar_prefetch=1, grid=(4, rows // tr),
        in_specs=[pl.BlockSpec((1, tr, cols), lambda p, i, c_ref: (2 * p + c_ref[0], i, 0)),
                  pl.BlockSpec((1, tr, cols), lambda p, i, c_ref: (p, i, 0))],
        out_specs=pl.BlockSpec((1, tr, cols), lambda p, i, c_ref: (p, i, 0)))
    return _call(
        body, name="rs_pair_sum", grid_spec=grid_spec, out_shape=jax.ShapeDtypeStruct((4, rows, cols), g.dtype),
        compiler_params=_params(("parallel", "parallel")),
    )(c_idx, g, got)


def _rs_to_chips(partial):
    _, rows, cols = partial.shape

    def body(p_ref, got_ref, send_sems, recv_sems):
        x, y, c = _coords()
        copies = []
        for k, (fx, fy) in enumerate(CHIP_FLIPS):
            px, py = _flip(x, fx), _flip(y, fy)
            cp = pltpu.make_async_remote_copy(
                src_ref=p_ref.at[2 * px + py], dst_ref=got_ref.at[k], send_sem=send_sems.at[k],
                recv_sem=recv_sems.at[k], device_id=(px, py, c), device_id_type=MESH)
            cp.start()
            copies.append(cp)
        for cp in copies:
            cp.wait()

    return _call(
        body, name="rs_to_chips", in_specs=[ANY], out_specs=ANY,
        out_shape=jax.ShapeDtypeStruct((3, rows, cols), partial.dtype),
        scratch_shapes=[pltpu.SemaphoreType.DMA((3,)), pltpu.SemaphoreType.DMA((3,))],
    )(partial)


def _rs_chip_sum(partial, got, chip_idx):
    _, rows, cols = partial.shape
    tr = _tile(rows, 512, 8)

    def body(p_idx_ref, p_ref, got_ref, o_ref):
        o_ref[...] = ((p_ref[0] + got_ref[0]) + got_ref[1]) + got_ref[2]

    grid_spec = pltpu.PrefetchScalarGridSpec(
        num_scalar_prefetch=1, grid=(rows // tr,),
        in_specs=[pl.BlockSpec((1, tr, cols), lambda i, p_ref: (p_ref[0], i, 0)),
                  pl.BlockSpec((3, tr, cols), lambda i, p_ref: (0, i, 0))],
        out_specs=pl.BlockSpec((tr, cols), lambda i, p_ref: (i, 0)))
    return _call(
        body, name="rs_chip_sum", grid_spec=grid_spec, out_shape=jax.ShapeDtypeStruct((rows, cols), partial.dtype),
        compiler_params=_params(("parallel",)),
    )(chip_idx, partial, got)


def _reduce_scatter(g):
    x, y, c = _coords()
    got = _rs_to_sibling(g)
    partial = _rs_pair_sum(g, got, jnp.reshape(c, (1,)).astype(jnp.int32))
    got2 = _rs_to_chips(partial)
    return _rs_chip_sum(partial, got2, jnp.reshape(2 * x + y, (1,)).astype(jnp.int32))


def _all_reduce_small(vec):
    rows, cols = vec.shape

    def body(v_ref, o_ref, buf, send_sems, recv_sems):
        x, y, c = _coords()
        me = 4 * x + 2 * y + c
        buf[me] = v_ref[...]
        copies = []
        for k in range(N_DEV - 1):
            fx, fy, fc = ((k + 1) >> 2) & 1, ((k + 1) >> 1) & 1, (k + 1) & 1
            cp = pltpu.make_async_remote_copy(
                src_ref=v_ref, dst_ref=buf.at[me], send_sem=send_sems.at[k], recv_sem=recv_sems.at[k],
                device_id=(_flip(x, fx), _flip(y, fy), _flip(c, fc)), device_id_type=MESH)
            cp.start()
            copies.append(cp)
        for cp in copies:
            cp.wait()
        total = buf[0]
        for j in range(1, N_DEV):
            total = total + buf[j]
        o_ref[...] = total

    vmem = pl.BlockSpec(memory_space=pltpu.VMEM)
    return _call(
        body, name="all_reduce_small", in_specs=[vmem], out_specs=vmem,
        out_shape=jax.ShapeDtypeStruct((rows, cols), F32),
        scratch_shapes=[pltpu.VMEM((N_DEV, rows, cols), F32), pltpu.SemaphoreType.DMA((N_DEV - 1,)),
                        pltpu.SemaphoreType.DMA((N_DEV - 1,))],
    )(vec)


def _pack_rows(n_elems, row_align):
    per = PACK_COLS * row_align
    return -(-n_elems // per) * row_align


def _f32_bits_as_bf16(a):
    return lax.bitcast_convert_type(a, BF16).reshape(-1)


def _bf16_bits_as_f32(a, shape):
    return lax.bitcast_convert_type(a.reshape(shape + (2,)), F32)


def kernel(x, meta_tokens, norm_w, w_in, conv_w, A_log, dt_bias, pool_mix, pool_scale, dn_norm_w, w_pool_out, w_dn_out, w_o, final_norm_w, loss_target, m_meta_tokens, m_norm_w, m_w_in, m_conv_w, m_A_log, m_dt_bias, m_pool_mix, m_pool_scale, m_dn_norm_w, m_w_pool_out, m_w_dn_out, m_w_o, m_final_norm_w, v_meta_tokens, v_norm_w, v_w_in, v_conv_w, v_A_log, v_dt_bias, v_pool_mix, v_pool_scale, v_dn_norm_w, v_w_pool_out, v_w_dn_out, v_w_o, v_final_norm_w):
    seq, d = x.shape[1], x.shape[2]
    n_meta = meta_tokens.shape[0]
    n_heads, hd = A_log.shape[-1], dn_norm_w.shape[-1]
    dn = n_heads * hd
    pw, ng = pool_scale.shape[-1], pool_mix.shape[1]
    pg = pw // ng
    kw = conv_w.shape[1]
    pad = (-n_meta) % CHUNK
    x0 = pad + n_meta
    lp = x0 + seq
    ns = w_in.shape[-1]
    in_cols = N_DEV * ns
    o_q, o_k, o_v, o_zd = 2 * pw, 2 * pw + dn, 2 * pw + 2 * dn, 2 * pw + 3 * dn
    o_ba = 2 * pw + 4 * dn
    o_gp, o_gd = o_ba, o_ba + d
    n_main = o_gd + d
    assert lp % CHUNK == 0 and in_cols == n_main + 2 * n_heads and 2 * n_heads <= LANES and hd == LANES
    cs, ms, ps = conv_w.shape[-1], meta_tokens.shape[-1], w_pool_out.shape[-1]
    mr, dr, orr = pool_mix.shape[2], w_dn_out.shape[1], w_o.shape[1]

    pieces = [w_in[0].astype(BF16).reshape(-1), _f32_bits_as_bf16(conv_w[0]), pool_mix[0].astype(BF16).reshape(-1),
              w_pool_out[0].astype(BF16).reshape(-1), w_dn_out[0].astype(BF16).reshape(-1), w_o[0].astype(BF16).reshape(-1),
              _f32_bits_as_bf16(meta_tokens)]
    sizes = [p.shape[0] for p in pieces]
    rows16 = _pack_rows(sum(sizes), 16)
    flat = jnp.concatenate(pieces + [jnp.zeros((rows16 * PACK_COLS - sum(sizes),), BF16)])
    gathered = _all_gather(flat.reshape(rows16, PACK_COLS)).reshape(N_DEV, -1)
    offs = [sum(sizes[:i]) for i in range(len(sizes))]
    take = lambda i: gathered[:, offs[i]:offs[i] + sizes[i]]
    cols_major = lambda t, r, c: jnp.transpose(t.reshape(N_DEV, r, c), (1, 0, 2)).reshape(r, N_DEV * c)
    w_in_f = cols_major(take(0), d, ns)
    w_main = jnp.concatenate([w_in_f[:, :o_ba], w_in_f[:, o_ba + 2 * n_heads:]], axis=1)
    w_ba = jnp.pad(w_in_f[:, o_ba:o_ba + 2 * n_heads], ((0, 0), (0, LANES - 2 * n_heads)))
    conv_f = cols_major(_bf16_bits_as_f32(take(1), (N_DEV, kw, cs)), kw, cs)
    mix_f = jnp.transpose(take(2).reshape(N_DEV, ng, mr, pg), (1, 0, 2, 3)).reshape(ng, pg, pg)
    wpo_f = cols_major(take(3), pw, ps)
    wdn_f = take(4).reshape(dn, d)
    wo_f = take(5).reshape(d, d)
    meta_f = cols_major(_bf16_bits_as_f32(take(6), (N_DEV, n_meta, ms)), n_meta, ms)

    h0 = jnp.concatenate([jnp.zeros((pad, d), F32), meta_f, x[0]], axis=0)
    tgt = jnp.concatenate([jnp.zeros((x0, d), F32), loss_target[0]], axis=0)
    xn = _norm_in_fwd(h0, norm_w)
    proj = _matmul(xn, w_main, NN, F32, 1056, 512, 2048, "proj_main")
    ba = _matmul(xn, w_ba, NN, F32, 1056, LANES, 2048, "proj_ba")
    y_pool = _pool_fwd(proj, mix_f, pool_scale, pad)
    conv_q, conv_k, conv_v = (conv_f[:, i * dn:(i + 1) * dn] for i in range(3))
    qn = _conv_fwd(proj, o_q, conv_q, hd, float(hd) ** -0.5, "conv_q_fwd")
    kn = _conv_fwd(proj, o_k, conv_k, hd, 1.0, "conv_k_fwd")
    vv = _conv_fwd(proj, o_v, conv_v, hd, None, "conv_v_fwd")
    prm = jnp.zeros((8, LANES), F32).at[0, n_heads:2 * n_heads].set(A_log[0]).at[1, n_heads:2 * n_heads].set(dt_bias[0])
    y_dn, hist = _chunk_fwd(qn, kn, vv, ba, proj, o_zd, prm, dn_norm_w, n_heads, pad)
    p_out = _matmul(y_pool, wpo_f, NN, F32, 1056, 1024, 1024, "pool_out")
    q_out = _matmul(y_dn, wdn_f, NN, F32, 1056, 1024, 2048, "dn_out")
    merged = _merge_fwd(p_out, q_out, proj, o_gp, o_gd)
    mo = _matmul(merged, wo_f, NN, F32, 1056, 1024, 2048, "w_o_fwd")
    dh1, d_fw, loss_part = _final_loss(h0, mo, final_norm_w.reshape(1, d), tgt, x0)

    d_merged = _matmul(dh1, wo_f, NT, F32, 1056, 1024, 1024, "w_o_bwd_x")
    g_wo = _matmul(merged, dh1, TN, F32, 1024, 1024, 704, "w_o_bwd_w")
    d_p, d_q, d_gp, d_gd = _merge_bwd(p_out, q_out, proj, o_gp, o_gd, d_merged)
    d_ypool = _matmul(d_p, wpo_f, NT, F32, 1056, 1024, 2048, "pool_out_bwd_x")
    g_wpo = _matmul(y_pool, d_p, TN, F32, 1024, 1024, 704, "pool_out_bwd_w")
    d_ydn = _matmul(d_q, wdn_f, NT, F32, 1056, 1024, 2048, "dn_out_bwd_x")
    g_wdn = _matmul(y_dn, d_q, TN, F32, 1024, 1024, 704, "dn_out_bwd_w")
    d_u, d_zp, g_mix, g_pscale = _pool_bwd(proj, mix_f, pool_scale, d_ypool, pad)
    d_qn, d_kn, d_vv, d_ba, d_zd, d_prm, g_dnw = _chunk_bwd(qn, kn, vv, ba, proj, o_zd, prm, dn_norm_w, hist, d_ydn, n_heads, pad)
    d_qr, g_cq = _conv_bwd(proj, o_q, conv_q, d_qn, hd, float(hd) ** -0.5, pad, "conv_q_bwd")
    d_kr, g_ck = _conv_bwd(proj, o_k, conv_k, d_kn, hd, 1.0, pad, "conv_k_bwd")
    d_vr, g_cv = _conv_bwd(proj, o_v, conv_v, d_vv, hd, None, pad, "conv_v_bwd")
    d_proj = jnp.concatenate([d_u, d_zp, d_qr, d_kr, d_vr, d_zd, d_gp, d_gd], axis=1)
    g_wmain = _matmul(xn, d_proj, TN, F32, 1024, 1024, 704, "w_in_bwd_w")
    g_wba = _matmul(xn, d_ba, TN, F32, 1024, LANES, 704, "w_ba_bwd_w")
    d_xn_a = _matmul(d_proj, w_main, NT, F32, 1056, 1024, 1024, "w_in_bwd_x")
    d_xn_b = _matmul(d_ba, w_ba, NT, F32, 1056, 1024, LANES, "w_ba_bwd_x")
    d_h0, g_nw = _norm_in_bwd(h0, norm_w, d_xn_a, d_xn_b, dh1)
    grad_x = d_h0[x0:][None]

    g_win = jnp.concatenate([g_wmain[:, :o_ba], g_wba[:, :2 * n_heads], g_wmain[:, o_ba:]], axis=1)
    by_cols = lambda t, r, c: jnp.transpose(t.reshape(r, N_DEV, c), (1, 0, 2)).reshape(N_DEV, r * c)
    g_conv = jnp.concatenate([g_cq, g_ck, g_cv], axis=1)
    g_pieces = [by_cols(g_win, d, ns), by_cols(g_conv, kw, cs),
                jnp.transpose(g_mix.reshape(ng, N_DEV, mr, pg), (1, 0, 2, 3)).reshape(N_DEV, -1),
                by_cols(g_wpo, pw, ps), g_wdn.reshape(N_DEV, -1), g_wo.reshape(N_DEV, -1),
                by_cols(d_h0[pad:x0], n_meta, ms)]
    g_sizes = [p.shape[1] for p in g_pieces]
    rows8 = _pack_rows(sum(g_sizes), 8)
    g_flat = jnp.concatenate(g_pieces + [jnp.zeros((N_DEV, rows8 * PACK_COLS - sum(g_sizes)), F32)], axis=1)
    g_mine = _reduce_scatter(g_flat.reshape(N_DEV, rows8, PACK_COLS)).reshape(-1)
    g_offs = [sum(g_sizes[:i]) for i in range(len(g_sizes))]
    g_take = lambda i, shape: g_mine[g_offs[i]:g_offs[i] + g_sizes[i]].reshape(shape)

    small = [g_nw[0], d_fw[0], g_pscale[0], g_dnw[0], d_prm[0], d_prm[1], loss_part[0]]
    s_sizes = [t.shape[0] for t in small]
    s_cols = -(-sum(s_sizes) // (8 * LANES)) * LANES
    s_vec = jnp.concatenate(small + [jnp.zeros((8 * s_cols - sum(s_sizes),), F32)]).reshape(8, s_cols)
    s_sum = _all_reduce_small(s_vec).reshape(-1)
    s_offs = [sum(s_sizes[:i]) for i in range(len(s_sizes))]
    s_take = lambda i, n=None, o=0: s_sum[s_offs[i] + o:s_offs[i] + o + (s_sizes[i] if n is None else n)]

    grads = {
        "meta_tokens": g_take(6, meta_tokens.shape), "norm_w": s_take(0).reshape(norm_w.shape),
        "w_in": g_take(0, w_in.shape), "conv_w": g_take(1, conv_w.shape),
        "A_log": s_take(4, n_heads, n_heads).reshape(A_log.shape), "dt_bias": s_take(5, n_heads, n_heads).reshape(dt_bias.shape),
        "pool_mix": g_take(2, pool_mix.shape), "pool_scale": s_take(2).reshape(pool_scale.shape),
        "dn_norm_w": s_take(3).reshape(dn_norm_w.shape), "w_pool_out": g_take(3, w_pool_out.shape),
        "w_dn_out": g_take(4, w_dn_out.shape), "w_o": g_take(5, w_o.shape),
        "final_norm_w": s_take(1).reshape(final_norm_w.shape),
    }
    loss = s_take(6, 1)[0]

    weights = dict(meta_tokens=meta_tokens, norm_w=norm_w, w_in=w_in, conv_w=conv_w, A_log=A_log, dt_bias=dt_bias,
                   pool_mix=pool_mix, pool_scale=pool_scale, dn_norm_w=dn_norm_w, w_pool_out=w_pool_out, w_dn_out=w_dn_out,
                   w_o=w_o, final_norm_w=final_norm_w)
    m_in = dict(meta_tokens=m_meta_tokens, norm_w=m_norm_w, w_in=m_w_in, conv_w=m_conv_w, A_log=m_A_log, dt_bias=m_dt_bias,
                pool_mix=m_pool_mix, pool_scale=m_pool_scale, dn_norm_w=m_dn_norm_w, w_pool_out=m_w_pool_out,
                w_dn_out=m_w_dn_out, w_o=m_w_o, final_norm_w=m_final_norm_w)
    v_in = dict(meta_tokens=v_meta_tokens, norm_w=v_norm_w, w_in=v_w_in, conv_w=v_conv_w, A_log=v_A_log, dt_bias=v_dt_bias,
                pool_mix=v_pool_mix, pool_scale=v_pool_scale, dn_norm_w=v_dn_norm_w, w_pool_out=v_w_pool_out,
                w_dn_out=v_w_dn_out, w_o=v_w_o, final_norm_w=v_final_norm_w)
    names = list(weights)
    upd = {n: _adamw(weights[n], grads[n], m_in[n], v_in[n], "adamw_" + n) for n in names}
    return (loss, grad_x, *[grads[n] for n in names], *[upd[n][0] for n in names], *[upd[n][1] for n in names],
            *[upd[n][2] for n in names])
```

```python
import functools

import jax
import jax.numpy as jnp
import numpy as np
from jax import lax
from jax.experimental import pallas as pl
from jax.experimental.pallas import tpu as pltpu

F32 = jnp.float32
BF16 = jnp.bfloat16
HIGHEST = lax.Precision.HIGHEST
MESH = pl.DeviceIdType.MESH

CHUNK = 64
NORM_EPS = 1e-6
POOL_WINDOWS = (2, 4, 8, 16)
ADAM_LR, ADAM_B1, ADAM_B2, ADAM_EPS, ADAM_WD, ADAM_STEP = 0.001, 0.9, 0.999, 1e-08, 0.01, 10
N_DEV = 8
LANES = 128
VMEM_LIMIT = 48 * 1024 * 1024

NN = (((1,), (0,)), ((), ()))
NT = (((1,), (1,)), ((), ()))
TN = (((0,), (0,)), ((), ()))


def _call(body, **kw):
    return pl.pallas_call(body, **kw)


def _params(sem=None):
    return pltpu.CompilerParams(dimension_semantics=sem, vmem_limit_bytes=VMEM_LIMIT)


def _tile(n, pref, align):
    for d in range(min(pref, n), 0, -1):
        if n % d == 0 and d % align == 0:
            return d
    return n


def _dot(a, b, dims=NN, precision=None):
    return lax.dot_general(a, b, dims, precision=precision, preferred_element_type=F32)


def _sigmoid(x):
    return 1.0 / (1.0 + jnp.exp(-x))


def _silu(x):
    return x * _sigmoid(x)


def _softplus(x):
    return jnp.maximum(x, 0.0) + jnp.log(1.0 + jnp.exp(-jnp.abs(x)))


def _rmsnorm(x, w):
    return x * lax.rsqrt(jnp.mean(x * x, axis=-1, keepdims=True) + NORM_EPS) * w


def _shift_down(x, j, row):
    if j == 0:
        return x
    return jnp.where(row >= j, pltpu.roll(x, j, 0), 0.0)


def _shift_up(x, j, row):
    if j == 0:
        return x
    n = x.shape[0]
    return jnp.where(row < n - j, pltpu.roll(x, n - j, 0), 0.0)


def _matmul(a, b, dims, out_dtype, tm, tn, tk, name, col_blocks=None):
    ta = dims == TN
    tb = dims == NT
    m, kdim = (a.shape[1], a.shape[0]) if ta else a.shape
    n = b.shape[0] if tb else b.shape[1]
    if col_blocks:
        tn = n // col_blocks
    tm, tn, tk = _tile(m, tm, 8), _tile(n, tn, LANES), _tile(kdim, tk, LANES if not ta else 16)
    nk = kdim // tk

    def body(a_ref, b_ref, o_ref, *scratch):
        part = _dot(a_ref[...].astype(BF16), b_ref[...].astype(BF16), dims)
        if nk == 1:
            o_ref[...] = part.astype(o_ref.dtype).reshape(o_ref.shape)
            return
        acc_ref, = scratch
        k = pl.program_id(2)

        @pl.when(k == 0)
        def _():
            acc_ref[...] = part

        @pl.when(k > 0)
        def _():
            acc_ref[...] += part

        @pl.when(k == nk - 1)
        def _():
            o_ref[...] = acc_ref[...].astype(o_ref.dtype).reshape(o_ref.shape)

    a_spec = pl.BlockSpec((tk, tm), lambda i, j, k: (k, i)) if ta else pl.BlockSpec((tm, tk), lambda i, j, k: (i, k))
    b_spec = pl.BlockSpec((tn, tk), lambda i, j, k: (j, k)) if tb else pl.BlockSpec((tk, tn), lambda i, j, k: (k, j))
    if col_blocks:
        out_spec = pl.BlockSpec((1, tm, tn), lambda i, j, k: (j, i, 0))
        out_shape = jax.ShapeDtypeStruct((col_blocks, m, tn), out_dtype)
    else:
        out_spec = pl.BlockSpec((tm, tn), lambda i, j, k: (i, j))
        out_shape = jax.ShapeDtypeStruct((m, n), out_dtype)
    return _call(
        body, name=name, grid=(m // tm, n // tn, nk),
        in_specs=[a_spec, b_spec], out_specs=out_spec, out_shape=out_shape,
        scratch_shapes=[] if nk == 1 else [pltpu.VMEM((tm, tn), F32)],
        compiler_params=_params(("parallel", "parallel", "arbitrary")),
    )(a, b)


def _norm_in_fwd(x2d, meta, w, pad):
    seq, d = x2d.shape
    tr = pad + meta.shape[0]
    assert seq % tr == 0 and tr % 16 == 0
    lp = tr + seq

    def body(x_ref, m_ref, w_ref, h_ref, o_ref):
        def emit(h):
            h_ref[...] = h
            o_ref[...] = _rmsnorm(h, w_ref[...]).astype(BF16)

        @pl.when(pl.program_id(0) == 0)
        def _():
            emit(jnp.concatenate([jnp.zeros((pad, d), F32), m_ref[...]], axis=0) if pad else m_ref[...])

        @pl.when(pl.program_id(0) > 0)
        def _():
            emit(x_ref[...])

    row = pl.BlockSpec((tr, d), lambda i: (i, 0))
    return _call(
        body, name="norm_in_fwd", grid=(lp // tr,),
        in_specs=[pl.BlockSpec((tr, d), lambda i: (jnp.maximum(i - 1, 0), 0)), pl.BlockSpec(meta.shape, lambda i: (0, 0)),
                  pl.BlockSpec((1, d), lambda i: (0, 0))],
        out_specs=[row, row],
        out_shape=[jax.ShapeDtypeStruct((lp, d), F32), jax.ShapeDtypeStruct((lp, d), BF16)],
        compiler_params=_params(("arbitrary",)),
    )(x2d, meta, w)


def _norm_in_bwd(h0, w, dxn, dh1):
    lp, d = h0.shape
    tr = _tile(lp, 264, 8)

    def body(h_ref, w_ref, da_ref, dh1_ref, dh_ref, dw_ref):
        _, vjp = jax.vjp(_rmsnorm, h_ref[...], w_ref[...])
        dh, dw = vjp(da_ref[...])
        dh_ref[...] = dh + dh1_ref[...]

        @pl.when(pl.program_id(0) == 0)
        def _():
            dw_ref[...] = jnp.zeros_like(dw_ref)

        dw_ref[...] += dw

    row = pl.BlockSpec((tr, d), lambda i: (i, 0))
    vec = pl.BlockSpec((1, d), lambda i: (0, 0))
    return _call(
        body, name="norm_in_bwd", grid=(lp // tr,),
        in_specs=[row, vec, row, row], out_specs=[row, vec],
        out_shape=[jax.ShapeDtypeStruct((lp, d), F32), jax.ShapeDtypeStruct((1, d), F32)],
        compiler_params=_params(("arbitrary",)),
    )(h0, w, dxn, dh1)


def _final_loss(h0, mo, fw, tgt, x0):
    lp, d = h0.shape
    tr = x0
    assert lp % tr == 0

    def body(h_ref, mo_ref, fw_ref, t_ref, dh_ref, dw_ref, loss_ref):
        i = pl.program_id(0)
        row = i * tr + lax.broadcasted_iota(jnp.int32, (tr, 1), 0)
        mask = jnp.where(row >= x0, 1.0, 0.0).astype(F32)
        tgt_v = t_ref[...]

        def loss_fn(h1, w):
            err = _rmsnorm(h1, w) - tgt_v
            return 0.5 * jnp.sum(jnp.mean(err * err, axis=-1, keepdims=True) * mask, axis=0, keepdims=True)

        loss, vjp = jax.vjp(loss_fn, h_ref[...] + mo_ref[...], fw_ref[...])
        dh, dw = vjp(jnp.ones((1, 1), F32))
        dh_ref[...] = dh

        @pl.when(i == 0)
        def _():
            dw_ref[...] = jnp.zeros_like(dw_ref)
            loss_ref[...] = jnp.zeros_like(loss_ref)

        dw_ref[...] += dw
        loss_ref[...] += jnp.broadcast_to(loss, loss_ref.shape)

    row_spec = pl.BlockSpec((tr, d), lambda i: (i, 0))
    vec = pl.BlockSpec((1, d), lambda i: (0, 0))
    return _call(
        body, name="final_loss", grid=(lp // tr,),
        in_specs=[row_spec, row_spec, vec, pl.BlockSpec((tr, d), lambda i: (jnp.maximum(i - 1, 0), 0))],
        out_specs=[row_spec, vec, pl.BlockSpec((8, LANES), lambda i: (0, 0))],
        out_shape=[jax.ShapeDtypeStruct((lp, d), F32), jax.ShapeDtypeStruct((1, d), F32), jax.ShapeDtypeStruct((8, LANES), F32)],
        compiler_params=_params(("arbitrary",)),
    )(h0, mo, fw, tgt)


def _pool_select(parts, g):
    out = parts[-1]
    for gi in range(len(parts) - 2, -1, -1):
        out = jnp.where(g == gi, parts[gi], out)
    return out


def _pool_count(row, g, pad):
    win = _pool_select([jnp.full(row.shape, float(w), F32) for w in POOL_WINDOWS], g)
    return jnp.maximum(jnp.minimum((row - pad + 1).astype(F32), win), 1.0)


def _pooled(u, g, row, pad):
    sums, s, span = [], u, 1
    for w in POOL_WINDOWS:
        while span < w:
            s = s + _shift_down(s, span, row)
            span *= 2
        sums.append(s)
    return _pool_select(sums, g) / _pool_count(row, g, pad) - u


def _pooled_adjoint(dp, g, row, pad):
    e = dp / _pool_count(row, g, pad)
    sums, s, span = [], e, 1
    for w in POOL_WINDOWS:
        while span < w:
            s = s + _shift_up(s, span, row)
            span *= 2
        sums.append(s)
    return _pool_select(sums, g) - dp


def _pool_specs(lp, pg, ng, z_off):
    u_spec = pl.BlockSpec((lp, pg), lambda g: (0, g))
    z_spec = pl.BlockSpec((lp, pg), lambda g: (0, z_off + g))
    mix_spec = pl.BlockSpec((1, pg, pg), lambda g: (g, 0, 0))
    vec_spec = pl.BlockSpec((1, pg), lambda g: (0, g))
    return u_spec, z_spec, mix_spec, vec_spec


def _pool_fwd(proj, mix, scale, pad):
    lp = proj.shape[0]
    ng, pg, _ = mix.shape
    pw = ng * pg

    def body(u_ref, z_ref, mix_ref, sc_ref, y_ref):
        g = pl.program_id(0)
        row = lax.broadcasted_iota(jnp.int32, (lp, 1), 0)
        pooled = _pooled(u_ref[...], g, row, pad)
        mixed = _dot(pooled.astype(BF16), mix_ref[0])
        y_ref[...] = (mixed * sc_ref[...] * _silu(z_ref[...])).astype(BF16)

    u_spec, z_spec, mix_spec, vec_spec = _pool_specs(lp, pg, ng, pw // pg)
    return _call(
        body, name="pool_fwd", grid=(ng,), in_specs=[u_spec, z_spec, mix_spec, vec_spec], out_specs=u_spec,
        out_shape=jax.ShapeDtypeStruct((lp, pw), BF16), compiler_params=_params(("parallel",)),
    )(proj, proj, mix, scale)


def _pool_bwd(proj, mix, scale, dy, pad):
    lp = proj.shape[0]
    ng, pg, _ = mix.shape
    pw = ng * pg

    def body(u_ref, z_ref, mix_ref, sc_ref, dy_ref, du_ref, dz_ref, dmix_ref, dsc_ref):
        g = pl.program_id(0)
        row = lax.broadcasted_iota(jnp.int32, (lp, 1), 0)
        real = row >= pad
        z = z_ref[...]
        pooled = _pooled(u_ref[...], g, row, pad).astype(BF16)
        mixed = _dot(pooled, mix_ref[0])
        sig = _sigmoid(z)
        sz = z * sig
        dyv = dy_ref[...]
        dsc_ref[...] = jnp.sum(dyv * mixed * sz, axis=0, keepdims=True)
        d_sz = dyv * mixed * sc_ref[...]
        dz_ref[...] = jnp.where(real, d_sz * (sig + sz * (1.0 - sig)), 0.0).astype(BF16)
        d_mixed = (dyv * sc_ref[...] * sz).astype(BF16)
        dmix_ref[0] = _dot(pooled, d_mixed, TN)
        d_pooled = _dot(d_mixed, mix_ref[0], NT)
        du_ref[...] = jnp.where(real, _pooled_adjoint(d_pooled, g, row, pad), 0.0).astype(BF16)

    u_spec, z_spec, mix_spec, vec_spec = _pool_specs(lp, pg, ng, pw // pg)
    return _call(
        body, name="pool_bwd", grid=(ng,),
        in_specs=[u_spec, z_spec, mix_spec, vec_spec, u_spec], out_specs=[u_spec, u_spec, mix_spec, vec_spec],
        out_shape=[jax.ShapeDtypeStruct((lp, pw), BF16), jax.ShapeDtypeStruct((lp, pw), BF16),
                   jax.ShapeDtypeStruct((ng, pg, pg), F32), jax.ShapeDtypeStruct((1, pw), F32)],
        compiler_params=_params(("parallel",)),
    )(proj, proj, mix, scale, dy)


def _conv_pre(x, w, row):
    kw = w.shape[0]
    y = w[kw - 1:kw, :] * x
    for kk in range(kw - 1):
        y = y + w[kk:kk + 1, :] * _shift_down(x, kw - 1 - kk, row)
    return y


def _conv_post(y, out_scale):
    s = _silu(y)
    if out_scale is None:
        return s
    return s * lax.rsqrt(jnp.sum(s * s, axis=-1, keepdims=True) + NORM_EPS) * out_scale


def _conv_fwd(proj, col_off, w, hd, out_scale, name):
    lp = proj.shape[0]
    kw, width = w.shape
    blk0 = col_off // hd

    def body(x_ref, w_ref, o_ref):
        row = lax.broadcasted_iota(jnp.int32, (lp, 1), 0)
        o_ref[...] = _conv_post(_conv_pre(x_ref[...], w_ref[...], row), out_scale)

    return _call(
        body, name=name, grid=(width // hd,),
        in_specs=[pl.BlockSpec((lp, hd), lambda j: (0, blk0 + j)), pl.BlockSpec((kw, hd), lambda j: (0, j))],
        out_specs=pl.BlockSpec((lp, hd), lambda j: (0, j)),
        out_shape=jax.ShapeDtypeStruct((lp, width), F32), compiler_params=_params(("parallel",)),
    )(proj, w)


def _conv_bwd(proj, col_off, w, d_out, hd, out_scale, pad, name):
    lp = proj.shape[0]
    kw, width = w.shape
    blk0 = col_off // hd

    def body(x_ref, w_ref, do_ref, dx_ref, dw_ref):
        row = lax.broadcasted_iota(jnp.int32, (lp, 1), 0)
        real = row >= pad
        x, wv = x_ref[...], w_ref[...]
        _, vjp = jax.vjp(functools.partial(_conv_post, out_scale=out_scale), _conv_pre(x, wv, row))
        dy = jnp.where(real, vjp(do_ref[...])[0], 0.0)
        dx = wv[kw - 1:kw, :] * dy
        dw_ref[kw - 1:kw, :] = jnp.sum(dy * x, axis=0, keepdims=True)
        for kk in range(kw - 1):
            j = kw - 1 - kk
            dx = dx + wv[kk:kk + 1, :] * _shift_up(dy, j, row)
            dw_ref[kk:kk + 1, :] = jnp.sum(dy * _shift_down(x, j, row), axis=0, keepdims=True)
        dx_ref[...] = jnp.where(real, dx, 0.0).astype(BF16)

    col = pl.BlockSpec((lp, hd), lambda j: (0, j))
    wspec = pl.BlockSpec((kw, hd), lambda j: (0, j))
    return _call(
        body, name=name, grid=(width // hd,),
        in_specs=[pl.BlockSpec((lp, hd), lambda j: (0, blk0 + j)), wspec, col], out_specs=[col, wspec],
        out_shape=[jax.ShapeDtypeStruct((lp, width), BF16), jax.ShapeDtypeStruct((kw, width), F32)],
        compiler_params=_params(("parallel",)),
    )(proj, w, d_out)


HEADS_PER_STEP = 4


def _dot3(a, b):
    a_hi, b_hi = a.astype(BF16), b.astype(BF16)
    a_lo, b_lo = (a - a_hi.astype(F32)).astype(BF16), (b - b_hi.astype(F32)).astype(BF16)
    return _dot(a_hi, b_hi) + (_dot(a_hi, b_lo) + _dot(a_lo, b_hi))


@jax.custom_vjp
def _unit_lower_inverse(lmat):
    c = lmat.shape[0]
    eye = lax.broadcasted_iota(jnp.int32, (c, c), 0) == lax.broadcasted_iota(jnp.int32, (c, c), 1)
    a = -lmat
    tmat = jnp.where(eye, 1.0, 0.0).astype(F32) + a
    span = 2
    while span < c:
        a = _dot3(a, a)
        tmat = tmat + _dot3(tmat, a)
        span *= 2
    return tmat


def _unit_lower_inverse_fwd(lmat):
    tmat = _unit_lower_inverse(lmat)
    return tmat, tmat


def _unit_lower_inverse_bwd(tmat, ct):
    return (-_dot(_dot(tmat, ct, TN, HIGHEST), tmat, NT, HIGHEST),)


_unit_lower_inverse.defvjp(_unit_lower_inverse_fwd, _unit_lower_inverse_bwd)


def _chunk_math(states, q, k, v, ba, z, prm, nw, head0, rowmask, n_heads):
    c = q.shape[0]
    hd = q.shape[1] // len(states)
    lane = lax.broadcasted_iota(jnp.int32, ba.shape, 1)
    sub = lax.broadcasted_iota(jnp.int32, (ba.shape[1], c), 0)
    ri = lax.broadcasted_iota(jnp.int32, (c, c), 0)
    ci = lax.broadcasted_iota(jnp.int32, (c, c), 1)
    last = lax.broadcasted_iota(jnp.int32, (c, 1), 0) == c - 1
    causal, strict = ri >= ci, ri > ci
    beta_all = _sigmoid(ba) * rowmask
    g_all = -jnp.exp(prm[0:1, :]) * _softplus(ba + prm[1:2, :]) * rowmask
    gcum_all = _dot(jnp.where(causal, 1.0, 0.0).astype(F32), g_all, precision=HIGHEST)
    gcum_t = gcum_all.T
    ys, new_states = [], []
    for j, state in enumerate(states):
        cols = slice(j * hd, (j + 1) * hd)
        qj, kj, vj = q[:, cols], k[:, cols], v[:, cols]
        beta = jnp.sum(jnp.where(lane == head0 + j, beta_all, 0.0), axis=1, keepdims=True)
        gcum = jnp.sum(jnp.where(lane == n_heads + head0 + j, gcum_all, 0.0), axis=1, keepdims=True)
        grow = jnp.sum(jnp.where(sub == n_heads + head0 + j, gcum_t, 0.0), axis=0, keepdims=True)
        glast = jnp.sum(jnp.where(last, gcum, 0.0), axis=0, keepdims=True)
        decay = jnp.where(causal, jnp.exp(jnp.where(causal, gcum - grow, 0.0)), 0.0)
        eg = jnp.exp(gcum)
        k_beta = kj * beta
        tmat = _unit_lower_inverse(jnp.where(strict, _dot(k_beta, kj, NT) * decay, 0.0))
        u_c = _dot(tmat, vj * beta)
        w_c = _dot(tmat, k_beta * eg)
        qk = jnp.where(causal, _dot(qj, kj, NT) * decay, 0.0)
        v_new = u_c - _dot(w_c, state)
        o = _dot(qj * eg, state) + _dot(qk, v_new)
        new_states.append(state * jnp.exp(glast) + _dot(kj * jnp.exp(glast - gcum), v_new, TN))
        ys.append(_rmsnorm(o, nw) * _silu(z[:, cols]))
    return jnp.concatenate(ys, axis=1), tuple(new_states)


def _chunk_specs(nc, hd, n_heads, z_off, ba_off, rev):
    cidx = (lambda c: nc - 1 - c) if rev else (lambda c: c)
    hb = HEADS_PER_STEP
    assert n_heads % hb == 0 and z_off % (hb * hd) == 0 and ba_off % LANES == 0
    blk = lambda off: pl.BlockSpec((CHUNK, hb * hd), lambda c, g: (cidx(c), off + g))
    ba_spec = lambda off: pl.BlockSpec((CHUNK, LANES), lambda c, g: (cidx(c), off // LANES))
    prm_spec = pl.BlockSpec((8, LANES), lambda c, g: (0, 0))
    nw_spec = pl.BlockSpec((1, hd), lambda c, g: (0, 0))
    st_spec = pl.BlockSpec((1, hb, hd, hd), lambda c, g: (cidx(c), g, 0, 0))
    return blk, ba_spec, prm_spec, nw_spec, st_spec, blk(z_off // (hb * hd))


def _rowmask(chunk_idx, pad):
    row = chunk_idx * CHUNK + lax.broadcasted_iota(jnp.int32, (CHUNK, 1), 0)
    return jnp.where(row >= pad, 1.0, 0.0).astype(F32)


def _chunk_fwd(qn, kn, vv, proj, z_off, ba_off, prm, nw, n_heads, pad):
    lp, dn = qn.shape
    hd = dn // n_heads
    nc = lp // CHUNK
    hb = HEADS_PER_STEP

    def body(q_ref, k_ref, v_ref, ba_ref, z_ref, prm_ref, nw_ref, y_ref, hist_ref, st_ref):
        c, g = pl.program_id(0), pl.program_id(1)

        @pl.when(c == 0)
        def _():
            for j in range(hb):
                st_ref[g * hb + j] = jnp.zeros((hd, hd), F32)

        states = tuple(st_ref[g * hb + j] for j in range(hb))
        for j in range(hb):
            hist_ref[0, j] = states[j]
        y, new_states = _chunk_math(states, q_ref[...], k_ref[...], v_ref[...], ba_ref[...], z_ref[...], prm_ref[...],
                                    nw_ref[...], g * hb, _rowmask(c, pad), n_heads)
        y_ref[...] = y.astype(BF16)
        for j in range(hb):
            st_ref[g * hb + j] = new_states[j]

    blk, ba_spec, prm_spec, nw_spec, st_spec, z_spec = _chunk_specs(nc, hd, n_heads, z_off, ba_off, False)
    return _call(
        body, name="chunk_fwd", grid=(nc, n_heads // hb),
        in_specs=[blk(0), blk(0), blk(0), ba_spec(ba_off), z_spec, prm_spec, nw_spec], out_specs=[blk(0), st_spec],
        out_shape=[jax.ShapeDtypeStruct((lp, dn), BF16), jax.ShapeDtypeStruct((nc, n_heads, hd, hd), F32)],
        scratch_shapes=[pltpu.VMEM((n_heads, hd, hd), F32)],
        compiler_params=_params(("arbitrary", "arbitrary")),
    )(qn, kn, vv, proj, proj, prm, nw)


def _chunk_bwd(qn, kn, vv, proj, z_off, ba_off, prm, nw, hist, dy, n_heads, pad):
    lp, dn = qn.shape
    hd = dn // n_heads
    nc = lp // CHUNK
    hb = HEADS_PER_STEP

    def body(q_ref, k_ref, v_ref, ba_ref, z_ref, prm_ref, nw_ref, hist_ref, dy_ref,
             dq_ref, dk_ref, dv_ref, dba_ref, dz_ref, dprm_ref, dnw_ref, dst_ref):
        step, g = pl.program_id(0), pl.program_id(1)

        @pl.when(step == 0)
        def _():
            for j in range(hb):
                dst_ref[g * hb + j] = jnp.zeros((hd, hd), F32)

        @pl.when((step == 0) & (g == 0))
        def _():
            dprm_ref[...] = jnp.zeros_like(dprm_ref)
            dnw_ref[...] = jnp.zeros_like(dnw_ref)

        @pl.when(g == 0)
        def _():
            dba_ref[...] = jnp.zeros_like(dba_ref)

        fn = functools.partial(_chunk_math, head0=g * hb, rowmask=_rowmask(nc - 1 - step, pad), n_heads=n_heads)
        states = tuple(hist_ref[0, j] for j in range(hb))
        _, vjp = jax.vjp(fn, states, q_ref[...], k_ref[...], v_ref[...], ba_ref[...], z_ref[...], prm_ref[...], nw_ref[...])
        dst, dq, dk, dv, dba, dz, dprm, dnw = vjp((dy_ref[...], tuple(dst_ref[g * hb + j] for j in range(hb))))
        for j in range(hb):
            dst_ref[g * hb + j] = dst[j]
        dq_ref[...] = dq
        dk_ref[...] = dk
        dv_ref[...] = dv
        dz_ref[...] = dz.astype(BF16)
        dba_ref[...] += dba
        dprm_ref[...] += dprm
        dnw_ref[...] += dnw

    blk, ba_spec, prm_spec, nw_spec, st_spec, z_spec = _chunk_specs(nc, hd, n_heads, z_off, ba_off, True)
    f32_full = jax.ShapeDtypeStruct((lp, dn), F32)
    return _call(
        body, name="chunk_bwd", grid=(nc, n_heads // hb),
        in_specs=[blk(0), blk(0), blk(0), ba_spec(ba_off), z_spec, prm_spec, nw_spec, st_spec, blk(0)],
        out_specs=[blk(0), blk(0), blk(0), ba_spec(0), blk(0), prm_spec, nw_spec],
        out_shape=[f32_full, f32_full, f32_full, jax.ShapeDtypeStruct((lp, LANES), F32), jax.ShapeDtypeStruct((lp, dn), BF16),
                   jax.ShapeDtypeStruct((8, LANES), F32), jax.ShapeDtypeStruct((1, hd), F32)],
        scratch_shapes=[pltpu.VMEM((n_heads, hd, hd), F32)],
        compiler_params=_params(("arbitrary", "arbitrary")),
    )(qn, kn, vv, proj, proj, prm, nw, hist, dy)


def _merge_math(p, q, gp, gd):
    return _sigmoid(gp) * p + _sigmoid(gd) * q


def _merge_specs(lp, d, gp_off, gd_off):
    tr, tc = _tile(lp, 264, 16), _tile(d, 1024, LANES)
    blk = pl.BlockSpec((tr, tc), lambda i, j: (i, j))
    gp_spec = pl.BlockSpec((tr, tc), lambda i, j: (i, gp_off // tc + j))
    gd_spec = pl.BlockSpec((tr, tc), lambda i, j: (i, gd_off // tc + j))
    return (lp // tr, d // tc), blk, gp_spec, gd_spec


def _merge_fwd(p, q, proj, gp_off, gd_off):
    lp, d = p.shape
    grid, blk, gp_spec, gd_spec = _merge_specs(lp, d, gp_off, gd_off)

    def body(p_ref, q_ref, gp_ref, gd_ref, o_ref):
        o_ref[...] = _merge_math(p_ref[...], q_ref[...], gp_ref[...], gd_ref[...]).astype(BF16)

    return _call(
        body, name="merge_fwd", grid=grid, in_specs=[blk, blk, gp_spec, gd_spec], out_specs=blk,
        out_shape=jax.ShapeDtypeStruct((lp, d), BF16), compiler_params=_params(("parallel", "parallel")),
    )(p, q, proj, proj)


def _merge_bwd(p, q, proj, gp_off, gd_off, dm):
    lp, d = p.shape
    grid, blk, gp_spec, gd_spec = _merge_specs(lp, d, gp_off, gd_off)

    def body(p_ref, q_ref, gp_ref, gd_ref, dm_ref, dp_ref, dq_ref, dgp_ref, dgd_ref):
        _, vjp = jax.vjp(_merge_math, p_ref[...], q_ref[...], gp_ref[...], gd_ref[...])
        for ref, val in zip((dp_ref, dq_ref, dgp_ref, dgd_ref), vjp(dm_ref[...])):
            ref[...] = val.astype(BF16)

    out = jax.ShapeDtypeStruct((lp, d), BF16)
    return _call(
        body, name="merge_bwd", grid=grid, in_specs=[blk, blk, gp_spec, gd_spec, blk], out_specs=[blk] * 4,
        out_shape=[out] * 4, compiler_params=_params(("parallel", "parallel")),
    )(p, q, proj, proj, dm)


def _adamw(w, g, m, v, name):
    shape = w.shape
    w2, g2, m2, v2 = (t.reshape((-1, shape[-1])) for t in (w, g, m, v))
    rows, cols = w2.shape
    tr = _tile(rows, 128, 8)

    def body(w_ref, g_ref, m_ref, v_ref, d_ref, nm_ref, nv_ref):
        gv = g_ref[...]
        nm = ADAM_B1 * m_ref[...] + (1.0 - ADAM_B1) * gv
        nv = ADAM_B2 * v_ref[...] + (1.0 - ADAM_B2) * (gv * gv)
        m_hat = nm / (1.0 - ADAM_B1 ** ADAM_STEP)
        v_hat = nv / (1.0 - ADAM_B2 ** ADAM_STEP)
        d_ref[...] = -ADAM_LR * (m_hat / (jnp.sqrt(v_hat) + ADAM_EPS) + ADAM_WD * w_ref[...])
        nm_ref[...] = nm
        nv_ref[...] = nv

    blk = pl.BlockSpec((tr, cols), lambda i: (i, 0))
    out = jax.ShapeDtypeStruct((rows, cols), F32)
    res = _call(
        body, name=name, grid=(rows // tr,), in_specs=[blk] * 4, out_specs=[blk] * 3, out_shape=[out] * 3,
        compiler_params=_params(("parallel",)),
    )(w2, g2, m2, v2)
    return tuple(t.reshape(shape) for t in res)


def _coords():
    return lax.axis_index("x"), lax.axis_index("y"), lax.axis_index("c")


def _flip(v, bit):
    return 1 - v if bit else v


CHIP_FLIPS = ((1, 0), (0, 1), (1, 1))
ANY = pl.BlockSpec(memory_space=pl.ANY)


def _all_gather(shards):
    n = len(shards)

    def body(*refs):
        x_refs, out_refs = refs[:n], refs[n:2 * n]
        send_sems, recv_sems, local_sems = refs[2 * n:]
        x, y, c = _coords()
        sibling = (x, y, 1 - c)
        chips = [(_flip(x, fx), _flip(y, fy)) for fx, fy in CHIP_FLIPS]

        def copy(a, k, block, to, from_input=False):
            px, py, pc = block
            slot = out_refs[a].at[4 * px + 2 * py + pc]
            return pltpu.make_async_remote_copy(
                src_ref=x_refs[a] if from_input else slot, dst_ref=slot,
                send_sem=send_sems.at[7 * a + k], recv_sem=recv_sems.at[7 * a + k], device_id=to, device_id_type=MESH)

        mine = [pltpu.make_async_copy(x_refs[a], out_refs[a].at[4 * x + 2 * y + c], local_sems.at[a]) for a in range(n)]
        first = []
        for a in range(n):
            mine[a].start()
            first.append(copy(a, 0, (x, y, c), sibling, True))
            first += [copy(a, 1 + j, (x, y, c), (*chip, c), True) for j, chip in enumerate(chips)]
        for cp in first:
            cp.start()
        passed = []
        for j, chip in enumerate(chips):
            for a in range(n):
                copy(a, 1 + j, (*chip, c), (x, y, c)).wait_recv()
                passed.append(copy(a, 4 + j, (*chip, c), sibling))
                passed[-1].start()
        for a in range(n):
            copy(a, 0, (x, y, 1 - c), (x, y, c)).wait_recv()
            for j, chip in enumerate(chips):
                copy(a, 4 + j, (*chip, 1 - c), (x, y, c)).wait_recv()
        for cp in first + passed:
            cp.wait_send()
        for cp in mine:
            cp.wait()

    return _call(
        body, name="all_gather", in_specs=[ANY] * n, out_specs=[ANY] * n,
        out_shape=[jax.ShapeDtypeStruct((N_DEV,) + s.shape, s.dtype) for s in shards],
        scratch_shapes=[pltpu.SemaphoreType.DMA((7 * n,)), pltpu.SemaphoreType.DMA((7 * n,)), pltpu.SemaphoreType.DMA((n,))],
    )(*shards)


def _rs_to_sibling(gs):
    n = len(gs)

    def body(*refs):
        g_refs, got_refs = refs[:n], refs[n:2 * n]
        send_sems, recv_sems = refs[2 * n:]
        x, y, c = _coords()
        copies = []
        for a in range(n):
            for p in range(4):
                cp = pltpu.make_async_remote_copy(
                    src_ref=g_refs[a].at[2 * p + (1 - c)], dst_ref=got_refs[a].at[p], send_sem=send_sems.at[4 * a + p],
                    recv_sem=recv_sems.at[4 * a + p], device_id=(x, y, 1 - c), device_id_type=MESH)
                cp.start()
                copies.append(cp)
        for cp in copies:
            cp.wait()

    return _call(
        body, name="rs_to_sibling", in_specs=[ANY] * n, out_specs=[ANY] * n,
        out_shape=[jax.ShapeDtypeStruct((4,) + g.shape[1:], g.dtype) for g in gs],
        scratch_shapes=[pltpu.SemaphoreType.DMA((4 * n,)), pltpu.SemaphoreType.DMA((4 * n,))],
    )(*gs)


def _rs_pair_sum(g, got, c_idx, name):
    _, rows, cols = g.shape
    tr = _tile(rows, 256, 16)

    def body(c_ref, g_ref, got_ref, o_ref):
        o_ref[...] = (g_ref[...].astype(F32) + got_ref[...].astype(F32)).astype(o_ref.dtype)

    grid_spec = pltpu.PrefetchScalarGridSpec(
        num_scalar_prefetch=1, grid=(4, rows // tr),
        in_specs=[pl.BlockSpec((1, tr, cols), lambda p, i, c_ref: (2 * p + c_ref[0], i, 0)),
                  pl.BlockSpec((1, tr, cols), lambda p, i, c_ref: (p, i, 0))],
        out_specs=pl.BlockSpec((1, tr, cols), lambda p, i, c_ref: (p, i, 0)))
    return _call(
        body, name=name, grid_spec=grid_spec, out_shape=jax.ShapeDtypeStruct((4, rows, cols), g.dtype),
        compiler_params=_params(("parallel", "parallel")),
    )(c_idx, g, got)


def _rs_to_chips(partials):
    n = len(partials)

    def body(*refs):
        p_refs, got_refs = refs[:n], refs[n:2 * n]
        send_sems, recv_sems = refs[2 * n:]
        x, y, c = _coords()
        copies = []
        for a in range(n):
            for k, (fx, fy) in enumerate(CHIP_FLIPS):
                px, py = _flip(x, fx), _flip(y, fy)
                cp = pltpu.make_async_remote_copy(
                    src_ref=p_refs[a].at[2 * px + py], dst_ref=got_refs[a].at[k], send_sem=send_sems.at[3 * a + k],
                    recv_sem=recv_sems.at[3 * a + k], device_id=(px, py, c), device_id_type=MESH)
                cp.start()
                copies.append(cp)
        for cp in copies:
            cp.wait()

    return _call(
        body, name="rs_to_chips", in_specs=[ANY] * n, out_specs=[ANY] * n,
        out_shape=[jax.ShapeDtypeStruct((3,) + p.shape[1:], p.dtype) for p in partials],
        scratch_shapes=[pltpu.SemaphoreType.DMA((3 * n,)), pltpu.SemaphoreType.DMA((3 * n,))],
    )(*partials)


def _rs_chip_sum(partial, got, chip_idx, name):
    _, rows, cols = partial.shape
    tr = _tile(rows, 256, 16)

    def body(p_idx_ref, p_ref, got_ref, o_ref):
        o_ref[...] = ((p_ref[0].astype(F32) + got_ref[0].astype(F32)) + got_ref[1].astype(F32)) + got_ref[2].astype(F32)

    grid_spec = pltpu.PrefetchScalarGridSpec(
        num_scalar_prefetch=1, grid=(rows // tr,),
        in_specs=[pl.BlockSpec((1, tr, cols), lambda i, p_ref: (p_ref[0], i, 0)),
                  pl.BlockSpec((3, tr, cols), lambda i, p_ref: (0, i, 0))],
        out_specs=pl.BlockSpec((tr, cols), lambda i, p_ref: (i, 0)))
    return _call(
        body, name=name, grid_spec=grid_spec, out_shape=jax.ShapeDtypeStruct((rows, cols), F32),
        compiler_params=_params(("parallel",)),
    )(chip_idx, partial, got)


def _reduce_scatter(gs):
    x, y, c = _coords()
    c_idx = jnp.reshape(c, (1,)).astype(jnp.int32)
    chip_idx = jnp.reshape(2 * x + y, (1,)).astype(jnp.int32)
    gots = _rs_to_sibling(gs)
    partials = [_rs_pair_sum(g, got, c_idx, "rs_pair_sum_%d" % a) for a, (g, got) in enumerate(zip(gs, gots))]
    gots2 = _rs_to_chips(partials)
    return [_rs_chip_sum(p, got, chip_idx, "rs_chip_sum_%d" % a) for a, (p, got) in enumerate(zip(partials, gots2))]


RUNS = 3
RUN_FIELDS = 6


def _lane_gather_table(src_of, src_width):
    n_blocks = src_of.shape[0] // LANES
    tab = np.zeros((n_blocks + 1, RUNS, RUN_FIELDS), np.int32)
    tab[:, :, 5] = LANES
    for t in range(n_blocks):
        runs = []
        for lane in range(LANES):
            slab, col = (int(v) for v in src_of[t * LANES + lane])
            if slab < 0:
                continue
            key = (slab, col // LANES, col % LANES - lane)
            if runs and runs[-1][0] == key and runs[-1][2] == lane:
                runs[-1][2] = lane + 1
            else:
                runs.append([key, lane, lane + 1])
        assert len(runs) <= RUNS
        for e, (key, lo, hi) in enumerate(runs):
            tab[t, e] = (key[0], key[1], key[2], lo, hi, min(LANES, src_width - key[1] * LANES))
    return tab.reshape(-1)


def _lane_gather(src, table, out_slabs, out_width, name):
    _, rows, _ = src.shape
    blocks_per_slab = -(-out_width // LANES)

    def body(tab_ref, *refs):
        o_ref = refs[RUNS]
        t = pl.program_id(0)
        lane = lax.broadcasted_iota(jnp.int32, (1, LANES), 1)
        li = lax.broadcasted_iota(jnp.int32, (LANES, LANES), 0)
        ci = lax.broadcasted_iota(jnp.int32, (LANES, LANES), 1)
        acc = None
        for e in range(RUNS):
            base = (t * RUNS + e) * RUN_FIELDS
            shift, lo, hi, valid = tab_ref[base + 2], tab_ref[base + 3], tab_ref[base + 4], tab_ref[base + 5]
            a = jnp.where(lane < valid, refs[e][0], 0.0).astype(BF16)
            sel = jnp.where((li == ci + shift) & (ci >= lo) & (ci < hi), 1.0, 0.0).astype(BF16)
            part = _dot(a, sel)
            acc = part if acc is None else acc + part
        o_ref[0] = acc.astype(BF16)

    def src_spec(e):
        return pl.BlockSpec((1, rows, LANES), lambda t, tab: (tab[(t * RUNS + e) * RUN_FIELDS], 0, tab[(t * RUNS + e) * RUN_FIELDS + 1]))

    grid_spec = pltpu.PrefetchScalarGridSpec(
        num_scalar_prefetch=1, grid=(out_slabs * blocks_per_slab,), in_specs=[src_spec(e) for e in range(RUNS)],
        out_specs=pl.BlockSpec((1, rows, LANES), lambda t, tab: (t // blocks_per_slab, 0, t % blocks_per_slab)))
    return _call(
        body, name=name, grid_spec=grid_spec, out_shape=jax.ShapeDtypeStruct((out_slabs, rows, out_width), BF16),
        compiler_params=_params(("parallel",)),
    )(jnp.asarray(table), src, src, src)


def _all_reduce_small(vec):
    rows, cols = vec.shape

    def body(v_ref, o_ref, buf, send_sems, recv_sems):
        x, y, c = _coords()
        me = 4 * x + 2 * y + c
        buf[me] = v_ref[...]
        copies = []
        for k in range(N_DEV - 1):
            fx, fy, fc = ((k + 1) >> 2) & 1, ((k + 1) >> 1) & 1, (k + 1) & 1
            cp = pltpu.make_async_remote_copy(
                src_ref=v_ref, dst_ref=buf.at[me], send_sem=send_sems.at[k], recv_sem=recv_sems.at[k],
                device_id=(_flip(x, fx), _flip(y, fy), _flip(c, fc)), device_id_type=MESH)
            cp.start()
            copies.append(cp)
        for cp in copies:
            cp.wait()
        total = buf[0]
        for j in range(1, N_DEV):
            total = total + buf[j]
        o_ref[...] = total

    vmem = pl.BlockSpec(memory_space=pltpu.VMEM)
    return _call(
        body, name="all_reduce_small", in_specs=[vmem], out_specs=vmem,
        out_shape=jax.ShapeDtypeStruct((rows, cols), F32),
        scratch_shapes=[pltpu.VMEM((N_DEV, rows, cols), F32), pltpu.SemaphoreType.DMA((N_DEV - 1,)),
                        pltpu.SemaphoreType.DMA((N_DEV - 1,))],
    )(vec)


def _w_in_column_maps(ns, o_ba, n_logit, n_main, n_all):
    own = np.arange(N_DEV * ns)
    work_of_own = np.where(own < o_ba, own, np.where(own < o_ba + n_logit, n_main + own - o_ba, own - n_logit))
    to_work = np.full((n_all, 2), -1, np.int64)
    to_work[work_of_own, 0] = own // ns
    to_work[work_of_own, 1] = own % ns
    slab_width = -(-ns // LANES) * LANES
    to_own = np.full((N_DEV, slab_width, 2), -1, np.int64)
    to_own[:, :ns, 0] = 0
    to_own[:, :ns, 1] = work_of_own.reshape(N_DEV, ns)
    return to_work, to_own.reshape(-1, 2)


def kernel(x, meta_tokens, norm_w, w_in, conv_w, A_log, dt_bias, pool_mix, pool_scale, dn_norm_w, w_pool_out, w_dn_out, w_o, final_norm_w, loss_target, m_meta_tokens, m_norm_w, m_w_in, m_conv_w, m_A_log, m_dt_bias, m_pool_mix, m_pool_scale, m_dn_norm_w, m_w_pool_out, m_w_dn_out, m_w_o, m_final_norm_w, v_meta_tokens, v_norm_w, v_w_in, v_conv_w, v_A_log, v_dt_bias, v_pool_mix, v_pool_scale, v_dn_norm_w, v_w_pool_out, v_w_dn_out, v_w_o, v_final_norm_w):
    seq, d = x.shape[1], x.shape[2]
    n_meta = meta_tokens.shape[0]
    n_heads, hd = A_log.shape[-1], dn_norm_w.shape[-1]
    dn = n_heads * hd
    pw, ng = pool_scale.shape[-1], pool_mix.shape[1]
    pg = pw // ng
    kw = conv_w.shape[1]
    pad = (-n_meta) % CHUNK
    x0 = pad + n_meta
    lp = x0 + seq
    ns = w_in.shape[-1]
    in_cols = N_DEV * ns
    o_q, o_k, o_v, o_zd = 2 * pw, 2 * pw + dn, 2 * pw + 2 * dn, 2 * pw + 3 * dn
    o_ba = 2 * pw + 4 * dn
    o_gp, o_gd = o_ba, o_ba + d
    n_main = o_gd + d
    n_all = n_main + 2 * LANES
    assert lp % CHUNK == 0 and in_cols == n_main + 2 * n_heads and 2 * n_heads <= LANES and hd == LANES
    cs, ms = conv_w.shape[-1], meta_tokens.shape[-1]
    mr = pool_mix.shape[2]
    assert ms == pg and cs % pg == 0
    to_work, to_own = _w_in_column_maps(ns, o_ba, 2 * n_heads, n_main, n_all)
    cols_major = lambda t: jnp.transpose(t, (1, 0, 2)).reshape(t.shape[1], N_DEV * t.shape[2])

    win_g, wpo_g, wdn_g, wo_g, mix_g, conv_g, meta_g = _all_gather(
        [w_in[0].astype(BF16), w_pool_out[0].astype(BF16), w_dn_out[0].astype(BF16), w_o[0].astype(BF16),
         pool_mix[0].reshape(ng * mr, pg).astype(BF16), conv_w[0], meta_tokens])
    w_all = _lane_gather(win_g, _lane_gather_table(to_work, ns), 1, n_all, "w_in_to_work").reshape(d, n_all)
    wpo_f = cols_major(wpo_g)
    wdn_f = wdn_g.reshape(dn, d)
    wo_f = wo_g.reshape(d, d)
    mix_f = jnp.transpose(mix_g.reshape(N_DEV, ng, mr, pg), (1, 0, 2, 3)).reshape(ng, pg, pg)
    conv_f = cols_major(conv_g)
    meta_f = cols_major(meta_g)

    h0, xn = _norm_in_fwd(x[0], meta_f, norm_w, pad)
    proj = _matmul(xn, w_all, NN, F32, 1056, 768, 2048, "proj")
    y_pool = _pool_fwd(proj, mix_f, pool_scale, pad)
    conv_q, conv_k, conv_v = (conv_f[:, i * dn:(i + 1) * dn] for i in range(3))
    qn = _conv_fwd(proj, o_q, conv_q, hd, float(hd) ** -0.5, "conv_q_fwd")
    kn = _conv_fwd(proj, o_k, conv_k, hd, 1.0, "conv_k_fwd")
    vv = _conv_fwd(proj, o_v, conv_v, hd, None, "conv_v_fwd")
    logit_lanes = (n_heads, LANES - 2 * n_heads)
    prm = jnp.pad(A_log, ((0, 7), logit_lanes)) + jnp.pad(dt_bias, ((1, 6), logit_lanes))
    y_dn, hist = _chunk_fwd(qn, kn, vv, proj, o_zd, n_main, prm, dn_norm_w, n_heads, pad)
    p_out = _matmul(y_pool, wpo_f, NN, F32, 1056, 1024, 1024, "pool_out")
    q_out = _matmul(y_dn, wdn_f, NN, F32, 1056, 1024, 2048, "dn_out")
    merged = _merge_fwd(p_out, q_out, proj, o_gp, o_gd)
    mo = _matmul(merged, wo_f, NN, F32, 1056, 1024, 2048, "w_o_fwd")
    dh1, d_fw, loss_part = _final_loss(h0, mo, final_norm_w.reshape(1, d), loss_target[0], x0)

    d_merged = _matmul(dh1, wo_f, NT, F32, 1056, 1024, 1024, "w_o_bwd_x")
    g_wo = _matmul(merged, dh1, TN, BF16, 1024, 1024, 704, "w_o_bwd_w")
    d_p, d_q, d_gp, d_gd = _merge_bwd(p_out, q_out, proj, o_gp, o_gd, d_merged)
    d_ypool = _matmul(d_p, wpo_f, NT, F32, 1056, 1024, 2048, "pool_out_bwd_x")
    g_wpo = _matmul(y_pool, d_p, TN, BF16, 1024, 1024, 704, "pool_out_bwd_w", col_blocks=N_DEV)
    d_ydn = _matmul(d_q, wdn_f, NT, F32, 1056, 1024, 2048, "dn_out_bwd_x")
    g_wdn = _matmul(y_dn, d_q, TN, BF16, 1024, 1024, 704, "dn_out_bwd_w")
    d_u, d_zp, g_mix, g_pscale = _pool_bwd(proj, mix_f, pool_scale, d_ypool, pad)
    d_qn, d_kn, d_vv, d_ba, d_zd, d_prm, g_dnw = _chunk_bwd(qn, kn, vv, proj, o_zd, n_main, prm, dn_norm_w, hist, d_ydn, n_heads, pad)
    d_qr, g_cq = _conv_bwd(proj, o_q, conv_q, d_qn, hd, float(hd) ** -0.5, pad, "conv_q_bwd")
    d_kr, g_ck = _conv_bwd(proj, o_k, conv_k, d_kn, hd, 1.0, pad, "conv_k_bwd")
    d_vr, g_cv = _conv_bwd(proj, o_v, conv_v, d_vv, hd, None, pad, "conv_v_bwd")
    d_proj = jnp.concatenate([d_u, d_zp, d_qr, d_kr, d_vr, d_zd, d_gp, d_gd, d_ba.astype(BF16), jnp.zeros((lp, LANES), BF16)], axis=1)
    g_wall = _matmul(xn, d_proj, TN, F32, 1024, 768, 704, "w_in_bwd_w")
    d_xn = _matmul(d_proj, w_all, NT, F32, 1056, 1024, 768, "w_in_bwd_x")
    d_h0, g_nw = _norm_in_bwd(h0, norm_w, d_xn, dh1)
    grad_x = d_h0[x0:][None]

    g_win = _lane_gather(g_wall.reshape(1, d, n_all), _lane_gather_table(to_own, n_all), N_DEV, ns, "w_in_grad_to_own")
    by_cols = lambda t: jnp.transpose(t.reshape(t.shape[0], N_DEV, t.shape[1] // N_DEV), (1, 0, 2))
    g_conv = by_cols(jnp.concatenate([g_cq, g_ck, g_cv], axis=1)).reshape(N_DEV, kw * cs // pg, pg)
    conv_rows = -(-g_conv.shape[1] // 16) * 16
    g_small = jnp.concatenate(
        [jnp.transpose(g_mix.reshape(ng, N_DEV, mr, pg), (1, 0, 2, 3)).reshape(N_DEV, ng * mr, pg), by_cols(d_h0[pad:x0]),
         jnp.pad(g_conv, ((0, 0), (0, conv_rows - g_conv.shape[1]), (0, 0)))], axis=1).astype(BF16)
    r_win, r_wpo, r_wdn, r_wo, r_small = _reduce_scatter(
        [g_win, g_wpo, g_wdn.reshape(N_DEV, dn // N_DEV, d), g_wo.reshape(N_DEV, d // N_DEV, d), g_small])
    r_mix, r_meta = r_small[:ng * mr], r_small[ng * mr:ng * mr + n_meta]
    r_conv = r_small[ng * mr + n_meta:ng * mr + n_meta + kw * cs // pg]

    small = [g_nw[0], d_fw[0], g_pscale[0], g_dnw[0], d_prm[0], d_prm[1], loss_part[0]]
    s_sizes = [t.shape[0] for t in small]
    s_cols = -(-sum(s_sizes) // (8 * LANES)) * LANES
    s_vec = jnp.concatenate(small + [jnp.zeros((8 * s_cols - sum(s_sizes),), F32)]).reshape(8, s_cols)
    s_sum = _all_reduce_small(s_vec).reshape(-1)
    s_offs = [sum(s_sizes[:i]) for i in range(len(s_sizes))]
    s_take = lambda i, n=None, o=0: s_sum[s_offs[i] + o:s_offs[i] + o + (s_sizes[i] if n is None else n)]

    grads = {
        "meta_tokens": r_meta, "norm_w": s_take(0).reshape(norm_w.shape),
        "w_in": r_win.reshape(w_in.shape), "conv_w": r_conv.reshape(conv_w.shape),
        "A_log": s_take(4, n_heads, n_heads).reshape(A_log.shape), "dt_bias": s_take(5, n_heads, n_heads).reshape(dt_bias.shape),
        "pool_mix": r_mix.reshape(pool_mix.shape), "pool_scale": s_take(2).reshape(pool_scale.shape),
        "dn_norm_w": s_take(3).reshape(dn_norm_w.shape), "w_pool_out": r_wpo.reshape(w_pool_out.shape),
        "w_dn_out": r_wdn.reshape(w_dn_out.shape), "w_o": r_wo.reshape(w_o.shape),
        "final_norm_w": s_take(1).reshape(final_norm_w.shape),
    }
    loss = s_take(6, 1)[0]

    weights = dict(meta_tokens=meta_tokens, norm_w=norm_w, w_in=w_in, conv_w=conv_w, A_log=A_log, dt_bias=dt_bias,
                   pool_mix=pool_mix, pool_scale=pool_scale, dn_norm_w=dn_norm_w, w_pool_out=w_pool_out, w_dn_out=w_dn_out,
                   w_o=w_o, final_norm_w=final_norm_w)
    m_in = dict(meta_tokens=m_meta_tokens, norm_w=m_norm_w, w_in=m_w_in, conv_w=m_conv_w, A_log=m_A_log, dt_bias=m_dt_bias,
                pool_mix=m_pool_mix, pool_scale=m_pool_scale, dn_norm_w=m_dn_norm_w, w_pool_out=m_w_pool_out,
                w_dn_out=m_w_dn_out, w_o=m_w_o, final_norm_w=m_final_norm_w)
    v_in = dict(meta_tokens=v_meta_tokens, norm_w=v_norm_w, w_in=v_w_in, conv_w=v_conv_w, A_log=v_A_log, dt_bias=v_dt_bias,
                pool_mix=v_pool_mix, pool_scale=v_pool_scale, dn_norm_w=v_dn_norm_w, w_pool_out=v_w_pool_out,
                w_dn_out=v_w_dn_out, w_o=v_w_o, final_norm_w=v_final_norm_w)
    names = list(weights)
    upd = {n: _adamw(weights[n], grads[n], m_in[n], v_in[n], "adamw_" + n) for n in names}
    return (loss, grad_x, *[grads[n] for n in names], *[upd[n][0] for n in names], *[upd[n][1] for n in names],
            *[upd[n][2] for n in names])
```

```python
import functools

import jax
import jax.numpy as jnp
import numpy as np
from jax import lax
from jax.experimental import pallas as pl
from jax.experimental.pallas import tpu as pltpu

F32 = jnp.float32
BF16 = jnp.bfloat16
HIGHEST = lax.Precision.HIGHEST
MESH = pl.DeviceIdType.MESH

CHUNK = 64
NORM_EPS = 1e-6
POOL_WINDOWS = (2, 4, 8, 16)
ADAM_LR, ADAM_B1, ADAM_B2, ADAM_EPS, ADAM_WD, ADAM_STEP = 0.001, 0.9, 0.999, 1e-08, 0.01, 10
N_DEV = 8
LANES = 128
VMEM_LIMIT = 48 * 1024 * 1024

NN = (((1,), (0,)), ((), ()))
NT = (((1,), (1,)), ((), ()))
TN = (((0,), (0,)), ((), ()))


def _call(body, **kw):
    return pl.pallas_call(body, **kw)


def _params(sem=None):
    return pltpu.CompilerParams(dimension_semantics=sem, vmem_limit_bytes=VMEM_LIMIT)


def _tile(n, pref, align):
    for d in range(min(pref, n), 0, -1):
        if n % d == 0 and d % align == 0:
            return d
    return n


def _dot(a, b, dims=NN, precision=None):
    return lax.dot_general(a, b, dims, precision=precision, preferred_element_type=F32)


def _sigmoid(x):
    return 1.0 / (1.0 + jnp.exp(-x))


def _silu(x):
    return x * _sigmoid(x)


def _softplus(x):
    return jnp.maximum(x, 0.0) + jnp.log(1.0 + jnp.exp(-jnp.abs(x)))


def _rmsnorm(x, w):
    return x * lax.rsqrt(jnp.mean(x * x, axis=-1, keepdims=True) + NORM_EPS) * w


def _shift_down(x, j, row):
    if j == 0:
        return x
    return jnp.where(row >= j, pltpu.roll(x, j, 0), 0.0)


def _shift_up(x, j, row):
    if j == 0:
        return x
    n = x.shape[0]
    return jnp.where(row < n - j, pltpu.roll(x, n - j, 0), 0.0)


def _matmul(a, b, dims, out_dtype, tm, tn, tk, name, col_blocks=None):
    ta = dims == TN
    tb = dims == NT
    m, kdim = (a.shape[1], a.shape[0]) if ta else a.shape
    n = b.shape[0] if tb else b.shape[1]
    if col_blocks:
        tn = n // col_blocks
    tm, tn, tk = _tile(m, tm, 8), _tile(n, tn, LANES), _tile(kdim, tk, LANES if not ta else 16)
    nk = kdim // tk

    def body(a_ref, b_ref, o_ref, *scratch):
        part = _dot(a_ref[...].astype(BF16), b_ref[...].astype(BF16), dims)
        if nk == 1:
            o_ref[...] = part.astype(o_ref.dtype).reshape(o_ref.shape)
            return
        acc_ref, = scratch
        k = pl.program_id(2)

        @pl.when(k == 0)
        def _():
            acc_ref[...] = part

        @pl.when(k > 0)
        def _():
            acc_ref[...] += part

        @pl.when(k == nk - 1)
        def _():
            o_ref[...] = acc_ref[...].astype(o_ref.dtype).reshape(o_ref.shape)

    a_spec = pl.BlockSpec((tk, tm), lambda i, j, k: (k, i)) if ta else pl.BlockSpec((tm, tk), lambda i, j, k: (i, k))
    b_spec = pl.BlockSpec((tn, tk), lambda i, j, k: (j, k)) if tb else pl.BlockSpec((tk, tn), lambda i, j, k: (k, j))
    if col_blocks:
        out_spec = pl.BlockSpec((1, tm, tn), lambda i, j, k: (j, i, 0))
        out_shape = jax.ShapeDtypeStruct((col_blocks, m, tn), out_dtype)
    else:
        out_spec = pl.BlockSpec((tm, tn), lambda i, j, k: (i, j))
        out_shape = jax.ShapeDtypeStruct((m, n), out_dtype)
    return _call(
        body, name=name, grid=(m // tm, n // tn, nk),
        in_specs=[a_spec, b_spec], out_specs=out_spec, out_shape=out_shape,
        scratch_shapes=[] if nk == 1 else [pltpu.VMEM((tm, tn), F32)],
        compiler_params=_params(("parallel", "parallel", "arbitrary")),
    )(a, b)


def _norm_in_fwd(x2d, meta, w, pad):
    seq, d = x2d.shape
    tr = pad + meta.shape[0]
    assert seq % tr == 0 and tr % 16 == 0
    lp = tr + seq

    def body(x_ref, m_ref, w_ref, h_ref, o_ref):
        def emit(h):
            h_ref[...] = h
            o_ref[...] = _rmsnorm(h, w_ref[...]).astype(BF16)

        @pl.when(pl.program_id(0) == 0)
        def _():
            emit(jnp.concatenate([jnp.zeros((pad, d), F32), m_ref[...]], axis=0) if pad else m_ref[...])

        @pl.when(pl.program_id(0) > 0)
        def _():
            emit(x_ref[...])

    row = pl.BlockSpec((tr, d), lambda i: (i, 0))
    return _call(
        body, name="norm_in_fwd", grid=(lp // tr,),
        in_specs=[pl.BlockSpec((tr, d), lambda i: (jnp.maximum(i - 1, 0), 0)), pl.BlockSpec(meta.shape, lambda i: (0, 0)),
                  pl.BlockSpec((1, d), lambda i: (0, 0))],
        out_specs=[row, row],
        out_shape=[jax.ShapeDtypeStruct((lp, d), F32), jax.ShapeDtypeStruct((lp, d), BF16)],
        compiler_params=_params(("arbitrary",)),
    )(x2d, meta, w)


def _norm_in_bwd(h0, w, dxn, dh1):
    lp, d = h0.shape
    tr = _tile(lp, 264, 8)

    def body(h_ref, w_ref, da_ref, dh1_ref, dh_ref, dw_ref):
        _, vjp = jax.vjp(_rmsnorm, h_ref[...], w_ref[...])
        dh, dw = vjp(da_ref[...])
        dh_ref[...] = dh + dh1_ref[...]

        @pl.when(pl.program_id(0) == 0)
        def _():
            dw_ref[...] = jnp.zeros_like(dw_ref)

        dw_ref[...] += dw

    row = pl.BlockSpec((tr, d), lambda i: (i, 0))
    vec = pl.BlockSpec((1, d), lambda i: (0, 0))
    return _call(
        body, name="norm_in_bwd", grid=(lp // tr,),
        in_specs=[row, vec, row, row], out_specs=[row, vec],
        out_shape=[jax.ShapeDtypeStruct((lp, d), F32), jax.ShapeDtypeStruct((1, d), F32)],
        compiler_params=_params(("arbitrary",)),
    )(h0, w, dxn, dh1)


def _final_loss(h0, mo, fw, tgt, x0):
    lp, d = h0.shape
    tr = x0
    assert lp % tr == 0

    def body(h_ref, mo_ref, fw_ref, t_ref, dh_ref, dw_ref, loss_ref):
        i = pl.program_id(0)
        row = i * tr + lax.broadcasted_iota(jnp.int32, (tr, 1), 0)
        mask = jnp.where(row >= x0, 1.0, 0.0).astype(F32)
        tgt_v = t_ref[...]

        def loss_fn(h1, w):
            err = _rmsnorm(h1, w) - tgt_v
            return 0.5 * jnp.sum(jnp.mean(err * err, axis=-1, keepdims=True) * mask, axis=0, keepdims=True)

        loss, vjp = jax.vjp(loss_fn, h_ref[...] + mo_ref[...], fw_ref[...])
        dh, dw = vjp(jnp.ones((1, 1), F32))
        dh_ref[...] = dh

        @pl.when(i == 0)
        def _():
            dw_ref[...] = jnp.zeros_like(dw_ref)
            loss_ref[...] = jnp.zeros_like(loss_ref)

        dw_ref[...] += dw
        loss_ref[...] += jnp.broadcast_to(loss, loss_ref.shape)

    row_spec = pl.BlockSpec((tr, d), lambda i: (i, 0))
    vec = pl.BlockSpec((1, d), lambda i: (0, 0))
    return _call(
        body, name="final_loss", grid=(lp // tr,),
        in_specs=[row_spec, row_spec, vec, pl.BlockSpec((tr, d), lambda i: (jnp.maximum(i - 1, 0), 0))],
        out_specs=[row_spec, vec, pl.BlockSpec((8, LANES), lambda i: (0, 0))],
        out_shape=[jax.ShapeDtypeStruct((lp, d), F32), jax.ShapeDtypeStruct((1, d), F32), jax.ShapeDtypeStruct((8, LANES), F32)],
        compiler_params=_params(("arbitrary",)),
    )(h0, mo, fw, tgt)


def _pool_select(parts, g):
    out = parts[-1]
    for gi in range(len(parts) - 2, -1, -1):
        out = jnp.where(g == gi, parts[gi], out)
    return out


def _pool_count(row, g, pad):
    win = _pool_select([jnp.full(row.shape, float(w), F32) for w in POOL_WINDOWS], g)
    return jnp.maximum(jnp.minimum((row - pad + 1).astype(F32), win), 1.0)


def _pooled(u, g, row, pad):
    sums, s, span = [], u, 1
    for w in POOL_WINDOWS:
        while span < w:
            s = s + _shift_down(s, span, row)
            span *= 2
        sums.append(s)
    return _pool_select(sums, g) / _pool_count(row, g, pad) - u


def _pooled_adjoint(dp, g, row, pad):
    e = dp / _pool_count(row, g, pad)
    sums, s, span = [], e, 1
    for w in POOL_WINDOWS:
        while span < w:
            s = s + _shift_up(s, span, row)
            span *= 2
        sums.append(s)
    return _pool_select(sums, g) - dp


def _pool_specs(lp, pg, ng, z_off):
    u_spec = pl.BlockSpec((lp, pg), lambda g: (0, g))
    z_spec = pl.BlockSpec((lp, pg), lambda g: (0, z_off + g))
    mix_spec = pl.BlockSpec((1, pg, pg), lambda g: (g, 0, 0))
    vec_spec = pl.BlockSpec((1, pg), lambda g: (0, g))
    return u_spec, z_spec, mix_spec, vec_spec


def _pool_fwd(proj, mix, scale, pad):
    lp = proj.shape[0]
    ng, pg, _ = mix.shape
    pw = ng * pg

    def body(u_ref, z_ref, mix_ref, sc_ref, y_ref):
        g = pl.program_id(0)
        row = lax.broadcasted_iota(jnp.int32, (lp, 1), 0)
        pooled = _pooled(u_ref[...], g, row, pad)
        mixed = _dot(pooled.astype(BF16), mix_ref[0])
        y_ref[...] = (mixed * sc_ref[...] * _silu(z_ref[...])).astype(BF16)

    u_spec, z_spec, mix_spec, vec_spec = _pool_specs(lp, pg, ng, pw // pg)
    return _call(
        body, name="pool_fwd", grid=(ng,), in_specs=[u_spec, z_spec, mix_spec, vec_spec], out_specs=u_spec,
        out_shape=jax.ShapeDtypeStruct((lp, pw), BF16), compiler_params=_params(("parallel",)),
    )(proj, proj, mix, scale)


def _pool_bwd(proj, mix, scale, dy, pad):
    lp = proj.shape[0]
    ng, pg, _ = mix.shape
    pw = ng * pg

    def body(u_ref, z_ref, mix_ref, sc_ref, dy_ref, du_ref, dz_ref, dmix_ref, dsc_ref):
        g = pl.program_id(0)
        row = lax.broadcasted_iota(jnp.int32, (lp, 1), 0)
        real = row >= pad
        z = z_ref[...]
        pooled = _pooled(u_ref[...], g, row, pad).astype(BF16)
        mixed = _dot(pooled, mix_ref[0])
        sig = _sigmoid(z)
        sz = z * sig
        dyv = dy_ref[...]
        dsc_ref[...] = jnp.sum(dyv * mixed * sz, axis=0, keepdims=True)
        d_sz = dyv * mixed * sc_ref[...]
        dz_ref[...] = jnp.where(real, d_sz * (sig + sz * (1.0 - sig)), 0.0).astype(BF16)
        d_mixed = (dyv * sc_ref[...] * sz).astype(BF16)
        dmix_ref[0] = _dot(pooled, d_mixed, TN)
        d_pooled = _dot(d_mixed, mix_ref[0], NT)
        du_ref[...] = jnp.where(real, _pooled_adjoint(d_pooled, g, row, pad), 0.0).astype(BF16)

    u_spec, z_spec, mix_spec, vec_spec = _pool_specs(lp, pg, ng, pw // pg)
    return _call(
        body, name="pool_bwd", grid=(ng,),
        in_specs=[u_spec, z_spec, mix_spec, vec_spec, u_spec], out_specs=[u_spec, u_spec, mix_spec, vec_spec],
        out_shape=[jax.ShapeDtypeStruct((lp, pw), BF16), jax.ShapeDtypeStruct((lp, pw), BF16),
                   jax.ShapeDtypeStruct((ng, pg, pg), F32), jax.ShapeDtypeStruct((1, pw), F32)],
        compiler_params=_params(("parallel",)),
    )(proj, proj, mix, scale, dy)


def _conv_pre(x, w, row):
    kw = w.shape[0]
    y = w[kw - 1:kw, :] * x
    for kk in range(kw - 1):
        y = y + w[kk:kk + 1, :] * _shift_down(x, kw - 1 - kk, row)
    return y


def _conv_post(y, out_scale):
    s = _silu(y)
    if out_scale is None:
        return s
    return s * lax.rsqrt(jnp.sum(s * s, axis=-1, keepdims=True) + NORM_EPS) * out_scale


def _conv_fwd(proj, col_off, w, hd, out_scale, name):
    lp = proj.shape[0]
    kw, width = w.shape
    blk0 = col_off // hd

    def body(x_ref, w_ref, o_ref):
        row = lax.broadcasted_iota(jnp.int32, (lp, 1), 0)
        o_ref[...] = _conv_post(_conv_pre(x_ref[...], w_ref[...], row), out_scale)

    return _call(
        body, name=name, grid=(width // hd,),
        in_specs=[pl.BlockSpec((lp, hd), lambda j: (0, blk0 + j)), pl.BlockSpec((kw, hd), lambda j: (0, j))],
        out_specs=pl.BlockSpec((lp, hd), lambda j: (0, j)),
        out_shape=jax.ShapeDtypeStruct((lp, width), F32), compiler_params=_params(("parallel",)),
    )(proj, w)


def _conv_bwd(proj, col_off, w, d_out, hd, out_scale, pad, name):
    lp = proj.shape[0]
    kw, width = w.shape
    blk0 = col_off // hd

    def body(x_ref, w_ref, do_ref, dx_ref, dw_ref):
        row = lax.broadcasted_iota(jnp.int32, (lp, 1), 0)
        real = row >= pad
        x, wv = x_ref[...], w_ref[...]
        _, vjp = jax.vjp(functools.partial(_conv_post, out_scale=out_scale), _conv_pre(x, wv, row))
        dy = jnp.where(real, vjp(do_ref[...])[0], 0.0)
        dx = wv[kw - 1:kw, :] * dy
        dw_ref[kw - 1:kw, :] = jnp.sum(dy * x, axis=0, keepdims=True)
        for kk in range(kw - 1):
            j = kw - 1 - kk
            dx = dx + wv[kk:kk + 1, :] * _shift_up(dy, j, row)
            dw_ref[kk:kk + 1, :] = jnp.sum(dy * _shift_down(x, j, row), axis=0, keepdims=True)
        dx_ref[...] = jnp.where(real, dx, 0.0).astype(BF16)

    col = pl.BlockSpec((lp, hd), lambda j: (0, j))
    wspec = pl.BlockSpec((kw, hd), lambda j: (0, j))
    return _call(
        body, name=name, grid=(width // hd,),
        in_specs=[pl.BlockSpec((lp, hd), lambda j: (0, blk0 + j)), wspec, col], out_specs=[col, wspec],
        out_shape=[jax.ShapeDtypeStruct((lp, width), BF16), jax.ShapeDtypeStruct((kw, width), F32)],
        compiler_params=_params(("parallel",)),
    )(proj, w, d_out)


HEADS_PER_STEP = 16


def _each(fn, *lists):
    return [fn(*args) for args in zip(*lists)]


def _dot3_each(a_list, b_list):
    hi = lambda t: t.astype(BF16)
    lo = lambda t, t_hi: (t - t_hi.astype(F32)).astype(BF16)
    a_hi, b_hi = _each(hi, a_list), _each(hi, b_list)
    a_lo, b_lo = _each(lo, a_list, a_hi), _each(lo, b_list, b_hi)
    hh, hl, lh = _each(_dot, a_hi, b_hi), _each(_dot, a_hi, b_lo), _each(_dot, a_lo, b_hi)
    return _each(lambda x, y, w: x + (y + w), hh, hl, lh)


@jax.custom_vjp
def _unit_lower_inverse(lmats):
    c = lmats[0].shape[0]
    eye = lax.broadcasted_iota(jnp.int32, (c, c), 0) == lax.broadcasted_iota(jnp.int32, (c, c), 1)
    a = [-m for m in lmats]
    tmat = [jnp.where(eye, 1.0, 0.0).astype(F32) + m for m in a]
    span = 2
    while span < c:
        a = _dot3_each(a, a)
        tmat = _each(lambda t, u: t + u, tmat, _dot3_each(tmat, a))
        span *= 2
    return tuple(tmat)


def _unit_lower_inverse_fwd(lmats):
    tmats = _unit_lower_inverse(lmats)
    return tmats, tmats


def _unit_lower_inverse_bwd(tmats, cts):
    left = _each(lambda t, ct: _dot(t, ct, TN, HIGHEST), tmats, cts)
    return (tuple(_each(lambda m, t: -_dot(m, t, NT, HIGHEST), left, tmats)),)


_unit_lower_inverse.defvjp(_unit_lower_inverse_fwd, _unit_lower_inverse_bwd)


def _chunk_math(states, q, k, v, ba, z, prm, nw, head0, rowmask, n_heads):
    c = q.shape[0]
    heads = list(range(len(states)))
    hd = q.shape[1] // len(states)
    lane = lax.broadcasted_iota(jnp.int32, ba.shape, 1)
    sub = lax.broadcasted_iota(jnp.int32, (ba.shape[1], c), 0)
    ri = lax.broadcasted_iota(jnp.int32, (c, c), 0)
    ci = lax.broadcasted_iota(jnp.int32, (c, c), 1)
    last = lax.broadcasted_iota(jnp.int32, (c, 1), 0) == c - 1
    causal, strict = ri >= ci, ri > ci
    beta_all = _sigmoid(ba) * rowmask
    g_all = -jnp.exp(prm[0:1, :]) * _softplus(ba + prm[1:2, :]) * rowmask
    gcum_all = _dot(jnp.where(causal, 1.0, 0.0).astype(F32), g_all, precision=HIGHEST)
    gcum_t = gcum_all.T
    split = lambda t: [t[:, j * hd:(j + 1) * hd] for j in heads]
    qs, ks, vs, zs = split(q), split(k), split(v), split(z)
    beta = [jnp.sum(jnp.where(lane == head0 + j, beta_all, 0.0), axis=1, keepdims=True) for j in heads]
    gcum = [jnp.sum(jnp.where(lane == n_heads + head0 + j, gcum_all, 0.0), axis=1, keepdims=True) for j in heads]
    grow = [jnp.sum(jnp.where(sub == n_heads + head0 + j, gcum_t, 0.0), axis=0, keepdims=True) for j in heads]
    glast = _each(lambda gc: jnp.sum(jnp.where(last, gc, 0.0), axis=0, keepdims=True), gcum)
    decay = _each(lambda gc, gr: jnp.where(causal, jnp.exp(jnp.where(causal, gc - gr, 0.0)), 0.0), gcum, grow)
    eg = _each(jnp.exp, gcum)
    k_beta = _each(jnp.multiply, ks, beta)
    kk = _each(lambda a, b: _dot(a, b, NT), k_beta, ks)
    tmat = list(_unit_lower_inverse(tuple(_each(lambda m, dc: jnp.where(strict, m * dc, 0.0), kk, decay))))
    u_c = _each(_dot, tmat, _each(jnp.multiply, vs, beta))
    w_c = _each(_dot, tmat, _each(jnp.multiply, k_beta, eg))
    qk = _each(lambda a, b, dc: jnp.where(causal, _dot(a, b, NT) * dc, 0.0), qs, ks, decay)
    v_new = _each(lambda u, w, s: u - _dot(w, s), u_c, w_c, list(states))
    o = _each(lambda a, e, s, m, vn: _dot(a * e, s) + _dot(m, vn), qs, eg, list(states), qk, v_new)
    k_dec = _each(lambda a, gl, gc: a * jnp.exp(gl - gc), ks, glast, gcum)
    new_states = _each(lambda s, gl, kd, vn: s * jnp.exp(gl) + _dot(kd, vn, TN), list(states), glast, k_dec, v_new)
    ys = _each(lambda oj, zj: _rmsnorm(oj, nw) * _silu(zj), o, zs)
    return jnp.concatenate(ys, axis=1), tuple(new_states)


def _chunk_specs(nc, hd, n_heads, z_off, ba_off, rev):
    cidx = (lambda c: nc - 1 - c) if rev else (lambda c: c)
    hb = min(HEADS_PER_STEP, n_heads)
    assert n_heads % hb == 0 and z_off % (hb * hd) == 0 and ba_off % LANES == 0
    blk = lambda off: pl.BlockSpec((CHUNK, hb * hd), lambda c, g: (cidx(c), off + g))
    ba_spec = lambda off: pl.BlockSpec((CHUNK, LANES), lambda c, g: (cidx(c), off // LANES))
    prm_spec = pl.BlockSpec((8, LANES), lambda c, g: (0, 0))
    nw_spec = pl.BlockSpec((1, hd), lambda c, g: (0, 0))
    st_spec = pl.BlockSpec((1, hb, hd, hd), lambda c, g: (cidx(c), g, 0, 0))
    return blk, ba_spec, prm_spec, nw_spec, st_spec, blk(z_off // (hb * hd))


def _rowmask(chunk_idx, pad):
    row = chunk_idx * CHUNK + lax.broadcasted_iota(jnp.int32, (CHUNK, 1), 0)
    return jnp.where(row >= pad, 1.0, 0.0).astype(F32)


def _chunk_fwd(qn, kn, vv, proj, z_off, ba_off, prm, nw, n_heads, pad):
    lp, dn = qn.shape
    hd = dn // n_heads
    nc = lp // CHUNK
    hb = min(HEADS_PER_STEP, n_heads)

    def body(q_ref, k_ref, v_ref, ba_ref, z_ref, prm_ref, nw_ref, y_ref, hist_ref, st_ref):
        c, g = pl.program_id(0), pl.program_id(1)

        @pl.when(c == 0)
        def _():
            for j in range(hb):
                st_ref[g * hb + j] = jnp.zeros((hd, hd), F32)

        states = tuple(st_ref[g * hb + j] for j in range(hb))
        for j in range(hb):
            hist_ref[0, j] = states[j]
        y, new_states = _chunk_math(states, q_ref[...], k_ref[...], v_ref[...], ba_ref[...], z_ref[...], prm_ref[...],
                                    nw_ref[...], g * hb, _rowmask(c, pad), n_heads)
        y_ref[...] = y.astype(BF16)
        for j in range(hb):
            st_ref[g * hb + j] = new_states[j]

    blk, ba_spec, prm_spec, nw_spec, st_spec, z_spec = _chunk_specs(nc, hd, n_heads, z_off, ba_off, False)
    return _call(
        body, name="chunk_fwd", grid=(nc, n_heads // hb),
        in_specs=[blk(0), blk(0), blk(0), ba_spec(ba_off), z_spec, prm_spec, nw_spec], out_specs=[blk(0), st_spec],
        out_shape=[jax.ShapeDtypeStruct((lp, dn), BF16), jax.ShapeDtypeStruct((nc, n_heads, hd, hd), F32)],
        scratch_shapes=[pltpu.VMEM((n_heads, hd, hd), F32)],
        compiler_params=_params(("arbitrary", "arbitrary")),
    )(qn, kn, vv, proj, proj, prm, nw)


def _chunk_bwd(qn, kn, vv, proj, z_off, ba_off, prm, nw, hist, dy, n_heads, pad):
    lp, dn = qn.shape
    hd = dn // n_heads
    nc = lp // CHUNK
    hb = min(HEADS_PER_STEP, n_heads)

    def body(q_ref, k_ref, v_ref, ba_ref, z_ref, prm_ref, nw_ref, hist_ref, dy_ref,
             dq_ref, dk_ref, dv_ref, dba_ref, dz_ref, dprm_ref, dnw_ref, dst_ref):
        step, g = pl.program_id(0), pl.program_id(1)

        @pl.when(step == 0)
        def _():
            for j in range(hb):
                dst_ref[g * hb + j] = jnp.zeros((hd, hd), F32)

        @pl.when((step == 0) & (g == 0))
        def _():
            dprm_ref[...] = jnp.zeros_like(dprm_ref)
            dnw_ref[...] = jnp.zeros_like(dnw_ref)

        @pl.when(g == 0)
        def _():
            dba_ref[...] = jnp.zeros_like(dba_ref)

        fn = functools.partial(_chunk_math, head0=g * hb, rowmask=_rowmask(nc - 1 - step, pad), n_heads=n_heads)
        states = tuple(hist_ref[0, j] for j in range(hb))
        _, vjp = jax.vjp(fn, states, q_ref[...], k_ref[...], v_ref[...], ba_ref[...], z_ref[...], prm_ref[...], nw_ref[...])
        dst, dq, dk, dv, dba, dz, dprm, dnw = vjp((dy_ref[...], tuple(dst_ref[g * hb + j] for j in range(hb))))
        for j in range(hb):
            dst_ref[g * hb + j] = dst[j]
        dq_ref[...] = dq
        dk_ref[...] = dk
        dv_ref[...] = dv
        dz_ref[...] = dz.astype(BF16)
        dba_ref[...] += dba
        dprm_ref[...] += dprm
        dnw_ref[...] += dnw

    blk, ba_spec, prm_spec, nw_spec, st_spec, z_spec = _chunk_specs(nc, hd, n_heads, z_off, ba_off, True)
    f32_full = jax.ShapeDtypeStruct((lp, dn), F32)
    return _call(
        body, name="chunk_bwd", grid=(nc, n_heads // hb),
        in_specs=[blk(0), blk(0), blk(0), ba_spec(ba_off), z_spec, prm_spec, nw_spec, st_spec, blk(0)],
        out_specs=[blk(0), blk(0), blk(0), ba_spec(0), blk(0), prm_spec, nw_spec],
        out_shape=[f32_full, f32_full, f32_full, jax.ShapeDtypeStruct((lp, LANES), F32), jax.ShapeDtypeStruct((lp, dn), BF16),
                   jax.ShapeDtypeStruct((8, LANES), F32), jax.ShapeDtypeStruct((1, hd), F32)],
        scratch_shapes=[pltpu.VMEM((n_heads, hd, hd), F32)],
        compiler_params=_params(("arbitrary", "arbitrary")),
    )(qn, kn, vv, proj, proj, prm, nw, hist, dy)


def _merge_math(p, q, gp, gd):
    return _sigmoid(gp) * p + _sigmoid(gd) * q


def _merge_specs(lp, d, gp_off, gd_off):
    tr, tc = _tile(lp, 264, 16), _tile(d, 1024, LANES)
    blk = pl.BlockSpec((tr, tc), lambda i, j: (i, j))
    gp_spec = pl.BlockSpec((tr, tc), lambda i, j: (i, gp_off // tc + j))
    gd_spec = pl.BlockSpec((tr, tc), lambda i, j: (i, gd_off // tc + j))
    return (lp // tr, d // tc), blk, gp_spec, gd_spec


def _merge_fwd(p, q, proj, gp_off, gd_off):
    lp, d = p.shape
    grid, blk, gp_spec, gd_spec = _merge_specs(lp, d, gp_off, gd_off)

    def body(p_ref, q_ref, gp_ref, gd_ref, o_ref):
        o_ref[...] = _merge_math(p_ref[...], q_ref[...], gp_ref[...], gd_ref[...]).astype(BF16)

    return _call(
        body, name="merge_fwd", grid=grid, in_specs=[blk, blk, gp_spec, gd_spec], out_specs=blk,
        out_shape=jax.ShapeDtypeStruct((lp, d), BF16), compiler_params=_params(("parallel", "parallel")),
    )(p, q, proj, proj)


def _merge_bwd(p, q, proj, gp_off, gd_off, dm):
    lp, d = p.shape
    grid, blk, gp_spec, gd_spec = _merge_specs(lp, d, gp_off, gd_off)

    def body(p_ref, q_ref, gp_ref, gd_ref, dm_ref, dp_ref, dq_ref, dgp_ref, dgd_ref):
        _, vjp = jax.vjp(_merge_math, p_ref[...], q_ref[...], gp_ref[...], gd_ref[...])
        for ref, val in zip((dp_ref, dq_ref, dgp_ref, dgd_ref), vjp(dm_ref[...])):
            ref[...] = val.astype(BF16)

    out = jax.ShapeDtypeStruct((lp, d), BF16)
    return _call(
        body, name="merge_bwd", grid=grid, in_specs=[blk, blk, gp_spec, gd_spec, blk], out_specs=[blk] * 4,
        out_shape=[out] * 4, compiler_params=_params(("parallel", "parallel")),
    )(p, q, proj, proj, dm)


def _adamw(w, g, m, v, name):
    shape = w.shape
    w2, g2, m2, v2 = (t.reshape((-1, shape[-1])) for t in (w, g, m, v))
    rows, cols = w2.shape
    tr = _tile(rows, 128, 8)

    def body(w_ref, g_ref, m_ref, v_ref, d_ref, nm_ref, nv_ref):
        gv = g_ref[...]
        nm = ADAM_B1 * m_ref[...] + (1.0 - ADAM_B1) * gv
        nv = ADAM_B2 * v_ref[...] + (1.0 - ADAM_B2) * (gv * gv)
        m_hat = nm / (1.0 - ADAM_B1 ** ADAM_STEP)
        v_hat = nv / (1.0 - ADAM_B2 ** ADAM_STEP)
        d_ref[...] = -ADAM_LR * (m_hat / (jnp.sqrt(v_hat) + ADAM_EPS) + ADAM_WD * w_ref[...])
        nm_ref[...] = nm
        nv_ref[...] = nv

    blk = pl.BlockSpec((tr, cols), lambda i: (i, 0))
    out = jax.ShapeDtypeStruct((rows, cols), F32)
    res = _call(
        body, name=name, grid=(rows // tr,), in_specs=[blk] * 4, out_specs=[blk] * 3, out_shape=[out] * 3,
        compiler_params=_params(("parallel",)),
    )(w2, g2, m2, v2)
    return tuple(t.reshape(shape) for t in res)


def _coords():
    return lax.axis_index("x"), lax.axis_index("y"), lax.axis_index("c")


def _flip(v, bit):
    return 1 - v if bit else v


CHIP_FLIPS = ((1, 0), (0, 1), (1, 1))
ANY = pl.BlockSpec(memory_space=pl.ANY)


def _all_gather(shards):
    n = len(shards)

    def body(*refs):
        x_refs, out_refs = refs[:n], refs[n:2 * n]
        send_sems, recv_sems, local_sems = refs[2 * n:]
        x, y, c = _coords()
        sibling = (x, y, 1 - c)
        chips = [(_flip(x, fx), _flip(y, fy)) for fx, fy in CHIP_FLIPS]

        def copy(a, k, block, to, from_input=False):
            px, py, pc = block
            slot = out_refs[a].at[4 * px + 2 * py + pc]
            return pltpu.make_async_remote_copy(
                src_ref=x_refs[a] if from_input else slot, dst_ref=slot,
                send_sem=send_sems.at[7 * a + k], recv_sem=recv_sems.at[7 * a + k], device_id=to, device_id_type=MESH)

        mine = [pltpu.make_async_copy(x_refs[a], out_refs[a].at[4 * x + 2 * y + c], local_sems.at[a]) for a in range(n)]
        first = []
        for a in range(n):
            mine[a].start()
            first.append(copy(a, 0, (x, y, c), sibling, True))
            first += [copy(a, 1 + j, (x, y, c), (*chip, c), True) for j, chip in enumerate(chips)]
        for cp in first:
            cp.start()
        passed = []
        for j, chip in enumerate(chips):
            for a in range(n):
                copy(a, 1 + j, (*chip, c), (x, y, c)).wait_recv()
                passed.append(copy(a, 4 + j, (*chip, c), sibling))
                passed[-1].start()
        for a in range(n):
            copy(a, 0, (x, y, 1 - c), (x, y, c)).wait_recv()
            for j, chip in enumerate(chips):
                copy(a, 4 + j, (*chip, 1 - c), (x, y, c)).wait_recv()
        for cp in first + passed:
            cp.wait_send()
        for cp in mine:
            cp.wait()

    return _call(
        body, name="all_gather", in_specs=[ANY] * n, out_specs=[ANY] * n,
        out_shape=[jax.ShapeDtypeStruct((N_DEV,) + s.shape, s.dtype) for s in shards],
        scratch_shapes=[pltpu.SemaphoreType.DMA((7 * n,)), pltpu.SemaphoreType.DMA((7 * n,)), pltpu.SemaphoreType.DMA((n,))],
    )(*shards)


def _rs_to_sibling(gs):
    n = len(gs)

    def body(*refs):
        g_refs, got_refs = refs[:n], refs[n:2 * n]
        send_sems, recv_sems = refs[2 * n:]
        x, y, c = _coords()
        copies = []
        for a in range(n):
            for p in range(4):
                cp = pltpu.make_async_remote_copy(
                    src_ref=g_refs[a].at[2 * p + (1 - c)], dst_ref=got_refs[a].at[p], send_sem=send_sems.at[4 * a + p],
                    recv_sem=recv_sems.at[4 * a + p], device_id=(x, y, 1 - c), device_id_type=MESH)
                cp.start()
                copies.append(cp)
        for cp in copies:
            cp.wait()

    return _call(
        body, name="rs_to_sibling", in_specs=[ANY] * n, out_specs=[ANY] * n,
        out_shape=[jax.ShapeDtypeStruct((4,) + g.shape[1:], g.dtype) for g in gs],
        scratch_shapes=[pltpu.SemaphoreType.DMA((4 * n,)), pltpu.SemaphoreType.DMA((4 * n,))],
    )(*gs)


def _rs_pair_sum(g, got, c_idx, name):
    _, rows, cols = g.shape
    tr = _tile(rows, 256, 16)

    def body(c_ref, g_ref, got_ref, o_ref):
        o_ref[...] = (g_ref[...].astype(F32) + got_ref[...].astype(F32)).astype(o_ref.dtype)

    grid_spec = pltpu.PrefetchScalarGridSpec(
        num_scalar_prefetch=1, grid=(4, rows // tr),
        in_specs=[pl.BlockSpec((1, tr, cols), lambda p, i, c_ref: (2 * p + c_ref[0], i, 0)),
                  pl.BlockSpec((1, tr, cols), lambda p, i, c_ref: (p, i, 0))],
        out_specs=pl.BlockSpec((1, tr, cols), lambda p, i, c_ref: (p, i, 0)))
    return _call(
        body, name=name, grid_spec=grid_spec, out_shape=jax.ShapeDtypeStruct((4, rows, cols), g.dtype),
        compiler_params=_params(("parallel", "parallel")),
    )(c_idx, g, got)


def _rs_to_chips(partials):
    n = len(partials)

    def body(*refs):
        p_refs, got_refs = refs[:n], refs[n:2 * n]
        send_sems, recv_sems = refs[2 * n:]
        x, y, c = _coords()
        copies = []
        for a in range(n):
            for k, (fx, fy) in enumerate(CHIP_FLIPS):
                px, py = _flip(x, fx), _flip(y, fy)
                cp = pltpu.make_async_remote_copy(
                    src_ref=p_refs[a].at[2 * px + py], dst_ref=got_refs[a].at[k], send_sem=send_sems.at[3 * a + k],
                    recv_sem=recv_sems.at[3 * a + k], device_id=(px, py, c), device_id_type=MESH)
                cp.start()
                copies.append(cp)
        for cp in copies:
            cp.wait()

    return _call(
        body, name="rs_to_chips", in_specs=[ANY] * n, out_specs=[ANY] * n,
        out_shape=[jax.ShapeDtypeStruct((3,) + p.shape[1:], p.dtype) for p in partials],
        scratch_shapes=[pltpu.SemaphoreType.DMA((3 * n,)), pltpu.SemaphoreType.DMA((3 * n,))],
    )(*partials)


def _rs_chip_sum(partial, got, chip_idx, name):
    _, rows, cols = partial.shape
    tr = _tile(rows, 256, 16)

    def body(p_idx_ref, p_ref, got_ref, o_ref):
        o_ref[...] = ((p_ref[0].astype(F32) + got_ref[0].astype(F32)) + got_ref[1].astype(F32)) + got_ref[2].astype(F32)

    grid_spec = pltpu.PrefetchScalarGridSpec(
        num_scalar_prefetch=1, grid=(rows // tr,),
        in_specs=[pl.BlockSpec((1, tr, cols), lambda i, p_ref: (p_ref[0], i, 0)),
                  pl.BlockSpec((3, tr, cols), lambda i, p_ref: (0, i, 0))],
        out_specs=pl.BlockSpec((tr, cols), lambda i, p_ref: (i, 0)))
    return _call(
        body, name=name, grid_spec=grid_spec, out_shape=jax.ShapeDtypeStruct((rows, cols), F32),
        compiler_params=_params(("parallel",)),
    )(chip_idx, partial, got)


def _reduce_scatter(gs):
    x, y, c = _coords()
    c_idx = jnp.reshape(c, (1,)).astype(jnp.int32)
    chip_idx = jnp.reshape(2 * x + y, (1,)).astype(jnp.int32)
    gots = _rs_to_sibling(gs)
    partials = [_rs_pair_sum(g, got, c_idx, "rs_pair_sum_%d" % a) for a, (g, got) in enumerate(zip(gs, gots))]
    gots2 = _rs_to_chips(partials)
    return [_rs_chip_sum(p, got, chip_idx, "rs_chip_sum_%d" % a) for a, (p, got) in enumerate(zip(partials, gots2))]


RUNS = 3
RUN_FIELDS = 6


def _lane_gather_table(src_of, src_width):
    n_blocks = src_of.shape[0] // LANES
    tab = np.zeros((n_blocks + 1, RUNS, RUN_FIELDS), np.int32)
    tab[:, :, 5] = LANES
    for t in range(n_blocks):
        runs = []
        for lane in range(LANES):
            slab, col = (int(v) for v in src_of[t * LANES + lane])
            if slab < 0:
                continue
            key = (slab, col // LANES, col % LANES - lane)
            if runs and runs[-1][0] == key and runs[-1][2] == lane:
                runs[-1][2] = lane + 1
            else:
                runs.append([key, lane, lane + 1])
        assert len(runs) <= RUNS
        for e, (key, lo, hi) in enumerate(runs):
            tab[t, e] = (key[0], key[1], key[2], lo, hi, min(LANES, src_width - key[1] * LANES))
    return tab.reshape(-1)


def _lane_gather(src, table, out_slabs, out_width, name):
    _, rows, _ = src.shape
    blocks_per_slab = -(-out_width // LANES)

    def body(tab_ref, *refs):
        o_ref = refs[RUNS]
        t = pl.program_id(0)
        lane = lax.broadcasted_iota(jnp.int32, (1, LANES), 1)
        li = lax.broadcasted_iota(jnp.int32, (LANES, LANES), 0)
        ci = lax.broadcasted_iota(jnp.int32, (LANES, LANES), 1)
        acc = None
        for e in range(RUNS):
            base = (t * RUNS + e) * RUN_FIELDS
            shift, lo, hi, valid = tab_ref[base + 2], tab_ref[base + 3], tab_ref[base + 4], tab_ref[base + 5]
            a = jnp.where(lane < valid, refs[e][0], 0.0).astype(BF16)
            sel = jnp.where((li == ci + shift) & (ci >= lo) & (ci < hi), 1.0, 0.0).astype(BF16)
            part = _dot(a, sel)
            acc = part if acc is None else acc + part
        o_ref[0] = acc.astype(BF16)

    def src_spec(e):
        return pl.BlockSpec((1, rows, LANES), lambda t, tab: (tab[(t * RUNS + e) * RUN_FIELDS], 0, tab[(t * RUNS + e) * RUN_FIELDS + 1]))

    grid_spec = pltpu.PrefetchScalarGridSpec(
        num_scalar_prefetch=1, grid=(out_slabs * blocks_per_slab,), in_specs=[src_spec(e) for e in range(RUNS)],
        out_specs=pl.BlockSpec((1, rows, LANES), lambda t, tab: (t // blocks_per_slab, 0, t % blocks_per_slab)))
    return _call(
        body, name=name, grid_spec=grid_spec, out_shape=jax.ShapeDtypeStruct((out_slabs, rows, out_width), BF16),
        compiler_params=_params(("parallel",)),
    )(jnp.asarray(table), src, src, src)


def _all_reduce_small(vec):
    rows, cols = vec.shape

    def body(v_ref, o_ref, buf, send_sems, recv_sems):
        x, y, c = _coords()
        me = 4 * x + 2 * y + c
        buf[me] = v_ref[...]
        copies = []
        for k in range(N_DEV - 1):
            fx, fy, fc = ((k + 1) >> 2) & 1, ((k + 1) >> 1) & 1, (k + 1) & 1
            cp = pltpu.make_async_remote_copy(
                src_ref=v_ref, dst_ref=buf.at[me], send_sem=send_sems.at[k], recv_sem=recv_sems.at[k],
                device_id=(_flip(x, fx), _flip(y, fy), _flip(c, fc)), device_id_type=MESH)
            cp.start()
            copies.append(cp)
        for cp in copies:
            cp.wait()
        total = buf[0]
        for j in range(1, N_DEV):
            total = total + buf[j]
        o_ref[...] = total

    vmem = pl.BlockSpec(memory_space=pltpu.VMEM)
    return _call(
        body, name="all_reduce_small", in_specs=[vmem], out_specs=vmem,
        out_shape=jax.ShapeDtypeStruct((rows, cols), F32),
        scratch_shapes=[pltpu.VMEM((N_DEV, rows, cols), F32), pltpu.SemaphoreType.DMA((N_DEV - 1,)),
                        pltpu.SemaphoreType.DMA((N_DEV - 1,))],
    )(vec)


def _w_in_column_maps(ns, o_ba, n_logit, n_main, n_all):
    own = np.arange(N_DEV * ns)
    work_of_own = np.where(own < o_ba, own, np.where(own < o_ba + n_logit, n_main + own - o_ba, own - n_logit))
    to_work = np.full((n_all, 2), -1, np.int64)
    to_work[work_of_own, 0] = own // ns
    to_work[work_of_own, 1] = own % ns
    slab_width = -(-ns // LANES) * LANES
    to_own = np.full((N_DEV, slab_width, 2), -1, np.int64)
    to_own[:, :ns, 0] = 0
    to_own[:, :ns, 1] = work_of_own.reshape(N_DEV, ns)
    return to_work, to_own.reshape(-1, 2)


def kernel(x, meta_tokens, norm_w, w_in, conv_w, A_log, dt_bias, pool_mix, pool_scale, dn_norm_w, w_pool_out, w_dn_out, w_o, final_norm_w, loss_target, m_meta_tokens, m_norm_w, m_w_in, m_conv_w, m_A_log, m_dt_bias, m_pool_mix, m_pool_scale, m_dn_norm_w, m_w_pool_out, m_w_dn_out, m_w_o, m_final_norm_w, v_meta_tokens, v_norm_w, v_w_in, v_conv_w, v_A_log, v_dt_bias, v_pool_mix, v_pool_scale, v_dn_norm_w, v_w_pool_out, v_w_dn_out, v_w_o, v_final_norm_w):
    seq, d = x.shape[1], x.shape[2]
    n_meta = meta_tokens.shape[0]
    n_heads, hd = A_log.shape[-1], dn_norm_w.shape[-1]
    dn = n_heads * hd
    pw, ng = pool_scale.shape[-1], pool_mix.shape[1]
    pg = pw // ng
    kw = conv_w.shape[1]
    pad = (-n_meta) % CHUNK
    x0 = pad + n_meta
    lp = x0 + seq
    ns = w_in.shape[-1]
    in_cols = N_DEV * ns
    o_q, o_k, o_v, o_zd = 2 * pw, 2 * pw + dn, 2 * pw + 2 * dn, 2 * pw + 3 * dn
    o_ba = 2 * pw + 4 * dn
    o_gp, o_gd = o_ba, o_ba + d
    n_main = o_gd + d
    n_all = n_main + 2 * LANES
    assert lp % CHUNK == 0 and in_cols == n_main + 2 * n_heads and 2 * n_heads <= LANES and hd == LANES
    cs, ms = conv_w.shape[-1], meta_tokens.shape[-1]
    mr = pool_mix.shape[2]
    assert ms == pg and cs % pg == 0
    to_work, to_own = _w_in_column_maps(ns, o_ba, 2 * n_heads, n_main, n_all)
    cols_major = lambda t: jnp.transpose(t, (1, 0, 2)).reshape(t.shape[1], N_DEV * t.shape[2])

    win_g, wpo_g, wdn_g, wo_g, mix_g, conv_g, meta_g = _all_gather(
        [w_in[0].astype(BF16), w_pool_out[0].astype(BF16), w_dn_out[0].astype(BF16), w_o[0].astype(BF16),
         pool_mix[0].reshape(ng * mr, pg).astype(BF16), conv_w[0], meta_tokens])
    w_all = _lane_gather(win_g, _lane_gather_table(to_work, ns), 1, n_all, "w_in_to_work").reshape(d, n_all)
    wpo_f = cols_major(wpo_g)
    wdn_f = wdn_g.reshape(dn, d)
    wo_f = wo_g.reshape(d, d)
    mix_f = jnp.transpose(mix_g.reshape(N_DEV, ng, mr, pg), (1, 0, 2, 3)).reshape(ng, pg, pg)
    conv_f = cols_major(conv_g)
    meta_f = cols_major(meta_g)

    h0, xn = _norm_in_fwd(x[0], meta_f, norm_w, pad)
    proj = _matmul(xn, w_all, NN, F32, 1056, 768, 2048, "proj")
    y_pool = _pool_fwd(proj, mix_f, pool_scale, pad)
    conv_q, conv_k, conv_v = (conv_f[:, i * dn:(i + 1) * dn] for i in range(3))
    qn = _conv_fwd(proj, o_q, conv_q, hd, float(hd) ** -0.5, "conv_q_fwd")
    kn = _conv_fwd(proj, o_k, conv_k, hd, 1.0, "conv_k_fwd")
    vv = _conv_fwd(proj, o_v, conv_v, hd, None, "conv_v_fwd")
    logit_lanes = (n_heads, LANES - 2 * n_heads)
    prm = jnp.pad(A_log, ((0, 7), logit_lanes)) + jnp.pad(dt_bias, ((1, 6), logit_lanes))
    y_dn, hist = _chunk_fwd(qn, kn, vv, proj, o_zd, n_main, prm, dn_norm_w, n_heads, pad)
    p_out = _matmul(y_pool, wpo_f, NN, F32, 1056, 1024, 1024, "pool_out")
    q_out = _matmul(y_dn, wdn_f, NN, F32, 1056, 1024, 2048, "dn_out")
    merged = _merge_fwd(p_out, q_out, proj, o_gp, o_gd)
    mo = _matmul(merged, wo_f, NN, F32, 1056, 1024, 2048, "w_o_fwd")
    dh1, d_fw, loss_part = _final_loss(h0, mo, final_norm_w.reshape(1, d), loss_target[0], x0)

    d_merged = _matmul(dh1, wo_f, NT, F32, 1056, 1024, 1024, "w_o_bwd_x")
    g_wo = _matmul(merged, dh1, TN, BF16, 1024, 1024, 704, "w_o_bwd_w")
    d_p, d_q, d_gp, d_gd = _merge_bwd(p_out, q_out, proj, o_gp, o_gd, d_merged)
    d_ypool = _matmul(d_p, wpo_f, NT, F32, 1056, 1024, 2048, "pool_out_bwd_x")
    g_wpo = _matmul(y_pool, d_p, TN, BF16, 1024, 1024, 704, "pool_out_bwd_w", col_blocks=N_DEV)
    d_ydn = _matmul(d_q, wdn_f, NT, F32, 1056, 1024, 2048, "dn_out_bwd_x")
    g_wdn = _matmul(y_dn, d_q, TN, BF16, 1024, 1024, 704, "dn_out_bwd_w")
    d_u, d_zp, g_mix, g_pscale = _pool_bwd(proj, mix_f, pool_scale, d_ypool, pad)
    d_qn, d_kn, d_vv, d_ba, d_zd, d_prm, g_dnw = _chunk_bwd(qn, kn, vv, proj, o_zd, n_main, prm, dn_norm_w, hist, d_ydn, n_heads, pad)
    d_qr, g_cq = _conv_bwd(proj, o_q, conv_q, d_qn, hd, float(hd) ** -0.5, pad, "conv_q_bwd")
    d_kr, g_ck = _conv_bwd(proj, o_k, conv_k, d_kn, hd, 1.0, pad, "conv_k_bwd")
    d_vr, g_cv = _conv_bwd(proj, o_v, conv_v, d_vv, hd, None, pad, "conv_v_bwd")
    d_proj = jnp.concatenate([d_u, d_zp, d_qr, d_kr, d_vr, d_zd, d_gp, d_gd, d_ba.astype(BF16), jnp.zeros((lp, LANES), BF16)], axis=1)
    g_wall = _matmul(xn, d_proj, TN, F32, 1024, 768, 704, "w_in_bwd_w")
    d_xn = _matmul(d_proj, w_all, NT, F32, 1056, 1024, 768, "w_in_bwd_x")
    d_h0, g_nw = _norm_in_bwd(h0, norm_w, d_xn, dh1)
    grad_x = d_h0[x0:][None]

    g_win = _lane_gather(g_wall.reshape(1, d, n_all), _lane_gather_table(to_own, n_all), N_DEV, ns, "w_in_grad_to_own")
    by_cols = lambda t: jnp.transpose(t.reshape(t.shape[0], N_DEV, t.shape[1] // N_DEV), (1, 0, 2))
    g_conv = by_cols(jnp.concatenate([g_cq, g_ck, g_cv], axis=1)).reshape(N_DEV, kw * cs // pg, pg)
    conv_rows = -(-g_conv.shape[1] // 16) * 16
    g_small = jnp.concatenate(
        [jnp.transpose(g_mix.reshape(ng, N_DEV, mr, pg), (1, 0, 2, 3)).reshape(N_DEV, ng * mr, pg), by_cols(d_h0[pad:x0]),
         jnp.pad(g_conv, ((0, 0), (0, conv_rows - g_conv.shape[1]), (0, 0)))], axis=1).astype(BF16)
    r_win, r_wpo, r_wdn, r_wo, r_small = _reduce_scatter(
        [g_win, g_wpo, g_wdn.reshape(N_DEV, dn // N_DEV, d), g_wo.reshape(N_DEV, d // N_DEV, d), g_small])
    r_mix, r_meta = r_small[:ng * mr], r_small[ng * mr:ng * mr + n_meta]
    r_conv = r_small[ng * mr + n_meta:ng * mr + n_meta + kw * cs // pg]

    small = [g_nw[0], d_fw[0], g_pscale[0], g_dnw[0], d_prm[0], d_prm[1], loss_part[0]]
    s_sizes = [t.shape[0] for t in small]
    s_cols = -(-sum(s_sizes) // (8 * LANES)) * LANES
    s_vec = jnp.concatenate(small + [jnp.zeros((8 * s_cols - sum(s_sizes),), F32)]).reshape(8, s_cols)
    s_sum = _all_reduce_small(s_vec).reshape(-1)
    s_offs = [sum(s_sizes[:i]) for i in range(len(s_sizes))]
    s_take = lambda i, n=None, o=0: s_sum[s_offs[i] + o:s_offs[i] + o + (s_sizes[i] if n is None else n)]

    grads = {
        "meta_tokens": r_meta, "norm_w": s_take(0).reshape(norm_w.shape),
        "w_in": r_win.reshape(w_in.shape), "conv_w": r_conv.reshape(conv_w.shape),
        "A_log": s_take(4, n_heads, n_heads).reshape(A_log.shape), "dt_bias": s_take(5, n_heads, n_heads).reshape(dt_bias.shape),
        "pool_mix": r_mix.reshape(pool_mix.shape), "pool_scale": s_take(2).reshape(pool_scale.shape),
        "dn_norm_w": s_take(3).reshape(dn_norm_w.shape), "w_pool_out": r_wpo.reshape(w_pool_out.shape),
        "w_dn_out": r_wdn.reshape(w_dn_out.shape), "w_o": r_wo.reshape(w_o.shape),
        "final_norm_w": s_take(1).reshape(final_norm_w.shape),
    }
    loss = s_take(6, 1)[0]

    weights = dict(meta_tokens=meta_tokens, norm_w=norm_w, w_in=w_in, conv_w=conv_w, A_log=A_log, dt_bias=dt_bias,
                   pool_mix=pool_mix, pool_scale=pool_scale, dn_norm_w=dn_norm_w, w_pool_out=w_pool_out, w_dn_out=w_dn_out,
                   w_o=w_o, final_norm_w=final_norm_w)
    m_in = dict(meta_tokens=m_meta_tokens, norm_w=m_norm_w, w_in=m_w_in, conv_w=m_conv_w, A_log=m_A_log, dt_bias=m_dt_bias,
                pool_mix=m_pool_mix, pool_scale=m_pool_scale, dn_norm_w=m_dn_norm_w, w_pool_out=m_w_pool_out,
                w_dn_out=m_w_dn_out, w_o=m_w_o, final_norm_w=m_final_norm_w)
    v_in = dict(meta_tokens=v_meta_tokens, norm_w=v_norm_w, w_in=v_w_in, conv_w=v_conv_w, A_log=v_A_log, dt_bias=v_dt_bias,
                pool_mix=v_pool_mix, pool_scale=v_pool_scale, dn_norm_w=v_dn_norm_w, w_pool_out=v_w_pool_out,
                w_dn_out=v_w_dn_out, w_o=v_w_o, final_norm_w=v_final_norm_w)
    names = list(weights)
    upd = {n: _adamw(weights[n], grads[n], m_in[n], v_in[n], "adamw_" + n) for n in names}
    return (loss, grad_x, *[grads[n] for n in names], *[upd[n][0] for n in names], *[upd[n][1] for n in names],
            *[upd[n][2] for n in names])
```

```python
import functools

import jax
import jax.numpy as jnp
import numpy as np
from jax import lax
from jax.experimental import pallas as pl
from jax.experimental.pallas import tpu as pltpu

F32 = jnp.float32
BF16 = jnp.bfloat16
HIGHEST = lax.Precision.HIGHEST
MESH = pl.DeviceIdType.MESH

CHUNK = 64
NORM_EPS = 1e-6
POOL_WINDOWS = (2, 4, 8, 16)
ADAM_LR, ADAM_B1, ADAM_B2, ADAM_EPS, ADAM_WD, ADAM_STEP = 0.001, 0.9, 0.999, 1e-08, 0.01, 10
N_DEV = 8
LANES = 128
VMEM_LIMIT = 48 * 1024 * 1024

NN = (((1,), (0,)), ((), ()))
NT = (((1,), (1,)), ((), ()))
TN = (((0,), (0,)), ((), ()))


def _call(body, **kw):
    return pl.pallas_call(body, **kw)


def _params(sem=None):
    return pltpu.CompilerParams(dimension_semantics=sem, vmem_limit_bytes=VMEM_LIMIT)


def _tile(n, pref, align):
    for d in range(min(pref, n), 0, -1):
        if n % d == 0 and d % align == 0:
            return d
    return n


def _dot(a, b, dims=NN, precision=None):
    return lax.dot_general(a, b, dims, precision=precision, preferred_element_type=F32)


def _sigmoid(x):
    return 1.0 / (1.0 + jnp.exp(-x))


def _silu(x):
    return x * _sigmoid(x)


def _softplus(x):
    return jnp.maximum(x, 0.0) + jnp.log(1.0 + jnp.exp(-jnp.abs(x)))


def _rmsnorm(x, w):
    return x * lax.rsqrt(jnp.mean(x * x, axis=-1, keepdims=True) + NORM_EPS) * w


def _shift_down(x, j, row):
    if j == 0:
        return x
    return jnp.where(row >= j, pltpu.roll(x, j, 0), 0.0)


def _shift_up(x, j, row):
    if j == 0:
        return x
    n = x.shape[0]
    return jnp.where(row < n - j, pltpu.roll(x, n - j, 0), 0.0)


def _matmul(a, b, dims, out_dtype, tm, tn, tk, name, col_blocks=None, after=None):
    ta = dims == TN
    tb = dims == NT
    m, kdim = (a.shape[1], a.shape[0]) if ta else a.shape
    n = b.shape[0] if tb else b.shape[1]
    if col_blocks:
        tn = n // col_blocks
    tm, tn, tk = _tile(m, tm, 8), _tile(n, tn, LANES), _tile(kdim, tk, LANES if not ta else 16)
    nk = kdim // tk

    n_extra = 0 if after is None else 1

    def body(a_ref, b_ref, *refs):
        o_ref, scratch = refs[n_extra], refs[n_extra + 1:]
        part = _dot(a_ref[...].astype(BF16), b_ref[...].astype(BF16), dims)
        if nk == 1:
            o_ref[...] = part.astype(o_ref.dtype).reshape(o_ref.shape)
            return
        acc_ref, = scratch
        k = pl.program_id(2)

        @pl.when(k == 0)
        def _():
            acc_ref[...] = part

        @pl.when(k > 0)
        def _():
            acc_ref[...] += part

        @pl.when(k == nk - 1)
        def _():
            o_ref[...] = acc_ref[...].astype(o_ref.dtype).reshape(o_ref.shape)

    a_spec = pl.BlockSpec((tk, tm), lambda i, j, k: (k, i)) if ta else pl.BlockSpec((tm, tk), lambda i, j, k: (i, k))
    b_spec = pl.BlockSpec((tn, tk), lambda i, j, k: (j, k)) if tb else pl.BlockSpec((tk, tn), lambda i, j, k: (k, j))
    if col_blocks:
        out_spec = pl.BlockSpec((1, tm, tn), lambda i, j, k: (j, i, 0))
        out_shape = jax.ShapeDtypeStruct((col_blocks, m, tn), out_dtype)
    else:
        out_spec = pl.BlockSpec((tm, tn), lambda i, j, k: (i, j))
        out_shape = jax.ShapeDtypeStruct((m, n), out_dtype)
    return _call(
        body, name=name, grid=(m // tm, n // tn, nk),
        in_specs=[a_spec, b_spec] + [ANY] * n_extra, out_specs=out_spec, out_shape=out_shape,
        scratch_shapes=[] if nk == 1 else [pltpu.VMEM((tm, tn), F32)],
        compiler_params=_params(("parallel", "parallel", "arbitrary")),
    )(a, b, *([] if after is None else [after]))


def _norm_in_fwd(x2d, meta, w, pad):
    seq, d = x2d.shape
    tr = pad + meta.shape[0]
    assert seq % tr == 0 and tr % 16 == 0
    lp = tr + seq

    def body(x_ref, m_ref, w_ref, h_ref, o_ref):
        def emit(h):
            h_ref[...] = h
            o_ref[...] = _rmsnorm(h, w_ref[...]).astype(BF16)

        @pl.when(pl.program_id(0) == 0)
        def _():
            emit(jnp.concatenate([jnp.zeros((pad, d), F32), m_ref[...]], axis=0) if pad else m_ref[...])

        @pl.when(pl.program_id(0) > 0)
        def _():
            emit(x_ref[...])

    row = pl.BlockSpec((tr, d), lambda i: (i, 0))
    return _call(
        body, name="norm_in_fwd", grid=(lp // tr,),
        in_specs=[pl.BlockSpec((tr, d), lambda i: (jnp.maximum(i - 1, 0), 0)), pl.BlockSpec(meta.shape, lambda i: (0, 0)),
                  pl.BlockSpec((1, d), lambda i: (0, 0))],
        out_specs=[row, row],
        out_shape=[jax.ShapeDtypeStruct((lp, d), F32), jax.ShapeDtypeStruct((lp, d), BF16)],
        compiler_params=_params(("arbitrary",)),
    )(x2d, meta, w)


def _norm_in_bwd(h0, w, dxn, dh1):
    lp, d = h0.shape
    tr = _tile(lp, 264, 8)

    def body(h_ref, w_ref, da_ref, dh1_ref, dh_ref, dw_ref):
        _, vjp = jax.vjp(_rmsnorm, h_ref[...], w_ref[...])
        dh, dw = vjp(da_ref[...])
        dh_ref[...] = dh + dh1_ref[...]

        @pl.when(pl.program_id(0) == 0)
        def _():
            dw_ref[...] = jnp.zeros_like(dw_ref)

        dw_ref[...] += dw

    row = pl.BlockSpec((tr, d), lambda i: (i, 0))
    vec = pl.BlockSpec((1, d), lambda i: (0, 0))
    return _call(
        body, name="norm_in_bwd", grid=(lp // tr,),
        in_specs=[row, vec, row, row], out_specs=[row, vec],
        out_shape=[jax.ShapeDtypeStruct((lp, d), F32), jax.ShapeDtypeStruct((1, d), F32)],
        compiler_params=_params(("arbitrary",)),
    )(h0, w, dxn, dh1)


def _final_loss(h0, mo, fw, tgt, x0):
    lp, d = h0.shape
    tr = x0
    assert lp % tr == 0

    def body(h_ref, mo_ref, fw_ref, t_ref, dh_ref, dw_ref, loss_ref):
        i = pl.program_id(0)
        row = i * tr + lax.broadcasted_iota(jnp.int32, (tr, 1), 0)
        mask = jnp.where(row >= x0, 1.0, 0.0).astype(F32)
        tgt_v = t_ref[...]

        def loss_fn(h1, w):
            err = _rmsnorm(h1, w) - tgt_v
            return 0.5 * jnp.sum(jnp.mean(err * err, axis=-1, keepdims=True) * mask, axis=0, keepdims=True)

        loss, vjp = jax.vjp(loss_fn, h_ref[...] + mo_ref[...], fw_ref[...])
        dh, dw = vjp(jnp.ones((1, 1), F32))
        dh_ref[...] = dh

        @pl.when(i == 0)
        def _():
            dw_ref[...] = jnp.zeros_like(dw_ref)
            loss_ref[...] = jnp.zeros_like(loss_ref)

        dw_ref[...] += dw
        loss_ref[...] += jnp.broadcast_to(loss, loss_ref.shape)

    row_spec = pl.BlockSpec((tr, d), lambda i: (i, 0))
    vec = pl.BlockSpec((1, d), lambda i: (0, 0))
    return _call(
        body, name="final_loss", grid=(lp // tr,),
        in_specs=[row_spec, row_spec, vec, pl.BlockSpec((tr, d), lambda i: (jnp.maximum(i - 1, 0), 0))],
        out_specs=[row_spec, vec, pl.BlockSpec((8, LANES), lambda i: (0, 0))],
        out_shape=[jax.ShapeDtypeStruct((lp, d), F32), jax.ShapeDtypeStruct((1, d), F32), jax.ShapeDtypeStruct((8, LANES), F32)],
        compiler_params=_params(("arbitrary",)),
    )(h0, mo, fw, tgt)


def _pool_select(parts, g):
    out = parts[-1]
    for gi in range(len(parts) - 2, -1, -1):
        out = jnp.where(g == gi, parts[gi], out)
    return out


def _pool_count(row, g, pad):
    win = _pool_select([jnp.full(row.shape, float(w), F32) for w in POOL_WINDOWS], g)
    return jnp.maximum(jnp.minimum((row - pad + 1).astype(F32), win), 1.0)


def _pooled(u, g, row, pad):
    sums, s, span = [], u, 1
    for w in POOL_WINDOWS:
        while span < w:
            s = s + _shift_down(s, span, row)
            span *= 2
        sums.append(s)
    return _pool_select(sums, g) / _pool_count(row, g, pad) - u


def _pooled_adjoint(dp, g, row, pad):
    e = dp / _pool_count(row, g, pad)
    sums, s, span = [], e, 1
    for w in POOL_WINDOWS:
        while span < w:
            s = s + _shift_up(s, span, row)
            span *= 2
        sums.append(s)
    return _pool_select(sums, g) - dp


def _pool_specs(lp, pg, ng, z_off):
    u_spec = pl.BlockSpec((lp, pg), lambda g: (0, g))
    z_spec = pl.BlockSpec((lp, pg), lambda g: (0, z_off + g))
    mix_spec = pl.BlockSpec((1, pg, pg), lambda g: (g, 0, 0))
    vec_spec = pl.BlockSpec((1, pg), lambda g: (0, g))
    return u_spec, z_spec, mix_spec, vec_spec


def _pool_fwd(proj, mix, scale, pad):
    lp = proj.shape[0]
    ng, pg, _ = mix.shape
    pw = ng * pg

    def body(u_ref, z_ref, mix_ref, sc_ref, y_ref):
        g = pl.program_id(0)
        row = lax.broadcasted_iota(jnp.int32, (lp, 1), 0)
        pooled = _pooled(u_ref[...], g, row, pad)
        mixed = _dot(pooled.astype(BF16), mix_ref[0])
        y_ref[...] = (mixed * sc_ref[...] * _silu(z_ref[...])).astype(BF16)

    u_spec, z_spec, mix_spec, vec_spec = _pool_specs(lp, pg, ng, pw // pg)
    return _call(
        body, name="pool_fwd", grid=(ng,), in_specs=[u_spec, z_spec, mix_spec, vec_spec], out_specs=u_spec,
        out_shape=jax.ShapeDtypeStruct((lp, pw), BF16), compiler_params=_params(("parallel",)),
    )(proj, proj, mix, scale)


def _pool_bwd(proj, mix, scale, dy, pad):
    lp = proj.shape[0]
    ng, pg, _ = mix.shape
    pw = ng * pg

    def body(u_ref, z_ref, mix_ref, sc_ref, dy_ref, du_ref, dz_ref, dmix_ref, dsc_ref):
        g = pl.program_id(0)
        row = lax.broadcasted_iota(jnp.int32, (lp, 1), 0)
        real = row >= pad
        z = z_ref[...]
        pooled = _pooled(u_ref[...], g, row, pad).astype(BF16)
        mixed = _dot(pooled, mix_ref[0])
        sig = _sigmoid(z)
        sz = z * sig
        dyv = dy_ref[...]
        dsc_ref[...] = jnp.sum(dyv * mixed * sz, axis=0, keepdims=True)
        d_sz = dyv * mixed * sc_ref[...]
        dz_ref[...] = jnp.where(real, d_sz * (sig + sz * (1.0 - sig)), 0.0).astype(BF16)
        d_mixed = (dyv * sc_ref[...] * sz).astype(BF16)
        dmix_ref[0] = _dot(pooled, d_mixed, TN)
        d_pooled = _dot(d_mixed, mix_ref[0], NT)
        du_ref[...] = jnp.where(real, _pooled_adjoint(d_pooled, g, row, pad), 0.0).astype(BF16)

    u_spec, z_spec, mix_spec, vec_spec = _pool_specs(lp, pg, ng, pw // pg)
    return _call(
        body, name="pool_bwd", grid=(ng,),
        in_specs=[u_spec, z_spec, mix_spec, vec_spec, u_spec], out_specs=[u_spec, u_spec, mix_spec, vec_spec],
        out_shape=[jax.ShapeDtypeStruct((lp, pw), BF16), jax.ShapeDtypeStruct((lp, pw), BF16),
                   jax.ShapeDtypeStruct((ng, pg, pg), F32), jax.ShapeDtypeStruct((1, pw), F32)],
        compiler_params=_params(("parallel",)),
    )(proj, proj, mix, scale, dy)


def _conv_pre(x, w, row):
    kw = w.shape[0]
    y = w[kw - 1:kw, :] * x
    for kk in range(kw - 1):
        y = y + w[kk:kk + 1, :] * _shift_down(x, kw - 1 - kk, row)
    return y


def _conv_post(y, out_scale):
    s = _silu(y)
    if out_scale is None:
        return s
    return s * lax.rsqrt(jnp.sum(s * s, axis=-1, keepdims=True) + NORM_EPS) * out_scale


def _conv_fwd(proj, col_off, w, hd, out_scale, name):
    lp = proj.shape[0]
    kw, width = w.shape
    blk0 = col_off // hd

    def body(x_ref, w_ref, o_ref):
        row = lax.broadcasted_iota(jnp.int32, (lp, 1), 0)
        o_ref[...] = _conv_post(_conv_pre(x_ref[...], w_ref[...], row), out_scale)

    return _call(
        body, name=name, grid=(width // hd,),
        in_specs=[pl.BlockSpec((lp, hd), lambda j: (0, blk0 + j)), pl.BlockSpec((kw, hd), lambda j: (0, j))],
        out_specs=pl.BlockSpec((lp, hd), lambda j: (0, j)),
        out_shape=jax.ShapeDtypeStruct((lp, width), F32), compiler_params=_params(("parallel",)),
    )(proj, w)


def _conv_bwd(proj, col_off, w, d_out, hd, out_scale, pad, name):
    lp = proj.shape[0]
    kw, width = w.shape
    blk0 = col_off // hd

    def body(x_ref, w_ref, do_ref, dx_ref, dw_ref):
        row = lax.broadcasted_iota(jnp.int32, (lp, 1), 0)
        real = row >= pad
        x, wv = x_ref[...], w_ref[...]
        _, vjp = jax.vjp(functools.partial(_conv_post, out_scale=out_scale), _conv_pre(x, wv, row))
        dy = jnp.where(real, vjp(do_ref[...])[0], 0.0)
        dx = wv[kw - 1:kw, :] * dy
        dw_ref[kw - 1:kw, :] = jnp.sum(dy * x, axis=0, keepdims=True)
        for kk in range(kw - 1):
            j = kw - 1 - kk
            dx = dx + wv[kk:kk + 1, :] * _shift_up(dy, j, row)
            dw_ref[kk:kk + 1, :] = jnp.sum(dy * _shift_down(x, j, row), axis=0, keepdims=True)
        dx_ref[...] = jnp.where(real, dx, 0.0).astype(BF16)

    col = pl.BlockSpec((lp, hd), lambda j: (0, j))
    wspec = pl.BlockSpec((kw, hd), lambda j: (0, j))
    return _call(
        body, name=name, grid=(width // hd,),
        in_specs=[pl.BlockSpec((lp, hd), lambda j: (0, blk0 + j)), wspec, col], out_specs=[col, wspec],
        out_shape=[jax.ShapeDtypeStruct((lp, width), BF16), jax.ShapeDtypeStruct((kw, width), F32)],
        compiler_params=_params(("parallel",)),
    )(proj, w, d_out)


HEADS_PER_STEP = 16


def _each(fn, *lists):
    return [fn(*args) for args in zip(*lists)]


def _dot3_each(a_list, b_list):
    hi = lambda t: t.astype(BF16)
    lo = lambda t, t_hi: (t - t_hi.astype(F32)).astype(BF16)
    a_hi, b_hi = _each(hi, a_list), _each(hi, b_list)
    a_lo, b_lo = _each(lo, a_list, a_hi), _each(lo, b_list, b_hi)
    hh, hl, lh = _each(_dot, a_hi, b_hi), _each(_dot, a_hi, b_lo), _each(_dot, a_lo, b_hi)
    return _each(lambda x, y, w: x + (y + w), hh, hl, lh)


@jax.custom_vjp
def _unit_lower_inverse(lmats):
    c = lmats[0].shape[0]
    eye = lax.broadcasted_iota(jnp.int32, (c, c), 0) == lax.broadcasted_iota(jnp.int32, (c, c), 1)
    a = [-m for m in lmats]
    tmat = [jnp.where(eye, 1.0, 0.0).astype(F32) + m for m in a]
    span = 2
    while span < c:
        a = _dot3_each(a, a)
        tmat = _each(lambda t, u: t + u, tmat, _dot3_each(tmat, a))
        span *= 2
    return tuple(tmat)


def _unit_lower_inverse_fwd(lmats):
    tmats = _unit_lower_inverse(lmats)
    return tmats, tmats


def _unit_lower_inverse_bwd(tmats, cts):
    left = _each(lambda t, ct: _dot(t, ct, TN, HIGHEST), tmats, cts)
    return (tuple(_each(lambda m, t: -_dot(m, t, NT, HIGHEST), left, tmats)),)


_unit_lower_inverse.defvjp(_unit_lower_inverse_fwd, _unit_lower_inverse_bwd)


def _chunk_math(states, q, k, v, ba, z, prm, nw, head0, rowmask, n_heads):
    c = q.shape[0]
    heads = list(range(len(states)))
    hd = q.shape[1] // len(states)
    lane = lax.broadcasted_iota(jnp.int32, ba.shape, 1)
    sub = lax.broadcasted_iota(jnp.int32, (ba.shape[1], c), 0)
    ri = lax.broadcasted_iota(jnp.int32, (c, c), 0)
    ci = lax.broadcasted_iota(jnp.int32, (c, c), 1)
    last = lax.broadcasted_iota(jnp.int32, (c, 1), 0) == c - 1
    causal, strict = ri >= ci, ri > ci
    beta_all = _sigmoid(ba) * rowmask
    g_all = -jnp.exp(prm[0:1, :]) * _softplus(ba + prm[1:2, :]) * rowmask
    gcum_all = _dot(jnp.where(causal, 1.0, 0.0).astype(F32), g_all, precision=HIGHEST)
    gcum_t = gcum_all.T
    split = lambda t: [t[:, j * hd:(j + 1) * hd] for j in heads]
    qs, ks, vs, zs = split(q), split(k), split(v), split(z)
    beta = [jnp.sum(jnp.where(lane == head0 + j, beta_all, 0.0), axis=1, keepdims=True) for j in heads]
    gcum = [jnp.sum(jnp.where(lane == n_heads + head0 + j, gcum_all, 0.0), axis=1, keepdims=True) for j in heads]
    grow = [jnp.sum(jnp.where(sub == n_heads + head0 + j, gcum_t, 0.0), axis=0, keepdims=True) for j in heads]
    glast = _each(lambda gc: jnp.sum(jnp.where(last, gc, 0.0), axis=0, keepdims=True), gcum)
    decay = _each(lambda gc, gr: jnp.where(causal, jnp.exp(jnp.where(causal, gc - gr, 0.0)), 0.0), gcum, grow)
    eg = _each(jnp.exp, gcum)
    k_beta = _each(jnp.multiply, ks, beta)
    kk = _each(lambda a, b: _dot(a, b, NT), k_beta, ks)
    tmat = list(_unit_lower_inverse(tuple(_each(lambda m, dc: jnp.where(strict, m * dc, 0.0), kk, decay))))
    u_c = _each(_dot, tmat, _each(jnp.multiply, vs, beta))
    w_c = _each(_dot, tmat, _each(jnp.multiply, k_beta, eg))
    qk = _each(lambda a, b, dc: jnp.where(causal, _dot(a, b, NT) * dc, 0.0), qs, ks, decay)
    v_new = _each(lambda u, w, s: u - _dot(w, s), u_c, w_c, list(states))
    o = _each(lambda a, e, s, m, vn: _dot(a * e, s) + _dot(m, vn), qs, eg, list(states), qk, v_new)
    k_dec = _each(lambda a, gl, gc: a * jnp.exp(gl - gc), ks, glast, gcum)
    new_states = _each(lambda s, gl, kd, vn: s * jnp.exp(gl) + _dot(kd, vn, TN), list(states), glast, k_dec, v_new)
    ys = _each(lambda oj, zj: _rmsnorm(oj, nw) * _silu(zj), o, zs)
    return jnp.concatenate(ys, axis=1), tuple(new_states)


def _chunk_specs(nc, hd, n_heads, z_off, ba_off, rev):
    cidx = (lambda c: nc - 1 - c) if rev else (lambda c: c)
    hb = min(HEADS_PER_STEP, n_heads)
    assert n_heads % hb == 0 and z_off % (hb * hd) == 0 and ba_off % LANES == 0
    blk = lambda off: pl.BlockSpec((CHUNK, hb * hd), lambda c, g: (cidx(c), off + g))
    ba_spec = lambda off: pl.BlockSpec((CHUNK, LANES), lambda c, g: (cidx(c), off // LANES))
    prm_spec = pl.BlockSpec((8, LANES), lambda c, g: (0, 0))
    nw_spec = pl.BlockSpec((1, hd), lambda c, g: (0, 0))
    st_spec = pl.BlockSpec((1, hb, hd, hd), lambda c, g: (cidx(c), g, 0, 0))
    return blk, ba_spec, prm_spec, nw_spec, st_spec, blk(z_off // (hb * hd))


def _rowmask(chunk_idx, pad):
    row = chunk_idx * CHUNK + lax.broadcasted_iota(jnp.int32, (CHUNK, 1), 0)
    return jnp.where(row >= pad, 1.0, 0.0).astype(F32)


def _chunk_fwd(qn, kn, vv, proj, z_off, ba_off, prm, nw, n_heads, pad):
    lp, dn = qn.shape
    hd = dn // n_heads
    nc = lp // CHUNK
    hb = min(HEADS_PER_STEP, n_heads)

    def body(q_ref, k_ref, v_ref, ba_ref, z_ref, prm_ref, nw_ref, y_ref, hist_ref, st_ref):
        c, g = pl.program_id(0), pl.program_id(1)

        @pl.when(c == 0)
        def _():
            for j in range(hb):
                st_ref[g * hb + j] = jnp.zeros((hd, hd), F32)

        states = tuple(st_ref[g * hb + j] for j in range(hb))
        for j in range(hb):
            hist_ref[0, j] = states[j]
        y, new_states = _chunk_math(states, q_ref[...], k_ref[...], v_ref[...], ba_ref[...], z_ref[...], prm_ref[...],
                                    nw_ref[...], g * hb, _rowmask(c, pad), n_heads)
        y_ref[...] = y.astype(BF16)
        for j in range(hb):
            st_ref[g * hb + j] = new_states[j]

    blk, ba_spec, prm_spec, nw_spec, st_spec, z_spec = _chunk_specs(nc, hd, n_heads, z_off, ba_off, False)
    return _call(
        body, name="chunk_fwd", grid=(nc, n_heads // hb),
        in_specs=[blk(0), blk(0), blk(0), ba_spec(ba_off), z_spec, prm_spec, nw_spec], out_specs=[blk(0), st_spec],
        out_shape=[jax.ShapeDtypeStruct((lp, dn), BF16), jax.ShapeDtypeStruct((nc, n_heads, hd, hd), F32)],
        scratch_shapes=[pltpu.VMEM((n_heads, hd, hd), F32)],
        compiler_params=_params(("arbitrary", "arbitrary")),
    )(qn, kn, vv, proj, proj, prm, nw)


def _chunk_bwd(qn, kn, vv, proj, z_off, ba_off, prm, nw, hist, dy, n_heads, pad):
    lp, dn = qn.shape
    hd = dn // n_heads
    nc = lp // CHUNK
    hb = min(HEADS_PER_STEP, n_heads)

    def body(q_ref, k_ref, v_ref, ba_ref, z_ref, prm_ref, nw_ref, hist_ref, dy_ref,
             dq_ref, dk_ref, dv_ref, dba_ref, dz_ref, dprm_ref, dnw_ref, dst_ref):
        step, g = pl.program_id(0), pl.program_id(1)

        @pl.when(step == 0)
        def _():
            for j in range(hb):
                dst_ref[g * hb + j] = jnp.zeros((hd, hd), F32)

        @pl.when((step == 0) & (g == 0))
        def _():
            dprm_ref[...] = jnp.zeros_like(dprm_ref)
            dnw_ref[...] = jnp.zeros_like(dnw_ref)

        @pl.when(g == 0)
        def _():
            dba_ref[...] = jnp.zeros_like(dba_ref)

        fn = functools.partial(_chunk_math, head0=g * hb, rowmask=_rowmask(nc - 1 - step, pad), n_heads=n_heads)
        states = tuple(hist_ref[0, j] for j in range(hb))
        _, vjp = jax.vjp(fn, states, q_ref[...], k_ref[...], v_ref[...], ba_ref[...], z_ref[...], prm_ref[...], nw_ref[...])
        dst, dq, dk, dv, dba, dz, dprm, dnw = vjp((dy_ref[...], tuple(dst_ref[g * hb + j] for j in range(hb))))
        for j in range(hb):
            dst_ref[g * hb + j] = dst[j]
        dq_ref[...] = dq
        dk_ref[...] = dk
        dv_ref[...] = dv
        dz_ref[...] = dz.astype(BF16)
        dba_ref[...] += dba
        dprm_ref[...] += dprm
        dnw_ref[...] += dnw

    blk, ba_spec, prm_spec, nw_spec, st_spec, z_spec = _chunk_specs(nc, hd, n_heads, z_off, ba_off, True)
    f32_full = jax.ShapeDtypeStruct((lp, dn), F32)
    return _call(
        body, name="chunk_bwd", grid=(nc, n_heads // hb),
        in_specs=[blk(0), blk(0), blk(0), ba_spec(ba_off), z_spec, prm_spec, nw_spec, st_spec, blk(0)],
        out_specs=[blk(0), blk(0), blk(0), ba_spec(0), blk(0), prm_spec, nw_spec],
        out_shape=[f32_full, f32_full, f32_full, jax.ShapeDtypeStruct((lp, LANES), F32), jax.ShapeDtypeStruct((lp, dn), BF16),
                   jax.ShapeDtypeStruct((8, LANES), F32), jax.ShapeDtypeStruct((1, hd), F32)],
        scratch_shapes=[pltpu.VMEM((n_heads, hd, hd), F32)],
        compiler_params=_params(("arbitrary", "arbitrary")),
    )(qn, kn, vv, proj, proj, prm, nw, hist, dy)


def _merge_math(p, q, gp, gd):
    return _sigmoid(gp) * p + _sigmoid(gd) * q


def _merge_specs(lp, d, gp_off, gd_off):
    tr, tc = _tile(lp, 264, 16), _tile(d, 1024, LANES)
    blk = pl.BlockSpec((tr, tc), lambda i, j: (i, j))
    gp_spec = pl.BlockSpec((tr, tc), lambda i, j: (i, gp_off // tc + j))
    gd_spec = pl.BlockSpec((tr, tc), lambda i, j: (i, gd_off // tc + j))
    return (lp // tr, d // tc), blk, gp_spec, gd_spec


def _merge_fwd(p, q, proj, gp_off, gd_off):
    lp, d = p.shape
    grid, blk, gp_spec, gd_spec = _merge_specs(lp, d, gp_off, gd_off)

    def body(p_ref, q_ref, gp_ref, gd_ref, o_ref):
        o_ref[...] = _merge_math(p_ref[...], q_ref[...], gp_ref[...], gd_ref[...]).astype(BF16)

    return _call(
        body, name="merge_fwd", grid=grid, in_specs=[blk, blk, gp_spec, gd_spec], out_specs=blk,
        out_shape=jax.ShapeDtypeStruct((lp, d), BF16), compiler_params=_params(("parallel", "parallel")),
    )(p, q, proj, proj)


def _merge_bwd(p, q, proj, gp_off, gd_off, dm):
    lp, d = p.shape
    grid, blk, gp_spec, gd_spec = _merge_specs(lp, d, gp_off, gd_off)

    def body(p_ref, q_ref, gp_ref, gd_ref, dm_ref, dp_ref, dq_ref, dgp_ref, dgd_ref):
        _, vjp = jax.vjp(_merge_math, p_ref[...], q_ref[...], gp_ref[...], gd_ref[...])
        for ref, val in zip((dp_ref, dq_ref, dgp_ref, dgd_ref), vjp(dm_ref[...])):
            ref[...] = val.astype(BF16)

    out = jax.ShapeDtypeStruct((lp, d), BF16)
    return _call(
        body, name="merge_bwd", grid=grid, in_specs=[blk, blk, gp_spec, gd_spec, blk], out_specs=[blk] * 4,
        out_shape=[out] * 4, compiler_params=_params(("parallel", "parallel")),
    )(p, q, proj, proj, dm)


def _adamw(w, g, m, v, name):
    shape = w.shape
    w2, g2, m2, v2 = (t.reshape((-1, shape[-1])) for t in (w, g, m, v))
    rows, cols = w2.shape
    tr = _tile(rows, 128, 8)

    def body(w_ref, g_ref, m_ref, v_ref, d_ref, nm_ref, nv_ref):
        gv = g_ref[...]
        nm = ADAM_B1 * m_ref[...] + (1.0 - ADAM_B1) * gv
        nv = ADAM_B2 * v_ref[...] + (1.0 - ADAM_B2) * (gv * gv)
        m_hat = nm / (1.0 - ADAM_B1 ** ADAM_STEP)
        v_hat = nv / (1.0 - ADAM_B2 ** ADAM_STEP)
        d_ref[...] = -ADAM_LR * (m_hat / (jnp.sqrt(v_hat) + ADAM_EPS) + ADAM_WD * w_ref[...])
        nm_ref[...] = nm
        nv_ref[...] = nv

    blk = pl.BlockSpec((tr, cols), lambda i: (i, 0))
    out = jax.ShapeDtypeStruct((rows, cols), F32)
    res = _call(
        body, name=name, grid=(rows // tr,), in_specs=[blk] * 4, out_specs=[blk] * 3, out_shape=[out] * 3,
        compiler_params=_params(("parallel",)),
    )(w2, g2, m2, v2)
    return tuple(t.reshape(shape) for t in res)


def _coords():
    return lax.axis_index("x"), lax.axis_index("y"), lax.axis_index("c")


def _flip(v, bit):
    return 1 - v if bit else v


CHIP_FLIPS = ((1, 0), (0, 1), (1, 1))
ANY = pl.BlockSpec(memory_space=pl.ANY)


def _all_gather(shards):
    n = len(shards)

    def body(*refs):
        x_refs, out_refs = refs[:n], refs[n:2 * n]
        send_sems, recv_sems, local_sems = refs[2 * n:]
        x, y, c = _coords()
        sibling = (x, y, 1 - c)
        chips = [(_flip(x, fx), _flip(y, fy)) for fx, fy in CHIP_FLIPS]

        def copy(a, k, block, to, from_input=False):
            px, py, pc = block
            slot = out_refs[a].at[4 * px + 2 * py + pc]
            return pltpu.make_async_remote_copy(
                src_ref=x_refs[a] if from_input else slot, dst_ref=slot,
                send_sem=send_sems.at[7 * a + k], recv_sem=recv_sems.at[7 * a + k], device_id=to, device_id_type=MESH)

        mine = [pltpu.make_async_copy(x_refs[a], out_refs[a].at[4 * x + 2 * y + c], local_sems.at[a]) for a in range(n)]
        first = []
        for a in range(n):
            mine[a].start()
            first.append(copy(a, 0, (x, y, c), sibling, True))
            first += [copy(a, 1 + j, (x, y, c), (*chip, c), True) for j, chip in enumerate(chips)]
        for cp in first:
            cp.start()
        passed = []
        for j, chip in enumerate(chips):
            for a in range(n):
                copy(a, 1 + j, (*chip, c), (x, y, c)).wait_recv()
                passed.append(copy(a, 4 + j, (*chip, c), sibling))
                passed[-1].start()
        for a in range(n):
            copy(a, 0, (x, y, 1 - c), (x, y, c)).wait_recv()
            for j, chip in enumerate(chips):
                copy(a, 4 + j, (*chip, 1 - c), (x, y, c)).wait_recv()
        for cp in first + passed:
            cp.wait_send()
        for cp in mine:
            cp.wait()

    return _call(
        body, name="all_gather", in_specs=[ANY] * n, out_specs=[ANY] * n,
        out_shape=[jax.ShapeDtypeStruct((N_DEV,) + s.shape, s.dtype) for s in shards],
        scratch_shapes=[pltpu.SemaphoreType.DMA((7 * n,)), pltpu.SemaphoreType.DMA((7 * n,)), pltpu.SemaphoreType.DMA((n,))],
    )(*shards)


def _rs_to_sibling(gs, name):
    n = len(gs)

    def body(*refs):
        g_refs, got_refs = refs[:n], refs[n:2 * n]
        send_sems, recv_sems = refs[2 * n:]
        x, y, c = _coords()
        copies = []
        for a in range(n):
            for p in range(4):
                cp = pltpu.make_async_remote_copy(
                    src_ref=g_refs[a].at[2 * p + (1 - c)], dst_ref=got_refs[a].at[p], send_sem=send_sems.at[4 * a + p],
                    recv_sem=recv_sems.at[4 * a + p], device_id=(x, y, 1 - c), device_id_type=MESH)
                cp.start()
                copies.append(cp)
        for cp in copies:
            cp.wait()

    return _call(
        body, name=name, in_specs=[ANY] * n, out_specs=[ANY] * n,
        out_shape=[jax.ShapeDtypeStruct((4,) + g.shape[1:], g.dtype) for g in gs],
        scratch_shapes=[pltpu.SemaphoreType.DMA((4 * n,)), pltpu.SemaphoreType.DMA((4 * n,))],
    )(*gs)


def _rs_pair_sum(g, got, c_idx, name):
    _, rows, cols = g.shape
    tr = _tile(rows, 256, 16)

    def body(c_ref, g_ref, got_ref, o_ref):
        o_ref[...] = (g_ref[...].astype(F32) + got_ref[...].astype(F32)).astype(o_ref.dtype)

    grid_spec = pltpu.PrefetchScalarGridSpec(
        num_scalar_prefetch=1, grid=(4, rows // tr),
        in_specs=[pl.BlockSpec((1, tr, cols), lambda p, i, c_ref: (2 * p + c_ref[0], i, 0)),
                  pl.BlockSpec((1, tr, cols), lambda p, i, c_ref: (p, i, 0))],
        out_specs=pl.BlockSpec((1, tr, cols), lambda p, i, c_ref: (p, i, 0)))
    return _call(
        body, name=name, grid_spec=grid_spec, out_shape=jax.ShapeDtypeStruct((4, rows, cols), g.dtype),
        compiler_params=_params(("parallel", "parallel")),
    )(c_idx, g, got)


def _to_chips_copies(p_refs, got_refs, send_sems, recv_sems):
    x, y, c = _coords()
    copies = []
    for a in range(len(p_refs)):
        for k, (fx, fy) in enumerate(CHIP_FLIPS):
            px, py = _flip(x, fx), _flip(y, fy)
            copies.append(pltpu.make_async_remote_copy(
                src_ref=p_refs[a].at[2 * px + py], dst_ref=got_refs[a].at[k], send_sem=send_sems.at[3 * a + k],
                recv_sem=recv_sems.at[3 * a + k], device_id=(px, py, c), device_id_type=MESH))
    return copies


def _rs_to_chips(partials, name):
    n = len(partials)

    def body(*refs):
        copies = _to_chips_copies(refs[:n], refs[n:2 * n], *refs[2 * n:])
        for cp in copies:
            cp.start()
        for cp in copies:
            cp.wait()

    return _call(
        body, name=name, in_specs=[ANY] * n, out_specs=[ANY] * n,
        out_shape=[jax.ShapeDtypeStruct((3,) + p.shape[1:], p.dtype) for p in partials],
        scratch_shapes=[pltpu.SemaphoreType.DMA((3 * n,)), pltpu.SemaphoreType.DMA((3 * n,))],
    )(*partials)


HBM = pl.BlockSpec(memory_space=pltpu.HBM)
SEM = pl.BlockSpec(memory_space=pltpu.SEMAPHORE)
SIDE_EFFECT = pltpu.CompilerParams(has_side_effects=pltpu.SideEffectType.DATAFLOW_SIDE_EFFECTING)


def _split_start(copies_fn, srcs, land_shapes, n_sems, name):
    n, m = len(srcs), len(land_shapes)

    def body(*refs):
        send_sems, recv_sems, token = refs[n + m], refs[n + m + 1], refs[-1]
        for cp in copies_fn(refs[:n], refs[n:n + m], send_sems, recv_sems):
            cp.start()
        token[...] = jnp.zeros_like(token)

    ins = [pltpu.with_memory_space_constraint(t, pltpu.HBM) for t in list(srcs) + [lax.empty(s.shape, s.dtype) for s in land_shapes]]
    res = _call(
        body, name=name, in_specs=[HBM] * (n + m),
        out_specs=[SEM, SEM] + [HBM] * (n + m) + [pl.BlockSpec(memory_space=pltpu.VMEM)],
        out_shape=[pltpu.SemaphoreType.DMA((n_sems,)), pltpu.SemaphoreType.DMA((n_sems,))]
        + [pltpu.HBM(t.shape, t.dtype) for t in ins] + [jax.ShapeDtypeStruct((8, LANES), F32)],
        input_output_aliases={i: 2 + i for i in range(n + m)}, compiler_params=SIDE_EFFECT,
    )(*ins)
    return dict(sems=(res[0], res[1]), srcs=res[2:2 + n], lands=res[2 + n:2 + n + m], token=res[-1])


def _split_wait(copies_fn, started, after, name):
    n, m = len(started["srcs"]), len(started["lands"])

    def body(*refs):
        for cp in copies_fn(refs[:n], refs[n:n + m], refs[n + m], refs[n + m + 1]):
            cp.wait_send()
            cp.wait_recv()

    bufs = list(started["srcs"]) + list(started["lands"])
    res = _call(
        body, name=name, in_specs=[HBM] * (n + m) + [SEM, SEM, ANY], out_specs=[HBM] * (n + m),
        out_shape=[pltpu.HBM(t.shape, t.dtype) for t in bufs],
        input_output_aliases={i: i for i in range(n + m)}, compiler_params=SIDE_EFFECT,
    )(*bufs, *started["sems"], after)
    return res[:n], res[n:]


def _to_all_copies(x_refs, out_refs, send_sems, recv_sems):
    x, y, c = _coords()
    copies = []
    for a in range(len(x_refs)):
        for k in range(N_DEV - 1):
            fx, fy, fc = ((k + 1) >> 2) & 1, ((k + 1) >> 1) & 1, (k + 1) & 1
            copies.append(pltpu.make_async_remote_copy(
                src_ref=x_refs[a], dst_ref=out_refs[a].at[4 * x + 2 * y + c], send_sem=send_sems.at[7 * a + k],
                recv_sem=recv_sems.at[7 * a + k], device_id=(_flip(x, fx), _flip(y, fy), _flip(c, fc)), device_id_type=MESH))
    return copies


def _fill_own_block(gathered, shard, me_idx, name):
    rows, cols = shard.shape
    tr = _tile(rows, 512, 16)

    def body(me_ref, g_ref, s_ref, o_ref):
        o_ref[0] = s_ref[...]

    grid_spec = pltpu.PrefetchScalarGridSpec(
        num_scalar_prefetch=1, grid=(rows // tr,),
        in_specs=[ANY, pl.BlockSpec((tr, cols), lambda i, me: (i, 0))],
        out_specs=pl.BlockSpec((1, tr, cols), lambda i, me: (me[0], i, 0)))
    return _call(
        body, name=name, grid_spec=grid_spec, out_shape=jax.ShapeDtypeStruct(gathered.shape, gathered.dtype),
        input_output_aliases={1: 0}, compiler_params=_params(("arbitrary",)),
    )(me_idx, gathered, shard)


def _rs_chip_sum(partial, got, chip_idx, name):
    _, rows, cols = partial.shape
    tr = _tile(rows, 256, 16)

    def body(p_idx_ref, p_ref, got_ref, o_ref):
        o_ref[...] = ((p_ref[0].astype(F32) + got_ref[0].astype(F32)) + got_ref[1].astype(F32)) + got_ref[2].astype(F32)

    grid_spec = pltpu.PrefetchScalarGridSpec(
        num_scalar_prefetch=1, grid=(rows // tr,),
        in_specs=[pl.BlockSpec((1, tr, cols), lambda i, p_ref: (p_ref[0], i, 0)),
                  pl.BlockSpec((3, tr, cols), lambda i, p_ref: (0, i, 0))],
        out_specs=pl.BlockSpec((tr, cols), lambda i, p_ref: (i, 0)))
    return _call(
        body, name=name, grid_spec=grid_spec, out_shape=jax.ShapeDtypeStruct((rows, cols), F32),
        compiler_params=_params(("parallel",)),
    )(chip_idx, partial, got)


def _rs_begin(gs, tag, split):
    c_idx = jnp.reshape(lax.axis_index("c"), (1,)).astype(jnp.int32)
    gots = _rs_to_sibling(gs, "rs_to_sibling_" + tag)
    partials = [_rs_pair_sum(g, got, c_idx, "rs_pair_sum_%s%d" % (tag, a)) for a, (g, got) in enumerate(zip(gs, gots))]
    if not split:
        return dict(partials=partials, gots=_rs_to_chips(partials, "rs_to_chips_" + tag))
    lands = [jax.ShapeDtypeStruct((3,) + p.shape[1:], p.dtype) for p in partials]
    return _split_start(_to_chips_copies, partials, lands, 3 * len(partials), "rs_to_chips_start_" + tag)


def _rs_finish(begun, tag, after=None):
    x, y, _ = _coords()
    chip_idx = jnp.reshape(2 * x + y, (1,)).astype(jnp.int32)
    if "gots" in begun:
        partials, gots = begun["partials"], begun["gots"]
    else:
        partials, gots = _split_wait(_to_chips_copies, begun, after, "rs_to_chips_wait_" + tag)
    return [_rs_chip_sum(p, got, chip_idx, "rs_chip_sum_%s%d" % (tag, a)) for a, (p, got) in enumerate(zip(partials, gots))]


RUNS = 3
RUN_FIELDS = 6


def _lane_gather_table(src_of, src_width):
    n_blocks = src_of.shape[0] // LANES
    tab = np.zeros((n_blocks + 1, RUNS, RUN_FIELDS), np.int32)
    tab[:, :, 5] = LANES
    for t in range(n_blocks):
        runs = []
        for lane in range(LANES):
            slab, col = (int(v) for v in src_of[t * LANES + lane])
            if slab < 0:
                continue
            key = (slab, col // LANES, col % LANES - lane)
            if runs and runs[-1][0] == key and runs[-1][2] == lane:
                runs[-1][2] = lane + 1
            else:
                runs.append([key, lane, lane + 1])
        assert len(runs) <= RUNS
        for e, (key, lo, hi) in enumerate(runs):
            tab[t, e] = (key[0], key[1], key[2], lo, hi, min(LANES, src_width - key[1] * LANES))
    return tab.reshape(-1)


def _lane_gather(src, table, out_slabs, out_width, name):
    _, rows, _ = src.shape
    blocks_per_slab = -(-out_width // LANES)

    def body(tab_ref, *refs):
        o_ref = refs[RUNS]
        t = pl.program_id(0)
        lane = lax.broadcasted_iota(jnp.int32, (1, LANES), 1)
        li = lax.broadcasted_iota(jnp.int32, (LANES, LANES), 0)
        ci = lax.broadcasted_iota(jnp.int32, (LANES, LANES), 1)
        acc = None
        for e in range(RUNS):
            base = (t * RUNS + e) * RUN_FIELDS
            shift, lo, hi, valid = tab_ref[base + 2], tab_ref[base + 3], tab_ref[base + 4], tab_ref[base + 5]
            a = jnp.where(lane < valid, refs[e][0], 0.0).astype(BF16)
            sel = jnp.where((li == ci + shift) & (ci >= lo) & (ci < hi), 1.0, 0.0).astype(BF16)
            part = _dot(a, sel)
            acc = part if acc is None else acc + part
        o_ref[0] = acc.astype(BF16)

    def src_spec(e):
        return pl.BlockSpec((1, rows, LANES), lambda t, tab: (tab[(t * RUNS + e) * RUN_FIELDS], 0, tab[(t * RUNS + e) * RUN_FIELDS + 1]))

    grid_spec = pltpu.PrefetchScalarGridSpec(
        num_scalar_prefetch=1, grid=(out_slabs * blocks_per_slab,), in_specs=[src_spec(e) for e in range(RUNS)],
        out_specs=pl.BlockSpec((1, rows, LANES), lambda t, tab: (t // blocks_per_slab, 0, t % blocks_per_slab)))
    return _call(
        body, name=name, grid_spec=grid_spec, out_shape=jax.ShapeDtypeStruct((out_slabs, rows, out_width), BF16),
        compiler_params=_params(("parallel",)),
    )(jnp.asarray(table), src, src, src)


def _all_reduce_small(vec):
    rows, cols = vec.shape

    def body(v_ref, o_ref, buf, send_sems, recv_sems):
        x, y, c = _coords()
        me = 4 * x + 2 * y + c
        buf[me] = v_ref[...]
        copies = []
        for k in range(N_DEV - 1):
            fx, fy, fc = ((k + 1) >> 2) & 1, ((k + 1) >> 1) & 1, (k + 1) & 1
            cp = pltpu.make_async_remote_copy(
                src_ref=v_ref, dst_ref=buf.at[me], send_sem=send_sems.at[k], recv_sem=recv_sems.at[k],
                device_id=(_flip(x, fx), _flip(y, fy), _flip(c, fc)), device_id_type=MESH)
            cp.start()
            copies.append(cp)
        for cp in copies:
            cp.wait()
        total = buf[0]
        for j in range(1, N_DEV):
            total = total + buf[j]
        o_ref[...] = total

    vmem = pl.BlockSpec(memory_space=pltpu.VMEM)
    return _call(
        body, name="all_reduce_small", in_specs=[vmem], out_specs=vmem,
        out_shape=jax.ShapeDtypeStruct((rows, cols), F32),
        scratch_shapes=[pltpu.VMEM((N_DEV, rows, cols), F32), pltpu.SemaphoreType.DMA((N_DEV - 1,)),
                        pltpu.SemaphoreType.DMA((N_DEV - 1,))],
    )(vec)


def _w_in_column_maps(ns, o_ba, n_logit, n_main, n_all):
    own = np.arange(N_DEV * ns)
    work_of_own = np.where(own < o_ba, own, np.where(own < o_ba + n_logit, n_main + own - o_ba, own - n_logit))
    to_work = np.full((n_all, 2), -1, np.int64)
    to_work[work_of_own, 0] = own // ns
    to_work[work_of_own, 1] = own % ns
    slab_width = -(-ns // LANES) * LANES
    to_own = np.full((N_DEV, slab_width, 2), -1, np.int64)
    to_own[:, :ns, 0] = 0
    to_own[:, :ns, 1] = work_of_own.reshape(N_DEV, ns)
    return to_work, to_own.reshape(-1, 2)


def kernel(x, meta_tokens, norm_w, w_in, conv_w, A_log, dt_bias, pool_mix, pool_scale, dn_norm_w, w_pool_out, w_dn_out, w_o, final_norm_w, loss_target, m_meta_tokens, m_norm_w, m_w_in, m_conv_w, m_A_log, m_dt_bias, m_pool_mix, m_pool_scale, m_dn_norm_w, m_w_pool_out, m_w_dn_out, m_w_o, m_final_norm_w, v_meta_tokens, v_norm_w, v_w_in, v_conv_w, v_A_log, v_dt_bias, v_pool_mix, v_pool_scale, v_dn_norm_w, v_w_pool_out, v_w_dn_out, v_w_o, v_final_norm_w):
    seq, d = x.shape[1], x.shape[2]
    n_meta = meta_tokens.shape[0]
    n_heads, hd = A_log.shape[-1], dn_norm_w.shape[-1]
    dn = n_heads * hd
    pw, ng = pool_scale.shape[-1], pool_mix.shape[1]
    pg = pw // ng
    kw = conv_w.shape[1]
    pad = (-n_meta) % CHUNK
    x0 = pad + n_meta
    lp = x0 + seq
    ns = w_in.shape[-1]
    in_cols = N_DEV * ns
    o_q, o_k, o_v, o_zd = 2 * pw, 2 * pw + dn, 2 * pw + 2 * dn, 2 * pw + 3 * dn
    o_ba = 2 * pw + 4 * dn
    o_gp, o_gd = o_ba, o_ba + d
    n_main = o_gd + d
    n_all = n_main + 2 * LANES
    assert lp % CHUNK == 0 and in_cols == n_main + 2 * n_heads and 2 * n_heads <= LANES and hd == LANES
    cs, ms = conv_w.shape[-1], meta_tokens.shape[-1]
    mr = pool_mix.shape[2]
    assert ms == pg and cs % pg == 0
    to_work, to_own = _w_in_column_maps(ns, o_ba, 2 * n_heads, n_main, n_all)
    cols_major = lambda t: jnp.transpose(t, (1, 0, 2)).reshape(t.shape[1], N_DEV * t.shape[2])

    win_g, mix_g, conv_g, meta_g = _all_gather(
        [w_in[0].astype(BF16), pool_mix[0].reshape(ng * mr, pg).astype(BF16), conv_w[0], meta_tokens])
    late_shards = [w_pool_out[0].astype(BF16), w_dn_out[0].astype(BF16), w_o[0].astype(BF16)]
    late_weights = _split_start(_to_all_copies, late_shards, [jax.ShapeDtypeStruct((N_DEV,) + s.shape, BF16) for s in late_shards],
                                (N_DEV - 1) * len(late_shards), "gather_out_proj_start")
    norm_w_in = norm_w + late_weights["token"][0, 0]
    w_all = _lane_gather(win_g, _lane_gather_table(to_work, ns), 1, n_all, "w_in_to_work").reshape(d, n_all)
    mix_f = jnp.transpose(mix_g.reshape(N_DEV, ng, mr, pg), (1, 0, 2, 3)).reshape(ng, pg, pg)
    conv_f = cols_major(conv_g)
    meta_f = cols_major(meta_g)

    h0, xn = _norm_in_fwd(x[0], meta_f, norm_w_in, pad)
    proj = _matmul(xn, w_all, NN, F32, 1056, 768, 2048, "proj")
    y_pool = _pool_fwd(proj, mix_f, pool_scale, pad)
    conv_q, conv_k, conv_v = (conv_f[:, i * dn:(i + 1) * dn] for i in range(3))
    qn = _conv_fwd(proj, o_q, conv_q, hd, float(hd) ** -0.5, "conv_q_fwd")
    kn = _conv_fwd(proj, o_k, conv_k, hd, 1.0, "conv_k_fwd")
    vv = _conv_fwd(proj, o_v, conv_v, hd, None, "conv_v_fwd")
    logit_lanes = (n_heads, LANES - 2 * n_heads)
    prm = jnp.pad(A_log, ((0, 7), logit_lanes)) + jnp.pad(dt_bias, ((1, 6), logit_lanes))
    y_dn, hist = _chunk_fwd(qn, kn, vv, proj, o_zd, n_main, prm, dn_norm_w, n_heads, pad)
    me_idx = jnp.reshape(4 * lax.axis_index("x") + 2 * lax.axis_index("y") + lax.axis_index("c"), (1,)).astype(jnp.int32)
    _, landed = _split_wait(_to_all_copies, late_weights, y_dn, "gather_out_proj_wait")
    wpo_g, wdn_g, wo_g = (_fill_own_block(g, s, me_idx, "own_block_%d" % i) for i, (g, s) in enumerate(zip(landed, late_shards)))
    wpo_f = cols_major(wpo_g)
    wdn_f = wdn_g.reshape(dn, d)
    wo_f = wo_g.reshape(d, d)
    p_out = _matmul(y_pool, wpo_f, NN, F32, 1056, 1024, 1024, "pool_out")
    q_out = _matmul(y_dn, wdn_f, NN, F32, 1056, 1024, 2048, "dn_out")
    merged = _merge_fwd(p_out, q_out, proj, o_gp, o_gd)
    mo = _matmul(merged, wo_f, NN, F32, 1056, 1024, 2048, "w_o_fwd")
    dh1, d_fw, loss_part = _final_loss(h0, mo, final_norm_w.reshape(1, d), loss_target[0], x0)

    d_merged = _matmul(dh1, wo_f, NT, F32, 1056, 1024, 1024, "w_o_bwd_x")
    g_wo = _matmul(merged, dh1, TN, BF16, 1024, 1024, 704, "w_o_bwd_w")
    d_p, d_q, d_gp, d_gd = _merge_bwd(p_out, q_out, proj, o_gp, o_gd, d_merged)
    d_ypool = _matmul(d_p, wpo_f, NT, F32, 1056, 1024, 2048, "pool_out_bwd_x")
    g_wpo = _matmul(y_pool, d_p, TN, BF16, 1024, 1024, 704, "pool_out_bwd_w", col_blocks=N_DEV)
    d_ydn = _matmul(d_q, wdn_f, NT, F32, 1056, 1024, 2048, "dn_out_bwd_x")
    g_wdn = _matmul(y_dn, d_q, TN, BF16, 1024, 1024, 704, "dn_out_bwd_w")
    rs_early = _rs_begin([g_wpo, g_wdn.reshape(N_DEV, dn // N_DEV, d), g_wo.reshape(N_DEV, d // N_DEV, d)], "early", split=True)
    started = rs_early["token"][0, 0]
    d_u, d_zp, g_mix, g_pscale = _pool_bwd(proj, mix_f, pool_scale + started, d_ypool, pad)
    d_qn, d_kn, d_vv, d_ba, d_zd, d_prm, g_dnw = _chunk_bwd(qn, kn, vv, proj, o_zd, n_main, prm + started, dn_norm_w, hist, d_ydn, n_heads, pad)
    d_qr, g_cq = _conv_bwd(proj, o_q, conv_q, d_qn, hd, float(hd) ** -0.5, pad, "conv_q_bwd")
    d_kr, g_ck = _conv_bwd(proj, o_k, conv_k, d_kn, hd, 1.0, pad, "conv_k_bwd")
    d_vr, g_cv = _conv_bwd(proj, o_v, conv_v, d_vv, hd, None, pad, "conv_v_bwd")
    d_proj = jnp.concatenate([d_u, d_zp, d_qr, d_kr, d_vr, d_zd, d_gp, d_gd, d_ba.astype(BF16), jnp.zeros((lp, LANES), BF16)], axis=1)
    g_wall = _matmul(xn.T, d_proj, NN, F32, 1024, 768, lp, "w_in_bwd_w")
    g_win = _lane_gather(g_wall.reshape(1, d, n_all), _lane_gather_table(to_own, n_all), N_DEV, ns, "w_in_grad_to_own")
    rs_late = _rs_begin([g_win], "late", split=True)
    d_xn = _matmul(d_proj, w_all, NT, F32, 1056, 1024, 768, "w_in_bwd_x", after=rs_late["token"])
    d_h0, g_nw = _norm_in_bwd(h0, norm_w, d_xn, dh1)
    grad_x = d_h0[x0:][None]

    by_cols = lambda t: jnp.transpose(t.reshape(t.shape[0], N_DEV, t.shape[1] // N_DEV), (1, 0, 2))
    g_conv = by_cols(jnp.concatenate([g_cq, g_ck, g_cv], axis=1)).reshape(N_DEV, kw * cs // pg, pg)
    conv_rows = -(-g_conv.shape[1] // 16) * 16
    g_small = jnp.concatenate(
        [jnp.transpose(g_mix.reshape(ng, N_DEV, mr, pg), (1, 0, 2, 3)).reshape(N_DEV, ng * mr, pg), by_cols(d_h0[pad:x0]),
         jnp.pad(g_conv, ((0, 0), (0, conv_rows - g_conv.shape[1]), (0, 0)))], axis=1).astype(BF16)
    r_small, = _rs_finish(_rs_begin([g_small], "small", split=False), "small")
    r_mix, r_meta = r_small[:ng * mr], r_small[ng * mr:ng * mr + n_meta]
    r_conv = r_small[ng * mr + n_meta:ng * mr + n_meta + kw * cs // pg]
    r_wpo, r_wdn, r_wo = _rs_finish(rs_early, "early", after=r_small)

    small = [g_nw[0], d_fw[0], g_pscale[0], g_dnw[0], d_prm[0], d_prm[1], loss_part[0]]
    s_sizes = [t.shape[0] for t in small]
    s_cols = -(-sum(s_sizes) // (8 * LANES)) * LANES
    s_vec = jnp.concatenate(small + [jnp.zeros((8 * s_cols - sum(s_sizes),), F32)]).reshape(8, s_cols)
    s_red = _all_reduce_small(s_vec)
    s_sum = s_red.reshape(-1)
    r_win, = _rs_finish(rs_late, "late", after=s_red)
    s_offs = [sum(s_sizes[:i]) for i in range(len(s_sizes))]
    s_take = lambda i, n=None, o=0: s_sum[s_offs[i] + o:s_offs[i] + o + (s_sizes[i] if n is None else n)]

    grads = {
        "meta_tokens": r_meta, "norm_w": s_take(0).reshape(norm_w.shape),
        "w_in": r_win.reshape(w_in.shape), "conv_w": r_conv.reshape(conv_w.shape),
        "A_log": s_take(4, n_heads, n_heads).reshape(A_log.shape), "dt_bias": s_take(5, n_heads, n_heads).reshape(dt_bias.shape),
        "pool_mix": r_mix.reshape(pool_mix.shape), "pool_scale": s_take(2).reshape(pool_scale.shape),
        "dn_norm_w": s_take(3).reshape(dn_norm_w.shape), "w_pool_out": r_wpo.reshape(w_pool_out.shape),
        "w_dn_out": r_wdn.reshape(w_dn_out.shape), "w_o": r_wo.reshape(w_o.shape),
        "final_norm_w": s_take(1).reshape(final_norm_w.shape),
    }
    loss = s_take(6, 1)[0]

    weights = dict(meta_tokens=meta_tokens, norm_w=norm_w, w_in=w_in, conv_w=conv_w, A_log=A_log, dt_bias=dt_bias,
                   pool_mix=pool_mix, pool_scale=pool_scale, dn_norm_w=dn_norm_w, w_pool_out=w_pool_out, w_dn_out=w_dn_out,
                   w_o=w_o, final_norm_w=final_norm_w)
    m_in = dict(meta_tokens=m_meta_tokens, norm_w=m_norm_w, w_in=m_w_in, conv_w=m_conv_w, A_log=m_A_log, dt_bias=m_dt_bias,
                pool_mix=m_pool_mix, pool_scale=m_pool_scale, dn_norm_w=m_dn_norm_w, w_pool_out=m_w_pool_out,
                w_dn_out=m_w_dn_out, w_o=m_w_o, final_norm_w=m_final_norm_w)
    v_in = dict(meta_tokens=v_meta_tokens, norm_w=v_norm_w, w_in=v_w_in, conv_w=v_conv_w, A_log=v_A_log, dt_bias=v_dt_bias,
                pool_mix=v_pool_mix, pool_scale=v_pool_scale, dn_norm_w=v_dn_norm_w, w_pool_out=v_w_pool_out,
                w_dn_out=v_w_dn_out, w_o=v_w_o, final_norm_w=v_final_norm_w)
    names = list(weights)
    upd = {n: _adamw(weights[n], grads[n], m_in[n], v_in[n], "adamw_" + n) for n in names}
    return (loss, grad_x, *[grads[n] for n in names], *[upd[n][0] for n in names], *[upd[n][1] for n in names],
            *[upd[n][2] for n in names])
```

```python
import functools

import jax
import jax.numpy as jnp
import numpy as np
from jax import lax
from jax.experimental import pallas as pl
from jax.experimental.pallas import tpu as pltpu

F32 = jnp.float32
BF16 = jnp.bfloat16
HIGHEST = lax.Precision.HIGHEST
MESH = pl.DeviceIdType.MESH

CHUNK = 64
NORM_EPS = 1e-6
POOL_WINDOWS = (2, 4, 8, 16)
ADAM_LR, ADAM_B1, ADAM_B2, ADAM_EPS, ADAM_WD, ADAM_STEP = 0.001, 0.9, 0.999, 1e-08, 0.01, 10
N_DEV = 8
LANES = 128
VMEM_LIMIT = 48 * 1024 * 1024

NN = (((1,), (0,)), ((), ()))
NT = (((1,), (1,)), ((), ()))
TN = (((0,), (0,)), ((), ()))


def _call(body, **kw):
    return pl.pallas_call(body, **kw)


def _params(sem=None):
    return pltpu.CompilerParams(dimension_semantics=sem, vmem_limit_bytes=VMEM_LIMIT)


def _tile(n, pref, align):
    for d in range(min(pref, n), 0, -1):
        if n % d == 0 and d % align == 0:
            return d
    return n


def _dot(a, b, dims=NN, precision=None):
    return lax.dot_general(a, b, dims, precision=precision, preferred_element_type=F32)


def _sigmoid(x):
    return 1.0 / (1.0 + jnp.exp(-x))


def _silu(x):
    return x * _sigmoid(x)


def _softplus(x):
    return jnp.maximum(x, 0.0) + jnp.log(1.0 + jnp.exp(-jnp.abs(x)))


def _rmsnorm(x, w):
    return x * lax.rsqrt(jnp.mean(x * x, axis=-1, keepdims=True) + NORM_EPS) * w


def _shift_down(x, j, row):
    if j == 0:
        return x
    return jnp.where(row >= j, pltpu.roll(x, j, 0), 0.0)


def _shift_up(x, j, row):
    if j == 0:
        return x
    n = x.shape[0]
    return jnp.where(row < n - j, pltpu.roll(x, n - j, 0), 0.0)


def _matmul(a, b, dims, out_dtype, tm, tn, tk, name, col_blocks=None, after=None):
    ta = dims == TN
    tb = dims == NT
    m, kdim = (a.shape[1], a.shape[0]) if ta else a.shape
    n = b.shape[0] if tb else b.shape[1]
    if col_blocks:
        tn = n // col_blocks
    tm, tn, tk = _tile(m, tm, 8), _tile(n, tn, LANES), _tile(kdim, tk, LANES if not ta else 16)
    nk = kdim // tk

    n_extra = 0 if after is None else 1

    def body(a_ref, b_ref, *refs):
        o_ref, scratch = refs[n_extra], refs[n_extra + 1:]
        part = _dot(a_ref[...].astype(BF16), b_ref[...].astype(BF16), dims)
        if nk == 1:
            o_ref[...] = part.astype(o_ref.dtype).reshape(o_ref.shape)
            return
        acc_ref, = scratch
        k = pl.program_id(2)

        @pl.when(k == 0)
        def _():
            acc_ref[...] = part

        @pl.when(k > 0)
        def _():
            acc_ref[...] += part

        @pl.when(k == nk - 1)
        def _():
            o_ref[...] = acc_ref[...].astype(o_ref.dtype).reshape(o_ref.shape)

    a_spec = pl.BlockSpec((tk, tm), lambda i, j, k: (k, i)) if ta else pl.BlockSpec((tm, tk), lambda i, j, k: (i, k))
    b_spec = pl.BlockSpec((tn, tk), lambda i, j, k: (j, k)) if tb else pl.BlockSpec((tk, tn), lambda i, j, k: (k, j))
    if col_blocks:
        out_spec = pl.BlockSpec((1, tm, tn), lambda i, j, k: (j, i, 0))
        out_shape = jax.ShapeDtypeStruct((col_blocks, m, tn), out_dtype)
    else:
        out_spec = pl.BlockSpec((tm, tn), lambda i, j, k: (i, j))
        out_shape = jax.ShapeDtypeStruct((m, n), out_dtype)
    return _call(
        body, name=name, grid=(m // tm, n // tn, nk),
        in_specs=[a_spec, b_spec] + [ANY] * n_extra, out_specs=out_spec, out_shape=out_shape,
        scratch_shapes=[] if nk == 1 else [pltpu.VMEM((tm, tn), F32)],
        compiler_params=_params(("parallel", "parallel", "arbitrary")),
    )(a, b, *([] if after is None else [after]))


def _norm_in_fwd(x2d, meta, w, pad):
    seq, d = x2d.shape
    tr = pad + meta.shape[0]
    assert seq % tr == 0 and tr % 16 == 0
    lp = tr + seq

    def body(x_ref, m_ref, w_ref, h_ref, o_ref):
        def emit(h):
            h_ref[...] = h
            o_ref[...] = _rmsnorm(h, w_ref[...]).astype(BF16)

        @pl.when(pl.program_id(0) == 0)
        def _():
            emit(jnp.concatenate([jnp.zeros((pad, d), F32), m_ref[...]], axis=0) if pad else m_ref[...])

        @pl.when(pl.program_id(0) > 0)
        def _():
            emit(x_ref[...])

    row = pl.BlockSpec((tr, d), lambda i: (i, 0))
    return _call(
        body, name="norm_in_fwd", grid=(lp // tr,),
        in_specs=[pl.BlockSpec((tr, d), lambda i: (jnp.maximum(i - 1, 0), 0)), pl.BlockSpec(meta.shape, lambda i: (0, 0)),
                  pl.BlockSpec((1, d), lambda i: (0, 0))],
        out_specs=[row, row],
        out_shape=[jax.ShapeDtypeStruct((lp, d), F32), jax.ShapeDtypeStruct((lp, d), BF16)],
        compiler_params=_params(("arbitrary",)),
    )(x2d, meta, w)


def _norm_in_bwd(h0, w, dxn, dh1):
    lp, d = h0.shape
    tr = _tile(lp, 264, 8)

    def body(h_ref, w_ref, da_ref, dh1_ref, dh_ref, dw_ref):
        _, vjp = jax.vjp(_rmsnorm, h_ref[...], w_ref[...])
        dh, dw = vjp(da_ref[...])
        dh_ref[...] = dh + dh1_ref[...]

        @pl.when(pl.program_id(0) == 0)
        def _():
            dw_ref[...] = jnp.zeros_like(dw_ref)

        dw_ref[...] += dw

    row = pl.BlockSpec((tr, d), lambda i: (i, 0))
    vec = pl.BlockSpec((1, d), lambda i: (0, 0))
    return _call(
        body, name="norm_in_bwd", grid=(lp // tr,),
        in_specs=[row, vec, row, row], out_specs=[row, vec],
        out_shape=[jax.ShapeDtypeStruct((lp, d), F32), jax.ShapeDtypeStruct((1, d), F32)],
        compiler_params=_params(("arbitrary",)),
    )(h0, w, dxn, dh1)


def _final_loss(h0, mo, fw, tgt, x0):
    lp, d = h0.shape
    tr = x0
    assert lp % tr == 0

    def body(h_ref, mo_ref, fw_ref, t_ref, dh_ref, dw_ref, loss_ref):
        i = pl.program_id(0)
        row = i * tr + lax.broadcasted_iota(jnp.int32, (tr, 1), 0)
        mask = jnp.where(row >= x0, 1.0, 0.0).astype(F32)
        tgt_v = t_ref[...]

        def loss_fn(h1, w):
            err = _rmsnorm(h1, w) - tgt_v
            return 0.5 * jnp.sum(jnp.mean(err * err, axis=-1, keepdims=True) * mask, axis=0, keepdims=True)

        loss, vjp = jax.vjp(loss_fn, h_ref[...] + mo_ref[...], fw_ref[...])
        dh, dw = vjp(jnp.ones((1, 1), F32))
        dh_ref[...] = dh

        @pl.when(i == 0)
        def _():
            dw_ref[...] = jnp.zeros_like(dw_ref)
            loss_ref[...] = jnp.zeros_like(loss_ref)

        dw_ref[...] += dw
        loss_ref[...] += jnp.broadcast_to(loss, loss_ref.shape)

    row_spec = pl.BlockSpec((tr, d), lambda i: (i, 0))
    vec = pl.BlockSpec((1, d), lambda i: (0, 0))
    return _call(
        body, name="final_loss", grid=(lp // tr,),
        in_specs=[row_spec, row_spec, vec, pl.BlockSpec((tr, d), lambda i: (jnp.maximum(i - 1, 0), 0))],
        out_specs=[row_spec, vec, pl.BlockSpec((8, LANES), lambda i: (0, 0))],
        out_shape=[jax.ShapeDtypeStruct((lp, d), F32), jax.ShapeDtypeStruct((1, d), F32), jax.ShapeDtypeStruct((8, LANES), F32)],
        compiler_params=_params(("arbitrary",)),
    )(h0, mo, fw, tgt)


def _pool_select(parts, g):
    out = parts[-1]
    for gi in range(len(parts) - 2, -1, -1):
        out = jnp.where(g == gi, parts[gi], out)
    return out


def _pool_count(row, g, pad):
    win = _pool_select([jnp.full(row.shape, float(w), F32) for w in POOL_WINDOWS], g)
    return jnp.maximum(jnp.minimum((row - pad + 1).astype(F32), win), 1.0)


def _pooled(u, g, row, pad):
    sums, s, span = [], u, 1
    for w in POOL_WINDOWS:
        while span < w:
            s = s + _shift_down(s, span, row)
            span *= 2
        sums.append(s)
    return _pool_select(sums, g) / _pool_count(row, g, pad) - u


def _pooled_adjoint(dp, g, row, pad):
    e = dp / _pool_count(row, g, pad)
    sums, s, span = [], e, 1
    for w in POOL_WINDOWS:
        while span < w:
            s = s + _shift_up(s, span, row)
            span *= 2
        sums.append(s)
    return _pool_select(sums, g) - dp


def _pool_specs(lp, pg, ng, z_off):
    u_spec = pl.BlockSpec((lp, pg), lambda g: (0, g))
    z_spec = pl.BlockSpec((lp, pg), lambda g: (0, z_off + g))
    mix_spec = pl.BlockSpec((1, pg, pg), lambda g: (g, 0, 0))
    vec_spec = pl.BlockSpec((1, pg), lambda g: (0, g))
    return u_spec, z_spec, mix_spec, vec_spec


def _pool_fwd(proj, mix, scale, pad):
    lp = proj.shape[0]
    ng, pg, _ = mix.shape
    pw = ng * pg

    def body(u_ref, z_ref, mix_ref, sc_ref, y_ref):
        g = pl.program_id(0)
        row = lax.broadcasted_iota(jnp.int32, (lp, 1), 0)
        pooled = _pooled(u_ref[...], g, row, pad)
        mixed = _dot(pooled.astype(BF16), mix_ref[0])
        y_ref[...] = (mixed * sc_ref[...] * _silu(z_ref[...])).astype(BF16)

    u_spec, z_spec, mix_spec, vec_spec = _pool_specs(lp, pg, ng, pw // pg)
    return _call(
        body, name="pool_fwd", grid=(ng,), in_specs=[u_spec, z_spec, mix_spec, vec_spec], out_specs=u_spec,
        out_shape=jax.ShapeDtypeStruct((lp, pw), BF16), compiler_params=_params(("parallel",)),
    )(proj, proj, mix, scale)


def _pool_bwd(proj, mix, scale, dy, pad):
    lp = proj.shape[0]
    ng, pg, _ = mix.shape
    pw = ng * pg

    def body(u_ref, z_ref, mix_ref, sc_ref, dy_ref, du_ref, dz_ref, dmix_ref, dsc_ref):
        g = pl.program_id(0)
        row = lax.broadcasted_iota(jnp.int32, (lp, 1), 0)
        real = row >= pad
        z = z_ref[...]
        pooled = _pooled(u_ref[...], g, row, pad).astype(BF16)
        mixed = _dot(pooled, mix_ref[0])
        sig = _sigmoid(z)
        sz = z * sig
        dyv = dy_ref[...]
        dsc_ref[...] = jnp.sum(dyv * mixed * sz, axis=0, keepdims=True)
        d_sz = dyv * mixed * sc_ref[...]
        dz_ref[...] = jnp.where(real, d_sz * (sig + sz * (1.0 - sig)), 0.0).astype(BF16)
        d_mixed = (dyv * sc_ref[...] * sz).astype(BF16)
        dmix_ref[0] = _dot(pooled, d_mixed, TN)
        d_pooled = _dot(d_mixed, mix_ref[0], NT)
        du_ref[...] = jnp.where(real, _pooled_adjoint(d_pooled, g, row, pad), 0.0).astype(BF16)

    u_spec, z_spec, mix_spec, vec_spec = _pool_specs(lp, pg, ng, pw // pg)
    return _call(
        body, name="pool_bwd", grid=(ng,),
        in_specs=[u_spec, z_spec, mix_spec, vec_spec, u_spec], out_specs=[u_spec, u_spec, mix_spec, vec_spec],
        out_shape=[jax.ShapeDtypeStruct((lp, pw), BF16), jax.ShapeDtypeStruct((lp, pw), BF16),
                   jax.ShapeDtypeStruct((ng, pg, pg), F32), jax.ShapeDtypeStruct((1, pw), F32)],
        compiler_params=_params(("parallel",)),
    )(proj, proj, mix, scale, dy)


def _conv_pre(x, w, row):
    kw = w.shape[0]
    y = w[kw - 1:kw, :] * x
    for kk in range(kw - 1):
        y = y + w[kk:kk + 1, :] * _shift_down(x, kw - 1 - kk, row)
    return y


def _conv_post(y, out_scale):
    s = _silu(y)
    if out_scale is None:
        return s
    return s * lax.rsqrt(jnp.sum(s * s, axis=-1, keepdims=True) + NORM_EPS) * out_scale


def _conv_fwd(proj, col_off, w, hd, out_scale, name):
    lp = proj.shape[0]
    kw, width = w.shape
    blk0 = col_off // hd

    def body(x_ref, w_ref, o_ref):
        row = lax.broadcasted_iota(jnp.int32, (lp, 1), 0)
        o_ref[...] = _conv_post(_conv_pre(x_ref[...], w_ref[...], row), out_scale)

    return _call(
        body, name=name, grid=(width // hd,),
        in_specs=[pl.BlockSpec((lp, hd), lambda j: (0, blk0 + j)), pl.BlockSpec((kw, hd), lambda j: (0, j))],
        out_specs=pl.BlockSpec((lp, hd), lambda j: (0, j)),
        out_shape=jax.ShapeDtypeStruct((lp, width), F32), compiler_params=_params(("parallel",)),
    )(proj, w)


def _conv_bwd(proj, col_off, w, d_out, hd, out_scale, pad, name):
    lp = proj.shape[0]
    kw, width = w.shape
    blk0 = col_off // hd

    def body(x_ref, w_ref, do_ref, dx_ref, dw_ref):
        row = lax.broadcasted_iota(jnp.int32, (lp, 1), 0)
        real = row >= pad
        x, wv = x_ref[...], w_ref[...]
        _, vjp = jax.vjp(functools.partial(_conv_post, out_scale=out_scale), _conv_pre(x, wv, row))
        dy = jnp.where(real, vjp(do_ref[...])[0], 0.0)
        dx = wv[kw - 1:kw, :] * dy
        dw_ref[kw - 1:kw, :] = jnp.sum(dy * x, axis=0, keepdims=True)
        for kk in range(kw - 1):
            j = kw - 1 - kk
            dx = dx + wv[kk:kk + 1, :] * _shift_up(dy, j, row)
            dw_ref[kk:kk + 1, :] = jnp.sum(dy * _shift_down(x, j, row), axis=0, keepdims=True)
        dx_ref[...] = jnp.where(real, dx, 0.0).astype(BF16)

    col = pl.BlockSpec((lp, hd), lambda j: (0, j))
    wspec = pl.BlockSpec((kw, hd), lambda j: (0, j))
    return _call(
        body, name=name, grid=(width // hd,),
        in_specs=[pl.BlockSpec((lp, hd), lambda j: (0, blk0 + j)), wspec, col], out_specs=[col, wspec],
        out_shape=[jax.ShapeDtypeStruct((lp, width), BF16), jax.ShapeDtypeStruct((kw, width), F32)],
        compiler_params=_params(("parallel",)),
    )(proj, w, d_out)


HEADS_PER_STEP = 16


def _each(fn, *lists):
    return [fn(*args) for args in zip(*lists)]


def _dot3_each(a_list, b_list):
    hi = lambda t: t.astype(BF16)
    lo = lambda t, t_hi: (t - t_hi.astype(F32)).astype(BF16)
    a_hi, b_hi = _each(hi, a_list), _each(hi, b_list)
    a_lo, b_lo = _each(lo, a_list, a_hi), _each(lo, b_list, b_hi)
    hh, hl, lh = _each(_dot, a_hi, b_hi), _each(_dot, a_hi, b_lo), _each(_dot, a_lo, b_hi)
    return _each(lambda x, y, w: x + (y + w), hh, hl, lh)


@jax.custom_vjp
def _unit_lower_inverse(lmats):
    c = lmats[0].shape[0]
    eye = lax.broadcasted_iota(jnp.int32, (c, c), 0) == lax.broadcasted_iota(jnp.int32, (c, c), 1)
    a = [-m for m in lmats]
    tmat = [jnp.where(eye, 1.0, 0.0).astype(F32) + m for m in a]
    span = 2
    while span < c:
        a = _dot3_each(a, a)
        tmat = _each(lambda t, u: t + u, tmat, _dot3_each(tmat, a))
        span *= 2
    return tuple(tmat)


def _unit_lower_inverse_fwd(lmats):
    tmats = _unit_lower_inverse(lmats)
    return tmats, tmats


def _unit_lower_inverse_bwd(tmats, cts):
    left = _each(lambda t, ct: _dot(t, ct, TN, HIGHEST), tmats, cts)
    return (tuple(_each(lambda m, t: -_dot(m, t, NT, HIGHEST), left, tmats)),)


_unit_lower_inverse.defvjp(_unit_lower_inverse_fwd, _unit_lower_inverse_bwd)


def _chunk_math(states, q, k, v, ba, z, prm, nw, head0, rowmask, n_heads):
    c = q.shape[0]
    heads = list(range(len(states)))
    hd = q.shape[1] // len(states)
    lane = lax.broadcasted_iota(jnp.int32, ba.shape, 1)
    sub = lax.broadcasted_iota(jnp.int32, (ba.shape[1], c), 0)
    ri = lax.broadcasted_iota(jnp.int32, (c, c), 0)
    ci = lax.broadcasted_iota(jnp.int32, (c, c), 1)
    last = lax.broadcasted_iota(jnp.int32, (c, 1), 0) == c - 1
    causal, strict = ri >= ci, ri > ci
    beta_all = _sigmoid(ba) * rowmask
    g_all = -jnp.exp(prm[0:1, :]) * _softplus(ba + prm[1:2, :]) * rowmask
    gcum_all = _dot(jnp.where(causal, 1.0, 0.0).astype(F32), g_all, precision=HIGHEST)
    gcum_t = gcum_all.T
    split = lambda t: [t[:, j * hd:(j + 1) * hd] for j in heads]
    qs, ks, vs, zs = split(q), split(k), split(v), split(z)
    beta = [jnp.sum(jnp.where(lane == head0 + j, beta_all, 0.0), axis=1, keepdims=True) for j in heads]
    gcum = [jnp.sum(jnp.where(lane == n_heads + head0 + j, gcum_all, 0.0), axis=1, keepdims=True) for j in heads]
    grow = [jnp.sum(jnp.where(sub == n_heads + head0 + j, gcum_t, 0.0), axis=0, keepdims=True) for j in heads]
    glast = _each(lambda gc: jnp.sum(jnp.where(last, gc, 0.0), axis=0, keepdims=True), gcum)
    decay = _each(lambda gc, gr: jnp.where(causal, jnp.exp(jnp.where(causal, gc - gr, 0.0)), 0.0), gcum, grow)
    eg = _each(jnp.exp, gcum)
    k_beta = _each(jnp.multiply, ks, beta)
    kk = _each(lambda a, b: _dot(a, b, NT), k_beta, ks)
    tmat = list(_unit_lower_inverse(tuple(_each(lambda m, dc: jnp.where(strict, m * dc, 0.0), kk, decay))))
    u_c = _each(_dot, tmat, _each(jnp.multiply, vs, beta))
    w_c = _each(_dot, tmat, _each(jnp.multiply, k_beta, eg))
    qk = _each(lambda a, b, dc: jnp.where(causal, _dot(a, b, NT) * dc, 0.0), qs, ks, decay)
    v_new = _each(lambda u, w, s: u - _dot(w, s), u_c, w_c, list(states))
    o = _each(lambda a, e, s, m, vn: _dot(a * e, s) + _dot(m, vn), qs, eg, list(states), qk, v_new)
    k_dec = _each(lambda a, gl, gc: a * jnp.exp(gl - gc), ks, glast, gcum)
    new_states = _each(lambda s, gl, kd, vn: s * jnp.exp(gl) + _dot(kd, vn, TN), list(states), glast, k_dec, v_new)
    ys = _each(lambda oj, zj: _rmsnorm(oj, nw) * _silu(zj), o, zs)
    return jnp.concatenate(ys, axis=1), tuple(new_states)


def _chunk_specs(nc, hd, n_heads, z_off, ba_off, rev):
    cidx = (lambda c: nc - 1 - c) if rev else (lambda c: c)
    hb = min(HEADS_PER_STEP, n_heads)
    assert n_heads % hb == 0 and z_off % (hb * hd) == 0 and ba_off % LANES == 0
    blk = lambda off: pl.BlockSpec((CHUNK, hb * hd), lambda c, g: (cidx(c), off + g))
    ba_spec = lambda off: pl.BlockSpec((CHUNK, LANES), lambda c, g: (cidx(c), off // LANES))
    prm_spec = pl.BlockSpec((8, LANES), lambda c, g: (0, 0))
    nw_spec = pl.BlockSpec((1, hd), lambda c, g: (0, 0))
    st_spec = pl.BlockSpec((1, hb, hd, hd), lambda c, g: (cidx(c), g, 0, 0))
    return blk, ba_spec, prm_spec, nw_spec, st_spec, blk(z_off // (hb * hd))


def _rowmask(chunk_idx, pad):
    row = chunk_idx * CHUNK + lax.broadcasted_iota(jnp.int32, (CHUNK, 1), 0)
    return jnp.where(row >= pad, 1.0, 0.0).astype(F32)


def _chunk_fwd(qn, kn, vv, proj, z_off, ba_off, prm, nw, n_heads, pad):
    lp, dn = qn.shape
    hd = dn // n_heads
    nc = lp // CHUNK
    hb = min(HEADS_PER_STEP, n_heads)

    def body(q_ref, k_ref, v_ref, ba_ref, z_ref, prm_ref, nw_ref, y_ref, hist_ref, st_ref):
        c, g = pl.program_id(0), pl.program_id(1)

        @pl.when(c == 0)
        def _():
            for j in range(hb):
                st_ref[g * hb + j] = jnp.zeros((hd, hd), F32)

        states = tuple(st_ref[g * hb + j] for j in range(hb))
        for j in range(hb):
            hist_ref[0, j] = states[j]
        y, new_states = _chunk_math(states, q_ref[...], k_ref[...], v_ref[...], ba_ref[...], z_ref[...], prm_ref[...],
                                    nw_ref[...], g * hb, _rowmask(c, pad), n_heads)
        y_ref[...] = y.astype(BF16)
        for j in range(hb):
            st_ref[g * hb + j] = new_states[j]

    blk, ba_spec, prm_spec, nw_spec, st_spec, z_spec = _chunk_specs(nc, hd, n_heads, z_off, ba_off, False)
    return _call(
        body, name="chunk_fwd", grid=(nc, n_heads // hb),
        in_specs=[blk(0), blk(0), blk(0), ba_spec(ba_off), z_spec, prm_spec, nw_spec], out_specs=[blk(0), st_spec],
        out_shape=[jax.ShapeDtypeStruct((lp, dn), BF16), jax.ShapeDtypeStruct((nc, n_heads, hd, hd), F32)],
        scratch_shapes=[pltpu.VMEM((n_heads, hd, hd), F32)],
        compiler_params=_params(("arbitrary", "arbitrary")),
    )(qn, kn, vv, proj, proj, prm, nw)


def _chunk_bwd(qn, kn, vv, proj, z_off, ba_off, prm, nw, hist, dy, n_heads, pad):
    lp, dn = qn.shape
    hd = dn // n_heads
    nc = lp // CHUNK
    hb = min(HEADS_PER_STEP, n_heads)

    def body(q_ref, k_ref, v_ref, ba_ref, z_ref, prm_ref, nw_ref, hist_ref, dy_ref,
             dq_ref, dk_ref, dv_ref, dba_ref, dz_ref, dprm_ref, dnw_ref, dst_ref):
        step, g = pl.program_id(0), pl.program_id(1)

        @pl.when(step == 0)
        def _():
            for j in range(hb):
                dst_ref[g * hb + j] = jnp.zeros((hd, hd), F32)

        @pl.when((step == 0) & (g == 0))
        def _():
            dprm_ref[...] = jnp.zeros_like(dprm_ref)
            dnw_ref[...] = jnp.zeros_like(dnw_ref)

        @pl.when(g == 0)
        def _():
            dba_ref[...] = jnp.zeros_like(dba_ref)

        fn = functools.partial(_chunk_math, head0=g * hb, rowmask=_rowmask(nc - 1 - step, pad), n_heads=n_heads)
        states = tuple(hist_ref[0, j] for j in range(hb))
        _, vjp = jax.vjp(fn, states, q_ref[...], k_ref[...], v_ref[...], ba_ref[...], z_ref[...], prm_ref[...], nw_ref[...])
        dst, dq, dk, dv, dba, dz, dprm, dnw = vjp((dy_ref[...], tuple(dst_ref[g * hb + j] for j in range(hb))))
        for j in range(hb):
            dst_ref[g * hb + j] = dst[j]
        dq_ref[...] = dq
        dk_ref[...] = dk
        dv_ref[...] = dv
        dz_ref[...] = dz.astype(BF16)
        dba_ref[...] += dba
        dprm_ref[...] += dprm
        dnw_ref[...] += dnw

    blk, ba_spec, prm_spec, nw_spec, st_spec, z_spec = _chunk_specs(nc, hd, n_heads, z_off, ba_off, True)
    f32_full = jax.ShapeDtypeStruct((lp, dn), F32)
    return _call(
        body, name="chunk_bwd", grid=(nc, n_heads // hb),
        in_specs=[blk(0), blk(0), blk(0), ba_spec(ba_off), z_spec, prm_spec, nw_spec, st_spec, blk(0)],
        out_specs=[blk(0), blk(0), blk(0), ba_spec(0), blk(0), prm_spec, nw_spec],
        out_shape=[f32_full, f32_full, f32_full, jax.ShapeDtypeStruct((lp, LANES), F32), jax.ShapeDtypeStruct((lp, dn), BF16),
                   jax.ShapeDtypeStruct((8, LANES), F32), jax.ShapeDtypeStruct((1, hd), F32)],
        scratch_shapes=[pltpu.VMEM((n_heads, hd, hd), F32)],
        compiler_params=_params(("arbitrary", "arbitrary")),
    )(qn, kn, vv, proj, proj, prm, nw, hist, dy)


def _merge_math(p, q, gp, gd):
    return _sigmoid(gp) * p + _sigmoid(gd) * q


def _merge_specs(lp, d, gp_off, gd_off):
    tr, tc = _tile(lp, 264, 16), _tile(d, 1024, LANES)
    blk = pl.BlockSpec((tr, tc), lambda i, j: (i, j))
    gp_spec = pl.BlockSpec((tr, tc), lambda i, j: (i, gp_off // tc + j))
    gd_spec = pl.BlockSpec((tr, tc), lambda i, j: (i, gd_off // tc + j))
    return (lp // tr, d // tc), blk, gp_spec, gd_spec


def _merge_fwd(p, q, proj, gp_off, gd_off):
    lp, d = p.shape
    grid, blk, gp_spec, gd_spec = _merge_specs(lp, d, gp_off, gd_off)

    def body(p_ref, q_ref, gp_ref, gd_ref, o_ref):
        o_ref[...] = _merge_math(p_ref[...], q_ref[...], gp_ref[...], gd_ref[...]).astype(BF16)

    return _call(
        body, name="merge_fwd", grid=grid, in_specs=[blk, blk, gp_spec, gd_spec], out_specs=blk,
        out_shape=jax.ShapeDtypeStruct((lp, d), BF16), compiler_params=_params(("parallel", "parallel")),
    )(p, q, proj, proj)


def _merge_bwd(p, q, proj, gp_off, gd_off, dm):
    lp, d = p.shape
    grid, blk, gp_spec, gd_spec = _merge_specs(lp, d, gp_off, gd_off)

    def body(p_ref, q_ref, gp_ref, gd_ref, dm_ref, dp_ref, dq_ref, dgp_ref, dgd_ref):
        _, vjp = jax.vjp(_merge_math, p_ref[...], q_ref[...], gp_ref[...], gd_ref[...])
        for ref, val in zip((dp_ref, dq_ref, dgp_ref, dgd_ref), vjp(dm_ref[...])):
            ref[...] = val.astype(BF16)

    out = jax.ShapeDtypeStruct((lp, d), BF16)
    return _call(
        body, name="merge_bwd", grid=grid, in_specs=[blk, blk, gp_spec, gd_spec, blk], out_specs=[blk] * 4,
        out_shape=[out] * 4, compiler_params=_params(("parallel", "parallel")),
    )(p, q, proj, proj, dm)


def _adamw(w, g, m, v, name):
    shape = w.shape
    w2, g2, m2, v2 = (t.reshape((-1, shape[-1])) for t in (w, g, m, v))
    rows, cols = w2.shape
    tr = _tile(rows, 128, 8)

    def body(w_ref, g_ref, m_ref, v_ref, d_ref, nm_ref, nv_ref):
        gv = g_ref[...]
        nm = ADAM_B1 * m_ref[...] + (1.0 - ADAM_B1) * gv
        nv = ADAM_B2 * v_ref[...] + (1.0 - ADAM_B2) * (gv * gv)
        m_hat = nm / (1.0 - ADAM_B1 ** ADAM_STEP)
        v_hat = nv / (1.0 - ADAM_B2 ** ADAM_STEP)
        d_ref[...] = -ADAM_LR * (m_hat / (jnp.sqrt(v_hat) + ADAM_EPS) + ADAM_WD * w_ref[...])
        nm_ref[...] = nm
        nv_ref[...] = nv

    blk = pl.BlockSpec((tr, cols), lambda i: (i, 0))
    out = jax.ShapeDtypeStruct((rows, cols), F32)
    res = _call(
        body, name=name, grid=(rows // tr,), in_specs=[blk] * 4, out_specs=[blk] * 3, out_shape=[out] * 3,
        compiler_params=_params(("parallel",)),
    )(w2, g2, m2, v2)
    return tuple(t.reshape(shape) for t in res)


def _coords():
    return lax.axis_index("x"), lax.axis_index("y"), lax.axis_index("c")


def _flip(v, bit):
    return 1 - v if bit else v


CHIP_FLIPS = ((1, 0), (0, 1), (1, 1))
ANY = pl.BlockSpec(memory_space=pl.ANY)


def _all_gather(shards):
    n = len(shards)

    def body(*refs):
        x_refs, out_refs = refs[:n], refs[n:2 * n]
        send_sems, recv_sems, local_sems = refs[2 * n:]
        x, y, c = _coords()
        sibling = (x, y, 1 - c)
        chips = [(_flip(x, fx), _flip(y, fy)) for fx, fy in CHIP_FLIPS]

        def copy(a, k, block, to, from_input=False):
            px, py, pc = block
            slot = out_refs[a].at[4 * px + 2 * py + pc]
            return pltpu.make_async_remote_copy(
                src_ref=x_refs[a] if from_input else slot, dst_ref=slot,
                send_sem=send_sems.at[7 * a + k], recv_sem=recv_sems.at[7 * a + k], device_id=to, device_id_type=MESH)

        mine = [pltpu.make_async_copy(x_refs[a], out_refs[a].at[4 * x + 2 * y + c], local_sems.at[a]) for a in range(n)]
        first = []
        for a in range(n):
            mine[a].start()
            first.append(copy(a, 0, (x, y, c), sibling, True))
            first += [copy(a, 1 + j, (x, y, c), (*chip, c), True) for j, chip in enumerate(chips)]
        for cp in first:
            cp.start()
        passed = []
        for j, chip in enumerate(chips):
            for a in range(n):
                copy(a, 1 + j, (*chip, c), (x, y, c)).wait_recv()
                passed.append(copy(a, 4 + j, (*chip, c), sibling))
                passed[-1].start()
        for a in range(n):
            copy(a, 0, (x, y, 1 - c), (x, y, c)).wait_recv()
            for j, chip in enumerate(chips):
                copy(a, 4 + j, (*chip, 1 - c), (x, y, c)).wait_recv()
        for cp in first + passed:
            cp.wait_send()
        for cp in mine:
            cp.wait()

    return _call(
        body, name="all_gather", in_specs=[ANY] * n, out_specs=[ANY] * n,
        out_shape=[jax.ShapeDtypeStruct((N_DEV,) + s.shape, s.dtype) for s in shards],
        scratch_shapes=[pltpu.SemaphoreType.DMA((7 * n,)), pltpu.SemaphoreType.DMA((7 * n,)), pltpu.SemaphoreType.DMA((n,))],
    )(*shards)


def _rs_to_sibling(gs, name):
    n = len(gs)

    def body(*refs):
        g_refs, got_refs = refs[:n], refs[n:2 * n]
        send_sems, recv_sems = refs[2 * n:]
        x, y, c = _coords()
        copies = []
        for a in range(n):
            for p in range(4):
                cp = pltpu.make_async_remote_copy(
                    src_ref=g_refs[a].at[2 * p + (1 - c)], dst_ref=got_refs[a].at[p], send_sem=send_sems.at[4 * a + p],
                    recv_sem=recv_sems.at[4 * a + p], device_id=(x, y, 1 - c), device_id_type=MESH)
                cp.start()
                copies.append(cp)
        for cp in copies:
            cp.wait()

    return _call(
        body, name=name, in_specs=[ANY] * n, out_specs=[ANY] * n,
        out_shape=[jax.ShapeDtypeStruct((4,) + g.shape[1:], g.dtype) for g in gs],
        scratch_shapes=[pltpu.SemaphoreType.DMA((4 * n,)), pltpu.SemaphoreType.DMA((4 * n,))],
    )(*gs)


def _rs_pair_sum(g, got, c_idx, name):
    _, rows, cols = g.shape
    tr = _tile(rows, 256, 16)

    def body(c_ref, g_ref, got_ref, o_ref):
        o_ref[...] = (g_ref[...].astype(F32) + got_ref[...].astype(F32)).astype(o_ref.dtype)

    grid_spec = pltpu.PrefetchScalarGridSpec(
        num_scalar_prefetch=1, grid=(4, rows // tr),
        in_specs=[pl.BlockSpec((1, tr, cols), lambda p, i, c_ref: (2 * p + c_ref[0], i, 0)),
                  pl.BlockSpec((1, tr, cols), lambda p, i, c_ref: (p, i, 0))],
        out_specs=pl.BlockSpec((1, tr, cols), lambda p, i, c_ref: (p, i, 0)))
    return _call(
        body, name=name, grid_spec=grid_spec, out_shape=jax.ShapeDtypeStruct((4, rows, cols), g.dtype),
        compiler_params=_params(("parallel", "parallel")),
    )(c_idx, g, got)


def _to_chips_copies(p_refs, got_refs, send_sems, recv_sems):
    x, y, c = _coords()
    copies = []
    for a in range(len(p_refs)):
        for k, (fx, fy) in enumerate(CHIP_FLIPS):
            px, py = _flip(x, fx), _flip(y, fy)
            copies.append(pltpu.make_async_remote_copy(
                src_ref=p_refs[a].at[2 * px + py], dst_ref=got_refs[a].at[k], send_sem=send_sems.at[3 * a + k],
                recv_sem=recv_sems.at[3 * a + k], device_id=(px, py, c), device_id_type=MESH))
    return copies


def _rs_to_chips(partials, name):
    n = len(partials)

    def body(*refs):
        copies = _to_chips_copies(refs[:n], refs[n:2 * n], *refs[2 * n:])
        for cp in copies:
            cp.start()
        for cp in copies:
            cp.wait()

    return _call(
        body, name=name, in_specs=[ANY] * n, out_specs=[ANY] * n,
        out_shape=[jax.ShapeDtypeStruct((3,) + p.shape[1:], p.dtype) for p in partials],
        scratch_shapes=[pltpu.SemaphoreType.DMA((3 * n,)), pltpu.SemaphoreType.DMA((3 * n,))],
    )(*partials)


HBM = pl.BlockSpec(memory_space=pltpu.HBM)
SEM = pl.BlockSpec(memory_space=pltpu.SEMAPHORE)
SIDE_EFFECT = pltpu.CompilerParams(has_side_effects=pltpu.SideEffectType.DATAFLOW_SIDE_EFFECTING)


def _split_start(copies_fn, srcs, land_shapes, n_sems, name, after=None):
    n, m = len(srcs), len(land_shapes)
    extra = [] if after is None else [after]

    def body(*refs):
        outs = refs[n + m + len(extra):]
        send_sems, recv_sems, token = outs[0], outs[1], outs[-1]
        for cp in copies_fn(refs[:n], refs[n:n + m], send_sems, recv_sems):
            cp.start()
        token[...] = jnp.zeros_like(token)

    ins = [pltpu.with_memory_space_constraint(t, pltpu.HBM) for t in list(srcs) + [lax.empty(s.shape, s.dtype) for s in land_shapes]]
    res = _call(
        body, name=name, in_specs=[HBM] * (n + m) + [ANY] * len(extra),
        out_specs=[SEM, SEM] + [HBM] * (n + m) + [pl.BlockSpec(memory_space=pltpu.VMEM)],
        out_shape=[pltpu.SemaphoreType.DMA((n_sems,)), pltpu.SemaphoreType.DMA((n_sems,))]
        + [pltpu.HBM(t.shape, t.dtype) for t in ins] + [jax.ShapeDtypeStruct((8, LANES), F32)],
        input_output_aliases={i: 2 + i for i in range(n + m)}, compiler_params=SIDE_EFFECT,
    )(*ins, *extra)
    return dict(sems=(res[0], res[1]), srcs=res[2:2 + n], lands=res[2 + n:2 + n + m], token=res[-1])


def _split_wait(copies_fn, started, after, name):
    n, m = len(started["srcs"]), len(started["lands"])

    def body(*refs):
        for cp in copies_fn(refs[:n], refs[n:n + m], refs[n + m], refs[n + m + 1]):
            cp.wait_send()
            cp.wait_recv()

    bufs = list(started["srcs"]) + list(started["lands"])
    res = _call(
        body, name=name, in_specs=[HBM] * (n + m) + [SEM, SEM, ANY], out_specs=[HBM] * (n + m),
        out_shape=[pltpu.HBM(t.shape, t.dtype) for t in bufs],
        input_output_aliases={i: i for i in range(n + m)}, compiler_params=SIDE_EFFECT,
    )(*bufs, *started["sems"], after)
    return res[:n], res[n:]


def _to_all_copies(x_refs, out_refs, send_sems, recv_sems):
    x, y, c = _coords()
    copies = []
    for a in range(len(x_refs)):
        for k in range(N_DEV - 1):
            fx, fy, fc = ((k + 1) >> 2) & 1, ((k + 1) >> 1) & 1, (k + 1) & 1
            copies.append(pltpu.make_async_remote_copy(
                src_ref=x_refs[a], dst_ref=out_refs[a].at[4 * x + 2 * y + c], send_sem=send_sems.at[7 * a + k],
                recv_sem=recv_sems.at[7 * a + k], device_id=(_flip(x, fx), _flip(y, fy), _flip(c, fc)), device_id_type=MESH))
    return copies


def _fill_own_block(gathered, shard, me_idx, name):
    rows, cols = shard.shape
    tr = _tile(rows, 512, 16)

    def body(me_ref, g_ref, s_ref, o_ref):
        o_ref[0] = s_ref[...]

    grid_spec = pltpu.PrefetchScalarGridSpec(
        num_scalar_prefetch=1, grid=(rows // tr,),
        in_specs=[ANY, pl.BlockSpec((tr, cols), lambda i, me: (i, 0))],
        out_specs=pl.BlockSpec((1, tr, cols), lambda i, me: (me[0], i, 0)))
    return _call(
        body, name=name, grid_spec=grid_spec, out_shape=jax.ShapeDtypeStruct(gathered.shape, gathered.dtype),
        input_output_aliases={1: 0}, compiler_params=_params(("arbitrary",)),
    )(me_idx, gathered, shard)


def _rs_chip_sum(partial, got, chip_idx, name):
    _, rows, cols = partial.shape
    tr = _tile(rows, 256, 16)

    def body(p_idx_ref, p_ref, got_ref, o_ref):
        o_ref[...] = ((p_ref[0].astype(F32) + got_ref[0].astype(F32)) + got_ref[1].astype(F32)) + got_ref[2].astype(F32)

    grid_spec = pltpu.PrefetchScalarGridSpec(
        num_scalar_prefetch=1, grid=(rows // tr,),
        in_specs=[pl.BlockSpec((1, tr, cols), lambda i, p_ref: (p_ref[0], i, 0)),
                  pl.BlockSpec((3, tr, cols), lambda i, p_ref: (0, i, 0))],
        out_specs=pl.BlockSpec((tr, cols), lambda i, p_ref: (i, 0)))
    return _call(
        body, name=name, grid_spec=grid_spec, out_shape=jax.ShapeDtypeStruct((rows, cols), F32),
        compiler_params=_params(("parallel",)),
    )(chip_idx, partial, got)


def _rs_begin(gs, tag, split):
    c_idx = jnp.reshape(lax.axis_index("c"), (1,)).astype(jnp.int32)
    gots = _rs_to_sibling(gs, "rs_to_sibling_" + tag)
    partials = [_rs_pair_sum(g, got, c_idx, "rs_pair_sum_%s%d" % (tag, a)) for a, (g, got) in enumerate(zip(gs, gots))]
    if not split:
        return dict(partials=partials, gots=_rs_to_chips(partials, "rs_to_chips_" + tag))
    lands = [jax.ShapeDtypeStruct((3,) + p.shape[1:], p.dtype) for p in partials]
    return _split_start(_to_chips_copies, partials, lands, 3 * len(partials), "rs_to_chips_start_" + tag)


def _rs_finish(begun, tag, after=None):
    x, y, _ = _coords()
    chip_idx = jnp.reshape(2 * x + y, (1,)).astype(jnp.int32)
    if "gots" in begun:
        partials, gots = begun["partials"], begun["gots"]
    else:
        partials, gots = _split_wait(_to_chips_copies, begun, after, "rs_to_chips_wait_" + tag)
    return [_rs_chip_sum(p, got, chip_idx, "rs_chip_sum_%s%d" % (tag, a)) for a, (p, got) in enumerate(zip(partials, gots))]


RUNS = 3
RUN_FIELDS = 6


def _lane_gather_table(src_of, src_width):
    n_blocks = src_of.shape[0] // LANES
    tab = np.zeros((n_blocks + 1, RUNS, RUN_FIELDS), np.int32)
    tab[:, :, 5] = LANES
    for t in range(n_blocks):
        runs = []
        for lane in range(LANES):
            slab, col = (int(v) for v in src_of[t * LANES + lane])
            if slab < 0:
                continue
            key = (slab, col // LANES, col % LANES - lane)
            if runs and runs[-1][0] == key and runs[-1][2] == lane:
                runs[-1][2] = lane + 1
            else:
                runs.append([key, lane, lane + 1])
        assert len(runs) <= RUNS
        for e, (key, lo, hi) in enumerate(runs):
            tab[t, e] = (key[0], key[1], key[2], lo, hi, min(LANES, src_width - key[1] * LANES))
    return tab.reshape(-1)


def _lane_gather(src, table, out_slabs, out_width, name):
    _, rows, _ = src.shape
    blocks_per_slab = -(-out_width // LANES)

    def body(tab_ref, *refs):
        o_ref = refs[RUNS]
        t = pl.program_id(0)
        lane = lax.broadcasted_iota(jnp.int32, (1, LANES), 1)
        li = lax.broadcasted_iota(jnp.int32, (LANES, LANES), 0)
        ci = lax.broadcasted_iota(jnp.int32, (LANES, LANES), 1)
        acc = None
        for e in range(RUNS):
            base = (t * RUNS + e) * RUN_FIELDS
            shift, lo, hi, valid = tab_ref[base + 2], tab_ref[base + 3], tab_ref[base + 4], tab_ref[base + 5]
            a = jnp.where(lane < valid, refs[e][0], 0.0).astype(BF16)
            sel = jnp.where((li == ci + shift) & (ci >= lo) & (ci < hi), 1.0, 0.0).astype(BF16)
            part = _dot(a, sel)
            acc = part if acc is None else acc + part
        o_ref[0] = acc.astype(BF16)

    def src_spec(e):
        return pl.BlockSpec((1, rows, LANES), lambda t, tab: (tab[(t * RUNS + e) * RUN_FIELDS], 0, tab[(t * RUNS + e) * RUN_FIELDS + 1]))

    grid_spec = pltpu.PrefetchScalarGridSpec(
        num_scalar_prefetch=1, grid=(out_slabs * blocks_per_slab,), in_specs=[src_spec(e) for e in range(RUNS)],
        out_specs=pl.BlockSpec((1, rows, LANES), lambda t, tab: (t // blocks_per_slab, 0, t % blocks_per_slab)))
    return _call(
        body, name=name, grid_spec=grid_spec, out_shape=jax.ShapeDtypeStruct((out_slabs, rows, out_width), BF16),
        compiler_params=_params(("parallel",)),
    )(jnp.asarray(table), src, src, src)


def _all_reduce_small(vec):
    rows, cols = vec.shape

    def body(v_ref, o_ref, buf, send_sems, recv_sems):
        x, y, c = _coords()
        me = 4 * x + 2 * y + c
        buf[me] = v_ref[...]
        copies = []
        for k in range(N_DEV - 1):
            fx, fy, fc = ((k + 1) >> 2) & 1, ((k + 1) >> 1) & 1, (k + 1) & 1
            cp = pltpu.make_async_remote_copy(
                src_ref=v_ref, dst_ref=buf.at[me], send_sem=send_sems.at[k], recv_sem=recv_sems.at[k],
                device_id=(_flip(x, fx), _flip(y, fy), _flip(c, fc)), device_id_type=MESH)
            cp.start()
            copies.append(cp)
        for cp in copies:
            cp.wait()
        total = buf[0]
        for j in range(1, N_DEV):
            total = total + buf[j]
        o_ref[...] = total

    vmem = pl.BlockSpec(memory_space=pltpu.VMEM)
    return _call(
        body, name="all_reduce_small", in_specs=[vmem], out_specs=vmem,
        out_shape=jax.ShapeDtypeStruct((rows, cols), F32),
        scratch_shapes=[pltpu.VMEM((N_DEV, rows, cols), F32), pltpu.SemaphoreType.DMA((N_DEV - 1,)),
                        pltpu.SemaphoreType.DMA((N_DEV - 1,))],
    )(vec)


def _w_in_column_maps(ns, o_ba, n_logit, n_main, n_all):
    own = np.arange(N_DEV * ns)
    work_of_own = np.where(own < o_ba, own, np.where(own < o_ba + n_logit, n_main + own - o_ba, own - n_logit))
    to_work = np.full((n_all, 2), -1, np.int64)
    to_work[work_of_own, 0] = own // ns
    to_work[work_of_own, 1] = own % ns
    slab_width = -(-ns // LANES) * LANES
    to_own = np.full((N_DEV, slab_width, 2), -1, np.int64)
    to_own[:, :ns, 0] = 0
    to_own[:, :ns, 1] = work_of_own.reshape(N_DEV, ns)
    return to_work, to_own.reshape(-1, 2)


def kernel(x, meta_tokens, norm_w, w_in, conv_w, A_log, dt_bias, pool_mix, pool_scale, dn_norm_w, w_pool_out, w_dn_out, w_o, final_norm_w, loss_target, m_meta_tokens, m_norm_w, m_w_in, m_conv_w, m_A_log, m_dt_bias, m_pool_mix, m_pool_scale, m_dn_norm_w, m_w_pool_out, m_w_dn_out, m_w_o, m_final_norm_w, v_meta_tokens, v_norm_w, v_w_in, v_conv_w, v_A_log, v_dt_bias, v_pool_mix, v_pool_scale, v_dn_norm_w, v_w_pool_out, v_w_dn_out, v_w_o, v_final_norm_w):
    seq, d = x.shape[1], x.shape[2]
    n_meta = meta_tokens.shape[0]
    n_heads, hd = A_log.shape[-1], dn_norm_w.shape[-1]
    dn = n_heads * hd
    pw, ng = pool_scale.shape[-1], pool_mix.shape[1]
    pg = pw // ng
    kw = conv_w.shape[1]
    pad = (-n_meta) % CHUNK
    x0 = pad + n_meta
    lp = x0 + seq
    ns = w_in.shape[-1]
    in_cols = N_DEV * ns
    o_q, o_k, o_v, o_zd = 2 * pw, 2 * pw + dn, 2 * pw + 2 * dn, 2 * pw + 3 * dn
    o_ba = 2 * pw + 4 * dn
    o_gp, o_gd = o_ba, o_ba + d
    n_main = o_gd + d
    n_all = n_main + 2 * LANES
    assert lp % CHUNK == 0 and in_cols == n_main + 2 * n_heads and 2 * n_heads <= LANES and hd == LANES
    cs, ms = conv_w.shape[-1], meta_tokens.shape[-1]
    mr = pool_mix.shape[2]
    assert ms == pg and cs % pg == 0
    to_work, to_own = _w_in_column_maps(ns, o_ba, 2 * n_heads, n_main, n_all)
    cols_major = lambda t: jnp.transpose(t, (1, 0, 2)).reshape(t.shape[1], N_DEV * t.shape[2])

    win_g, mix_g, conv_g, meta_g = _all_gather(
        [w_in[0].astype(BF16), pool_mix[0].reshape(ng * mr, pg).astype(BF16), conv_w[0], meta_tokens])
    late_shards = [w_pool_out[0].astype(BF16), w_dn_out[0].astype(BF16), w_o[0].astype(BF16)]
    late_weights = _split_start(_to_all_copies, late_shards, [jax.ShapeDtypeStruct((N_DEV,) + s.shape, BF16) for s in late_shards],
                                (N_DEV - 1) * len(late_shards), "gather_out_proj_start", after=meta_g)
    norm_w_in = norm_w + late_weights["token"][0, 0]
    w_all = _lane_gather(win_g, _lane_gather_table(to_work, ns), 1, n_all, "w_in_to_work").reshape(d, n_all)
    mix_f = jnp.transpose(mix_g.reshape(N_DEV, ng, mr, pg), (1, 0, 2, 3)).reshape(ng, pg, pg)
    conv_f = cols_major(conv_g)
    meta_f = cols_major(meta_g)

    h0, xn = _norm_in_fwd(x[0], meta_f, norm_w_in, pad)
    proj = _matmul(xn, w_all, NN, F32, 1056, 768, 2048, "proj")
    y_pool = _pool_fwd(proj, mix_f, pool_scale, pad)
    conv_q, conv_k, conv_v = (conv_f[:, i * dn:(i + 1) * dn] for i in range(3))
    qn = _conv_fwd(proj, o_q, conv_q, hd, float(hd) ** -0.5, "conv_q_fwd")
    kn = _conv_fwd(proj, o_k, conv_k, hd, 1.0, "conv_k_fwd")
    vv = _conv_fwd(proj, o_v, conv_v, hd, None, "conv_v_fwd")
    logit_lanes = (n_heads, LANES - 2 * n_heads)
    prm = jnp.pad(A_log, ((0, 7), logit_lanes)) + jnp.pad(dt_bias, ((1, 6), logit_lanes))
    y_dn, hist = _chunk_fwd(qn, kn, vv, proj, o_zd, n_main, prm, dn_norm_w, n_heads, pad)
    me_idx = jnp.reshape(4 * lax.axis_index("x") + 2 * lax.axis_index("y") + lax.axis_index("c"), (1,)).astype(jnp.int32)
    _, landed = _split_wait(_to_all_copies, late_weights, y_dn, "gather_out_proj_wait")
    wpo_g, wdn_g, wo_g = (_fill_own_block(g, s, me_idx, "own_block_%d" % i) for i, (g, s) in enumerate(zip(landed, late_shards)))
    wpo_f = cols_major(wpo_g)
    wdn_f = wdn_g.reshape(dn, d)
    wo_f = wo_g.reshape(d, d)
    p_out = _matmul(y_pool, wpo_f, NN, F32, 1056, 1024, 1024, "pool_out")
    q_out = _matmul(y_dn, wdn_f, NN, F32, 1056, 1024, 2048, "dn_out")
    merged = _merge_fwd(p_out, q_out, proj, o_gp, o_gd)
    mo = _matmul(merged, wo_f, NN, F32, 1056, 1024, 2048, "w_o_fwd")
    dh1, d_fw, loss_part = _final_loss(h0, mo, final_norm_w.reshape(1, d), loss_target[0], x0)

    d_merged = _matmul(dh1, wo_f, NT, F32, 1056, 1024, 1024, "w_o_bwd_x")
    g_wo = _matmul(merged, dh1, TN, BF16, 1024, 1024, 704, "w_o_bwd_w")
    d_p, d_q, d_gp, d_gd = _merge_bwd(p_out, q_out, proj, o_gp, o_gd, d_merged)
    d_ypool = _matmul(d_p, wpo_f, NT, F32, 1056, 1024, 2048, "pool_out_bwd_x")
    g_wpo = _matmul(y_pool, d_p, TN, BF16, 1024, 1024, 704, "pool_out_bwd_w", col_blocks=N_DEV)
    d_ydn = _matmul(d_q, wdn_f, NT, F32, 1056, 1024, 2048, "dn_out_bwd_x")
    g_wdn = _matmul(y_dn, d_q, TN, BF16, 1024, 1024, 704, "dn_out_bwd_w")
    rs_early = _rs_begin([g_wpo, g_wdn.reshape(N_DEV, dn // N_DEV, d), g_wo.reshape(N_DEV, d // N_DEV, d)], "early", split=True)
    started = rs_early["token"][0, 0]
    d_u, d_zp, g_mix, g_pscale = _pool_bwd(proj, mix_f, pool_scale + started, d_ypool, pad)
    d_qn, d_kn, d_vv, d_ba, d_zd, d_prm, g_dnw = _chunk_bwd(qn, kn, vv, proj, o_zd, n_main, prm + started, dn_norm_w, hist, d_ydn, n_heads, pad)
    d_qr, g_cq = _conv_bwd(proj, o_q, conv_q, d_qn, hd, float(hd) ** -0.5, pad, "conv_q_bwd")
    d_kr, g_ck = _conv_bwd(proj, o_k, conv_k, d_kn, hd, 1.0, pad, "conv_k_bwd")
    d_vr, g_cv = _conv_bwd(proj, o_v, conv_v, d_vv, hd, None, pad, "conv_v_bwd")
    d_proj = jnp.concatenate([d_u, d_zp, d_qr, d_kr, d_vr, d_zd, d_gp, d_gd, d_ba.astype(BF16), jnp.zeros((lp, LANES), BF16)], axis=1)
    xn_t, rs_late, token = xn.T, [], None
    for half in range(2):
        rows = slice(half * (d // 2), (half + 1) * (d // 2))
        g_wall = _matmul(xn_t[rows], d_proj, NN, F32, 1024, 768, lp, "w_in_bwd_w_%d" % half, after=token)
        g_win = _lane_gather(g_wall.reshape(1, d // 2, n_all), _lane_gather_table(to_own, n_all), N_DEV, ns, "w_in_grad_to_own_%d" % half)
        rs_late.append(_rs_begin([g_win], "late%d" % half, split=True))
        token = rs_late[-1]["token"]
    d_xn = _matmul(d_proj, w_all, NT, F32, 1056, 1024, 768, "w_in_bwd_x", after=token)
    d_h0, g_nw = _norm_in_bwd(h0, norm_w, d_xn, dh1)
    grad_x = d_h0[x0:][None]

    by_cols = lambda t: jnp.transpose(t.reshape(t.shape[0], N_DEV, t.shape[1] // N_DEV), (1, 0, 2))
    g_conv = by_cols(jnp.concatenate([g_cq, g_ck, g_cv], axis=1)).reshape(N_DEV, kw * cs // pg, pg)
    conv_rows = -(-g_conv.shape[1] // 16) * 16
    g_small = jnp.concatenate(
        [jnp.transpose(g_mix.reshape(ng, N_DEV, mr, pg), (1, 0, 2, 3)).reshape(N_DEV, ng * mr, pg), by_cols(d_h0[pad:x0]),
         jnp.pad(g_conv, ((0, 0), (0, conv_rows - g_conv.shape[1]), (0, 0)))], axis=1).astype(BF16)
    r_small, = _rs_finish(_rs_begin([g_small], "small", split=False), "small")
    r_mix, r_meta = r_small[:ng * mr], r_small[ng * mr:ng * mr + n_meta]
    r_conv = r_small[ng * mr + n_meta:ng * mr + n_meta + kw * cs // pg]
    r_wpo, r_wdn, r_wo = _rs_finish(rs_early, "early", after=r_small)

    small = [g_nw[0], d_fw[0], g_pscale[0], g_dnw[0], d_prm[0], d_prm[1], loss_part[0]]
    s_sizes = [t.shape[0] for t in small]
    s_cols = -(-sum(s_sizes) // (8 * LANES)) * LANES
    s_vec = jnp.concatenate(small + [jnp.zeros((8 * s_cols - sum(s_sizes),), F32)]).reshape(8, s_cols)
    s_red = _all_reduce_small(s_vec)
    s_sum = s_red.reshape(-1)
    r_win = jnp.concatenate([_rs_finish(begun, "late%d" % half, after=s_red)[0] for half, begun in enumerate(rs_late)], axis=0)
    s_offs = [sum(s_sizes[:i]) for i in range(len(s_sizes))]
    s_take = lambda i, n=None, o=0: s_sum[s_offs[i] + o:s_offs[i] + o + (s_sizes[i] if n is None else n)]

    grads = {
        "meta_tokens": r_meta, "norm_w": s_take(0).reshape(norm_w.shape),
        "w_in": r_win.reshape(w_in.shape), "conv_w": r_conv.reshape(conv_w.shape),
        "A_log": s_take(4, n_heads, n_heads).reshape(A_log.shape), "dt_bias": s_take(5, n_heads, n_heads).reshape(dt_bias.shape),
        "pool_mix": r_mix.reshape(pool_mix.shape), "pool_scale": s_take(2).reshape(pool_scale.shape),
        "dn_norm_w": s_take(3).reshape(dn_norm_w.shape), "w_pool_out": r_wpo.reshape(w_pool_out.shape),
        "w_dn_out": r_wdn.reshape(w_dn_out.shape), "w_o": r_wo.reshape(w_o.shape),
        "final_norm_w": s_take(1).reshape(final_norm_w.shape),
    }
    loss = s_take(6, 1)[0]

    weights = dict(meta_tokens=meta_tokens, norm_w=norm_w, w_in=w_in, conv_w=conv_w, A_log=A_log, dt_bias=dt_bias,
                   pool_mix=pool_mix, pool_scale=pool_scale, dn_norm_w=dn_norm_w, w_pool_out=w_pool_out, w_dn_out=w_dn_out,
                   w_o=w_o, final_norm_w=final_norm_w)
    m_in = dict(meta_tokens=m_meta_tokens, norm_w=m_norm_w, w_in=m_w_in, conv_w=m_conv_w, A_log=m_A_log, dt_bias=m_dt_bias,
                pool_mix=m_pool_mix, pool_scale=m_pool_scale, dn_norm_w=m_dn_norm_w, w_pool_out=m_w_pool_out,
                w_dn_out=m_w_dn_out, w_o=m_w_o, final_norm_w=m_final_norm_w)
    v_in = dict(meta_tokens=v_meta_tokens, norm_w=v_norm_w, w_in=v_w_in, conv_w=v_conv_w, A_log=v_A_log, dt_bias=v_dt_bias,
                pool_mix=v_pool_mix, pool_scale=v_pool_scale, dn_norm_w=v_dn_norm_w, w_pool_out=v_w_pool_out,
                w_dn_out=v_w_dn_out, w_o=v_w_o, final_norm_w=v_final_norm_w)
    names = list(weights)
    upd = {n: _adamw(weights[n], grads[n], m_in[n], v_in[n], "adamw_" + n) for n in names}
    return (loss, grad_x, *[grads[n] for n in names], *[upd[n][0] for n in names], *[upd[n][1] for n in names],
            *[upd[n][2] for n in names])
```

```python
import functools

import jax
import jax.numpy as jnp
import numpy as np
from jax import lax
from jax.experimental import pallas as pl
from jax.experimental.pallas import tpu as pltpu

F32 = jnp.float32
BF16 = jnp.bfloat16
HIGHEST = lax.Precision.HIGHEST
MESH = pl.DeviceIdType.MESH

CHUNK = 64
NORM_EPS = 1e-6
POOL_WINDOWS = (2, 4, 8, 16)
ADAM_LR, ADAM_B1, ADAM_B2, ADAM_EPS, ADAM_WD, ADAM_STEP = 0.001, 0.9, 0.999, 1e-08, 0.01, 10
N_DEV = 8
LANES = 128
VMEM_LIMIT = 48 * 1024 * 1024

NN = (((1,), (0,)), ((), ()))
NT = (((1,), (1,)), ((), ()))
TN = (((0,), (0,)), ((), ()))


def _call(body, **kw):
    return pl.pallas_call(body, **kw)


def _params(sem=None):
    return pltpu.CompilerParams(dimension_semantics=sem, vmem_limit_bytes=VMEM_LIMIT)


def _tile(n, pref, align):
    for d in range(min(pref, n), 0, -1):
        if n % d == 0 and d % align == 0:
            return d
    return n


def _dot(a, b, dims=NN, precision=None):
    return lax.dot_general(a, b, dims, precision=precision, preferred_element_type=F32)


def _sigmoid(x):
    return 1.0 / (1.0 + jnp.exp(-x))


def _silu(x):
    return x * _sigmoid(x)


def _softplus(x):
    return jnp.maximum(x, 0.0) + jnp.log(1.0 + jnp.exp(-jnp.abs(x)))


def _rmsnorm(x, w):
    return x * lax.rsqrt(jnp.mean(x * x, axis=-1, keepdims=True) + NORM_EPS) * w


def _shift_down(x, j, row):
    if j == 0:
        return x
    return jnp.where(row >= j, pltpu.roll(x, j, 0), 0.0)


def _shift_up(x, j, row):
    if j == 0:
        return x
    n = x.shape[0]
    return jnp.where(row < n - j, pltpu.roll(x, n - j, 0), 0.0)


def _matmul(a, b, dims, out_dtype, tm, tn, tk, name, col_blocks=None, after=None):
    ta = dims == TN
    tb = dims == NT
    m, kdim = (a.shape[1], a.shape[0]) if ta else a.shape
    n = b.shape[0] if tb else b.shape[1]
    if col_blocks:
        tn = n // col_blocks
    tm, tn, tk = _tile(m, tm, 8), _tile(n, tn, LANES), _tile(kdim, tk, LANES if not ta else 16)
    nk = kdim // tk

    n_extra = 0 if after is None else 1

    def body(a_ref, b_ref, *refs):
        o_ref, scratch = refs[n_extra], refs[n_extra + 1:]
        part = _dot(a_ref[...].astype(BF16), b_ref[...].astype(BF16), dims)
        if nk == 1:
            o_ref[...] = part.astype(o_ref.dtype).reshape(o_ref.shape)
            return
        acc_ref, = scratch
        k = pl.program_id(2)

        @pl.when(k == 0)
        def _():
            acc_ref[...] = part

        @pl.when(k > 0)
        def _():
            acc_ref[...] += part

        @pl.when(k == nk - 1)
        def _():
            o_ref[...] = acc_ref[...].astype(o_ref.dtype).reshape(o_ref.shape)

    a_spec = pl.BlockSpec((tk, tm), lambda i, j, k: (k, i)) if ta else pl.BlockSpec((tm, tk), lambda i, j, k: (i, k))
    b_spec = pl.BlockSpec((tn, tk), lambda i, j, k: (j, k)) if tb else pl.BlockSpec((tk, tn), lambda i, j, k: (k, j))
    if col_blocks:
        out_spec = pl.BlockSpec((1, tm, tn), lambda i, j, k: (j, i, 0))
        out_shape = jax.ShapeDtypeStruct((col_blocks, m, tn), out_dtype)
    else:
        out_spec = pl.BlockSpec((tm, tn), lambda i, j, k: (i, j))
        out_shape = jax.ShapeDtypeStruct((m, n), out_dtype)
    return _call(
        body, name=name, grid=(m // tm, n // tn, nk),
        in_specs=[a_spec, b_spec] + [ANY] * n_extra, out_specs=out_spec, out_shape=out_shape,
        scratch_shapes=[] if nk == 1 else [pltpu.VMEM((tm, tn), F32)],
        compiler_params=_params(("parallel", "parallel", "arbitrary")),
    )(a, b, *([] if after is None else [after]))


def _norm_in_fwd(x2d, meta, w, pad):
    seq, d = x2d.shape
    tr = pad + meta.shape[0]
    assert seq % tr == 0 and tr % 16 == 0
    lp = tr + seq

    def body(x_ref, m_ref, w_ref, h_ref, o_ref):
        def emit(h):
            h_ref[...] = h
            o_ref[...] = _rmsnorm(h, w_ref[...]).astype(BF16)

        @pl.when(pl.program_id(0) == 0)
        def _():
            emit(jnp.concatenate([jnp.zeros((pad, d), F32), m_ref[...]], axis=0) if pad else m_ref[...])

        @pl.when(pl.program_id(0) > 0)
        def _():
            emit(x_ref[...])

    row = pl.BlockSpec((tr, d), lambda i: (i, 0))
    return _call(
        body, name="norm_in_fwd", grid=(lp // tr,),
        in_specs=[pl.BlockSpec((tr, d), lambda i: (jnp.maximum(i - 1, 0), 0)), pl.BlockSpec(meta.shape, lambda i: (0, 0)),
                  pl.BlockSpec((1, d), lambda i: (0, 0))],
        out_specs=[row, row],
        out_shape=[jax.ShapeDtypeStruct((lp, d), F32), jax.ShapeDtypeStruct((lp, d), BF16)],
        compiler_params=_params(("arbitrary",)),
    )(x2d, meta, w)


def _norm_in_bwd(h0, w, dxn, dh1, x0):
    lp, d = h0.shape
    tr = x0
    assert lp % tr == 0

    def body(h_ref, w_ref, da_ref, dh1_ref, head_ref, gx_ref, dw_ref):
        i = pl.program_id(0)
        _, vjp = jax.vjp(_rmsnorm, h_ref[...], w_ref[...])
        dh, dw = vjp(da_ref[...])
        dh = dh + dh1_ref[...]

        @pl.when(i == 0)
        def _():
            head_ref[...] = dh
            dw_ref[...] = dw

        @pl.when(i > 0)
        def _():
            gx_ref[...] = dh
            dw_ref[...] += dw

    row = pl.BlockSpec((tr, d), lambda i: (i, 0))
    vec = pl.BlockSpec((1, d), lambda i: (0, 0))
    return _call(
        body, name="norm_in_bwd", grid=(lp // tr,),
        in_specs=[row, vec, row, row],
        out_specs=[pl.BlockSpec((tr, d), lambda i: (0, 0)), pl.BlockSpec((tr, d), lambda i: (jnp.maximum(i - 1, 0), 0)), vec],
        out_shape=[jax.ShapeDtypeStruct((tr, d), F32), jax.ShapeDtypeStruct((lp - tr, d), F32), jax.ShapeDtypeStruct((1, d), F32)],
        compiler_params=_params(("arbitrary",)),
    )(h0, w, dxn, dh1)


def _final_loss(h0, mo, fw, tgt, x0):
    lp, d = h0.shape
    tr = x0
    assert lp % tr == 0

    def body(h_ref, mo_ref, fw_ref, t_ref, dh_ref, dw_ref, loss_ref):
        i = pl.program_id(0)
        row = i * tr + lax.broadcasted_iota(jnp.int32, (tr, 1), 0)
        mask = jnp.where(row >= x0, 1.0, 0.0).astype(F32)
        tgt_v = t_ref[...]

        def loss_fn(h1, w):
            err = _rmsnorm(h1, w) - tgt_v
            return 0.5 * jnp.sum(jnp.mean(err * err, axis=-1, keepdims=True) * mask, axis=0, keepdims=True)

        loss, vjp = jax.vjp(loss_fn, h_ref[...] + mo_ref[...], fw_ref[...])
        dh, dw = vjp(jnp.ones((1, 1), F32))
        dh_ref[...] = dh

        @pl.when(i == 0)
        def _():
            dw_ref[...] = jnp.zeros_like(dw_ref)
            loss_ref[...] = jnp.zeros_like(loss_ref)

        dw_ref[...] += dw
        loss_ref[...] += jnp.broadcast_to(loss, loss_ref.shape)

    row_spec = pl.BlockSpec((tr, d), lambda i: (i, 0))
    vec = pl.BlockSpec((1, d), lambda i: (0, 0))
    return _call(
        body, name="final_loss", grid=(lp // tr,),
        in_specs=[row_spec, row_spec, vec, pl.BlockSpec((tr, d), lambda i: (jnp.maximum(i - 1, 0), 0))],
        out_specs=[row_spec, vec, pl.BlockSpec((8, LANES), lambda i: (0, 0))],
        out_shape=[jax.ShapeDtypeStruct((lp, d), F32), jax.ShapeDtypeStruct((1, d), F32), jax.ShapeDtypeStruct((8, LANES), F32)],
        compiler_params=_params(("arbitrary",)),
    )(h0, mo, fw, tgt)


def _pool_select(parts, g):
    out = parts[-1]
    for gi in range(len(parts) - 2, -1, -1):
        out = jnp.where(g == gi, parts[gi], out)
    return out


def _pool_count(row, g, pad):
    win = _pool_select([jnp.full(row.shape, float(w), F32) for w in POOL_WINDOWS], g)
    return jnp.maximum(jnp.minimum((row - pad + 1).astype(F32), win), 1.0)


def _pooled(u, g, row, pad):
    sums, s, span = [], u, 1
    for w in POOL_WINDOWS:
        while span < w:
            s = s + _shift_down(s, span, row)
            span *= 2
        sums.append(s)
    return _pool_select(sums, g) / _pool_count(row, g, pad) - u


def _pooled_adjoint(dp, g, row, pad):
    e = dp / _pool_count(row, g, pad)
    sums, s, span = [], e, 1
    for w in POOL_WINDOWS:
        while span < w:
            s = s + _shift_up(s, span, row)
            span *= 2
        sums.append(s)
    return _pool_select(sums, g) - dp


def _pool_specs(lp, pg, ng, z_off):
    u_spec = pl.BlockSpec((lp, pg), lambda g: (0, g))
    z_spec = pl.BlockSpec((lp, pg), lambda g: (0, z_off + g))
    mix_spec = pl.BlockSpec((1, pg, pg), lambda g: (g, 0, 0))
    vec_spec = pl.BlockSpec((1, pg), lambda g: (0, g))
    return u_spec, z_spec, mix_spec, vec_spec


def _pool_fwd(proj, mix, scale, pad):
    lp = proj.shape[0]
    ng, pg, _ = mix.shape
    pw = ng * pg

    def body(u_ref, z_ref, mix_ref, sc_ref, y_ref):
        g = pl.program_id(0)
        row = lax.broadcasted_iota(jnp.int32, (lp, 1), 0)
        pooled = _pooled(u_ref[...], g, row, pad)
        mixed = _dot(pooled.astype(BF16), mix_ref[0])
        y_ref[...] = (mixed * sc_ref[...] * _silu(z_ref[...])).astype(BF16)

    u_spec, z_spec, mix_spec, vec_spec = _pool_specs(lp, pg, ng, pw // pg)
    return _call(
        body, name="pool_fwd", grid=(ng,), in_specs=[u_spec, z_spec, mix_spec, vec_spec], out_specs=u_spec,
        out_shape=jax.ShapeDtypeStruct((lp, pw), BF16), compiler_params=_params(("parallel",)),
    )(proj, proj, mix, scale)


def _pool_bwd(proj, mix, scale, dy, pad):
    lp = proj.shape[0]
    ng, pg, _ = mix.shape
    pw = ng * pg

    def body(u_ref, z_ref, mix_ref, sc_ref, dy_ref, du_ref, dz_ref, dmix_ref, dsc_ref):
        g = pl.program_id(0)
        row = lax.broadcasted_iota(jnp.int32, (lp, 1), 0)
        real = row >= pad
        z = z_ref[...]
        pooled = _pooled(u_ref[...], g, row, pad).astype(BF16)
        mixed = _dot(pooled, mix_ref[0])
        sig = _sigmoid(z)
        sz = z * sig
        dyv = dy_ref[...]
        dsc_ref[...] = jnp.sum(dyv * mixed * sz, axis=0, keepdims=True)
        d_sz = dyv * mixed * sc_ref[...]
        dz_ref[...] = jnp.where(real, d_sz * (sig + sz * (1.0 - sig)), 0.0).astype(BF16)
        d_mixed = (dyv * sc_ref[...] * sz).astype(BF16)
        dmix_ref[0] = _dot(pooled, d_mixed, TN)
        d_pooled = _dot(d_mixed, mix_ref[0], NT)
        du_ref[...] = jnp.where(real, _pooled_adjoint(d_pooled, g, row, pad), 0.0).astype(BF16)

    u_spec, z_spec, mix_spec, vec_spec = _pool_specs(lp, pg, ng, pw // pg)
    return _call(
        body, name="pool_bwd", grid=(ng,),
        in_specs=[u_spec, z_spec, mix_spec, vec_spec, u_spec], out_specs=[u_spec, u_spec, mix_spec, vec_spec],
        out_shape=[jax.ShapeDtypeStruct((lp, pw), BF16), jax.ShapeDtypeStruct((lp, pw), BF16),
                   jax.ShapeDtypeStruct((ng, pg, pg), F32), jax.ShapeDtypeStruct((1, pw), F32)],
        compiler_params=_params(("parallel",)),
    )(proj, proj, mix, scale, dy)


def _conv_pre(x, w, row):
    kw = w.shape[0]
    y = w[kw - 1:kw, :] * x
    for kk in range(kw - 1):
        y = y + w[kk:kk + 1, :] * _shift_down(x, kw - 1 - kk, row)
    return y


def _conv_post(y, out_scale):
    s = _silu(y)
    if out_scale is None:
        return s
    return s * lax.rsqrt(jnp.sum(s * s, axis=-1, keepdims=True) + NORM_EPS) * out_scale


def _conv_fwd(proj, col_off, w, hd, out_scale, name):
    lp = proj.shape[0]
    kw, width = w.shape
    blk0 = col_off // hd

    def body(x_ref, w_ref, o_ref):
        row = lax.broadcasted_iota(jnp.int32, (lp, 1), 0)
        o_ref[...] = _conv_post(_conv_pre(x_ref[...], w_ref[...], row), out_scale)

    return _call(
        body, name=name, grid=(width // hd,),
        in_specs=[pl.BlockSpec((lp, hd), lambda j: (0, blk0 + j)), pl.BlockSpec((kw, hd), lambda j: (0, j))],
        out_specs=pl.BlockSpec((lp, hd), lambda j: (0, j)),
        out_shape=jax.ShapeDtypeStruct((lp, width), F32), compiler_params=_params(("parallel",)),
    )(proj, w)


def _conv_bwd(proj, col_off, w, d_out, hd, out_scale, pad, name):
    lp = proj.shape[0]
    kw, width = w.shape
    blk0 = col_off // hd

    def body(x_ref, w_ref, do_ref, dx_ref, dw_ref):
        row = lax.broadcasted_iota(jnp.int32, (lp, 1), 0)
        real = row >= pad
        x, wv = x_ref[...], w_ref[...]
        _, vjp = jax.vjp(functools.partial(_conv_post, out_scale=out_scale), _conv_pre(x, wv, row))
        dy = jnp.where(real, vjp(do_ref[...])[0], 0.0)
        dx = wv[kw - 1:kw, :] * dy
        dw_ref[kw - 1:kw, :] = jnp.sum(dy * x, axis=0, keepdims=True)
        for kk in range(kw - 1):
            j = kw - 1 - kk
            dx = dx + wv[kk:kk + 1, :] * _shift_up(dy, j, row)
            dw_ref[kk:kk + 1, :] = jnp.sum(dy * _shift_down(x, j, row), axis=0, keepdims=True)
        dx_ref[...] = jnp.where(real, dx, 0.0).astype(BF16)

    col = pl.BlockSpec((lp, hd), lambda j: (0, j))
    wspec = pl.BlockSpec((kw, hd), lambda j: (0, j))
    return _call(
        body, name=name, grid=(width // hd,),
        in_specs=[pl.BlockSpec((lp, hd), lambda j: (0, blk0 + j)), wspec, col], out_specs=[col, wspec],
        out_shape=[jax.ShapeDtypeStruct((lp, width), BF16), jax.ShapeDtypeStruct((kw, width), F32)],
        compiler_params=_params(("parallel",)),
    )(proj, w, d_out)


HEADS_PER_STEP = 16


def _each(fn, *lists):
    return [fn(*args) for args in zip(*lists)]


def _dot3_each(a_list, b_list):
    hi = lambda t: t.astype(BF16)
    lo = lambda t, t_hi: (t - t_hi.astype(F32)).astype(BF16)
    a_hi, b_hi = _each(hi, a_list), _each(hi, b_list)
    a_lo, b_lo = _each(lo, a_list, a_hi), _each(lo, b_list, b_hi)
    hh, hl, lh = _each(_dot, a_hi, b_hi), _each(_dot, a_hi, b_lo), _each(_dot, a_lo, b_hi)
    return _each(lambda x, y, w: x + (y + w), hh, hl, lh)


@jax.custom_vjp
def _unit_lower_inverse(lmats):
    c = lmats[0].shape[0]
    eye = lax.broadcasted_iota(jnp.int32, (c, c), 0) == lax.broadcasted_iota(jnp.int32, (c, c), 1)
    a = [-m for m in lmats]
    tmat = [jnp.where(eye, 1.0, 0.0).astype(F32) + m for m in a]
    span = 2
    while span < c:
        a = _dot3_each(a, a)
        tmat = _each(lambda t, u: t + u, tmat, _dot3_each(tmat, a))
        span *= 2
    return tuple(tmat)


def _unit_lower_inverse_fwd(lmats):
    tmats = _unit_lower_inverse(lmats)
    return tmats, tmats


def _unit_lower_inverse_bwd(tmats, cts):
    left = _each(lambda t, ct: _dot(t, ct, TN, HIGHEST), tmats, cts)
    return (tuple(_each(lambda m, t: -_dot(m, t, NT, HIGHEST), left, tmats)),)


_unit_lower_inverse.defvjp(_unit_lower_inverse_fwd, _unit_lower_inverse_bwd)


def _chunk_math(states, q, k, v, ba, z, prm, nw, head0, rowmask, n_heads):
    c = q.shape[0]
    heads = list(range(len(states)))
    hd = q.shape[1] // len(states)
    lane = lax.broadcasted_iota(jnp.int32, ba.shape, 1)
    sub = lax.broadcasted_iota(jnp.int32, (ba.shape[1], c), 0)
    ri = lax.broadcasted_iota(jnp.int32, (c, c), 0)
    ci = lax.broadcasted_iota(jnp.int32, (c, c), 1)
    last = lax.broadcasted_iota(jnp.int32, (c, 1), 0) == c - 1
    causal, strict = ri >= ci, ri > ci
    beta_all = _sigmoid(ba) * rowmask
    g_all = -jnp.exp(prm[0:1, :]) * _softplus(ba + prm[1:2, :]) * rowmask
    gcum_all = _dot(jnp.where(causal, 1.0, 0.0).astype(F32), g_all, precision=HIGHEST)
    gcum_t = gcum_all.T
    split = lambda t: [t[:, j * hd:(j + 1) * hd] for j in heads]
    qs, ks, vs, zs = split(q), split(k), split(v), split(z)
    beta = [jnp.sum(jnp.where(lane == head0 + j, beta_all, 0.0), axis=1, keepdims=True) for j in heads]
    gcum = [jnp.sum(jnp.where(lane == n_heads + head0 + j, gcum_all, 0.0), axis=1, keepdims=True) for j in heads]
    grow = [jnp.sum(jnp.where(sub == n_heads + head0 + j, gcum_t, 0.0), axis=0, keepdims=True) for j in heads]
    glast = _each(lambda gc: jnp.sum(jnp.where(last, gc, 0.0), axis=0, keepdims=True), gcum)
    decay = _each(lambda gc, gr: jnp.where(causal, jnp.exp(jnp.where(causal, gc - gr, 0.0)), 0.0), gcum, grow)
    eg = _each(jnp.exp, gcum)
    k_beta = _each(jnp.multiply, ks, beta)
    kk = _each(lambda a, b: _dot(a, b, NT), k_beta, ks)
    tmat = list(_unit_lower_inverse(tuple(_each(lambda m, dc: jnp.where(strict, m * dc, 0.0), kk, decay))))
    u_c = _each(_dot, tmat, _each(jnp.multiply, vs, beta))
    w_c = _each(_dot, tmat, _each(jnp.multiply, k_beta, eg))
    qk = _each(lambda a, b, dc: jnp.where(causal, _dot(a, b, NT) * dc, 0.0), qs, ks, decay)
    v_new = _each(lambda u, w, s: u - _dot(w, s), u_c, w_c, list(states))
    o = _each(lambda a, e, s, m, vn: _dot(a * e, s) + _dot(m, vn), qs, eg, list(states), qk, v_new)
    k_dec = _each(lambda a, gl, gc: a * jnp.exp(gl - gc), ks, glast, gcum)
    new_states = _each(lambda s, gl, kd, vn: s * jnp.exp(gl) + _dot(kd, vn, TN), list(states), glast, k_dec, v_new)
    ys = _each(lambda oj, zj: _rmsnorm(oj, nw) * _silu(zj), o, zs)
    return jnp.concatenate(ys, axis=1), tuple(new_states)


def _chunk_specs(nc, hd, n_heads, z_off, ba_off, rev):
    cidx = (lambda c: nc - 1 - c) if rev else (lambda c: c)
    hb = min(HEADS_PER_STEP, n_heads)
    assert n_heads % hb == 0 and z_off % (hb * hd) == 0 and ba_off % LANES == 0
    blk = lambda off: pl.BlockSpec((CHUNK, hb * hd), lambda c, g: (cidx(c), off + g))
    ba_spec = lambda off: pl.BlockSpec((CHUNK, LANES), lambda c, g: (cidx(c), off // LANES))
    prm_spec = pl.BlockSpec((8, LANES), lambda c, g: (0, 0))
    nw_spec = pl.BlockSpec((1, hd), lambda c, g: (0, 0))
    st_spec = pl.BlockSpec((1, hb, hd, hd), lambda c, g: (cidx(c), g, 0, 0))
    return blk, ba_spec, prm_spec, nw_spec, st_spec, blk(z_off // (hb * hd))


def _rowmask(chunk_idx, pad):
    row = chunk_idx * CHUNK + lax.broadcasted_iota(jnp.int32, (CHUNK, 1), 0)
    return jnp.where(row >= pad, 1.0, 0.0).astype(F32)


def _chunk_fwd(qn, kn, vv, proj, z_off, ba_off, prm, nw, n_heads, pad):
    lp, dn = qn.shape
    hd = dn // n_heads
    nc = lp // CHUNK
    hb = min(HEADS_PER_STEP, n_heads)

    def body(q_ref, k_ref, v_ref, ba_ref, z_ref, prm_ref, nw_ref, y_ref, hist_ref, st_ref):
        c, g = pl.program_id(0), pl.program_id(1)

        @pl.when(c == 0)
        def _():
            for j in range(hb):
                st_ref[g * hb + j] = jnp.zeros((hd, hd), F32)

        states = tuple(st_ref[g * hb + j] for j in range(hb))
        for j in range(hb):
            hist_ref[0, j] = states[j]
        y, new_states = _chunk_math(states, q_ref[...], k_ref[...], v_ref[...], ba_ref[...], z_ref[...], prm_ref[...],
                                    nw_ref[...], g * hb, _rowmask(c, pad), n_heads)
        y_ref[...] = y.astype(BF16)
        for j in range(hb):
            st_ref[g * hb + j] = new_states[j]

    blk, ba_spec, prm_spec, nw_spec, st_spec, z_spec = _chunk_specs(nc, hd, n_heads, z_off, ba_off, False)
    return _call(
        body, name="chunk_fwd", grid=(nc, n_heads // hb),
        in_specs=[blk(0), blk(0), blk(0), ba_spec(ba_off), z_spec, prm_spec, nw_spec], out_specs=[blk(0), st_spec],
        out_shape=[jax.ShapeDtypeStruct((lp, dn), BF16), jax.ShapeDtypeStruct((nc, n_heads, hd, hd), F32)],
        scratch_shapes=[pltpu.VMEM((n_heads, hd, hd), F32)],
        compiler_params=_params(("arbitrary", "arbitrary")),
    )(qn, kn, vv, proj, proj, prm, nw)


def _chunk_bwd(qn, kn, vv, proj, z_off, ba_off, prm, nw, hist, dy, n_heads, pad):
    lp, dn = qn.shape
    hd = dn // n_heads
    nc = lp // CHUNK
    hb = min(HEADS_PER_STEP, n_heads)

    def body(q_ref, k_ref, v_ref, ba_ref, z_ref, prm_ref, nw_ref, hist_ref, dy_ref,
             dq_ref, dk_ref, dv_ref, dba_ref, dz_ref, dprm_ref, dnw_ref, dst_ref):
        step, g = pl.program_id(0), pl.program_id(1)

        @pl.when(step == 0)
        def _():
            for j in range(hb):
                dst_ref[g * hb + j] = jnp.zeros((hd, hd), F32)

        @pl.when((step == 0) & (g == 0))
        def _():
            dprm_ref[...] = jnp.zeros_like(dprm_ref)
            dnw_ref[...] = jnp.zeros_like(dnw_ref)

        @pl.when(g == 0)
        def _():
            dba_ref[...] = jnp.zeros_like(dba_ref)

        fn = functools.partial(_chunk_math, head0=g * hb, rowmask=_rowmask(nc - 1 - step, pad), n_heads=n_heads)
        states = tuple(hist_ref[0, j] for j in range(hb))
        _, vjp = jax.vjp(fn, states, q_ref[...], k_ref[...], v_ref[...], ba_ref[...], z_ref[...], prm_ref[...], nw_ref[...])
        dst, dq, dk, dv, dba, dz, dprm, dnw = vjp((dy_ref[...], tuple(dst_ref[g * hb + j] for j in range(hb))))
        for j in range(hb):
            dst_ref[g * hb + j] = dst[j]
        dq_ref[...] = dq
        dk_ref[...] = dk
        dv_ref[...] = dv
        dz_ref[...] = dz.astype(BF16)
        dba_ref[...] += dba
        dprm_ref[...] += dprm
        dnw_ref[...] += dnw

    blk, ba_spec, prm_spec, nw_spec, st_spec, z_spec = _chunk_specs(nc, hd, n_heads, z_off, ba_off, True)
    f32_full = jax.ShapeDtypeStruct((lp, dn), F32)
    return _call(
        body, name="chunk_bwd", grid=(nc, n_heads // hb),
        in_specs=[blk(0), blk(0), blk(0), ba_spec(ba_off), z_spec, prm_spec, nw_spec, st_spec, blk(0)],
        out_specs=[blk(0), blk(0), blk(0), ba_spec(0), blk(0), prm_spec, nw_spec],
        out_shape=[f32_full, f32_full, f32_full, jax.ShapeDtypeStruct((lp, LANES), F32), jax.ShapeDtypeStruct((lp, dn), BF16),
                   jax.ShapeDtypeStruct((8, LANES), F32), jax.ShapeDtypeStruct((1, hd), F32)],
        scratch_shapes=[pltpu.VMEM((n_heads, hd, hd), F32)],
        compiler_params=_params(("arbitrary", "arbitrary")),
    )(qn, kn, vv, proj, proj, prm, nw, hist, dy)


def _merge_math(p, q, gp, gd):
    return _sigmoid(gp) * p + _sigmoid(gd) * q


def _merge_specs(lp, d, gp_off, gd_off):
    tr, tc = _tile(lp, 264, 16), _tile(d, 1024, LANES)
    blk = pl.BlockSpec((tr, tc), lambda i, j: (i, j))
    gp_spec = pl.BlockSpec((tr, tc), lambda i, j: (i, gp_off // tc + j))
    gd_spec = pl.BlockSpec((tr, tc), lambda i, j: (i, gd_off // tc + j))
    return (lp // tr, d // tc), blk, gp_spec, gd_spec


def _merge_fwd(p, q, proj, gp_off, gd_off):
    lp, d = p.shape
    grid, blk, gp_spec, gd_spec = _merge_specs(lp, d, gp_off, gd_off)

    def body(p_ref, q_ref, gp_ref, gd_ref, o_ref):
        o_ref[...] = _merge_math(p_ref[...], q_ref[...], gp_ref[...], gd_ref[...]).astype(BF16)

    return _call(
        body, name="merge_fwd", grid=grid, in_specs=[blk, blk, gp_spec, gd_spec], out_specs=blk,
        out_shape=jax.ShapeDtypeStruct((lp, d), BF16), compiler_params=_params(("parallel", "parallel")),
    )(p, q, proj, proj)


def _merge_bwd(p, q, proj, gp_off, gd_off, dm):
    lp, d = p.shape
    grid, blk, gp_spec, gd_spec = _merge_specs(lp, d, gp_off, gd_off)

    def body(p_ref, q_ref, gp_ref, gd_ref, dm_ref, dp_ref, dq_ref, dgp_ref, dgd_ref):
        _, vjp = jax.vjp(_merge_math, p_ref[...], q_ref[...], gp_ref[...], gd_ref[...])
        for ref, val in zip((dp_ref, dq_ref, dgp_ref, dgd_ref), vjp(dm_ref[...])):
            ref[...] = val.astype(BF16)

    out = jax.ShapeDtypeStruct((lp, d), BF16)
    return _call(
        body, name="merge_bwd", grid=grid, in_specs=[blk, blk, gp_spec, gd_spec, blk], out_specs=[blk] * 4,
        out_shape=[out] * 4, compiler_params=_params(("parallel", "parallel")),
    )(p, q, proj, proj, dm)


def _adamw(w, g, m, v, name):
    shape = w.shape
    w2, g2, m2, v2 = (t.reshape((-1, shape[-1])) for t in (w, g, m, v))
    rows, cols = w2.shape
    tr = _tile(rows, 128, 8)

    def body(w_ref, g_ref, m_ref, v_ref, d_ref, nm_ref, nv_ref):
        gv = g_ref[...]
        nm = ADAM_B1 * m_ref[...] + (1.0 - ADAM_B1) * gv
        nv = ADAM_B2 * v_ref[...] + (1.0 - ADAM_B2) * (gv * gv)
        m_hat = nm / (1.0 - ADAM_B1 ** ADAM_STEP)
        v_hat = nv / (1.0 - ADAM_B2 ** ADAM_STEP)
        d_ref[...] = -ADAM_LR * (m_hat / (jnp.sqrt(v_hat) + ADAM_EPS) + ADAM_WD * w_ref[...])
        nm_ref[...] = nm
        nv_ref[...] = nv

    blk = pl.BlockSpec((tr, cols), lambda i: (i, 0))
    out = jax.ShapeDtypeStruct((rows, cols), F32)
    res = _call(
        body, name=name, grid=(rows // tr,), in_specs=[blk] * 4, out_specs=[blk] * 3, out_shape=[out] * 3,
        compiler_params=_params(("parallel",)),
    )(w2, g2, m2, v2)
    return tuple(t.reshape(shape) for t in res)


def _coords():
    return lax.axis_index("x"), lax.axis_index("y"), lax.axis_index("c")


def _flip(v, bit):
    return 1 - v if bit else v


CHIP_FLIPS = ((1, 0), (0, 1), (1, 1))
ANY = pl.BlockSpec(memory_space=pl.ANY)


def _all_gather(shards):
    n = len(shards)

    def body(*refs):
        x_refs, out_refs = refs[:n], refs[n:2 * n]
        send_sems, recv_sems, local_sems = refs[2 * n:]
        x, y, c = _coords()
        sibling = (x, y, 1 - c)
        chips = [(_flip(x, fx), _flip(y, fy)) for fx, fy in CHIP_FLIPS]

        def copy(a, k, block, to, from_input=False):
            px, py, pc = block
            slot = out_refs[a].at[4 * px + 2 * py + pc]
            return pltpu.make_async_remote_copy(
                src_ref=x_refs[a] if from_input else slot, dst_ref=slot,
                send_sem=send_sems.at[7 * a + k], recv_sem=recv_sems.at[7 * a + k], device_id=to, device_id_type=MESH)

        mine = [pltpu.make_async_copy(x_refs[a], out_refs[a].at[4 * x + 2 * y + c], local_sems.at[a]) for a in range(n)]
        first = []
        for a in range(n):
            mine[a].start()
            first.append(copy(a, 0, (x, y, c), sibling, True))
            first += [copy(a, 1 + j, (x, y, c), (*chip, c), True) for j, chip in enumerate(chips)]
        for cp in first:
            cp.start()
        passed = []
        for j, chip in enumerate(chips):
            for a in range(n):
                copy(a, 1 + j, (*chip, c), (x, y, c)).wait_recv()
                passed.append(copy(a, 4 + j, (*chip, c), sibling))
                passed[-1].start()
        for a in range(n):
            copy(a, 0, (x, y, 1 - c), (x, y, c)).wait_recv()
            for j, chip in enumerate(chips):
                copy(a, 4 + j, (*chip, 1 - c), (x, y, c)).wait_recv()
        for cp in first + passed:
            cp.wait_send()
        for cp in mine:
            cp.wait()

    return _call(
        body, name="all_gather", in_specs=[ANY] * n, out_specs=[ANY] * n,
        out_shape=[jax.ShapeDtypeStruct((N_DEV,) + s.shape, s.dtype) for s in shards],
        scratch_shapes=[pltpu.SemaphoreType.DMA((7 * n,)), pltpu.SemaphoreType.DMA((7 * n,)), pltpu.SemaphoreType.DMA((n,))],
    )(*shards)


def _rs_to_sibling(gs, name):
    n = len(gs)

    def body(*refs):
        g_refs, got_refs = refs[:n], refs[n:2 * n]
        send_sems, recv_sems = refs[2 * n:]
        x, y, c = _coords()
        copies = []
        for a in range(n):
            for p in range(4):
                cp = pltpu.make_async_remote_copy(
                    src_ref=g_refs[a].at[2 * p + (1 - c)], dst_ref=got_refs[a].at[p], send_sem=send_sems.at[4 * a + p],
                    recv_sem=recv_sems.at[4 * a + p], device_id=(x, y, 1 - c), device_id_type=MESH)
                cp.start()
                copies.append(cp)
        for cp in copies:
            cp.wait()

    return _call(
        body, name=name, in_specs=[ANY] * n, out_specs=[ANY] * n,
        out_shape=[jax.ShapeDtypeStruct((4,) + g.shape[1:], g.dtype) for g in gs],
        scratch_shapes=[pltpu.SemaphoreType.DMA((4 * n,)), pltpu.SemaphoreType.DMA((4 * n,))],
    )(*gs)


def _rs_pair_sum(g, got, c_idx, name):
    _, rows, cols = g.shape
    tr = _tile(rows, 256, 16)

    def body(c_ref, g_ref, got_ref, o_ref):
        o_ref[...] = (g_ref[...].astype(F32) + got_ref[...].astype(F32)).astype(o_ref.dtype)

    grid_spec = pltpu.PrefetchScalarGridSpec(
        num_scalar_prefetch=1, grid=(4, rows // tr),
        in_specs=[pl.BlockSpec((1, tr, cols), lambda p, i, c_ref: (2 * p + c_ref[0], i, 0)),
                  pl.BlockSpec((1, tr, cols), lambda p, i, c_ref: (p, i, 0))],
        out_specs=pl.BlockSpec((1, tr, cols), lambda p, i, c_ref: (p, i, 0)))
    return _call(
        body, name=name, grid_spec=grid_spec, out_shape=jax.ShapeDtypeStruct((4, rows, cols), g.dtype),
        compiler_params=_params(("parallel", "parallel")),
    )(c_idx, g, got)


def _to_chips_copies(p_refs, got_refs, send_sems, recv_sems):
    x, y, c = _coords()
    copies = []
    for a in range(len(p_refs)):
        for k, (fx, fy) in enumerate(CHIP_FLIPS):
            px, py = _flip(x, fx), _flip(y, fy)
            copies.append(pltpu.make_async_remote_copy(
                src_ref=p_refs[a].at[2 * px + py], dst_ref=got_refs[a].at[k], send_sem=send_sems.at[3 * a + k],
                recv_sem=recv_sems.at[3 * a + k], device_id=(px, py, c), device_id_type=MESH))
    return copies


def _rs_to_chips(partials, name):
    n = len(partials)

    def body(*refs):
        copies = _to_chips_copies(refs[:n], refs[n:2 * n], *refs[2 * n:])
        for cp in copies:
            cp.start()
        for cp in copies:
            cp.wait()

    return _call(
        body, name=name, in_specs=[ANY] * n, out_specs=[ANY] * n,
        out_shape=[jax.ShapeDtypeStruct((3,) + p.shape[1:], p.dtype) for p in partials],
        scratch_shapes=[pltpu.SemaphoreType.DMA((3 * n,)), pltpu.SemaphoreType.DMA((3 * n,))],
    )(*partials)


HBM = pl.BlockSpec(memory_space=pltpu.HBM)
SEM = pl.BlockSpec(memory_space=pltpu.SEMAPHORE)
SIDE_EFFECT = pltpu.CompilerParams(has_side_effects=pltpu.SideEffectType.DATAFLOW_SIDE_EFFECTING)


def _split_start(copies_fn, srcs, land_shapes, n_sems, name, after=None):
    n, m = len(srcs), len(land_shapes)
    extra = [] if after is None else [after]

    def body(*refs):
        outs = refs[n + m + len(extra):]
        send_sems, recv_sems, token = outs[0], outs[1], outs[-1]
        for cp in copies_fn(refs[:n], refs[n:n + m], send_sems, recv_sems):
            cp.start()
        token[...] = jnp.zeros_like(token)

    ins = [pltpu.with_memory_space_constraint(t, pltpu.HBM) for t in list(srcs) + [lax.empty(s.shape, s.dtype) for s in land_shapes]]
    res = _call(
        body, name=name, in_specs=[HBM] * (n + m) + [ANY] * len(extra),
        out_specs=[SEM, SEM] + [HBM] * (n + m) + [pl.BlockSpec(memory_space=pltpu.VMEM)],
        out_shape=[pltpu.SemaphoreType.DMA((n_sems,)), pltpu.SemaphoreType.DMA((n_sems,))]
        + [pltpu.HBM(t.shape, t.dtype) for t in ins] + [jax.ShapeDtypeStruct((8, LANES), F32)],
        input_output_aliases={i: 2 + i for i in range(n + m)}, compiler_params=SIDE_EFFECT,
    )(*ins, *extra)
    return dict(sems=(res[0], res[1]), srcs=res[2:2 + n], lands=res[2 + n:2 + n + m], token=res[-1])


def _split_wait(copies_fn, started, after, name):
    n, m = len(started["srcs"]), len(started["lands"])

    def body(*refs):
        for cp in copies_fn(refs[:n], refs[n:n + m], refs[n + m], refs[n + m + 1]):
            cp.wait_send()
            cp.wait_recv()

    bufs = list(started["srcs"]) + list(started["lands"])
    res = _call(
        body, name=name, in_specs=[HBM] * (n + m) + [SEM, SEM, ANY], out_specs=[HBM] * (n + m),
        out_shape=[pltpu.HBM(t.shape, t.dtype) for t in bufs],
        input_output_aliases={i: i for i in range(n + m)}, compiler_params=SIDE_EFFECT,
    )(*bufs, *started["sems"], after)
    return res[:n], res[n:]


def _to_all_copies(x_refs, out_refs, send_sems, recv_sems):
    x, y, c = _coords()
    copies = []
    for a in range(len(x_refs)):
        for k in range(N_DEV - 1):
            fx, fy, fc = ((k + 1) >> 2) & 1, ((k + 1) >> 1) & 1, (k + 1) & 1
            copies.append(pltpu.make_async_remote_copy(
                src_ref=x_refs[a], dst_ref=out_refs[a].at[4 * x + 2 * y + c], send_sem=send_sems.at[7 * a + k],
                recv_sem=recv_sems.at[7 * a + k], device_id=(_flip(x, fx), _flip(y, fy), _flip(c, fc)), device_id_type=MESH))
    return copies


def _fill_own_block(gathered, shard, me_idx, name):
    rows, cols = shard.shape
    tr = _tile(rows, 512, 16)

    def body(me_ref, g_ref, s_ref, o_ref):
        o_ref[0] = s_ref[...]

    grid_spec = pltpu.PrefetchScalarGridSpec(
        num_scalar_prefetch=1, grid=(rows // tr,),
        in_specs=[ANY, pl.BlockSpec((tr, cols), lambda i, me: (i, 0))],
        out_specs=pl.BlockSpec((1, tr, cols), lambda i, me: (me[0], i, 0)))
    return _call(
        body, name=name, grid_spec=grid_spec, out_shape=jax.ShapeDtypeStruct(gathered.shape, gathered.dtype),
        input_output_aliases={1: 0}, compiler_params=_params(("arbitrary",)),
    )(me_idx, gathered, shard)


def _rs_chip_sum(partial, got, chip_idx, name, part=0, n_parts=1, dst=None):
    _, rows, cols = partial.shape
    tr = _tile(rows, 256, 16)
    steps = rows // tr
    n_dst = 0 if dst is None else 1

    def body(p_idx_ref, p_ref, got_ref, *refs):
        refs[n_dst][...] = ((p_ref[0].astype(F32) + got_ref[0].astype(F32)) + got_ref[1].astype(F32)) + got_ref[2].astype(F32)

    grid_spec = pltpu.PrefetchScalarGridSpec(
        num_scalar_prefetch=1, grid=(steps,),
        in_specs=[pl.BlockSpec((1, tr, cols), lambda i, p_ref: (p_ref[0], i, 0)),
                  pl.BlockSpec((3, tr, cols), lambda i, p_ref: (0, i, 0))] + [ANY] * n_dst,
        out_specs=pl.BlockSpec((tr, cols), lambda i, p_ref: (part * steps + i, 0)))
    return _call(
        body, name=name, grid_spec=grid_spec, out_shape=jax.ShapeDtypeStruct((n_parts * rows, cols), F32),
        input_output_aliases={3: 0} if n_dst else {}, compiler_params=_params(("parallel",)),
    )(chip_idx, partial, got, *([] if dst is None else [dst]))


def _rs_begin(gs, tag, split):
    c_idx = jnp.reshape(lax.axis_index("c"), (1,)).astype(jnp.int32)
    gots = _rs_to_sibling(gs, "rs_to_sibling_" + tag)
    partials = [_rs_pair_sum(g, got, c_idx, "rs_pair_sum_%s%d" % (tag, a)) for a, (g, got) in enumerate(zip(gs, gots))]
    if not split:
        return dict(partials=partials, gots=_rs_to_chips(partials, "rs_to_chips_" + tag))
    lands = [jax.ShapeDtypeStruct((3,) + p.shape[1:], p.dtype) for p in partials]
    return _split_start(_to_chips_copies, partials, lands, 3 * len(partials), "rs_to_chips_start_" + tag)


def _rs_finish(begun, tag, after=None, part=0, n_parts=1, dsts=None):
    x, y, _ = _coords()
    chip_idx = jnp.reshape(2 * x + y, (1,)).astype(jnp.int32)
    if "gots" in begun:
        partials, gots = begun["partials"], begun["gots"]
    else:
        partials, gots = _split_wait(_to_chips_copies, begun, after, "rs_to_chips_wait_" + tag)
    return [_rs_chip_sum(p, got, chip_idx, "rs_chip_sum_%s%d" % (tag, a), part, n_parts, None if dsts is None else dsts[a])
            for a, (p, got) in enumerate(zip(partials, gots))]


RUNS = 3
RUN_FIELDS = 6


def _lane_gather_table(src_of, src_width):
    n_blocks = src_of.shape[0] // LANES
    tab = np.zeros((n_blocks + 1, RUNS, RUN_FIELDS), np.int32)
    tab[:, :, 5] = LANES
    for t in range(n_blocks):
        runs = []
        for lane in range(LANES):
            slab, col = (int(v) for v in src_of[t * LANES + lane])
            if slab < 0:
                continue
            key = (slab, col // LANES, col % LANES - lane)
            if runs and runs[-1][0] == key and runs[-1][2] == lane:
                runs[-1][2] = lane + 1
            else:
                runs.append([key, lane, lane + 1])
        assert len(runs) <= RUNS
        slots = [None] * RUNS
        for key, lo, hi in sorted(runs, key=lambda r: r[0][:2]):
            e = key[1] % 2 if slots[key[1] % 2] is None else slots.index(None)
            slots[e] = (key[0], key[1], key[2], lo, hi, min(LANES, src_width - key[1] * LANES))
        for e in range(RUNS):
            tab[t, e] = slots[e] if slots[e] is not None else (tab[t - 1, e, 0], tab[t - 1, e, 1], 0, 0, 0, LANES) if t else tab[t, e]
    tab[n_blocks, :, :2] = tab[n_blocks - 1, :, :2]
    return tab.reshape(-1)


def _lane_gather(src, table, out_slabs, out_width, name):
    _, rows, _ = src.shape
    blocks_per_slab = -(-out_width // LANES)

    def body(tab_ref, *refs):
        o_ref, acc_ref = refs[RUNS], refs[RUNS + 1]
        t = pl.program_id(0)
        lane = lax.broadcasted_iota(jnp.int32, (1, LANES), 1)
        li = lax.broadcasted_iota(jnp.int32, (LANES, LANES), 0)
        ci = lax.broadcasted_iota(jnp.int32, (LANES, LANES), 1)
        acc_ref[...] = jnp.zeros_like(acc_ref)
        for e in range(RUNS):
            base = (t * RUNS + e) * RUN_FIELDS
            shift, lo, hi, valid = tab_ref[base + 2], tab_ref[base + 3], tab_ref[base + 4], tab_ref[base + 5]

            @pl.when(hi > lo)
            def _():
                a = jnp.where(lane < valid, refs[e][0], 0.0).astype(BF16)
                sel = jnp.where((li == ci + shift) & (ci >= lo) & (ci < hi), 1.0, 0.0).astype(BF16)
                acc_ref[...] += _dot(a, sel)

        o_ref[0] = acc_ref[...].astype(BF16)

    def src_spec(e):
        return pl.BlockSpec((1, rows, LANES), lambda t, tab: (tab[(t * RUNS + e) * RUN_FIELDS], 0, tab[(t * RUNS + e) * RUN_FIELDS + 1]))

    grid_spec = pltpu.PrefetchScalarGridSpec(
        num_scalar_prefetch=1, grid=(out_slabs * blocks_per_slab,), in_specs=[src_spec(e) for e in range(RUNS)],
        out_specs=pl.BlockSpec((1, rows, LANES), lambda t, tab: (t // blocks_per_slab, 0, t % blocks_per_slab)),
        scratch_shapes=[pltpu.VMEM((rows, LANES), F32)])
    return _call(
        body, name=name, grid_spec=grid_spec, out_shape=jax.ShapeDtypeStruct((out_slabs, rows, out_width), BF16),
        compiler_params=_params(("arbitrary",)),
    )(jnp.asarray(table), src, src, src)


def _all_reduce_small(vec):
    rows, cols = vec.shape

    def body(v_ref, o_ref, buf, send_sems, recv_sems):
        x, y, c = _coords()
        me = 4 * x + 2 * y + c
        buf[me] = v_ref[...]
        copies = []
        for k in range(N_DEV - 1):
            fx, fy, fc = ((k + 1) >> 2) & 1, ((k + 1) >> 1) & 1, (k + 1) & 1
            cp = pltpu.make_async_remote_copy(
                src_ref=v_ref, dst_ref=buf.at[me], send_sem=send_sems.at[k], recv_sem=recv_sems.at[k],
                device_id=(_flip(x, fx), _flip(y, fy), _flip(c, fc)), device_id_type=MESH)
            cp.start()
            copies.append(cp)
        for cp in copies:
            cp.wait()
        total = buf[0]
        for j in range(1, N_DEV):
            total = total + buf[j]
        o_ref[...] = total

    vmem = pl.BlockSpec(memory_space=pltpu.VMEM)
    return _call(
        body, name="all_reduce_small", in_specs=[vmem], out_specs=vmem,
        out_shape=jax.ShapeDtypeStruct((rows, cols), F32),
        scratch_shapes=[pltpu.VMEM((N_DEV, rows, cols), F32), pltpu.SemaphoreType.DMA((N_DEV - 1,)),
                        pltpu.SemaphoreType.DMA((N_DEV - 1,))],
    )(vec)


def _w_in_column_maps(ns, o_ba, n_logit, n_main, n_all):
    own = np.arange(N_DEV * ns)
    work_of_own = np.where(own < o_ba, own, np.where(own < o_ba + n_logit, n_main + own - o_ba, own - n_logit))
    to_work = np.full((n_all, 2), -1, np.int64)
    to_work[work_of_own, 0] = own // ns
    to_work[work_of_own, 1] = own % ns
    slab_width = -(-ns // LANES) * LANES
    to_own = np.full((N_DEV, slab_width, 2), -1, np.int64)
    to_own[:, :ns, 0] = 0
    to_own[:, :ns, 1] = work_of_own.reshape(N_DEV, ns)
    return to_work, to_own.reshape(-1, 2)


def kernel(x, meta_tokens, norm_w, w_in, conv_w, A_log, dt_bias, pool_mix, pool_scale, dn_norm_w, w_pool_out, w_dn_out, w_o, final_norm_w, loss_target, m_meta_tokens, m_norm_w, m_w_in, m_conv_w, m_A_log, m_dt_bias, m_pool_mix, m_pool_scale, m_dn_norm_w, m_w_pool_out, m_w_dn_out, m_w_o, m_final_norm_w, v_meta_tokens, v_norm_w, v_w_in, v_conv_w, v_A_log, v_dt_bias, v_pool_mix, v_pool_scale, v_dn_norm_w, v_w_pool_out, v_w_dn_out, v_w_o, v_final_norm_w):
    seq, d = x.shape[1], x.shape[2]
    n_meta = meta_tokens.shape[0]
    n_heads, hd = A_log.shape[-1], dn_norm_w.shape[-1]
    dn = n_heads * hd
    pw, ng = pool_scale.shape[-1], pool_mix.shape[1]
    pg = pw // ng
    kw = conv_w.shape[1]
    pad = (-n_meta) % CHUNK
    x0 = pad + n_meta
    lp = x0 + seq
    ns = w_in.shape[-1]
    in_cols = N_DEV * ns
    o_q, o_k, o_v, o_zd = 2 * pw, 2 * pw + dn, 2 * pw + 2 * dn, 2 * pw + 3 * dn
    o_ba = 2 * pw + 4 * dn
    o_gp, o_gd = o_ba, o_ba + d
    n_main = o_gd + d
    n_all = n_main + 2 * LANES
    assert lp % CHUNK == 0 and in_cols == n_main + 2 * n_heads and 2 * n_heads <= LANES and hd == LANES
    cs, ms = conv_w.shape[-1], meta_tokens.shape[-1]
    mr = pool_mix.shape[2]
    assert ms == pg and cs % pg == 0
    to_work, to_own = _w_in_column_maps(ns, o_ba, 2 * n_heads, n_main, n_all)
    cols_major = lambda t: jnp.transpose(t, (1, 0, 2)).reshape(t.shape[1], N_DEV * t.shape[2])

    win_g, mix_g, conv_g, meta_g = _all_gather(
        [w_in[0].astype(BF16), pool_mix[0].reshape(ng * mr, pg).astype(BF16), conv_w[0], meta_tokens])
    late_shards = [w_pool_out[0].astype(BF16), w_dn_out[0].astype(BF16), w_o[0].astype(BF16)]
    late_weights = _split_start(_to_all_copies, late_shards, [jax.ShapeDtypeStruct((N_DEV,) + s.shape, BF16) for s in late_shards],
                                (N_DEV - 1) * len(late_shards), "gather_out_proj_start", after=meta_g)
    norm_w_in = norm_w + late_weights["token"][0, 0]
    w_all = _lane_gather(win_g, _lane_gather_table(to_work, ns), 1, n_all, "w_in_to_work").reshape(d, n_all)
    mix_f = jnp.transpose(mix_g.reshape(N_DEV, ng, mr, pg), (1, 0, 2, 3)).reshape(ng, pg, pg)
    conv_f = cols_major(conv_g)
    meta_f = cols_major(meta_g)

    h0, xn = _norm_in_fwd(x[0], meta_f, norm_w_in, pad)
    proj = _matmul(xn, w_all, NN, F32, 1056, 768, 2048, "proj")
    y_pool = _pool_fwd(proj, mix_f, pool_scale, pad)
    conv_q, conv_k, conv_v = (conv_f[:, i * dn:(i + 1) * dn] for i in range(3))
    qn = _conv_fwd(proj, o_q, conv_q, hd, float(hd) ** -0.5, "conv_q_fwd")
    kn = _conv_fwd(proj, o_k, conv_k, hd, 1.0, "conv_k_fwd")
    vv = _conv_fwd(proj, o_v, conv_v, hd, None, "conv_v_fwd")
    logit_lanes = (n_heads, LANES - 2 * n_heads)
    prm = jnp.pad(A_log, ((0, 7), logit_lanes)) + jnp.pad(dt_bias, ((1, 6), logit_lanes))
    y_dn, hist = _chunk_fwd(qn, kn, vv, proj, o_zd, n_main, prm, dn_norm_w, n_heads, pad)
    me_idx = jnp.reshape(4 * lax.axis_index("x") + 2 * lax.axis_index("y") + lax.axis_index("c"), (1,)).astype(jnp.int32)
    _, landed = _split_wait(_to_all_copies, late_weights, y_dn, "gather_out_proj_wait")
    wpo_g, wdn_g, wo_g = (_fill_own_block(g, s, me_idx, "own_block_%d" % i) for i, (g, s) in enumerate(zip(landed, late_shards)))
    wpo_f = cols_major(wpo_g)
    wdn_f = wdn_g.reshape(dn, d)
    wo_f = wo_g.reshape(d, d)
    p_out = _matmul(y_pool, wpo_f, NN, F32, 1056, 1024, 1024, "pool_out")
    q_out = _matmul(y_dn, wdn_f, NN, F32, 1056, 1024, 2048, "dn_out")
    merged = _merge_fwd(p_out, q_out, proj, o_gp, o_gd)
    mo = _matmul(merged, wo_f, NN, F32, 1056, 1024, 2048, "w_o_fwd")
    dh1, d_fw, loss_part = _final_loss(h0, mo, final_norm_w.reshape(1, d), loss_target[0], x0)

    d_merged = _matmul(dh1, wo_f, NT, F32, 1056, 1024, 1024, "w_o_bwd_x")
    g_wo = _matmul(merged.T, dh1, NN, BF16, 1024, 1024, lp, "w_o_bwd_w")
    d_p, d_q, d_gp, d_gd = _merge_bwd(p_out, q_out, proj, o_gp, o_gd, d_merged)
    d_ypool = _matmul(d_p, wpo_f, NT, F32, 1056, 1024, 2048, "pool_out_bwd_x")
    g_wpo = _matmul(y_pool.T, d_p, NN, BF16, 1024, 1024, lp, "pool_out_bwd_w", col_blocks=N_DEV)
    d_ydn = _matmul(d_q, wdn_f, NT, F32, 1056, 1024, 2048, "dn_out_bwd_x")
    g_wdn = _matmul(y_dn.T, d_q, NN, BF16, 1024, 1024, lp, "dn_out_bwd_w")
    rs_early = _rs_begin([g_wpo, g_wdn.reshape(N_DEV, dn // N_DEV, d), g_wo.reshape(N_DEV, d // N_DEV, d)], "early", split=True)
    started = rs_early["token"][0, 0]
    d_u, d_zp, g_mix, g_pscale = _pool_bwd(proj, mix_f, pool_scale + started, d_ypool, pad)
    d_qn, d_kn, d_vv, d_ba, d_zd, d_prm, g_dnw = _chunk_bwd(qn, kn, vv, proj, o_zd, n_main, prm + started, dn_norm_w, hist, d_ydn, n_heads, pad)
    d_qr, g_cq = _conv_bwd(proj, o_q, conv_q, d_qn, hd, float(hd) ** -0.5, pad, "conv_q_bwd")
    d_kr, g_ck = _conv_bwd(proj, o_k, conv_k, d_kn, hd, 1.0, pad, "conv_k_bwd")
    d_vr, g_cv = _conv_bwd(proj, o_v, conv_v, d_vv, hd, None, pad, "conv_v_bwd")
    d_proj = jnp.concatenate([d_u, d_zp, d_qr, d_kr, d_vr, d_zd, d_gp, d_gd, d_ba.astype(BF16), jnp.zeros((lp, LANES), BF16)], axis=1)
    xn_t, rs_late, token = xn.T, [], None
    for half in range(2):
        rows = slice(half * (d // 2), (half + 1) * (d // 2))
        g_wall = _matmul(xn_t[rows], d_proj, NN, F32, 1024, 768, lp, "w_in_bwd_w_%d" % half, after=token)
        g_win = _lane_gather(g_wall.reshape(1, d // 2, n_all), _lane_gather_table(to_own, n_all), N_DEV, ns, "w_in_grad_to_own_%d" % half)
        rs_late.append(_rs_begin([g_win], "late%d" % half, split=True))
        token = rs_late[-1]["token"]
    d_xn = _matmul(d_proj, w_all, NT, F32, 1056, 1024, 768, "w_in_bwd_x", after=token)
    d_head, grad_x, g_nw = _norm_in_bwd(h0, norm_w, d_xn, dh1, x0)
    grad_x = grad_x[None]

    by_cols = lambda t: jnp.transpose(t.reshape(t.shape[0], N_DEV, t.shape[1] // N_DEV), (1, 0, 2))
    g_conv = by_cols(jnp.concatenate([g_cq, g_ck, g_cv], axis=1)).reshape(N_DEV, kw * cs // pg, pg)
    conv_rows = -(-g_conv.shape[1] // 16) * 16
    g_small = jnp.concatenate(
        [jnp.transpose(g_mix.reshape(ng, N_DEV, mr, pg), (1, 0, 2, 3)).reshape(N_DEV, ng * mr, pg), by_cols(d_head[pad:x0]),
         jnp.pad(g_conv, ((0, 0), (0, conv_rows - g_conv.shape[1]), (0, 0)))], axis=1).astype(BF16)
    r_small, = _rs_finish(_rs_begin([g_small], "small", split=False), "small")
    r_mix, r_meta = r_small[:ng * mr], r_small[ng * mr:ng * mr + n_meta]
    r_conv = r_small[ng * mr + n_meta:ng * mr + n_meta + kw * cs // pg]
    r_wpo, r_wdn, r_wo = _rs_finish(rs_early, "early", after=r_small)

    small = [g_nw[0], d_fw[0], g_pscale[0], g_dnw[0], d_prm[0], d_prm[1], loss_part[0]]
    s_sizes = [t.shape[0] for t in small]
    s_cols = -(-sum(s_sizes) // (8 * LANES)) * LANES
    s_vec = jnp.concatenate(small + [jnp.zeros((8 * s_cols - sum(s_sizes),), F32)]).reshape(8, s_cols)
    s_red = _all_reduce_small(s_vec)
    s_sum = s_red.reshape(-1)
    r_win = None
    for half, begun in enumerate(rs_late):
        r_win = _rs_finish(begun, "late%d" % half, after=s_red, part=half, n_parts=2, dsts=r_win)
    r_win, = r_win
    s_offs = [sum(s_sizes[:i]) for i in range(len(s_sizes))]
    s_take = lambda i, n=None, o=0: s_sum[s_offs[i] + o:s_offs[i] + o + (s_sizes[i] if n is None else n)]

    grads = {
        "meta_tokens": r_meta, "norm_w": s_take(0).reshape(norm_w.shape),
        "w_in": r_win.reshape(w_in.shape), "conv_w": r_conv.reshape(conv_w.shape),
        "A_log": s_take(4, n_heads, n_heads).reshape(A_log.shape), "dt_bias": s_take(5, n_heads, n_heads).reshape(dt_bias.shape),
        "pool_mix": r_mix.reshape(pool_mix.shape), "pool_scale": s_take(2).reshape(pool_scale.shape),
        "dn_norm_w": s_take(3).reshape(dn_norm_w.shape), "w_pool_out": r_wpo.reshape(w_pool_out.shape),
        "w_dn_out": r_wdn.reshape(w_dn_out.shape), "w_o": r_wo.reshape(w_o.shape),
        "final_norm_w": s_take(1).reshape(final_norm_w.shape),
    }
    loss = s_take(6, 1)[0]

    weights = dict(meta_tokens=meta_tokens, norm_w=norm_w, w_in=w_in, conv_w=conv_w, A_log=A_log, dt_bias=dt_bias,
                   pool_mix=pool_mix, pool_scale=pool_scale, dn_norm_w=dn_norm_w, w_pool_out=w_pool_out, w_dn_out=w_dn_out,
                   w_o=w_o, final_norm_w=final_norm_w)
    m_in = dict(meta_tokens=m_meta_tokens, norm_w=m_norm_w, w_in=m_w_in, conv_w=m_conv_w, A_log=m_A_log, dt_bias=m_dt_bias,
                pool_mix=m_pool_mix, pool_scale=m_pool_scale, dn_norm_w=m_dn_norm_w, w_pool_out=m_w_pool_out,
                w_dn_out=m_w_dn_out, w_o=m_w_o, final_norm_w=m_final_norm_w)
    v_in = dict(meta_tokens=v_meta_tokens, norm_w=v_norm_w, w_in=v_w_in, conv_w=v_conv_w, A_log=v_A_log, dt_bias=v_dt_bias,
                pool_mix=v_pool_mix, pool_scale=v_pool_scale, dn_norm_w=v_dn_norm_w, w_pool_out=v_w_pool_out,
                w_dn_out=v_w_dn_out, w_o=v_w_o, final_norm_w=v_final_norm_w)
    names = list(weights)
    upd = {n: _adamw(weights[n], grads[n], m_in[n], v_in[n], "adamw_" + n) for n in names}
    return (loss, grad_x, *[grads[n] for n in names], *[upd[n][0] for n in names], *[upd[n][1] for n in names],
            *[upd[n][2] for n in names])
```

```python
import functools

import jax
import jax.numpy as jnp
import numpy as np
from jax import lax
from jax.experimental import pallas as pl
from jax.experimental.pallas import tpu as pltpu

F32 = jnp.float32
BF16 = jnp.bfloat16
HIGHEST = lax.Precision.HIGHEST
MESH = pl.DeviceIdType.MESH

CHUNK = 64
NORM_EPS = 1e-6
POOL_WINDOWS = (2, 4, 8, 16)
ADAM_LR, ADAM_B1, ADAM_B2, ADAM_EPS, ADAM_WD, ADAM_STEP = 0.001, 0.9, 0.999, 1e-08, 0.01, 10
N_DEV = 8
LANES = 128
VMEM_LIMIT = 48 * 1024 * 1024

NN = (((1,), (0,)), ((), ()))
NT = (((1,), (1,)), ((), ()))
TN = (((0,), (0,)), ((), ()))


def _call(body, **kw):
    return pl.pallas_call(body, **kw)


def _params(sem=None):
    return pltpu.CompilerParams(dimension_semantics=sem, vmem_limit_bytes=VMEM_LIMIT)


def _tile(n, pref, align):
    for d in range(min(pref, n), 0, -1):
        if n % d == 0 and d % align == 0:
            return d
    return n


def _dot(a, b, dims=NN, precision=None):
    return lax.dot_general(a, b, dims, precision=precision, preferred_element_type=F32)


def _sigmoid(x):
    return 1.0 / (1.0 + jnp.exp(-x))


def _silu(x):
    return x * _sigmoid(x)


def _softplus(x):
    return jnp.maximum(x, 0.0) + jnp.log(1.0 + jnp.exp(-jnp.abs(x)))


def _rmsnorm(x, w):
    return x * lax.rsqrt(jnp.mean(x * x, axis=-1, keepdims=True) + NORM_EPS) * w


def _shift_down(x, j, row):
    if j == 0:
        return x
    return jnp.where(row >= j, pltpu.roll(x, j, 0), 0.0)


def _shift_up(x, j, row):
    if j == 0:
        return x
    n = x.shape[0]
    return jnp.where(row < n - j, pltpu.roll(x, n - j, 0), 0.0)


def _matmul(a, b, dims, out_dtype, tm, tn, tk, name, col_blocks=None, after=None):
    ta = dims == TN
    tb = dims == NT
    m, kdim = (a.shape[1], a.shape[0]) if ta else a.shape
    n = b.shape[0] if tb else b.shape[1]
    if col_blocks:
        tn = n // col_blocks
    tm, tn, tk = _tile(m, tm, 8), _tile(n, tn, LANES), _tile(kdim, tk, LANES if not ta else 16)
    nk = kdim // tk

    n_extra = 0 if after is None else 1

    def body(a_ref, b_ref, *refs):
        o_ref, scratch = refs[n_extra], refs[n_extra + 1:]
        part = _dot(a_ref[...].astype(BF16), b_ref[...].astype(BF16), dims)
        if nk == 1:
            o_ref[...] = part.astype(o_ref.dtype).reshape(o_ref.shape)
            return
        acc_ref, = scratch
        k = pl.program_id(2)

        @pl.when(k == 0)
        def _():
            acc_ref[...] = part

        @pl.when(k > 0)
        def _():
            acc_ref[...] += part

        @pl.when(k == nk - 1)
        def _():
            o_ref[...] = acc_ref[...].astype(o_ref.dtype).reshape(o_ref.shape)

    a_spec = pl.BlockSpec((tk, tm), lambda i, j, k: (k, i)) if ta else pl.BlockSpec((tm, tk), lambda i, j, k: (i, k))
    b_spec = pl.BlockSpec((tn, tk), lambda i, j, k: (j, k)) if tb else pl.BlockSpec((tk, tn), lambda i, j, k: (k, j))
    if col_blocks:
        out_spec = pl.BlockSpec((1, tm, tn), lambda i, j, k: (j, i, 0))
        out_shape = jax.ShapeDtypeStruct((col_blocks, m, tn), out_dtype)
    else:
        out_spec = pl.BlockSpec((tm, tn), lambda i, j, k: (i, j))
        out_shape = jax.ShapeDtypeStruct((m, n), out_dtype)
    return _call(
        body, name=name, grid=(m // tm, n // tn, nk),
        in_specs=[a_spec, b_spec] + [ANY] * n_extra, out_specs=out_spec, out_shape=out_shape,
        scratch_shapes=[] if nk == 1 else [pltpu.VMEM((tm, tn), F32)],
        compiler_params=_params(("parallel", "parallel", "arbitrary")),
    )(a, b, *([] if after is None else [after]))


def _norm_in_fwd(x2d, meta, w, pad):
    seq, d = x2d.shape
    tr = pad + meta.shape[0]
    assert seq % tr == 0 and tr % 16 == 0
    lp = tr + seq

    def body(x_ref, m_ref, w_ref, h_ref, o_ref):
        def emit(h):
            h_ref[...] = h
            o_ref[...] = _rmsnorm(h, w_ref[...]).astype(BF16)

        @pl.when(pl.program_id(0) == 0)
        def _():
            emit(jnp.concatenate([jnp.zeros((pad, d), F32), m_ref[...]], axis=0) if pad else m_ref[...])

        @pl.when(pl.program_id(0) > 0)
        def _():
            emit(x_ref[...])

    row = pl.BlockSpec((tr, d), lambda i: (i, 0))
    return _call(
        body, name="norm_in_fwd", grid=(lp // tr,),
        in_specs=[pl.BlockSpec((tr, d), lambda i: (jnp.maximum(i - 1, 0), 0)), pl.BlockSpec(meta.shape, lambda i: (0, 0)),
                  pl.BlockSpec((1, d), lambda i: (0, 0))],
        out_specs=[row, row],
        out_shape=[jax.ShapeDtypeStruct((lp, d), F32), jax.ShapeDtypeStruct((lp, d), BF16)],
        compiler_params=_params(("arbitrary",)),
    )(x2d, meta, w)


def _norm_in_bwd(h0, w, dxn, dh1, x0):
    lp, d = h0.shape
    tr = x0
    assert lp % tr == 0

    def body(h_ref, w_ref, da_ref, dh1_ref, head_ref, gx_ref, dw_ref):
        i = pl.program_id(0)
        _, vjp = jax.vjp(_rmsnorm, h_ref[...], w_ref[...])
        dh, dw = vjp(da_ref[...])
        dh = dh + dh1_ref[...]

        @pl.when(i == 0)
        def _():
            head_ref[...] = dh
            dw_ref[...] = dw

        @pl.when(i > 0)
        def _():
            gx_ref[...] = dh
            dw_ref[...] += dw

    row = pl.BlockSpec((tr, d), lambda i: (i, 0))
    vec = pl.BlockSpec((1, d), lambda i: (0, 0))
    return _call(
        body, name="norm_in_bwd", grid=(lp // tr,),
        in_specs=[row, vec, row, row],
        out_specs=[pl.BlockSpec((tr, d), lambda i: (0, 0)), pl.BlockSpec((tr, d), lambda i: (jnp.maximum(i - 1, 0), 0)), vec],
        out_shape=[jax.ShapeDtypeStruct((tr, d), F32), jax.ShapeDtypeStruct((lp - tr, d), F32), jax.ShapeDtypeStruct((1, d), F32)],
        compiler_params=_params(("arbitrary",)),
    )(h0, w, dxn, dh1)


def _final_loss(h0, mo, fw, tgt, x0):
    lp, d = h0.shape
    tr = x0
    assert lp % tr == 0

    def body(h_ref, mo_ref, fw_ref, t_ref, dh_ref, dw_ref, loss_ref):
        i = pl.program_id(0)
        row = i * tr + lax.broadcasted_iota(jnp.int32, (tr, 1), 0)
        mask = jnp.where(row >= x0, 1.0, 0.0).astype(F32)
        tgt_v = t_ref[...]

        def loss_fn(h1, w):
            err = _rmsnorm(h1, w) - tgt_v
            return 0.5 * jnp.sum(jnp.mean(err * err, axis=-1, keepdims=True) * mask, axis=0, keepdims=True)

        loss, vjp = jax.vjp(loss_fn, h_ref[...] + mo_ref[...], fw_ref[...])
        dh, dw = vjp(jnp.ones((1, 1), F32))
        dh_ref[...] = dh

        @pl.when(i == 0)
        def _():
            dw_ref[...] = jnp.zeros_like(dw_ref)
            loss_ref[...] = jnp.zeros_like(loss_ref)

        dw_ref[...] += dw
        loss_ref[...] += jnp.broadcast_to(loss, loss_ref.shape)

    row_spec = pl.BlockSpec((tr, d), lambda i: (i, 0))
    vec = pl.BlockSpec((1, d), lambda i: (0, 0))
    return _call(
        body, name="final_loss", grid=(lp // tr,),
        in_specs=[row_spec, row_spec, vec, pl.BlockSpec((tr, d), lambda i: (jnp.maximum(i - 1, 0), 0))],
        out_specs=[row_spec, vec, pl.BlockSpec((8, LANES), lambda i: (0, 0))],
        out_shape=[jax.ShapeDtypeStruct((lp, d), F32), jax.ShapeDtypeStruct((1, d), F32), jax.ShapeDtypeStruct((8, LANES), F32)],
        compiler_params=_params(("arbitrary",)),
    )(h0, mo, fw, tgt)


def _pool_select(parts, g):
    out = parts[-1]
    for gi in range(len(parts) - 2, -1, -1):
        out = jnp.where(g == gi, parts[gi], out)
    return out


def _pool_count(row, g, pad):
    win = _pool_select([jnp.full(row.shape, float(w), F32) for w in POOL_WINDOWS], g)
    return jnp.maximum(jnp.minimum((row - pad + 1).astype(F32), win), 1.0)


def _pooled(u, g, row, pad):
    sums, s, span = [], u, 1
    for w in POOL_WINDOWS:
        while span < w:
            s = s + _shift_down(s, span, row)
            span *= 2
        sums.append(s)
    return _pool_select(sums, g) / _pool_count(row, g, pad) - u


def _pooled_adjoint(dp, g, row, pad):
    e = dp / _pool_count(row, g, pad)
    sums, s, span = [], e, 1
    for w in POOL_WINDOWS:
        while span < w:
            s = s + _shift_up(s, span, row)
            span *= 2
        sums.append(s)
    return _pool_select(sums, g) - dp


def _pool_specs(lp, pg, ng, z_off):
    u_spec = pl.BlockSpec((lp, pg), lambda g: (0, g))
    z_spec = pl.BlockSpec((lp, pg), lambda g: (0, z_off + g))
    mix_spec = pl.BlockSpec((1, pg, pg), lambda g: (g, 0, 0))
    vec_spec = pl.BlockSpec((1, pg), lambda g: (0, g))
    return u_spec, z_spec, mix_spec, vec_spec


def _pool_fwd(proj, mix, scale, pad):
    lp = proj.shape[0]
    ng, pg, _ = mix.shape
    pw = ng * pg

    def body(u_ref, z_ref, mix_ref, sc_ref, y_ref):
        g = pl.program_id(0)
        row = lax.broadcasted_iota(jnp.int32, (lp, 1), 0)
        pooled = _pooled(u_ref[...], g, row, pad)
        mixed = _dot(pooled.astype(BF16), mix_ref[0])
        y_ref[...] = (mixed * sc_ref[...] * _silu(z_ref[...])).astype(BF16)

    u_spec, z_spec, mix_spec, vec_spec = _pool_specs(lp, pg, ng, pw // pg)
    return _call(
        body, name="pool_fwd", grid=(ng,), in_specs=[u_spec, z_spec, mix_spec, vec_spec], out_specs=u_spec,
        out_shape=jax.ShapeDtypeStruct((lp, pw), BF16), compiler_params=_params(("parallel",)),
    )(proj, proj, mix, scale)


def _pool_bwd(proj, mix, scale, dy, pad):
    lp = proj.shape[0]
    ng, pg, _ = mix.shape
    pw = ng * pg

    def body(u_ref, z_ref, mix_ref, sc_ref, dy_ref, du_ref, dz_ref, dmix_ref, dsc_ref):
        g = pl.program_id(0)
        row = lax.broadcasted_iota(jnp.int32, (lp, 1), 0)
        real = row >= pad
        z = z_ref[...]
        pooled = _pooled(u_ref[...], g, row, pad).astype(BF16)
        mixed = _dot(pooled, mix_ref[0])
        sig = _sigmoid(z)
        sz = z * sig
        dyv = dy_ref[...]
        dsc_ref[...] = jnp.sum(dyv * mixed * sz, axis=0, keepdims=True)
        d_sz = dyv * mixed * sc_ref[...]
        dz_ref[...] = jnp.where(real, d_sz * (sig + sz * (1.0 - sig)), 0.0).astype(BF16)
        d_mixed = (dyv * sc_ref[...] * sz).astype(BF16)
        dmix_ref[0] = _dot(pooled, d_mixed, TN)
        d_pooled = _dot(d_mixed, mix_ref[0], NT)
        du_ref[...] = jnp.where(real, _pooled_adjoint(d_pooled, g, row, pad), 0.0).astype(BF16)

    u_spec, z_spec, mix_spec, vec_spec = _pool_specs(lp, pg, ng, pw // pg)
    return _call(
        body, name="pool_bwd", grid=(ng,),
        in_specs=[u_spec, z_spec, mix_spec, vec_spec, u_spec], out_specs=[u_spec, u_spec, mix_spec, vec_spec],
        out_shape=[jax.ShapeDtypeStruct((lp, pw), BF16), jax.ShapeDtypeStruct((lp, pw), BF16),
                   jax.ShapeDtypeStruct((ng, pg, pg), F32), jax.ShapeDtypeStruct((1, pw), F32)],
        compiler_params=_params(("parallel",)),
    )(proj, proj, mix, scale, dy)


def _conv_pre(x, w, row):
    kw = w.shape[0]
    y = w[kw - 1:kw, :] * x
    for kk in range(kw - 1):
        y = y + w[kk:kk + 1, :] * _shift_down(x, kw - 1 - kk, row)
    return y


def _conv_post(y, out_scale):
    s = _silu(y)
    if out_scale is None:
        return s
    return s * lax.rsqrt(jnp.sum(s * s, axis=-1, keepdims=True) + NORM_EPS) * out_scale


def _conv_fwd(proj, col_off, w, hd, out_scale, name):
    lp = proj.shape[0]
    kw, width = w.shape
    blk0 = col_off // hd

    def body(x_ref, w_ref, o_ref):
        row = lax.broadcasted_iota(jnp.int32, (lp, 1), 0)
        o_ref[...] = _conv_post(_conv_pre(x_ref[...], w_ref[...], row), out_scale)

    return _call(
        body, name=name, grid=(width // hd,),
        in_specs=[pl.BlockSpec((lp, hd), lambda j: (0, blk0 + j)), pl.BlockSpec((kw, hd), lambda j: (0, j))],
        out_specs=pl.BlockSpec((lp, hd), lambda j: (0, j)),
        out_shape=jax.ShapeDtypeStruct((lp, width), F32), compiler_params=_params(("parallel",)),
    )(proj, w)


def _conv_bwd(proj, col_off, w, d_out, hd, out_scale, pad, name):
    lp = proj.shape[0]
    kw, width = w.shape
    blk0 = col_off // hd

    def body(x_ref, w_ref, do_ref, dx_ref, dw_ref):
        row = lax.broadcasted_iota(jnp.int32, (lp, 1), 0)
        real = row >= pad
        x, wv = x_ref[...], w_ref[...]
        _, vjp = jax.vjp(functools.partial(_conv_post, out_scale=out_scale), _conv_pre(x, wv, row))
        dy = jnp.where(real, vjp(do_ref[...])[0], 0.0)
        dx = wv[kw - 1:kw, :] * dy
        dw_ref[kw - 1:kw, :] = jnp.sum(dy * x, axis=0, keepdims=True)
        for kk in range(kw - 1):
            j = kw - 1 - kk
            dx = dx + wv[kk:kk + 1, :] * _shift_up(dy, j, row)
            dw_ref[kk:kk + 1, :] = jnp.sum(dy * _shift_down(x, j, row), axis=0, keepdims=True)
        dx_ref[...] = jnp.where(real, dx, 0.0).astype(BF16)

    col = pl.BlockSpec((lp, hd), lambda j: (0, j))
    wspec = pl.BlockSpec((kw, hd), lambda j: (0, j))
    return _call(
        body, name=name, grid=(width // hd,),
        in_specs=[pl.BlockSpec((lp, hd), lambda j: (0, blk0 + j)), wspec, col], out_specs=[col, wspec],
        out_shape=[jax.ShapeDtypeStruct((lp, width), BF16), jax.ShapeDtypeStruct((kw, width), F32)],
        compiler_params=_params(("parallel",)),
    )(proj, w, d_out)


HEADS_PER_STEP = 16


def _each(fn, *lists):
    return [fn(*args) for args in zip(*lists)]


def _dot3_each(a_list, b_list, dims=NN):
    hi = lambda t: t.astype(BF16)
    lo = lambda t, t_hi: (t - t_hi.astype(F32)).astype(BF16)
    dot = lambda x, y: _dot(x, y, dims)
    a_hi, b_hi = _each(hi, a_list), _each(hi, b_list)
    a_lo, b_lo = _each(lo, a_list, a_hi), _each(lo, b_list, b_hi)
    hh, hl, lh = _each(dot, a_hi, b_hi), _each(dot, a_hi, b_lo), _each(dot, a_lo, b_hi)
    return _each(lambda x, y, w: x + (y + w), hh, hl, lh)


@jax.custom_vjp
def _unit_lower_inverse(lmats):
    c = lmats[0].shape[0]
    eye = lax.broadcasted_iota(jnp.int32, (c, c), 0) == lax.broadcasted_iota(jnp.int32, (c, c), 1)
    a = [-m for m in lmats]
    tmat = [jnp.where(eye, 1.0, 0.0).astype(F32) + m for m in a]
    span = 2
    while span < c:
        a = _dot3_each(a, a)
        tmat = _each(lambda t, u: t + u, tmat, _dot3_each(tmat, a))
        span *= 2
    return tuple(tmat)


def _unit_lower_inverse_fwd(lmats):
    tmats = _unit_lower_inverse(lmats)
    return tmats, tmats


def _unit_lower_inverse_bwd(tmats, cts):
    left = _each(lambda t, ct: _dot(t, ct, TN, HIGHEST), tmats, cts)
    return (tuple(_each(lambda m, t: -_dot(m, t, NT, HIGHEST), left, tmats)),)


_unit_lower_inverse.defvjp(_unit_lower_inverse_fwd, _unit_lower_inverse_bwd)


@jax.custom_vjp
def _known_inverse(lmats, tmats):
    return tmats


def _known_inverse_fwd(lmats, tmats):
    return tmats, tmats


def _known_inverse_bwd(tmats, cts):
    return _unit_lower_inverse_bwd(tmats, cts)[0], tuple(jnp.zeros_like(t) for t in tmats)


_known_inverse.defvjp(_known_inverse_fwd, _known_inverse_bwd)


def _chunk_math(states, q, k, v, ba, z, prm, nw, head0, rowmask, n_heads, tmats=None, keep_tmats=False):
    c = q.shape[0]
    heads = list(range(len(states)))
    hd = q.shape[1] // len(states)
    lane = lax.broadcasted_iota(jnp.int32, ba.shape, 1)
    sub = lax.broadcasted_iota(jnp.int32, (ba.shape[1], c), 0)
    ri = lax.broadcasted_iota(jnp.int32, (c, c), 0)
    ci = lax.broadcasted_iota(jnp.int32, (c, c), 1)
    last = lax.broadcasted_iota(jnp.int32, (c, 1), 0) == c - 1
    causal, strict = ri >= ci, ri > ci
    beta_all = _sigmoid(ba) * rowmask
    g_all = -jnp.exp(prm[0:1, :]) * _softplus(ba + prm[1:2, :]) * rowmask
    gcum_all = _dot(jnp.where(causal, 1.0, 0.0).astype(F32), g_all, precision=HIGHEST)
    gcum_t = gcum_all.T
    split = lambda t: [t[:, j * hd:(j + 1) * hd] for j in heads]
    qs, ks, vs, zs = split(q), split(k), split(v), split(z)
    beta = [jnp.sum(jnp.where(lane == head0 + j, beta_all, 0.0), axis=1, keepdims=True) for j in heads]
    gcum = [jnp.sum(jnp.where(lane == n_heads + head0 + j, gcum_all, 0.0), axis=1, keepdims=True) for j in heads]
    grow = [jnp.sum(jnp.where(sub == n_heads + head0 + j, gcum_t, 0.0), axis=0, keepdims=True) for j in heads]
    glast = _each(lambda gc: jnp.sum(jnp.where(last, gc, 0.0), axis=0, keepdims=True), gcum)
    decay = _each(lambda gc, gr: jnp.where(causal, jnp.exp(jnp.where(causal, gc - gr, 0.0)), 0.0), gcum, grow)
    eg = _each(jnp.exp, gcum)
    k_beta = _each(jnp.multiply, ks, beta)
    kk = _each(lambda a, b: _dot(a, b, NT), k_beta, ks)
    lmats = tuple(_each(lambda m, dc: jnp.where(strict, m * dc, 0.0), kk, decay))
    tmat = list(_unit_lower_inverse(lmats) if tmats is None else _known_inverse(lmats, tuple(tmats)))
    u_c = _each(_dot, tmat, _each(jnp.multiply, vs, beta))
    w_c = _each(_dot, tmat, _each(jnp.multiply, k_beta, eg))
    qk = _each(lambda a, b, dc: jnp.where(causal, _dot(a, b, NT) * dc, 0.0), qs, ks, decay)
    v_new = _each(lambda u, w, s: u - _dot(w, s), u_c, w_c, list(states))
    o = _each(lambda a, e, s, m, vn: _dot(a * e, s) + _dot(m, vn), qs, eg, list(states), qk, v_new)
    k_dec = _each(lambda a, gl, gc: a * jnp.exp(gl - gc), ks, glast, gcum)
    new_states = _each(lambda s, gl, kd, vn: s * jnp.exp(gl) + _dot(kd, vn, TN), list(states), glast, k_dec, v_new)
    ys = _each(lambda oj, zj: _rmsnorm(oj, nw) * _silu(zj), o, zs)
    if keep_tmats:
        return jnp.concatenate(ys, axis=1), tuple(new_states), tuple(tmat)
    return jnp.concatenate(ys, axis=1), tuple(new_states)


def _chunk_specs(nc, hd, n_heads, z_off, ba_off, rev):
    cidx = (lambda c: nc - 1 - c) if rev else (lambda c: c)
    hb = min(HEADS_PER_STEP, n_heads)
    assert n_heads % hb == 0 and z_off % (hb * hd) == 0 and ba_off % LANES == 0
    blk = lambda off: pl.BlockSpec((CHUNK, hb * hd), lambda c, g: (cidx(c), off + g))
    ba_spec = lambda off: pl.BlockSpec((CHUNK, LANES), lambda c, g: (cidx(c), off // LANES))
    prm_spec = pl.BlockSpec((8, LANES), lambda c, g: (0, 0))
    nw_spec = pl.BlockSpec((1, hd), lambda c, g: (0, 0))
    st_spec = pl.BlockSpec((1, hb, hd, hd), lambda c, g: (cidx(c), g, 0, 0))
    return blk, ba_spec, prm_spec, nw_spec, st_spec, blk(z_off // (hb * hd))


def _rowmask(chunk_idx, pad):
    row = chunk_idx * CHUNK + lax.broadcasted_iota(jnp.int32, (CHUNK, 1), 0)
    return jnp.where(row >= pad, 1.0, 0.0).astype(F32)


def _chunk_fwd(qn, kn, vv, proj, z_off, ba_off, prm, nw, n_heads, pad):
    lp, dn = qn.shape
    hd = dn // n_heads
    nc = lp // CHUNK
    hb = min(HEADS_PER_STEP, n_heads)

    def body(q_ref, k_ref, v_ref, ba_ref, z_ref, prm_ref, nw_ref, y_ref, hist_ref, tm_ref, st_ref):
        c, g = pl.program_id(0), pl.program_id(1)

        @pl.when(c == 0)
        def _():
            for j in range(hb):
                st_ref[g * hb + j] = jnp.zeros((hd, hd), F32)

        states = tuple(st_ref[g * hb + j] for j in range(hb))
        for j in range(hb):
            hist_ref[0, j] = states[j]
        y, new_states, tmats = _chunk_math(states, q_ref[...], k_ref[...], v_ref[...], ba_ref[...], z_ref[...], prm_ref[...],
                                           nw_ref[...], g * hb, _rowmask(c, pad), n_heads, keep_tmats=True)
        y_ref[...] = y.astype(BF16)
        for j in range(hb):
            st_ref[g * hb + j] = new_states[j]
            tm_ref[0, j] = tmats[j]

    blk, ba_spec, prm_spec, nw_spec, st_spec, z_spec = _chunk_specs(nc, hd, n_heads, z_off, ba_off, False)
    tm_spec = pl.BlockSpec((1, hb, CHUNK, CHUNK), lambda c, g: (c, g, 0, 0))
    return _call(
        body, name="chunk_fwd", grid=(nc, n_heads // hb),
        in_specs=[blk(0), blk(0), blk(0), ba_spec(ba_off), z_spec, prm_spec, nw_spec], out_specs=[blk(0), st_spec, tm_spec],
        out_shape=[jax.ShapeDtypeStruct((lp, dn), BF16), jax.ShapeDtypeStruct((nc, n_heads, hd, hd), F32),
                   jax.ShapeDtypeStruct((nc, n_heads, CHUNK, CHUNK), F32)],
        scratch_shapes=[pltpu.VMEM((n_heads, hd, hd), F32)],
        compiler_params=_params(("arbitrary", "arbitrary")),
    )(qn, kn, vv, proj, proj, prm, nw)


def _chunk_bwd(qn, kn, vv, proj, z_off, ba_off, prm, nw, hist, tmats, dy, n_heads, pad):
    lp, dn = qn.shape
    hd = dn // n_heads
    nc = lp // CHUNK
    hb = min(HEADS_PER_STEP, n_heads)

    def body(q_ref, k_ref, v_ref, ba_ref, z_ref, prm_ref, nw_ref, hist_ref, tm_ref, dy_ref,
             dq_ref, dk_ref, dv_ref, dba_ref, dz_ref, dprm_ref, dnw_ref, dst_ref):
        step, g = pl.program_id(0), pl.program_id(1)

        @pl.when(step == 0)
        def _():
            for j in range(hb):
                dst_ref[g * hb + j] = jnp.zeros((hd, hd), F32)

        @pl.when((step == 0) & (g == 0))
        def _():
            dprm_ref[...] = jnp.zeros_like(dprm_ref)
            dnw_ref[...] = jnp.zeros_like(dnw_ref)

        @pl.when(g == 0)
        def _():
            dba_ref[...] = jnp.zeros_like(dba_ref)

        def fn(states, q, k, v, ba, z, prm_v, nw_v, known):
            return _chunk_math(states, q, k, v, ba, z, prm_v, nw_v, g * hb, _rowmask(nc - 1 - step, pad), n_heads, tmats=known)

        states = tuple(hist_ref[0, j] for j in range(hb))
        known = tuple(tm_ref[0, j] for j in range(hb))
        _, vjp = jax.vjp(fn, states, q_ref[...], k_ref[...], v_ref[...], ba_ref[...], z_ref[...], prm_ref[...], nw_ref[...], known)
        dst, dq, dk, dv, dba, dz, dprm, dnw, _ = vjp((dy_ref[...], tuple(dst_ref[g * hb + j] for j in range(hb))))
        for j in range(hb):
            dst_ref[g * hb + j] = dst[j]
        dq_ref[...] = dq
        dk_ref[...] = dk
        dv_ref[...] = dv
        dz_ref[...] = dz.astype(BF16)
        dba_ref[...] += dba
        dprm_ref[...] += dprm
        dnw_ref[...] += dnw

    blk, ba_spec, prm_spec, nw_spec, st_spec, z_spec = _chunk_specs(nc, hd, n_heads, z_off, ba_off, True)
    f32_full = jax.ShapeDtypeStruct((lp, dn), F32)
    tm_spec = pl.BlockSpec((1, hb, CHUNK, CHUNK), lambda c, g: (nc - 1 - c, g, 0, 0))
    return _call(
        body, name="chunk_bwd", grid=(nc, n_heads // hb),
        in_specs=[blk(0), blk(0), blk(0), ba_spec(ba_off), z_spec, prm_spec, nw_spec, st_spec, tm_spec, blk(0)],
        out_specs=[blk(0), blk(0), blk(0), ba_spec(0), blk(0), prm_spec, nw_spec],
        out_shape=[f32_full, f32_full, f32_full, jax.ShapeDtypeStruct((lp, LANES), F32), jax.ShapeDtypeStruct((lp, dn), BF16),
                   jax.ShapeDtypeStruct((8, LANES), F32), jax.ShapeDtypeStruct((1, hd), F32)],
        scratch_shapes=[pltpu.VMEM((n_heads, hd, hd), F32)],
        compiler_params=_params(("arbitrary", "arbitrary")),
    )(qn, kn, vv, proj, proj, prm, nw, hist, tmats, dy)


def _merge_math(p, q, gp, gd):
    return _sigmoid(gp) * p + _sigmoid(gd) * q


def _merge_specs(lp, d, gp_off, gd_off):
    tr, tc = _tile(lp, 264, 16), _tile(d, 1024, LANES)
    blk = pl.BlockSpec((tr, tc), lambda i, j: (i, j))
    gp_spec = pl.BlockSpec((tr, tc), lambda i, j: (i, gp_off // tc + j))
    gd_spec = pl.BlockSpec((tr, tc), lambda i, j: (i, gd_off // tc + j))
    return (lp // tr, d // tc), blk, gp_spec, gd_spec


def _merge_fwd(p, q, proj, gp_off, gd_off):
    lp, d = p.shape
    grid, blk, gp_spec, gd_spec = _merge_specs(lp, d, gp_off, gd_off)

    def body(p_ref, q_ref, gp_ref, gd_ref, o_ref):
        o_ref[...] = _merge_math(p_ref[...], q_ref[...], gp_ref[...], gd_ref[...]).astype(BF16)

    return _call(
        body, name="merge_fwd", grid=grid, in_specs=[blk, blk, gp_spec, gd_spec], out_specs=blk,
        out_shape=jax.ShapeDtypeStruct((lp, d), BF16), compiler_params=_params(("parallel", "parallel")),
    )(p, q, proj, proj)


def _merge_bwd(p, q, proj, gp_off, gd_off, dm):
    lp, d = p.shape
    grid, blk, gp_spec, gd_spec = _merge_specs(lp, d, gp_off, gd_off)

    def body(p_ref, q_ref, gp_ref, gd_ref, dm_ref, dp_ref, dq_ref, dgp_ref, dgd_ref):
        _, vjp = jax.vjp(_merge_math, p_ref[...], q_ref[...], gp_ref[...], gd_ref[...])
        for ref, val in zip((dp_ref, dq_ref, dgp_ref, dgd_ref), vjp(dm_ref[...])):
            ref[...] = val.astype(BF16)

    out = jax.ShapeDtypeStruct((lp, d), BF16)
    return _call(
        body, name="merge_bwd", grid=grid, in_specs=[blk, blk, gp_spec, gd_spec, blk], out_specs=[blk] * 4,
        out_shape=[out] * 4, compiler_params=_params(("parallel", "parallel")),
    )(p, q, proj, proj, dm)


def _adamw(w, g, m, v, name):
    shape = w.shape
    w2, g2, m2, v2 = (t.reshape((-1, shape[-1])) for t in (w, g, m, v))
    rows, cols = w2.shape
    tr = _tile(rows, 128, 8)

    def body(w_ref, g_ref, m_ref, v_ref, d_ref, nm_ref, nv_ref):
        gv = g_ref[...]
        nm = ADAM_B1 * m_ref[...] + (1.0 - ADAM_B1) * gv
        nv = ADAM_B2 * v_ref[...] + (1.0 - ADAM_B2) * (gv * gv)
        m_hat = nm / (1.0 - ADAM_B1 ** ADAM_STEP)
        v_hat = nv / (1.0 - ADAM_B2 ** ADAM_STEP)
        d_ref[...] = -ADAM_LR * (m_hat / (jnp.sqrt(v_hat) + ADAM_EPS) + ADAM_WD * w_ref[...])
        nm_ref[...] = nm
        nv_ref[...] = nv

    blk = pl.BlockSpec((tr, cols), lambda i: (i, 0))
    out = jax.ShapeDtypeStruct((rows, cols), F32)
    res = _call(
        body, name=name, grid=(rows // tr,), in_specs=[blk] * 4, out_specs=[blk] * 3, out_shape=[out] * 3,
        compiler_params=_params(("parallel",)),
    )(w2, g2, m2, v2)
    return tuple(t.reshape(shape) for t in res)


def _coords():
    return lax.axis_index("x"), lax.axis_index("y"), lax.axis_index("c")


def _flip(v, bit):
    return 1 - v if bit else v


CHIP_FLIPS = ((1, 0), (0, 1), (1, 1))
ANY = pl.BlockSpec(memory_space=pl.ANY)


def _all_gather(shards):
    n = len(shards)

    def body(*refs):
        x_refs, out_refs = refs[:n], refs[n:2 * n]
        send_sems, recv_sems, local_sems = refs[2 * n:]
        x, y, c = _coords()
        sibling = (x, y, 1 - c)
        chips = [(_flip(x, fx), _flip(y, fy)) for fx, fy in CHIP_FLIPS]

        def copy(a, k, block, to, from_input=False):
            px, py, pc = block
            slot = out_refs[a].at[4 * px + 2 * py + pc]
            return pltpu.make_async_remote_copy(
                src_ref=x_refs[a] if from_input else slot, dst_ref=slot,
                send_sem=send_sems.at[7 * a + k], recv_sem=recv_sems.at[7 * a + k], device_id=to, device_id_type=MESH)

        mine = [pltpu.make_async_copy(x_refs[a], out_refs[a].at[4 * x + 2 * y + c], local_sems.at[a]) for a in range(n)]
        first = []
        for a in range(n):
            mine[a].start()
            first.append(copy(a, 0, (x, y, c), sibling, True))
            first += [copy(a, 1 + j, (x, y, c), (*chip, c), True) for j, chip in enumerate(chips)]
        for cp in first:
            cp.start()
        passed = []
        for j, chip in enumerate(chips):
            for a in range(n):
                copy(a, 1 + j, (*chip, c), (x, y, c)).wait_recv()
                passed.append(copy(a, 4 + j, (*chip, c), sibling))
                passed[-1].start()
        for a in range(n):
            copy(a, 0, (x, y, 1 - c), (x, y, c)).wait_recv()
            for j, chip in enumerate(chips):
                copy(a, 4 + j, (*chip, 1 - c), (x, y, c)).wait_recv()
        for cp in first + passed:
            cp.wait_send()
        for cp in mine:
            cp.wait()

    return _call(
        body, name="all_gather", in_specs=[ANY] * n, out_specs=[ANY] * n,
        out_shape=[jax.ShapeDtypeStruct((N_DEV,) + s.shape, s.dtype) for s in shards],
        scratch_shapes=[pltpu.SemaphoreType.DMA((7 * n,)), pltpu.SemaphoreType.DMA((7 * n,)), pltpu.SemaphoreType.DMA((n,))],
    )(*shards)


def _rs_to_sibling(gs, name):
    n = len(gs)

    def body(*refs):
        g_refs, got_refs = refs[:n], refs[n:2 * n]
        send_sems, recv_sems = refs[2 * n:]
        x, y, c = _coords()
        copies = []
        for a in range(n):
            for p in range(4):
                cp = pltpu.make_async_remote_copy(
                    src_ref=g_refs[a].at[2 * p + (1 - c)], dst_ref=got_refs[a].at[p], send_sem=send_sems.at[4 * a + p],
                    recv_sem=recv_sems.at[4 * a + p], device_id=(x, y, 1 - c), device_id_type=MESH)
                cp.start()
                copies.append(cp)
        for cp in copies:
            cp.wait()

    return _call(
        body, name=name, in_specs=[ANY] * n, out_specs=[ANY] * n,
        out_shape=[jax.ShapeDtypeStruct((4,) + g.shape[1:], g.dtype) for g in gs],
        scratch_shapes=[pltpu.SemaphoreType.DMA((4 * n,)), pltpu.SemaphoreType.DMA((4 * n,))],
    )(*gs)


def _rs_pair_sum(g, got, c_idx, name):
    _, rows, cols = g.shape
    tr = _tile(rows, 256, 16)

    def body(c_ref, g_ref, got_ref, o_ref):
        o_ref[...] = (g_ref[...].astype(F32) + got_ref[...].astype(F32)).astype(o_ref.dtype)

    grid_spec = pltpu.PrefetchScalarGridSpec(
        num_scalar_prefetch=1, grid=(4, rows // tr),
        in_specs=[pl.BlockSpec((1, tr, cols), lambda p, i, c_ref: (2 * p + c_ref[0], i, 0)),
                  pl.BlockSpec((1, tr, cols), lambda p, i, c_ref: (p, i, 0))],
        out_specs=pl.BlockSpec((1, tr, cols), lambda p, i, c_ref: (p, i, 0)))
    return _call(
        body, name=name, grid_spec=grid_spec, out_shape=jax.ShapeDtypeStruct((4, rows, cols), g.dtype),
        compiler_params=_params(("parallel", "parallel")),
    )(c_idx, g, got)


def _to_chips_copies(p_refs, got_refs, send_sems, recv_sems):
    x, y, c = _coords()
    copies = []
    for a in range(len(p_refs)):
        for k, (fx, fy) in enumerate(CHIP_FLIPS):
            px, py = _flip(x, fx), _flip(y, fy)
            copies.append(pltpu.make_async_remote_copy(
                src_ref=p_refs[a].at[2 * px + py], dst_ref=got_refs[a].at[k], send_sem=send_sems.at[3 * a + k],
                recv_sem=recv_sems.at[3 * a + k], device_id=(px, py, c), device_id_type=MESH))
    return copies


def _rs_to_chips(partials, name):
    n = len(partials)

    def body(*refs):
        copies = _to_chips_copies(refs[:n], refs[n:2 * n], *refs[2 * n:])
        for cp in copies:
            cp.start()
        for cp in copies:
            cp.wait()

    return _call(
        body, name=name, in_specs=[ANY] * n, out_specs=[ANY] * n,
        out_shape=[jax.ShapeDtypeStruct((3,) + p.shape[1:], p.dtype) for p in partials],
        scratch_shapes=[pltpu.SemaphoreType.DMA((3 * n,)), pltpu.SemaphoreType.DMA((3 * n,))],
    )(*partials)


HBM = pl.BlockSpec(memory_space=pltpu.HBM)
SEM = pl.BlockSpec(memory_space=pltpu.SEMAPHORE)
SIDE_EFFECT = pltpu.CompilerParams(has_side_effects=pltpu.SideEffectType.DATAFLOW_SIDE_EFFECTING)


def _split_start(copies_fn, srcs, land_shapes, n_sems, name, after=None):
    n, m = len(srcs), len(land_shapes)
    extra = [] if after is None else [after]

    def body(*refs):
        outs = refs[n + m + len(extra):]
        send_sems, recv_sems, token = outs[0], outs[1], outs[-1]
        for cp in copies_fn(refs[:n], refs[n:n + m], send_sems, recv_sems):
            cp.start()
        token[...] = jnp.zeros_like(token)

    ins = [pltpu.with_memory_space_constraint(t, pltpu.HBM) for t in list(srcs) + [lax.empty(s.shape, s.dtype) for s in land_shapes]]
    res = _call(
        body, name=name, in_specs=[HBM] * (n + m) + [ANY] * len(extra),
        out_specs=[SEM, SEM] + [HBM] * (n + m) + [pl.BlockSpec(memory_space=pltpu.VMEM)],
        out_shape=[pltpu.SemaphoreType.DMA((n_sems,)), pltpu.SemaphoreType.DMA((n_sems,))]
        + [pltpu.HBM(t.shape, t.dtype) for t in ins] + [jax.ShapeDtypeStruct((8, LANES), F32)],
        input_output_aliases={i: 2 + i for i in range(n + m)}, compiler_params=SIDE_EFFECT,
    )(*ins, *extra)
    return dict(sems=(res[0], res[1]), srcs=res[2:2 + n], lands=res[2 + n:2 + n + m], token=res[-1])


def _split_wait(copies_fn, started, after, name):
    n, m = len(started["srcs"]), len(started["lands"])

    def body(*refs):
        for cp in copies_fn(refs[:n], refs[n:n + m], refs[n + m], refs[n + m + 1]):
            cp.wait_send()
            cp.wait_recv()

    bufs = list(started["srcs"]) + list(started["lands"])
    res = _call(
        body, name=name, in_specs=[HBM] * (n + m) + [SEM, SEM, ANY], out_specs=[HBM] * (n + m),
        out_shape=[pltpu.HBM(t.shape, t.dtype) for t in bufs],
        input_output_aliases={i: i for i in range(n + m)}, compiler_params=SIDE_EFFECT,
    )(*bufs, *started["sems"], after)
    return res[:n], res[n:]


def _to_all_copies(x_refs, out_refs, send_sems, recv_sems):
    x, y, c = _coords()
    copies = []
    for a in range(len(x_refs)):
        for k in range(N_DEV - 1):
            fx, fy, fc = ((k + 1) >> 2) & 1, ((k + 1) >> 1) & 1, (k + 1) & 1
            copies.append(pltpu.make_async_remote_copy(
                src_ref=x_refs[a], dst_ref=out_refs[a].at[4 * x + 2 * y + c], send_sem=send_sems.at[7 * a + k],
                recv_sem=recv_sems.at[7 * a + k], device_id=(_flip(x, fx), _flip(y, fy), _flip(c, fc)), device_id_type=MESH))
    return copies


def _fill_own_block(gathered, shard, me_idx, name):
    rows, cols = shard.shape
    tr = _tile(rows, 512, 16)

    def body(me_ref, g_ref, s_ref, o_ref):
        o_ref[0] = s_ref[...]

    grid_spec = pltpu.PrefetchScalarGridSpec(
        num_scalar_prefetch=1, grid=(rows // tr,),
        in_specs=[ANY, pl.BlockSpec((tr, cols), lambda i, me: (i, 0))],
        out_specs=pl.BlockSpec((1, tr, cols), lambda i, me: (me[0], i, 0)))
    return _call(
        body, name=name, grid_spec=grid_spec, out_shape=jax.ShapeDtypeStruct(gathered.shape, gathered.dtype),
        input_output_aliases={1: 0}, compiler_params=_params(("arbitrary",)),
    )(me_idx, gathered, shard)


def _rs_chip_sum(partial, got, chip_idx, name, part=0, n_parts=1, dst=None):
    _, rows, cols = partial.shape
    tr = _tile(rows, 256, 16)
    steps = rows // tr
    n_dst = 0 if dst is None else 1

    def body(p_idx_ref, p_ref, got_ref, *refs):
        refs[n_dst][...] = ((p_ref[0].astype(F32) + got_ref[0].astype(F32)) + got_ref[1].astype(F32)) + got_ref[2].astype(F32)

    grid_spec = pltpu.PrefetchScalarGridSpec(
        num_scalar_prefetch=1, grid=(steps,),
        in_specs=[pl.BlockSpec((1, tr, cols), lambda i, p_ref: (p_ref[0], i, 0)),
                  pl.BlockSpec((3, tr, cols), lambda i, p_ref: (0, i, 0))] + [ANY] * n_dst,
        out_specs=pl.BlockSpec((tr, cols), lambda i, p_ref: (part * steps + i, 0)))
    return _call(
        body, name=name, grid_spec=grid_spec, out_shape=jax.ShapeDtypeStruct((n_parts * rows, cols), F32),
        input_output_aliases={3: 0} if n_dst else {}, compiler_params=_params(("parallel",)),
    )(chip_idx, partial, got, *([] if dst is None else [dst]))


def _rs_begin(gs, tag, split):
    c_idx = jnp.reshape(lax.axis_index("c"), (1,)).astype(jnp.int32)
    gots = _rs_to_sibling(gs, "rs_to_sibling_" + tag)
    partials = [_rs_pair_sum(g, got, c_idx, "rs_pair_sum_%s%d" % (tag, a)) for a, (g, got) in enumerate(zip(gs, gots))]
    if not split:
        return dict(partials=partials, gots=_rs_to_chips(partials, "rs_to_chips_" + tag))
    lands = [jax.ShapeDtypeStruct((3,) + p.shape[1:], p.dtype) for p in partials]
    return _split_start(_to_chips_copies, partials, lands, 3 * len(partials), "rs_to_chips_start_" + tag)


def _rs_finish(begun, tag, after=None, part=0, n_parts=1, dsts=None):
    x, y, _ = _coords()
    chip_idx = jnp.reshape(2 * x + y, (1,)).astype(jnp.int32)
    if "gots" in begun:
        partials, gots = begun["partials"], begun["gots"]
    else:
        partials, gots = _split_wait(_to_chips_copies, begun, after, "rs_to_chips_wait_" + tag)
    return [_rs_chip_sum(p, got, chip_idx, "rs_chip_sum_%s%d" % (tag, a), part, n_parts, None if dsts is None else dsts[a])
            for a, (p, got) in enumerate(zip(partials, gots))]


RUNS = 3
RUN_FIELDS = 6


def _lane_gather_table(src_of, src_width):
    n_blocks = src_of.shape[0] // LANES
    tab = np.zeros((n_blocks + 1, RUNS, RUN_FIELDS), np.int32)
    tab[:, :, 5] = LANES
    for t in range(n_blocks):
        runs = []
        for lane in range(LANES):
            slab, col = (int(v) for v in src_of[t * LANES + lane])
            if slab < 0:
                continue
            key = (slab, col // LANES, col % LANES - lane)
            if runs and runs[-1][0] == key and runs[-1][2] == lane:
                runs[-1][2] = lane + 1
            else:
                runs.append([key, lane, lane + 1])
        assert len(runs) <= RUNS
        slots = [None] * RUNS
        for key, lo, hi in sorted(runs, key=lambda r: r[0][:2]):
            e = key[1] % 2 if slots[key[1] % 2] is None else slots.index(None)
            slots[e] = (key[0], key[1], key[2], lo, hi, min(LANES, src_width - key[1] * LANES))
        for e in range(RUNS):
            tab[t, e] = slots[e] if slots[e] is not None else (tab[t - 1, e, 0], tab[t - 1, e, 1], 0, 0, 0, LANES) if t else tab[t, e]
    tab[n_blocks, :, :2] = tab[n_blocks - 1, :, :2]
    return tab.reshape(-1)


def _lane_gather(src, table, out_slabs, out_width, name):
    _, rows, _ = src.shape
    blocks_per_slab = -(-out_width // LANES)

    def body(tab_ref, *refs):
        o_ref = refs[RUNS]
        t = pl.program_id(0)
        lane = lax.broadcasted_iota(jnp.int32, (1, LANES), 1)
        out = jnp.zeros((rows, LANES), F32)
        for e in range(RUNS):
            base = (t * RUNS + e) * RUN_FIELDS
            shift, lo, hi = tab_ref[base + 2], tab_ref[base + 3], tab_ref[base + 4]
            turned = pltpu.roll(refs[e][0].astype(F32), (LANES - shift) % LANES, 1)
            out = jnp.where((lane >= lo) & (lane < hi), turned, out)
        o_ref[0] = out.astype(BF16)

    def src_spec(e):
        return pl.BlockSpec((1, rows, LANES), lambda t, tab: (tab[(t * RUNS + e) * RUN_FIELDS], 0, tab[(t * RUNS + e) * RUN_FIELDS + 1]))

    grid_spec = pltpu.PrefetchScalarGridSpec(
        num_scalar_prefetch=1, grid=(out_slabs * blocks_per_slab,), in_specs=[src_spec(e) for e in range(RUNS)],
        out_specs=pl.BlockSpec((1, rows, LANES), lambda t, tab: (t // blocks_per_slab, 0, t % blocks_per_slab)))
    return _call(
        body, name=name, grid_spec=grid_spec, out_shape=jax.ShapeDtypeStruct((out_slabs, rows, out_width), BF16),
        compiler_params=_params(("arbitrary",)),
    )(jnp.asarray(table), src, src, src)


def _all_reduce_small(vec):
    rows, cols = vec.shape

    def body(v_ref, o_ref, buf, send_sems, recv_sems):
        x, y, c = _coords()
        me = 4 * x + 2 * y + c
        buf[me] = v_ref[...]
        copies = []
        for k in range(N_DEV - 1):
            fx, fy, fc = ((k + 1) >> 2) & 1, ((k + 1) >> 1) & 1, (k + 1) & 1
            cp = pltpu.make_async_remote_copy(
                src_ref=v_ref, dst_ref=buf.at[me], send_sem=send_sems.at[k], recv_sem=recv_sems.at[k],
                device_id=(_flip(x, fx), _flip(y, fy), _flip(c, fc)), device_id_type=MESH)
            cp.start()
            copies.append(cp)
        for cp in copies:
            cp.wait()
        total = buf[0]
        for j in range(1, N_DEV):
            total = total + buf[j]
        o_ref[...] = total

    vmem = pl.BlockSpec(memory_space=pltpu.VMEM)
    return _call(
        body, name="all_reduce_small", in_specs=[vmem], out_specs=vmem,
        out_shape=jax.ShapeDtypeStruct((rows, cols), F32),
        scratch_shapes=[pltpu.VMEM((N_DEV, rows, cols), F32), pltpu.SemaphoreType.DMA((N_DEV - 1,)),
                        pltpu.SemaphoreType.DMA((N_DEV - 1,))],
    )(vec)


def _w_in_column_maps(ns, o_ba, n_logit, n_main, n_all):
    own = np.arange(N_DEV * ns)
    work_of_own = np.where(own < o_ba, own, np.where(own < o_ba + n_logit, n_main + own - o_ba, own - n_logit))
    to_work = np.full((n_all, 2), -1, np.int64)
    to_work[work_of_own, 0] = own // ns
    to_work[work_of_own, 1] = own % ns
    slab_width = -(-ns // LANES) * LANES
    to_own = np.full((N_DEV, slab_width, 2), -1, np.int64)
    to_own[:, :ns, 0] = 0
    to_own[:, :ns, 1] = work_of_own.reshape(N_DEV, ns)
    return to_work, to_own.reshape(-1, 2)


def kernel(x, meta_tokens, norm_w, w_in, conv_w, A_log, dt_bias, pool_mix, pool_scale, dn_norm_w, w_pool_out, w_dn_out, w_o, final_norm_w, loss_target, m_meta_tokens, m_norm_w, m_w_in, m_conv_w, m_A_log, m_dt_bias, m_pool_mix, m_pool_scale, m_dn_norm_w, m_w_pool_out, m_w_dn_out, m_w_o, m_final_norm_w, v_meta_tokens, v_norm_w, v_w_in, v_conv_w, v_A_log, v_dt_bias, v_pool_mix, v_pool_scale, v_dn_norm_w, v_w_pool_out, v_w_dn_out, v_w_o, v_final_norm_w):
    seq, d = x.shape[1], x.shape[2]
    n_meta = meta_tokens.shape[0]
    n_heads, hd = A_log.shape[-1], dn_norm_w.shape[-1]
    dn = n_heads * hd
    pw, ng = pool_scale.shape[-1], pool_mix.shape[1]
    pg = pw // ng
    kw = conv_w.shape[1]
    pad = (-n_meta) % CHUNK
    x0 = pad + n_meta
    lp = x0 + seq
    ns = w_in.shape[-1]
    in_cols = N_DEV * ns
    o_q, o_k, o_v, o_zd = 2 * pw, 2 * pw + dn, 2 * pw + 2 * dn, 2 * pw + 3 * dn
    o_ba = 2 * pw + 4 * dn
    o_gp, o_gd = o_ba, o_ba + d
    n_main = o_gd + d
    n_all = n_main + 2 * LANES
    assert lp % CHUNK == 0 and in_cols == n_main + 2 * n_heads and 2 * n_heads <= LANES and hd == LANES
    cs, ms = conv_w.shape[-1], meta_tokens.shape[-1]
    mr = pool_mix.shape[2]
    assert ms == pg and cs % pg == 0
    to_work, to_own = _w_in_column_maps(ns, o_ba, 2 * n_heads, n_main, n_all)
    cols_major = lambda t: jnp.transpose(t, (1, 0, 2)).reshape(t.shape[1], N_DEV * t.shape[2])

    win_g, mix_g, conv_g, meta_g = _all_gather(
        [w_in[0].astype(BF16), pool_mix[0].reshape(ng * mr, pg).astype(BF16), conv_w[0], meta_tokens])
    late_shards = [w_pool_out[0].astype(BF16), w_dn_out[0].astype(BF16), w_o[0].astype(BF16)]
    late_weights = _split_start(_to_all_copies, late_shards, [jax.ShapeDtypeStruct((N_DEV,) + s.shape, BF16) for s in late_shards],
                                (N_DEV - 1) * len(late_shards), "gather_out_proj_start", after=meta_g)
    norm_w_in = norm_w + late_weights["token"][0, 0]
    w_all = _lane_gather(win_g, _lane_gather_table(to_work, ns), 1, n_all, "w_in_to_work").reshape(d, n_all)
    mix_f = jnp.transpose(mix_g.reshape(N_DEV, ng, mr, pg), (1, 0, 2, 3)).reshape(ng, pg, pg)
    conv_f = cols_major(conv_g)
    meta_f = cols_major(meta_g)

    h0, xn = _norm_in_fwd(x[0], meta_f, norm_w_in, pad)
    proj = _matmul(xn, w_all, NN, F32, 1056, 768, 2048, "proj")
    y_pool = _pool_fwd(proj, mix_f, pool_scale, pad)
    conv_q, conv_k, conv_v = (conv_f[:, i * dn:(i + 1) * dn] for i in range(3))
    qn = _conv_fwd(proj, o_q, conv_q, hd, float(hd) ** -0.5, "conv_q_fwd")
    kn = _conv_fwd(proj, o_k, conv_k, hd, 1.0, "conv_k_fwd")
    vv = _conv_fwd(proj, o_v, conv_v, hd, None, "conv_v_fwd")
    logit_lanes = (n_heads, LANES - 2 * n_heads)
    prm = jnp.pad(A_log, ((0, 7), logit_lanes)) + jnp.pad(dt_bias, ((1, 6), logit_lanes))
    y_dn, hist, tmats = _chunk_fwd(qn, kn, vv, proj, o_zd, n_main, prm, dn_norm_w, n_heads, pad)
    me_idx = jnp.reshape(4 * lax.axis_index("x") + 2 * lax.axis_index("y") + lax.axis_index("c"), (1,)).astype(jnp.int32)
    _, landed = _split_wait(_to_all_copies, late_weights, y_dn, "gather_out_proj_wait")
    wpo_g, wdn_g, wo_g = (_fill_own_block(g, s, me_idx, "own_block_%d" % i) for i, (g, s) in enumerate(zip(landed, late_shards)))
    wpo_f = cols_major(wpo_g)
    wdn_f = wdn_g.reshape(dn, d)
    wo_f = wo_g.reshape(d, d)
    p_out = _matmul(y_pool, wpo_f, NN, F32, 1056, 1024, 1024, "pool_out")
    q_out = _matmul(y_dn, wdn_f, NN, F32, 1056, 1024, 2048, "dn_out")
    merged = _merge_fwd(p_out, q_out, proj, o_gp, o_gd)
    mo = _matmul(merged, wo_f, NN, F32, 1056, 1024, 2048, "w_o_fwd")
    dh1, d_fw, loss_part = _final_loss(h0, mo, final_norm_w.reshape(1, d), loss_target[0], x0)

    d_merged = _matmul(dh1, wo_f, NT, F32, 1056, 1024, 1024, "w_o_bwd_x")
    g_wo = _matmul(merged.T, dh1, NN, BF16, 1024, 1024, lp, "w_o_bwd_w")
    d_p, d_q, d_gp, d_gd = _merge_bwd(p_out, q_out, proj, o_gp, o_gd, d_merged)
    d_ypool = _matmul(d_p, wpo_f, NT, F32, 1056, 1024, 2048, "pool_out_bwd_x")
    g_wpo = _matmul(y_pool.T, d_p, NN, BF16, 1024, 1024, lp, "pool_out_bwd_w", col_blocks=N_DEV)
    d_ydn = _matmul(d_q, wdn_f, NT, F32, 1056, 1024, 2048, "dn_out_bwd_x")
    g_wdn = _matmul(y_dn.T, d_q, NN, BF16, 1024, 1024, lp, "dn_out_bwd_w")
    rs_early = _rs_begin([g_wpo, g_wdn.reshape(N_DEV, dn // N_DEV, d), g_wo.reshape(N_DEV, d // N_DEV, d)], "early", split=True)
    started = rs_early["token"][0, 0]
    d_u, d_zp, g_mix, g_pscale = _pool_bwd(proj, mix_f, pool_scale + started, d_ypool, pad)
    d_qn, d_kn, d_vv, d_ba, d_zd, d_prm, g_dnw = _chunk_bwd(qn, kn, vv, proj, o_zd, n_main, prm + started, dn_norm_w, hist, tmats, d_ydn, n_heads, pad)
    d_qr, g_cq = _conv_bwd(proj, o_q, conv_q, d_qn, hd, float(hd) ** -0.5, pad, "conv_q_bwd")
    d_kr, g_ck = _conv_bwd(proj, o_k, conv_k, d_kn, hd, 1.0, pad, "conv_k_bwd")
    d_vr, g_cv = _conv_bwd(proj, o_v, conv_v, d_vv, hd, None, pad, "conv_v_bwd")
    d_proj = jnp.concatenate([d_u, d_zp, d_qr, d_kr, d_vr, d_zd, d_gp, d_gd, d_ba.astype(BF16), jnp.zeros((lp, LANES), BF16)], axis=1)
    xn_t, rs_late, token = xn.T, [], None
    for half in range(2):
        rows = slice(half * (d // 2), (half + 1) * (d // 2))
        g_wall = _matmul(xn_t[rows], d_proj, NN, F32, 1024, 768, lp, "w_in_bwd_w_%d" % half, after=token)
        g_win = _lane_gather(g_wall.reshape(1, d // 2, n_all), _lane_gather_table(to_own, n_all), N_DEV, ns, "w_in_grad_to_own_%d" % half)
        rs_late.append(_rs_begin([g_win], "late%d" % half, split=True))
        token = rs_late[-1]["token"]
    d_xn = _matmul(d_proj, w_all, NT, F32, 1056, 1024, 768, "w_in_bwd_x", after=token)
    d_head, grad_x, g_nw = _norm_in_bwd(h0, norm_w, d_xn, dh1, x0)
    grad_x = grad_x[None]

    by_cols = lambda t: jnp.transpose(t.reshape(t.shape[0], N_DEV, t.shape[1] // N_DEV), (1, 0, 2))
    g_conv = by_cols(jnp.concatenate([g_cq, g_ck, g_cv], axis=1)).reshape(N_DEV, kw * cs // pg, pg)
    conv_rows = -(-g_conv.shape[1] // 16) * 16
    g_small = jnp.concatenate(
        [jnp.transpose(g_mix.reshape(ng, N_DEV, mr, pg), (1, 0, 2, 3)).reshape(N_DEV, ng * mr, pg), by_cols(d_head[pad:x0]),
         jnp.pad(g_conv, ((0, 0), (0, conv_rows - g_conv.shape[1]), (0, 0)))], axis=1).astype(BF16)
    r_small, = _rs_finish(_rs_begin([g_small], "small", split=False), "small")
    r_mix, r_meta = r_small[:ng * mr], r_small[ng * mr:ng * mr + n_meta]
    r_conv = r_small[ng * mr + n_meta:ng * mr + n_meta + kw * cs // pg]
    r_wpo, r_wdn, r_wo = _rs_finish(rs_early, "early", after=r_small)

    small = [g_nw[0], d_fw[0], g_pscale[0], g_dnw[0], d_prm[0], d_prm[1], loss_part[0]]
    s_sizes = [t.shape[0] for t in small]
    s_cols = -(-sum(s_sizes) // (8 * LANES)) * LANES
    s_vec = jnp.concatenate(small + [jnp.zeros((8 * s_cols - sum(s_sizes),), F32)]).reshape(8, s_cols)
    s_red = _all_reduce_small(s_vec)
    s_sum = s_red.reshape(-1)
    r_win = None
    for half, begun in enumerate(rs_late):
        r_win = _rs_finish(begun, "late%d" % half, after=s_red, part=half, n_parts=2, dsts=r_win)
    r_win, = r_win
    s_offs = [sum(s_sizes[:i]) for i in range(len(s_sizes))]
    s_take = lambda i, n=None, o=0: s_sum[s_offs[i] + o:s_offs[i] + o + (s_sizes[i] if n is None else n)]

    grads = {
        "meta_tokens": r_meta, "norm_w": s_take(0).reshape(norm_w.shape),
        "w_in": r_win.reshape(w_in.shape), "conv_w": r_conv.reshape(conv_w.shape),
        "A_log": s_take(4, n_heads, n_heads).reshape(A_log.shape), "dt_bias": s_take(5, n_heads, n_heads).reshape(dt_bias.shape),
        "pool_mix": r_mix.reshape(pool_mix.shape), "pool_scale": s_take(2).reshape(pool_scale.shape),
        "dn_norm_w": s_take(3).reshape(dn_norm_w.shape), "w_pool_out": r_wpo.reshape(w_pool_out.shape),
        "w_dn_out": r_wdn.reshape(w_dn_out.shape), "w_o": r_wo.reshape(w_o.shape),
        "final_norm_w": s_take(1).reshape(final_norm_w.shape),
    }
    loss = s_take(6, 1)[0]

    weights = dict(meta_tokens=meta_tokens, norm_w=norm_w, w_in=w_in, conv_w=conv_w, A_log=A_log, dt_bias=dt_bias,
                   pool_mix=pool_mix, pool_scale=pool_scale, dn_norm_w=dn_norm_w, w_pool_out=w_pool_out, w_dn_out=w_dn_out,
                   w_o=w_o, final_norm_w=final_norm_w)
    m_in = dict(meta_tokens=m_meta_tokens, norm_w=m_norm_w, w_in=m_w_in, conv_w=m_conv_w, A_log=m_A_log, dt_bias=m_dt_bias,
                pool_mix=m_pool_mix, pool_scale=m_pool_scale, dn_norm_w=m_dn_norm_w, w_pool_out=m_w_pool_out,
                w_dn_out=m_w_dn_out, w_o=m_w_o, final_norm_w=m_final_norm_w)
    v_in = dict(meta_tokens=v_meta_tokens, norm_w=v_norm_w, w_in=v_w_in, conv_w=v_conv_w, A_log=v_A_log, dt_bias=v_dt_bias,
                pool_mix=v_pool_mix, pool_scale=v_pool_scale, dn_norm_w=v_dn_norm_w, w_pool_out=v_w_pool_out,
                w_dn_out=v_w_dn_out, w_o=v_w_o, final_norm_w=v_final_norm_w)
    names = list(weights)
    upd = {n: _adamw(weights[n], grads[n], m_in[n], v_in[n], "adamw_" + n) for n in names}
    return (loss, grad_x, *[grads[n] for n in names], *[upd[n][0] for n in names], *[upd[n][1] for n in names],
            *[upd[n][2] for n in names])
```

```python
import functools

import jax
import jax.numpy as jnp
import numpy as np
from jax import lax
from jax.experimental import pallas as pl
from jax.experimental.pallas import tpu as pltpu

F32 = jnp.float32
BF16 = jnp.bfloat16
HIGHEST = lax.Precision.HIGHEST
MESH = pl.DeviceIdType.MESH

CHUNK = 64
NORM_EPS = 1e-6
POOL_WINDOWS = (2, 4, 8, 16)
ADAM_LR, ADAM_B1, ADAM_B2, ADAM_EPS, ADAM_WD, ADAM_STEP = 0.001, 0.9, 0.999, 1e-08, 0.01, 10
N_DEV = 8
LANES = 128
VMEM_LIMIT = 48 * 1024 * 1024

NN = (((1,), (0,)), ((), ()))
NT = (((1,), (1,)), ((), ()))
TN = (((0,), (0,)), ((), ()))


def _call(body, **kw):
    return pl.pallas_call(body, **kw)


def _params(sem=None):
    return pltpu.CompilerParams(dimension_semantics=sem, vmem_limit_bytes=VMEM_LIMIT)


def _tile(n, pref, align):
    for d in range(min(pref, n), 0, -1):
        if n % d == 0 and d % align == 0:
            return d
    return n


def _dot(a, b, dims=NN, precision=None):
    return lax.dot_general(a, b, dims, precision=precision, preferred_element_type=F32)


def _sigmoid(x):
    return 1.0 / (1.0 + jnp.exp(-x))


def _silu(x):
    return x * _sigmoid(x)


def _softplus(x):
    return jnp.maximum(x, 0.0) + jnp.log(1.0 + jnp.exp(-jnp.abs(x)))


def _rmsnorm(x, w):
    return x * lax.rsqrt(jnp.mean(x * x, axis=-1, keepdims=True) + NORM_EPS) * w


def _shift_down(x, j, row):
    if j == 0:
        return x
    return jnp.where(row >= j, pltpu.roll(x, j, 0), 0.0)


def _shift_up(x, j, row):
    if j == 0:
        return x
    n = x.shape[0]
    return jnp.where(row < n - j, pltpu.roll(x, n - j, 0), 0.0)


def _matmul(a, b, dims, out_dtype, tm, tn, tk, name, col_blocks=None, after=None):
    ta = dims == TN
    tb = dims == NT
    m, kdim = (a.shape[1], a.shape[0]) if ta else a.shape
    n = b.shape[0] if tb else b.shape[1]
    if col_blocks:
        tn = n // col_blocks
    tm, tn, tk = _tile(m, tm, 8), _tile(n, tn, LANES), _tile(kdim, tk, LANES if not ta else 16)
    nk = kdim // tk

    n_extra = 0 if after is None else 1

    def body(a_ref, b_ref, *refs):
        o_ref, scratch = refs[n_extra], refs[n_extra + 1:]
        part = _dot(a_ref[...].astype(BF16), b_ref[...].astype(BF16), dims)
        if nk == 1:
            o_ref[...] = part.astype(o_ref.dtype).reshape(o_ref.shape)
            return
        acc_ref, = scratch
        k = pl.program_id(2)

        @pl.when(k == 0)
        def _():
            acc_ref[...] = part

        @pl.when(k > 0)
        def _():
            acc_ref[...] += part

        @pl.when(k == nk - 1)
        def _():
            o_ref[...] = acc_ref[...].astype(o_ref.dtype).reshape(o_ref.shape)

    a_spec = pl.BlockSpec((tk, tm), lambda i, j, k: (k, i)) if ta else pl.BlockSpec((tm, tk), lambda i, j, k: (i, k))
    b_spec = pl.BlockSpec((tn, tk), lambda i, j, k: (j, k)) if tb else pl.BlockSpec((tk, tn), lambda i, j, k: (k, j))
    if col_blocks:
        out_spec = pl.BlockSpec((1, tm, tn), lambda i, j, k: (j, i, 0))
        out_shape = jax.ShapeDtypeStruct((col_blocks, m, tn), out_dtype)
    else:
        out_spec = pl.BlockSpec((tm, tn), lambda i, j, k: (i, j))
        out_shape = jax.ShapeDtypeStruct((m, n), out_dtype)
    return _call(
        body, name=name, grid=(m // tm, n // tn, nk),
        in_specs=[a_spec, b_spec] + [ANY] * n_extra, out_specs=out_spec, out_shape=out_shape,
        scratch_shapes=[] if nk == 1 else [pltpu.VMEM((tm, tn), F32)],
        compiler_params=_params(("parallel", "parallel", "arbitrary")),
    )(a, b, *([] if after is None else [after]))


def _norm_in_fwd(x2d, meta, w, pad):
    seq, d = x2d.shape
    tr = pad + meta.shape[0]
    assert seq % tr == 0 and tr % 16 == 0
    lp = tr + seq

    def body(x_ref, m_ref, w_ref, h_ref, o_ref):
        def emit(h):
            h_ref[...] = h
            o_ref[...] = _rmsnorm(h, w_ref[...]).astype(BF16)

        @pl.when(pl.program_id(0) == 0)
        def _():
            emit(jnp.concatenate([jnp.zeros((pad, d), F32), m_ref[...]], axis=0) if pad else m_ref[...])

        @pl.when(pl.program_id(0) > 0)
        def _():
            emit(x_ref[...])

    row = pl.BlockSpec((tr, d), lambda i: (i, 0))
    return _call(
        body, name="norm_in_fwd", grid=(lp // tr,),
        in_specs=[pl.BlockSpec((tr, d), lambda i: (jnp.maximum(i - 1, 0), 0)), pl.BlockSpec(meta.shape, lambda i: (0, 0)),
                  pl.BlockSpec((1, d), lambda i: (0, 0))],
        out_specs=[row, row],
        out_shape=[jax.ShapeDtypeStruct((lp, d), F32), jax.ShapeDtypeStruct((lp, d), BF16)],
        compiler_params=_params(("arbitrary",)),
    )(x2d, meta, w)


def _norm_in_bwd(h0, w, dxn, dh1, x0):
    lp, d = h0.shape
    tr = x0
    assert lp % tr == 0

    def body(h_ref, w_ref, da_ref, dh1_ref, head_ref, gx_ref, dw_ref):
        i = pl.program_id(0)
        _, vjp = jax.vjp(_rmsnorm, h_ref[...], w_ref[...])
        dh, dw = vjp(da_ref[...])
        dh = dh + dh1_ref[...]

        @pl.when(i == 0)
        def _():
            head_ref[...] = dh
            dw_ref[...] = dw

        @pl.when(i > 0)
        def _():
            gx_ref[...] = dh
            dw_ref[...] += dw

    row = pl.BlockSpec((tr, d), lambda i: (i, 0))
    vec = pl.BlockSpec((1, d), lambda i: (0, 0))
    return _call(
        body, name="norm_in_bwd", grid=(lp // tr,),
        in_specs=[row, vec, row, row],
        out_specs=[pl.BlockSpec((tr, d), lambda i: (0, 0)), pl.BlockSpec((tr, d), lambda i: (jnp.maximum(i - 1, 0), 0)), vec],
        out_shape=[jax.ShapeDtypeStruct((tr, d), F32), jax.ShapeDtypeStruct((lp - tr, d), F32), jax.ShapeDtypeStruct((1, d), F32)],
        compiler_params=_params(("arbitrary",)),
    )(h0, w, dxn, dh1)


def _final_loss(h0, mo, fw, tgt, x0):
    lp, d = h0.shape
    tr = x0
    assert lp % tr == 0

    def body(h_ref, mo_ref, fw_ref, t_ref, dh_ref, dw_ref, loss_ref):
        i = pl.program_id(0)
        row = i * tr + lax.broadcasted_iota(jnp.int32, (tr, 1), 0)
        mask = jnp.where(row >= x0, 1.0, 0.0).astype(F32)
        tgt_v = t_ref[...]

        def loss_fn(h1, w):
            err = _rmsnorm(h1, w) - tgt_v
            return 0.5 * jnp.sum(jnp.mean(err * err, axis=-1, keepdims=True) * mask, axis=0, keepdims=True)

        loss, vjp = jax.vjp(loss_fn, h_ref[...] + mo_ref[...], fw_ref[...])
        dh, dw = vjp(jnp.ones((1, 1), F32))
        dh_ref[...] = dh

        @pl.when(i == 0)
        def _():
            dw_ref[...] = jnp.zeros_like(dw_ref)
            loss_ref[...] = jnp.zeros_like(loss_ref)

        dw_ref[...] += dw
        loss_ref[...] += jnp.broadcast_to(loss, loss_ref.shape)

    row_spec = pl.BlockSpec((tr, d), lambda i: (i, 0))
    vec = pl.BlockSpec((1, d), lambda i: (0, 0))
    return _call(
        body, name="final_loss", grid=(lp // tr,),
        in_specs=[row_spec, row_spec, vec, pl.BlockSpec((tr, d), lambda i: (jnp.maximum(i - 1, 0), 0))],
        out_specs=[row_spec, vec, pl.BlockSpec((8, LANES), lambda i: (0, 0))],
        out_shape=[jax.ShapeDtypeStruct((lp, d), F32), jax.ShapeDtypeStruct((1, d), F32), jax.ShapeDtypeStruct((8, LANES), F32)],
        compiler_params=_params(("arbitrary",)),
    )(h0, mo, fw, tgt)


def _pool_select(parts, g):
    out = parts[-1]
    for gi in range(len(parts) - 2, -1, -1):
        out = jnp.where(g == gi, parts[gi], out)
    return out


def _pool_count(row, g, pad):
    win = _pool_select([jnp.full(row.shape, float(w), F32) for w in POOL_WINDOWS], g)
    return jnp.maximum(jnp.minimum((row - pad + 1).astype(F32), win), 1.0)


def _pooled(u, g, row, pad):
    sums, s, span = [], u, 1
    for w in POOL_WINDOWS:
        while span < w:
            s = s + _shift_down(s, span, row)
            span *= 2
        sums.append(s)
    return _pool_select(sums, g) / _pool_count(row, g, pad) - u


def _pooled_adjoint(dp, g, row, pad):
    e = dp / _pool_count(row, g, pad)
    sums, s, span = [], e, 1
    for w in POOL_WINDOWS:
        while span < w:
            s = s + _shift_up(s, span, row)
            span *= 2
        sums.append(s)
    return _pool_select(sums, g) - dp


def _pool_specs(lp, pg, ng, z_off):
    u_spec = pl.BlockSpec((lp, pg), lambda g: (0, g))
    z_spec = pl.BlockSpec((lp, pg), lambda g: (0, z_off + g))
    mix_spec = pl.BlockSpec((1, pg, pg), lambda g: (g, 0, 0))
    vec_spec = pl.BlockSpec((1, pg), lambda g: (0, g))
    return u_spec, z_spec, mix_spec, vec_spec


def _pool_fwd(proj, mix, scale, pad):
    lp = proj.shape[0]
    ng, pg, _ = mix.shape
    pw = ng * pg

    def body(u_ref, z_ref, mix_ref, sc_ref, y_ref):
        g = pl.program_id(0)
        row = lax.broadcasted_iota(jnp.int32, (lp, 1), 0)
        pooled = _pooled(u_ref[...], g, row, pad)
        mixed = _dot(pooled.astype(BF16), mix_ref[0])
        y_ref[...] = (mixed * sc_ref[...] * _silu(z_ref[...])).astype(BF16)

    u_spec, z_spec, mix_spec, vec_spec = _pool_specs(lp, pg, ng, pw // pg)
    return _call(
        body, name="pool_fwd", grid=(ng,), in_specs=[u_spec, z_spec, mix_spec, vec_spec], out_specs=u_spec,
        out_shape=jax.ShapeDtypeStruct((lp, pw), BF16), compiler_params=_params(("parallel",)),
    )(proj, proj, mix, scale)


def _pool_bwd(proj, mix, scale, dy, pad):
    lp = proj.shape[0]
    ng, pg, _ = mix.shape
    pw = ng * pg

    def body(u_ref, z_ref, mix_ref, sc_ref, dy_ref, du_ref, dz_ref, dmix_ref, dsc_ref):
        g = pl.program_id(0)
        row = lax.broadcasted_iota(jnp.int32, (lp, 1), 0)
        real = row >= pad
        z = z_ref[...]
        pooled = _pooled(u_ref[...], g, row, pad).astype(BF16)
        mixed = _dot(pooled, mix_ref[0])
        sig = _sigmoid(z)
        sz = z * sig
        dyv = dy_ref[...]
        dsc_ref[...] = jnp.sum(dyv * mixed * sz, axis=0, keepdims=True)
        d_sz = dyv * mixed * sc_ref[...]
        dz_ref[...] = jnp.where(real, d_sz * (sig + sz * (1.0 - sig)), 0.0).astype(BF16)
        d_mixed = (dyv * sc_ref[...] * sz).astype(BF16)
        dmix_ref[0] = _dot(pooled, d_mixed, TN)
        d_pooled = _dot(d_mixed, mix_ref[0], NT)
        du_ref[...] = jnp.where(real, _pooled_adjoint(d_pooled, g, row, pad), 0.0).astype(BF16)

    u_spec, z_spec, mix_spec, vec_spec = _pool_specs(lp, pg, ng, pw // pg)
    return _call(
        body, name="pool_bwd", grid=(ng,),
        in_specs=[u_spec, z_spec, mix_spec, vec_spec, u_spec], out_specs=[u_spec, u_spec, mix_spec, vec_spec],
        out_shape=[jax.ShapeDtypeStruct((lp, pw), BF16), jax.ShapeDtypeStruct((lp, pw), BF16),
                   jax.ShapeDtypeStruct((ng, pg, pg), F32), jax.ShapeDtypeStruct((1, pw), F32)],
        compiler_params=_params(("parallel",)),
    )(proj, proj, mix, scale, dy)


def _conv_pre(x, w, row):
    kw = w.shape[0]
    y = w[kw - 1:kw, :] * x
    for kk in range(kw - 1):
        y = y + w[kk:kk + 1, :] * _shift_down(x, kw - 1 - kk, row)
    return y


def _conv_post(y, out_scale):
    s = _silu(y)
    if out_scale is None:
        return s
    return s * lax.rsqrt(jnp.sum(s * s, axis=-1, keepdims=True) + NORM_EPS) * out_scale


def _conv_fwd(proj, col_off, w, hd, out_scale, name):
    lp = proj.shape[0]
    kw, width = w.shape
    blk0 = col_off // hd

    def body(x_ref, w_ref, o_ref):
        row = lax.broadcasted_iota(jnp.int32, (lp, 1), 0)
        o_ref[...] = _conv_post(_conv_pre(x_ref[...], w_ref[...], row), out_scale)

    return _call(
        body, name=name, grid=(width // hd,),
        in_specs=[pl.BlockSpec((lp, hd), lambda j: (0, blk0 + j)), pl.BlockSpec((kw, hd), lambda j: (0, j))],
        out_specs=pl.BlockSpec((lp, hd), lambda j: (0, j)),
        out_shape=jax.ShapeDtypeStruct((lp, width), F32), compiler_params=_params(("parallel",)),
    )(proj, w)


def _conv_bwd(proj, col_off, w, d_out, hd, out_scale, pad, name):
    lp = proj.shape[0]
    kw, width = w.shape
    blk0 = col_off // hd

    def body(x_ref, w_ref, do_ref, dx_ref, dw_ref):
        row = lax.broadcasted_iota(jnp.int32, (lp, 1), 0)
        real = row >= pad
        x, wv = x_ref[...], w_ref[...]
        _, vjp = jax.vjp(functools.partial(_conv_post, out_scale=out_scale), _conv_pre(x, wv, row))
        dy = jnp.where(real, vjp(do_ref[...])[0], 0.0)
        dx = wv[kw - 1:kw, :] * dy
        dw_ref[kw - 1:kw, :] = jnp.sum(dy * x, axis=0, keepdims=True)
        for kk in range(kw - 1):
            j = kw - 1 - kk
            dx = dx + wv[kk:kk + 1, :] * _shift_up(dy, j, row)
            dw_ref[kk:kk + 1, :] = jnp.sum(dy * _shift_down(x, j, row), axis=0, keepdims=True)
        dx_ref[...] = jnp.where(real, dx, 0.0).astype(BF16)

    col = pl.BlockSpec((lp, hd), lambda j: (0, j))
    wspec = pl.BlockSpec((kw, hd), lambda j: (0, j))
    return _call(
        body, name=name, grid=(width // hd,),
        in_specs=[pl.BlockSpec((lp, hd), lambda j: (0, blk0 + j)), wspec, col], out_specs=[col, wspec],
        out_shape=[jax.ShapeDtypeStruct((lp, width), BF16), jax.ShapeDtypeStruct((kw, width), F32)],
        compiler_params=_params(("parallel",)),
    )(proj, w, d_out)


HEADS_PER_STEP = 16


def _each(fn, *lists):
    return [fn(*args) for args in zip(*lists)]


def _dot3_each(a_list, b_list, dims=NN):
    hi = lambda t: t.astype(BF16)
    lo = lambda t, t_hi: (t - t_hi.astype(F32)).astype(BF16)
    dot = lambda x, y: _dot(x, y, dims)
    a_hi, b_hi = _each(hi, a_list), _each(hi, b_list)
    a_lo, b_lo = _each(lo, a_list, a_hi), _each(lo, b_list, b_hi)
    hh, hl, lh = _each(dot, a_hi, b_hi), _each(dot, a_hi, b_lo), _each(dot, a_lo, b_hi)
    return _each(lambda x, y, w: x + (y + w), hh, hl, lh)


@jax.custom_vjp
def _unit_lower_inverse(lmats):
    c = lmats[0].shape[0]
    eye = lax.broadcasted_iota(jnp.int32, (c, c), 0) == lax.broadcasted_iota(jnp.int32, (c, c), 1)
    a = [-m for m in lmats]
    tmat = [jnp.where(eye, 1.0, 0.0).astype(F32) + m for m in a]
    span = 2
    while span < c:
        a = _dot3_each(a, a)
        tmat = _each(lambda t, u: t + u, tmat, _dot3_each(tmat, a))
        span *= 2
    return tuple(tmat)


def _unit_lower_inverse_fwd(lmats):
    tmats = _unit_lower_inverse(lmats)
    return tmats, tmats


def _unit_lower_inverse_bwd(tmats, cts):
    left = _each(lambda t, ct: _dot(t, ct, TN, HIGHEST), tmats, cts)
    return (tuple(_each(lambda m, t: -_dot(m, t, NT, HIGHEST), left, tmats)),)


_unit_lower_inverse.defvjp(_unit_lower_inverse_fwd, _unit_lower_inverse_bwd)


@jax.custom_vjp
def _known_inverse(lmats, tmats):
    return tmats


def _known_inverse_fwd(lmats, tmats):
    return tmats, tmats


def _known_inverse_bwd(tmats, cts):
    return _unit_lower_inverse_bwd(tmats, cts)[0], tuple(jnp.zeros_like(t) for t in tmats)


_known_inverse.defvjp(_known_inverse_fwd, _known_inverse_bwd)


def _chunk_math(states, q, k, v, ba, z, prm, nw, head0, rowmask, n_heads, tmats=None, keep_tmats=False):
    c = q.shape[0]
    heads = list(range(len(states)))
    hd = q.shape[1] // len(states)
    lane = lax.broadcasted_iota(jnp.int32, ba.shape, 1)
    sub = lax.broadcasted_iota(jnp.int32, (ba.shape[1], c), 0)
    ri = lax.broadcasted_iota(jnp.int32, (c, c), 0)
    ci = lax.broadcasted_iota(jnp.int32, (c, c), 1)
    last = lax.broadcasted_iota(jnp.int32, (c, 1), 0) == c - 1
    causal, strict = ri >= ci, ri > ci
    beta_all = _sigmoid(ba) * rowmask
    g_all = -jnp.exp(prm[0:1, :]) * _softplus(ba + prm[1:2, :]) * rowmask
    gcum_all = _dot(jnp.where(causal, 1.0, 0.0).astype(F32), g_all, precision=HIGHEST)
    gcum_t = gcum_all.T
    split = lambda t: [t[:, j * hd:(j + 1) * hd] for j in heads]
    qs, ks, vs, zs = split(q), split(k), split(v), split(z)
    beta = [jnp.sum(jnp.where(lane == head0 + j, beta_all, 0.0), axis=1, keepdims=True) for j in heads]
    gcum = [jnp.sum(jnp.where(lane == n_heads + head0 + j, gcum_all, 0.0), axis=1, keepdims=True) for j in heads]
    grow = [jnp.sum(jnp.where(sub == n_heads + head0 + j, gcum_t, 0.0), axis=0, keepdims=True) for j in heads]
    glast = _each(lambda gc: jnp.sum(jnp.where(last, gc, 0.0), axis=0, keepdims=True), gcum)
    decay = _each(lambda gc, gr: jnp.where(causal, jnp.exp(jnp.where(causal, gc - gr, 0.0)), 0.0), gcum, grow)
    eg = _each(jnp.exp, gcum)
    k_beta = _each(jnp.multiply, ks, beta)
    kk = _each(lambda a, b: _dot(a, b, NT), k_beta, ks)
    lmats = tuple(_each(lambda m, dc: jnp.where(strict, m * dc, 0.0), kk, decay))
    tmat = list(_unit_lower_inverse(lmats) if tmats is None else _known_inverse(lmats, tuple(tmats)))
    u_c = _each(_dot, tmat, _each(jnp.multiply, vs, beta))
    w_c = _each(_dot, tmat, _each(jnp.multiply, k_beta, eg))
    qk = _each(lambda a, b, dc: jnp.where(causal, _dot(a, b, NT) * dc, 0.0), qs, ks, decay)
    v_new = _each(lambda u, w, s: u - _dot(w, s), u_c, w_c, list(states))
    o = _each(lambda a, e, s, m, vn: _dot(a * e, s) + _dot(m, vn), qs, eg, list(states), qk, v_new)
    k_dec = _each(lambda a, gl, gc: a * jnp.exp(gl - gc), ks, glast, gcum)
    new_states = _each(lambda s, gl, kd, vn: s * jnp.exp(gl) + _dot(kd, vn, TN), list(states), glast, k_dec, v_new)
    ys = _each(lambda oj, zj: _rmsnorm(oj, nw) * _silu(zj), o, zs)
    if keep_tmats:
        return jnp.concatenate(ys, axis=1), tuple(new_states), tuple(tmat)
    return jnp.concatenate(ys, axis=1), tuple(new_states)


def _chunk_specs(nc, hd, n_heads, z_off, ba_off, rev):
    cidx = (lambda c: nc - 1 - c) if rev else (lambda c: c)
    hb = min(HEADS_PER_STEP, n_heads)
    assert n_heads % hb == 0 and z_off % (hb * hd) == 0 and ba_off % LANES == 0
    blk = lambda off: pl.BlockSpec((CHUNK, hb * hd), lambda c, g: (cidx(c), off + g))
    ba_spec = lambda off: pl.BlockSpec((CHUNK, LANES), lambda c, g: (cidx(c), off // LANES))
    prm_spec = pl.BlockSpec((8, LANES), lambda c, g: (0, 0))
    nw_spec = pl.BlockSpec((1, hd), lambda c, g: (0, 0))
    st_spec = pl.BlockSpec((1, hb, hd, hd), lambda c, g: (cidx(c), g, 0, 0))
    return blk, ba_spec, prm_spec, nw_spec, st_spec, blk(z_off // (hb * hd))


def _rowmask(chunk_idx, pad):
    row = chunk_idx * CHUNK + lax.broadcasted_iota(jnp.int32, (CHUNK, 1), 0)
    return jnp.where(row >= pad, 1.0, 0.0).astype(F32)


def _chunk_fwd(qn, kn, vv, proj, z_off, ba_off, prm, nw, n_heads, pad):
    lp, dn = qn.shape
    hd = dn // n_heads
    nc = lp // CHUNK
    hb = min(HEADS_PER_STEP, n_heads)

    def body(q_ref, k_ref, v_ref, ba_ref, z_ref, prm_ref, nw_ref, y_ref, hist_ref, tm_ref, st_ref):
        c, g = pl.program_id(0), pl.program_id(1)

        @pl.when(c == 0)
        def _():
            for j in range(hb):
                st_ref[g * hb + j] = jnp.zeros((hd, hd), F32)

        states = tuple(st_ref[g * hb + j] for j in range(hb))
        for j in range(hb):
            hist_ref[0, j] = states[j]
        y, new_states, tmats = _chunk_math(states, q_ref[...], k_ref[...], v_ref[...], ba_ref[...], z_ref[...], prm_ref[...],
                                           nw_ref[...], g * hb, _rowmask(c, pad), n_heads, keep_tmats=True)
        y_ref[...] = y.astype(BF16)
        for j in range(hb):
            st_ref[g * hb + j] = new_states[j]
            tm_ref[0, j] = tmats[j]

    blk, ba_spec, prm_spec, nw_spec, st_spec, z_spec = _chunk_specs(nc, hd, n_heads, z_off, ba_off, False)
    tm_spec = pl.BlockSpec((1, hb, CHUNK, CHUNK), lambda c, g: (c, g, 0, 0))
    return _call(
        body, name="chunk_fwd", grid=(nc, n_heads // hb),
        in_specs=[blk(0), blk(0), blk(0), ba_spec(ba_off), z_spec, prm_spec, nw_spec], out_specs=[blk(0), st_spec, tm_spec],
        out_shape=[jax.ShapeDtypeStruct((lp, dn), BF16), jax.ShapeDtypeStruct((nc, n_heads, hd, hd), F32),
                   jax.ShapeDtypeStruct((nc, n_heads, CHUNK, CHUNK), F32)],
        scratch_shapes=[pltpu.VMEM((n_heads, hd, hd), F32)],
        compiler_params=_params(("arbitrary", "arbitrary")),
    )(qn, kn, vv, proj, proj, prm, nw)


def _chunk_bwd(qn, kn, vv, proj, z_off, ba_off, prm, nw, hist, tmats, dy, n_heads, pad):
    lp, dn = qn.shape
    hd = dn // n_heads
    nc = lp // CHUNK
    hb = min(HEADS_PER_STEP, n_heads)

    def body(q_ref, k_ref, v_ref, ba_ref, z_ref, prm_ref, nw_ref, hist_ref, tm_ref, dy_ref,
             dq_ref, dk_ref, dv_ref, dba_ref, dz_ref, dprm_ref, dnw_ref, dst_ref):
        step, g = pl.program_id(0), pl.program_id(1)

        @pl.when(step == 0)
        def _():
            for j in range(hb):
                dst_ref[g * hb + j] = jnp.zeros((hd, hd), F32)

        @pl.when((step == 0) & (g == 0))
        def _():
            dprm_ref[...] = jnp.zeros_like(dprm_ref)
            dnw_ref[...] = jnp.zeros_like(dnw_ref)

        @pl.when(g == 0)
        def _():
            dba_ref[...] = jnp.zeros_like(dba_ref)

        def fn(states, q, k, v, ba, z, prm_v, nw_v, known):
            return _chunk_math(states, q, k, v, ba, z, prm_v, nw_v, g * hb, _rowmask(nc - 1 - step, pad), n_heads, tmats=known)

        states = tuple(hist_ref[0, j] for j in range(hb))
        known = tuple(tm_ref[0, j] for j in range(hb))
        _, vjp = jax.vjp(fn, states, q_ref[...], k_ref[...], v_ref[...], ba_ref[...], z_ref[...], prm_ref[...], nw_ref[...], known)
        dst, dq, dk, dv, dba, dz, dprm, dnw, _ = vjp((dy_ref[...], tuple(dst_ref[g * hb + j] for j in range(hb))))
        for j in range(hb):
            dst_ref[g * hb + j] = dst[j]
        dq_ref[...] = dq
        dk_ref[...] = dk
        dv_ref[...] = dv
        dz_ref[...] = dz.astype(BF16)
        dba_ref[...] += dba
        dprm_ref[...] += dprm
        dnw_ref[...] += dnw

    blk, ba_spec, prm_spec, nw_spec, st_spec, z_spec = _chunk_specs(nc, hd, n_heads, z_off, ba_off, True)
    f32_full = jax.ShapeDtypeStruct((lp, dn), F32)
    tm_spec = pl.BlockSpec((1, hb, CHUNK, CHUNK), lambda c, g: (nc - 1 - c, g, 0, 0))
    return _call(
        body, name="chunk_bwd", grid=(nc, n_heads // hb),
        in_specs=[blk(0), blk(0), blk(0), ba_spec(ba_off), z_spec, prm_spec, nw_spec, st_spec, tm_spec, blk(0)],
        out_specs=[blk(0), blk(0), blk(0), ba_spec(0), blk(0), prm_spec, nw_spec],
        out_shape=[f32_full, f32_full, f32_full, jax.ShapeDtypeStruct((lp, LANES), F32), jax.ShapeDtypeStruct((lp, dn), BF16),
                   jax.ShapeDtypeStruct((8, LANES), F32), jax.ShapeDtypeStruct((1, hd), F32)],
        scratch_shapes=[pltpu.VMEM((n_heads, hd, hd), F32)],
        compiler_params=_params(("arbitrary", "arbitrary")),
    )(qn, kn, vv, proj, proj, prm, nw, hist, tmats, dy)


def _merge_math(p, q, gp, gd):
    return _sigmoid(gp) * p + _sigmoid(gd) * q


def _merge_specs(lp, d, gp_off, gd_off):
    tr, tc = _tile(lp, 264, 16), _tile(d, 1024, LANES)
    blk = pl.BlockSpec((tr, tc), lambda i, j: (i, j))
    gp_spec = pl.BlockSpec((tr, tc), lambda i, j: (i, gp_off // tc + j))
    gd_spec = pl.BlockSpec((tr, tc), lambda i, j: (i, gd_off // tc + j))
    return (lp // tr, d // tc), blk, gp_spec, gd_spec


def _merge_fwd(p, q, proj, gp_off, gd_off):
    lp, d = p.shape
    grid, blk, gp_spec, gd_spec = _merge_specs(lp, d, gp_off, gd_off)

    def body(p_ref, q_ref, gp_ref, gd_ref, o_ref):
        o_ref[...] = _merge_math(p_ref[...], q_ref[...], gp_ref[...], gd_ref[...]).astype(BF16)

    return _call(
        body, name="merge_fwd", grid=grid, in_specs=[blk, blk, gp_spec, gd_spec], out_specs=blk,
        out_shape=jax.ShapeDtypeStruct((lp, d), BF16), compiler_params=_params(("parallel", "parallel")),
    )(p, q, proj, proj)


def _merge_bwd(p, q, proj, gp_off, gd_off, dm):
    lp, d = p.shape
    grid, blk, gp_spec, gd_spec = _merge_specs(lp, d, gp_off, gd_off)

    def body(p_ref, q_ref, gp_ref, gd_ref, dm_ref, dp_ref, dq_ref, dgp_ref, dgd_ref):
        _, vjp = jax.vjp(_merge_math, p_ref[...], q_ref[...], gp_ref[...], gd_ref[...])
        for ref, val in zip((dp_ref, dq_ref, dgp_ref, dgd_ref), vjp(dm_ref[...])):
            ref[...] = val.astype(BF16)

    out = jax.ShapeDtypeStruct((lp, d), BF16)
    return _call(
        body, name="merge_bwd", grid=grid, in_specs=[blk, blk, gp_spec, gd_spec, blk], out_specs=[blk] * 4,
        out_shape=[out] * 4, compiler_params=_params(("parallel", "parallel")),
    )(p, q, proj, proj, dm)


def _adamw(w, g, m, v, name):
    shape = w.shape
    w2, g2, m2, v2 = (t.reshape((-1, shape[-1])) for t in (w, g, m, v))
    rows, cols = w2.shape
    tr = _tile(rows, 128, 8)

    def body(w_ref, g_ref, m_ref, v_ref, d_ref, nm_ref, nv_ref):
        gv = g_ref[...]
        nm = ADAM_B1 * m_ref[...] + (1.0 - ADAM_B1) * gv
        nv = ADAM_B2 * v_ref[...] + (1.0 - ADAM_B2) * (gv * gv)
        m_hat = nm / (1.0 - ADAM_B1 ** ADAM_STEP)
        v_hat = nv / (1.0 - ADAM_B2 ** ADAM_STEP)
        d_ref[...] = -ADAM_LR * (m_hat / (jnp.sqrt(v_hat) + ADAM_EPS) + ADAM_WD * w_ref[...])
        nm_ref[...] = nm
        nv_ref[...] = nv

    blk = pl.BlockSpec((tr, cols), lambda i: (i, 0))
    out = jax.ShapeDtypeStruct((rows, cols), F32)
    res = _call(
        body, name=name, grid=(rows // tr,), in_specs=[blk] * 4, out_specs=[blk] * 3, out_shape=[out] * 3,
        compiler_params=_params(("parallel",)),
    )(w2, g2, m2, v2)
    return tuple(t.reshape(shape) for t in res)


def _coords():
    return lax.axis_index("x"), lax.axis_index("y"), lax.axis_index("c")


def _flip(v, bit):
    return 1 - v if bit else v


CHIP_FLIPS = ((1, 0), (0, 1), (1, 1))
ANY = pl.BlockSpec(memory_space=pl.ANY)


def _all_gather(shards):
    n = len(shards)

    def body(*refs):
        x_refs, out_refs = refs[:n], refs[n:2 * n]
        send_sems, recv_sems, local_sems = refs[2 * n:]
        x, y, c = _coords()
        sibling = (x, y, 1 - c)
        chips = [(_flip(x, fx), _flip(y, fy)) for fx, fy in CHIP_FLIPS]

        def copy(a, k, block, to, from_input=False):
            px, py, pc = block
            slot = out_refs[a].at[4 * px + 2 * py + pc]
            return pltpu.make_async_remote_copy(
                src_ref=x_refs[a] if from_input else slot, dst_ref=slot,
                send_sem=send_sems.at[7 * a + k], recv_sem=recv_sems.at[7 * a + k], device_id=to, device_id_type=MESH)

        mine = [pltpu.make_async_copy(x_refs[a], out_refs[a].at[4 * x + 2 * y + c], local_sems.at[a]) for a in range(n)]
        first = []
        for a in range(n):
            mine[a].start()
            first.append(copy(a, 0, (x, y, c), sibling, True))
            first += [copy(a, 1 + j, (x, y, c), (*chip, c), True) for j, chip in enumerate(chips)]
        for cp in first:
            cp.start()
        passed = []
        for j, chip in enumerate(chips):
            for a in range(n):
                copy(a, 1 + j, (*chip, c), (x, y, c)).wait_recv()
                passed.append(copy(a, 4 + j, (*chip, c), sibling))
                passed[-1].start()
        for a in range(n):
            copy(a, 0, (x, y, 1 - c), (x, y, c)).wait_recv()
            for j, chip in enumerate(chips):
                copy(a, 4 + j, (*chip, 1 - c), (x, y, c)).wait_recv()
        for cp in first + passed:
            cp.wait_send()
        for cp in mine:
            cp.wait()

    return _call(
        body, name="all_gather", in_specs=[ANY] * n, out_specs=[ANY] * n,
        out_shape=[jax.ShapeDtypeStruct((N_DEV,) + s.shape, s.dtype) for s in shards],
        scratch_shapes=[pltpu.SemaphoreType.DMA((7 * n,)), pltpu.SemaphoreType.DMA((7 * n,)), pltpu.SemaphoreType.DMA((n,))],
    )(*shards)


def _all_gather_tree(shard):
    rows, cols = shard.shape
    half = rows // 2
    assert rows % 32 == 0

    def body(x_ref, out_ref, send_sems, recv_sems, local_sem):
        x, y, c = _coords()
        me, sibling = (x, y, c), (x, y, 1 - c)
        x_nbr, y_nbr, diag = (1 - x, y), (x, 1 - y), (1 - x, 1 - y)

        def part(ref, h):
            return ref if h is None else ref.at[pl.ds(h * half, half)]

        def copy(k, block, to, h=None, from_input=False):
            px, py, pc = block
            slot = part(out_ref.at[4 * px + 2 * py + pc], h)
            return pltpu.make_async_remote_copy(
                src_ref=part(x_ref, h) if from_input else slot, dst_ref=slot,
                send_sem=send_sems.at[k], recv_sem=recv_sems.at[k], device_id=to, device_id_type=MESH)

        mine = pltpu.make_async_copy(x_ref, out_ref.at[4 * x + 2 * y + c], local_sem)
        mine.start()
        started = [copy(0, me, sibling, None, True),
                   copy(1, me, (*x_nbr, c), 0, True), copy(2, me, (*x_nbr, c), 1, True),
                   copy(4, me, (*y_nbr, c), 1, True), copy(3, me, (*y_nbr, c), 0, True)]
        for cp in started:
            cp.start()
        copy(1, (*x_nbr, c), me, 0).wait_recv()
        started.append(copy(5, (*x_nbr, c), (*y_nbr, c), 0))
        started[-1].start()
        copy(4, (*y_nbr, c), me, 1).wait_recv()
        started.append(copy(6, (*y_nbr, c), (*x_nbr, c), 1))
        started[-1].start()
        copy(2, (*x_nbr, c), me, 1).wait_recv()
        started.append(copy(7, (*x_nbr, c), sibling))
        started[-1].start()
        copy(3, (*y_nbr, c), me, 0).wait_recv()
        started.append(copy(8, (*y_nbr, c), sibling))
        started[-1].start()
        copy(5, (*diag, c), me, 0).wait_recv()
        copy(6, (*diag, c), me, 1).wait_recv()
        started.append(copy(9, (*diag, c), sibling))
        started[-1].start()
        copy(0, sibling, me).wait_recv()
        for k, chip in ((7, x_nbr), (8, y_nbr), (9, diag)):
            copy(k, (*chip, 1 - c), me).wait_recv()
        for cp in started:
            cp.wait_send()
        mine.wait()

    return _call(
        body, name="all_gather_tree", in_specs=[ANY], out_specs=ANY,
        out_shape=jax.ShapeDtypeStruct((N_DEV, rows, cols), shard.dtype),
        scratch_shapes=[pltpu.SemaphoreType.DMA((10,)), pltpu.SemaphoreType.DMA((10,)), pltpu.SemaphoreType.DMA],
    )(shard)


def _rs_to_sibling(gs, name):
    n = len(gs)

    def body(*refs):
        g_refs, got_refs = refs[:n], refs[n:2 * n]
        send_sems, recv_sems = refs[2 * n:]
        x, y, c = _coords()
        copies = []
        for a in range(n):
            for p in range(4):
                cp = pltpu.make_async_remote_copy(
                    src_ref=g_refs[a].at[2 * p + (1 - c)], dst_ref=got_refs[a].at[p], send_sem=send_sems.at[4 * a + p],
                    recv_sem=recv_sems.at[4 * a + p], device_id=(x, y, 1 - c), device_id_type=MESH)
                cp.start()
                copies.append(cp)
        for cp in copies:
            cp.wait()

    return _call(
        body, name=name, in_specs=[ANY] * n, out_specs=[ANY] * n,
        out_shape=[jax.ShapeDtypeStruct((4,) + g.shape[1:], g.dtype) for g in gs],
        scratch_shapes=[pltpu.SemaphoreType.DMA((4 * n,)), pltpu.SemaphoreType.DMA((4 * n,))],
    )(*gs)


def _rs_pair_sum(g, got, c_idx, name):
    _, rows, cols = g.shape
    tr = _tile(rows, 256, 16)

    def body(c_ref, g_ref, got_ref, o_ref):
        o_ref[...] = (g_ref[...].astype(F32) + got_ref[...].astype(F32)).astype(o_ref.dtype)

    grid_spec = pltpu.PrefetchScalarGridSpec(
        num_scalar_prefetch=1, grid=(4, rows // tr),
        in_specs=[pl.BlockSpec((1, tr, cols), lambda p, i, c_ref: (2 * p + c_ref[0], i, 0)),
                  pl.BlockSpec((1, tr, cols), lambda p, i, c_ref: (p, i, 0))],
        out_specs=pl.BlockSpec((1, tr, cols), lambda p, i, c_ref: (p, i, 0)))
    return _call(
        body, name=name, grid_spec=grid_spec, out_shape=jax.ShapeDtypeStruct((4, rows, cols), g.dtype),
        compiler_params=_params(("parallel", "parallel")),
    )(c_idx, g, got)


def _to_chips_copies(p_refs, got_refs, send_sems, recv_sems):
    x, y, c = _coords()
    copies = []
    for a in range(len(p_refs)):
        for k, (fx, fy) in enumerate(CHIP_FLIPS):
            px, py = _flip(x, fx), _flip(y, fy)
            copies.append(pltpu.make_async_remote_copy(
                src_ref=p_refs[a].at[2 * px + py], dst_ref=got_refs[a].at[k], send_sem=send_sems.at[3 * a + k],
                recv_sem=recv_sems.at[3 * a + k], device_id=(px, py, c), device_id_type=MESH))
    return copies


def _rs_to_chips(partials, name):
    n = len(partials)

    def body(*refs):
        copies = _to_chips_copies(refs[:n], refs[n:2 * n], *refs[2 * n:])
        for cp in copies:
            cp.start()
        for cp in copies:
            cp.wait()

    return _call(
        body, name=name, in_specs=[ANY] * n, out_specs=[ANY] * n,
        out_shape=[jax.ShapeDtypeStruct((3,) + p.shape[1:], p.dtype) for p in partials],
        scratch_shapes=[pltpu.SemaphoreType.DMA((3 * n,)), pltpu.SemaphoreType.DMA((3 * n,))],
    )(*partials)


HBM = pl.BlockSpec(memory_space=pltpu.HBM)
SEM = pl.BlockSpec(memory_space=pltpu.SEMAPHORE)
SIDE_EFFECT = pltpu.CompilerParams(has_side_effects=pltpu.SideEffectType.DATAFLOW_SIDE_EFFECTING)


def _split_start(copies_fn, srcs, land_shapes, n_sems, name, after=None):
    n, m = len(srcs), len(land_shapes)
    extra = [] if after is None else [after]

    def body(*refs):
        outs = refs[n + m + len(extra):]
        send_sems, recv_sems, token = outs[0], outs[1], outs[-1]
        for cp in copies_fn(refs[:n], refs[n:n + m], send_sems, recv_sems):
            cp.start()
        token[...] = jnp.zeros_like(token)

    ins = [pltpu.with_memory_space_constraint(t, pltpu.HBM) for t in list(srcs) + [lax.empty(s.shape, s.dtype) for s in land_shapes]]
    res = _call(
        body, name=name, in_specs=[HBM] * (n + m) + [ANY] * len(extra),
        out_specs=[SEM, SEM] + [HBM] * (n + m) + [pl.BlockSpec(memory_space=pltpu.VMEM)],
        out_shape=[pltpu.SemaphoreType.DMA((n_sems,)), pltpu.SemaphoreType.DMA((n_sems,))]
        + [pltpu.HBM(t.shape, t.dtype) for t in ins] + [jax.ShapeDtypeStruct((8, LANES), F32)],
        input_output_aliases={i: 2 + i for i in range(n + m)}, compiler_params=SIDE_EFFECT,
    )(*ins, *extra)
    return dict(sems=(res[0], res[1]), srcs=res[2:2 + n], lands=res[2 + n:2 + n + m], token=res[-1])


def _split_wait(copies_fn, started, after, name):
    n, m = len(started["srcs"]), len(started["lands"])

    def body(*refs):
        for cp in copies_fn(refs[:n], refs[n:n + m], refs[n + m], refs[n + m + 1]):
            cp.wait_send()
            cp.wait_recv()

    bufs = list(started["srcs"]) + list(started["lands"])
    res = _call(
        body, name=name, in_specs=[HBM] * (n + m) + [SEM, SEM, ANY], out_specs=[HBM] * (n + m),
        out_shape=[pltpu.HBM(t.shape, t.dtype) for t in bufs],
        input_output_aliases={i: i for i in range(n + m)}, compiler_params=SIDE_EFFECT,
    )(*bufs, *started["sems"], after)
    return res[:n], res[n:]


def _to_all_copies(x_refs, out_refs, send_sems, recv_sems):
    x, y, c = _coords()
    copies = []
    for a in range(len(x_refs)):
        for k in range(N_DEV - 1):
            fx, fy, fc = ((k + 1) >> 2) & 1, ((k + 1) >> 1) & 1, (k + 1) & 1
            copies.append(pltpu.make_async_remote_copy(
                src_ref=x_refs[a], dst_ref=out_refs[a].at[4 * x + 2 * y + c], send_sem=send_sems.at[7 * a + k],
                recv_sem=recv_sems.at[7 * a + k], device_id=(_flip(x, fx), _flip(y, fy), _flip(c, fc)), device_id_type=MESH))
    return copies


def _fill_own_block(gathered, shard, me_idx, name):
    rows, cols = shard.shape
    tr = _tile(rows, 512, 16)

    def body(me_ref, g_ref, s_ref, o_ref):
        o_ref[0] = s_ref[...]

    grid_spec = pltpu.PrefetchScalarGridSpec(
        num_scalar_prefetch=1, grid=(rows // tr,),
        in_specs=[ANY, pl.BlockSpec((tr, cols), lambda i, me: (i, 0))],
        out_specs=pl.BlockSpec((1, tr, cols), lambda i, me: (me[0], i, 0)))
    return _call(
        body, name=name, grid_spec=grid_spec, out_shape=jax.ShapeDtypeStruct(gathered.shape, gathered.dtype),
        input_output_aliases={1: 0}, compiler_params=_params(("arbitrary",)),
    )(me_idx, gathered, shard)


def _rs_chip_sum(partial, got, chip_idx, name, part=0, n_parts=1, dst=None):
    _, rows, cols = partial.shape
    tr = _tile(rows, 256, 16)
    steps = rows // tr
    n_dst = 0 if dst is None else 1

    def body(p_idx_ref, p_ref, got_ref, *refs):
        refs[n_dst][...] = ((p_ref[0].astype(F32) + got_ref[0].astype(F32)) + got_ref[1].astype(F32)) + got_ref[2].astype(F32)

    grid_spec = pltpu.PrefetchScalarGridSpec(
        num_scalar_prefetch=1, grid=(steps,),
        in_specs=[pl.BlockSpec((1, tr, cols), lambda i, p_ref: (p_ref[0], i, 0)),
                  pl.BlockSpec((3, tr, cols), lambda i, p_ref: (0, i, 0))] + [ANY] * n_dst,
        out_specs=pl.BlockSpec((tr, cols), lambda i, p_ref: (part * steps + i, 0)))
    return _call(
        body, name=name, grid_spec=grid_spec, out_shape=jax.ShapeDtypeStruct((n_parts * rows, cols), F32),
        input_output_aliases={3: 0} if n_dst else {}, compiler_params=_params(("parallel",)),
    )(chip_idx, partial, got, *([] if dst is None else [dst]))


def _rs_begin(gs, tag, split):
    c_idx = jnp.reshape(lax.axis_index("c"), (1,)).astype(jnp.int32)
    gots = _rs_to_sibling(gs, "rs_to_sibling_" + tag)
    partials = [_rs_pair_sum(g, got, c_idx, "rs_pair_sum_%s%d" % (tag, a)) for a, (g, got) in enumerate(zip(gs, gots))]
    if not split:
        return dict(partials=partials, gots=_rs_to_chips(partials, "rs_to_chips_" + tag))
    lands = [jax.ShapeDtypeStruct((3,) + p.shape[1:], p.dtype) for p in partials]
    return _split_start(_to_chips_copies, partials, lands, 3 * len(partials), "rs_to_chips_start_" + tag)


def _rs_finish(begun, tag, after=None, part=0, n_parts=1, dsts=None):
    x, y, _ = _coords()
    chip_idx = jnp.reshape(2 * x + y, (1,)).astype(jnp.int32)
    if "gots" in begun:
        partials, gots = begun["partials"], begun["gots"]
    else:
        partials, gots = _split_wait(_to_chips_copies, begun, after, "rs_to_chips_wait_" + tag)
    return [_rs_chip_sum(p, got, chip_idx, "rs_chip_sum_%s%d" % (tag, a), part, n_parts, None if dsts is None else dsts[a])
            for a, (p, got) in enumerate(zip(partials, gots))]


RUNS = 3
RUN_FIELDS = 6


def _lane_gather_table(src_of, src_width):
    n_blocks = src_of.shape[0] // LANES
    tab = np.zeros((n_blocks + 1, RUNS, RUN_FIELDS), np.int32)
    tab[:, :, 5] = LANES
    for t in range(n_blocks):
        runs = []
        for lane in range(LANES):
            slab, col = (int(v) for v in src_of[t * LANES + lane])
            if slab < 0:
                continue
            key = (slab, col // LANES, col % LANES - lane)
            if runs and runs[-1][0] == key and runs[-1][2] == lane:
                runs[-1][2] = lane + 1
            else:
                runs.append([key, lane, lane + 1])
        assert len(runs) <= RUNS
        slots = [None] * RUNS
        for key, lo, hi in sorted(runs, key=lambda r: r[0][:2]):
            e = key[1] % 2 if slots[key[1] % 2] is None else slots.index(None)
            slots[e] = (key[0], key[1], key[2], lo, hi, min(LANES, src_width - key[1] * LANES))
        for e in range(RUNS):
            tab[t, e] = slots[e] if slots[e] is not None else (tab[t - 1, e, 0], tab[t - 1, e, 1], 0, 0, 0, LANES) if t else tab[t, e]
    tab[n_blocks, :, :2] = tab[n_blocks - 1, :, :2]
    return tab.reshape(-1)


def _gathered_block(tab_ref, t, load):
    lane = lax.broadcasted_iota(jnp.int32, (1, LANES), 1)
    out = None
    for e in range(RUNS):
        base = (t * RUNS + e) * RUN_FIELDS
        slab, blk, shift, lo, hi = (tab_ref[base + i] for i in range(5))
        turned = pltpu.roll(load(e, slab, blk).astype(F32), (LANES - shift) % LANES, 1)
        out = jnp.where((lane >= lo) & (lane < hi), turned, 0.0 if out is None else out)
    return out.astype(BF16)


def _lane_gather_cols(src, table, out_slabs, out_width, name):
    _, rows, _ = src.shape
    blocks_per_slab = -(-out_width // LANES)

    def body(tab_ref, *refs):
        refs[RUNS][0] = _gathered_block(tab_ref, pl.program_id(0), lambda e, slab, blk: refs[e][0])

    def src_spec(e):
        return pl.BlockSpec((1, rows, LANES), lambda t, tab: (tab[(t * RUNS + e) * RUN_FIELDS], 0, tab[(t * RUNS + e) * RUN_FIELDS + 1]))

    grid_spec = pltpu.PrefetchScalarGridSpec(
        num_scalar_prefetch=1, grid=(out_slabs * blocks_per_slab,), in_specs=[src_spec(e) for e in range(RUNS)],
        out_specs=pl.BlockSpec((1, rows, LANES), lambda t, tab: (t // blocks_per_slab, 0, t % blocks_per_slab)))
    return _call(
        body, name=name, grid_spec=grid_spec, out_shape=jax.ShapeDtypeStruct((out_slabs, rows, out_width), BF16),
        compiler_params=_params(("arbitrary",)),
    )(jnp.asarray(table), src, src, src)


def _all_reduce_small(vec):
    rows, cols = vec.shape

    def body(v_ref, o_ref, buf, send_sems, recv_sems):
        x, y, c = _coords()
        me = 4 * x + 2 * y + c
        buf[me] = v_ref[...]
        copies = []
        for k in range(N_DEV - 1):
            fx, fy, fc = ((k + 1) >> 2) & 1, ((k + 1) >> 1) & 1, (k + 1) & 1
            cp = pltpu.make_async_remote_copy(
                src_ref=v_ref, dst_ref=buf.at[me], send_sem=send_sems.at[k], recv_sem=recv_sems.at[k],
                device_id=(_flip(x, fx), _flip(y, fy), _flip(c, fc)), device_id_type=MESH)
            cp.start()
            copies.append(cp)
        for cp in copies:
            cp.wait()
        total = buf[0]
        for j in range(1, N_DEV):
            total = total + buf[j]
        o_ref[...] = total

    vmem = pl.BlockSpec(memory_space=pltpu.VMEM)
    return _call(
        body, name="all_reduce_small", in_specs=[vmem], out_specs=vmem,
        out_shape=jax.ShapeDtypeStruct((rows, cols), F32),
        scratch_shapes=[pltpu.VMEM((N_DEV, rows, cols), F32), pltpu.SemaphoreType.DMA((N_DEV - 1,)),
                        pltpu.SemaphoreType.DMA((N_DEV - 1,))],
    )(vec)


def _w_in_column_maps(ns, o_ba, n_logit, n_main, n_all):
    own = np.arange(N_DEV * ns)
    work_of_own = np.where(own < o_ba, own, np.where(own < o_ba + n_logit, n_main + own - o_ba, own - n_logit))
    to_work = np.full((n_all, 2), -1, np.int64)
    to_work[work_of_own, 0] = own // ns
    to_work[work_of_own, 1] = own % ns
    slab_width = -(-ns // LANES) * LANES
    to_own = np.full((N_DEV, slab_width, 2), -1, np.int64)
    to_own[:, :ns, 0] = 0
    to_own[:, :ns, 1] = work_of_own.reshape(N_DEV, ns)
    return to_work, to_own.reshape(-1, 2)


def kernel(x, meta_tokens, norm_w, w_in, conv_w, A_log, dt_bias, pool_mix, pool_scale, dn_norm_w, w_pool_out, w_dn_out, w_o, final_norm_w, loss_target, m_meta_tokens, m_norm_w, m_w_in, m_conv_w, m_A_log, m_dt_bias, m_pool_mix, m_pool_scale, m_dn_norm_w, m_w_pool_out, m_w_dn_out, m_w_o, m_final_norm_w, v_meta_tokens, v_norm_w, v_w_in, v_conv_w, v_A_log, v_dt_bias, v_pool_mix, v_pool_scale, v_dn_norm_w, v_w_pool_out, v_w_dn_out, v_w_o, v_final_norm_w):
    seq, d = x.shape[1], x.shape[2]
    n_meta = meta_tokens.shape[0]
    n_heads, hd = A_log.shape[-1], dn_norm_w.shape[-1]
    dn = n_heads * hd
    pw, ng = pool_scale.shape[-1], pool_mix.shape[1]
    pg = pw // ng
    kw = conv_w.shape[1]
    pad = (-n_meta) % CHUNK
    x0 = pad + n_meta
    lp = x0 + seq
    ns = w_in.shape[-1]
    in_cols = N_DEV * ns
    o_q, o_k, o_v, o_zd = 2 * pw, 2 * pw + dn, 2 * pw + 2 * dn, 2 * pw + 3 * dn
    o_ba = 2 * pw + 4 * dn
    o_gp, o_gd = o_ba, o_ba + d
    n_main = o_gd + d
    n_all = n_main + 2 * LANES
    assert lp % CHUNK == 0 and in_cols == n_main + 2 * n_heads and 2 * n_heads <= LANES and hd == LANES
    cs, ms = conv_w.shape[-1], meta_tokens.shape[-1]
    mr = pool_mix.shape[2]
    assert ms == pg and cs % pg == 0
    to_work, to_own = _w_in_column_maps(ns, o_ba, 2 * n_heads, n_main, n_all)
    cols_major = lambda t: jnp.transpose(t, (1, 0, 2)).reshape(t.shape[1], N_DEV * t.shape[2])

    mix_g, conv_g, meta_g = _all_gather([pool_mix[0].reshape(ng * mr, pg).astype(BF16), conv_w[0], meta_tokens])
    win_g = _all_gather_tree(w_in[0].astype(BF16))
    late_shards = [w_pool_out[0].astype(BF16), w_dn_out[0].astype(BF16), w_o[0].astype(BF16)]
    late_weights = _split_start(_to_all_copies, late_shards, [jax.ShapeDtypeStruct((N_DEV,) + s.shape, BF16) for s in late_shards],
                                (N_DEV - 1) * len(late_shards), "gather_out_proj_start", after=win_g)
    norm_w_in = norm_w + late_weights["token"][0, 0]
    w_all = _lane_gather_cols(win_g, _lane_gather_table(to_work, ns), 1, n_all, "w_in_to_work").reshape(d, n_all)
    mix_f = jnp.transpose(mix_g.reshape(N_DEV, ng, mr, pg), (1, 0, 2, 3)).reshape(ng, pg, pg)
    conv_f = cols_major(conv_g)
    meta_f = cols_major(meta_g)

    h0, xn = _norm_in_fwd(x[0], meta_f, norm_w_in, pad)
    proj = _matmul(xn, w_all, NN, F32, lp, 768, 2048, "proj")
    y_pool = _pool_fwd(proj, mix_f, pool_scale, pad)
    conv_q, conv_k, conv_v = (conv_f[:, i * dn:(i + 1) * dn] for i in range(3))
    qn = _conv_fwd(proj, o_q, conv_q, hd, float(hd) ** -0.5, "conv_q_fwd")
    kn = _conv_fwd(proj, o_k, conv_k, hd, 1.0, "conv_k_fwd")
    vv = _conv_fwd(proj, o_v, conv_v, hd, None, "conv_v_fwd")
    logit_lanes = (n_heads, LANES - 2 * n_heads)
    prm = jnp.pad(A_log, ((0, 7), logit_lanes)) + jnp.pad(dt_bias, ((1, 6), logit_lanes))
    y_dn, hist, tmats = _chunk_fwd(qn, kn, vv, proj, o_zd, n_main, prm, dn_norm_w, n_heads, pad)
    me_idx = jnp.reshape(4 * lax.axis_index("x") + 2 * lax.axis_index("y") + lax.axis_index("c"), (1,)).astype(jnp.int32)
    _, landed = _split_wait(_to_all_copies, late_weights, y_dn, "gather_out_proj_wait")
    wpo_g, wdn_g, wo_g = (_fill_own_block(g, s, me_idx, "own_block_%d" % i) for i, (g, s) in enumerate(zip(landed, late_shards)))
    wpo_f = cols_major(wpo_g)
    wdn_f = wdn_g.reshape(dn, d)
    wo_f = wo_g.reshape(d, d)
    p_out = _matmul(y_pool, wpo_f, NN, F32, 1056, 1024, 1024, "pool_out")
    q_out = _matmul(y_dn, wdn_f, NN, F32, 1056, 1024, 2048, "dn_out")
    merged = _merge_fwd(p_out, q_out, proj, o_gp, o_gd)
    mo = _matmul(merged, wo_f, NN, F32, 1056, 1024, 2048, "w_o_fwd")
    dh1, d_fw, loss_part = _final_loss(h0, mo, final_norm_w.reshape(1, d), loss_target[0], x0)

    d_merged = _matmul(dh1, wo_f, NT, F32, 1056, 1024, 1024, "w_o_bwd_x")
    g_wo = _matmul(merged.T, dh1, NN, BF16, 1024, 1024, lp, "w_o_bwd_w")
    d_p, d_q, d_gp, d_gd = _merge_bwd(p_out, q_out, proj, o_gp, o_gd, d_merged)
    d_ypool = _matmul(d_p, wpo_f, NT, F32, 1056, 1024, 2048, "pool_out_bwd_x")
    g_wpo = _matmul(y_pool.T, d_p, NN, BF16, 1024, 1024, lp, "pool_out_bwd_w", col_blocks=N_DEV)
    d_ydn = _matmul(d_q, wdn_f, NT, F32, 1056, 1024, 2048, "dn_out_bwd_x")
    g_wdn = _matmul(y_dn.T, d_q, NN, BF16, 1024, 1024, lp, "dn_out_bwd_w")
    rs_early = _rs_begin([g_wpo, g_wdn.reshape(N_DEV, dn // N_DEV, d), g_wo.reshape(N_DEV, d // N_DEV, d)], "early", split=True)
    started = rs_early["token"][0, 0]
    d_u, d_zp, g_mix, g_pscale = _pool_bwd(proj, mix_f, pool_scale + started, d_ypool, pad)
    d_qn, d_kn, d_vv, d_ba, d_zd, d_prm, g_dnw = _chunk_bwd(qn, kn, vv, proj, o_zd, n_main, prm + started, dn_norm_w, hist, tmats, d_ydn, n_heads, pad)
    d_qr, g_cq = _conv_bwd(proj, o_q, conv_q, d_qn, hd, float(hd) ** -0.5, pad, "conv_q_bwd")
    d_kr, g_ck = _conv_bwd(proj, o_k, conv_k, d_kn, hd, 1.0, pad, "conv_k_bwd")
    d_vr, g_cv = _conv_bwd(proj, o_v, conv_v, d_vv, hd, None, pad, "conv_v_bwd")
    d_proj = jnp.concatenate([d_u, d_zp, d_qr, d_kr, d_vr, d_zd, d_gp, d_gd, d_ba.astype(BF16), jnp.zeros((lp, LANES), BF16)], axis=1)
    xn_t, rs_late, token = xn.T, [], None
    for half in range(2):
        rows = slice(half * (d // 2), (half + 1) * (d // 2))
        g_wall = _matmul(xn_t[rows], d_proj, NN, F32, 1024, 768, lp, "w_in_bwd_w_%d" % half, after=token)
        g_win = _lane_gather_cols(g_wall.reshape(1, d // 2, n_all), _lane_gather_table(to_own, n_all), N_DEV, ns, "w_in_grad_to_own_%d" % half)
        rs_late.append(_rs_begin([g_win], "late%d" % half, split=True))
        token = rs_late[-1]["token"]
    d_xn = _matmul(d_proj, w_all, NT, F32, lp, 512, 768, "w_in_bwd_x", after=token)
    d_head, grad_x, g_nw = _norm_in_bwd(h0, norm_w, d_xn, dh1, x0)
    grad_x = grad_x[None]

    by_cols = lambda t: jnp.transpose(t.reshape(t.shape[0], N_DEV, t.shape[1] // N_DEV), (1, 0, 2))
    g_conv = by_cols(jnp.concatenate([g_cq, g_ck, g_cv], axis=1)).reshape(N_DEV, kw * cs // pg, pg)
    conv_rows = -(-g_conv.shape[1] // 16) * 16
    g_small = jnp.concatenate(
        [jnp.transpose(g_mix.reshape(ng, N_DEV, mr, pg), (1, 0, 2, 3)).reshape(N_DEV, ng * mr, pg), by_cols(d_head[pad:x0]),
         jnp.pad(g_conv, ((0, 0), (0, conv_rows - g_conv.shape[1]), (0, 0)))], axis=1).astype(BF16)
    r_small, = _rs_finish(_rs_begin([g_small], "small", split=False), "small")
    r_mix, r_meta = r_small[:ng * mr], r_small[ng * mr:ng * mr + n_meta]
    r_conv = r_small[ng * mr + n_meta:ng * mr + n_meta + kw * cs // pg]
    r_wpo, r_wdn, r_wo = _rs_finish(rs_early, "early", after=r_small)

    small = [g_nw[0], d_fw[0], g_pscale[0], g_dnw[0], d_prm[0], d_prm[1], loss_part[0]]
    s_sizes = [t.shape[0] for t in small]
    s_cols = -(-sum(s_sizes) // (8 * LANES)) * LANES
    s_vec = jnp.concatenate(small + [jnp.zeros((8 * s_cols - sum(s_sizes),), F32)]).reshape(8, s_cols)
    s_red = _all_reduce_small(s_vec)
    s_sum = s_red.reshape(-1)
    r_win = None
    for half, begun in enumerate(rs_late):
        r_win = _rs_finish(begun, "late%d" % half, after=s_red, part=half, n_parts=2, dsts=r_win)
    r_win, = r_win
    s_offs = [sum(s_sizes[:i]) for i in range(len(s_sizes))]
    s_take = lambda i, n=None, o=0: s_sum[s_offs[i] + o:s_offs[i] + o + (s_sizes[i] if n is None else n)]

    grads = {
        "meta_tokens": r_meta, "norm_w": s_take(0).reshape(norm_w.shape),
        "w_in": r_win.reshape(w_in.shape), "conv_w": r_conv.reshape(conv_w.shape),
        "A_log": s_take(4, n_heads, n_heads).reshape(A_log.shape), "dt_bias": s_take(5, n_heads, n_heads).reshape(dt_bias.shape),
        "pool_mix": r_mix.reshape(pool_mix.shape), "pool_scale": s_take(2).reshape(pool_scale.shape),
        "dn_norm_w": s_take(3).reshape(dn_norm_w.shape), "w_pool_out": r_wpo.reshape(w_pool_out.shape),
        "w_dn_out": r_wdn.reshape(w_dn_out.shape), "w_o": r_wo.reshape(w_o.shape),
        "final_norm_w": s_take(1).reshape(final_norm_w.shape),
    }
    loss = s_take(6, 1)[0]

    weights = dict(meta_tokens=meta_tokens, norm_w=norm_w, w_in=w_in, conv_w=conv_w, A_log=A_log, dt_bias=dt_bias,
                   pool_mix=pool_mix, pool_scale=pool_scale, dn_norm_w=dn_norm_w, w_pool_out=w_pool_out, w_dn_out=w_dn_out,
                   w_o=w_o, final_norm_w=final_norm_w)
    m_in = dict(meta_tokens=m_meta_tokens, norm_w=m_norm_w, w_in=m_w_in, conv_w=m_conv_w, A_log=m_A_log, dt_bias=m_dt_bias,
                pool_mix=m_pool_mix, pool_scale=m_pool_scale, dn_norm_w=m_dn_norm_w, w_pool_out=m_w_pool_out,
                w_dn_out=m_w_dn_out, w_o=m_w_o, final_norm_w=m_final_norm_w)
    v_in = dict(meta_tokens=v_meta_tokens, norm_w=v_norm_w, w_in=v_w_in, conv_w=v_conv_w, A_log=v_A_log, dt_bias=v_dt_bias,
                pool_mix=v_pool_mix, pool_scale=v_pool_scale, dn_norm_w=v_dn_norm_w, w_pool_out=v_w_pool_out,
                w_dn_out=v_w_dn_out, w_o=v_w_o, final_norm_w=v_final_norm_w)
    names = list(weights)
    upd = {n: _adamw(weights[n], grads[n], m_in[n], v_in[n], "adamw_" + n) for n in names}
    return (loss, grad_x, *[grads[n] for n in names], *[upd[n][0] for n in names], *[upd[n][1] for n in names],
            *[upd[n][2] for n in names])
```

```python
import functools

import jax
import jax.numpy as jnp
import numpy as np
from jax import lax
from jax.experimental import pallas as pl
from jax.experimental.pallas import tpu as pltpu

F32 = jnp.float32
BF16 = jnp.bfloat16
HIGHEST = lax.Precision.HIGHEST
MESH = pl.DeviceIdType.MESH

CHUNK = 64
NORM_EPS = 1e-6
POOL_WINDOWS = (2, 4, 8, 16)
ADAM_LR, ADAM_B1, ADAM_B2, ADAM_EPS, ADAM_WD, ADAM_STEP = 0.001, 0.9, 0.999, 1e-08, 0.01, 10
N_DEV = 8
LANES = 128
VMEM_LIMIT = 48 * 1024 * 1024

NN = (((1,), (0,)), ((), ()))
NT = (((1,), (1,)), ((), ()))
TN = (((0,), (0,)), ((), ()))


def _call(body, **kw):
    return pl.pallas_call(body, **kw)


def _params(sem=None):
    return pltpu.CompilerParams(dimension_semantics=sem, vmem_limit_bytes=VMEM_LIMIT)


def _tile(n, pref, align):
    for d in range(min(pref, n), 0, -1):
        if n % d == 0 and d % align == 0:
            return d
    return n


def _dot(a, b, dims=NN, precision=None):
    return lax.dot_general(a, b, dims, precision=precision, preferred_element_type=F32)


def _sigmoid(x):
    return 1.0 / (1.0 + jnp.exp(-x))


def _silu(x):
    return x * _sigmoid(x)


def _softplus(x):
    return jnp.maximum(x, 0.0) + jnp.log(1.0 + jnp.exp(-jnp.abs(x)))


def _rmsnorm(x, w):
    return x * lax.rsqrt(jnp.mean(x * x, axis=-1, keepdims=True) + NORM_EPS) * w


def _shift_down(x, j, row):
    if j == 0:
        return x
    return jnp.where(row >= j, pltpu.roll(x, j, 0), 0.0)


def _shift_up(x, j, row):
    if j == 0:
        return x
    n = x.shape[0]
    return jnp.where(row < n - j, pltpu.roll(x, n - j, 0), 0.0)


def _matmul(a, b, dims, out_dtype, tm, tn, tk, name, col_blocks=None, after=None):
    ta = dims == TN
    tb = dims == NT
    m, kdim = (a.shape[1], a.shape[0]) if ta else a.shape
    n = b.shape[0] if tb else b.shape[1]
    if col_blocks:
        tn = n // col_blocks
    tm, tn, tk = _tile(m, tm, 8), _tile(n, tn, LANES), _tile(kdim, tk, LANES if not ta else 16)
    nk = kdim // tk

    n_extra = 0 if after is None else 1

    def body(a_ref, b_ref, *refs):
        o_ref, scratch = refs[n_extra], refs[n_extra + 1:]
        part = _dot(a_ref[...].astype(BF16), b_ref[...].astype(BF16), dims)
        if nk == 1:
            o_ref[...] = part.astype(o_ref.dtype).reshape(o_ref.shape)
            return
        acc_ref, = scratch
        k = pl.program_id(2)

        @pl.when(k == 0)
        def _():
            acc_ref[...] = part

        @pl.when(k > 0)
        def _():
            acc_ref[...] += part

        @pl.when(k == nk - 1)
        def _():
            o_ref[...] = acc_ref[...].astype(o_ref.dtype).reshape(o_ref.shape)

    a_spec = pl.BlockSpec((tk, tm), lambda i, j, k: (k, i)) if ta else pl.BlockSpec((tm, tk), lambda i, j, k: (i, k))
    b_spec = pl.BlockSpec((tn, tk), lambda i, j, k: (j, k)) if tb else pl.BlockSpec((tk, tn), lambda i, j, k: (k, j))
    if col_blocks:
        out_spec = pl.BlockSpec((1, tm, tn), lambda i, j, k: (j, i, 0))
        out_shape = jax.ShapeDtypeStruct((col_blocks, m, tn), out_dtype)
    else:
        out_spec = pl.BlockSpec((tm, tn), lambda i, j, k: (i, j))
        out_shape = jax.ShapeDtypeStruct((m, n), out_dtype)
    return _call(
        body, name=name, grid=(m // tm, n // tn, nk),
        in_specs=[a_spec, b_spec] + [ANY] * n_extra, out_specs=out_spec, out_shape=out_shape,
        scratch_shapes=[] if nk == 1 else [pltpu.VMEM((tm, tn), F32)],
        compiler_params=_params(("parallel", "parallel", "arbitrary")),
    )(a, b, *([] if after is None else [after]))


def _norm_in_fwd(x2d, meta, w, pad):
    seq, d = x2d.shape
    tr = pad + meta.shape[0]
    assert seq % tr == 0 and tr % 16 == 0
    lp = tr + seq

    def body(x_ref, m_ref, w_ref, h_ref, o_ref):
        def emit(h):
            h_ref[...] = h
            o_ref[...] = _rmsnorm(h, w_ref[...]).astype(BF16)

        @pl.when(pl.program_id(0) == 0)
        def _():
            emit(jnp.concatenate([jnp.zeros((pad, d), F32), m_ref[...]], axis=0) if pad else m_ref[...])

        @pl.when(pl.program_id(0) > 0)
        def _():
            emit(x_ref[...])

    row = pl.BlockSpec((tr, d), lambda i: (i, 0))
    return _call(
        body, name="norm_in_fwd", grid=(lp // tr,),
        in_specs=[pl.BlockSpec((tr, d), lambda i: (jnp.maximum(i - 1, 0), 0)), pl.BlockSpec(meta.shape, lambda i: (0, 0)),
                  pl.BlockSpec((1, d), lambda i: (0, 0))],
        out_specs=[row, row],
        out_shape=[jax.ShapeDtypeStruct((lp, d), F32), jax.ShapeDtypeStruct((lp, d), BF16)],
        compiler_params=_params(("arbitrary",)),
    )(x2d, meta, w)


def _norm_in_bwd(h0, w, dxn, dh1, x0):
    lp, d = h0.shape
    tr = x0
    assert lp % tr == 0

    def body(h_ref, w_ref, da_ref, dh1_ref, head_ref, gx_ref, dw_ref):
        i = pl.program_id(0)
        _, vjp = jax.vjp(_rmsnorm, h_ref[...], w_ref[...])
        dh, dw = vjp(da_ref[...])
        dh = dh + dh1_ref[...]

        @pl.when(i == 0)
        def _():
            head_ref[...] = dh
            dw_ref[...] = dw

        @pl.when(i > 0)
        def _():
            gx_ref[...] = dh
            dw_ref[...] += dw

    row = pl.BlockSpec((tr, d), lambda i: (i, 0))
    vec = pl.BlockSpec((1, d), lambda i: (0, 0))
    return _call(
        body, name="norm_in_bwd", grid=(lp // tr,),
        in_specs=[row, vec, row, row],
        out_specs=[pl.BlockSpec((tr, d), lambda i: (0, 0)), pl.BlockSpec((tr, d), lambda i: (jnp.maximum(i - 1, 0), 0)), vec],
        out_shape=[jax.ShapeDtypeStruct((tr, d), F32), jax.ShapeDtypeStruct((lp - tr, d), F32), jax.ShapeDtypeStruct((1, d), F32)],
        compiler_params=_params(("arbitrary",)),
    )(h0, w, dxn, dh1)


def _final_loss(h0, mo, fw, tgt, x0):
    lp, d = h0.shape
    tr = x0
    assert lp % tr == 0

    def body(h_ref, mo_ref, fw_ref, t_ref, dh_ref, dw_ref, loss_ref):
        i = pl.program_id(0)
        row = i * tr + lax.broadcasted_iota(jnp.int32, (tr, 1), 0)
        mask = jnp.where(row >= x0, 1.0, 0.0).astype(F32)
        tgt_v = t_ref[...]

        def loss_fn(h1, w):
            err = _rmsnorm(h1, w) - tgt_v
            return 0.5 * jnp.sum(jnp.mean(err * err, axis=-1, keepdims=True) * mask, axis=0, keepdims=True)

        loss, vjp = jax.vjp(loss_fn, h_ref[...] + mo_ref[...], fw_ref[...])
        dh, dw = vjp(jnp.ones((1, 1), F32))
        dh_ref[...] = dh

        @pl.when(i == 0)
        def _():
            dw_ref[...] = jnp.zeros_like(dw_ref)
            loss_ref[...] = jnp.zeros_like(loss_ref)

        dw_ref[...] += dw
        loss_ref[...] += jnp.broadcast_to(loss, loss_ref.shape)

    row_spec = pl.BlockSpec((tr, d), lambda i: (i, 0))
    vec = pl.BlockSpec((1, d), lambda i: (0, 0))
    return _call(
        body, name="final_loss", grid=(lp // tr,),
        in_specs=[row_spec, row_spec, vec, pl.BlockSpec((tr, d), lambda i: (jnp.maximum(i - 1, 0), 0))],
        out_specs=[row_spec, vec, pl.BlockSpec((8, LANES), lambda i: (0, 0))],
        out_shape=[jax.ShapeDtypeStruct((lp, d), F32), jax.ShapeDtypeStruct((1, d), F32), jax.ShapeDtypeStruct((8, LANES), F32)],
        compiler_params=_params(("arbitrary",)),
    )(h0, mo, fw, tgt)


def _pool_select(parts, g):
    out = parts[-1]
    for gi in range(len(parts) - 2, -1, -1):
        out = jnp.where(g == gi, parts[gi], out)
    return out


def _pool_count(row, g, pad):
    win = _pool_select([jnp.full(row.shape, float(w), F32) for w in POOL_WINDOWS], g)
    return jnp.maximum(jnp.minimum((row - pad + 1).astype(F32), win), 1.0)


def _pooled(u, g, row, pad):
    sums, s, span = [], u, 1
    for w in POOL_WINDOWS:
        while span < w:
            s = s + _shift_down(s, span, row)
            span *= 2
        sums.append(s)
    return _pool_select(sums, g) / _pool_count(row, g, pad) - u


def _pooled_adjoint(dp, g, row, pad):
    e = dp / _pool_count(row, g, pad)
    sums, s, span = [], e, 1
    for w in POOL_WINDOWS:
        while span < w:
            s = s + _shift_up(s, span, row)
            span *= 2
        sums.append(s)
    return _pool_select(sums, g) - dp


def _pool_specs(lp, pg, ng, z_off):
    u_spec = pl.BlockSpec((lp, pg), lambda g: (0, g))
    z_spec = pl.BlockSpec((lp, pg), lambda g: (0, z_off + g))
    mix_spec = pl.BlockSpec((1, pg, pg), lambda g: (g, 0, 0))
    vec_spec = pl.BlockSpec((1, pg), lambda g: (0, g))
    return u_spec, z_spec, mix_spec, vec_spec


def _pool_fwd(proj, mix, scale, pad):
    lp = proj.shape[0]
    ng, pg, _ = mix.shape
    pw = ng * pg

    def body(u_ref, z_ref, mix_ref, sc_ref, y_ref):
        g = pl.program_id(0)
        row = lax.broadcasted_iota(jnp.int32, (lp, 1), 0)
        pooled = _pooled(u_ref[...], g, row, pad)
        mixed = _dot(pooled.astype(BF16), mix_ref[0])
        y_ref[...] = (mixed * sc_ref[...] * _silu(z_ref[...])).astype(BF16)

    u_spec, z_spec, mix_spec, vec_spec = _pool_specs(lp, pg, ng, pw // pg)
    return _call(
        body, name="pool_fwd", grid=(ng,), in_specs=[u_spec, z_spec, mix_spec, vec_spec], out_specs=u_spec,
        out_shape=jax.ShapeDtypeStruct((lp, pw), BF16), compiler_params=_params(("parallel",)),
    )(proj, proj, mix, scale)


def _pool_bwd(proj, mix, scale, dy, pad):
    lp = proj.shape[0]
    ng, pg, _ = mix.shape
    pw = ng * pg

    def body(u_ref, z_ref, mix_ref, sc_ref, dy_ref, du_ref, dz_ref, dmix_ref, dsc_ref):
        g = pl.program_id(0)
        row = lax.broadcasted_iota(jnp.int32, (lp, 1), 0)
        real = row >= pad
        z = z_ref[...]
        pooled = _pooled(u_ref[...], g, row, pad).astype(BF16)
        mixed = _dot(pooled, mix_ref[0])
        sig = _sigmoid(z)
        sz = z * sig
        dyv = dy_ref[...]
        dsc_ref[...] = jnp.sum(dyv * mixed * sz, axis=0, keepdims=True)
        d_sz = dyv * mixed * sc_ref[...]
        dz_ref[...] = jnp.where(real, d_sz * (sig + sz * (1.0 - sig)), 0.0).astype(BF16)
        d_mixed = (dyv * sc_ref[...] * sz).astype(BF16)
        dmix_ref[0] = _dot(pooled, d_mixed, TN)
        d_pooled = _dot(d_mixed, mix_ref[0], NT)
        du_ref[...] = jnp.where(real, _pooled_adjoint(d_pooled, g, row, pad), 0.0).astype(BF16)

    u_spec, z_spec, mix_spec, vec_spec = _pool_specs(lp, pg, ng, pw // pg)
    return _call(
        body, name="pool_bwd", grid=(ng,),
        in_specs=[u_spec, z_spec, mix_spec, vec_spec, u_spec], out_specs=[u_spec, u_spec, mix_spec, vec_spec],
        out_shape=[jax.ShapeDtypeStruct((lp, pw), BF16), jax.ShapeDtypeStruct((lp, pw), BF16),
                   jax.ShapeDtypeStruct((ng, pg, pg), F32), jax.ShapeDtypeStruct((1, pw), F32)],
        compiler_params=_params(("parallel",)),
    )(proj, proj, mix, scale, dy)


def _conv_pre(x, w, row):
    kw = w.shape[0]
    y = w[kw - 1:kw, :] * x
    for kk in range(kw - 1):
        y = y + w[kk:kk + 1, :] * _shift_down(x, kw - 1 - kk, row)
    return y


def _conv_post(y, out_scale):
    s = _silu(y)
    if out_scale is None:
        return s
    return s * lax.rsqrt(jnp.sum(s * s, axis=-1, keepdims=True) + NORM_EPS) * out_scale


def _conv_fwd(proj, col_off, w, hd, out_scale, name):
    lp = proj.shape[0]
    kw, width = w.shape
    blk0 = col_off // hd

    def body(x_ref, w_ref, o_ref):
        row = lax.broadcasted_iota(jnp.int32, (lp, 1), 0)
        o_ref[...] = _conv_post(_conv_pre(x_ref[...], w_ref[...], row), out_scale)

    return _call(
        body, name=name, grid=(width // hd,),
        in_specs=[pl.BlockSpec((lp, hd), lambda j: (0, blk0 + j)), pl.BlockSpec((kw, hd), lambda j: (0, j))],
        out_specs=pl.BlockSpec((lp, hd), lambda j: (0, j)),
        out_shape=jax.ShapeDtypeStruct((lp, width), F32), compiler_params=_params(("parallel",)),
    )(proj, w)


def _conv_bwd(proj, col_off, w, d_out, hd, out_scale, pad, name):
    lp = proj.shape[0]
    kw, width = w.shape
    blk0 = col_off // hd

    def body(x_ref, w_ref, do_ref, dx_ref, dw_ref):
        row = lax.broadcasted_iota(jnp.int32, (lp, 1), 0)
        real = row >= pad
        x, wv = x_ref[...], w_ref[...]
        _, vjp = jax.vjp(functools.partial(_conv_post, out_scale=out_scale), _conv_pre(x, wv, row))
        dy = jnp.where(real, vjp(do_ref[...])[0], 0.0)
        dx = wv[kw - 1:kw, :] * dy
        dw_ref[kw - 1:kw, :] = jnp.sum(dy * x, axis=0, keepdims=True)
        for kk in range(kw - 1):
            j = kw - 1 - kk
            dx = dx + wv[kk:kk + 1, :] * _shift_up(dy, j, row)
            dw_ref[kk:kk + 1, :] = jnp.sum(dy * _shift_down(x, j, row), axis=0, keepdims=True)
        dx_ref[...] = jnp.where(real, dx, 0.0).astype(BF16)

    col = pl.BlockSpec((lp, hd), lambda j: (0, j))
    wspec = pl.BlockSpec((kw, hd), lambda j: (0, j))
    return _call(
        body, name=name, grid=(width // hd,),
        in_specs=[pl.BlockSpec((lp, hd), lambda j: (0, blk0 + j)), wspec, col], out_specs=[col, wspec],
        out_shape=[jax.ShapeDtypeStruct((lp, width), BF16), jax.ShapeDtypeStruct((kw, width), F32)],
        compiler_params=_params(("parallel",)),
    )(proj, w, d_out)


HEADS_PER_STEP = 16


def _each(fn, *lists):
    return [fn(*args) for args in zip(*lists)]


def _dot3_each(a_list, b_list, dims=NN):
    hi = lambda t: t.astype(BF16)
    lo = lambda t, t_hi: (t - t_hi.astype(F32)).astype(BF16)
    dot = lambda x, y: _dot(x, y, dims)
    a_hi, b_hi = _each(hi, a_list), _each(hi, b_list)
    a_lo, b_lo = _each(lo, a_list, a_hi), _each(lo, b_list, b_hi)
    hh, hl, lh = _each(dot, a_hi, b_hi), _each(dot, a_hi, b_lo), _each(dot, a_lo, b_hi)
    return _each(lambda x, y, w: x + (y + w), hh, hl, lh)


@jax.custom_vjp
def _unit_lower_inverse(lmats):
    c = lmats[0].shape[0]
    eye = lax.broadcasted_iota(jnp.int32, (c, c), 0) == lax.broadcasted_iota(jnp.int32, (c, c), 1)
    a = [-m for m in lmats]
    tmat = [jnp.where(eye, 1.0, 0.0).astype(F32) + m for m in a]
    span = 2
    while span < c:
        a = _dot3_each(a, a)
        tmat = _each(lambda t, u: t + u, tmat, _dot3_each(tmat, a))
        span *= 2
    return tuple(tmat)


def _unit_lower_inverse_fwd(lmats):
    tmats = _unit_lower_inverse(lmats)
    return tmats, tmats


def _unit_lower_inverse_bwd(tmats, cts):
    left = _each(lambda t, ct: _dot(t, ct, TN, HIGHEST), tmats, cts)
    return (tuple(_each(lambda m, t: -_dot(m, t, NT, HIGHEST), left, tmats)),)


_unit_lower_inverse.defvjp(_unit_lower_inverse_fwd, _unit_lower_inverse_bwd)


@jax.custom_vjp
def _known_inverse(lmats, tmats):
    return tmats


def _known_inverse_fwd(lmats, tmats):
    return tmats, tmats


def _known_inverse_bwd(tmats, cts):
    return _unit_lower_inverse_bwd(tmats, cts)[0], tuple(jnp.zeros_like(t) for t in tmats)


_known_inverse.defvjp(_known_inverse_fwd, _known_inverse_bwd)


def _chunk_math(states, q, k, v, ba, z, prm, nw, head0, rowmask, n_heads, tmats=None, keep_tmats=False):
    c = q.shape[0]
    heads = list(range(len(states)))
    hd = q.shape[1] // len(states)
    lane = lax.broadcasted_iota(jnp.int32, ba.shape, 1)
    sub = lax.broadcasted_iota(jnp.int32, (ba.shape[1], c), 0)
    ri = lax.broadcasted_iota(jnp.int32, (c, c), 0)
    ci = lax.broadcasted_iota(jnp.int32, (c, c), 1)
    last = lax.broadcasted_iota(jnp.int32, (c, 1), 0) == c - 1
    causal, strict = ri >= ci, ri > ci
    beta_all = _sigmoid(ba) * rowmask
    g_all = -jnp.exp(prm[0:1, :]) * _softplus(ba + prm[1:2, :]) * rowmask
    gcum_all = _dot(jnp.where(causal, 1.0, 0.0).astype(F32), g_all, precision=HIGHEST)
    gcum_t = gcum_all.T
    split = lambda t: [t[:, j * hd:(j + 1) * hd] for j in heads]
    qs, ks, vs, zs = split(q), split(k), split(v), split(z)
    beta = [jnp.sum(jnp.where(lane == head0 + j, beta_all, 0.0), axis=1, keepdims=True) for j in heads]
    gcum = [jnp.sum(jnp.where(lane == n_heads + head0 + j, gcum_all, 0.0), axis=1, keepdims=True) for j in heads]
    grow = [jnp.sum(jnp.where(sub == n_heads + head0 + j, gcum_t, 0.0), axis=0, keepdims=True) for j in heads]
    glast = _each(lambda gc: jnp.sum(jnp.where(last, gc, 0.0), axis=0, keepdims=True), gcum)
    decay = _each(lambda gc, gr: jnp.where(causal, jnp.exp(jnp.where(causal, gc - gr, 0.0)), 0.0), gcum, grow)
    eg = _each(jnp.exp, gcum)
    k_beta = _each(jnp.multiply, ks, beta)
    kk = _each(lambda a, b: _dot(a, b, NT), k_beta, ks)
    lmats = tuple(_each(lambda m, dc: jnp.where(strict, m * dc, 0.0), kk, decay))
    tmat = list(_unit_lower_inverse(lmats) if tmats is None else _known_inverse(lmats, tuple(tmats)))
    u_c = _each(_dot, tmat, _each(jnp.multiply, vs, beta))
    w_c = _each(_dot, tmat, _each(jnp.multiply, k_beta, eg))
    qk = _each(lambda a, b, dc: jnp.where(causal, _dot(a, b, NT) * dc, 0.0), qs, ks, decay)
    v_new = _each(lambda u, w, s: u - _dot(w, s), u_c, w_c, list(states))
    o = _each(lambda a, e, s, m, vn: _dot(a * e, s) + _dot(m, vn), qs, eg, list(states), qk, v_new)
    k_dec = _each(lambda a, gl, gc: a * jnp.exp(gl - gc), ks, glast, gcum)
    new_states = _each(lambda s, gl, kd, vn: s * jnp.exp(gl) + _dot(kd, vn, TN), list(states), glast, k_dec, v_new)
    ys = _each(lambda oj, zj: _rmsnorm(oj, nw) * _silu(zj), o, zs)
    if keep_tmats:
        return jnp.concatenate(ys, axis=1), tuple(new_states), tuple(tmat)
    return jnp.concatenate(ys, axis=1), tuple(new_states)


def _chunk_specs(nc, hd, n_heads, z_off, ba_off, rev):
    cidx = (lambda c: nc - 1 - c) if rev else (lambda c: c)
    hb = min(HEADS_PER_STEP, n_heads)
    assert n_heads % hb == 0 and z_off % (hb * hd) == 0 and ba_off % LANES == 0
    blk = lambda off: pl.BlockSpec((CHUNK, hb * hd), lambda c, g: (cidx(c), off + g))
    ba_spec = lambda off: pl.BlockSpec((CHUNK, LANES), lambda c, g: (cidx(c), off // LANES))
    prm_spec = pl.BlockSpec((8, LANES), lambda c, g: (0, 0))
    nw_spec = pl.BlockSpec((1, hd), lambda c, g: (0, 0))
    st_spec = pl.BlockSpec((1, hb, hd, hd), lambda c, g: (cidx(c), g, 0, 0))
    return blk, ba_spec, prm_spec, nw_spec, st_spec, blk(z_off // (hb * hd))


def _rowmask(chunk_idx, pad):
    row = chunk_idx * CHUNK + lax.broadcasted_iota(jnp.int32, (CHUNK, 1), 0)
    return jnp.where(row >= pad, 1.0, 0.0).astype(F32)


def _chunk_fwd(qn, kn, vv, proj, z_off, ba_off, prm, nw, n_heads, pad):
    lp, dn = qn.shape
    hd = dn // n_heads
    nc = lp // CHUNK
    hb = min(HEADS_PER_STEP, n_heads)

    def body(q_ref, k_ref, v_ref, ba_ref, z_ref, prm_ref, nw_ref, y_ref, hist_ref, tm_ref, st_ref):
        c, g = pl.program_id(0), pl.program_id(1)

        @pl.when(c == 0)
        def _():
            for j in range(hb):
                st_ref[g * hb + j] = jnp.zeros((hd, hd), F32)

        states = tuple(st_ref[g * hb + j] for j in range(hb))
        for j in range(hb):
            hist_ref[0, j] = states[j]
        y, new_states, tmats = _chunk_math(states, q_ref[...], k_ref[...], v_ref[...], ba_ref[...], z_ref[...], prm_ref[...],
                                           nw_ref[...], g * hb, _rowmask(c, pad), n_heads, keep_tmats=True)
        y_ref[...] = y.astype(BF16)
        for j in range(hb):
            st_ref[g * hb + j] = new_states[j]
            tm_ref[0, j] = tmats[j]

    blk, ba_spec, prm_spec, nw_spec, st_spec, z_spec = _chunk_specs(nc, hd, n_heads, z_off, ba_off, False)
    tm_spec = pl.BlockSpec((1, hb, CHUNK, CHUNK), lambda c, g: (c, g, 0, 0))
    return _call(
        body, name="chunk_fwd", grid=(nc, n_heads // hb),
        in_specs=[blk(0), blk(0), blk(0), ba_spec(ba_off), z_spec, prm_spec, nw_spec], out_specs=[blk(0), st_spec, tm_spec],
        out_shape=[jax.ShapeDtypeStruct((lp, dn), BF16), jax.ShapeDtypeStruct((nc, n_heads, hd, hd), F32),
                   jax.ShapeDtypeStruct((nc, n_heads, CHUNK, CHUNK), F32)],
        scratch_shapes=[pltpu.VMEM((n_heads, hd, hd), F32)],
        compiler_params=_params(("arbitrary", "arbitrary")),
    )(qn, kn, vv, proj, proj, prm, nw)


def _chunk_bwd(qn, kn, vv, proj, z_off, ba_off, prm, nw, hist, tmats, dy, n_heads, pad):
    lp, dn = qn.shape
    hd = dn // n_heads
    nc = lp // CHUNK
    hb = min(HEADS_PER_STEP, n_heads)

    def body(q_ref, k_ref, v_ref, ba_ref, z_ref, prm_ref, nw_ref, hist_ref, tm_ref, dy_ref,
             dq_ref, dk_ref, dv_ref, dba_ref, dz_ref, dprm_ref, dnw_ref, dst_ref):
        step, g = pl.program_id(0), pl.program_id(1)

        @pl.when(step == 0)
        def _():
            for j in range(hb):
                dst_ref[g * hb + j] = jnp.zeros((hd, hd), F32)

        @pl.when((step == 0) & (g == 0))
        def _():
            dprm_ref[...] = jnp.zeros_like(dprm_ref)
            dnw_ref[...] = jnp.zeros_like(dnw_ref)

        @pl.when(g == 0)
        def _():
            dba_ref[...] = jnp.zeros_like(dba_ref)

        def fn(states, q, k, v, ba, z, prm_v, nw_v, known):
            return _chunk_math(states, q, k, v, ba, z, prm_v, nw_v, g * hb, _rowmask(nc - 1 - step, pad), n_heads, tmats=known)

        states = tuple(hist_ref[0, j] for j in range(hb))
        known = tuple(tm_ref[0, j] for j in range(hb))
        _, vjp = jax.vjp(fn, states, q_ref[...], k_ref[...], v_ref[...], ba_ref[...], z_ref[...], prm_ref[...], nw_ref[...], known)
        dst, dq, dk, dv, dba, dz, dprm, dnw, _ = vjp((dy_ref[...], tuple(dst_ref[g * hb + j] for j in range(hb))))
        for j in range(hb):
            dst_ref[g * hb + j] = dst[j]
        dq_ref[...] = dq
        dk_ref[...] = dk
        dv_ref[...] = dv
        dz_ref[...] = dz.astype(BF16)
        dba_ref[...] += dba
        dprm_ref[...] += dprm
        dnw_ref[...] += dnw

    blk, ba_spec, prm_spec, nw_spec, st_spec, z_spec = _chunk_specs(nc, hd, n_heads, z_off, ba_off, True)
    f32_full = jax.ShapeDtypeStruct((lp, dn), F32)
    tm_spec = pl.BlockSpec((1, hb, CHUNK, CHUNK), lambda c, g: (nc - 1 - c, g, 0, 0))
    return _call(
        body, name="chunk_bwd", grid=(nc, n_heads // hb),
        in_specs=[blk(0), blk(0), blk(0), ba_spec(ba_off), z_spec, prm_spec, nw_spec, st_spec, tm_spec, blk(0)],
        out_specs=[blk(0), blk(0), blk(0), ba_spec(0), blk(0), prm_spec, nw_spec],
        out_shape=[f32_full, f32_full, f32_full, jax.ShapeDtypeStruct((lp, LANES), F32), jax.ShapeDtypeStruct((lp, dn), BF16),
                   jax.ShapeDtypeStruct((8, LANES), F32), jax.ShapeDtypeStruct((1, hd), F32)],
        scratch_shapes=[pltpu.VMEM((n_heads, hd, hd), F32)],
        compiler_params=_params(("arbitrary", "arbitrary")),
    )(qn, kn, vv, proj, proj, prm, nw, hist, tmats, dy)


def _merge_math(p, q, gp, gd):
    return _sigmoid(gp) * p + _sigmoid(gd) * q


def _merge_specs(lp, d, gp_off, gd_off):
    tr, tc = _tile(lp, 264, 16), _tile(d, 1024, LANES)
    blk = pl.BlockSpec((tr, tc), lambda i, j: (i, j))
    gp_spec = pl.BlockSpec((tr, tc), lambda i, j: (i, gp_off // tc + j))
    gd_spec = pl.BlockSpec((tr, tc), lambda i, j: (i, gd_off // tc + j))
    return (lp // tr, d // tc), blk, gp_spec, gd_spec


def _merge_fwd(p, q, proj, gp_off, gd_off):
    lp, d = p.shape
    grid, blk, gp_spec, gd_spec = _merge_specs(lp, d, gp_off, gd_off)

    def body(p_ref, q_ref, gp_ref, gd_ref, o_ref):
        o_ref[...] = _merge_math(p_ref[...], q_ref[...], gp_ref[...], gd_ref[...]).astype(BF16)

    return _call(
        body, name="merge_fwd", grid=grid, in_specs=[blk, blk, gp_spec, gd_spec], out_specs=blk,
        out_shape=jax.ShapeDtypeStruct((lp, d), BF16), compiler_params=_params(("parallel", "parallel")),
    )(p, q, proj, proj)


def _merge_bwd(p, q, proj, gp_off, gd_off, dm):
    lp, d = p.shape
    grid, blk, gp_spec, gd_spec = _merge_specs(lp, d, gp_off, gd_off)

    def body(p_ref, q_ref, gp_ref, gd_ref, dm_ref, dp_ref, dq_ref, dgp_ref, dgd_ref):
        _, vjp = jax.vjp(_merge_math, p_ref[...], q_ref[...], gp_ref[...], gd_ref[...])
        for ref, val in zip((dp_ref, dq_ref, dgp_ref, dgd_ref), vjp(dm_ref[...])):
            ref[...] = val.astype(BF16)

    out = jax.ShapeDtypeStruct((lp, d), BF16)
    return _call(
        body, name="merge_bwd", grid=grid, in_specs=[blk, blk, gp_spec, gd_spec, blk], out_specs=[blk] * 4,
        out_shape=[out] * 4, compiler_params=_params(("parallel", "parallel")),
    )(p, q, proj, proj, dm)


def _adamw(w, g, m, v, name):
    shape = w.shape
    w2, g2, m2, v2 = (t.reshape((-1, shape[-1])) for t in (w, g, m, v))
    rows, cols = w2.shape
    tr = _tile(rows, 128, 8)

    def body(w_ref, g_ref, m_ref, v_ref, d_ref, nm_ref, nv_ref):
        gv = g_ref[...]
        nm = ADAM_B1 * m_ref[...] + (1.0 - ADAM_B1) * gv
        nv = ADAM_B2 * v_ref[...] + (1.0 - ADAM_B2) * (gv * gv)
        m_hat = nm / (1.0 - ADAM_B1 ** ADAM_STEP)
        v_hat = nv / (1.0 - ADAM_B2 ** ADAM_STEP)
        d_ref[...] = -ADAM_LR * (m_hat / (jnp.sqrt(v_hat) + ADAM_EPS) + ADAM_WD * w_ref[...])
        nm_ref[...] = nm
        nv_ref[...] = nv

    blk = pl.BlockSpec((tr, cols), lambda i: (i, 0))
    out = jax.ShapeDtypeStruct((rows, cols), F32)
    res = _call(
        body, name=name, grid=(rows // tr,), in_specs=[blk] * 4, out_specs=[blk] * 3, out_shape=[out] * 3,
        compiler_params=_params(("parallel",)),
    )(w2, g2, m2, v2)
    return tuple(t.reshape(shape) for t in res)


def _coords():
    return lax.axis_index("x"), lax.axis_index("y"), lax.axis_index("c")


def _flip(v, bit):
    return 1 - v if bit else v


CHIP_FLIPS = ((1, 0), (0, 1), (1, 1))
ANY = pl.BlockSpec(memory_space=pl.ANY)


def _all_gather(shards):
    n = len(shards)

    def body(*refs):
        x_refs, out_refs = refs[:n], refs[n:2 * n]
        send_sems, recv_sems, local_sems = refs[2 * n:]
        x, y, c = _coords()
        sibling = (x, y, 1 - c)
        chips = [(_flip(x, fx), _flip(y, fy)) for fx, fy in CHIP_FLIPS]

        def copy(a, k, block, to, from_input=False):
            px, py, pc = block
            slot = out_refs[a].at[4 * px + 2 * py + pc]
            return pltpu.make_async_remote_copy(
                src_ref=x_refs[a] if from_input else slot, dst_ref=slot,
                send_sem=send_sems.at[7 * a + k], recv_sem=recv_sems.at[7 * a + k], device_id=to, device_id_type=MESH)

        mine = [pltpu.make_async_copy(x_refs[a], out_refs[a].at[4 * x + 2 * y + c], local_sems.at[a]) for a in range(n)]
        first = []
        for a in range(n):
            mine[a].start()
            first.append(copy(a, 0, (x, y, c), sibling, True))
            first += [copy(a, 1 + j, (x, y, c), (*chip, c), True) for j, chip in enumerate(chips)]
        for cp in first:
            cp.start()
        passed = []
        for j, chip in enumerate(chips):
            for a in range(n):
                copy(a, 1 + j, (*chip, c), (x, y, c)).wait_recv()
                passed.append(copy(a, 4 + j, (*chip, c), sibling))
                passed[-1].start()
        for a in range(n):
            copy(a, 0, (x, y, 1 - c), (x, y, c)).wait_recv()
            for j, chip in enumerate(chips):
                copy(a, 4 + j, (*chip, 1 - c), (x, y, c)).wait_recv()
        for cp in first + passed:
            cp.wait_send()
        for cp in mine:
            cp.wait()

    return _call(
        body, name="all_gather", in_specs=[ANY] * n, out_specs=[ANY] * n,
        out_shape=[jax.ShapeDtypeStruct((N_DEV,) + s.shape, s.dtype) for s in shards],
        scratch_shapes=[pltpu.SemaphoreType.DMA((7 * n,)), pltpu.SemaphoreType.DMA((7 * n,)), pltpu.SemaphoreType.DMA((n,))],
    )(*shards)


def _all_gather_tree(shard, after):
    rows, cols = shard.shape
    half = rows // 2
    assert rows % 32 == 0

    def body(x_ref, after_ref, out_ref, send_sems, recv_sems, local_sem):
        x, y, c = _coords()
        me, sibling = (x, y, c), (x, y, 1 - c)
        x_nbr, y_nbr, diag = (1 - x, y), (x, 1 - y), (1 - x, 1 - y)

        def part(ref, h):
            return ref if h is None else ref.at[pl.ds(h * half, half)]

        def copy(k, block, to, h=None, from_input=False):
            px, py, pc = block
            slot = part(out_ref.at[4 * px + 2 * py + pc], h)
            return pltpu.make_async_remote_copy(
                src_ref=part(x_ref, h) if from_input else slot, dst_ref=slot,
                send_sem=send_sems.at[k], recv_sem=recv_sems.at[k], device_id=to, device_id_type=MESH)

        mine = pltpu.make_async_copy(x_ref, out_ref.at[4 * x + 2 * y + c], local_sem)
        mine.start()
        started = [copy(0, me, sibling, None, True),
                   copy(1, me, (*x_nbr, c), 0, True), copy(2, me, (*x_nbr, c), 1, True),
                   copy(4, me, (*y_nbr, c), 1, True), copy(3, me, (*y_nbr, c), 0, True)]
        for cp in started:
            cp.start()
        copy(1, (*x_nbr, c), me, 0).wait_recv()
        started.append(copy(5, (*x_nbr, c), (*y_nbr, c), 0))
        started[-1].start()
        copy(4, (*y_nbr, c), me, 1).wait_recv()
        started.append(copy(6, (*y_nbr, c), (*x_nbr, c), 1))
        started[-1].start()
        copy(2, (*x_nbr, c), me, 1).wait_recv()
        started.append(copy(7, (*x_nbr, c), sibling))
        started[-1].start()
        copy(3, (*y_nbr, c), me, 0).wait_recv()
        started.append(copy(8, (*y_nbr, c), sibling))
        started[-1].start()
        copy(5, (*diag, c), me, 0).wait_recv()
        copy(6, (*diag, c), me, 1).wait_recv()
        started.append(copy(9, (*diag, c), sibling))
        started[-1].start()
        copy(0, sibling, me).wait_recv()
        for k, chip in ((7, x_nbr), (8, y_nbr), (9, diag)):
            copy(k, (*chip, 1 - c), me).wait_recv()
        for cp in started:
            cp.wait_send()
        mine.wait()

    return _call(
        body, name="all_gather_tree", in_specs=[ANY, ANY], out_specs=ANY,
        out_shape=jax.ShapeDtypeStruct((N_DEV, rows, cols), shard.dtype),
        scratch_shapes=[pltpu.SemaphoreType.DMA((10,)), pltpu.SemaphoreType.DMA((10,)), pltpu.SemaphoreType.DMA],
    )(shard, after)


def _rs_to_sibling(gs, name):
    n = len(gs)

    def body(*refs):
        g_refs, got_refs = refs[:n], refs[n:2 * n]
        send_sems, recv_sems = refs[2 * n:]
        x, y, c = _coords()
        copies = []
        for a in range(n):
            for p in range(4):
                cp = pltpu.make_async_remote_copy(
                    src_ref=g_refs[a].at[2 * p + (1 - c)], dst_ref=got_refs[a].at[p], send_sem=send_sems.at[4 * a + p],
                    recv_sem=recv_sems.at[4 * a + p], device_id=(x, y, 1 - c), device_id_type=MESH)
                cp.start()
                copies.append(cp)
        for cp in copies:
            cp.wait()

    return _call(
        body, name=name, in_specs=[ANY] * n, out_specs=[ANY] * n,
        out_shape=[jax.ShapeDtypeStruct((4,) + g.shape[1:], g.dtype) for g in gs],
        scratch_shapes=[pltpu.SemaphoreType.DMA((4 * n,)), pltpu.SemaphoreType.DMA((4 * n,))],
    )(*gs)


def _rs_pair_sum(g, got, c_idx, name):
    _, rows, cols = g.shape
    tr = _tile(rows, 256, 16)

    def body(c_ref, g_ref, got_ref, o_ref):
        o_ref[...] = (g_ref[...].astype(F32) + got_ref[...].astype(F32)).astype(o_ref.dtype)

    grid_spec = pltpu.PrefetchScalarGridSpec(
        num_scalar_prefetch=1, grid=(4, rows // tr),
        in_specs=[pl.BlockSpec((1, tr, cols), lambda p, i, c_ref: (2 * p + c_ref[0], i, 0)),
                  pl.BlockSpec((1, tr, cols), lambda p, i, c_ref: (p, i, 0))],
        out_specs=pl.BlockSpec((1, tr, cols), lambda p, i, c_ref: (p, i, 0)))
    return _call(
        body, name=name, grid_spec=grid_spec, out_shape=jax.ShapeDtypeStruct((4, rows, cols), g.dtype),
        compiler_params=_params(("parallel", "parallel")),
    )(c_idx, g, got)


def _to_chips_copies(p_refs, got_refs, send_sems, recv_sems):
    x, y, c = _coords()
    copies = []
    for a in range(len(p_refs)):
        for k, (fx, fy) in enumerate(CHIP_FLIPS):
            px, py = _flip(x, fx), _flip(y, fy)
            copies.append(pltpu.make_async_remote_copy(
                src_ref=p_refs[a].at[2 * px + py], dst_ref=got_refs[a].at[k], send_sem=send_sems.at[3 * a + k],
                recv_sem=recv_sems.at[3 * a + k], device_id=(px, py, c), device_id_type=MESH))
    return copies


def _rs_to_chips(partials, name):
    n = len(partials)

    def body(*refs):
        copies = _to_chips_copies(refs[:n], refs[n:2 * n], *refs[2 * n:])
        for cp in copies:
            cp.start()
        for cp in copies:
            cp.wait()

    return _call(
        body, name=name, in_specs=[ANY] * n, out_specs=[ANY] * n,
        out_shape=[jax.ShapeDtypeStruct((3,) + p.shape[1:], p.dtype) for p in partials],
        scratch_shapes=[pltpu.SemaphoreType.DMA((3 * n,)), pltpu.SemaphoreType.DMA((3 * n,))],
    )(*partials)


HBM = pl.BlockSpec(memory_space=pltpu.HBM)
SEM = pl.BlockSpec(memory_space=pltpu.SEMAPHORE)
SIDE_EFFECT = pltpu.CompilerParams(has_side_effects=pltpu.SideEffectType.DATAFLOW_SIDE_EFFECTING)


def _split_start(copies_fn, srcs, land_shapes, n_sems, name, after=None):
    n, m = len(srcs), len(land_shapes)
    extra = [] if after is None else [after]

    def body(*refs):
        outs = refs[n + m + len(extra):]
        send_sems, recv_sems, token = outs[0], outs[1], outs[-1]
        for cp in copies_fn(refs[:n], refs[n:n + m], send_sems, recv_sems):
            cp.start()
        token[...] = jnp.zeros_like(token)

    ins = [pltpu.with_memory_space_constraint(t, pltpu.HBM) for t in list(srcs) + [lax.empty(s.shape, s.dtype) for s in land_shapes]]
    res = _call(
        body, name=name, in_specs=[HBM] * (n + m) + [ANY] * len(extra),
        out_specs=[SEM, SEM] + [HBM] * (n + m) + [pl.BlockSpec(memory_space=pltpu.VMEM)],
        out_shape=[pltpu.SemaphoreType.DMA((n_sems,)), pltpu.SemaphoreType.DMA((n_sems,))]
        + [pltpu.HBM(t.shape, t.dtype) for t in ins] + [jax.ShapeDtypeStruct((8, LANES), F32)],
        input_output_aliases={i: 2 + i for i in range(n + m)}, compiler_params=SIDE_EFFECT,
    )(*ins, *extra)
    return dict(sems=(res[0], res[1]), srcs=res[2:2 + n], lands=res[2 + n:2 + n + m], token=res[-1])


def _split_wait(copies_fn, started, after, name):
    n, m = len(started["srcs"]), len(started["lands"])

    def body(*refs):
        for cp in copies_fn(refs[:n], refs[n:n + m], refs[n + m], refs[n + m + 1]):
            cp.wait_send()
            cp.wait_recv()

    bufs = list(started["srcs"]) + list(started["lands"])
    res = _call(
        body, name=name, in_specs=[HBM] * (n + m) + [SEM, SEM, ANY], out_specs=[HBM] * (n + m),
        out_shape=[pltpu.HBM(t.shape, t.dtype) for t in bufs],
        input_output_aliases={i: i for i in range(n + m)}, compiler_params=SIDE_EFFECT,
    )(*bufs, *started["sems"], after)
    return res[:n], res[n:]


def _to_all_copies(x_refs, out_refs, send_sems, recv_sems):
    x, y, c = _coords()
    copies = []
    for a in range(len(x_refs)):
        for k in range(N_DEV - 1):
            fx, fy, fc = ((k + 1) >> 2) & 1, ((k + 1) >> 1) & 1, (k + 1) & 1
            copies.append(pltpu.make_async_remote_copy(
                src_ref=x_refs[a], dst_ref=out_refs[a].at[4 * x + 2 * y + c], send_sem=send_sems.at[7 * a + k],
                recv_sem=recv_sems.at[7 * a + k], device_id=(_flip(x, fx), _flip(y, fy), _flip(c, fc)), device_id_type=MESH))
    return copies


def _fill_own_block(gathered, shard, me_idx, name):
    rows, cols = shard.shape
    tr = _tile(rows, 512, 16)

    def body(me_ref, g_ref, s_ref, o_ref):
        o_ref[0] = s_ref[...]

    grid_spec = pltpu.PrefetchScalarGridSpec(
        num_scalar_prefetch=1, grid=(rows // tr,),
        in_specs=[ANY, pl.BlockSpec((tr, cols), lambda i, me: (i, 0))],
        out_specs=pl.BlockSpec((1, tr, cols), lambda i, me: (me[0], i, 0)))
    return _call(
        body, name=name, grid_spec=grid_spec, out_shape=jax.ShapeDtypeStruct(gathered.shape, gathered.dtype),
        input_output_aliases={1: 0}, compiler_params=_params(("arbitrary",)),
    )(me_idx, gathered, shard)


def _rs_chip_sum(partial, got, chip_idx, name, part=0, n_parts=1, dst=None):
    _, rows, cols = partial.shape
    tr = _tile(rows, 256, 16)
    steps = rows // tr
    n_dst = 0 if dst is None else 1

    def body(p_idx_ref, p_ref, got_ref, *refs):
        refs[n_dst][...] = ((p_ref[0].astype(F32) + got_ref[0].astype(F32)) + got_ref[1].astype(F32)) + got_ref[2].astype(F32)

    grid_spec = pltpu.PrefetchScalarGridSpec(
        num_scalar_prefetch=1, grid=(steps,),
        in_specs=[pl.BlockSpec((1, tr, cols), lambda i, p_ref: (p_ref[0], i, 0)),
                  pl.BlockSpec((3, tr, cols), lambda i, p_ref: (0, i, 0))] + [ANY] * n_dst,
        out_specs=pl.BlockSpec((tr, cols), lambda i, p_ref: (part * steps + i, 0)))
    return _call(
        body, name=name, grid_spec=grid_spec, out_shape=jax.ShapeDtypeStruct((n_parts * rows, cols), F32),
        input_output_aliases={3: 0} if n_dst else {}, compiler_params=_params(("parallel",)),
    )(chip_idx, partial, got, *([] if dst is None else [dst]))


def _rs_begin(gs, tag, split):
    c_idx = jnp.reshape(lax.axis_index("c"), (1,)).astype(jnp.int32)
    gots = _rs_to_sibling(gs, "rs_to_sibling_" + tag)
    partials = [_rs_pair_sum(g, got, c_idx, "rs_pair_sum_%s%d" % (tag, a)) for a, (g, got) in enumerate(zip(gs, gots))]
    if not split:
        return dict(partials=partials, gots=_rs_to_chips(partials, "rs_to_chips_" + tag))
    lands = [jax.ShapeDtypeStruct((3,) + p.shape[1:], p.dtype) for p in partials]
    return _split_start(_to_chips_copies, partials, lands, 3 * len(partials), "rs_to_chips_start_" + tag)


def _rs_finish(begun, tag, after=None, part=0, n_parts=1, dsts=None):
    x, y, _ = _coords()
    chip_idx = jnp.reshape(2 * x + y, (1,)).astype(jnp.int32)
    if "gots" in begun:
        partials, gots = begun["partials"], begun["gots"]
    else:
        partials, gots = _split_wait(_to_chips_copies, begun, after, "rs_to_chips_wait_" + tag)
    return [_rs_chip_sum(p, got, chip_idx, "rs_chip_sum_%s%d" % (tag, a), part, n_parts, None if dsts is None else dsts[a])
            for a, (p, got) in enumerate(zip(partials, gots))]


RUNS = 3
RUN_FIELDS = 6


def _lane_gather_table(src_of, src_width):
    n_blocks = src_of.shape[0] // LANES
    tab = np.zeros((n_blocks + 1, RUNS, RUN_FIELDS), np.int32)
    tab[:, :, 5] = LANES
    for t in range(n_blocks):
        runs = []
        for lane in range(LANES):
            slab, col = (int(v) for v in src_of[t * LANES + lane])
            if slab < 0:
                continue
            key = (slab, col // LANES, col % LANES - lane)
            if runs and runs[-1][0] == key and runs[-1][2] == lane:
                runs[-1][2] = lane + 1
            else:
                runs.append([key, lane, lane + 1])
        assert len(runs) <= RUNS
        slots = [None] * RUNS
        for key, lo, hi in sorted(runs, key=lambda r: r[0][:2]):
            e = key[1] % 2 if slots[key[1] % 2] is None else slots.index(None)
            slots[e] = (key[0], key[1], key[2], lo, hi, min(LANES, src_width - key[1] * LANES))
        for e in range(RUNS):
            tab[t, e] = slots[e] if slots[e] is not None else (tab[t - 1, e, 0], tab[t - 1, e, 1], 0, 0, 0, LANES) if t else tab[t, e]
    tab[n_blocks, :, :2] = tab[n_blocks - 1, :, :2]
    return tab.reshape(-1)


def _gathered_block(tab_ref, t, load):
    lane = lax.broadcasted_iota(jnp.int32, (1, LANES), 1)
    out = None
    for e in range(RUNS):
        base = (t * RUNS + e) * RUN_FIELDS
        slab, blk, shift, lo, hi = (tab_ref[base + i] for i in range(5))
        turned = pltpu.roll(load(e, slab, blk).astype(F32), (LANES - shift) % LANES, 1)
        out = jnp.where((lane >= lo) & (lane < hi), turned, 0.0 if out is None else out)
    return out.astype(BF16)


def _lane_gather_cols(src, table, out_slabs, out_width, name):
    _, rows, _ = src.shape
    blocks_per_slab = -(-out_width // LANES)

    def body(tab_ref, *refs):
        refs[RUNS][0] = _gathered_block(tab_ref, pl.program_id(0), lambda e, slab, blk: refs[e][0])

    def src_spec(e):
        return pl.BlockSpec((1, rows, LANES), lambda t, tab: (tab[(t * RUNS + e) * RUN_FIELDS], 0, tab[(t * RUNS + e) * RUN_FIELDS + 1]))

    grid_spec = pltpu.PrefetchScalarGridSpec(
        num_scalar_prefetch=1, grid=(out_slabs * blocks_per_slab,), in_specs=[src_spec(e) for e in range(RUNS)],
        out_specs=pl.BlockSpec((1, rows, LANES), lambda t, tab: (t // blocks_per_slab, 0, t % blocks_per_slab)))
    return _call(
        body, name=name, grid_spec=grid_spec, out_shape=jax.ShapeDtypeStruct((out_slabs, rows, out_width), BF16),
        compiler_params=_params(("arbitrary",)),
    )(jnp.asarray(table), src, src, src)


def _all_reduce_small(vec):
    rows, cols = vec.shape

    def body(v_ref, o_ref, buf, send_sems, recv_sems):
        x, y, c = _coords()
        me = 4 * x + 2 * y + c
        buf[me] = v_ref[...]
        copies = []
        for k in range(N_DEV - 1):
            fx, fy, fc = ((k + 1) >> 2) & 1, ((k + 1) >> 1) & 1, (k + 1) & 1
            cp = pltpu.make_async_remote_copy(
                src_ref=v_ref, dst_ref=buf.at[me], send_sem=send_sems.at[k], recv_sem=recv_sems.at[k],
                device_id=(_flip(x, fx), _flip(y, fy), _flip(c, fc)), device_id_type=MESH)
            cp.start()
            copies.append(cp)
        for cp in copies:
            cp.wait()
        total = buf[0]
        for j in range(1, N_DEV):
            total = total + buf[j]
        o_ref[...] = total

    vmem = pl.BlockSpec(memory_space=pltpu.VMEM)
    return _call(
        body, name="all_reduce_small", in_specs=[vmem], out_specs=vmem,
        out_shape=jax.ShapeDtypeStruct((rows, cols), F32),
        scratch_shapes=[pltpu.VMEM((N_DEV, rows, cols), F32), pltpu.SemaphoreType.DMA((N_DEV - 1,)),
                        pltpu.SemaphoreType.DMA((N_DEV - 1,))],
    )(vec)


def _w_in_column_maps(ns, o_ba, n_logit, n_main, n_all):
    own = np.arange(N_DEV * ns)
    work_of_own = np.where(own < o_ba, own, np.where(own < o_ba + n_logit, n_main + own - o_ba, own - n_logit))
    to_work = np.full((n_all, 2), -1, np.int64)
    to_work[work_of_own, 0] = own // ns
    to_work[work_of_own, 1] = own % ns
    slab_width = -(-ns // LANES) * LANES
    to_own = np.full((N_DEV, slab_width, 2), -1, np.int64)
    to_own[:, :ns, 0] = 0
    to_own[:, :ns, 1] = work_of_own.reshape(N_DEV, ns)
    return to_work, to_own.reshape(-1, 2)


def kernel(x, meta_tokens, norm_w, w_in, conv_w, A_log, dt_bias, pool_mix, pool_scale, dn_norm_w, w_pool_out, w_dn_out, w_o, final_norm_w, loss_target, m_meta_tokens, m_norm_w, m_w_in, m_conv_w, m_A_log, m_dt_bias, m_pool_mix, m_pool_scale, m_dn_norm_w, m_w_pool_out, m_w_dn_out, m_w_o, m_final_norm_w, v_meta_tokens, v_norm_w, v_w_in, v_conv_w, v_A_log, v_dt_bias, v_pool_mix, v_pool_scale, v_dn_norm_w, v_w_pool_out, v_w_dn_out, v_w_o, v_final_norm_w):
    seq, d = x.shape[1], x.shape[2]
    n_meta = meta_tokens.shape[0]
    n_heads, hd = A_log.shape[-1], dn_norm_w.shape[-1]
    dn = n_heads * hd
    pw, ng = pool_scale.shape[-1], pool_mix.shape[1]
    pg = pw // ng
    kw = conv_w.shape[1]
    pad = (-n_meta) % CHUNK
    x0 = pad + n_meta
    lp = x0 + seq
    ns = w_in.shape[-1]
    in_cols = N_DEV * ns
    o_q, o_k, o_v, o_zd = 2 * pw, 2 * pw + dn, 2 * pw + 2 * dn, 2 * pw + 3 * dn
    o_ba = 2 * pw + 4 * dn
    o_gp, o_gd = o_ba, o_ba + d
    n_main = o_gd + d
    n_all = n_main + 2 * LANES
    assert lp % CHUNK == 0 and in_cols == n_main + 2 * n_heads and 2 * n_heads <= LANES and hd == LANES
    cs, ms = conv_w.shape[-1], meta_tokens.shape[-1]
    mr = pool_mix.shape[2]
    assert ms == pg and cs % pg == 0
    to_work, to_own = _w_in_column_maps(ns, o_ba, 2 * n_heads, n_main, n_all)
    cols_major = lambda t: jnp.transpose(t, (1, 0, 2)).reshape(t.shape[1], N_DEV * t.shape[2])

    mix_g, conv_g, meta_g = _all_gather([pool_mix[0].reshape(ng * mr, pg).astype(BF16), conv_w[0], meta_tokens])
    win_g = _all_gather_tree(w_in[0].astype(BF16), after=meta_g)
    late_shards = [w_pool_out[0].astype(BF16), w_dn_out[0].astype(BF16), w_o[0].astype(BF16)]
    late_weights = _split_start(_to_all_copies, late_shards, [jax.ShapeDtypeStruct((N_DEV,) + s.shape, BF16) for s in late_shards],
                                (N_DEV - 1) * len(late_shards), "gather_out_proj_start", after=win_g)
    norm_w_in = norm_w + late_weights["token"][0, 0]
    w_all = _lane_gather_cols(win_g, _lane_gather_table(to_work, ns), 1, n_all, "w_in_to_work").reshape(d, n_all)
    mix_f = jnp.transpose(mix_g.reshape(N_DEV, ng, mr, pg), (1, 0, 2, 3)).reshape(ng, pg, pg)
    conv_f = cols_major(conv_g)
    meta_f = cols_major(meta_g)

    h0, xn = _norm_in_fwd(x[0], meta_f, norm_w_in, pad)
    proj = _matmul(xn, w_all, NN, F32, lp, 768, 2048, "proj")
    y_pool = _pool_fwd(proj, mix_f, pool_scale, pad)
    conv_q, conv_k, conv_v = (conv_f[:, i * dn:(i + 1) * dn] for i in range(3))
    qn = _conv_fwd(proj, o_q, conv_q, hd, float(hd) ** -0.5, "conv_q_fwd")
    kn = _conv_fwd(proj, o_k, conv_k, hd, 1.0, "conv_k_fwd")
    vv = _conv_fwd(proj, o_v, conv_v, hd, None, "conv_v_fwd")
    logit_lanes = (n_heads, LANES - 2 * n_heads)
    prm = jnp.pad(A_log, ((0, 7), logit_lanes)) + jnp.pad(dt_bias, ((1, 6), logit_lanes))
    y_dn, hist, tmats = _chunk_fwd(qn, kn, vv, proj, o_zd, n_main, prm, dn_norm_w, n_heads, pad)
    me_idx = jnp.reshape(4 * lax.axis_index("x") + 2 * lax.axis_index("y") + lax.axis_index("c"), (1,)).astype(jnp.int32)
    _, landed = _split_wait(_to_all_copies, late_weights, y_dn, "gather_out_proj_wait")
    wpo_g, wdn_g, wo_g = (_fill_own_block(g, s, me_idx, "own_block_%d" % i) for i, (g, s) in enumerate(zip(landed, late_shards)))
    wpo_f = cols_major(wpo_g)
    wdn_f = wdn_g.reshape(dn, d)
    wo_f = wo_g.reshape(d, d)
    p_out = _matmul(y_pool, wpo_f, NN, F32, 1056, 1024, 1024, "pool_out")
    q_out = _matmul(y_dn, wdn_f, NN, F32, 1056, 1024, 2048, "dn_out")
    merged = _merge_fwd(p_out, q_out, proj, o_gp, o_gd)
    mo = _matmul(merged, wo_f, NN, F32, 1056, 1024, 2048, "w_o_fwd")
    dh1, d_fw, loss_part = _final_loss(h0, mo, final_norm_w.reshape(1, d), loss_target[0], x0)

    d_merged = _matmul(dh1, wo_f, NT, F32, 1056, 1024, 1024, "w_o_bwd_x")
    g_wo = _matmul(merged.T, dh1, NN, BF16, 1024, 1024, lp, "w_o_bwd_w")
    d_p, d_q, d_gp, d_gd = _merge_bwd(p_out, q_out, proj, o_gp, o_gd, d_merged)
    d_ypool = _matmul(d_p, wpo_f, NT, F32, 1056, 1024, 2048, "pool_out_bwd_x")
    g_wpo = _matmul(y_pool.T, d_p, NN, BF16, 1024, 1024, lp, "pool_out_bwd_w", col_blocks=N_DEV)
    d_ydn = _matmul(d_q, wdn_f, NT, F32, 1056, 1024, 2048, "dn_out_bwd_x")
    g_wdn = _matmul(y_dn.T, d_q, NN, BF16, 1024, 1024, lp, "dn_out_bwd_w")
    rs_early = _rs_begin([g_wpo, g_wdn.reshape(N_DEV, dn // N_DEV, d), g_wo.reshape(N_DEV, d // N_DEV, d)], "early", split=True)
    started = rs_early["token"][0, 0]
    d_u, d_zp, g_mix, g_pscale = _pool_bwd(proj, mix_f, pool_scale + started, d_ypool, pad)
    d_qn, d_kn, d_vv, d_ba, d_zd, d_prm, g_dnw = _chunk_bwd(qn, kn, vv, proj, o_zd, n_main, prm + started, dn_norm_w, hist, tmats, d_ydn, n_heads, pad)
    d_qr, g_cq = _conv_bwd(proj, o_q, conv_q, d_qn, hd, float(hd) ** -0.5, pad, "conv_q_bwd")
    d_kr, g_ck = _conv_bwd(proj, o_k, conv_k, d_kn, hd, 1.0, pad, "conv_k_bwd")
    d_vr, g_cv = _conv_bwd(proj, o_v, conv_v, d_vv, hd, None, pad, "conv_v_bwd")
    d_proj = jnp.concatenate([d_u, d_zp, d_qr, d_kr, d_vr, d_zd, d_gp, d_gd, d_ba.astype(BF16), jnp.zeros((lp, LANES), BF16)], axis=1)
    xn_t, rs_late, token = xn.T, [], None
    for half in range(2):
        rows = slice(half * (d // 2), (half + 1) * (d // 2))
        g_wall = _matmul(xn_t[rows], d_proj, NN, BF16, 1024, 768, lp, "w_in_bwd_w_%d" % half, after=token)
        g_win = _lane_gather_cols(g_wall.reshape(1, d // 2, n_all), _lane_gather_table(to_own, n_all), N_DEV, ns, "w_in_grad_to_own_%d" % half)
        rs_late.append(_rs_begin([g_win], "late%d" % half, split=True))
        token = rs_late[-1]["token"]
    d_xn = _matmul(d_proj, w_all, NT, F32, lp, 512, 768, "w_in_bwd_x", after=token)
    d_head, grad_x, g_nw = _norm_in_bwd(h0, norm_w, d_xn, dh1, x0)
    grad_x = grad_x[None]

    by_cols = lambda t: jnp.transpose(t.reshape(t.shape[0], N_DEV, t.shape[1] // N_DEV), (1, 0, 2))
    g_conv = by_cols(jnp.concatenate([g_cq, g_ck, g_cv], axis=1)).reshape(N_DEV, kw * cs // pg, pg)
    conv_rows = -(-g_conv.shape[1] // 16) * 16
    g_small = jnp.concatenate(
        [jnp.transpose(g_mix.reshape(ng, N_DEV, mr, pg), (1, 0, 2, 3)).reshape(N_DEV, ng * mr, pg), by_cols(d_head[pad:x0]),
         jnp.pad(g_conv, ((0, 0), (0, conv_rows - g_conv.shape[1]), (0, 0)))], axis=1).astype(BF16)
    r_small, = _rs_finish(_rs_begin([g_small], "small", split=False), "small")
    r_mix, r_meta = r_small[:ng * mr], r_small[ng * mr:ng * mr + n_meta]
    r_conv = r_small[ng * mr + n_meta:ng * mr + n_meta + kw * cs // pg]
    r_wpo, r_wdn, r_wo = _rs_finish(rs_early, "early", after=r_small)

    small = [g_nw[0], d_fw[0], g_pscale[0], g_dnw[0], d_prm[0], d_prm[1], loss_part[0]]
    s_sizes = [t.shape[0] for t in small]
    s_cols = -(-sum(s_sizes) // (8 * LANES)) * LANES
    s_vec = jnp.concatenate(small + [jnp.zeros((8 * s_cols - sum(s_sizes),), F32)]).reshape(8, s_cols)
    s_red = _all_reduce_small(s_vec)
    s_sum = s_red.reshape(-1)
    r_win = None
    for half, begun in enumerate(rs_late):
        r_win = _rs_finish(begun, "late%d" % half, after=s_red, part=half, n_parts=2, dsts=r_win)
    r_win, = r_win
    s_offs = [sum(s_sizes[:i]) for i in range(len(s_sizes))]
    s_take = lambda i, n=None, o=0: s_sum[s_offs[i] + o:s_offs[i] + o + (s_sizes[i] if n is None else n)]

    grads = {
        "meta_tokens": r_meta, "norm_w": s_take(0).reshape(norm_w.shape),
        "w_in": r_win.reshape(w_in.shape), "conv_w": r_conv.reshape(conv_w.shape),
        "A_log": s_take(4, n_heads, n_heads).reshape(A_log.shape), "dt_bias": s_take(5, n_heads, n_heads).reshape(dt_bias.shape),
        "pool_mix": r_mix.reshape(pool_mix.shape), "pool_scale": s_take(2).reshape(pool_scale.shape),
        "dn_norm_w": s_take(3).reshape(dn_norm_w.shape), "w_pool_out": r_wpo.reshape(w_pool_out.shape),
        "w_dn_out": r_wdn.reshape(w_dn_out.shape), "w_o": r_wo.reshape(w_o.shape),
        "final_norm_w": s_take(1).reshape(final_norm_w.shape),
    }
    loss = s_take(6, 1)[0]

    weights = dict(meta_tokens=meta_tokens, norm_w=norm_w, w_in=w_in, conv_w=conv_w, A_log=A_log, dt_bias=dt_bias,
                   pool_mix=pool_mix, pool_scale=pool_scale, dn_norm_w=dn_norm_w, w_pool_out=w_pool_out, w_dn_out=w_dn_out,
                   w_o=w_o, final_norm_w=final_norm_w)
    m_in = dict(meta_tokens=m_meta_tokens, norm_w=m_norm_w, w_in=m_w_in, conv_w=m_conv_w, A_log=m_A_log, dt_bias=m_dt_bias,
                pool_mix=m_pool_mix, pool_scale=m_pool_scale, dn_norm_w=m_dn_norm_w, w_pool_out=m_w_pool_out,
                w_dn_out=m_w_dn_out, w_o=m_w_o, final_norm_w=m_final_norm_w)
    v_in = dict(meta_tokens=v_meta_tokens, norm_w=v_norm_w, w_in=v_w_in, conv_w=v_conv_w, A_log=v_A_log, dt_bias=v_dt_bias,
                pool_mix=v_pool_mix, pool_scale=v_pool_scale, dn_norm_w=v_dn_norm_w, w_pool_out=v_w_pool_out,
                w_dn_out=v_w_dn_out, w_o=v_w_o, final_norm_w=v_final_norm_w)
    names = list(weights)
    upd = {n: _adamw(weights[n], grads[n], m_in[n], v_in[n], "adamw_" + n) for n in names}
    return (loss, grad_x, *[grads[n] for n in names], *[upd[n][0] for n in names], *[upd[n][1] for n in names],
            *[upd[n][2] for n in names])
```

```python
import functools

import jax
import jax.numpy as jnp
import numpy as np
from jax import lax
from jax.experimental import pallas as pl
from jax.experimental.pallas import tpu as pltpu

F32 = jnp.float32
BF16 = jnp.bfloat16
HIGHEST = lax.Precision.HIGHEST
MESH = pl.DeviceIdType.MESH

CHUNK = 64
NORM_EPS = 1e-6
POOL_WINDOWS = (2, 4, 8, 16)
ADAM_LR, ADAM_B1, ADAM_B2, ADAM_EPS, ADAM_WD, ADAM_STEP = 0.001, 0.9, 0.999, 1e-08, 0.01, 10
N_DEV = 8
LANES = 128
VMEM_LIMIT = 48 * 1024 * 1024

NN = (((1,), (0,)), ((), ()))
NT = (((1,), (1,)), ((), ()))
TN = (((0,), (0,)), ((), ()))


def _call(body, **kw):
    return pl.pallas_call(body, **kw)


def _params(sem=None):
    return pltpu.CompilerParams(dimension_semantics=sem, vmem_limit_bytes=VMEM_LIMIT)


def _tile(n, pref, align):
    for d in range(min(pref, n), 0, -1):
        if n % d == 0 and d % align == 0:
            return d
    return n


def _dot(a, b, dims=NN, precision=None):
    return lax.dot_general(a, b, dims, precision=precision, preferred_element_type=F32)


def _sigmoid(x):
    return 0.5 * jnp.tanh(0.5 * x) + 0.5


def _silu(x):
    return x * _sigmoid(x)


def _softplus(x):
    return jnp.maximum(x, 0.0) + jnp.log(1.0 + jnp.exp(-jnp.abs(x)))


def _rmsnorm(x, w):
    return x * lax.rsqrt(jnp.mean(x * x, axis=-1, keepdims=True) + NORM_EPS) * w


def _shift_down(x, j, row):
    if j == 0:
        return x
    return jnp.where(row >= j, pltpu.roll(x, j, 0), 0.0)


def _shift_up(x, j, row):
    if j == 0:
        return x
    n = x.shape[0]
    return jnp.where(row < n - j, pltpu.roll(x, n - j, 0), 0.0)


def _matmul(a, b, dims, out_dtype, tm, tn, tk, name, col_blocks=None, after=None):
    ta = dims == TN
    tb = dims == NT
    m, kdim = (a.shape[1], a.shape[0]) if ta else a.shape
    n = b.shape[0] if tb else b.shape[1]
    if col_blocks:
        tn = n // col_blocks
    tm, tn, tk = _tile(m, tm, 8), _tile(n, tn, LANES), _tile(kdim, tk, LANES if not ta else 16)
    nk = kdim // tk

    n_extra = 0 if after is None else 1

    def body(a_ref, b_ref, *refs):
        o_ref, scratch = refs[n_extra], refs[n_extra + 1:]
        part = _dot(a_ref[...].astype(BF16), b_ref[...].astype(BF16), dims)
        if nk == 1:
            o_ref[...] = part.astype(o_ref.dtype).reshape(o_ref.shape)
            return
        acc_ref, = scratch
        k = pl.program_id(2)

        @pl.when(k == 0)
        def _():
            acc_ref[...] = part

        @pl.when(k > 0)
        def _():
            acc_ref[...] += part

        @pl.when(k == nk - 1)
        def _():
            o_ref[...] = acc_ref[...].astype(o_ref.dtype).reshape(o_ref.shape)

    a_spec = pl.BlockSpec((tk, tm), lambda i, j, k: (k, i)) if ta else pl.BlockSpec((tm, tk), lambda i, j, k: (i, k))
    b_spec = pl.BlockSpec((tn, tk), lambda i, j, k: (j, k)) if tb else pl.BlockSpec((tk, tn), lambda i, j, k: (k, j))
    if col_blocks:
        out_spec = pl.BlockSpec((1, tm, tn), lambda i, j, k: (j, i, 0))
        out_shape = jax.ShapeDtypeStruct((col_blocks, m, tn), out_dtype)
    else:
        out_spec = pl.BlockSpec((tm, tn), lambda i, j, k: (i, j))
        out_shape = jax.ShapeDtypeStruct((m, n), out_dtype)
    return _call(
        body, name=name, grid=(m // tm, n // tn, nk),
        in_specs=[a_spec, b_spec] + [ANY] * n_extra, out_specs=out_spec, out_shape=out_shape,
        scratch_shapes=[] if nk == 1 else [pltpu.VMEM((tm, tn), F32)],
        compiler_params=_params(("parallel", "parallel", "arbitrary")),
    )(a, b, *([] if after is None else [after]))


def _norm_in_fwd(x2d, meta, w, pad):
    seq, d = x2d.shape
    tr = pad + meta.shape[0]
    assert seq % tr == 0 and tr % 16 == 0
    lp = tr + seq

    def body(x_ref, m_ref, w_ref, h_ref, o_ref):
        def emit(h):
            h_ref[...] = h
            o_ref[...] = _rmsnorm(h, w_ref[...]).astype(BF16)

        @pl.when(pl.program_id(0) == 0)
        def _():
            emit(jnp.concatenate([jnp.zeros((pad, d), F32), m_ref[...]], axis=0) if pad else m_ref[...])

        @pl.when(pl.program_id(0) > 0)
        def _():
            emit(x_ref[...])

    row = pl.BlockSpec((tr, d), lambda i: (i, 0))
    return _call(
        body, name="norm_in_fwd", grid=(lp // tr,),
        in_specs=[pl.BlockSpec((tr, d), lambda i: (jnp.maximum(i - 1, 0), 0)), pl.BlockSpec(meta.shape, lambda i: (0, 0)),
                  pl.BlockSpec((1, d), lambda i: (0, 0))],
        out_specs=[row, row],
        out_shape=[jax.ShapeDtypeStruct((lp, d), F32), jax.ShapeDtypeStruct((lp, d), BF16)],
        compiler_params=_params(("arbitrary",)),
    )(x2d, meta, w)


def _norm_in_bwd(h0, w, dxn, dh1, x0):
    lp, d = h0.shape
    tr = x0
    assert lp % tr == 0

    def body(h_ref, w_ref, da_ref, dh1_ref, head_ref, gx_ref, dw_ref):
        i = pl.program_id(0)
        _, vjp = jax.vjp(_rmsnorm, h_ref[...], w_ref[...])
        dh, dw = vjp(da_ref[...])
        dh = dh + dh1_ref[...]

        @pl.when(i == 0)
        def _():
            head_ref[...] = dh
            dw_ref[...] = dw

        @pl.when(i > 0)
        def _():
            gx_ref[...] = dh
            dw_ref[...] += dw

    row = pl.BlockSpec((tr, d), lambda i: (i, 0))
    vec = pl.BlockSpec((1, d), lambda i: (0, 0))
    return _call(
        body, name="norm_in_bwd", grid=(lp // tr,),
        in_specs=[row, vec, row, row],
        out_specs=[pl.BlockSpec((tr, d), lambda i: (0, 0)), pl.BlockSpec((tr, d), lambda i: (jnp.maximum(i - 1, 0), 0)), vec],
        out_shape=[jax.ShapeDtypeStruct((tr, d), F32), jax.ShapeDtypeStruct((lp - tr, d), F32), jax.ShapeDtypeStruct((1, d), F32)],
        compiler_params=_params(("arbitrary",)),
    )(h0, w, dxn, dh1)


def _final_loss(h0, mo, fw, tgt, x0):
    lp, d = h0.shape
    tr = x0
    assert lp % tr == 0

    def body(h_ref, mo_ref, fw_ref, t_ref, dh_ref, dw_ref, loss_ref):
        i = pl.program_id(0)
        row = i * tr + lax.broadcasted_iota(jnp.int32, (tr, 1), 0)
        mask = jnp.where(row >= x0, 1.0, 0.0).astype(F32)
        tgt_v = t_ref[...]

        def loss_fn(h1, w):
            err = _rmsnorm(h1, w) - tgt_v
            return 0.5 * jnp.sum(jnp.mean(err * err, axis=-1, keepdims=True) * mask, axis=0, keepdims=True)

        loss, vjp = jax.vjp(loss_fn, h_ref[...] + mo_ref[...], fw_ref[...])
        dh, dw = vjp(jnp.ones((1, 1), F32))
        dh_ref[...] = dh

        @pl.when(i == 0)
        def _():
            dw_ref[...] = jnp.zeros_like(dw_ref)
            loss_ref[...] = jnp.zeros_like(loss_ref)

        dw_ref[...] += dw
        loss_ref[...] += jnp.broadcast_to(loss, loss_ref.shape)

    row_spec = pl.BlockSpec((tr, d), lambda i: (i, 0))
    vec = pl.BlockSpec((1, d), lambda i: (0, 0))
    return _call(
        body, name="final_loss", grid=(lp // tr,),
        in_specs=[row_spec, row_spec, vec, pl.BlockSpec((tr, d), lambda i: (jnp.maximum(i - 1, 0), 0))],
        out_specs=[row_spec, vec, pl.BlockSpec((8, LANES), lambda i: (0, 0))],
        out_shape=[jax.ShapeDtypeStruct((lp, d), F32), jax.ShapeDtypeStruct((1, d), F32), jax.ShapeDtypeStruct((8, LANES), F32)],
        compiler_params=_params(("arbitrary",)),
    )(h0, mo, fw, tgt)


def _pool_select(parts, g):
    out = parts[-1]
    for gi in range(len(parts) - 2, -1, -1):
        out = jnp.where(g == gi, parts[gi], out)
    return out


def _pool_count(row, g, pad):
    win = _pool_select([jnp.full(row.shape, float(w), F32) for w in POOL_WINDOWS], g)
    return jnp.maximum(jnp.minimum((row - pad + 1).astype(F32), win), 1.0)


def _pooled(u, g, row, pad):
    sums, s, span = [], u, 1
    for w in POOL_WINDOWS:
        while span < w:
            s = s + _shift_down(s, span, row)
            span *= 2
        sums.append(s)
    return _pool_select(sums, g) / _pool_count(row, g, pad) - u


def _pooled_adjoint(dp, g, row, pad):
    e = dp / _pool_count(row, g, pad)
    sums, s, span = [], e, 1
    for w in POOL_WINDOWS:
        while span < w:
            s = s + _shift_up(s, span, row)
            span *= 2
        sums.append(s)
    return _pool_select(sums, g) - dp


def _pool_specs(lp, pg, ng, z_off):
    u_spec = pl.BlockSpec((lp, pg), lambda g: (0, g))
    z_spec = pl.BlockSpec((lp, pg), lambda g: (0, z_off + g))
    mix_spec = pl.BlockSpec((1, pg, pg), lambda g: (g, 0, 0))
    vec_spec = pl.BlockSpec((1, pg), lambda g: (0, g))
    return u_spec, z_spec, mix_spec, vec_spec


def _pool_fwd(proj, mix, scale, pad):
    lp = proj.shape[0]
    ng, pg, _ = mix.shape
    pw = ng * pg

    def body(u_ref, z_ref, mix_ref, sc_ref, y_ref):
        g = pl.program_id(0)
        row = lax.broadcasted_iota(jnp.int32, (lp, 1), 0)
        pooled = _pooled(u_ref[...], g, row, pad)
        mixed = _dot(pooled.astype(BF16), mix_ref[0])
        y_ref[...] = (mixed * sc_ref[...] * _silu(z_ref[...])).astype(BF16)

    u_spec, z_spec, mix_spec, vec_spec = _pool_specs(lp, pg, ng, pw // pg)
    return _call(
        body, name="pool_fwd", grid=(ng,), in_specs=[u_spec, z_spec, mix_spec, vec_spec], out_specs=u_spec,
        out_shape=jax.ShapeDtypeStruct((lp, pw), BF16), compiler_params=_params(("parallel",)),
    )(proj, proj, mix, scale)


def _pool_bwd(proj, mix, scale, dy, pad):
    lp = proj.shape[0]
    ng, pg, _ = mix.shape
    pw = ng * pg

    def body(u_ref, z_ref, mix_ref, sc_ref, dy_ref, du_ref, dz_ref, dmix_ref, dsc_ref):
        g = pl.program_id(0)
        row = lax.broadcasted_iota(jnp.int32, (lp, 1), 0)
        real = row >= pad
        z = z_ref[...]
        pooled = _pooled(u_ref[...], g, row, pad).astype(BF16)
        mixed = _dot(pooled, mix_ref[0])
        sig = _sigmoid(z)
        sz = z * sig
        dyv = dy_ref[...]
        dsc_ref[...] = jnp.sum(dyv * mixed * sz, axis=0, keepdims=True)
        d_sz = dyv * mixed * sc_ref[...]
        dz_ref[...] = jnp.where(real, d_sz * (sig + sz * (1.0 - sig)), 0.0).astype(BF16)
        d_mixed = (dyv * sc_ref[...] * sz).astype(BF16)
        dmix_ref[0] = _dot(pooled, d_mixed, TN)
        d_pooled = _dot(d_mixed, mix_ref[0], NT)
        du_ref[...] = jnp.where(real, _pooled_adjoint(d_pooled, g, row, pad), 0.0).astype(BF16)

    u_spec, z_spec, mix_spec, vec_spec = _pool_specs(lp, pg, ng, pw // pg)
    return _call(
        body, name="pool_bwd", grid=(ng,),
        in_specs=[u_spec, z_spec, mix_spec, vec_spec, u_spec], out_specs=[u_spec, u_spec, mix_spec, vec_spec],
        out_shape=[jax.ShapeDtypeStruct((lp, pw), BF16), jax.ShapeDtypeStruct((lp, pw), BF16),
                   jax.ShapeDtypeStruct((ng, pg, pg), F32), jax.ShapeDtypeStruct((1, pw), F32)],
        compiler_params=_params(("parallel",)),
    )(proj, proj, mix, scale, dy)


def _conv_pre(x, w, row):
    kw = w.shape[0]
    y = w[kw - 1:kw, :] * x
    for kk in range(kw - 1):
        y = y + w[kk:kk + 1, :] * _shift_down(x, kw - 1 - kk, row)
    return y


def _conv_post(y, out_scale):
    s = _silu(y)
    if out_scale is None:
        return s
    return s * lax.rsqrt(jnp.sum(s * s, axis=-1, keepdims=True) + NORM_EPS) * out_scale


def _conv_fwd(proj, col_off, w, hd, out_scale, name):
    lp = proj.shape[0]
    kw, width = w.shape
    blk0 = col_off // hd

    def body(x_ref, w_ref, o_ref):
        row = lax.broadcasted_iota(jnp.int32, (lp, 1), 0)
        o_ref[...] = _conv_post(_conv_pre(x_ref[...], w_ref[...], row), out_scale)

    return _call(
        body, name=name, grid=(width // hd,),
        in_specs=[pl.BlockSpec((lp, hd), lambda j: (0, blk0 + j)), pl.BlockSpec((kw, hd), lambda j: (0, j))],
        out_specs=pl.BlockSpec((lp, hd), lambda j: (0, j)),
        out_shape=jax.ShapeDtypeStruct((lp, width), F32), compiler_params=_params(("parallel",)),
    )(proj, w)


def _conv_bwd(proj, col_off, w, d_out, hd, out_scale, pad, name):
    lp = proj.shape[0]
    kw, width = w.shape
    blk0 = col_off // hd

    def body(x_ref, w_ref, do_ref, dx_ref, dw_ref):
        row = lax.broadcasted_iota(jnp.int32, (lp, 1), 0)
        real = row >= pad
        x, wv = x_ref[...], w_ref[...]
        _, vjp = jax.vjp(functools.partial(_conv_post, out_scale=out_scale), _conv_pre(x, wv, row))
        dy = jnp.where(real, vjp(do_ref[...])[0], 0.0)
        dx = wv[kw - 1:kw, :] * dy
        dw_ref[kw - 1:kw, :] = jnp.sum(dy * x, axis=0, keepdims=True)
        for kk in range(kw - 1):
            j = kw - 1 - kk
            dx = dx + wv[kk:kk + 1, :] * _shift_up(dy, j, row)
            dw_ref[kk:kk + 1, :] = jnp.sum(dy * _shift_down(x, j, row), axis=0, keepdims=True)
        dx_ref[...] = jnp.where(real, dx, 0.0).astype(BF16)

    col = pl.BlockSpec((lp, hd), lambda j: (0, j))
    wspec = pl.BlockSpec((kw, hd), lambda j: (0, j))
    return _call(
        body, name=name, grid=(width // hd,),
        in_specs=[pl.BlockSpec((lp, hd), lambda j: (0, blk0 + j)), wspec, col], out_specs=[col, wspec],
        out_shape=[jax.ShapeDtypeStruct((lp, width), BF16), jax.ShapeDtypeStruct((kw, width), F32)],
        compiler_params=_params(("parallel",)),
    )(proj, w, d_out)


HEADS_PER_STEP = 16
HEADS_PER_STEP_BWD = 8


def _each(fn, *lists):
    return [fn(*args) for args in zip(*lists)]


def _dot3_each(a_list, b_list, dims=NN):
    hi = lambda t: t.astype(BF16)
    lo = lambda t, t_hi: (t - t_hi.astype(F32)).astype(BF16)
    dot = lambda x, y: _dot(x, y, dims)
    a_hi, b_hi = _each(hi, a_list), _each(hi, b_list)
    a_lo, b_lo = _each(lo, a_list, a_hi), _each(lo, b_list, b_hi)
    hh, hl, lh = _each(dot, a_hi, b_hi), _each(dot, a_hi, b_lo), _each(dot, a_lo, b_hi)
    return _each(lambda x, y, w: x + (y + w), hh, hl, lh)


@jax.custom_vjp
def _unit_lower_inverse(lmats):
    c = lmats[0].shape[0]
    eye = lax.broadcasted_iota(jnp.int32, (c, c), 0) == lax.broadcasted_iota(jnp.int32, (c, c), 1)
    a = [-m for m in lmats]
    tmat = [jnp.where(eye, 1.0, 0.0).astype(F32) + m for m in a]
    span = 2
    while span < c:
        a = _dot3_each(a, a)
        tmat = _each(lambda t, u: t + u, tmat, _dot3_each(tmat, a))
        span *= 2
    return tuple(tmat)


def _unit_lower_inverse_fwd(lmats):
    tmats = _unit_lower_inverse(lmats)
    return tmats, tmats


def _unit_lower_inverse_bwd(tmats, cts):
    left = _each(lambda t, ct: _dot(t, ct, TN, HIGHEST), tmats, cts)
    return (tuple(_each(lambda m, t: -_dot(m, t, NT, HIGHEST), left, tmats)),)


_unit_lower_inverse.defvjp(_unit_lower_inverse_fwd, _unit_lower_inverse_bwd)


@jax.custom_vjp
def _known_inverse(lmats, tmats):
    return tmats


def _known_inverse_fwd(lmats, tmats):
    return tmats, tmats


def _known_inverse_bwd(tmats, cts):
    return _unit_lower_inverse_bwd(tmats, cts)[0], tuple(jnp.zeros_like(t) for t in tmats)


_known_inverse.defvjp(_known_inverse_fwd, _known_inverse_bwd)


def _chunk_math(states, q, k, v, ba, z, prm, nw, head0, rowmask, n_heads, tmats=None, keep_tmats=False):
    c = q.shape[0]
    heads = list(range(len(states)))
    hd = q.shape[1] // len(states)
    lane = lax.broadcasted_iota(jnp.int32, ba.shape, 1)
    sub = lax.broadcasted_iota(jnp.int32, (ba.shape[1], c), 0)
    ri = lax.broadcasted_iota(jnp.int32, (c, c), 0)
    ci = lax.broadcasted_iota(jnp.int32, (c, c), 1)
    last = lax.broadcasted_iota(jnp.int32, (c, 1), 0) == c - 1
    causal, strict = ri >= ci, ri > ci
    beta_all = _sigmoid(ba) * rowmask
    g_all = -jnp.exp(prm[0:1, :]) * _softplus(ba + prm[1:2, :]) * rowmask
    gcum_all = _dot(jnp.where(causal, 1.0, 0.0).astype(F32), g_all, precision=HIGHEST)
    gcum_t = gcum_all.T
    split = lambda t: [t[:, j * hd:(j + 1) * hd] for j in heads]
    qs, ks, vs, zs = split(q), split(k), split(v), split(z)
    beta = [jnp.sum(jnp.where(lane == head0 + j, beta_all, 0.0), axis=1, keepdims=True) for j in heads]
    gcum = [jnp.sum(jnp.where(lane == n_heads + head0 + j, gcum_all, 0.0), axis=1, keepdims=True) for j in heads]
    grow = [jnp.sum(jnp.where(sub == n_heads + head0 + j, gcum_t, 0.0), axis=0, keepdims=True) for j in heads]
    glast = _each(lambda gc: jnp.sum(jnp.where(last, gc, 0.0), axis=0, keepdims=True), gcum)
    decay = _each(lambda gc, gr: jnp.where(causal, jnp.exp(jnp.where(causal, gc - gr, 0.0)), 0.0), gcum, grow)
    eg = _each(jnp.exp, gcum)
    k_beta = _each(jnp.multiply, ks, beta)
    kk = _each(lambda a, b: _dot(a, b, NT), k_beta, ks)
    lmats = tuple(_each(lambda m, dc: jnp.where(strict, m * dc, 0.0), kk, decay))
    tmat = list(_unit_lower_inverse(lmats) if tmats is None else _known_inverse(lmats, tuple(tmats)))
    u_c = _each(_dot, tmat, _each(jnp.multiply, vs, beta))
    w_c = _each(_dot, tmat, _each(jnp.multiply, k_beta, eg))
    qk = _each(lambda a, b, dc: jnp.where(causal, _dot(a, b, NT) * dc, 0.0), qs, ks, decay)
    v_new = _each(lambda u, w, s: u - _dot(w, s), u_c, w_c, list(states))
    o = _each(lambda a, e, s, m, vn: _dot(a * e, s) + _dot(m, vn), qs, eg, list(states), qk, v_new)
    k_dec = _each(lambda a, gl, gc: a * jnp.exp(gl - gc), ks, glast, gcum)
    new_states = _each(lambda s, gl, kd, vn: s * jnp.exp(gl) + _dot(kd, vn, TN), list(states), glast, k_dec, v_new)
    ys = _each(lambda oj, zj: _rmsnorm(oj, nw) * _silu(zj), o, zs)
    if keep_tmats:
        return jnp.concatenate(ys, axis=1), tuple(new_states), tuple(tmat)
    return jnp.concatenate(ys, axis=1), tuple(new_states)


def _chunk_specs(nc, hd, n_heads, z_off, ba_off, rev):
    cidx = (lambda c: nc - 1 - c) if rev else (lambda c: c)
    hb = min(HEADS_PER_STEP_BWD if rev else HEADS_PER_STEP, n_heads)
    assert n_heads % hb == 0 and z_off % (hb * hd) == 0 and ba_off % LANES == 0
    blk = lambda off: pl.BlockSpec((CHUNK, hb * hd), lambda c, g: (cidx(c), off + g))
    ba_spec = lambda off: pl.BlockSpec((CHUNK, LANES), lambda c, g: (cidx(c), off // LANES))
    prm_spec = pl.BlockSpec((8, LANES), lambda c, g: (0, 0))
    nw_spec = pl.BlockSpec((1, hd), lambda c, g: (0, 0))
    st_spec = pl.BlockSpec((1, hb, hd, hd), lambda c, g: (cidx(c), g, 0, 0))
    return blk, ba_spec, prm_spec, nw_spec, st_spec, blk(z_off // (hb * hd))


def _rowmask(chunk_idx, pad):
    row = chunk_idx * CHUNK + lax.broadcasted_iota(jnp.int32, (CHUNK, 1), 0)
    return jnp.where(row >= pad, 1.0, 0.0).astype(F32)


def _chunk_fwd(qn, kn, vv, proj, z_off, ba_off, prm, nw, n_heads, pad):
    lp, dn = qn.shape
    hd = dn // n_heads
    nc = lp // CHUNK
    hb = min(HEADS_PER_STEP, n_heads)

    def body(q_ref, k_ref, v_ref, ba_ref, z_ref, prm_ref, nw_ref, y_ref, hist_ref, tm_ref, st_ref):
        c, g = pl.program_id(0), pl.program_id(1)

        @pl.when(c == 0)
        def _():
            for j in range(hb):
                st_ref[g * hb + j] = jnp.zeros((hd, hd), F32)

        states = tuple(st_ref[g * hb + j] for j in range(hb))
        for j in range(hb):
            hist_ref[0, j] = states[j]
        y, new_states, tmats = _chunk_math(states, q_ref[...], k_ref[...], v_ref[...], ba_ref[...], z_ref[...], prm_ref[...],
                                           nw_ref[...], g * hb, _rowmask(c, pad), n_heads, keep_tmats=True)
        y_ref[...] = y.astype(BF16)
        for j in range(hb):
            st_ref[g * hb + j] = new_states[j]
            tm_ref[0, j] = tmats[j]

    blk, ba_spec, prm_spec, nw_spec, st_spec, z_spec = _chunk_specs(nc, hd, n_heads, z_off, ba_off, False)
    tm_spec = pl.BlockSpec((1, hb, CHUNK, CHUNK), lambda c, g: (c, g, 0, 0))
    return _call(
        body, name="chunk_fwd", grid=(nc, n_heads // hb),
        in_specs=[blk(0), blk(0), blk(0), ba_spec(ba_off), z_spec, prm_spec, nw_spec], out_specs=[blk(0), st_spec, tm_spec],
        out_shape=[jax.ShapeDtypeStruct((lp, dn), BF16), jax.ShapeDtypeStruct((nc, n_heads, hd, hd), F32),
                   jax.ShapeDtypeStruct((nc, n_heads, CHUNK, CHUNK), F32)],
        scratch_shapes=[pltpu.VMEM((n_heads, hd, hd), F32)],
        compiler_params=_params(("arbitrary", "arbitrary")),
    )(qn, kn, vv, proj, proj, prm, nw)


def _chunk_bwd(qn, kn, vv, proj, z_off, ba_off, prm, nw, hist, tmats, dy, n_heads, pad):
    lp, dn = qn.shape
    hd = dn // n_heads
    nc = lp // CHUNK
    hb = min(HEADS_PER_STEP_BWD, n_heads)

    def body(q_ref, k_ref, v_ref, ba_ref, z_ref, prm_ref, nw_ref, hist_ref, tm_ref, dy_ref,
             dq_ref, dk_ref, dv_ref, dba_ref, dz_ref, dprm_ref, dnw_ref, dst_ref):
        step, g = pl.program_id(0), pl.program_id(1)

        @pl.when(step == 0)
        def _():
            for j in range(hb):
                dst_ref[g * hb + j] = jnp.zeros((hd, hd), F32)

        @pl.when((step == 0) & (g == 0))
        def _():
            dprm_ref[...] = jnp.zeros_like(dprm_ref)
            dnw_ref[...] = jnp.zeros_like(dnw_ref)

        @pl.when(g == 0)
        def _():
            dba_ref[...] = jnp.zeros_like(dba_ref)

        def fn(states, q, k, v, ba, z, prm_v, nw_v, known):
            return _chunk_math(states, q, k, v, ba, z, prm_v, nw_v, g * hb, _rowmask(nc - 1 - step, pad), n_heads, tmats=known)

        states = tuple(hist_ref[0, j] for j in range(hb))
        known = tuple(tm_ref[0, j] for j in range(hb))
        _, vjp = jax.vjp(fn, states, q_ref[...], k_ref[...], v_ref[...], ba_ref[...], z_ref[...], prm_ref[...], nw_ref[...], known)
        dst, dq, dk, dv, dba, dz, dprm, dnw, _ = vjp((dy_ref[...], tuple(dst_ref[g * hb + j] for j in range(hb))))
        for j in range(hb):
            dst_ref[g * hb + j] = dst[j]
        dq_ref[...] = dq
        dk_ref[...] = dk
        dv_ref[...] = dv
        dz_ref[...] = dz.astype(BF16)
        dba_ref[...] += dba
        dprm_ref[...] += dprm
        dnw_ref[...] += dnw

    blk, ba_spec, prm_spec, nw_spec, st_spec, z_spec = _chunk_specs(nc, hd, n_heads, z_off, ba_off, True)
    f32_full = jax.ShapeDtypeStruct((lp, dn), F32)
    tm_spec = pl.BlockSpec((1, hb, CHUNK, CHUNK), lambda c, g: (nc - 1 - c, g, 0, 0))
    return _call(
        body, name="chunk_bwd", grid=(nc, n_heads // hb),
        in_specs=[blk(0), blk(0), blk(0), ba_spec(ba_off), z_spec, prm_spec, nw_spec, st_spec, tm_spec, blk(0)],
        out_specs=[blk(0), blk(0), blk(0), ba_spec(0), blk(0), prm_spec, nw_spec],
        out_shape=[f32_full, f32_full, f32_full, jax.ShapeDtypeStruct((lp, LANES), F32), jax.ShapeDtypeStruct((lp, dn), BF16),
                   jax.ShapeDtypeStruct((8, LANES), F32), jax.ShapeDtypeStruct((1, hd), F32)],
        scratch_shapes=[pltpu.VMEM((n_heads, hd, hd), F32)],
        compiler_params=_params(("arbitrary", "arbitrary")),
    )(qn, kn, vv, proj, proj, prm, nw, hist, tmats, dy)


def _merge_math(p, q, gp, gd):
    return _sigmoid(gp) * p + _sigmoid(gd) * q


def _merge_specs(lp, d, gp_off, gd_off):
    tr, tc = _tile(lp, 264, 16), _tile(d, 1024, LANES)
    blk = pl.BlockSpec((tr, tc), lambda i, j: (i, j))
    gp_spec = pl.BlockSpec((tr, tc), lambda i, j: (i, gp_off // tc + j))
    gd_spec = pl.BlockSpec((tr, tc), lambda i, j: (i, gd_off // tc + j))
    return (lp // tr, d // tc), blk, gp_spec, gd_spec


def _merge_fwd(p, q, proj, gp_off, gd_off):
    lp, d = p.shape
    grid, blk, gp_spec, gd_spec = _merge_specs(lp, d, gp_off, gd_off)

    def body(p_ref, q_ref, gp_ref, gd_ref, o_ref):
        o_ref[...] = _merge_math(p_ref[...], q_ref[...], gp_ref[...], gd_ref[...]).astype(BF16)

    return _call(
        body, name="merge_fwd", grid=grid, in_specs=[blk, blk, gp_spec, gd_spec], out_specs=blk,
        out_shape=jax.ShapeDtypeStruct((lp, d), BF16), compiler_params=_params(("parallel", "parallel")),
    )(p, q, proj, proj)


def _merge_bwd(p, q, proj, gp_off, gd_off, dm):
    lp, d = p.shape
    grid, blk, gp_spec, gd_spec = _merge_specs(lp, d, gp_off, gd_off)

    def body(p_ref, q_ref, gp_ref, gd_ref, dm_ref, dp_ref, dq_ref, dgp_ref, dgd_ref):
        _, vjp = jax.vjp(_merge_math, p_ref[...], q_ref[...], gp_ref[...], gd_ref[...])
        for ref, val in zip((dp_ref, dq_ref, dgp_ref, dgd_ref), vjp(dm_ref[...])):
            ref[...] = val.astype(BF16)

    out = jax.ShapeDtypeStruct((lp, d), BF16)
    return _call(
        body, name="merge_bwd", grid=grid, in_specs=[blk, blk, gp_spec, gd_spec, blk], out_specs=[blk] * 4,
        out_shape=[out] * 4, compiler_params=_params(("parallel", "parallel")),
    )(p, q, proj, proj, dm)


def _adamw(w, g, m, v, name):
    shape = w.shape
    w2, g2, m2, v2 = (t.reshape((-1, shape[-1])) for t in (w, g, m, v))
    rows, cols = w2.shape
    tr = _tile(rows, 128, 8)

    def body(w_ref, g_ref, m_ref, v_ref, d_ref, nm_ref, nv_ref):
        gv = g_ref[...]
        nm = ADAM_B1 * m_ref[...] + (1.0 - ADAM_B1) * gv
        nv = ADAM_B2 * v_ref[...] + (1.0 - ADAM_B2) * (gv * gv)
        m_hat = nm / (1.0 - ADAM_B1 ** ADAM_STEP)
        v_hat = nv / (1.0 - ADAM_B2 ** ADAM_STEP)
        d_ref[...] = -ADAM_LR * (m_hat / (jnp.sqrt(v_hat) + ADAM_EPS) + ADAM_WD * w_ref[...])
        nm_ref[...] = nm
        nv_ref[...] = nv

    blk = pl.BlockSpec((tr, cols), lambda i: (i, 0))
    out = jax.ShapeDtypeStruct((rows, cols), F32)
    res = _call(
        body, name=name, grid=(rows // tr,), in_specs=[blk] * 4, out_specs=[blk] * 3, out_shape=[out] * 3,
        compiler_params=_params(("parallel",)),
    )(w2, g2, m2, v2)
    return tuple(t.reshape(shape) for t in res)


def _coords():
    return lax.axis_index("x"), lax.axis_index("y"), lax.axis_index("c")


def _flip(v, bit):
    return 1 - v if bit else v


CHIP_FLIPS = ((1, 0), (0, 1), (1, 1))
ANY = pl.BlockSpec(memory_space=pl.ANY)


def _all_gather(shards):
    n = len(shards)

    def body(*refs):
        x_refs, out_refs = refs[:n], refs[n:2 * n]
        send_sems, recv_sems, local_sems = refs[2 * n:]
        x, y, c = _coords()
        sibling = (x, y, 1 - c)
        chips = [(_flip(x, fx), _flip(y, fy)) for fx, fy in CHIP_FLIPS]

        def copy(a, k, block, to, from_input=False):
            px, py, pc = block
            slot = out_refs[a].at[4 * px + 2 * py + pc]
            return pltpu.make_async_remote_copy(
                src_ref=x_refs[a] if from_input else slot, dst_ref=slot,
                send_sem=send_sems.at[7 * a + k], recv_sem=recv_sems.at[7 * a + k], device_id=to, device_id_type=MESH)

        mine = [pltpu.make_async_copy(x_refs[a], out_refs[a].at[4 * x + 2 * y + c], local_sems.at[a]) for a in range(n)]
        first = []
        for a in range(n):
            mine[a].start()
            first.append(copy(a, 0, (x, y, c), sibling, True))
            first += [copy(a, 1 + j, (x, y, c), (*chip, c), True) for j, chip in enumerate(chips)]
        for cp in first:
            cp.start()
        passed = []
        for j, chip in enumerate(chips):
            for a in range(n):
                copy(a, 1 + j, (*chip, c), (x, y, c)).wait_recv()
                passed.append(copy(a, 4 + j, (*chip, c), sibling))
                passed[-1].start()
        for a in range(n):
            copy(a, 0, (x, y, 1 - c), (x, y, c)).wait_recv()
            for j, chip in enumerate(chips):
                copy(a, 4 + j, (*chip, 1 - c), (x, y, c)).wait_recv()
        for cp in first + passed:
            cp.wait_send()
        for cp in mine:
            cp.wait()

    return _call(
        body, name="all_gather", in_specs=[ANY] * n, out_specs=[ANY] * n,
        out_shape=[jax.ShapeDtypeStruct((N_DEV,) + s.shape, s.dtype) for s in shards],
        scratch_shapes=[pltpu.SemaphoreType.DMA((7 * n,)), pltpu.SemaphoreType.DMA((7 * n,)), pltpu.SemaphoreType.DMA((n,))],
    )(*shards)


def _all_gather_tree(shard, after):
    rows, cols = shard.shape
    half = rows // 2
    assert rows % 32 == 0

    def body(x_ref, after_ref, out_ref, send_sems, recv_sems, local_sem):
        x, y, c = _coords()
        me, sibling = (x, y, c), (x, y, 1 - c)
        x_nbr, y_nbr, diag = (1 - x, y), (x, 1 - y), (1 - x, 1 - y)

        def part(ref, h):
            return ref if h is None else ref.at[pl.ds(h * half, half)]

        def copy(k, block, to, h=None, from_input=False):
            px, py, pc = block
            slot = part(out_ref.at[4 * px + 2 * py + pc], h)
            return pltpu.make_async_remote_copy(
                src_ref=part(x_ref, h) if from_input else slot, dst_ref=slot,
                send_sem=send_sems.at[k], recv_sem=recv_sems.at[k], device_id=to, device_id_type=MESH)

        mine = pltpu.make_async_copy(x_ref, out_ref.at[4 * x + 2 * y + c], local_sem)
        mine.start()
        started = [copy(0, me, sibling, None, True),
                   copy(1, me, (*x_nbr, c), 0, True), copy(2, me, (*x_nbr, c), 1, True),
                   copy(4, me, (*y_nbr, c), 1, True), copy(3, me, (*y_nbr, c), 0, True)]
        for cp in started:
            cp.start()
        copy(1, (*x_nbr, c), me, 0).wait_recv()
        started.append(copy(5, (*x_nbr, c), (*y_nbr, c), 0))
        started[-1].start()
        copy(4, (*y_nbr, c), me, 1).wait_recv()
        started.append(copy(6, (*y_nbr, c), (*x_nbr, c), 1))
        started[-1].start()
        copy(2, (*x_nbr, c), me, 1).wait_recv()
        started.append(copy(7, (*x_nbr, c), sibling))
        started[-1].start()
        copy(3, (*y_nbr, c), me, 0).wait_recv()
        started.append(copy(8, (*y_nbr, c), sibling))
        started[-1].start()
        copy(5, (*diag, c), me, 0).wait_recv()
        copy(6, (*diag, c), me, 1).wait_recv()
        started.append(copy(9, (*diag, c), sibling))
        started[-1].start()
        copy(0, sibling, me).wait_recv()
        for k, chip in ((7, x_nbr), (8, y_nbr), (9, diag)):
            copy(k, (*chip, 1 - c), me).wait_recv()
        for cp in started:
            cp.wait_send()
        mine.wait()

    return _call(
        body, name="all_gather_tree", in_specs=[ANY, ANY], out_specs=ANY,
        out_shape=jax.ShapeDtypeStruct((N_DEV, rows, cols), shard.dtype),
        scratch_shapes=[pltpu.SemaphoreType.DMA((10,)), pltpu.SemaphoreType.DMA((10,)), pltpu.SemaphoreType.DMA],
    )(shard, after)


def _rs_to_sibling(gs, name):
    n = len(gs)

    def body(*refs):
        g_refs, got_refs = refs[:n], refs[n:2 * n]
        send_sems, recv_sems = refs[2 * n:]
        x, y, c = _coords()
        copies = []
        for a in range(n):
            for p in range(4):
                cp = pltpu.make_async_remote_copy(
                    src_ref=g_refs[a].at[2 * p + (1 - c)], dst_ref=got_refs[a].at[p], send_sem=send_sems.at[4 * a + p],
                    recv_sem=recv_sems.at[4 * a + p], device_id=(x, y, 1 - c), device_id_type=MESH)
                cp.start()
                copies.append(cp)
        for cp in copies:
            cp.wait()

    return _call(
        body, name=name, in_specs=[ANY] * n, out_specs=[ANY] * n,
        out_shape=[jax.ShapeDtypeStruct((4,) + g.shape[1:], g.dtype) for g in gs],
        scratch_shapes=[pltpu.SemaphoreType.DMA((4 * n,)), pltpu.SemaphoreType.DMA((4 * n,))],
    )(*gs)


def _rs_pair_sum(g, got, c_idx, name):
    _, rows, cols = g.shape
    tr = _tile(rows, 256, 16)

    def body(c_ref, g_ref, got_ref, o_ref):
        o_ref[...] = (g_ref[...].astype(F32) + got_ref[...].astype(F32)).astype(o_ref.dtype)

    grid_spec = pltpu.PrefetchScalarGridSpec(
        num_scalar_prefetch=1, grid=(4, rows // tr),
        in_specs=[pl.BlockSpec((1, tr, cols), lambda p, i, c_ref: (2 * p + c_ref[0], i, 0)),
                  pl.BlockSpec((1, tr, cols), lambda p, i, c_ref: (p, i, 0))],
        out_specs=pl.BlockSpec((1, tr, cols), lambda p, i, c_ref: (p, i, 0)))
    return _call(
        body, name=name, grid_spec=grid_spec, out_shape=jax.ShapeDtypeStruct((4, rows, cols), g.dtype),
        compiler_params=_params(("parallel", "parallel")),
    )(c_idx, g, got)


def _to_chips_copies(p_refs, got_refs, send_sems, recv_sems):
    x, y, c = _coords()
    copies = []
    for a in range(len(p_refs)):
        for k, (fx, fy) in enumerate(CHIP_FLIPS):
            px, py = _flip(x, fx), _flip(y, fy)
            copies.append(pltpu.make_async_remote_copy(
                src_ref=p_refs[a].at[2 * px + py], dst_ref=got_refs[a].at[k], send_sem=send_sems.at[3 * a + k],
                recv_sem=recv_sems.at[3 * a + k], device_id=(px, py, c), device_id_type=MESH))
    return copies


def _rs_to_chips(partials, name):
    n = len(partials)

    def body(*refs):
        copies = _to_chips_copies(refs[:n], refs[n:2 * n], *refs[2 * n:])
        for cp in copies:
            cp.start()
        for cp in copies:
            cp.wait()

    return _call(
        body, name=name, in_specs=[ANY] * n, out_specs=[ANY] * n,
        out_shape=[jax.ShapeDtypeStruct((3,) + p.shape[1:], p.dtype) for p in partials],
        scratch_shapes=[pltpu.SemaphoreType.DMA((3 * n,)), pltpu.SemaphoreType.DMA((3 * n,))],
    )(*partials)


HBM = pl.BlockSpec(memory_space=pltpu.HBM)
SEM = pl.BlockSpec(memory_space=pltpu.SEMAPHORE)
SIDE_EFFECT = pltpu.CompilerParams(has_side_effects=pltpu.SideEffectType.DATAFLOW_SIDE_EFFECTING)


def _split_start(copies_fn, srcs, land_shapes, n_sems, name, after=None):
    n, m = len(srcs), len(land_shapes)
    extra = [] if after is None else [after]

    def body(*refs):
        outs = refs[n + m + len(extra):]
        send_sems, recv_sems, token = outs[0], outs[1], outs[-1]
        for cp in copies_fn(refs[:n], refs[n:n + m], send_sems, recv_sems):
            cp.start()
        token[...] = jnp.zeros_like(token)

    ins = [pltpu.with_memory_space_constraint(t, pltpu.HBM) for t in list(srcs) + [lax.empty(s.shape, s.dtype) for s in land_shapes]]
    res = _call(
        body, name=name, in_specs=[HBM] * (n + m) + [ANY] * len(extra),
        out_specs=[SEM, SEM] + [HBM] * (n + m) + [pl.BlockSpec(memory_space=pltpu.VMEM)],
        out_shape=[pltpu.SemaphoreType.DMA((n_sems,)), pltpu.SemaphoreType.DMA((n_sems,))]
        + [pltpu.HBM(t.shape, t.dtype) for t in ins] + [jax.ShapeDtypeStruct((8, LANES), F32)],
        input_output_aliases={i: 2 + i for i in range(n + m)}, compiler_params=SIDE_EFFECT,
    )(*ins, *extra)
    return dict(sems=(res[0], res[1]), srcs=res[2:2 + n], lands=res[2 + n:2 + n + m], token=res[-1])


def _split_wait(copies_fn, started, after, name):
    n, m = len(started["srcs"]), len(started["lands"])

    def body(*refs):
        for cp in copies_fn(refs[:n], refs[n:n + m], refs[n + m], refs[n + m + 1]):
            cp.wait_send()
            cp.wait_recv()

    bufs = list(started["srcs"]) + list(started["lands"])
    res = _call(
        body, name=name, in_specs=[HBM] * (n + m) + [SEM, SEM, ANY], out_specs=[HBM] * (n + m),
        out_shape=[pltpu.HBM(t.shape, t.dtype) for t in bufs],
        input_output_aliases={i: i for i in range(n + m)}, compiler_params=SIDE_EFFECT,
    )(*bufs, *started["sems"], after)
    return res[:n], res[n:]


def _to_all_copies(x_refs, out_refs, send_sems, recv_sems):
    x, y, c = _coords()
    copies = []
    for a in range(len(x_refs)):
        for k in range(N_DEV - 1):
            fx, fy, fc = ((k + 1) >> 2) & 1, ((k + 1) >> 1) & 1, (k + 1) & 1
            copies.append(pltpu.make_async_remote_copy(
                src_ref=x_refs[a], dst_ref=out_refs[a].at[4 * x + 2 * y + c], send_sem=send_sems.at[7 * a + k],
                recv_sem=recv_sems.at[7 * a + k], device_id=(_flip(x, fx), _flip(y, fy), _flip(c, fc)), device_id_type=MESH))
    return copies


def _fill_own_block(gathered, shard, me_idx, name):
    rows, cols = shard.shape
    tr = _tile(rows, 512, 16)

    def body(me_ref, g_ref, s_ref, o_ref):
        o_ref[0] = s_ref[...]

    grid_spec = pltpu.PrefetchScalarGridSpec(
        num_scalar_prefetch=1, grid=(rows // tr,),
        in_specs=[ANY, pl.BlockSpec((tr, cols), lambda i, me: (i, 0))],
        out_specs=pl.BlockSpec((1, tr, cols), lambda i, me: (me[0], i, 0)))
    return _call(
        body, name=name, grid_spec=grid_spec, out_shape=jax.ShapeDtypeStruct(gathered.shape, gathered.dtype),
        input_output_aliases={1: 0}, compiler_params=_params(("arbitrary",)),
    )(me_idx, gathered, shard)


def _rs_chip_sum(partial, got, chip_idx, name, part=0, n_parts=1, dst=None):
    _, rows, cols = partial.shape
    tr = _tile(rows, 256, 16)
    steps = rows // tr
    n_dst = 0 if dst is None else 1

    def body(p_idx_ref, p_ref, got_ref, *refs):
        refs[n_dst][...] = ((p_ref[0].astype(F32) + got_ref[0].astype(F32)) + got_ref[1].astype(F32)) + got_ref[2].astype(F32)

    grid_spec = pltpu.PrefetchScalarGridSpec(
        num_scalar_prefetch=1, grid=(steps,),
        in_specs=[pl.BlockSpec((1, tr, cols), lambda i, p_ref: (p_ref[0], i, 0)),
                  pl.BlockSpec((3, tr, cols), lambda i, p_ref: (0, i, 0))] + [ANY] * n_dst,
        out_specs=pl.BlockSpec((tr, cols), lambda i, p_ref: (part * steps + i, 0)))
    return _call(
        body, name=name, grid_spec=grid_spec, out_shape=jax.ShapeDtypeStruct((n_parts * rows, cols), F32),
        input_output_aliases={3: 0} if n_dst else {}, compiler_params=_params(("parallel",)),
    )(chip_idx, partial, got, *([] if dst is None else [dst]))


def _rs_begin(gs, tag, split):
    c_idx = jnp.reshape(lax.axis_index("c"), (1,)).astype(jnp.int32)
    gots = _rs_to_sibling(gs, "rs_to_sibling_" + tag)
    partials = [_rs_pair_sum(g, got, c_idx, "rs_pair_sum_%s%d" % (tag, a)) for a, (g, got) in enumerate(zip(gs, gots))]
    if not split:
        return dict(partials=partials, gots=_rs_to_chips(partials, "rs_to_chips_" + tag))
    lands = [jax.ShapeDtypeStruct((3,) + p.shape[1:], p.dtype) for p in partials]
    return _split_start(_to_chips_copies, partials, lands, 3 * len(partials), "rs_to_chips_start_" + tag)


def _rs_finish(begun, tag, after=None, part=0, n_parts=1, dsts=None):
    x, y, _ = _coords()
    chip_idx = jnp.reshape(2 * x + y, (1,)).astype(jnp.int32)
    if "gots" in begun:
        partials, gots = begun["partials"], begun["gots"]
    else:
        partials, gots = _split_wait(_to_chips_copies, begun, after, "rs_to_chips_wait_" + tag)
    return [_rs_chip_sum(p, got, chip_idx, "rs_chip_sum_%s%d" % (tag, a), part, n_parts, None if dsts is None else dsts[a])
            for a, (p, got) in enumerate(zip(partials, gots))]


RUNS = 3
RUN_FIELDS = 6


def _lane_gather_table(src_of, src_width):
    n_blocks = src_of.shape[0] // LANES
    tab = np.zeros((n_blocks + 1, RUNS, RUN_FIELDS), np.int32)
    tab[:, :, 5] = LANES
    for t in range(n_blocks):
        runs = []
        for lane in range(LANES):
            slab, col = (int(v) for v in src_of[t * LANES + lane])
            if slab < 0:
                continue
            key = (slab, col // LANES, col % LANES - lane)
            if runs and runs[-1][0] == key and runs[-1][2] == lane:
                runs[-1][2] = lane + 1
            else:
                runs.append([key, lane, lane + 1])
        assert len(runs) <= RUNS
        slots = [None] * RUNS
        for key, lo, hi in sorted(runs, key=lambda r: r[0][:2]):
            e = key[1] % 2 if slots[key[1] % 2] is None else slots.index(None)
            slots[e] = (key[0], key[1], key[2], lo, hi, min(LANES, src_width - key[1] * LANES))
        for e in range(RUNS):
            tab[t, e] = slots[e] if slots[e] is not None else (tab[t - 1, e, 0], tab[t - 1, e, 1], 0, 0, 0, LANES) if t else tab[t, e]
    tab[n_blocks, :, :2] = tab[n_blocks - 1, :, :2]
    return tab.reshape(-1)


def _place_run(tab_ref, t, e, block, under):
    base = (t * RUNS + e) * RUN_FIELDS
    shift, lo, hi = tab_ref[base + 2], tab_ref[base + 3], tab_ref[base + 4]
    lane = lax.broadcasted_iota(jnp.int32, (1, LANES), 1)
    return jnp.where((lane >= lo) & (lane < hi), pltpu.roll(block.astype(F32), (LANES - shift) % LANES, 1), under)


def _lane_gather_cols(src, table, out_slabs, out_width, name):
    _, rows, _ = src.shape
    blocks_per_slab = -(-out_width // LANES)

    def body(tab_ref, *refs):
        t, o_ref = pl.program_id(0), refs[RUNS]
        o_ref[0] = _place_run(tab_ref, t, 1, refs[1][0], _place_run(tab_ref, t, 0, refs[0][0], 0.0)).astype(BF16)
        last = (t * RUNS + RUNS - 1) * RUN_FIELDS

        @pl.when(tab_ref[last + 4] > tab_ref[last + 3])
        def _():
            o_ref[0] = _place_run(tab_ref, t, RUNS - 1, refs[RUNS - 1][0], o_ref[0].astype(F32)).astype(BF16)

    def src_spec(e):
        return pl.BlockSpec((1, rows, LANES), lambda t, tab: (tab[(t * RUNS + e) * RUN_FIELDS], 0, tab[(t * RUNS + e) * RUN_FIELDS + 1]))

    grid_spec = pltpu.PrefetchScalarGridSpec(
        num_scalar_prefetch=1, grid=(out_slabs * blocks_per_slab,), in_specs=[src_spec(e) for e in range(RUNS)],
        out_specs=pl.BlockSpec((1, rows, LANES), lambda t, tab: (t // blocks_per_slab, 0, t % blocks_per_slab)))
    return _call(
        body, name=name, grid_spec=grid_spec, out_shape=jax.ShapeDtypeStruct((out_slabs, rows, out_width), BF16),
        compiler_params=_params(("arbitrary",)),
    )(jnp.asarray(table), src, src, src)


def _all_reduce_small(vec):
    rows, cols = vec.shape

    def body(v_ref, o_ref, buf, send_sems, recv_sems):
        x, y, c = _coords()
        me = 4 * x + 2 * y + c
        buf[me] = v_ref[...]
        copies = []
        for k in range(N_DEV - 1):
            fx, fy, fc = ((k + 1) >> 2) & 1, ((k + 1) >> 1) & 1, (k + 1) & 1
            cp = pltpu.make_async_remote_copy(
                src_ref=v_ref, dst_ref=buf.at[me], send_sem=send_sems.at[k], recv_sem=recv_sems.at[k],
                device_id=(_flip(x, fx), _flip(y, fy), _flip(c, fc)), device_id_type=MESH)
            cp.start()
            copies.append(cp)
        for cp in copies:
            cp.wait()
        total = buf[0]
        for j in range(1, N_DEV):
            total = total + buf[j]
        o_ref[...] = total

    vmem = pl.BlockSpec(memory_space=pltpu.VMEM)
    return _call(
        body, name="all_reduce_small", in_specs=[vmem], out_specs=vmem,
        out_shape=jax.ShapeDtypeStruct((rows, cols), F32),
        scratch_shapes=[pltpu.VMEM((N_DEV, rows, cols), F32), pltpu.SemaphoreType.DMA((N_DEV - 1,)),
                        pltpu.SemaphoreType.DMA((N_DEV - 1,))],
    )(vec)


def _w_in_column_maps(ns, o_ba, n_logit, n_main, n_all):
    own = np.arange(N_DEV * ns)
    work_of_own = np.where(own < o_ba, own, np.where(own < o_ba + n_logit, n_main + own - o_ba, own - n_logit))
    to_work = np.full((n_all, 2), -1, np.int64)
    to_work[work_of_own, 0] = own // ns
    to_work[work_of_own, 1] = own % ns
    slab_width = -(-ns // LANES) * LANES
    to_own = np.full((N_DEV, slab_width, 2), -1, np.int64)
    to_own[:, :ns, 0] = 0
    to_own[:, :ns, 1] = work_of_own.reshape(N_DEV, ns)
    return to_work, to_own.reshape(-1, 2)


def kernel(x, meta_tokens, norm_w, w_in, conv_w, A_log, dt_bias, pool_mix, pool_scale, dn_norm_w, w_pool_out, w_dn_out, w_o, final_norm_w, loss_target, m_meta_tokens, m_norm_w, m_w_in, m_conv_w, m_A_log, m_dt_bias, m_pool_mix, m_pool_scale, m_dn_norm_w, m_w_pool_out, m_w_dn_out, m_w_o, m_final_norm_w, v_meta_tokens, v_norm_w, v_w_in, v_conv_w, v_A_log, v_dt_bias, v_pool_mix, v_pool_scale, v_dn_norm_w, v_w_pool_out, v_w_dn_out, v_w_o, v_final_norm_w):
    seq, d = x.shape[1], x.shape[2]
    n_meta = meta_tokens.shape[0]
    n_heads, hd = A_log.shape[-1], dn_norm_w.shape[-1]
    dn = n_heads * hd
    pw, ng = pool_scale.shape[-1], pool_mix.shape[1]
    pg = pw // ng
    kw = conv_w.shape[1]
    pad = (-n_meta) % CHUNK
    x0 = pad + n_meta
    lp = x0 + seq
    ns = w_in.shape[-1]
    in_cols = N_DEV * ns
    o_q, o_k, o_v, o_zd = 2 * pw, 2 * pw + dn, 2 * pw + 2 * dn, 2 * pw + 3 * dn
    o_ba = 2 * pw + 4 * dn
    o_gp, o_gd = o_ba, o_ba + d
    n_main = o_gd + d
    n_all = n_main + 2 * LANES
    assert lp % CHUNK == 0 and in_cols == n_main + 2 * n_heads and 2 * n_heads <= LANES and hd == LANES
    cs, ms = conv_w.shape[-1], meta_tokens.shape[-1]
    mr = pool_mix.shape[2]
    assert ms == pg and cs % pg == 0
    to_work, to_own = _w_in_column_maps(ns, o_ba, 2 * n_heads, n_main, n_all)
    cols_major = lambda t: jnp.transpose(t, (1, 0, 2)).reshape(t.shape[1], N_DEV * t.shape[2])

    mix_g, conv_g, meta_g = _all_gather([pool_mix[0].reshape(ng * mr, pg).astype(BF16), conv_w[0], meta_tokens])
    win_g = _all_gather_tree(w_in[0].astype(BF16), after=meta_g)
    late_shards = [w_pool_out[0].astype(BF16), w_dn_out[0].astype(BF16), w_o[0].astype(BF16)]
    late_weights = _split_start(_to_all_copies, late_shards, [jax.ShapeDtypeStruct((N_DEV,) + s.shape, BF16) for s in late_shards],
                                (N_DEV - 1) * len(late_shards), "gather_out_proj_start", after=win_g)
    norm_w_in = norm_w + late_weights["token"][0, 0]
    w_all = _lane_gather_cols(win_g, _lane_gather_table(to_work, ns), 1, n_all, "w_in_to_work").reshape(d, n_all)
    mix_f = jnp.transpose(mix_g.reshape(N_DEV, ng, mr, pg), (1, 0, 2, 3)).reshape(ng, pg, pg)
    conv_f = cols_major(conv_g)
    meta_f = cols_major(meta_g)

    h0, xn = _norm_in_fwd(x[0], meta_f, norm_w_in, pad)
    proj = _matmul(xn, w_all, NN, F32, lp, 768, 2048, "proj")
    y_pool = _pool_fwd(proj, mix_f, pool_scale, pad)
    conv_q, conv_k, conv_v = (conv_f[:, i * dn:(i + 1) * dn] for i in range(3))
    qn = _conv_fwd(proj, o_q, conv_q, hd, float(hd) ** -0.5, "conv_q_fwd")
    kn = _conv_fwd(proj, o_k, conv_k, hd, 1.0, "conv_k_fwd")
    vv = _conv_fwd(proj, o_v, conv_v, hd, None, "conv_v_fwd")
    logit_lanes = (n_heads, LANES - 2 * n_heads)
    prm = jnp.pad(A_log, ((0, 7), logit_lanes)) + jnp.pad(dt_bias, ((1, 6), logit_lanes))
    y_dn, hist, tmats = _chunk_fwd(qn, kn, vv, proj, o_zd, n_main, prm, dn_norm_w, n_heads, pad)
    me_idx = jnp.reshape(4 * lax.axis_index("x") + 2 * lax.axis_index("y") + lax.axis_index("c"), (1,)).astype(jnp.int32)
    _, landed = _split_wait(_to_all_copies, late_weights, y_dn, "gather_out_proj_wait")
    wpo_g, wdn_g, wo_g = (_fill_own_block(g, s, me_idx, "own_block_%d" % i) for i, (g, s) in enumerate(zip(landed, late_shards)))
    wpo_f = cols_major(wpo_g)
    wdn_f = wdn_g.reshape(dn, d)
    wo_f = wo_g.reshape(d, d)
    p_out = _matmul(y_pool, wpo_f, NN, F32, 1056, 1024, 1024, "pool_out")
    q_out = _matmul(y_dn, wdn_f, NN, F32, 1056, 1024, 2048, "dn_out")
    merged = _merge_fwd(p_out, q_out, proj, o_gp, o_gd)
    mo = _matmul(merged, wo_f, NN, F32, 1056, 1024, 2048, "w_o_fwd")
    dh1, d_fw, loss_part = _final_loss(h0, mo, final_norm_w.reshape(1, d), loss_target[0], x0)

    d_merged = _matmul(dh1, wo_f, NT, F32, 1056, 1024, 1024, "w_o_bwd_x")
    g_wo = _matmul(merged, dh1, TN, BF16, 1024, 1024, 704, "w_o_bwd_w")
    d_p, d_q, d_gp, d_gd = _merge_bwd(p_out, q_out, proj, o_gp, o_gd, d_merged)
    d_ypool = _matmul(d_p, wpo_f, NT, F32, 1056, 1024, 2048, "pool_out_bwd_x")
    g_wpo = _matmul(y_pool.T, d_p, NN, BF16, 1024, 1024, lp, "pool_out_bwd_w", col_blocks=N_DEV)
    d_ydn = _matmul(d_q, wdn_f, NT, F32, 1056, 1024, 2048, "dn_out_bwd_x")
    g_wdn = _matmul(y_dn, d_q, TN, BF16, 1024, 1024, 704, "dn_out_bwd_w")
    rs_early = _rs_begin([g_wpo, g_wdn.reshape(N_DEV, dn // N_DEV, d), g_wo.reshape(N_DEV, d // N_DEV, d)], "early", split=True)
    started = rs_early["token"][0, 0]
    d_u, d_zp, g_mix, g_pscale = _pool_bwd(proj, mix_f, pool_scale + started, d_ypool, pad)
    d_qn, d_kn, d_vv, d_ba, d_zd, d_prm, g_dnw = _chunk_bwd(qn, kn, vv, proj, o_zd, n_main, prm + started, dn_norm_w, hist, tmats, d_ydn, n_heads, pad)
    d_qr, g_cq = _conv_bwd(proj, o_q, conv_q, d_qn, hd, float(hd) ** -0.5, pad, "conv_q_bwd")
    d_kr, g_ck = _conv_bwd(proj, o_k, conv_k, d_kn, hd, 1.0, pad, "conv_k_bwd")
    d_vr, g_cv = _conv_bwd(proj, o_v, conv_v, d_vv, hd, None, pad, "conv_v_bwd")
    d_proj = jnp.concatenate([d_u, d_zp, d_qr, d_kr, d_vr, d_zd, d_gp, d_gd, d_ba.astype(BF16), jnp.zeros((lp, LANES), BF16)], axis=1)
    xn_t, rs_late, token = xn.T, [], None
    for half in range(2):
        rows = slice(half * (d // 2), (half + 1) * (d // 2))
        g_wall = _matmul(xn_t[rows], d_proj, NN, BF16, 1024, 768, lp, "w_in_bwd_w_%d" % half, after=token)
        g_win = _lane_gather_cols(g_wall.reshape(1, d // 2, n_all), _lane_gather_table(to_own, n_all), N_DEV, ns, "w_in_grad_to_own_%d" % half)
        rs_late.append(_rs_begin([g_win], "late%d" % half, split=True))
        token = rs_late[-1]["token"]
    d_xn = _matmul(d_proj, w_all, NT, F32, lp, 512, 768, "w_in_bwd_x", after=token)
    d_head, grad_x, g_nw = _norm_in_bwd(h0, norm_w, d_xn, dh1, x0)
    grad_x = grad_x[None]

    by_cols = lambda t: jnp.transpose(t.reshape(t.shape[0], N_DEV, t.shape[1] // N_DEV), (1, 0, 2))
    g_conv = by_cols(jnp.concatenate([g_cq, g_ck, g_cv], axis=1)).reshape(N_DEV, kw * cs // pg, pg)
    conv_rows = -(-g_conv.shape[1] // 16) * 16
    g_small = jnp.concatenate(
        [jnp.transpose(g_mix.reshape(ng, N_DEV, mr, pg), (1, 0, 2, 3)).reshape(N_DEV, ng * mr, pg), by_cols(d_head[pad:x0]),
         jnp.pad(g_conv, ((0, 0), (0, conv_rows - g_conv.shape[1]), (0, 0)))], axis=1).astype(BF16)
    r_small, = _rs_finish(_rs_begin([g_small], "small", split=False), "small")
    r_mix, r_meta = r_small[:ng * mr], r_small[ng * mr:ng * mr + n_meta]
    r_conv = r_small[ng * mr + n_meta:ng * mr + n_meta + kw * cs // pg]
    r_wpo, r_wdn, r_wo = _rs_finish(rs_early, "early", after=r_small)

    small = [g_nw[0], d_fw[0], g_pscale[0], g_dnw[0], d_prm[0], d_prm[1], loss_part[0]]
    s_sizes = [t.shape[0] for t in small]
    s_cols = -(-sum(s_sizes) // (8 * LANES)) * LANES
    s_vec = jnp.concatenate(small + [jnp.zeros((8 * s_cols - sum(s_sizes),), F32)]).reshape(8, s_cols)
    s_red = _all_reduce_small(s_vec)
    s_sum = s_red.reshape(-1)
    r_win = None
    for half, begun in enumerate(rs_late):
        r_win = _rs_finish(begun, "late%d" % half, after=s_red, part=half, n_parts=2, dsts=r_win)
    r_win, = r_win
    s_offs = [sum(s_sizes[:i]) for i in range(len(s_sizes))]
    s_take = lambda i, n=None, o=0: s_sum[s_offs[i] + o:s_offs[i] + o + (s_sizes[i] if n is None else n)]

    grads = {
        "meta_tokens": r_meta, "norm_w": s_take(0).reshape(norm_w.shape),
        "w_in": r_win.reshape(w_in.shape), "conv_w": r_conv.reshape(conv_w.shape),
        "A_log": s_take(4, n_heads, n_heads).reshape(A_log.shape), "dt_bias": s_take(5, n_heads, n_heads).reshape(dt_bias.shape),
        "pool_mix": r_mix.reshape(pool_mix.shape), "pool_scale": s_take(2).reshape(pool_scale.shape),
        "dn_norm_w": s_take(3).reshape(dn_norm_w.shape), "w_pool_out": r_wpo.reshape(w_pool_out.shape),
        "w_dn_out": r_wdn.reshape(w_dn_out.shape), "w_o": r_wo.reshape(w_o.shape),
        "final_norm_w": s_take(1).reshape(final_norm_w.shape),
    }
    loss = s_take(6, 1)[0]

    weights = dict(meta_tokens=meta_tokens, norm_w=norm_w, w_in=w_in, conv_w=conv_w, A_log=A_log, dt_bias=dt_bias,
                   pool_mix=pool_mix, pool_scale=pool_scale, dn_norm_w=dn_norm_w, w_pool_out=w_pool_out, w_dn_out=w_dn_out,
                   w_o=w_o, final_norm_w=final_norm_w)
    m_in = dict(meta_tokens=m_meta_tokens, norm_w=m_norm_w, w_in=m_w_in, conv_w=m_conv_w, A_log=m_A_log, dt_bias=m_dt_bias,
                pool_mix=m_pool_mix, pool_scale=m_pool_scale, dn_norm_w=m_dn_norm_w, w_pool_out=m_w_pool_out,
                w_dn_out=m_w_dn_out, w_o=m_w_o, final_norm_w=m_final_norm_w)
    v_in = dict(meta_tokens=v_meta_tokens, norm_w=v_norm_w, w_in=v_w_in, conv_w=v_conv_w, A_log=v_A_log, dt_bias=v_dt_bias,
                pool_mix=v_pool_mix, pool_scale=v_pool_scale, dn_norm_w=v_dn_norm_w, w_pool_out=v_w_pool_out,
                w_dn_out=v_w_dn_out, w_o=v_w_o, final_norm_w=v_final_norm_w)
    names = list(weights)
    upd = {n: _adamw(weights[n], grads[n], m_in[n], v_in[n], "adamw_" + n) for n in names}
    return (loss, grad_x, *[grads[n] for n in names], *[upd[n][0] for n in names], *[upd[n][1] for n in names],
            *[upd[n][2] for n in names])
```

```python
import functools

import jax
import jax.numpy as jnp
import numpy as np
from jax import lax
from jax.experimental import pallas as pl
from jax.experimental.pallas import tpu as pltpu

F32 = jnp.float32
BF16 = jnp.bfloat16
HIGHEST = lax.Precision.HIGHEST
MESH = pl.DeviceIdType.MESH

CHUNK = 64
NORM_EPS = 1e-6
POOL_WINDOWS = (2, 4, 8, 16)
ADAM_LR, ADAM_B1, ADAM_B2, ADAM_EPS, ADAM_WD, ADAM_STEP = 0.001, 0.9, 0.999, 1e-08, 0.01, 10
N_DEV = 8
LANES = 128
VMEM_LIMIT = 48 * 1024 * 1024

NN = (((1,), (0,)), ((), ()))
NT = (((1,), (1,)), ((), ()))
TN = (((0,), (0,)), ((), ()))


def _call(body, **kw):
    return pl.pallas_call(body, **kw)


def _params(sem=None):
    return pltpu.CompilerParams(dimension_semantics=sem, vmem_limit_bytes=VMEM_LIMIT)


def _tile(n, pref, align):
    for d in range(min(pref, n), 0, -1):
        if n % d == 0 and d % align == 0:
            return d
    return n


def _dot(a, b, dims=NN, precision=None):
    return lax.dot_general(a, b, dims, precision=precision, preferred_element_type=F32)


def _sigmoid(x):
    return 0.5 * jnp.tanh(0.5 * x) + 0.5


def _silu(x):
    return x * _sigmoid(x)


def _softplus(x):
    return jnp.maximum(x, 0.0) + jnp.log(1.0 + jnp.exp(-jnp.abs(x)))


def _rmsnorm(x, w):
    return x * lax.rsqrt(jnp.mean(x * x, axis=-1, keepdims=True) + NORM_EPS) * w


def _shift_down(x, j, row):
    if j == 0:
        return x
    return jnp.where(row >= j, pltpu.roll(x, j, 0), 0.0)


def _shift_up(x, j, row):
    if j == 0:
        return x
    n = x.shape[0]
    return jnp.where(row < n - j, pltpu.roll(x, n - j, 0), 0.0)


def _matmul(a, b, dims, out_dtype, tm, tn, tk, name, col_blocks=None, after=None):
    ta = dims == TN
    tb = dims == NT
    m, kdim = (a.shape[1], a.shape[0]) if ta else a.shape
    n = b.shape[0] if tb else b.shape[1]
    if col_blocks:
        tn = n // col_blocks
    tm, tn, tk = _tile(m, tm, 8), _tile(n, tn, LANES), _tile(kdim, tk, LANES if not ta else 16)
    nk = kdim // tk

    n_extra = 0 if after is None else 1

    def body(a_ref, b_ref, *refs):
        o_ref, scratch = refs[n_extra], refs[n_extra + 1:]
        part = _dot(a_ref[...].astype(BF16), b_ref[...].astype(BF16), dims)
        if nk == 1:
            o_ref[...] = part.astype(o_ref.dtype).reshape(o_ref.shape)
            return
        acc_ref, = scratch
        k = pl.program_id(2)

        @pl.when(k == 0)
        def _():
            acc_ref[...] = part

        @pl.when(k > 0)
        def _():
            acc_ref[...] += part

        @pl.when(k == nk - 1)
        def _():
            o_ref[...] = acc_ref[...].astype(o_ref.dtype).reshape(o_ref.shape)

    a_spec = pl.BlockSpec((tk, tm), lambda i, j, k: (k, i)) if ta else pl.BlockSpec((tm, tk), lambda i, j, k: (i, k))
    b_spec = pl.BlockSpec((tn, tk), lambda i, j, k: (j, k)) if tb else pl.BlockSpec((tk, tn), lambda i, j, k: (k, j))
    if col_blocks:
        out_spec = pl.BlockSpec((1, tm, tn), lambda i, j, k: (j, i, 0))
        out_shape = jax.ShapeDtypeStruct((col_blocks, m, tn), out_dtype)
    else:
        out_spec = pl.BlockSpec((tm, tn), lambda i, j, k: (i, j))
        out_shape = jax.ShapeDtypeStruct((m, n), out_dtype)
    return _call(
        body, name=name, grid=(m // tm, n // tn, nk),
        in_specs=[a_spec, b_spec] + [ANY] * n_extra, out_specs=out_spec, out_shape=out_shape,
        scratch_shapes=[] if nk == 1 else [pltpu.VMEM((tm, tn), F32)],
        compiler_params=_params(("parallel", "parallel", "arbitrary")),
    )(a, b, *([] if after is None else [after]))


def _norm_in_fwd(x2d, meta, w, pad):
    seq, d = x2d.shape
    tr = pad + meta.shape[0]
    assert seq % tr == 0 and tr % 16 == 0
    lp = tr + seq

    def body(x_ref, m_ref, w_ref, h_ref, o_ref):
        def emit(h):
            h_ref[...] = h
            o_ref[...] = _rmsnorm(h, w_ref[...]).astype(BF16)

        @pl.when(pl.program_id(0) == 0)
        def _():
            emit(jnp.concatenate([jnp.zeros((pad, d), F32), m_ref[...]], axis=0) if pad else m_ref[...])

        @pl.when(pl.program_id(0) > 0)
        def _():
            emit(x_ref[...])

    row = pl.BlockSpec((tr, d), lambda i: (i, 0))
    return _call(
        body, name="norm_in_fwd", grid=(lp // tr,),
        in_specs=[pl.BlockSpec((tr, d), lambda i: (jnp.maximum(i - 1, 0), 0)), pl.BlockSpec(meta.shape, lambda i: (0, 0)),
                  pl.BlockSpec((1, d), lambda i: (0, 0))],
        out_specs=[row, row],
        out_shape=[jax.ShapeDtypeStruct((lp, d), F32), jax.ShapeDtypeStruct((lp, d), BF16)],
        compiler_params=_params(("arbitrary",)),
    )(x2d, meta, w)


def _norm_in_bwd(h0, w, dxn, dh1, x0):
    lp, d = h0.shape
    tr = x0
    assert lp % tr == 0

    def body(h_ref, w_ref, da_ref, dh1_ref, head_ref, gx_ref, dw_ref):
        i = pl.program_id(0)
        _, vjp = jax.vjp(_rmsnorm, h_ref[...], w_ref[...])
        dh, dw = vjp(da_ref[...])
        dh = dh + dh1_ref[...]

        @pl.when(i == 0)
        def _():
            head_ref[...] = dh
            dw_ref[...] = dw

        @pl.when(i > 0)
        def _():
            gx_ref[...] = dh
            dw_ref[...] += dw

    row = pl.BlockSpec((tr, d), lambda i: (i, 0))
    vec = pl.BlockSpec((1, d), lambda i: (0, 0))
    return _call(
        body, name="norm_in_bwd", grid=(lp // tr,),
        in_specs=[row, vec, row, row],
        out_specs=[pl.BlockSpec((tr, d), lambda i: (0, 0)), pl.BlockSpec((tr, d), lambda i: (jnp.maximum(i - 1, 0), 0)), vec],
        out_shape=[jax.ShapeDtypeStruct((tr, d), F32), jax.ShapeDtypeStruct((lp - tr, d), F32), jax.ShapeDtypeStruct((1, d), F32)],
        compiler_params=_params(("arbitrary",)),
    )(h0, w, dxn, dh1)


def _final_loss(h0, mo, fw, tgt, x0):
    lp, d = h0.shape
    tr = x0
    assert lp % tr == 0

    def body(h_ref, mo_ref, fw_ref, t_ref, dh_ref, dw_ref, loss_ref):
        i = pl.program_id(0)
        row = i * tr + lax.broadcasted_iota(jnp.int32, (tr, 1), 0)
        mask = jnp.where(row >= x0, 1.0, 0.0).astype(F32)
        tgt_v = t_ref[...]

        def loss_fn(h1, w):
            err = _rmsnorm(h1, w) - tgt_v
            return 0.5 * jnp.sum(jnp.mean(err * err, axis=-1, keepdims=True) * mask, axis=0, keepdims=True)

        loss, vjp = jax.vjp(loss_fn, h_ref[...] + mo_ref[...], fw_ref[...])
        dh, dw = vjp(jnp.ones((1, 1), F32))
        dh_ref[...] = dh

        @pl.when(i == 0)
        def _():
            dw_ref[...] = jnp.zeros_like(dw_ref)
            loss_ref[...] = jnp.zeros_like(loss_ref)

        dw_ref[...] += dw
        loss_ref[...] += jnp.broadcast_to(loss, loss_ref.shape)

    row_spec = pl.BlockSpec((tr, d), lambda i: (i, 0))
    vec = pl.BlockSpec((1, d), lambda i: (0, 0))
    return _call(
        body, name="final_loss", grid=(lp // tr,),
        in_specs=[row_spec, row_spec, vec, pl.BlockSpec((tr, d), lambda i: (jnp.maximum(i - 1, 0), 0))],
        out_specs=[row_spec, vec, pl.BlockSpec((8, LANES), lambda i: (0, 0))],
        out_shape=[jax.ShapeDtypeStruct((lp, d), F32), jax.ShapeDtypeStruct((1, d), F32), jax.ShapeDtypeStruct((8, LANES), F32)],
        compiler_params=_params(("arbitrary",)),
    )(h0, mo, fw, tgt)


def _pool_select(parts, g):
    out = parts[-1]
    for gi in range(len(parts) - 2, -1, -1):
        out = jnp.where(g == gi, parts[gi], out)
    return out


def _pool_count(row, g, pad):
    win = _pool_select([jnp.full(row.shape, float(w), F32) for w in POOL_WINDOWS], g)
    return jnp.maximum(jnp.minimum((row - pad + 1).astype(F32), win), 1.0)


def _pooled(u, g, row, pad):
    sums, s, span = [], u, 1
    for w in POOL_WINDOWS:
        while span < w:
            s = s + _shift_down(s, span, row)
            span *= 2
        sums.append(s)
    return _pool_select(sums, g) / _pool_count(row, g, pad) - u


def _pooled_adjoint(dp, g, row, pad):
    e = dp / _pool_count(row, g, pad)
    sums, s, span = [], e, 1
    for w in POOL_WINDOWS:
        while span < w:
            s = s + _shift_up(s, span, row)
            span *= 2
        sums.append(s)
    return _pool_select(sums, g) - dp


def _pool_specs(lp, pg, ng, z_off):
    u_spec = pl.BlockSpec((lp, pg), lambda g: (0, g))
    z_spec = pl.BlockSpec((lp, pg), lambda g: (0, z_off + g))
    mix_spec = pl.BlockSpec((1, pg, pg), lambda g: (g, 0, 0))
    vec_spec = pl.BlockSpec((1, pg), lambda g: (0, g))
    return u_spec, z_spec, mix_spec, vec_spec


def _pool_fwd(proj, mix, scale, pad):
    lp = proj.shape[0]
    ng, pg, _ = mix.shape
    pw = ng * pg

    def body(u_ref, z_ref, mix_ref, sc_ref, y_ref):
        g = pl.program_id(0)
        row = lax.broadcasted_iota(jnp.int32, (lp, 1), 0)
        pooled = _pooled(u_ref[...], g, row, pad)
        mixed = _dot(pooled.astype(BF16), mix_ref[0])
        y_ref[...] = (mixed * sc_ref[...] * _silu(z_ref[...])).astype(BF16)

    u_spec, z_spec, mix_spec, vec_spec = _pool_specs(lp, pg, ng, pw // pg)
    return _call(
        body, name="pool_fwd", grid=(ng,), in_specs=[u_spec, z_spec, mix_spec, vec_spec], out_specs=u_spec,
        out_shape=jax.ShapeDtypeStruct((lp, pw), BF16), compiler_params=_params(("parallel",)),
    )(proj, proj, mix, scale)


def _pool_bwd(proj, mix, scale, dy, pad):
    lp = proj.shape[0]
    ng, pg, _ = mix.shape
    pw = ng * pg

    def body(u_ref, z_ref, mix_ref, sc_ref, dy_ref, du_ref, dz_ref, dmix_ref, dsc_ref):
        g = pl.program_id(0)
        row = lax.broadcasted_iota(jnp.int32, (lp, 1), 0)
        real = row >= pad
        z = z_ref[...]
        pooled = _pooled(u_ref[...], g, row, pad).astype(BF16)
        mixed = _dot(pooled, mix_ref[0])
        sig = _sigmoid(z)
        sz = z * sig
        dyv = dy_ref[...]
        dsc_ref[...] = jnp.sum(dyv * mixed * sz, axis=0, keepdims=True)
        d_sz = dyv * mixed * sc_ref[...]
        dz_ref[...] = jnp.where(real, d_sz * (sig + sz * (1.0 - sig)), 0.0).astype(BF16)
        d_mixed = (dyv * sc_ref[...] * sz).astype(BF16)
        dmix_ref[0] = _dot(pooled, d_mixed, TN)
        d_pooled = _dot(d_mixed, mix_ref[0], NT)
        du_ref[...] = jnp.where(real, _pooled_adjoint(d_pooled, g, row, pad), 0.0).astype(BF16)

    u_spec, z_spec, mix_spec, vec_spec = _pool_specs(lp, pg, ng, pw // pg)
    return _call(
        body, name="pool_bwd", grid=(ng,),
        in_specs=[u_spec, z_spec, mix_spec, vec_spec, u_spec], out_specs=[u_spec, u_spec, mix_spec, vec_spec],
        out_shape=[jax.ShapeDtypeStruct((lp, pw), BF16), jax.ShapeDtypeStruct((lp, pw), BF16),
                   jax.ShapeDtypeStruct((ng, pg, pg), F32), jax.ShapeDtypeStruct((1, pw), F32)],
        compiler_params=_params(("parallel",)),
    )(proj, proj, mix, scale, dy)


def _conv_pre(x, w, row):
    kw = w.shape[0]
    y = w[kw - 1:kw, :] * x
    for kk in range(kw - 1):
        y = y + w[kk:kk + 1, :] * _shift_down(x, kw - 1 - kk, row)
    return y


def _conv_post(y, out_scale):
    s = _silu(y)
    if out_scale is None:
        return s
    return s * lax.rsqrt(jnp.sum(s * s, axis=-1, keepdims=True) + NORM_EPS) * out_scale


def _conv_fwd(proj, col_off, w, hd, out_scale, name):
    lp = proj.shape[0]
    kw, width = w.shape
    blk0 = col_off // hd

    def body(x_ref, w_ref, o_ref):
        row = lax.broadcasted_iota(jnp.int32, (lp, 1), 0)
        o_ref[...] = _conv_post(_conv_pre(x_ref[...], w_ref[...], row), out_scale)

    return _call(
        body, name=name, grid=(width // hd,),
        in_specs=[pl.BlockSpec((lp, hd), lambda j: (0, blk0 + j)), pl.BlockSpec((kw, hd), lambda j: (0, j))],
        out_specs=pl.BlockSpec((lp, hd), lambda j: (0, j)),
        out_shape=jax.ShapeDtypeStruct((lp, width), F32), compiler_params=_params(("parallel",)),
    )(proj, w)


def _conv_bwd(proj, col_off, w, d_out, hd, out_scale, pad, name):
    lp = proj.shape[0]
    kw, width = w.shape
    blk0 = col_off // hd

    def body(x_ref, w_ref, do_ref, dx_ref, dw_ref):
        row = lax.broadcasted_iota(jnp.int32, (lp, 1), 0)
        real = row >= pad
        x, wv = x_ref[...], w_ref[...]
        _, vjp = jax.vjp(functools.partial(_conv_post, out_scale=out_scale), _conv_pre(x, wv, row))
        dy = jnp.where(real, vjp(do_ref[...])[0], 0.0)
        dx = wv[kw - 1:kw, :] * dy
        dw_ref[kw - 1:kw, :] = jnp.sum(dy * x, axis=0, keepdims=True)
        for kk in range(kw - 1):
            j = kw - 1 - kk
            dx = dx + wv[kk:kk + 1, :] * _shift_up(dy, j, row)
            dw_ref[kk:kk + 1, :] = jnp.sum(dy * _shift_down(x, j, row), axis=0, keepdims=True)
        dx_ref[...] = jnp.where(real, dx, 0.0).astype(BF16)

    col = pl.BlockSpec((lp, hd), lambda j: (0, j))
    wspec = pl.BlockSpec((kw, hd), lambda j: (0, j))
    return _call(
        body, name=name, grid=(width // hd,),
        in_specs=[pl.BlockSpec((lp, hd), lambda j: (0, blk0 + j)), wspec, col], out_specs=[col, wspec],
        out_shape=[jax.ShapeDtypeStruct((lp, width), BF16), jax.ShapeDtypeStruct((kw, width), F32)],
        compiler_params=_params(("parallel",)),
    )(proj, w, d_out)


HEADS_PER_STEP = 16
HEADS_PER_STEP_BWD = 8


def _each(fn, *lists):
    return [fn(*args) for args in zip(*lists)]


def _dot3_each(a_list, b_list, dims=NN):
    hi = lambda t: t.astype(BF16)
    lo = lambda t, t_hi: (t - t_hi.astype(F32)).astype(BF16)
    dot = lambda x, y: _dot(x, y, dims)
    a_hi, b_hi = _each(hi, a_list), _each(hi, b_list)
    a_lo, b_lo = _each(lo, a_list, a_hi), _each(lo, b_list, b_hi)
    hh, hl, lh = _each(dot, a_hi, b_hi), _each(dot, a_hi, b_lo), _each(dot, a_lo, b_hi)
    return _each(lambda x, y, w: x + (y + w), hh, hl, lh)


@jax.custom_vjp
def _unit_lower_inverse(lmats):
    c = lmats[0].shape[0]
    eye = lax.broadcasted_iota(jnp.int32, (c, c), 0) == lax.broadcasted_iota(jnp.int32, (c, c), 1)
    a = [-m for m in lmats]
    tmat = [jnp.where(eye, 1.0, 0.0).astype(F32) + m for m in a]
    span = 2
    while span < c:
        a = _dot3_each(a, a)
        tmat = _each(lambda t, u: t + u, tmat, _dot3_each(tmat, a))
        span *= 2
    return tuple(tmat)


def _unit_lower_inverse_fwd(lmats):
    tmats = _unit_lower_inverse(lmats)
    return tmats, tmats


def _unit_lower_inverse_bwd(tmats, cts):
    left = _each(lambda t, ct: _dot(t, ct, TN, HIGHEST), tmats, cts)
    return (tuple(_each(lambda m, t: -_dot(m, t, NT, HIGHEST), left, tmats)),)


_unit_lower_inverse.defvjp(_unit_lower_inverse_fwd, _unit_lower_inverse_bwd)


@jax.custom_vjp
def _known_inverse(lmats, tmats):
    return tmats


def _known_inverse_fwd(lmats, tmats):
    return tmats, tmats


def _known_inverse_bwd(tmats, cts):
    return _unit_lower_inverse_bwd(tmats, cts)[0], tuple(jnp.zeros_like(t) for t in tmats)


_known_inverse.defvjp(_known_inverse_fwd, _known_inverse_bwd)


def _chunk_math(states, q, k, v, ba, z, prm, nw, head0, rowmask, n_heads, tmats=None, keep_tmats=False):
    c = q.shape[0]
    heads = list(range(len(states)))
    hd = q.shape[1] // len(states)
    lane = lax.broadcasted_iota(jnp.int32, ba.shape, 1)
    sub = lax.broadcasted_iota(jnp.int32, (ba.shape[1], c), 0)
    ri = lax.broadcasted_iota(jnp.int32, (c, c), 0)
    ci = lax.broadcasted_iota(jnp.int32, (c, c), 1)
    last = lax.broadcasted_iota(jnp.int32, (c, 1), 0) == c - 1
    causal, strict = ri >= ci, ri > ci
    beta_all = _sigmoid(ba) * rowmask
    g_all = -jnp.exp(prm[0:1, :]) * _softplus(ba + prm[1:2, :]) * rowmask
    gcum_all = _dot(jnp.where(causal, 1.0, 0.0).astype(F32), g_all, precision=HIGHEST)
    gcum_t = gcum_all.T
    split = lambda t: [t[:, j * hd:(j + 1) * hd] for j in heads]
    qs, ks, vs, zs = split(q), split(k), split(v), split(z)
    beta = [jnp.sum(jnp.where(lane == head0 + j, beta_all, 0.0), axis=1, keepdims=True) for j in heads]
    gcum = [jnp.sum(jnp.where(lane == n_heads + head0 + j, gcum_all, 0.0), axis=1, keepdims=True) for j in heads]
    grow = [jnp.sum(jnp.where(sub == n_heads + head0 + j, gcum_t, 0.0), axis=0, keepdims=True) for j in heads]
    glast = _each(lambda gc: jnp.sum(jnp.where(last, gc, 0.0), axis=0, keepdims=True), gcum)
    decay = _each(lambda gc, gr: jnp.where(causal, jnp.exp(jnp.where(causal, gc - gr, 0.0)), 0.0), gcum, grow)
    eg = _each(jnp.exp, gcum)
    k_beta = _each(jnp.multiply, ks, beta)
    kk = _each(lambda a, b: _dot(a, b, NT), k_beta, ks)
    lmats = tuple(_each(lambda m, dc: jnp.where(strict, m * dc, 0.0), kk, decay))
    tmat = list(_unit_lower_inverse(lmats) if tmats is None else _known_inverse(lmats, tuple(tmats)))
    u_c = _each(_dot, tmat, _each(jnp.multiply, vs, beta))
    w_c = _each(_dot, tmat, _each(jnp.multiply, k_beta, eg))
    qk = _each(lambda a, b, dc: jnp.where(causal, _dot(a, b, NT) * dc, 0.0), qs, ks, decay)
    v_new = _each(lambda u, w, s: u - _dot(w, s), u_c, w_c, list(states))
    o = _each(lambda a, e, s, m, vn: _dot(a * e, s) + _dot(m, vn), qs, eg, list(states), qk, v_new)
    k_dec = _each(lambda a, gl, gc: a * jnp.exp(gl - gc), ks, glast, gcum)
    new_states = _each(lambda s, gl, kd, vn: s * jnp.exp(gl) + _dot(kd, vn, TN), list(states), glast, k_dec, v_new)
    ys = _each(lambda oj, zj: _rmsnorm(oj, nw) * _silu(zj), o, zs)
    if keep_tmats:
        return jnp.concatenate(ys, axis=1), tuple(new_states), tuple(tmat)
    return jnp.concatenate(ys, axis=1), tuple(new_states)


def _chunk_specs(nc, hd, n_heads, z_off, ba_off, rev):
    cidx = (lambda c: nc - 1 - c) if rev else (lambda c: c)
    hb = min(HEADS_PER_STEP_BWD if rev else HEADS_PER_STEP, n_heads)
    assert n_heads % hb == 0 and z_off % (hb * hd) == 0 and ba_off % LANES == 0
    blk = lambda off: pl.BlockSpec((CHUNK, hb * hd), lambda c, g: (cidx(c), off + g))
    ba_spec = lambda off: pl.BlockSpec((CHUNK, LANES), lambda c, g: (cidx(c), off // LANES))
    prm_spec = pl.BlockSpec((8, LANES), lambda c, g: (0, 0))
    nw_spec = pl.BlockSpec((1, hd), lambda c, g: (0, 0))
    st_spec = pl.BlockSpec((1, hb, hd, hd), lambda c, g: (cidx(c), g, 0, 0))
    return blk, ba_spec, prm_spec, nw_spec, st_spec, blk(z_off // (hb * hd))


def _rowmask(chunk_idx, pad):
    row = chunk_idx * CHUNK + lax.broadcasted_iota(jnp.int32, (CHUNK, 1), 0)
    return jnp.where(row >= pad, 1.0, 0.0).astype(F32)


def _chunk_fwd(qn, kn, vv, proj, z_off, ba_off, prm, nw, n_heads, pad):
    lp, dn = qn.shape
    hd = dn // n_heads
    nc = lp // CHUNK
    hb = min(HEADS_PER_STEP, n_heads)

    def body(q_ref, k_ref, v_ref, ba_ref, z_ref, prm_ref, nw_ref, y_ref, hist_ref, tm_ref, st_ref):
        c, g = pl.program_id(0), pl.program_id(1)

        @pl.when(c == 0)
        def _():
            for j in range(hb):
                st_ref[g * hb + j] = jnp.zeros((hd, hd), F32)

        states = tuple(st_ref[g * hb + j] for j in range(hb))
        for j in range(hb):
            hist_ref[0, j] = states[j]
        y, new_states, tmats = _chunk_math(states, q_ref[...], k_ref[...], v_ref[...], ba_ref[...], z_ref[...], prm_ref[...],
                                           nw_ref[...], g * hb, _rowmask(c, pad), n_heads, keep_tmats=True)
        y_ref[...] = y.astype(BF16)
        for j in range(hb):
            st_ref[g * hb + j] = new_states[j]
            tm_ref[0, j] = tmats[j]

    blk, ba_spec, prm_spec, nw_spec, st_spec, z_spec = _chunk_specs(nc, hd, n_heads, z_off, ba_off, False)
    tm_spec = pl.BlockSpec((1, hb, CHUNK, CHUNK), lambda c, g: (c, g, 0, 0))
    return _call(
        body, name="chunk_fwd", grid=(nc, n_heads // hb),
        in_specs=[blk(0), blk(0), blk(0), ba_spec(ba_off), z_spec, prm_spec, nw_spec], out_specs=[blk(0), st_spec, tm_spec],
        out_shape=[jax.ShapeDtypeStruct((lp, dn), BF16), jax.ShapeDtypeStruct((nc, n_heads, hd, hd), F32),
                   jax.ShapeDtypeStruct((nc, n_heads, CHUNK, CHUNK), F32)],
        scratch_shapes=[pltpu.VMEM((n_heads, hd, hd), F32)],
        compiler_params=_params(("arbitrary", "arbitrary")),
    )(qn, kn, vv, proj, proj, prm, nw)


def _chunk_bwd(qn, kn, vv, proj, z_off, ba_off, prm, nw, hist, tmats, dy, n_heads, pad):
    lp, dn = qn.shape
    hd = dn // n_heads
    nc = lp // CHUNK
    hb = min(HEADS_PER_STEP_BWD, n_heads)

    def body(q_ref, k_ref, v_ref, ba_ref, z_ref, prm_ref, nw_ref, hist_ref, tm_ref, dy_ref,
             dq_ref, dk_ref, dv_ref, dba_ref, dz_ref, dprm_ref, dnw_ref, dst_ref):
        step, g = pl.program_id(0), pl.program_id(1)

        @pl.when(step == 0)
        def _():
            for j in range(hb):
                dst_ref[g * hb + j] = jnp.zeros((hd, hd), F32)

        @pl.when((step == 0) & (g == 0))
        def _():
            dprm_ref[...] = jnp.zeros_like(dprm_ref)
            dnw_ref[...] = jnp.zeros_like(dnw_ref)

        @pl.when(g == 0)
        def _():
            dba_ref[...] = jnp.zeros_like(dba_ref)

        def fn(states, q, k, v, ba, z, prm_v, nw_v, known):
            return _chunk_math(states, q, k, v, ba, z, prm_v, nw_v, g * hb, _rowmask(nc - 1 - step, pad), n_heads, tmats=known)

        states = tuple(hist_ref[0, j] for j in range(hb))
        known = tuple(tm_ref[0, j] for j in range(hb))
        _, vjp = jax.vjp(fn, states, q_ref[...], k_ref[...], v_ref[...], ba_ref[...], z_ref[...], prm_ref[...], nw_ref[...], known)
        dst, dq, dk, dv, dba, dz, dprm, dnw, _ = vjp((dy_ref[...], tuple(dst_ref[g * hb + j] for j in range(hb))))
        for j in range(hb):
            dst_ref[g * hb + j] = dst[j]
        dq_ref[...] = dq
        dk_ref[...] = dk
        dv_ref[...] = dv
        dz_ref[...] = dz.astype(BF16)
        dba_ref[...] += dba
        dprm_ref[...] += dprm
        dnw_ref[...] += dnw

    blk, ba_spec, prm_spec, nw_spec, st_spec, z_spec = _chunk_specs(nc, hd, n_heads, z_off, ba_off, True)
    f32_full = jax.ShapeDtypeStruct((lp, dn), F32)
    tm_spec = pl.BlockSpec((1, hb, CHUNK, CHUNK), lambda c, g: (nc - 1 - c, g, 0, 0))
    return _call(
        body, name="chunk_bwd", grid=(nc, n_heads // hb),
        in_specs=[blk(0), blk(0), blk(0), ba_spec(ba_off), z_spec, prm_spec, nw_spec, st_spec, tm_spec, blk(0)],
        out_specs=[blk(0), blk(0), blk(0), ba_spec(0), blk(0), prm_spec, nw_spec],
        out_shape=[f32_full, f32_full, f32_full, jax.ShapeDtypeStruct((lp, LANES), F32), jax.ShapeDtypeStruct((lp, dn), BF16),
                   jax.ShapeDtypeStruct((8, LANES), F32), jax.ShapeDtypeStruct((1, hd), F32)],
        scratch_shapes=[pltpu.VMEM((n_heads, hd, hd), F32)],
        compiler_params=_params(("arbitrary", "arbitrary")),
    )(qn, kn, vv, proj, proj, prm, nw, hist, tmats, dy)


def _merge_math(p, q, gp, gd):
    return _sigmoid(gp) * p + _sigmoid(gd) * q


def _merge_specs(lp, d, gp_off, gd_off):
    tr, tc = _tile(lp, 264, 16), _tile(d, 1024, LANES)
    blk = pl.BlockSpec((tr, tc), lambda i, j: (i, j))
    gp_spec = pl.BlockSpec((tr, tc), lambda i, j: (i, gp_off // tc + j))
    gd_spec = pl.BlockSpec((tr, tc), lambda i, j: (i, gd_off // tc + j))
    return (lp // tr, d // tc), blk, gp_spec, gd_spec


def _merge_fwd(p, q, proj, gp_off, gd_off):
    lp, d = p.shape
    grid, blk, gp_spec, gd_spec = _merge_specs(lp, d, gp_off, gd_off)

    def body(p_ref, q_ref, gp_ref, gd_ref, o_ref):
        o_ref[...] = _merge_math(p_ref[...], q_ref[...], gp_ref[...], gd_ref[...]).astype(BF16)

    return _call(
        body, name="merge_fwd", grid=grid, in_specs=[blk, blk, gp_spec, gd_spec], out_specs=blk,
        out_shape=jax.ShapeDtypeStruct((lp, d), BF16), compiler_params=_params(("parallel", "parallel")),
    )(p, q, proj, proj)


def _merge_bwd(p, q, proj, gp_off, gd_off, dm):
    lp, d = p.shape
    grid, blk, gp_spec, gd_spec = _merge_specs(lp, d, gp_off, gd_off)

    def body(p_ref, q_ref, gp_ref, gd_ref, dm_ref, dp_ref, dq_ref, dgp_ref, dgd_ref):
        _, vjp = jax.vjp(_merge_math, p_ref[...], q_ref[...], gp_ref[...], gd_ref[...])
        for ref, val in zip((dp_ref, dq_ref, dgp_ref, dgd_ref), vjp(dm_ref[...])):
            ref[...] = val.astype(BF16)

    out = jax.ShapeDtypeStruct((lp, d), BF16)
    return _call(
        body, name="merge_bwd", grid=grid, in_specs=[blk, blk, gp_spec, gd_spec, blk], out_specs=[blk] * 4,
        out_shape=[out] * 4, compiler_params=_params(("parallel", "parallel")),
    )(p, q, proj, proj, dm)


def _adamw(w, g, m, v, name):
    shape = w.shape
    w2, g2, m2, v2 = (t.reshape((-1, shape[-1])) for t in (w, g, m, v))
    rows, cols = w2.shape
    tr = _tile(rows, 128, 8)

    def body(w_ref, g_ref, m_ref, v_ref, d_ref, nm_ref, nv_ref):
        gv = g_ref[...]
        nm = ADAM_B1 * m_ref[...] + (1.0 - ADAM_B1) * gv
        nv = ADAM_B2 * v_ref[...] + (1.0 - ADAM_B2) * (gv * gv)
        m_hat = nm / (1.0 - ADAM_B1 ** ADAM_STEP)
        v_hat = nv / (1.0 - ADAM_B2 ** ADAM_STEP)
        d_ref[...] = -ADAM_LR * (m_hat / (jnp.sqrt(v_hat) + ADAM_EPS) + ADAM_WD * w_ref[...])
        nm_ref[...] = nm
        nv_ref[...] = nv

    blk = pl.BlockSpec((tr, cols), lambda i: (i, 0))
    out = jax.ShapeDtypeStruct((rows, cols), F32)
    res = _call(
        body, name=name, grid=(rows // tr,), in_specs=[blk] * 4, out_specs=[blk] * 3, out_shape=[out] * 3,
        compiler_params=_params(("parallel",)),
    )(w2, g2, m2, v2)
    return tuple(t.reshape(shape) for t in res)


def _coords():
    return lax.axis_index("x"), lax.axis_index("y"), lax.axis_index("c")


def _flip(v, bit):
    return 1 - v if bit else v


CHIP_FLIPS = ((1, 0), (0, 1), (1, 1))
ANY = pl.BlockSpec(memory_space=pl.ANY)


def _all_gather(shards):
    n = len(shards)

    def body(*refs):
        x_refs, out_refs = refs[:n], refs[n:2 * n]
        send_sems, recv_sems, local_sems = refs[2 * n:]
        x, y, c = _coords()
        sibling = (x, y, 1 - c)
        chips = [(_flip(x, fx), _flip(y, fy)) for fx, fy in CHIP_FLIPS]

        def copy(a, k, block, to, from_input=False):
            px, py, pc = block
            slot = out_refs[a].at[4 * px + 2 * py + pc]
            return pltpu.make_async_remote_copy(
                src_ref=x_refs[a] if from_input else slot, dst_ref=slot,
                send_sem=send_sems.at[7 * a + k], recv_sem=recv_sems.at[7 * a + k], device_id=to, device_id_type=MESH)

        mine = [pltpu.make_async_copy(x_refs[a], out_refs[a].at[4 * x + 2 * y + c], local_sems.at[a]) for a in range(n)]
        first = []
        for a in range(n):
            mine[a].start()
            first.append(copy(a, 0, (x, y, c), sibling, True))
            first += [copy(a, 1 + j, (x, y, c), (*chip, c), True) for j, chip in enumerate(chips)]
        for cp in first:
            cp.start()
        passed = []
        for j, chip in enumerate(chips):
            for a in range(n):
                copy(a, 1 + j, (*chip, c), (x, y, c)).wait_recv()
                passed.append(copy(a, 4 + j, (*chip, c), sibling))
                passed[-1].start()
        for a in range(n):
            copy(a, 0, (x, y, 1 - c), (x, y, c)).wait_recv()
            for j, chip in enumerate(chips):
                copy(a, 4 + j, (*chip, 1 - c), (x, y, c)).wait_recv()
        for cp in first + passed:
            cp.wait_send()
        for cp in mine:
            cp.wait()

    return _call(
        body, name="all_gather", in_specs=[ANY] * n, out_specs=[ANY] * n,
        out_shape=[jax.ShapeDtypeStruct((N_DEV,) + s.shape, s.dtype) for s in shards],
        scratch_shapes=[pltpu.SemaphoreType.DMA((7 * n,)), pltpu.SemaphoreType.DMA((7 * n,)), pltpu.SemaphoreType.DMA((n,))],
    )(*shards)


def _all_gather_tree(shard, after):
    rows, cols = shard.shape
    half = rows // 2
    assert rows % 32 == 0

    def body(x_ref, after_ref, out_ref, send_sems, recv_sems, local_sem):
        x, y, c = _coords()
        me, sibling = (x, y, c), (x, y, 1 - c)
        x_nbr, y_nbr, diag = (1 - x, y), (x, 1 - y), (1 - x, 1 - y)

        def part(ref, h):
            return ref if h is None else ref.at[pl.ds(h * half, half)]

        def copy(k, block, to, h=None, from_input=False):
            px, py, pc = block
            slot = part(out_ref.at[4 * px + 2 * py + pc], h)
            return pltpu.make_async_remote_copy(
                src_ref=part(x_ref, h) if from_input else slot, dst_ref=slot,
                send_sem=send_sems.at[k], recv_sem=recv_sems.at[k], device_id=to, device_id_type=MESH)

        mine = pltpu.make_async_copy(x_ref, out_ref.at[4 * x + 2 * y + c], local_sem)
        mine.start()
        started = [copy(0, me, sibling, None, True),
                   copy(1, me, (*x_nbr, c), 0, True), copy(2, me, (*x_nbr, c), 1, True),
                   copy(4, me, (*y_nbr, c), 1, True), copy(3, me, (*y_nbr, c), 0, True)]
        for cp in started:
            cp.start()
        copy(1, (*x_nbr, c), me, 0).wait_recv()
        started.append(copy(5, (*x_nbr, c), (*y_nbr, c), 0))
        started[-1].start()
        copy(4, (*y_nbr, c), me, 1).wait_recv()
        started.append(copy(6, (*y_nbr, c), (*x_nbr, c), 1))
        started[-1].start()
        copy(2, (*x_nbr, c), me, 1).wait_recv()
        started.append(copy(7, (*x_nbr, c), sibling))
        started[-1].start()
        copy(3, (*y_nbr, c), me, 0).wait_recv()
        started.append(copy(8, (*y_nbr, c), sibling))
        started[-1].start()
        copy(5, (*diag, c), me, 0).wait_recv()
        copy(6, (*diag, c), me, 1).wait_recv()
        started.append(copy(9, (*diag, c), sibling))
        started[-1].start()
        copy(0, sibling, me).wait_recv()
        for k, chip in ((7, x_nbr), (8, y_nbr), (9, diag)):
            copy(k, (*chip, 1 - c), me).wait_recv()
        for cp in started:
            cp.wait_send()
        mine.wait()

    return _call(
        body, name="all_gather_tree", in_specs=[ANY, ANY], out_specs=ANY,
        out_shape=jax.ShapeDtypeStruct((N_DEV, rows, cols), shard.dtype),
        scratch_shapes=[pltpu.SemaphoreType.DMA((10,)), pltpu.SemaphoreType.DMA((10,)), pltpu.SemaphoreType.DMA],
    )(shard, after)


def _rs_to_sibling(gs, name):
    n = len(gs)

    def body(*refs):
        g_refs, got_refs = refs[:n], refs[n:2 * n]
        send_sems, recv_sems = refs[2 * n:]
        x, y, c = _coords()
        copies = []
        for a in range(n):
            for p in range(4):
                cp = pltpu.make_async_remote_copy(
                    src_ref=g_refs[a].at[2 * p + (1 - c)], dst_ref=got_refs[a].at[p], send_sem=send_sems.at[4 * a + p],
                    recv_sem=recv_sems.at[4 * a + p], device_id=(x, y, 1 - c), device_id_type=MESH)
                cp.start()
                copies.append(cp)
        for cp in copies:
            cp.wait()

    return _call(
        body, name=name, in_specs=[ANY] * n, out_specs=[ANY] * n,
        out_shape=[jax.ShapeDtypeStruct((4,) + g.shape[1:], g.dtype) for g in gs],
        scratch_shapes=[pltpu.SemaphoreType.DMA((4 * n,)), pltpu.SemaphoreType.DMA((4 * n,))],
    )(*gs)


def _rs_pair_sum(g, got, c_idx, name):
    _, rows, cols = g.shape
    tr = _tile(rows, 256, 16)

    def body(c_ref, g_ref, got_ref, o_ref):
        o_ref[...] = (g_ref[...].astype(F32) + got_ref[...].astype(F32)).astype(o_ref.dtype)

    grid_spec = pltpu.PrefetchScalarGridSpec(
        num_scalar_prefetch=1, grid=(4, rows // tr),
        in_specs=[pl.BlockSpec((1, tr, cols), lambda p, i, c_ref: (2 * p + c_ref[0], i, 0)),
                  pl.BlockSpec((1, tr, cols), lambda p, i, c_ref: (p, i, 0))],
        out_specs=pl.BlockSpec((1, tr, cols), lambda p, i, c_ref: (p, i, 0)))
    return _call(
        body, name=name, grid_spec=grid_spec, out_shape=jax.ShapeDtypeStruct((4, rows, cols), g.dtype),
        compiler_params=_params(("parallel", "parallel")),
    )(c_idx, g, got)


def _to_chips_copies(p_refs, got_refs, send_sems, recv_sems):
    x, y, c = _coords()
    copies = []
    for a in range(len(p_refs)):
        for k, (fx, fy) in enumerate(CHIP_FLIPS):
            px, py = _flip(x, fx), _flip(y, fy)
            copies.append(pltpu.make_async_remote_copy(
                src_ref=p_refs[a].at[2 * px + py], dst_ref=got_refs[a].at[k], send_sem=send_sems.at[3 * a + k],
                recv_sem=recv_sems.at[3 * a + k], device_id=(px, py, c), device_id_type=MESH))
    return copies


def _rs_to_chips(partials, name):
    n = len(partials)

    def body(*refs):
        copies = _to_chips_copies(refs[:n], refs[n:2 * n], *refs[2 * n:])
        for cp in copies:
            cp.start()
        for cp in copies:
            cp.wait()

    return _call(
        body, name=name, in_specs=[ANY] * n, out_specs=[ANY] * n,
        out_shape=[jax.ShapeDtypeStruct((3,) + p.shape[1:], p.dtype) for p in partials],
        scratch_shapes=[pltpu.SemaphoreType.DMA((3 * n,)), pltpu.SemaphoreType.DMA((3 * n,))],
    )(*partials)


HBM = pl.BlockSpec(memory_space=pltpu.HBM)
SEM = pl.BlockSpec(memory_space=pltpu.SEMAPHORE)
SIDE_EFFECT = pltpu.CompilerParams(has_side_effects=pltpu.SideEffectType.DATAFLOW_SIDE_EFFECTING)


def _split_start(copies_fn, srcs, land_shapes, n_sems, name, after=None):
    n, m = len(srcs), len(land_shapes)
    extra = [] if after is None else [after]

    def body(*refs):
        outs = refs[n + m + len(extra):]
        send_sems, recv_sems, token = outs[0], outs[1], outs[-1]
        for cp in copies_fn(refs[:n], refs[n:n + m], send_sems, recv_sems):
            cp.start()
        token[...] = jnp.zeros_like(token)

    ins = [pltpu.with_memory_space_constraint(t, pltpu.HBM) for t in list(srcs) + [lax.empty(s.shape, s.dtype) for s in land_shapes]]
    res = _call(
        body, name=name, in_specs=[HBM] * (n + m) + [ANY] * len(extra),
        out_specs=[SEM, SEM] + [HBM] * (n + m) + [pl.BlockSpec(memory_space=pltpu.VMEM)],
        out_shape=[pltpu.SemaphoreType.DMA((n_sems,)), pltpu.SemaphoreType.DMA((n_sems,))]
        + [pltpu.HBM(t.shape, t.dtype) for t in ins] + [jax.ShapeDtypeStruct((8, LANES), F32)],
        input_output_aliases={i: 2 + i for i in range(n + m)}, compiler_params=SIDE_EFFECT,
    )(*ins, *extra)
    return dict(sems=(res[0], res[1]), srcs=res[2:2 + n], lands=res[2 + n:2 + n + m], token=res[-1])


def _split_wait(copies_fn, started, after, name):
    n, m = len(started["srcs"]), len(started["lands"])

    def body(*refs):
        for cp in copies_fn(refs[:n], refs[n:n + m], refs[n + m], refs[n + m + 1]):
            cp.wait_send()
            cp.wait_recv()

    bufs = list(started["srcs"]) + list(started["lands"])
    res = _call(
        body, name=name, in_specs=[HBM] * (n + m) + [SEM, SEM, ANY], out_specs=[HBM] * (n + m),
        out_shape=[pltpu.HBM(t.shape, t.dtype) for t in bufs],
        input_output_aliases={i: i for i in range(n + m)}, compiler_params=SIDE_EFFECT,
    )(*bufs, *started["sems"], after)
    return res[:n], res[n:]


def _to_all_copies(x_refs, out_refs, send_sems, recv_sems):
    x, y, c = _coords()
    copies = []
    for a in range(len(x_refs)):
        for k in range(N_DEV - 1):
            fx, fy, fc = ((k + 1) >> 2) & 1, ((k + 1) >> 1) & 1, (k + 1) & 1
            copies.append(pltpu.make_async_remote_copy(
                src_ref=x_refs[a], dst_ref=out_refs[a].at[4 * x + 2 * y + c], send_sem=send_sems.at[7 * a + k],
                recv_sem=recv_sems.at[7 * a + k], device_id=(_flip(x, fx), _flip(y, fy), _flip(c, fc)), device_id_type=MESH))
    return copies


def _fill_own_block(gathered, shard, me_idx, name):
    rows, cols = shard.shape
    tr = _tile(rows, 512, 16)

    def body(me_ref, g_ref, s_ref, o_ref):
        o_ref[0] = s_ref[...]

    grid_spec = pltpu.PrefetchScalarGridSpec(
        num_scalar_prefetch=1, grid=(rows // tr,),
        in_specs=[ANY, pl.BlockSpec((tr, cols), lambda i, me: (i, 0))],
        out_specs=pl.BlockSpec((1, tr, cols), lambda i, me: (me[0], i, 0)))
    return _call(
        body, name=name, grid_spec=grid_spec, out_shape=jax.ShapeDtypeStruct(gathered.shape, gathered.dtype),
        input_output_aliases={1: 0}, compiler_params=_params(("arbitrary",)),
    )(me_idx, gathered, shard)


def _rs_chip_sum(partial, got, chip_idx, name, part=0, n_parts=1, dst=None):
    _, rows, cols = partial.shape
    tr = _tile(rows, 256, 16)
    steps = rows // tr
    n_dst = 0 if dst is None else 1

    def body(p_idx_ref, p_ref, got_ref, *refs):
        refs[n_dst][...] = ((p_ref[0].astype(F32) + got_ref[0].astype(F32)) + got_ref[1].astype(F32)) + got_ref[2].astype(F32)

    grid_spec = pltpu.PrefetchScalarGridSpec(
        num_scalar_prefetch=1, grid=(steps,),
        in_specs=[pl.BlockSpec((1, tr, cols), lambda i, p_ref: (p_ref[0], i, 0)),
                  pl.BlockSpec((3, tr, cols), lambda i, p_ref: (0, i, 0))] + [ANY] * n_dst,
        out_specs=pl.BlockSpec((tr, cols), lambda i, p_ref: (part * steps + i, 0)))
    return _call(
        body, name=name, grid_spec=grid_spec, out_shape=jax.ShapeDtypeStruct((n_parts * rows, cols), F32),
        input_output_aliases={3: 0} if n_dst else {}, compiler_params=_params(("parallel",)),
    )(chip_idx, partial, got, *([] if dst is None else [dst]))


def _rs_begin(gs, tag, split):
    c_idx = jnp.reshape(lax.axis_index("c"), (1,)).astype(jnp.int32)
    gots = _rs_to_sibling(gs, "rs_to_sibling_" + tag)
    partials = [_rs_pair_sum(g, got, c_idx, "rs_pair_sum_%s%d" % (tag, a)) for a, (g, got) in enumerate(zip(gs, gots))]
    if not split:
        return dict(partials=partials, gots=_rs_to_chips(partials, "rs_to_chips_" + tag))
    lands = [jax.ShapeDtypeStruct((3,) + p.shape[1:], p.dtype) for p in partials]
    return _split_start(_to_chips_copies, partials, lands, 3 * len(partials), "rs_to_chips_start_" + tag)


def _rs_finish(begun, tag, after=None, part=0, n_parts=1, dsts=None):
    x, y, _ = _coords()
    chip_idx = jnp.reshape(2 * x + y, (1,)).astype(jnp.int32)
    if "gots" in begun:
        partials, gots = begun["partials"], begun["gots"]
    else:
        partials, gots = _split_wait(_to_chips_copies, begun, after, "rs_to_chips_wait_" + tag)
    return [_rs_chip_sum(p, got, chip_idx, "rs_chip_sum_%s%d" % (tag, a), part, n_parts, None if dsts is None else dsts[a])
            for a, (p, got) in enumerate(zip(partials, gots))]


RUNS = 3
RUN_FIELDS = 6


def _lane_gather_table(src_of, src_width):
    n_blocks = src_of.shape[0] // LANES
    tab = np.zeros((n_blocks + 1, RUNS, RUN_FIELDS), np.int32)
    tab[:, :, 5] = LANES
    for t in range(n_blocks):
        runs = []
        for lane in range(LANES):
            slab, col = (int(v) for v in src_of[t * LANES + lane])
            if slab < 0:
                continue
            key = (slab, col // LANES, col % LANES - lane)
            if runs and runs[-1][0] == key and runs[-1][2] == lane:
                runs[-1][2] = lane + 1
            else:
                runs.append([key, lane, lane + 1])
        assert len(runs) <= RUNS
        slots = [None] * RUNS
        for key, lo, hi in sorted(runs, key=lambda r: r[0][:2]):
            e = key[1] % 2 if slots[key[1] % 2] is None else slots.index(None)
            slots[e] = (key[0], key[1], key[2], lo, hi, min(LANES, src_width - key[1] * LANES))
        for e in range(RUNS):
            tab[t, e] = slots[e] if slots[e] is not None else (tab[t - 1, e, 0], tab[t - 1, e, 1], 0, 0, 0, LANES) if t else tab[t, e]
    tab[n_blocks, :, :2] = tab[n_blocks - 1, :, :2]
    return tab.reshape(-1)


def _place_run(tab_ref, t, e, block, under):
    base = (t * RUNS + e) * RUN_FIELDS
    shift, lo, hi = tab_ref[base + 2], tab_ref[base + 3], tab_ref[base + 4]
    lane = lax.broadcasted_iota(jnp.int32, (1, LANES), 1)
    return jnp.where((lane >= lo) & (lane < hi), pltpu.roll(block.astype(F32), (LANES - shift) % LANES, 1), under)


def _lane_gather_cols(src, table, out_slabs, out_width, name):
    _, rows, _ = src.shape
    blocks_per_slab = -(-out_width // LANES)

    def body(tab_ref, *refs):
        t, o_ref = pl.program_id(0), refs[RUNS]
        o_ref[0] = _place_run(tab_ref, t, 1, refs[1][0], _place_run(tab_ref, t, 0, refs[0][0], 0.0)).astype(BF16)
        last = (t * RUNS + RUNS - 1) * RUN_FIELDS

        @pl.when(tab_ref[last + 4] > tab_ref[last + 3])
        def _():
            o_ref[0] = _place_run(tab_ref, t, RUNS - 1, refs[RUNS - 1][0], o_ref[0].astype(F32)).astype(BF16)

    def src_spec(e):
        return pl.BlockSpec((1, rows, LANES), lambda t, tab: (tab[(t * RUNS + e) * RUN_FIELDS], 0, tab[(t * RUNS + e) * RUN_FIELDS + 1]))

    grid_spec = pltpu.PrefetchScalarGridSpec(
        num_scalar_prefetch=1, grid=(out_slabs * blocks_per_slab,), in_specs=[src_spec(e) for e in range(RUNS)],
        out_specs=pl.BlockSpec((1, rows, LANES), lambda t, tab: (t // blocks_per_slab, 0, t % blocks_per_slab)))
    return _call(
        body, name=name, grid_spec=grid_spec, out_shape=jax.ShapeDtypeStruct((out_slabs, rows, out_width), BF16),
        compiler_params=_params(("arbitrary",)),
    )(jnp.asarray(table), src, src, src)


def _all_reduce_small(vec):
    rows, cols = vec.shape

    def body(v_ref, o_ref, buf, send_sems, recv_sems):
        x, y, c = _coords()
        me = 4 * x + 2 * y + c
        buf[me] = v_ref[...]
        copies = []
        for k in range(N_DEV - 1):
            fx, fy, fc = ((k + 1) >> 2) & 1, ((k + 1) >> 1) & 1, (k + 1) & 1
            cp = pltpu.make_async_remote_copy(
                src_ref=v_ref, dst_ref=buf.at[me], send_sem=send_sems.at[k], recv_sem=recv_sems.at[k],
                device_id=(_flip(x, fx), _flip(y, fy), _flip(c, fc)), device_id_type=MESH)
            cp.start()
            copies.append(cp)
        for cp in copies:
            cp.wait()
        total = buf[0]
        for j in range(1, N_DEV):
            total = total + buf[j]
        o_ref[...] = total

    vmem = pl.BlockSpec(memory_space=pltpu.VMEM)
    return _call(
        body, name="all_reduce_small", in_specs=[vmem], out_specs=vmem,
        out_shape=jax.ShapeDtypeStruct((rows, cols), F32),
        scratch_shapes=[pltpu.VMEM((N_DEV, rows, cols), F32), pltpu.SemaphoreType.DMA((N_DEV - 1,)),
                        pltpu.SemaphoreType.DMA((N_DEV - 1,))],
    )(vec)


def _w_in_column_maps(ns, o_ba, n_logit, n_main, n_all):
    own = np.arange(N_DEV * ns)
    work_of_own = np.where(own < o_ba, own, np.where(own < o_ba + n_logit, n_main + own - o_ba, own - n_logit))
    to_work = np.full((n_all, 2), -1, np.int64)
    to_work[work_of_own, 0] = own // ns
    to_work[work_of_own, 1] = own % ns
    slab_width = -(-ns // LANES) * LANES
    to_own = np.full((N_DEV, slab_width, 2), -1, np.int64)
    to_own[:, :ns, 0] = 0
    to_own[:, :ns, 1] = work_of_own.reshape(N_DEV, ns)
    return to_work, to_own.reshape(-1, 2)


def kernel(x, meta_tokens, norm_w, w_in, conv_w, A_log, dt_bias, pool_mix, pool_scale, dn_norm_w, w_pool_out, w_dn_out, w_o, final_norm_w, loss_target, m_meta_tokens, m_norm_w, m_w_in, m_conv_w, m_A_log, m_dt_bias, m_pool_mix, m_pool_scale, m_dn_norm_w, m_w_pool_out, m_w_dn_out, m_w_o, m_final_norm_w, v_meta_tokens, v_norm_w, v_w_in, v_conv_w, v_A_log, v_dt_bias, v_pool_mix, v_pool_scale, v_dn_norm_w, v_w_pool_out, v_w_dn_out, v_w_o, v_final_norm_w):
    seq, d = x.shape[1], x.shape[2]
    n_meta = meta_tokens.shape[0]
    n_heads, hd = A_log.shape[-1], dn_norm_w.shape[-1]
    dn = n_heads * hd
    pw, ng = pool_scale.shape[-1], pool_mix.shape[1]
    pg = pw // ng
    kw = conv_w.shape[1]
    pad = (-n_meta) % CHUNK
    x0 = pad + n_meta
    lp = x0 + seq
    ns = w_in.shape[-1]
    in_cols = N_DEV * ns
    o_q, o_k, o_v, o_zd = 2 * pw, 2 * pw + dn, 2 * pw + 2 * dn, 2 * pw + 3 * dn
    o_ba = 2 * pw + 4 * dn
    o_gp, o_gd = o_ba, o_ba + d
    n_main = o_gd + d
    n_all = n_main + 2 * LANES
    assert lp % CHUNK == 0 and in_cols == n_main + 2 * n_heads and 2 * n_heads <= LANES and hd == LANES
    cs, ms = conv_w.shape[-1], meta_tokens.shape[-1]
    mr = pool_mix.shape[2]
    assert ms == pg and cs % pg == 0
    to_work, to_own = _w_in_column_maps(ns, o_ba, 2 * n_heads, n_main, n_all)
    cols_major = lambda t: jnp.transpose(t, (1, 0, 2)).reshape(t.shape[1], N_DEV * t.shape[2])

    mix_g, conv_g, meta_g = _all_gather([pool_mix[0].reshape(ng * mr, pg).astype(BF16), conv_w[0], meta_tokens])
    win_g = _all_gather_tree(w_in[0].astype(BF16), after=meta_g)
    late_shards = [w_pool_out[0].astype(BF16), w_dn_out[0].astype(BF16), w_o[0].astype(BF16)]
    late_weights = _split_start(_to_all_copies, late_shards, [jax.ShapeDtypeStruct((N_DEV,) + s.shape, BF16) for s in late_shards],
                                (N_DEV - 1) * len(late_shards), "gather_out_proj_start", after=win_g)
    norm_w_in = norm_w + late_weights["token"][0, 0]
    w_all = _lane_gather_cols(win_g, _lane_gather_table(to_work, ns), 1, n_all, "w_in_to_work").reshape(d, n_all)
    mix_f = jnp.transpose(mix_g.reshape(N_DEV, ng, mr, pg), (1, 0, 2, 3)).reshape(ng, pg, pg)
    conv_f = cols_major(conv_g)
    meta_f = cols_major(meta_g)

    h0, xn = _norm_in_fwd(x[0], meta_f, norm_w_in, pad)
    proj = _matmul(xn, w_all, NN, F32, lp, 768, 2048, "proj")
    y_pool = _pool_fwd(proj, mix_f, pool_scale, pad)
    conv_q, conv_k, conv_v = (conv_f[:, i * dn:(i + 1) * dn] for i in range(3))
    qn = _conv_fwd(proj, o_q, conv_q, hd, float(hd) ** -0.5, "conv_q_fwd")
    kn = _conv_fwd(proj, o_k, conv_k, hd, 1.0, "conv_k_fwd")
    vv = _conv_fwd(proj, o_v, conv_v, hd, None, "conv_v_fwd")
    logit_lanes = (n_heads, LANES - 2 * n_heads)
    prm = jnp.pad(A_log, ((0, 7), logit_lanes)) + jnp.pad(dt_bias, ((1, 6), logit_lanes))
    y_dn, hist, tmats = _chunk_fwd(qn, kn, vv, proj, o_zd, n_main, prm, dn_norm_w, n_heads, pad)
    me_idx = jnp.reshape(4 * lax.axis_index("x") + 2 * lax.axis_index("y") + lax.axis_index("c"), (1,)).astype(jnp.int32)
    _, landed = _split_wait(_to_all_copies, late_weights, y_dn, "gather_out_proj_wait")
    wpo_g, wdn_g, wo_g = (_fill_own_block(g, s, me_idx, "own_block_%d" % i) for i, (g, s) in enumerate(zip(landed, late_shards)))
    wpo_f = cols_major(wpo_g)
    wdn_f = wdn_g.reshape(dn, d)
    wo_f = wo_g.reshape(d, d)
    p_out = _matmul(y_pool, wpo_f, NN, F32, 1056, 1024, 1024, "pool_out")
    q_out = _matmul(y_dn, wdn_f, NN, F32, 1056, 1024, 2048, "dn_out")
    merged = _merge_fwd(p_out, q_out, proj, o_gp, o_gd)
    mo = _matmul(merged, wo_f, NN, F32, 1056, 1024, 2048, "w_o_fwd")
    dh1, d_fw, loss_part = _final_loss(h0, mo, final_norm_w.reshape(1, d), loss_target[0], x0)

    d_merged = _matmul(dh1, wo_f, NT, F32, 1056, 1024, 1024, "w_o_bwd_x")
    g_wo = _matmul(merged, dh1, TN, BF16, 1024, 1024, 704, "w_o_bwd_w")
    d_p, d_q, d_gp, d_gd = _merge_bwd(p_out, q_out, proj, o_gp, o_gd, d_merged)
    d_ypool = _matmul(d_p, wpo_f, NT, F32, 1056, 1024, 2048, "pool_out_bwd_x")
    g_wpo = _matmul(y_pool.T, d_p, NN, BF16, 1024, 1024, lp, "pool_out_bwd_w", col_blocks=N_DEV)
    d_ydn = _matmul(d_q, wdn_f, NT, F32, 1056, 1024, 2048, "dn_out_bwd_x")
    g_wdn = _matmul(y_dn, d_q, TN, BF16, 1024, 1024, 704, "dn_out_bwd_w")
    rs_early = _rs_begin([g_wpo, g_wdn.reshape(N_DEV, dn // N_DEV, d), g_wo.reshape(N_DEV, d // N_DEV, d)], "early", split=True)
    started = rs_early["token"][0, 0]
    d_u, d_zp, g_mix, g_pscale = _pool_bwd(proj, mix_f, pool_scale + started, d_ypool, pad)
    d_qn, d_kn, d_vv, d_ba, d_zd, d_prm, g_dnw = _chunk_bwd(qn, kn, vv, proj, o_zd, n_main, prm + started, dn_norm_w, hist, tmats, d_ydn, n_heads, pad)
    d_qr, g_cq = _conv_bwd(proj, o_q, conv_q, d_qn, hd, float(hd) ** -0.5, pad, "conv_q_bwd")
    d_kr, g_ck = _conv_bwd(proj, o_k, conv_k, d_kn, hd, 1.0, pad, "conv_k_bwd")
    d_vr, g_cv = _conv_bwd(proj, o_v, conv_v, d_vv, hd, None, pad, "conv_v_bwd")
    d_proj = jnp.concatenate([d_u, d_zp, d_qr, d_kr, d_vr, d_zd, d_gp, d_gd, d_ba.astype(BF16), jnp.zeros((lp, LANES), BF16)], axis=1)
    xn_t, rs_late, token = xn.T, [], None
    for half in range(2):
        rows = slice(half * (d // 2), (half + 1) * (d // 2))
        g_wall = _matmul(xn_t[rows], d_proj, NN, BF16, 1024, 768, lp, "w_in_bwd_w_%d" % half, after=token)
        g_win = _lane_gather_cols(g_wall.reshape(1, d // 2, n_all), _lane_gather_table(to_own, n_all), N_DEV, ns, "w_in_grad_to_own_%d" % half)
        rs_late.append(_rs_begin([g_win], "late%d" % half, split=True))
        token = rs_late[-1]["token"]
    d_xn = _matmul(d_proj, w_all, NT, F32, lp, 512, 2432, "w_in_bwd_x", after=token)
    d_head, grad_x, g_nw = _norm_in_bwd(h0, norm_w, d_xn, dh1, x0)
    grad_x = grad_x[None]

    by_cols = lambda t: jnp.transpose(t.reshape(t.shape[0], N_DEV, t.shape[1] // N_DEV), (1, 0, 2))
    g_conv = by_cols(jnp.concatenate([g_cq, g_ck, g_cv], axis=1)).reshape(N_DEV, kw * cs // pg, pg)
    conv_rows = -(-g_conv.shape[1] // 16) * 16
    g_small = jnp.concatenate(
        [jnp.transpose(g_mix.reshape(ng, N_DEV, mr, pg), (1, 0, 2, 3)).reshape(N_DEV, ng * mr, pg), by_cols(d_head[pad:x0]),
         jnp.pad(g_conv, ((0, 0), (0, conv_rows - g_conv.shape[1]), (0, 0)))], axis=1).astype(BF16)
    r_small, = _rs_finish(_rs_begin([g_small], "small", split=False), "small")
    r_mix, r_meta = r_small[:ng * mr], r_small[ng * mr:ng * mr + n_meta]
    r_conv = r_small[ng * mr + n_meta:ng * mr + n_meta + kw * cs // pg]
    r_wpo, r_wdn, r_wo = _rs_finish(rs_early, "early", after=r_small)

    small = [g_nw[0], d_fw[0], g_pscale[0], g_dnw[0], d_prm[0], d_prm[1], loss_part[0]]
    s_sizes = [t.shape[0] for t in small]
    s_cols = -(-sum(s_sizes) // (8 * LANES)) * LANES
    s_vec = jnp.concatenate(small + [jnp.zeros((8 * s_cols - sum(s_sizes),), F32)]).reshape(8, s_cols)
    s_red = _all_reduce_small(s_vec)
    s_sum = s_red.reshape(-1)
    r_win = None
    for half, begun in enumerate(rs_late):
        r_win = _rs_finish(begun, "late%d" % half, after=s_red, part=half, n_parts=2, dsts=r_win)
    r_win, = r_win
    s_offs = [sum(s_sizes[:i]) for i in range(len(s_sizes))]
    s_take = lambda i, n=None, o=0: s_sum[s_offs[i] + o:s_offs[i] + o + (s_sizes[i] if n is None else n)]

    grads = {
        "meta_tokens": r_meta, "norm_w": s_take(0).reshape(norm_w.shape),
        "w_in": r_win.reshape(w_in.shape), "conv_w": r_conv.reshape(conv_w.shape),
        "A_log": s_take(4, n_heads, n_heads).reshape(A_log.shape), "dt_bias": s_take(5, n_heads, n_heads).reshape(dt_bias.shape),
        "pool_mix": r_mix.reshape(pool_mix.shape), "pool_scale": s_take(2).reshape(pool_scale.shape),
        "dn_norm_w": s_take(3).reshape(dn_norm_w.shape), "w_pool_out": r_wpo.reshape(w_pool_out.shape),
        "w_dn_out": r_wdn.reshape(w_dn_out.shape), "w_o": r_wo.reshape(w_o.shape),
        "final_norm_w": s_take(1).reshape(final_norm_w.shape),
    }
    loss = s_take(6, 1)[0]

    weights = dict(meta_tokens=meta_tokens, norm_w=norm_w, w_in=w_in, conv_w=conv_w, A_log=A_log, dt_bias=dt_bias,
                   pool_mix=pool_mix, pool_scale=pool_scale, dn_norm_w=dn_norm_w, w_pool_out=w_pool_out, w_dn_out=w_dn_out,
                   w_o=w_o, final_norm_w=final_norm_w)
    m_in = dict(meta_tokens=m_meta_tokens, norm_w=m_norm_w, w_in=m_w_in, conv_w=m_conv_w, A_log=m_A_log, dt_bias=m_dt_bias,
                pool_mix=m_pool_mix, pool_scale=m_pool_scale, dn_norm_w=m_dn_norm_w, w_pool_out=m_w_pool_out,
                w_dn_out=m_w_dn_out, w_o=m_w_o, final_norm_w=m_final_norm_w)
    v_in = dict(meta_tokens=v_meta_tokens, norm_w=v_norm_w, w_in=v_w_in, conv_w=v_conv_w, A_log=v_A_log, dt_bias=v_dt_bias,
                pool_mix=v_pool_mix, pool_scale=v_pool_scale, dn_norm_w=v_dn_norm_w, w_pool_out=v_w_pool_out,
                w_dn_out=v_w_dn_out, w_o=v_w_o, final_norm_w=v_final_norm_w)
    names = list(weights)
    upd = {n: _adamw(weights[n], grads[n], m_in[n], v_in[n], "adamw_" + n) for n in names}
    return (loss, grad_x, *[grads[n] for n in names], *[upd[n][0] for n in names], *[upd[n][1] for n in names],
            *[upd[n][2] for n in names])
```

```python
import functools

import jax
import jax.numpy as jnp
import numpy as np
from jax import lax
from jax.experimental import pallas as pl
from jax.experimental.pallas import tpu as pltpu

F32 = jnp.float32
BF16 = jnp.bfloat16
HIGHEST = lax.Precision.HIGHEST
MESH = pl.DeviceIdType.MESH

CHUNK = 64
NORM_EPS = 1e-6
POOL_WINDOWS = (2, 4, 8, 16)
ADAM_LR, ADAM_B1, ADAM_B2, ADAM_EPS, ADAM_WD, ADAM_STEP = 0.001, 0.9, 0.999, 1e-08, 0.01, 10
N_DEV = 8
LANES = 128
VMEM_LIMIT = 48 * 1024 * 1024

NN = (((1,), (0,)), ((), ()))
NT = (((1,), (1,)), ((), ()))
TN = (((0,), (0,)), ((), ()))


def _call(body, **kw):
    return pl.pallas_call(body, **kw)


def _params(sem=None):
    return pltpu.CompilerParams(dimension_semantics=sem, vmem_limit_bytes=VMEM_LIMIT)


def _tile(n, pref, align):
    for d in range(min(pref, n), 0, -1):
        if n % d == 0 and d % align == 0:
            return d
    return n


def _dot(a, b, dims=NN, precision=None):
    return lax.dot_general(a, b, dims, precision=precision, preferred_element_type=F32)


def _sigmoid(x):
    return 0.5 * jnp.tanh(0.5 * x) + 0.5


def _silu(x):
    return x * _sigmoid(x)


def _softplus(x):
    return jnp.maximum(x, 0.0) + jnp.log(1.0 + jnp.exp(-jnp.abs(x)))


def _rmsnorm(x, w):
    return x * lax.rsqrt(jnp.mean(x * x, axis=-1, keepdims=True) + NORM_EPS) * w


def _shift_down(x, j, row):
    if j == 0:
        return x
    return jnp.where(row >= j, pltpu.roll(x, j, 0), 0.0)


def _shift_up(x, j, row):
    if j == 0:
        return x
    n = x.shape[0]
    return jnp.where(row < n - j, pltpu.roll(x, n - j, 0), 0.0)


def _matmul(a, b, dims, out_dtype, tm, tn, tk, name, col_blocks=None, after=None):
    ta = dims == TN
    tb = dims == NT
    m, kdim = (a.shape[1], a.shape[0]) if ta else a.shape
    n = b.shape[0] if tb else b.shape[1]
    if col_blocks:
        tn = n // col_blocks
    tm, tn, tk = _tile(m, tm, 8), _tile(n, tn, LANES), _tile(kdim, tk, LANES if not ta else 16)
    nk = kdim // tk

    n_extra = 0 if after is None else 1

    def body(a_ref, b_ref, *refs):
        o_ref, scratch = refs[n_extra], refs[n_extra + 1:]
        part = _dot(a_ref[...].astype(BF16), b_ref[...].astype(BF16), dims)
        if nk == 1:
            o_ref[...] = part.astype(o_ref.dtype).reshape(o_ref.shape)
            return
        acc_ref, = scratch
        k = pl.program_id(2)

        @pl.when(k == 0)
        def _():
            acc_ref[...] = part

        @pl.when(k > 0)
        def _():
            acc_ref[...] += part

        @pl.when(k == nk - 1)
        def _():
            o_ref[...] = acc_ref[...].astype(o_ref.dtype).reshape(o_ref.shape)

    a_spec = pl.BlockSpec((tk, tm), lambda i, j, k: (k, i)) if ta else pl.BlockSpec((tm, tk), lambda i, j, k: (i, k))
    b_spec = pl.BlockSpec((tn, tk), lambda i, j, k: (j, k)) if tb else pl.BlockSpec((tk, tn), lambda i, j, k: (k, j))
    if col_blocks:
        out_spec = pl.BlockSpec((1, tm, tn), lambda i, j, k: (j, i, 0))
        out_shape = jax.ShapeDtypeStruct((col_blocks, m, tn), out_dtype)
    else:
        out_spec = pl.BlockSpec((tm, tn), lambda i, j, k: (i, j))
        out_shape = jax.ShapeDtypeStruct((m, n), out_dtype)
    return _call(
        body, name=name, grid=(m // tm, n // tn, nk),
        in_specs=[a_spec, b_spec] + [ANY] * n_extra, out_specs=out_spec, out_shape=out_shape,
        scratch_shapes=[] if nk == 1 else [pltpu.VMEM((tm, tn), F32)],
        compiler_params=_params(("parallel", "parallel", "arbitrary")),
    )(a, b, *([] if after is None else [after]))


def _norm_in_fwd(x2d, meta, w, pad):
    seq, d = x2d.shape
    tr = pad + meta.shape[0]
    assert seq % tr == 0 and tr % 16 == 0
    lp = tr + seq

    def body(x_ref, m_ref, w_ref, h_ref, o_ref):
        def emit(h):
            h_ref[...] = h
            o_ref[...] = _rmsnorm(h, w_ref[...]).astype(BF16)

        @pl.when(pl.program_id(0) == 0)
        def _():
            emit(jnp.concatenate([jnp.zeros((pad, d), F32), m_ref[...]], axis=0) if pad else m_ref[...])

        @pl.when(pl.program_id(0) > 0)
        def _():
            emit(x_ref[...])

    row = pl.BlockSpec((tr, d), lambda i: (i, 0))
    return _call(
        body, name="norm_in_fwd", grid=(lp // tr,),
        in_specs=[pl.BlockSpec((tr, d), lambda i: (jnp.maximum(i - 1, 0), 0)), pl.BlockSpec(meta.shape, lambda i: (0, 0)),
                  pl.BlockSpec((1, d), lambda i: (0, 0))],
        out_specs=[row, row],
        out_shape=[jax.ShapeDtypeStruct((lp, d), F32), jax.ShapeDtypeStruct((lp, d), BF16)],
        compiler_params=_params(("arbitrary",)),
    )(x2d, meta, w)


def _norm_in_bwd(h0, w, dxn, dh1, x0):
    lp, d = h0.shape
    tr = x0
    assert lp % tr == 0

    def body(h_ref, w_ref, da_ref, dh1_ref, head_ref, gx_ref, dw_ref):
        i = pl.program_id(0)
        _, vjp = jax.vjp(_rmsnorm, h_ref[...], w_ref[...])
        dh, dw = vjp(da_ref[...])
        dh = dh + dh1_ref[...]

        @pl.when(i == 0)
        def _():
            head_ref[...] = dh
            dw_ref[...] = dw

        @pl.when(i > 0)
        def _():
            gx_ref[...] = dh
            dw_ref[...] += dw

    row = pl.BlockSpec((tr, d), lambda i: (i, 0))
    vec = pl.BlockSpec((1, d), lambda i: (0, 0))
    return _call(
        body, name="norm_in_bwd", grid=(lp // tr,),
        in_specs=[row, vec, row, row],
        out_specs=[pl.BlockSpec((tr, d), lambda i: (0, 0)), pl.BlockSpec((tr, d), lambda i: (jnp.maximum(i - 1, 0), 0)), vec],
        out_shape=[jax.ShapeDtypeStruct((tr, d), F32), jax.ShapeDtypeStruct((lp - tr, d), F32), jax.ShapeDtypeStruct((1, d), F32)],
        compiler_params=_params(("arbitrary",)),
    )(h0, w, dxn, dh1)


def _final_loss(h0, mo, fw, tgt, x0):
    lp, d = h0.shape
    tr = x0
    assert lp % tr == 0

    def body(h_ref, mo_ref, fw_ref, t_ref, dh_ref, dw_ref, loss_ref):
        i = pl.program_id(0)
        row = i * tr + lax.broadcasted_iota(jnp.int32, (tr, 1), 0)
        mask = jnp.where(row >= x0, 1.0, 0.0).astype(F32)
        tgt_v = t_ref[...]

        def loss_fn(h1, w):
            err = _rmsnorm(h1, w) - tgt_v
            return 0.5 * jnp.sum(jnp.mean(err * err, axis=-1, keepdims=True) * mask, axis=0, keepdims=True)

        loss, vjp = jax.vjp(loss_fn, h_ref[...] + mo_ref[...], fw_ref[...])
        dh, dw = vjp(jnp.ones((1, 1), F32))
        dh_ref[...] = dh

        @pl.when(i == 0)
        def _():
            dw_ref[...] = jnp.zeros_like(dw_ref)
            loss_ref[...] = jnp.zeros_like(loss_ref)

        dw_ref[...] += dw
        loss_ref[...] += jnp.broadcast_to(loss, loss_ref.shape)

    row_spec = pl.BlockSpec((tr, d), lambda i: (i, 0))
    vec = pl.BlockSpec((1, d), lambda i: (0, 0))
    return _call(
        body, name="final_loss", grid=(lp // tr,),
        in_specs=[row_spec, row_spec, vec, pl.BlockSpec((tr, d), lambda i: (jnp.maximum(i - 1, 0), 0))],
        out_specs=[row_spec, vec, pl.BlockSpec((8, LANES), lambda i: (0, 0))],
        out_shape=[jax.ShapeDtypeStruct((lp, d), F32), jax.ShapeDtypeStruct((1, d), F32), jax.ShapeDtypeStruct((8, LANES), F32)],
        compiler_params=_params(("arbitrary",)),
    )(h0, mo, fw, tgt)


def _pool_select(parts, g):
    out = parts[-1]
    for gi in range(len(parts) - 2, -1, -1):
        out = jnp.where(g == gi, parts[gi], out)
    return out


def _pool_count(row, g, pad):
    win = _pool_select([jnp.full(row.shape, float(w), F32) for w in POOL_WINDOWS], g)
    return jnp.maximum(jnp.minimum((row - pad + 1).astype(F32), win), 1.0)


def _pooled(u, g, row, pad):
    sums, s, span = [], u, 1
    for w in POOL_WINDOWS:
        while span < w:
            s = s + _shift_down(s, span, row)
            span *= 2
        sums.append(s)
    return _pool_select(sums, g) / _pool_count(row, g, pad) - u


def _pooled_adjoint(dp, g, row, pad):
    e = dp / _pool_count(row, g, pad)
    sums, s, span = [], e, 1
    for w in POOL_WINDOWS:
        while span < w:
            s = s + _shift_up(s, span, row)
            span *= 2
        sums.append(s)
    return _pool_select(sums, g) - dp


def _pool_specs(lp, pg, ng, z_off):
    u_spec = pl.BlockSpec((lp, pg), lambda g: (0, g))
    z_spec = pl.BlockSpec((lp, pg), lambda g: (0, z_off + g))
    mix_spec = pl.BlockSpec((1, pg, pg), lambda g: (g, 0, 0))
    vec_spec = pl.BlockSpec((1, pg), lambda g: (0, g))
    return u_spec, z_spec, mix_spec, vec_spec


def _pool_fwd(proj, mix, scale, pad):
    lp = proj.shape[0]
    ng, pg, _ = mix.shape
    pw = ng * pg

    def body(u_ref, z_ref, mix_ref, sc_ref, y_ref):
        g = pl.program_id(0)
        row = lax.broadcasted_iota(jnp.int32, (lp, 1), 0)
        pooled = _pooled(u_ref[...], g, row, pad)
        mixed = _dot(pooled.astype(BF16), mix_ref[0])
        y_ref[...] = (mixed * sc_ref[...] * _silu(z_ref[...])).astype(BF16)

    u_spec, z_spec, mix_spec, vec_spec = _pool_specs(lp, pg, ng, pw // pg)
    return _call(
        body, name="pool_fwd", grid=(ng,), in_specs=[u_spec, z_spec, mix_spec, vec_spec], out_specs=u_spec,
        out_shape=jax.ShapeDtypeStruct((lp, pw), BF16), compiler_params=_params(("parallel",)),
    )(proj, proj, mix, scale)


def _pool_bwd(proj, mix, scale, dy, pad):
    lp = proj.shape[0]
    ng, pg, _ = mix.shape
    pw = ng * pg

    def body(u_ref, z_ref, mix_ref, sc_ref, dy_ref, du_ref, dz_ref, dmix_ref, dsc_ref):
        g = pl.program_id(0)
        row = lax.broadcasted_iota(jnp.int32, (lp, 1), 0)
        real = row >= pad
        z = z_ref[...]
        pooled = _pooled(u_ref[...], g, row, pad).astype(BF16)
        mixed = _dot(pooled, mix_ref[0])
        sig = _sigmoid(z)
        sz = z * sig
        dyv = dy_ref[...]
        dsc_ref[...] = jnp.sum(dyv * mixed * sz, axis=0, keepdims=True)
        d_sz = dyv * mixed * sc_ref[...]
        dz_ref[...] = jnp.where(real, d_sz * (sig + sz * (1.0 - sig)), 0.0).astype(BF16)
        d_mixed = (dyv * sc_ref[...] * sz).astype(BF16)
        dmix_ref[0] = _dot(pooled, d_mixed, TN)
        d_pooled = _dot(d_mixed, mix_ref[0], NT)
        du_ref[...] = jnp.where(real, _pooled_adjoint(d_pooled, g, row, pad), 0.0).astype(BF16)

    u_spec, z_spec, mix_spec, vec_spec = _pool_specs(lp, pg, ng, pw // pg)
    return _call(
        body, name="pool_bwd", grid=(ng,),
        in_specs=[u_spec, z_spec, mix_spec, vec_spec, u_spec], out_specs=[u_spec, u_spec, mix_spec, vec_spec],
        out_shape=[jax.ShapeDtypeStruct((lp, pw), BF16), jax.ShapeDtypeStruct((lp, pw), BF16),
                   jax.ShapeDtypeStruct((ng, pg, pg), F32), jax.ShapeDtypeStruct((1, pw), F32)],
        compiler_params=_params(("parallel",)),
    )(proj, proj, mix, scale, dy)


def _conv_pre(x, w, row):
    kw = w.shape[0]
    y = w[kw - 1:kw, :] * x
    for kk in range(kw - 1):
        y = y + w[kk:kk + 1, :] * _shift_down(x, kw - 1 - kk, row)
    return y


def _conv_post(y, out_scale):
    s = _silu(y)
    if out_scale is None:
        return s
    return s * lax.rsqrt(jnp.sum(s * s, axis=-1, keepdims=True) + NORM_EPS) * out_scale


def _conv_fwd(proj, col_off, w, hd, out_scale, name):
    lp = proj.shape[0]
    kw, width = w.shape
    blk0 = col_off // hd

    def body(x_ref, w_ref, o_ref):
        row = lax.broadcasted_iota(jnp.int32, (lp, 1), 0)
        o_ref[...] = _conv_post(_conv_pre(x_ref[...], w_ref[...], row), out_scale)

    return _call(
        body, name=name, grid=(width // hd,),
        in_specs=[pl.BlockSpec((lp, hd), lambda j: (0, blk0 + j)), pl.BlockSpec((kw, hd), lambda j: (0, j))],
        out_specs=pl.BlockSpec((lp, hd), lambda j: (0, j)),
        out_shape=jax.ShapeDtypeStruct((lp, width), F32), compiler_params=_params(("parallel",)),
    )(proj, w)


def _conv_bwd(proj, col_off, w, d_out, hd, out_scale, pad, name):
    lp = proj.shape[0]
    kw, width = w.shape
    blk0 = col_off // hd

    def body(x_ref, w_ref, do_ref, dx_ref, dw_ref):
        row = lax.broadcasted_iota(jnp.int32, (lp, 1), 0)
        real = row >= pad
        x, wv = x_ref[...], w_ref[...]
        _, vjp = jax.vjp(functools.partial(_conv_post, out_scale=out_scale), _conv_pre(x, wv, row))
        dy = jnp.where(real, vjp(do_ref[...])[0], 0.0)
        dx = wv[kw - 1:kw, :] * dy
        dw_ref[kw - 1:kw, :] = jnp.sum(dy * x, axis=0, keepdims=True)
        for kk in range(kw - 1):
            ahead = _shift_up(dy, kw - 1 - kk, row)
            dx = dx + wv[kk:kk + 1, :] * ahead
            dw_ref[kk:kk + 1, :] = jnp.sum(ahead * x, axis=0, keepdims=True)
        dx_ref[...] = jnp.where(real, dx, 0.0).astype(BF16)

    col = pl.BlockSpec((lp, hd), lambda j: (0, j))
    wspec = pl.BlockSpec((kw, hd), lambda j: (0, j))
    return _call(
        body, name=name, grid=(width // hd,),
        in_specs=[pl.BlockSpec((lp, hd), lambda j: (0, blk0 + j)), wspec, col], out_specs=[col, wspec],
        out_shape=[jax.ShapeDtypeStruct((lp, width), BF16), jax.ShapeDtypeStruct((kw, width), F32)],
        compiler_params=_params(("parallel",)),
    )(proj, w, d_out)


HEADS_PER_STEP = 16
HEADS_PER_STEP_BWD = 8


def _each(fn, *lists):
    return [fn(*args) for args in zip(*lists)]


def _dot3_each(a_list, b_list, dims=NN):
    hi = lambda t: t.astype(BF16)
    lo = lambda t, t_hi: (t - t_hi.astype(F32)).astype(BF16)
    dot = lambda x, y: _dot(x, y, dims)
    a_hi, b_hi = _each(hi, a_list), _each(hi, b_list)
    a_lo, b_lo = _each(lo, a_list, a_hi), _each(lo, b_list, b_hi)
    hh, hl, lh = _each(dot, a_hi, b_hi), _each(dot, a_hi, b_lo), _each(dot, a_lo, b_hi)
    return _each(lambda x, y, w: x + (y + w), hh, hl, lh)


@jax.custom_vjp
def _unit_lower_inverse(lmats):
    c = lmats[0].shape[0]
    eye = lax.broadcasted_iota(jnp.int32, (c, c), 0) == lax.broadcasted_iota(jnp.int32, (c, c), 1)
    a = [-m for m in lmats]
    tmat = [jnp.where(eye, 1.0, 0.0).astype(F32) + m for m in a]
    span = 2
    while span < c:
        a = _dot3_each(a, a)
        tmat = _each(lambda t, u: t + u, tmat, _dot3_each(tmat, a))
        span *= 2
    return tuple(tmat)


def _unit_lower_inverse_fwd(lmats):
    tmats = _unit_lower_inverse(lmats)
    return tmats, tmats


def _unit_lower_inverse_bwd(tmats, cts):
    left = _each(lambda t, ct: _dot(t, ct, TN, HIGHEST), tmats, cts)
    return (tuple(_each(lambda m, t: -_dot(m, t, NT, HIGHEST), left, tmats)),)


_unit_lower_inverse.defvjp(_unit_lower_inverse_fwd, _unit_lower_inverse_bwd)


@jax.custom_vjp
def _known_inverse(lmats, tmats):
    return tmats


def _known_inverse_fwd(lmats, tmats):
    return tmats, tmats


def _known_inverse_bwd(tmats, cts):
    return _unit_lower_inverse_bwd(tmats, cts)[0], tuple(jnp.zeros_like(t) for t in tmats)


_known_inverse.defvjp(_known_inverse_fwd, _known_inverse_bwd)


def _chunk_math(states, q, k, v, ba, z, prm, nw, head0, rowmask, n_heads, tmats=None, keep_tmats=False):
    c = q.shape[0]
    heads = list(range(len(states)))
    hd = q.shape[1] // len(states)
    lane = lax.broadcasted_iota(jnp.int32, ba.shape, 1)
    sub = lax.broadcasted_iota(jnp.int32, (ba.shape[1], c), 0)
    ri = lax.broadcasted_iota(jnp.int32, (c, c), 0)
    ci = lax.broadcasted_iota(jnp.int32, (c, c), 1)
    last = lax.broadcasted_iota(jnp.int32, (c, 1), 0) == c - 1
    causal, strict = ri >= ci, ri > ci
    beta_all = _sigmoid(ba) * rowmask
    g_all = -jnp.exp(prm[0:1, :]) * _softplus(ba + prm[1:2, :]) * rowmask
    gcum_all = _dot(jnp.where(causal, 1.0, 0.0).astype(F32), g_all, precision=HIGHEST)
    gcum_t = gcum_all.T
    split = lambda t: [t[:, j * hd:(j + 1) * hd] for j in heads]
    qs, ks, vs, zs = split(q), split(k), split(v), split(z)
    beta = [jnp.sum(jnp.where(lane == head0 + j, beta_all, 0.0), axis=1, keepdims=True) for j in heads]
    gcum = [jnp.sum(jnp.where(lane == n_heads + head0 + j, gcum_all, 0.0), axis=1, keepdims=True) for j in heads]
    grow = [jnp.sum(jnp.where(sub == n_heads + head0 + j, gcum_t, 0.0), axis=0, keepdims=True) for j in heads]
    glast = _each(lambda gc: jnp.sum(jnp.where(last, gc, 0.0), axis=0, keepdims=True), gcum)
    decay = _each(lambda gc, gr: jnp.where(causal, jnp.exp(jnp.where(causal, gc - gr, 0.0)), 0.0), gcum, grow)
    eg = _each(jnp.exp, gcum)
    k_beta = _each(jnp.multiply, ks, beta)
    kk = _each(lambda a, b: _dot(a, b, NT), k_beta, ks)
    lmats = tuple(_each(lambda m, dc: jnp.where(strict, m * dc, 0.0), kk, decay))
    tmat = list(_unit_lower_inverse(lmats) if tmats is None else _known_inverse(lmats, tuple(tmats)))
    u_c = _each(_dot, tmat, _each(jnp.multiply, vs, beta))
    w_c = _each(_dot, tmat, _each(jnp.multiply, k_beta, eg))
    qk = _each(lambda a, b, dc: jnp.where(causal, _dot(a, b, NT) * dc, 0.0), qs, ks, decay)
    v_new = _each(lambda u, w, s: u - _dot(w, s), u_c, w_c, list(states))
    o = _each(lambda a, e, s, m, vn: _dot(a * e, s) + _dot(m, vn), qs, eg, list(states), qk, v_new)
    k_dec = _each(lambda a, gl, gc: a * jnp.exp(gl - gc), ks, glast, gcum)
    new_states = _each(lambda s, gl, kd, vn: s * jnp.exp(gl) + _dot(kd, vn, TN), list(states), glast, k_dec, v_new)
    ys = _each(lambda oj, zj: _rmsnorm(oj, nw) * _silu(zj), o, zs)
    if keep_tmats:
        return jnp.concatenate(ys, axis=1), tuple(new_states), tuple(tmat)
    return jnp.concatenate(ys, axis=1), tuple(new_states)


def _chunk_specs(nc, hd, n_heads, z_off, ba_off, rev):
    cidx = (lambda c: nc - 1 - c) if rev else (lambda c: c)
    hb = min(HEADS_PER_STEP_BWD if rev else HEADS_PER_STEP, n_heads)
    assert n_heads % hb == 0 and z_off % (hb * hd) == 0 and ba_off % LANES == 0
    blk = lambda off: pl.BlockSpec((CHUNK, hb * hd), lambda c, g: (cidx(c), off + g))
    ba_spec = lambda off: pl.BlockSpec((CHUNK, LANES), lambda c, g: (cidx(c), off // LANES))
    prm_spec = pl.BlockSpec((8, LANES), lambda c, g: (0, 0))
    nw_spec = pl.BlockSpec((1, hd), lambda c, g: (0, 0))
    st_spec = pl.BlockSpec((1, hb, hd, hd), lambda c, g: (cidx(c), g, 0, 0))
    return blk, ba_spec, prm_spec, nw_spec, st_spec, blk(z_off // (hb * hd))


def _rowmask(chunk_idx, pad):
    row = chunk_idx * CHUNK + lax.broadcasted_iota(jnp.int32, (CHUNK, 1), 0)
    return jnp.where(row >= pad, 1.0, 0.0).astype(F32)


def _chunk_fwd(qn, kn, vv, proj, z_off, ba_off, prm, nw, n_heads, pad):
    lp, dn = qn.shape
    hd = dn // n_heads
    nc = lp // CHUNK
    hb = min(HEADS_PER_STEP, n_heads)

    def body(q_ref, k_ref, v_ref, ba_ref, z_ref, prm_ref, nw_ref, y_ref, hist_ref, tm_ref, st_ref):
        c, g = pl.program_id(0), pl.program_id(1)

        @pl.when(c == 0)
        def _():
            for j in range(hb):
                st_ref[g * hb + j] = jnp.zeros((hd, hd), F32)

        states = tuple(st_ref[g * hb + j] for j in range(hb))
        for j in range(hb):
            hist_ref[0, j] = states[j]
        y, new_states, tmats = _chunk_math(states, q_ref[...], k_ref[...], v_ref[...], ba_ref[...], z_ref[...], prm_ref[...],
                                           nw_ref[...], g * hb, _rowmask(c, pad), n_heads, keep_tmats=True)
        y_ref[...] = y.astype(BF16)
        for j in range(hb):
            st_ref[g * hb + j] = new_states[j]
            tm_ref[0, j] = tmats[j]

    blk, ba_spec, prm_spec, nw_spec, st_spec, z_spec = _chunk_specs(nc, hd, n_heads, z_off, ba_off, False)
    tm_spec = pl.BlockSpec((1, hb, CHUNK, CHUNK), lambda c, g: (c, g, 0, 0))
    return _call(
        body, name="chunk_fwd", grid=(nc, n_heads // hb),
        in_specs=[blk(0), blk(0), blk(0), ba_spec(ba_off), z_spec, prm_spec, nw_spec], out_specs=[blk(0), st_spec, tm_spec],
        out_shape=[jax.ShapeDtypeStruct((lp, dn), BF16), jax.ShapeDtypeStruct((nc, n_heads, hd, hd), F32),
                   jax.ShapeDtypeStruct((nc, n_heads, CHUNK, CHUNK), F32)],
        scratch_shapes=[pltpu.VMEM((n_heads, hd, hd), F32)],
        compiler_params=_params(("arbitrary", "arbitrary")),
    )(qn, kn, vv, proj, proj, prm, nw)


def _chunk_bwd(qn, kn, vv, proj, z_off, ba_off, prm, nw, hist, tmats, dy, n_heads, pad):
    lp, dn = qn.shape
    hd = dn // n_heads
    nc = lp // CHUNK
    hb = min(HEADS_PER_STEP_BWD, n_heads)

    def body(q_ref, k_ref, v_ref, ba_ref, z_ref, prm_ref, nw_ref, hist_ref, tm_ref, dy_ref,
             dq_ref, dk_ref, dv_ref, dba_ref, dz_ref, dprm_ref, dnw_ref, dst_ref):
        step, g = pl.program_id(0), pl.program_id(1)

        @pl.when(step == 0)
        def _():
            for j in range(hb):
                dst_ref[g * hb + j] = jnp.zeros((hd, hd), F32)

        @pl.when((step == 0) & (g == 0))
        def _():
            dprm_ref[...] = jnp.zeros_like(dprm_ref)
            dnw_ref[...] = jnp.zeros_like(dnw_ref)

        @pl.when(g == 0)
        def _():
            dba_ref[...] = jnp.zeros_like(dba_ref)

        def fn(states, q, k, v, ba, z, prm_v, nw_v, known):
            return _chunk_math(states, q, k, v, ba, z, prm_v, nw_v, g * hb, _rowmask(nc - 1 - step, pad), n_heads, tmats=known)

        states = tuple(hist_ref[0, j] for j in range(hb))
        known = tuple(tm_ref[0, j] for j in range(hb))
        _, vjp = jax.vjp(fn, states, q_ref[...], k_ref[...], v_ref[...], ba_ref[...], z_ref[...], prm_ref[...], nw_ref[...], known)
        dst, dq, dk, dv, dba, dz, dprm, dnw, _ = vjp((dy_ref[...], tuple(dst_ref[g * hb + j] for j in range(hb))))
        for j in range(hb):
            dst_ref[g * hb + j] = dst[j]
        dq_ref[...] = dq
        dk_ref[...] = dk
        dv_ref[...] = dv
        dz_ref[...] = dz.astype(BF16)
        dba_ref[...] += dba
        dprm_ref[...] += dprm
        dnw_ref[...] += dnw

    blk, ba_spec, prm_spec, nw_spec, st_spec, z_spec = _chunk_specs(nc, hd, n_heads, z_off, ba_off, True)
    f32_full = jax.ShapeDtypeStruct((lp, dn), F32)
    tm_spec = pl.BlockSpec((1, hb, CHUNK, CHUNK), lambda c, g: (nc - 1 - c, g, 0, 0))
    return _call(
        body, name="chunk_bwd", grid=(nc, n_heads // hb),
        in_specs=[blk(0), blk(0), blk(0), ba_spec(ba_off), z_spec, prm_spec, nw_spec, st_spec, tm_spec, blk(0)],
        out_specs=[blk(0), blk(0), blk(0), ba_spec(0), blk(0), prm_spec, nw_spec],
        out_shape=[f32_full, f32_full, f32_full, jax.ShapeDtypeStruct((lp, LANES), F32), jax.ShapeDtypeStruct((lp, dn), BF16),
                   jax.ShapeDtypeStruct((8, LANES), F32), jax.ShapeDtypeStruct((1, hd), F32)],
        scratch_shapes=[pltpu.VMEM((n_heads, hd, hd), F32)],
        compiler_params=_params(("arbitrary", "arbitrary")),
    )(qn, kn, vv, proj, proj, prm, nw, hist, tmats, dy)


def _merge_math(p, q, gp, gd):
    return _sigmoid(gp) * p + _sigmoid(gd) * q


def _merge_specs(lp, d, gp_off, gd_off):
    tr, tc = _tile(lp, 264, 16), _tile(d, 1024, LANES)
    blk = pl.BlockSpec((tr, tc), lambda i, j: (i, j))
    gp_spec = pl.BlockSpec((tr, tc), lambda i, j: (i, gp_off // tc + j))
    gd_spec = pl.BlockSpec((tr, tc), lambda i, j: (i, gd_off // tc + j))
    return (lp // tr, d // tc), blk, gp_spec, gd_spec


def _merge_fwd(p, q, proj, gp_off, gd_off):
    lp, d = p.shape
    grid, blk, gp_spec, gd_spec = _merge_specs(lp, d, gp_off, gd_off)

    def body(p_ref, q_ref, gp_ref, gd_ref, o_ref):
        o_ref[...] = _merge_math(p_ref[...], q_ref[...], gp_ref[...], gd_ref[...]).astype(BF16)

    return _call(
        body, name="merge_fwd", grid=grid, in_specs=[blk, blk, gp_spec, gd_spec], out_specs=blk,
        out_shape=jax.ShapeDtypeStruct((lp, d), BF16), compiler_params=_params(("parallel", "parallel")),
    )(p, q, proj, proj)


def _merge_bwd(p, q, proj, gp_off, gd_off, dm):
    lp, d = p.shape
    grid, blk, gp_spec, gd_spec = _merge_specs(lp, d, gp_off, gd_off)

    def body(p_ref, q_ref, gp_ref, gd_ref, dm_ref, dp_ref, dq_ref, dgp_ref, dgd_ref):
        _, vjp = jax.vjp(_merge_math, p_ref[...], q_ref[...], gp_ref[...], gd_ref[...])
        for ref, val in zip((dp_ref, dq_ref, dgp_ref, dgd_ref), vjp(dm_ref[...])):
            ref[...] = val.astype(BF16)

    out = jax.ShapeDtypeStruct((lp, d), BF16)
    return _call(
        body, name="merge_bwd", grid=grid, in_specs=[blk, blk, gp_spec, gd_spec, blk], out_specs=[blk] * 4,
        out_shape=[out] * 4, compiler_params=_params(("parallel", "parallel")),
    )(p, q, proj, proj, dm)


def _adamw(w, g, m, v, name):
    shape = w.shape
    w2, g2, m2, v2 = (t.reshape((-1, shape[-1])) for t in (w, g, m, v))
    rows, cols = w2.shape
    tr = _tile(rows, 128, 8)

    def body(w_ref, g_ref, m_ref, v_ref, d_ref, nm_ref, nv_ref):
        gv = g_ref[...]
        nm = ADAM_B1 * m_ref[...] + (1.0 - ADAM_B1) * gv
        nv = ADAM_B2 * v_ref[...] + (1.0 - ADAM_B2) * (gv * gv)
        m_hat = nm / (1.0 - ADAM_B1 ** ADAM_STEP)
        v_hat = nv / (1.0 - ADAM_B2 ** ADAM_STEP)
        d_ref[...] = -ADAM_LR * (m_hat / (jnp.sqrt(v_hat) + ADAM_EPS) + ADAM_WD * w_ref[...])
        nm_ref[...] = nm
        nv_ref[...] = nv

    blk = pl.BlockSpec((tr, cols), lambda i: (i, 0))
    out = jax.ShapeDtypeStruct((rows, cols), F32)
    res = _call(
        body, name=name, grid=(rows // tr,), in_specs=[blk] * 4, out_specs=[blk] * 3, out_shape=[out] * 3,
        compiler_params=_params(("parallel",)),
    )(w2, g2, m2, v2)
    return tuple(t.reshape(shape) for t in res)


def _coords():
    return lax.axis_index("x"), lax.axis_index("y"), lax.axis_index("c")


def _flip(v, bit):
    return 1 - v if bit else v


CHIP_FLIPS = ((1, 0), (0, 1), (1, 1))
ANY = pl.BlockSpec(memory_space=pl.ANY)


def _all_gather(shards):
    n = len(shards)

    def body(*refs):
        x_refs, out_refs = refs[:n], refs[n:2 * n]
        send_sems, recv_sems, local_sems = refs[2 * n:]
        x, y, c = _coords()
        sibling = (x, y, 1 - c)
        chips = [(_flip(x, fx), _flip(y, fy)) for fx, fy in CHIP_FLIPS]

        def copy(a, k, block, to, from_input=False):
            px, py, pc = block
            slot = out_refs[a].at[4 * px + 2 * py + pc]
            return pltpu.make_async_remote_copy(
                src_ref=x_refs[a] if from_input else slot, dst_ref=slot,
                send_sem=send_sems.at[7 * a + k], recv_sem=recv_sems.at[7 * a + k], device_id=to, device_id_type=MESH)

        mine = [pltpu.make_async_copy(x_refs[a], out_refs[a].at[4 * x + 2 * y + c], local_sems.at[a]) for a in range(n)]
        first = []
        for a in range(n):
            mine[a].start()
            first.append(copy(a, 0, (x, y, c), sibling, True))
            first += [copy(a, 1 + j, (x, y, c), (*chip, c), True) for j, chip in enumerate(chips)]
        for cp in first:
            cp.start()
        passed = []
        for j, chip in enumerate(chips):
            for a in range(n):
                copy(a, 1 + j, (*chip, c), (x, y, c)).wait_recv()
                passed.append(copy(a, 4 + j, (*chip, c), sibling))
                passed[-1].start()
        for a in range(n):
            copy(a, 0, (x, y, 1 - c), (x, y, c)).wait_recv()
            for j, chip in enumerate(chips):
                copy(a, 4 + j, (*chip, 1 - c), (x, y, c)).wait_recv()
        for cp in first + passed:
            cp.wait_send()
        for cp in mine:
            cp.wait()

    return _call(
        body, name="all_gather", in_specs=[ANY] * n, out_specs=[ANY] * n,
        out_shape=[jax.ShapeDtypeStruct((N_DEV,) + s.shape, s.dtype) for s in shards],
        scratch_shapes=[pltpu.SemaphoreType.DMA((7 * n,)), pltpu.SemaphoreType.DMA((7 * n,)), pltpu.SemaphoreType.DMA((n,))],
    )(*shards)


def _all_gather_tree(shard, after):
    rows, cols = shard.shape
    half = rows // 2
    assert rows % 32 == 0

    def body(x_ref, after_ref, out_ref, send_sems, recv_sems, local_sem):
        x, y, c = _coords()
        me, sibling = (x, y, c), (x, y, 1 - c)
        x_nbr, y_nbr, diag = (1 - x, y), (x, 1 - y), (1 - x, 1 - y)

        def part(ref, h):
            return ref if h is None else ref.at[pl.ds(h * half, half)]

        def copy(k, block, to, h=None, from_input=False):
            px, py, pc = block
            slot = part(out_ref.at[4 * px + 2 * py + pc], h)
            return pltpu.make_async_remote_copy(
                src_ref=part(x_ref, h) if from_input else slot, dst_ref=slot,
                send_sem=send_sems.at[k], recv_sem=recv_sems.at[k], device_id=to, device_id_type=MESH)

        mine = pltpu.make_async_copy(x_ref, out_ref.at[4 * x + 2 * y + c], local_sem)
        mine.start()
        started = [copy(0, me, sibling, None, True),
                   copy(1, me, (*x_nbr, c), 0, True), copy(2, me, (*x_nbr, c), 1, True),
                   copy(4, me, (*y_nbr, c), 1, True), copy(3, me, (*y_nbr, c), 0, True)]
        for cp in started:
            cp.start()
        copy(1, (*x_nbr, c), me, 0).wait_recv()
        started.append(copy(5, (*x_nbr, c), (*y_nbr, c), 0))
        started[-1].start()
        copy(4, (*y_nbr, c), me, 1).wait_recv()
        started.append(copy(6, (*y_nbr, c), (*x_nbr, c), 1))
        started[-1].start()
        copy(2, (*x_nbr, c), me, 1).wait_recv()
        started.append(copy(7, (*x_nbr, c), sibling))
        started[-1].start()
        copy(3, (*y_nbr, c), me, 0).wait_recv()
        started.append(copy(8, (*y_nbr, c), sibling))
        started[-1].start()
        copy(5, (*diag, c), me, 0).wait_recv()
        copy(6, (*diag, c), me, 1).wait_recv()
        started.append(copy(9, (*diag, c), sibling))
        started[-1].start()
        copy(0, sibling, me).wait_recv()
        for k, chip in ((7, x_nbr), (8, y_nbr), (9, diag)):
            copy(k, (*chip, 1 - c), me).wait_recv()
        for cp in started:
            cp.wait_send()
        mine.wait()

    return _call(
        body, name="all_gather_tree", in_specs=[ANY, ANY], out_specs=ANY,
        out_shape=jax.ShapeDtypeStruct((N_DEV, rows, cols), shard.dtype),
        scratch_shapes=[pltpu.SemaphoreType.DMA((10,)), pltpu.SemaphoreType.DMA((10,)), pltpu.SemaphoreType.DMA],
    )(shard, after)


def _rs_to_sibling(gs, name):
    n = len(gs)

    def body(*refs):
        g_refs, got_refs = refs[:n], refs[n:2 * n]
        send_sems, recv_sems = refs[2 * n:]
        x, y, c = _coords()
        copies = []
        for a in range(n):
            for p in range(4):
                cp = pltpu.make_async_remote_copy(
                    src_ref=g_refs[a].at[2 * p + (1 - c)], dst_ref=got_refs[a].at[p], send_sem=send_sems.at[4 * a + p],
                    recv_sem=recv_sems.at[4 * a + p], device_id=(x, y, 1 - c), device_id_type=MESH)
                cp.start()
                copies.append(cp)
        for cp in copies:
            cp.wait()

    return _call(
        body, name=name, in_specs=[ANY] * n, out_specs=[ANY] * n,
        out_shape=[jax.ShapeDtypeStruct((4,) + g.shape[1:], g.dtype) for g in gs],
        scratch_shapes=[pltpu.SemaphoreType.DMA((4 * n,)), pltpu.SemaphoreType.DMA((4 * n,))],
    )(*gs)


def _rs_pair_sum(g, got, c_idx, name):
    _, rows, cols = g.shape
    tr = _tile(rows, 256, 16)

    def body(c_ref, g_ref, got_ref, o_ref):
        o_ref[...] = (g_ref[...].astype(F32) + got_ref[...].astype(F32)).astype(o_ref.dtype)

    grid_spec = pltpu.PrefetchScalarGridSpec(
        num_scalar_prefetch=1, grid=(4, rows // tr),
        in_specs=[pl.BlockSpec((1, tr, cols), lambda p, i, c_ref: (2 * p + c_ref[0], i, 0)),
                  pl.BlockSpec((1, tr, cols), lambda p, i, c_ref: (p, i, 0))],
        out_specs=pl.BlockSpec((1, tr, cols), lambda p, i, c_ref: (p, i, 0)))
    return _call(
        body, name=name, grid_spec=grid_spec, out_shape=jax.ShapeDtypeStruct((4, rows, cols), g.dtype),
        compiler_params=_params(("parallel", "parallel")),
    )(c_idx, g, got)


def _to_chips_copies(p_refs, got_refs, send_sems, recv_sems):
    x, y, c = _coords()
    copies = []
    for a in range(len(p_refs)):
        for k, (fx, fy) in enumerate(CHIP_FLIPS):
            px, py = _flip(x, fx), _flip(y, fy)
            copies.append(pltpu.make_async_remote_copy(
                src_ref=p_refs[a].at[2 * px + py], dst_ref=got_refs[a].at[k], send_sem=send_sems.at[3 * a + k],
                recv_sem=recv_sems.at[3 * a + k], device_id=(px, py, c), device_id_type=MESH))
    return copies


def _rs_to_chips(partials, name):
    n = len(partials)

    def body(*refs):
        copies = _to_chips_copies(refs[:n], refs[n:2 * n], *refs[2 * n:])
        for cp in copies:
            cp.start()
        for cp in copies:
            cp.wait()

    return _call(
        body, name=name, in_specs=[ANY] * n, out_specs=[ANY] * n,
        out_shape=[jax.ShapeDtypeStruct((3,) + p.shape[1:], p.dtype) for p in partials],
        scratch_shapes=[pltpu.SemaphoreType.DMA((3 * n,)), pltpu.SemaphoreType.DMA((3 * n,))],
    )(*partials)


HBM = pl.BlockSpec(memory_space=pltpu.HBM)
SEM = pl.BlockSpec(memory_space=pltpu.SEMAPHORE)
SIDE_EFFECT = pltpu.CompilerParams(has_side_effects=pltpu.SideEffectType.DATAFLOW_SIDE_EFFECTING)


def _split_start(copies_fn, srcs, land_shapes, n_sems, name, after=None):
    n, m = len(srcs), len(land_shapes)
    extra = [] if after is None else [after]

    def body(*refs):
        outs = refs[n + m + len(extra):]
        send_sems, recv_sems, token = outs[0], outs[1], outs[-1]
        for cp in copies_fn(refs[:n], refs[n:n + m], send_sems, recv_sems):
            cp.start()
        token[...] = jnp.zeros_like(token)

    ins = [pltpu.with_memory_space_constraint(t, pltpu.HBM) for t in list(srcs) + [lax.empty(s.shape, s.dtype) for s in land_shapes]]
    res = _call(
        body, name=name, in_specs=[HBM] * (n + m) + [ANY] * len(extra),
        out_specs=[SEM, SEM] + [HBM] * (n + m) + [pl.BlockSpec(memory_space=pltpu.VMEM)],
        out_shape=[pltpu.SemaphoreType.DMA((n_sems,)), pltpu.SemaphoreType.DMA((n_sems,))]
        + [pltpu.HBM(t.shape, t.dtype) for t in ins] + [jax.ShapeDtypeStruct((8, LANES), F32)],
        input_output_aliases={i: 2 + i for i in range(n + m)}, compiler_params=SIDE_EFFECT,
    )(*ins, *extra)
    return dict(sems=(res[0], res[1]), srcs=res[2:2 + n], lands=res[2 + n:2 + n + m], token=res[-1])


def _split_wait(copies_fn, started, after, name):
    n, m = len(started["srcs"]), len(started["lands"])

    def body(*refs):
        for cp in copies_fn(refs[:n], refs[n:n + m], refs[n + m], refs[n + m + 1]):
            cp.wait_send()
            cp.wait_recv()

    bufs = list(started["srcs"]) + list(started["lands"])
    res = _call(
        body, name=name, in_specs=[HBM] * (n + m) + [SEM, SEM, ANY], out_specs=[HBM] * (n + m),
        out_shape=[pltpu.HBM(t.shape, t.dtype) for t in bufs],
        input_output_aliases={i: i for i in range(n + m)}, compiler_params=SIDE_EFFECT,
    )(*bufs, *started["sems"], after)
    return res[:n], res[n:]


def _to_all_copies(x_refs, out_refs, send_sems, recv_sems):
    x, y, c = _coords()
    copies = []
    for a in range(len(x_refs)):
        for k in range(N_DEV - 1):
            fx, fy, fc = ((k + 1) >> 2) & 1, ((k + 1) >> 1) & 1, (k + 1) & 1
            copies.append(pltpu.make_async_remote_copy(
                src_ref=x_refs[a], dst_ref=out_refs[a].at[4 * x + 2 * y + c], send_sem=send_sems.at[7 * a + k],
                recv_sem=recv_sems.at[7 * a + k], device_id=(_flip(x, fx), _flip(y, fy), _flip(c, fc)), device_id_type=MESH))
    return copies


def _fill_own_block(gathered, shard, me_idx, name):
    rows, cols = shard.shape
    tr = _tile(rows, 512, 16)

    def body(me_ref, g_ref, s_ref, o_ref):
        o_ref[0] = s_ref[...]

    grid_spec = pltpu.PrefetchScalarGridSpec(
        num_scalar_prefetch=1, grid=(rows // tr,),
        in_specs=[ANY, pl.BlockSpec((tr, cols), lambda i, me: (i, 0))],
        out_specs=pl.BlockSpec((1, tr, cols), lambda i, me: (me[0], i, 0)))
    return _call(
        body, name=name, grid_spec=grid_spec, out_shape=jax.ShapeDtypeStruct(gathered.shape, gathered.dtype),
        input_output_aliases={1: 0}, compiler_params=_params(("arbitrary",)),
    )(me_idx, gathered, shard)


def _rs_chip_sum(partial, got, chip_idx, name, part=0, n_parts=1, dst=None):
    _, rows, cols = partial.shape
    tr = _tile(rows, 256, 16)
    steps = rows // tr
    n_dst = 0 if dst is None else 1

    def body(p_idx_ref, p_ref, got_ref, *refs):
        refs[n_dst][...] = ((p_ref[0].astype(F32) + got_ref[0].astype(F32)) + got_ref[1].astype(F32)) + got_ref[2].astype(F32)

    grid_spec = pltpu.PrefetchScalarGridSpec(
        num_scalar_prefetch=1, grid=(steps,),
        in_specs=[pl.BlockSpec((1, tr, cols), lambda i, p_ref: (p_ref[0], i, 0)),
                  pl.BlockSpec((3, tr, cols), lambda i, p_ref: (0, i, 0))] + [ANY] * n_dst,
        out_specs=pl.BlockSpec((tr, cols), lambda i, p_ref: (part * steps + i, 0)))
    return _call(
        body, name=name, grid_spec=grid_spec, out_shape=jax.ShapeDtypeStruct((n_parts * rows, cols), F32),
        input_output_aliases={3: 0} if n_dst else {}, compiler_params=_params(("parallel",)),
    )(chip_idx, partial, got, *([] if dst is None else [dst]))


def _rs_begin(gs, tag, split):
    c_idx = jnp.reshape(lax.axis_index("c"), (1,)).astype(jnp.int32)
    gots = _rs_to_sibling(gs, "rs_to_sibling_" + tag)
    partials = [_rs_pair_sum(g, got, c_idx, "rs_pair_sum_%s%d" % (tag, a)) for a, (g, got) in enumerate(zip(gs, gots))]
    if not split:
        return dict(partials=partials, gots=_rs_to_chips(partials, "rs_to_chips_" + tag))
    lands = [jax.ShapeDtypeStruct((3,) + p.shape[1:], p.dtype) for p in partials]
    return _split_start(_to_chips_copies, partials, lands, 3 * len(partials), "rs_to_chips_start_" + tag)


def _rs_finish(begun, tag, after=None, part=0, n_parts=1, dsts=None):
    x, y, _ = _coords()
    chip_idx = jnp.reshape(2 * x + y, (1,)).astype(jnp.int32)
    if "gots" in begun:
        partials, gots = begun["partials"], begun["gots"]
    else:
        partials, gots = _split_wait(_to_chips_copies, begun, after, "rs_to_chips_wait_" + tag)
    return [_rs_chip_sum(p, got, chip_idx, "rs_chip_sum_%s%d" % (tag, a), part, n_parts, None if dsts is None else dsts[a])
            for a, (p, got) in enumerate(zip(partials, gots))]


RUNS = 3
RUN_FIELDS = 6


def _lane_gather_table(src_of, src_width):
    n_blocks = src_of.shape[0] // LANES
    tab = np.zeros((n_blocks + 1, RUNS, RUN_FIELDS), np.int32)
    tab[:, :, 5] = LANES
    for t in range(n_blocks):
        runs = []
        for lane in range(LANES):
            slab, col = (int(v) for v in src_of[t * LANES + lane])
            if slab < 0:
                continue
            key = (slab, col // LANES, col % LANES - lane)
            if runs and runs[-1][0] == key and runs[-1][2] == lane:
                runs[-1][2] = lane + 1
            else:
                runs.append([key, lane, lane + 1])
        assert len(runs) <= RUNS
        slots = [None] * RUNS
        for key, lo, hi in sorted(runs, key=lambda r: r[0][:2]):
            e = key[1] % 2 if slots[key[1] % 2] is None else slots.index(None)
            slots[e] = (key[0], key[1], key[2], lo, hi, min(LANES, src_width - key[1] * LANES))
        for e in range(RUNS):
            tab[t, e] = slots[e] if slots[e] is not None else (tab[t - 1, e, 0], tab[t - 1, e, 1], 0, 0, 0, LANES) if t else tab[t, e]
    tab[n_blocks, :, :2] = tab[n_blocks - 1, :, :2]
    return tab.reshape(-1)


def _place_run(tab_ref, t, e, block, under):
    base = (t * RUNS + e) * RUN_FIELDS
    shift, lo, hi = tab_ref[base + 2], tab_ref[base + 3], tab_ref[base + 4]
    lane = lax.broadcasted_iota(jnp.int32, (1, LANES), 1)
    return jnp.where((lane >= lo) & (lane < hi), pltpu.roll(block.astype(F32), (LANES - shift) % LANES, 1), under)


def _lane_gather_cols(src, table, out_slabs, out_width, name):
    _, rows, _ = src.shape
    blocks_per_slab = -(-out_width // LANES)

    def body(tab_ref, *refs):
        t, o_ref = pl.program_id(0), refs[RUNS]
        o_ref[0] = _place_run(tab_ref, t, 1, refs[1][0], _place_run(tab_ref, t, 0, refs[0][0], 0.0)).astype(BF16)
        last = (t * RUNS + RUNS - 1) * RUN_FIELDS

        @pl.when(tab_ref[last + 4] > tab_ref[last + 3])
        def _():
            o_ref[0] = _place_run(tab_ref, t, RUNS - 1, refs[RUNS - 1][0], o_ref[0].astype(F32)).astype(BF16)

    def src_spec(e):
        return pl.BlockSpec((1, rows, LANES), lambda t, tab: (tab[(t * RUNS + e) * RUN_FIELDS], 0, tab[(t * RUNS + e) * RUN_FIELDS + 1]))

    grid_spec = pltpu.PrefetchScalarGridSpec(
        num_scalar_prefetch=1, grid=(out_slabs * blocks_per_slab,), in_specs=[src_spec(e) for e in range(RUNS)],
        out_specs=pl.BlockSpec((1, rows, LANES), lambda t, tab: (t // blocks_per_slab, 0, t % blocks_per_slab)))
    return _call(
        body, name=name, grid_spec=grid_spec, out_shape=jax.ShapeDtypeStruct((out_slabs, rows, out_width), BF16),
        compiler_params=_params(("arbitrary",)),
    )(jnp.asarray(table), src, src, src)


def _all_reduce_small(vec):
    rows, cols = vec.shape

    def body(v_ref, o_ref, buf, send_sems, recv_sems):
        x, y, c = _coords()
        me = 4 * x + 2 * y + c
        buf[me] = v_ref[...]
        copies = []
        for k in range(N_DEV - 1):
            fx, fy, fc = ((k + 1) >> 2) & 1, ((k + 1) >> 1) & 1, (k + 1) & 1
            cp = pltpu.make_async_remote_copy(
                src_ref=v_ref, dst_ref=buf.at[me], send_sem=send_sems.at[k], recv_sem=recv_sems.at[k],
                device_id=(_flip(x, fx), _flip(y, fy), _flip(c, fc)), device_id_type=MESH)
            cp.start()
            copies.append(cp)
        for cp in copies:
            cp.wait()
        total = buf[0]
        for j in range(1, N_DEV):
            total = total + buf[j]
        o_ref[...] = total

    vmem = pl.BlockSpec(memory_space=pltpu.VMEM)
    return _call(
        body, name="all_reduce_small", in_specs=[vmem], out_specs=vmem,
        out_shape=jax.ShapeDtypeStruct((rows, cols), F32),
        scratch_shapes=[pltpu.VMEM((N_DEV, rows, cols), F32), pltpu.SemaphoreType.DMA((N_DEV - 1,)),
                        pltpu.SemaphoreType.DMA((N_DEV - 1,))],
    )(vec)


def _w_in_column_maps(ns, o_ba, n_logit, n_main, n_all):
    own = np.arange(N_DEV * ns)
    work_of_own = np.where(own < o_ba, own, np.where(own < o_ba + n_logit, n_main + own - o_ba, own - n_logit))
    to_work = np.full((n_all, 2), -1, np.int64)
    to_work[work_of_own, 0] = own // ns
    to_work[work_of_own, 1] = own % ns
    slab_width = -(-ns // LANES) * LANES
    to_own = np.full((N_DEV, slab_width, 2), -1, np.int64)
    to_own[:, :ns, 0] = 0
    to_own[:, :ns, 1] = work_of_own.reshape(N_DEV, ns)
    return to_work, to_own.reshape(-1, 2)


def kernel(x, meta_tokens, norm_w, w_in, conv_w, A_log, dt_bias, pool_mix, pool_scale, dn_norm_w, w_pool_out, w_dn_out, w_o, final_norm_w, loss_target, m_meta_tokens, m_norm_w, m_w_in, m_conv_w, m_A_log, m_dt_bias, m_pool_mix, m_pool_scale, m_dn_norm_w, m_w_pool_out, m_w_dn_out, m_w_o, m_final_norm_w, v_meta_tokens, v_norm_w, v_w_in, v_conv_w, v_A_log, v_dt_bias, v_pool_mix, v_pool_scale, v_dn_norm_w, v_w_pool_out, v_w_dn_out, v_w_o, v_final_norm_w):
    seq, d = x.shape[1], x.shape[2]
    n_meta = meta_tokens.shape[0]
    n_heads, hd = A_log.shape[-1], dn_norm_w.shape[-1]
    dn = n_heads * hd
    pw, ng = pool_scale.shape[-1], pool_mix.shape[1]
    pg = pw // ng
    kw = conv_w.shape[1]
    pad = (-n_meta) % CHUNK
    x0 = pad + n_meta
    lp = x0 + seq
    ns = w_in.shape[-1]
    in_cols = N_DEV * ns
    o_q, o_k, o_v, o_zd = 2 * pw, 2 * pw + dn, 2 * pw + 2 * dn, 2 * pw + 3 * dn
    o_ba = 2 * pw + 4 * dn
    o_gp, o_gd = o_ba, o_ba + d
    n_main = o_gd + d
    n_all = n_main + 2 * LANES
    assert lp % CHUNK == 0 and in_cols == n_main + 2 * n_heads and 2 * n_heads <= LANES and hd == LANES
    cs, ms = conv_w.shape[-1], meta_tokens.shape[-1]
    mr = pool_mix.shape[2]
    assert ms == pg and cs % pg == 0
    to_work, to_own = _w_in_column_maps(ns, o_ba, 2 * n_heads, n_main, n_all)
    cols_major = lambda t: jnp.transpose(t, (1, 0, 2)).reshape(t.shape[1], N_DEV * t.shape[2])

    mix_g, conv_g, meta_g = _all_gather([pool_mix[0].reshape(ng * mr, pg).astype(BF16), conv_w[0], meta_tokens])
    win_g = _all_gather_tree(w_in[0].astype(BF16), after=meta_g)
    late_shards = [w_pool_out[0].astype(BF16), w_dn_out[0].astype(BF16), w_o[0].astype(BF16)]
    late_weights = _split_start(_to_all_copies, late_shards, [jax.ShapeDtypeStruct((N_DEV,) + s.shape, BF16) for s in late_shards],
                                (N_DEV - 1) * len(late_shards), "gather_out_proj_start", after=win_g)
    norm_w_in = norm_w + late_weights["token"][0, 0]
    w_all = _lane_gather_cols(win_g, _lane_gather_table(to_work, ns), 1, n_all, "w_in_to_work").reshape(d, n_all)
    mix_f = jnp.transpose(mix_g.reshape(N_DEV, ng, mr, pg), (1, 0, 2, 3)).reshape(ng, pg, pg)
    conv_f = cols_major(conv_g)
    meta_f = cols_major(meta_g)

    h0, xn = _norm_in_fwd(x[0], meta_f, norm_w_in, pad)
    proj = _matmul(xn, w_all, NN, F32, lp, 768, 2048, "proj")
    y_pool = _pool_fwd(proj, mix_f, pool_scale, pad)
    conv_q, conv_k, conv_v = (conv_f[:, i * dn:(i + 1) * dn] for i in range(3))
    qn = _conv_fwd(proj, o_q, conv_q, hd, float(hd) ** -0.5, "conv_q_fwd")
    kn = _conv_fwd(proj, o_k, conv_k, hd, 1.0, "conv_k_fwd")
    vv = _conv_fwd(proj, o_v, conv_v, hd, None, "conv_v_fwd")
    logit_lanes = (n_heads, LANES - 2 * n_heads)
    prm = jnp.pad(A_log, ((0, 7), logit_lanes)) + jnp.pad(dt_bias, ((1, 6), logit_lanes))
    y_dn, hist, tmats = _chunk_fwd(qn, kn, vv, proj, o_zd, n_main, prm, dn_norm_w, n_heads, pad)
    me_idx = jnp.reshape(4 * lax.axis_index("x") + 2 * lax.axis_index("y") + lax.axis_index("c"), (1,)).astype(jnp.int32)
    _, landed = _split_wait(_to_all_copies, late_weights, y_dn, "gather_out_proj_wait")
    wpo_g, wdn_g, wo_g = (_fill_own_block(g, s, me_idx, "own_block_%d" % i) for i, (g, s) in enumerate(zip(landed, late_shards)))
    wpo_f = cols_major(wpo_g)
    wdn_f = wdn_g.reshape(dn, d)
    wo_f = wo_g.reshape(d, d)
    p_out = _matmul(y_pool, wpo_f, NN, F32, 1056, 1024, 1024, "pool_out")
    q_out = _matmul(y_dn, wdn_f, NN, F32, 1056, 1024, 2048, "dn_out")
    merged = _merge_fwd(p_out, q_out, proj, o_gp, o_gd)
    mo = _matmul(merged, wo_f, NN, F32, 1056, 1024, 2048, "w_o_fwd")
    dh1, d_fw, loss_part = _final_loss(h0, mo, final_norm_w.reshape(1, d), loss_target[0], x0)

    d_merged = _matmul(dh1, wo_f, NT, F32, 1056, 1024, 1024, "w_o_bwd_x")
    g_wo = _matmul(merged, dh1, TN, BF16, 1024, 1024, lp, "w_o_bwd_w")
    d_p, d_q, d_gp, d_gd = _merge_bwd(p_out, q_out, proj, o_gp, o_gd, d_merged)
    d_ypool = _matmul(d_p, wpo_f, NT, F32, 1056, 1024, 2048, "pool_out_bwd_x")
    g_wpo = _matmul(y_pool.T, d_p, NN, BF16, 1024, 1024, lp, "pool_out_bwd_w", col_blocks=N_DEV)
    d_ydn = _matmul(d_q, wdn_f, NT, F32, 1056, 1024, 2048, "dn_out_bwd_x")
    g_wdn = _matmul(y_dn, d_q, TN, BF16, 1024, 1024, lp, "dn_out_bwd_w")
    rs_early = _rs_begin([g_wpo, g_wdn.reshape(N_DEV, dn // N_DEV, d), g_wo.reshape(N_DEV, d // N_DEV, d)], "early", split=True)
    started = rs_early["token"][0, 0]
    d_u, d_zp, g_mix, g_pscale = _pool_bwd(proj, mix_f, pool_scale + started, d_ypool, pad)
    d_qn, d_kn, d_vv, d_ba, d_zd, d_prm, g_dnw = _chunk_bwd(qn, kn, vv, proj, o_zd, n_main, prm + started, dn_norm_w, hist, tmats, d_ydn, n_heads, pad)
    d_qr, g_cq = _conv_bwd(proj, o_q, conv_q, d_qn, hd, float(hd) ** -0.5, pad, "conv_q_bwd")
    d_kr, g_ck = _conv_bwd(proj, o_k, conv_k, d_kn, hd, 1.0, pad, "conv_k_bwd")
    d_vr, g_cv = _conv_bwd(proj, o_v, conv_v, d_vv, hd, None, pad, "conv_v_bwd")
    d_proj = jnp.concatenate([d_u, d_zp, d_qr, d_kr, d_vr, d_zd, d_gp, d_gd, d_ba.astype(BF16), jnp.zeros((lp, LANES), BF16)], axis=1)
    xn_t, rs_late, token = xn.T, [], None
    for half in range(2):
        rows = slice(half * (d // 2), (half + 1) * (d // 2))
        g_wall = _matmul(xn_t[rows], d_proj, NN, BF16, 1024, 768, lp, "w_in_bwd_w_%d" % half, after=token)
        g_win = _lane_gather_cols(g_wall.reshape(1, d // 2, n_all), _lane_gather_table(to_own, n_all), N_DEV, ns, "w_in_grad_to_own_%d" % half)
        rs_late.append(_rs_begin([g_win], "late%d" % half, split=True))
        token = rs_late[-1]["token"]
    d_xn = _matmul(d_proj, w_all, NT, F32, lp, 512, 2432, "w_in_bwd_x", after=token)
    d_head, grad_x, g_nw = _norm_in_bwd(h0, norm_w, d_xn, dh1, x0)
    grad_x = grad_x[None]

    by_cols = lambda t: jnp.transpose(t.reshape(t.shape[0], N_DEV, t.shape[1] // N_DEV), (1, 0, 2))
    g_conv = by_cols(jnp.concatenate([g_cq, g_ck, g_cv], axis=1)).reshape(N_DEV, kw * cs // pg, pg)
    conv_rows = -(-g_conv.shape[1] // 16) * 16
    g_small = jnp.concatenate(
        [jnp.transpose(g_mix.reshape(ng, N_DEV, mr, pg), (1, 0, 2, 3)).reshape(N_DEV, ng * mr, pg), by_cols(d_head[pad:x0]),
         jnp.pad(g_conv, ((0, 0), (0, conv_rows - g_conv.shape[1]), (0, 0)))], axis=1).astype(BF16)
    r_small, = _rs_finish(_rs_begin([g_small], "small", split=False), "small")
    r_mix, r_meta = r_small[:ng * mr], r_small[ng * mr:ng * mr + n_meta]
    r_conv = r_small[ng * mr + n_meta:ng * mr + n_meta + kw * cs // pg]
    r_wpo, r_wdn, r_wo = _rs_finish(rs_early, "early", after=r_small)

    small = [g_nw[0], d_fw[0], g_pscale[0], g_dnw[0], d_prm[0], d_prm[1], loss_part[0]]
    s_sizes = [t.shape[0] for t in small]
    s_cols = -(-sum(s_sizes) // (8 * LANES)) * LANES
    s_vec = jnp.concatenate(small + [jnp.zeros((8 * s_cols - sum(s_sizes),), F32)]).reshape(8, s_cols)
    s_red = _all_reduce_small(s_vec)
    s_sum = s_red.reshape(-1)
    r_win = None
    for half, begun in enumerate(rs_late):
        r_win = _rs_finish(begun, "late%d" % half, after=s_red, part=half, n_parts=2, dsts=r_win)
    r_win, = r_win
    s_offs = [sum(s_sizes[:i]) for i in range(len(s_sizes))]
    s_take = lambda i, n=None, o=0: s_sum[s_offs[i] + o:s_offs[i] + o + (s_sizes[i] if n is None else n)]

    grads = {
        "meta_tokens": r_meta, "norm_w": s_take(0).reshape(norm_w.shape),
        "w_in": r_win.reshape(w_in.shape), "conv_w": r_conv.reshape(conv_w.shape),
        "A_log": s_take(4, n_heads, n_heads).reshape(A_log.shape), "dt_bias": s_take(5, n_heads, n_heads).reshape(dt_bias.shape),
        "pool_mix": r_mix.reshape(pool_mix.shape), "pool_scale": s_take(2).reshape(pool_scale.shape),
        "dn_norm_w": s_take(3).reshape(dn_norm_w.shape), "w_pool_out": r_wpo.reshape(w_pool_out.shape),
        "w_dn_out": r_wdn.reshape(w_dn_out.shape), "w_o": r_wo.reshape(w_o.shape),
        "final_norm_w": s_take(1).reshape(final_norm_w.shape),
    }
    loss = s_take(6, 1)[0]

    weights = dict(meta_tokens=meta_tokens, norm_w=norm_w, w_in=w_in, conv_w=conv_w, A_log=A_log, dt_bias=dt_bias,
                   pool_mix=pool_mix, pool_scale=pool_scale, dn_norm_w=dn_norm_w, w_pool_out=w_pool_out, w_dn_out=w_dn_out,
                   w_o=w_o, final_norm_w=final_norm_w)
    m_in = dict(meta_tokens=m_meta_tokens, norm_w=m_norm_w, w_in=m_w_in, conv_w=m_conv_w, A_log=m_A_log, dt_bias=m_dt_bias,
                pool_mix=m_pool_mix, pool_scale=m_pool_scale, dn_norm_w=m_dn_norm_w, w_pool_out=m_w_pool_out,
                w_dn_out=m_w_dn_out, w_o=m_w_o, final_norm_w=m_final_norm_w)
    v_in = dict(meta_tokens=v_meta_tokens, norm_w=v_norm_w, w_in=v_w_in, conv_w=v_conv_w, A_log=v_A_log, dt_bias=v_dt_bias,
                pool_mix=v_pool_mix, pool_scale=v_pool_scale, dn_norm_w=v_dn_norm_w, w_pool_out=v_w_pool_out,
                w_dn_out=v_w_dn_out, w_o=v_w_o, final_norm_w=v_final_norm_w)
    names = list(weights)
    upd = {n: _adamw(weights[n], grads[n], m_in[n], v_in[n], "adamw_" + n) for n in names}
    return (loss, grad_x, *[grads[n] for n in names], *[upd[n][0] for n in names], *[upd[n][1] for n in names],
            *[upd[n][2] for n in names])
```

```python
import functools

import jax
import jax.numpy as jnp
import numpy as np
from jax import lax
from jax.experimental import pallas as pl
from jax.experimental.pallas import tpu as pltpu

F32 = jnp.float32
BF16 = jnp.bfloat16
HIGHEST = lax.Precision.HIGHEST
MESH = pl.DeviceIdType.MESH

CHUNK = 64
NORM_EPS = 1e-6
POOL_WINDOWS = (2, 4, 8, 16)
ADAM_LR, ADAM_B1, ADAM_B2, ADAM_EPS, ADAM_WD, ADAM_STEP = 0.001, 0.9, 0.999, 1e-08, 0.01, 10
N_DEV = 8
LANES = 128
VMEM_LIMIT = 48 * 1024 * 1024

NN = (((1,), (0,)), ((), ()))
NT = (((1,), (1,)), ((), ()))
TN = (((0,), (0,)), ((), ()))


def _call(body, **kw):
    return pl.pallas_call(body, **kw)


def _params(sem=None):
    return pltpu.CompilerParams(dimension_semantics=sem, vmem_limit_bytes=VMEM_LIMIT)


def _tile(n, pref, align):
    for d in range(min(pref, n), 0, -1):
        if n % d == 0 and d % align == 0:
            return d
    return n


def _dot(a, b, dims=NN, precision=None):
    return lax.dot_general(a, b, dims, precision=precision, preferred_element_type=F32)


def _sigmoid(x):
    return 0.5 * jnp.tanh(0.5 * x) + 0.5


def _silu(x):
    return x * _sigmoid(x)


def _softplus(x):
    return jnp.maximum(x, 0.0) + jnp.log(1.0 + jnp.exp(-jnp.abs(x)))


def _rmsnorm(x, w):
    return x * lax.rsqrt(jnp.mean(x * x, axis=-1, keepdims=True) + NORM_EPS) * w


def _shift_down(x, j, row):
    if j == 0:
        return x
    return jnp.where(row >= j, pltpu.roll(x, j, 0), 0.0)


def _shift_up(x, j, row):
    if j == 0:
        return x
    n = x.shape[0]
    return jnp.where(row < n - j, pltpu.roll(x, n - j, 0), 0.0)


def _matmul(a, b, dims, out_dtype, tm, tn, tk, name, col_blocks=None, after=None):
    ta = dims == TN
    tb = dims == NT
    m, kdim = (a.shape[1], a.shape[0]) if ta else a.shape
    n = b.shape[0] if tb else b.shape[1]
    if col_blocks:
        tn = n // col_blocks
    tm, tn, tk = _tile(m, tm, 8), _tile(n, tn, LANES), _tile(kdim, tk, LANES if not ta else 16)
    nk = kdim // tk

    n_extra = 0 if after is None else 1

    def body(a_ref, b_ref, *refs):
        o_ref, scratch = refs[n_extra], refs[n_extra + 1:]
        part = _dot(a_ref[...].astype(BF16), b_ref[...].astype(BF16), dims)
        if nk == 1:
            o_ref[...] = part.astype(o_ref.dtype).reshape(o_ref.shape)
            return
        acc_ref, = scratch
        k = pl.program_id(2)

        @pl.when(k == 0)
        def _():
            acc_ref[...] = part

        @pl.when(k > 0)
        def _():
            acc_ref[...] += part

        @pl.when(k == nk - 1)
        def _():
            o_ref[...] = acc_ref[...].astype(o_ref.dtype).reshape(o_ref.shape)

    a_spec = pl.BlockSpec((tk, tm), lambda i, j, k: (k, i)) if ta else pl.BlockSpec((tm, tk), lambda i, j, k: (i, k))
    b_spec = pl.BlockSpec((tn, tk), lambda i, j, k: (j, k)) if tb else pl.BlockSpec((tk, tn), lambda i, j, k: (k, j))
    if col_blocks:
        out_spec = pl.BlockSpec((1, tm, tn), lambda i, j, k: (j, i, 0))
        out_shape = jax.ShapeDtypeStruct((col_blocks, m, tn), out_dtype)
    else:
        out_spec = pl.BlockSpec((tm, tn), lambda i, j, k: (i, j))
        out_shape = jax.ShapeDtypeStruct((m, n), out_dtype)
    return _call(
        body, name=name, grid=(m // tm, n // tn, nk),
        in_specs=[a_spec, b_spec] + [ANY] * n_extra, out_specs=out_spec, out_shape=out_shape,
        scratch_shapes=[] if nk == 1 else [pltpu.VMEM((tm, tn), F32)],
        compiler_params=_params(("parallel", "parallel", "arbitrary")),
    )(a, b, *([] if after is None else [after]))


def _norm_in_fwd(x2d, meta, w, pad):
    seq, d = x2d.shape
    tr = pad + meta.shape[0]
    assert seq % tr == 0 and tr % 16 == 0
    lp = tr + seq

    def body(x_ref, m_ref, w_ref, h_ref, o_ref):
        def emit(h):
            h_ref[...] = h
            o_ref[...] = _rmsnorm(h, w_ref[...]).astype(BF16)

        @pl.when(pl.program_id(0) == 0)
        def _():
            emit(jnp.concatenate([jnp.zeros((pad, d), F32), m_ref[...]], axis=0) if pad else m_ref[...])

        @pl.when(pl.program_id(0) > 0)
        def _():
            emit(x_ref[...])

    row = pl.BlockSpec((tr, d), lambda i: (i, 0))
    return _call(
        body, name="norm_in_fwd", grid=(lp // tr,),
        in_specs=[pl.BlockSpec((tr, d), lambda i: (jnp.maximum(i - 1, 0), 0)), pl.BlockSpec(meta.shape, lambda i: (0, 0)),
                  pl.BlockSpec((1, d), lambda i: (0, 0))],
        out_specs=[row, row],
        out_shape=[jax.ShapeDtypeStruct((lp, d), F32), jax.ShapeDtypeStruct((lp, d), BF16)],
        compiler_params=_params(("arbitrary",)),
    )(x2d, meta, w)


def _norm_in_bwd(h0, w, dxn, dh1, x0):
    lp, d = h0.shape
    tr = x0
    assert lp % tr == 0

    def body(h_ref, w_ref, da_ref, dh1_ref, head_ref, gx_ref, dw_ref):
        i = pl.program_id(0)
        _, vjp = jax.vjp(_rmsnorm, h_ref[...], w_ref[...])
        dh, dw = vjp(da_ref[...])
        dh = dh + dh1_ref[...]

        @pl.when(i == 0)
        def _():
            head_ref[...] = dh
            dw_ref[...] = dw

        @pl.when(i > 0)
        def _():
            gx_ref[...] = dh
            dw_ref[...] += dw

    row = pl.BlockSpec((tr, d), lambda i: (i, 0))
    vec = pl.BlockSpec((1, d), lambda i: (0, 0))
    return _call(
        body, name="norm_in_bwd", grid=(lp // tr,),
        in_specs=[row, vec, row, row],
        out_specs=[pl.BlockSpec((tr, d), lambda i: (0, 0)), pl.BlockSpec((tr, d), lambda i: (jnp.maximum(i - 1, 0), 0)), vec],
        out_shape=[jax.ShapeDtypeStruct((tr, d), F32), jax.ShapeDtypeStruct((lp - tr, d), F32), jax.ShapeDtypeStruct((1, d), F32)],
        compiler_params=_params(("arbitrary",)),
    )(h0, w, dxn, dh1)


def _final_loss(h0, mo, fw, tgt, x0):
    lp, d = h0.shape
    tr = x0
    assert lp % tr == 0

    def body(h_ref, mo_ref, fw_ref, t_ref, dh_ref, dw_ref, loss_ref):
        i = pl.program_id(0)
        row = i * tr + lax.broadcasted_iota(jnp.int32, (tr, 1), 0)
        mask = jnp.where(row >= x0, 1.0, 0.0).astype(F32)
        tgt_v = t_ref[...]

        def loss_fn(h1, w):
            err = _rmsnorm(h1, w) - tgt_v
            return 0.5 * jnp.sum(jnp.mean(err * err, axis=-1, keepdims=True) * mask, axis=0, keepdims=True)

        loss, vjp = jax.vjp(loss_fn, h_ref[...] + mo_ref[...], fw_ref[...])
        dh, dw = vjp(jnp.ones((1, 1), F32))
        dh_ref[...] = dh

        @pl.when(i == 0)
        def _():
            dw_ref[...] = jnp.zeros_like(dw_ref)
            loss_ref[...] = jnp.zeros_like(loss_ref)

        dw_ref[...] += dw
        loss_ref[...] += jnp.broadcast_to(loss, loss_ref.shape)

    row_spec = pl.BlockSpec((tr, d), lambda i: (i, 0))
    vec = pl.BlockSpec((1, d), lambda i: (0, 0))
    return _call(
        body, name="final_loss", grid=(lp // tr,),
        in_specs=[row_spec, row_spec, vec, pl.BlockSpec((tr, d), lambda i: (jnp.maximum(i - 1, 0), 0))],
        out_specs=[row_spec, vec, pl.BlockSpec((8, LANES), lambda i: (0, 0))],
        out_shape=[jax.ShapeDtypeStruct((lp, d), F32), jax.ShapeDtypeStruct((1, d), F32), jax.ShapeDtypeStruct((8, LANES), F32)],
        compiler_params=_params(("arbitrary",)),
    )(h0, mo, fw, tgt)


def _pool_select(parts, g):
    out = parts[-1]
    for gi in range(len(parts) - 2, -1, -1):
        out = jnp.where(g == gi, parts[gi], out)
    return out


def _pool_count(row, g, pad):
    win = _pool_select([jnp.full(row.shape, float(w), F32) for w in POOL_WINDOWS], g)
    return jnp.maximum(jnp.minimum((row - pad + 1).astype(F32), win), 1.0)


def _pooled(u, g, row, pad):
    sums, s, span = [], u, 1
    for w in POOL_WINDOWS:
        while span < w:
            s = s + _shift_down(s, span, row)
            span *= 2
        sums.append(s)
    return _pool_select(sums, g) / _pool_count(row, g, pad) - u


def _pooled_adjoint(dp, g, row, pad):
    e = dp / _pool_count(row, g, pad)
    sums, s, span = [], e, 1
    for w in POOL_WINDOWS:
        while span < w:
            s = s + _shift_up(s, span, row)
            span *= 2
        sums.append(s)
    return _pool_select(sums, g) - dp


def _pool_specs(lp, pg, ng, z_off):
    u_spec = pl.BlockSpec((lp, pg), lambda g: (0, g))
    z_spec = pl.BlockSpec((lp, pg), lambda g: (0, z_off + g))
    mix_spec = pl.BlockSpec((1, pg, pg), lambda g: (g, 0, 0))
    vec_spec = pl.BlockSpec((1, pg), lambda g: (0, g))
    return u_spec, z_spec, mix_spec, vec_spec


def _pool_fwd(proj, mix, scale, pad):
    lp = proj.shape[0]
    ng, pg, _ = mix.shape
    pw = ng * pg

    def body(u_ref, z_ref, mix_ref, sc_ref, y_ref):
        g = pl.program_id(0)
        row = lax.broadcasted_iota(jnp.int32, (lp, 1), 0)
        pooled = _pooled(u_ref[...], g, row, pad)
        mixed = _dot(pooled.astype(BF16), mix_ref[0])
        y_ref[...] = (mixed * sc_ref[...] * _silu(z_ref[...])).astype(BF16)

    u_spec, z_spec, mix_spec, vec_spec = _pool_specs(lp, pg, ng, pw // pg)
    return _call(
        body, name="pool_fwd", grid=(ng,), in_specs=[u_spec, z_spec, mix_spec, vec_spec], out_specs=u_spec,
        out_shape=jax.ShapeDtypeStruct((lp, pw), BF16), compiler_params=_params(("parallel",)),
    )(proj, proj, mix, scale)


def _pool_bwd(proj, mix, scale, dy, pad):
    lp = proj.shape[0]
    ng, pg, _ = mix.shape
    pw = ng * pg

    def body(u_ref, z_ref, mix_ref, sc_ref, dy_ref, du_ref, dz_ref, dmix_ref, dsc_ref):
        g = pl.program_id(0)
        row = lax.broadcasted_iota(jnp.int32, (lp, 1), 0)
        real = row >= pad
        z = z_ref[...]
        pooled = _pooled(u_ref[...], g, row, pad).astype(BF16)
        mixed = _dot(pooled, mix_ref[0])
        sig = _sigmoid(z)
        sz = z * sig
        dyv = dy_ref[...]
        dsc_ref[...] = jnp.sum(dyv * mixed * sz, axis=0, keepdims=True)
        d_sz = dyv * mixed * sc_ref[...]
        dz_ref[...] = jnp.where(real, d_sz * (sig + sz * (1.0 - sig)), 0.0).astype(BF16)
        d_mixed = (dyv * sc_ref[...] * sz).astype(BF16)
        dmix_ref[0] = _dot(pooled, d_mixed, TN)
        d_pooled = _dot(d_mixed, mix_ref[0], NT)
        du_ref[...] = jnp.where(real, _pooled_adjoint(d_pooled, g, row, pad), 0.0).astype(BF16)

    u_spec, z_spec, mix_spec, vec_spec = _pool_specs(lp, pg, ng, pw // pg)
    return _call(
        body, name="pool_bwd", grid=(ng,),
        in_specs=[u_spec, z_spec, mix_spec, vec_spec, u_spec], out_specs=[u_spec, u_spec, mix_spec, vec_spec],
        out_shape=[jax.ShapeDtypeStruct((lp, pw), BF16), jax.ShapeDtypeStruct((lp, pw), BF16),
                   jax.ShapeDtypeStruct((ng, pg, pg), F32), jax.ShapeDtypeStruct((1, pw), F32)],
        compiler_params=_params(("parallel",)),
    )(proj, proj, mix, scale, dy)


def _conv_pre(x, w, row):
    kw = w.shape[0]
    y = w[kw - 1:kw, :] * x
    for kk in range(kw - 1):
        y = y + w[kk:kk + 1, :] * _shift_down(x, kw - 1 - kk, row)
    return y


def _conv_post(y, out_scale):
    s = _silu(y)
    if out_scale is None:
        return s
    return s * lax.rsqrt(jnp.sum(s * s, axis=-1, keepdims=True) + NORM_EPS) * out_scale


def _conv_fwd(proj, col_off, w, hd, out_scale, name):
    lp = proj.shape[0]
    kw, width = w.shape
    blk0 = col_off // hd

    def body(x_ref, w_ref, o_ref):
        row = lax.broadcasted_iota(jnp.int32, (lp, 1), 0)
        o_ref[...] = _conv_post(_conv_pre(x_ref[...], w_ref[...], row), out_scale)

    return _call(
        body, name=name, grid=(width // hd,),
        in_specs=[pl.BlockSpec((lp, hd), lambda j: (0, blk0 + j)), pl.BlockSpec((kw, hd), lambda j: (0, j))],
        out_specs=pl.BlockSpec((lp, hd), lambda j: (0, j)),
        out_shape=jax.ShapeDtypeStruct((lp, width), F32), compiler_params=_params(("parallel",)),
    )(proj, w)


def _conv_bwd(proj, col_off, w, d_out, hd, out_scale, pad, name, dst):
    lp = proj.shape[0]
    kw, width = w.shape
    blk0 = col_off // hd

    def body(x_ref, w_ref, do_ref, dst_ref, dx_ref, dw_ref):
        row = lax.broadcasted_iota(jnp.int32, (lp, 1), 0)
        real = row >= pad
        x, wv = x_ref[...], w_ref[...]
        _, vjp = jax.vjp(functools.partial(_conv_post, out_scale=out_scale), _conv_pre(x, wv, row))
        dy = jnp.where(real, vjp(do_ref[...])[0], 0.0)
        dx = wv[kw - 1:kw, :] * dy
        dw_ref[kw - 1:kw, :] = jnp.sum(dy * x, axis=0, keepdims=True)
        for kk in range(kw - 1):
            ahead = _shift_up(dy, kw - 1 - kk, row)
            dx = dx + wv[kk:kk + 1, :] * ahead
            dw_ref[kk:kk + 1, :] = jnp.sum(ahead * x, axis=0, keepdims=True)
        dx_ref[...] = jnp.where(real, dx, 0.0).astype(BF16)

    col = pl.BlockSpec((lp, hd), lambda j: (0, j))
    at_off = pl.BlockSpec((lp, hd), lambda j: (0, blk0 + j))
    wspec = pl.BlockSpec((kw, hd), lambda j: (0, j))
    return _call(
        body, name=name, grid=(width // hd,),
        in_specs=[at_off, wspec, col, ANY], out_specs=[at_off, wspec],
        out_shape=[jax.ShapeDtypeStruct(dst.shape, BF16), jax.ShapeDtypeStruct((kw, width), F32)],
        input_output_aliases={3: 0}, compiler_params=_params(("parallel",)),
    )(proj, w, d_out, dst)


HEADS_PER_STEP = 16
HEADS_PER_STEP_BWD = 8


def _each(fn, *lists):
    return [fn(*args) for args in zip(*lists)]


def _dot3_each(a_list, b_list, dims=NN):
    hi = lambda t: t.astype(BF16)
    lo = lambda t, t_hi: (t - t_hi.astype(F32)).astype(BF16)
    dot = lambda x, y: _dot(x, y, dims)
    a_hi, b_hi = _each(hi, a_list), _each(hi, b_list)
    a_lo, b_lo = _each(lo, a_list, a_hi), _each(lo, b_list, b_hi)
    hh, hl, lh = _each(dot, a_hi, b_hi), _each(dot, a_hi, b_lo), _each(dot, a_lo, b_hi)
    return _each(lambda x, y, w: x + (y + w), hh, hl, lh)


@jax.custom_vjp
def _unit_lower_inverse(lmats):
    c = lmats[0].shape[0]
    eye = lax.broadcasted_iota(jnp.int32, (c, c), 0) == lax.broadcasted_iota(jnp.int32, (c, c), 1)
    a = [-m for m in lmats]
    tmat = [jnp.where(eye, 1.0, 0.0).astype(F32) + m for m in a]
    span = 2
    while span < c:
        a = _dot3_each(a, a)
        tmat = _each(lambda t, u: t + u, tmat, _dot3_each(tmat, a))
        span *= 2
    return tuple(tmat)


def _unit_lower_inverse_fwd(lmats):
    tmats = _unit_lower_inverse(lmats)
    return tmats, tmats


def _unit_lower_inverse_bwd(tmats, cts):
    left = _each(lambda t, ct: _dot(t, ct, TN, HIGHEST), tmats, cts)
    return (tuple(_each(lambda m, t: -_dot(m, t, NT, HIGHEST), left, tmats)),)


_unit_lower_inverse.defvjp(_unit_lower_inverse_fwd, _unit_lower_inverse_bwd)


@jax.custom_vjp
def _known_inverse(lmats, tmats):
    return tmats


def _known_inverse_fwd(lmats, tmats):
    return tmats, tmats


def _known_inverse_bwd(tmats, cts):
    return _unit_lower_inverse_bwd(tmats, cts)[0], tuple(jnp.zeros_like(t) for t in tmats)


_known_inverse.defvjp(_known_inverse_fwd, _known_inverse_bwd)


def _chunk_math(states, q, k, v, ba, z, prm, nw, head0, rowmask, n_heads, tmats=None, keep_tmats=False):
    c = q.shape[0]
    heads = list(range(len(states)))
    hd = q.shape[1] // len(states)
    lane = lax.broadcasted_iota(jnp.int32, ba.shape, 1)
    sub = lax.broadcasted_iota(jnp.int32, (ba.shape[1], c), 0)
    ri = lax.broadcasted_iota(jnp.int32, (c, c), 0)
    ci = lax.broadcasted_iota(jnp.int32, (c, c), 1)
    last = lax.broadcasted_iota(jnp.int32, (c, 1), 0) == c - 1
    causal, strict = ri >= ci, ri > ci
    beta_all = _sigmoid(ba) * rowmask
    g_all = -jnp.exp(prm[0:1, :]) * _softplus(ba + prm[1:2, :]) * rowmask
    gcum_all = _dot(jnp.where(causal, 1.0, 0.0).astype(F32), g_all, precision=HIGHEST)
    gcum_t = gcum_all.T
    split = lambda t: [t[:, j * hd:(j + 1) * hd] for j in heads]
    qs, ks, vs, zs = split(q), split(k), split(v), split(z)
    beta = [jnp.sum(jnp.where(lane == head0 + j, beta_all, 0.0), axis=1, keepdims=True) for j in heads]
    gcum = [jnp.sum(jnp.where(lane == n_heads + head0 + j, gcum_all, 0.0), axis=1, keepdims=True) for j in heads]
    grow = [jnp.sum(jnp.where(sub == n_heads + head0 + j, gcum_t, 0.0), axis=0, keepdims=True) for j in heads]
    glast = _each(lambda gc: jnp.sum(jnp.where(last, gc, 0.0), axis=0, keepdims=True), gcum)
    decay = _each(lambda gc, gr: jnp.where(causal, jnp.exp(jnp.where(causal, gc - gr, 0.0)), 0.0), gcum, grow)
    eg = _each(jnp.exp, gcum)
    k_beta = _each(jnp.multiply, ks, beta)
    kk = _each(lambda a, b: _dot(a, b, NT), k_beta, ks)
    lmats = tuple(_each(lambda m, dc: jnp.where(strict, m * dc, 0.0), kk, decay))
    tmat = list(_unit_lower_inverse(lmats) if tmats is None else _known_inverse(lmats, tuple(tmats)))
    u_c = _each(_dot, tmat, _each(jnp.multiply, vs, beta))
    w_c = _each(_dot, tmat, _each(jnp.multiply, k_beta, eg))
    qk = _each(lambda a, b, dc: jnp.where(causal, _dot(a, b, NT) * dc, 0.0), qs, ks, decay)
    v_new = _each(lambda u, w, s: u - _dot(w, s), u_c, w_c, list(states))
    o = _each(lambda a, e, s, m, vn: _dot(a * e, s) + _dot(m, vn), qs, eg, list(states), qk, v_new)
    k_dec = _each(lambda a, gl, gc: a * jnp.exp(gl - gc), ks, glast, gcum)
    new_states = _each(lambda s, gl, kd, vn: s * jnp.exp(gl) + _dot(kd, vn, TN), list(states), glast, k_dec, v_new)
    ys = _each(lambda oj, zj: _rmsnorm(oj, nw) * _silu(zj), o, zs)
    if keep_tmats:
        return jnp.concatenate(ys, axis=1), tuple(new_states), tuple(tmat)
    return jnp.concatenate(ys, axis=1), tuple(new_states)


def _chunk_specs(nc, hd, n_heads, z_off, ba_off, rev):
    cidx = (lambda c: nc - 1 - c) if rev else (lambda c: c)
    hb = min(HEADS_PER_STEP_BWD if rev else HEADS_PER_STEP, n_heads)
    assert n_heads % hb == 0 and z_off % (hb * hd) == 0 and ba_off % LANES == 0
    blk = lambda off: pl.BlockSpec((CHUNK, hb * hd), lambda c, g: (cidx(c), off + g))
    ba_spec = lambda off: pl.BlockSpec((CHUNK, LANES), lambda c, g: (cidx(c), off // LANES))
    prm_spec = pl.BlockSpec((8, LANES), lambda c, g: (0, 0))
    nw_spec = pl.BlockSpec((1, hd), lambda c, g: (0, 0))
    st_spec = pl.BlockSpec((1, hb, hd, hd), lambda c, g: (cidx(c), g, 0, 0))
    return blk, ba_spec, prm_spec, nw_spec, st_spec, blk(z_off // (hb * hd))


def _rowmask(chunk_idx, pad):
    row = chunk_idx * CHUNK + lax.broadcasted_iota(jnp.int32, (CHUNK, 1), 0)
    return jnp.where(row >= pad, 1.0, 0.0).astype(F32)


def _chunk_fwd(qn, kn, vv, proj, z_off, ba_off, prm, nw, n_heads, pad):
    lp, dn = qn.shape
    hd = dn // n_heads
    nc = lp // CHUNK
    hb = min(HEADS_PER_STEP, n_heads)

    def body(q_ref, k_ref, v_ref, ba_ref, z_ref, prm_ref, nw_ref, y_ref, hist_ref, tm_ref, st_ref):
        c, g = pl.program_id(0), pl.program_id(1)

        @pl.when(c == 0)
        def _():
            for j in range(hb):
                st_ref[g * hb + j] = jnp.zeros((hd, hd), F32)

        states = tuple(st_ref[g * hb + j] for j in range(hb))
        for j in range(hb):
            hist_ref[0, j] = states[j]
        y, new_states, tmats = _chunk_math(states, q_ref[...], k_ref[...], v_ref[...], ba_ref[...], z_ref[...], prm_ref[...],
                                           nw_ref[...], g * hb, _rowmask(c, pad), n_heads, keep_tmats=True)
        y_ref[...] = y.astype(BF16)
        for j in range(hb):
            st_ref[g * hb + j] = new_states[j]
            tm_ref[0, j] = tmats[j]

    blk, ba_spec, prm_spec, nw_spec, st_spec, z_spec = _chunk_specs(nc, hd, n_heads, z_off, ba_off, False)
    tm_spec = pl.BlockSpec((1, hb, CHUNK, CHUNK), lambda c, g: (c, g, 0, 0))
    return _call(
        body, name="chunk_fwd", grid=(nc, n_heads // hb),
        in_specs=[blk(0), blk(0), blk(0), ba_spec(ba_off), z_spec, prm_spec, nw_spec], out_specs=[blk(0), st_spec, tm_spec],
        out_shape=[jax.ShapeDtypeStruct((lp, dn), BF16), jax.ShapeDtypeStruct((nc, n_heads, hd, hd), F32),
                   jax.ShapeDtypeStruct((nc, n_heads, CHUNK, CHUNK), F32)],
        scratch_shapes=[pltpu.VMEM((n_heads, hd, hd), F32)],
        compiler_params=_params(("arbitrary", "arbitrary")),
    )(qn, kn, vv, proj, proj, prm, nw)


def _chunk_bwd(qn, kn, vv, proj, z_off, ba_off, prm, nw, hist, tmats, dy, n_heads, pad, d_proj):
    lp, dn = qn.shape
    hd = dn // n_heads
    nc = lp // CHUNK
    hb = min(HEADS_PER_STEP_BWD, n_heads)

    def body(q_ref, k_ref, v_ref, ba_ref, z_ref, prm_ref, nw_ref, hist_ref, tm_ref, dy_ref, d_proj_ref,
             dq_ref, dk_ref, dv_ref, dba_ref, dz_ref, dprm_ref, dnw_ref, dst_ref):
        step, g = pl.program_id(0), pl.program_id(1)

        @pl.when(step == 0)
        def _():
            for j in range(hb):
                dst_ref[g * hb + j] = jnp.zeros((hd, hd), F32)

        @pl.when((step == 0) & (g == 0))
        def _():
            dprm_ref[...] = jnp.zeros_like(dprm_ref)
            dnw_ref[...] = jnp.zeros_like(dnw_ref)

        @pl.when(g == 0)
        def _():
            dba_ref[...] = jnp.zeros_like(dba_ref)

        def fn(states, q, k, v, ba, z, prm_v, nw_v, known):
            return _chunk_math(states, q, k, v, ba, z, prm_v, nw_v, g * hb, _rowmask(nc - 1 - step, pad), n_heads, tmats=known)

        states = tuple(hist_ref[0, j] for j in range(hb))
        known = tuple(tm_ref[0, j] for j in range(hb))
        _, vjp = jax.vjp(fn, states, q_ref[...], k_ref[...], v_ref[...], ba_ref[...], z_ref[...], prm_ref[...], nw_ref[...], known)
        dst, dq, dk, dv, dba, dz, dprm, dnw, _ = vjp((dy_ref[...], tuple(dst_ref[g * hb + j] for j in range(hb))))
        for j in range(hb):
            dst_ref[g * hb + j] = dst[j]
        dq_ref[...] = dq
        dk_ref[...] = dk
        dv_ref[...] = dv
        dz_ref[...] = dz.astype(BF16)
        dba_ref[...] += dba
        dprm_ref[...] += dprm
        dnw_ref[...] += dnw

    blk, ba_spec, prm_spec, nw_spec, st_spec, z_spec = _chunk_specs(nc, hd, n_heads, z_off, ba_off, True)
    f32_full = jax.ShapeDtypeStruct((lp, dn), F32)
    tm_spec = pl.BlockSpec((1, hb, CHUNK, CHUNK), lambda c, g: (nc - 1 - c, g, 0, 0))
    return _call(
        body, name="chunk_bwd", grid=(nc, n_heads // hb),
        in_specs=[blk(0), blk(0), blk(0), ba_spec(ba_off), z_spec, prm_spec, nw_spec, st_spec, tm_spec, blk(0), ANY],
        out_specs=[blk(0), blk(0), blk(0), ba_spec(0), z_spec, prm_spec, nw_spec],
        out_shape=[f32_full, f32_full, f32_full, jax.ShapeDtypeStruct((lp, LANES), F32), jax.ShapeDtypeStruct(d_proj.shape, BF16),
                   jax.ShapeDtypeStruct((8, LANES), F32), jax.ShapeDtypeStruct((1, hd), F32)],
        scratch_shapes=[pltpu.VMEM((n_heads, hd, hd), F32)],
        input_output_aliases={10: 4}, compiler_params=_params(("arbitrary", "arbitrary")),
    )(qn, kn, vv, proj, proj, prm, nw, hist, tmats, dy, d_proj)


def _merge_math(p, q, gp, gd):
    return _sigmoid(gp) * p + _sigmoid(gd) * q


def _merge_specs(lp, d, gp_off, gd_off):
    tr, tc = _tile(lp, 264, 16), _tile(d, 1024, LANES)
    blk = pl.BlockSpec((tr, tc), lambda i, j: (i, j))
    gp_spec = pl.BlockSpec((tr, tc), lambda i, j: (i, gp_off // tc + j))
    gd_spec = pl.BlockSpec((tr, tc), lambda i, j: (i, gd_off // tc + j))
    return (lp // tr, d // tc), blk, gp_spec, gd_spec


def _merge_fwd(p, q, proj, gp_off, gd_off):
    lp, d = p.shape
    grid, blk, gp_spec, gd_spec = _merge_specs(lp, d, gp_off, gd_off)

    def body(p_ref, q_ref, gp_ref, gd_ref, o_ref):
        o_ref[...] = _merge_math(p_ref[...], q_ref[...], gp_ref[...], gd_ref[...]).astype(BF16)

    return _call(
        body, name="merge_fwd", grid=grid, in_specs=[blk, blk, gp_spec, gd_spec], out_specs=blk,
        out_shape=jax.ShapeDtypeStruct((lp, d), BF16), compiler_params=_params(("parallel", "parallel")),
    )(p, q, proj, proj)


def _merge_bwd(p, q, proj, gp_off, gd_off, dm):
    lp, d = p.shape
    grid, blk, gp_spec, gd_spec = _merge_specs(lp, d, gp_off, gd_off)

    def body(p_ref, q_ref, gp_ref, gd_ref, dm_ref, dp_ref, dq_ref, dgp_ref, dgd_ref):
        _, vjp = jax.vjp(_merge_math, p_ref[...], q_ref[...], gp_ref[...], gd_ref[...])
        for ref, val in zip((dp_ref, dq_ref, dgp_ref, dgd_ref), vjp(dm_ref[...])):
            ref[...] = val.astype(BF16)

    out = jax.ShapeDtypeStruct((lp, d), BF16)
    return _call(
        body, name="merge_bwd", grid=grid, in_specs=[blk, blk, gp_spec, gd_spec, blk], out_specs=[blk] * 4,
        out_shape=[out] * 4, compiler_params=_params(("parallel", "parallel")),
    )(p, q, proj, proj, dm)


def _adamw(w, g, m, v, name):
    shape = w.shape
    w2, g2, m2, v2 = (t.reshape((-1, shape[-1])) for t in (w, g, m, v))
    rows, cols = w2.shape
    tr = _tile(rows, 128, 8)

    def body(w_ref, g_ref, m_ref, v_ref, d_ref, nm_ref, nv_ref):
        gv = g_ref[...]
        nm = ADAM_B1 * m_ref[...] + (1.0 - ADAM_B1) * gv
        nv = ADAM_B2 * v_ref[...] + (1.0 - ADAM_B2) * (gv * gv)
        m_hat = nm / (1.0 - ADAM_B1 ** ADAM_STEP)
        v_hat = nv / (1.0 - ADAM_B2 ** ADAM_STEP)
        d_ref[...] = -ADAM_LR * (m_hat / (jnp.sqrt(v_hat) + ADAM_EPS) + ADAM_WD * w_ref[...])
        nm_ref[...] = nm
        nv_ref[...] = nv

    blk = pl.BlockSpec((tr, cols), lambda i: (i, 0))
    out = jax.ShapeDtypeStruct((rows, cols), F32)
    res = _call(
        body, name=name, grid=(rows // tr,), in_specs=[blk] * 4, out_specs=[blk] * 3, out_shape=[out] * 3,
        compiler_params=_params(("parallel",)),
    )(w2, g2, m2, v2)
    return tuple(t.reshape(shape) for t in res)


def _coords():
    return lax.axis_index("x"), lax.axis_index("y"), lax.axis_index("c")


def _flip(v, bit):
    return 1 - v if bit else v


CHIP_FLIPS = ((1, 0), (0, 1), (1, 1))
ANY = pl.BlockSpec(memory_space=pl.ANY)


def _all_gather(shards):
    n = len(shards)

    def body(*refs):
        x_refs, out_refs = refs[:n], refs[n:2 * n]
        send_sems, recv_sems, local_sems = refs[2 * n:]
        x, y, c = _coords()
        sibling = (x, y, 1 - c)
        chips = [(_flip(x, fx), _flip(y, fy)) for fx, fy in CHIP_FLIPS]

        def copy(a, k, block, to, from_input=False):
            px, py, pc = block
            slot = out_refs[a].at[4 * px + 2 * py + pc]
            return pltpu.make_async_remote_copy(
                src_ref=x_refs[a] if from_input else slot, dst_ref=slot,
                send_sem=send_sems.at[7 * a + k], recv_sem=recv_sems.at[7 * a + k], device_id=to, device_id_type=MESH)

        mine = [pltpu.make_async_copy(x_refs[a], out_refs[a].at[4 * x + 2 * y + c], local_sems.at[a]) for a in range(n)]
        first = []
        for a in range(n):
            mine[a].start()
            first.append(copy(a, 0, (x, y, c), sibling, True))
            first += [copy(a, 1 + j, (x, y, c), (*chip, c), True) for j, chip in enumerate(chips)]
        for cp in first:
            cp.start()
        passed = []
        for j, chip in enumerate(chips):
            for a in range(n):
                copy(a, 1 + j, (*chip, c), (x, y, c)).wait_recv()
                passed.append(copy(a, 4 + j, (*chip, c), sibling))
                passed[-1].start()
        for a in range(n):
            copy(a, 0, (x, y, 1 - c), (x, y, c)).wait_recv()
            for j, chip in enumerate(chips):
                copy(a, 4 + j, (*chip, 1 - c), (x, y, c)).wait_recv()
        for cp in first + passed:
            cp.wait_send()
        for cp in mine:
            cp.wait()

    return _call(
        body, name="all_gather", in_specs=[ANY] * n, out_specs=[ANY] * n,
        out_shape=[jax.ShapeDtypeStruct((N_DEV,) + s.shape, s.dtype) for s in shards],
        scratch_shapes=[pltpu.SemaphoreType.DMA((7 * n,)), pltpu.SemaphoreType.DMA((7 * n,)), pltpu.SemaphoreType.DMA((n,))],
    )(*shards)


def _all_gather_tree(shard, after):
    rows, cols = shard.shape
    half = rows // 2
    assert rows % 32 == 0

    def body(x_ref, after_ref, out_ref, send_sems, recv_sems, local_sem):
        x, y, c = _coords()
        me, sibling = (x, y, c), (x, y, 1 - c)
        x_nbr, y_nbr, diag = (1 - x, y), (x, 1 - y), (1 - x, 1 - y)

        def part(ref, h):
            return ref if h is None else ref.at[pl.ds(h * half, half)]

        def copy(k, block, to, h=None, from_input=False):
            px, py, pc = block
            slot = part(out_ref.at[4 * px + 2 * py + pc], h)
            return pltpu.make_async_remote_copy(
                src_ref=part(x_ref, h) if from_input else slot, dst_ref=slot,
                send_sem=send_sems.at[k], recv_sem=recv_sems.at[k], device_id=to, device_id_type=MESH)

        mine = pltpu.make_async_copy(x_ref, out_ref.at[4 * x + 2 * y + c], local_sem)
        mine.start()
        started = [copy(0, me, sibling, None, True),
                   copy(1, me, (*x_nbr, c), 0, True), copy(2, me, (*x_nbr, c), 1, True),
                   copy(4, me, (*y_nbr, c), 1, True), copy(3, me, (*y_nbr, c), 0, True)]
        for cp in started:
            cp.start()
        copy(1, (*x_nbr, c), me, 0).wait_recv()
        started.append(copy(5, (*x_nbr, c), (*y_nbr, c), 0))
        started[-1].start()
        copy(4, (*y_nbr, c), me, 1).wait_recv()
        started.append(copy(6, (*y_nbr, c), (*x_nbr, c), 1))
        started[-1].start()
        copy(2, (*x_nbr, c), me, 1).wait_recv()
        started.append(copy(7, (*x_nbr, c), sibling))
        started[-1].start()
        copy(3, (*y_nbr, c), me, 0).wait_recv()
        started.append(copy(8, (*y_nbr, c), sibling))
        started[-1].start()
        copy(5, (*diag, c), me, 0).wait_recv()
        copy(6, (*diag, c), me, 1).wait_recv()
        started.append(copy(9, (*diag, c), sibling))
        started[-1].start()
        copy(0, sibling, me).wait_recv()
        for k, chip in ((7, x_nbr), (8, y_nbr), (9, diag)):
            copy(k, (*chip, 1 - c), me).wait_recv()
        for cp in started:
            cp.wait_send()
        mine.wait()

    return _call(
        body, name="all_gather_tree", in_specs=[ANY, ANY], out_specs=ANY,
        out_shape=jax.ShapeDtypeStruct((N_DEV, rows, cols), shard.dtype),
        scratch_shapes=[pltpu.SemaphoreType.DMA((10,)), pltpu.SemaphoreType.DMA((10,)), pltpu.SemaphoreType.DMA],
    )(shard, after)


def _rs_to_sibling(gs, name):
    n = len(gs)

    def body(*refs):
        g_refs, got_refs = refs[:n], refs[n:2 * n]
        send_sems, recv_sems = refs[2 * n:]
        x, y, c = _coords()
        copies = []
        for a in range(n):
            for p in range(4):
                cp = pltpu.make_async_remote_copy(
                    src_ref=g_refs[a].at[2 * p + (1 - c)], dst_ref=got_refs[a].at[p], send_sem=send_sems.at[4 * a + p],
                    recv_sem=recv_sems.at[4 * a + p], device_id=(x, y, 1 - c), device_id_type=MESH)
                cp.start()
                copies.append(cp)
        for cp in copies:
            cp.wait()

    return _call(
        body, name=name, in_specs=[ANY] * n, out_specs=[ANY] * n,
        out_shape=[jax.ShapeDtypeStruct((4,) + g.shape[1:], g.dtype) for g in gs],
        scratch_shapes=[pltpu.SemaphoreType.DMA((4 * n,)), pltpu.SemaphoreType.DMA((4 * n,))],
    )(*gs)


def _rs_pair_sum(g, got, c_idx, name):
    _, rows, cols = g.shape
    tr = _tile(rows, 256, 16)

    def body(c_ref, g_ref, got_ref, o_ref):
        o_ref[...] = (g_ref[...].astype(F32) + got_ref[...].astype(F32)).astype(o_ref.dtype)

    grid_spec = pltpu.PrefetchScalarGridSpec(
        num_scalar_prefetch=1, grid=(4, rows // tr),
        in_specs=[pl.BlockSpec((1, tr, cols), lambda p, i, c_ref: (2 * p + c_ref[0], i, 0)),
                  pl.BlockSpec((1, tr, cols), lambda p, i, c_ref: (p, i, 0))],
        out_specs=pl.BlockSpec((1, tr, cols), lambda p, i, c_ref: (p, i, 0)))
    return _call(
        body, name=name, grid_spec=grid_spec, out_shape=jax.ShapeDtypeStruct((4, rows, cols), g.dtype),
        compiler_params=_params(("parallel", "parallel")),
    )(c_idx, g, got)


def _to_chips_copies(p_refs, got_refs, send_sems, recv_sems):
    x, y, c = _coords()
    copies = []
    for a in range(len(p_refs)):
        for k, (fx, fy) in enumerate(CHIP_FLIPS):
            px, py = _flip(x, fx), _flip(y, fy)
            copies.append(pltpu.make_async_remote_copy(
                src_ref=p_refs[a].at[2 * px + py], dst_ref=got_refs[a].at[k], send_sem=send_sems.at[3 * a + k],
                recv_sem=recv_sems.at[3 * a + k], device_id=(px, py, c), device_id_type=MESH))
    return copies


def _rs_to_chips(partials, name):
    n = len(partials)

    def body(*refs):
        copies = _to_chips_copies(refs[:n], refs[n:2 * n], *refs[2 * n:])
        for cp in copies:
            cp.start()
        for cp in copies:
            cp.wait()

    return _call(
        body, name=name, in_specs=[ANY] * n, out_specs=[ANY] * n,
        out_shape=[jax.ShapeDtypeStruct((3,) + p.shape[1:], p.dtype) for p in partials],
        scratch_shapes=[pltpu.SemaphoreType.DMA((3 * n,)), pltpu.SemaphoreType.DMA((3 * n,))],
    )(*partials)


HBM = pl.BlockSpec(memory_space=pltpu.HBM)
SEM = pl.BlockSpec(memory_space=pltpu.SEMAPHORE)
SIDE_EFFECT = pltpu.CompilerParams(has_side_effects=pltpu.SideEffectType.DATAFLOW_SIDE_EFFECTING)


def _split_start(copies_fn, srcs, land_shapes, n_sems, name, after=None):
    n, m = len(srcs), len(land_shapes)
    extra = [] if after is None else [after]

    def body(*refs):
        outs = refs[n + m + len(extra):]
        send_sems, recv_sems, token = outs[0], outs[1], outs[-1]
        for cp in copies_fn(refs[:n], refs[n:n + m], send_sems, recv_sems):
            cp.start()
        token[...] = jnp.zeros_like(token)

    ins = [pltpu.with_memory_space_constraint(t, pltpu.HBM) for t in list(srcs) + [lax.empty(s.shape, s.dtype) for s in land_shapes]]
    res = _call(
        body, name=name, in_specs=[HBM] * (n + m) + [ANY] * len(extra),
        out_specs=[SEM, SEM] + [HBM] * (n + m) + [pl.BlockSpec(memory_space=pltpu.VMEM)],
        out_shape=[pltpu.SemaphoreType.DMA((n_sems,)), pltpu.SemaphoreType.DMA((n_sems,))]
        + [pltpu.HBM(t.shape, t.dtype) for t in ins] + [jax.ShapeDtypeStruct((8, LANES), F32)],
        input_output_aliases={i: 2 + i for i in range(n + m)}, compiler_params=SIDE_EFFECT,
    )(*ins, *extra)
    return dict(sems=(res[0], res[1]), srcs=res[2:2 + n], lands=res[2 + n:2 + n + m], token=res[-1])


def _split_wait(copies_fn, started, after, name):
    n, m = len(started["srcs"]), len(started["lands"])

    def body(*refs):
        for cp in copies_fn(refs[:n], refs[n:n + m], refs[n + m], refs[n + m + 1]):
            cp.wait_send()
            cp.wait_recv()

    bufs = list(started["srcs"]) + list(started["lands"])
    res = _call(
        body, name=name, in_specs=[HBM] * (n + m) + [SEM, SEM, ANY], out_specs=[HBM] * (n + m),
        out_shape=[pltpu.HBM(t.shape, t.dtype) for t in bufs],
        input_output_aliases={i: i for i in range(n + m)}, compiler_params=SIDE_EFFECT,
    )(*bufs, *started["sems"], after)
    return res[:n], res[n:]


def _to_all_copies(x_refs, out_refs, send_sems, recv_sems):
    x, y, c = _coords()
    copies = []
    for a in range(len(x_refs)):
        for k in range(N_DEV - 1):
            fx, fy, fc = ((k + 1) >> 2) & 1, ((k + 1) >> 1) & 1, (k + 1) & 1
            copies.append(pltpu.make_async_remote_copy(
                src_ref=x_refs[a], dst_ref=out_refs[a].at[4 * x + 2 * y + c], send_sem=send_sems.at[7 * a + k],
                recv_sem=recv_sems.at[7 * a + k], device_id=(_flip(x, fx), _flip(y, fy), _flip(c, fc)), device_id_type=MESH))
    return copies


def _fill_own_block(gathered, shard, me_idx, name):
    rows, cols = shard.shape
    tr = _tile(rows, 512, 16)

    def body(me_ref, g_ref, s_ref, o_ref):
        o_ref[0] = s_ref[...]

    grid_spec = pltpu.PrefetchScalarGridSpec(
        num_scalar_prefetch=1, grid=(rows // tr,),
        in_specs=[ANY, pl.BlockSpec((tr, cols), lambda i, me: (i, 0))],
        out_specs=pl.BlockSpec((1, tr, cols), lambda i, me: (me[0], i, 0)))
    return _call(
        body, name=name, grid_spec=grid_spec, out_shape=jax.ShapeDtypeStruct(gathered.shape, gathered.dtype),
        input_output_aliases={1: 0}, compiler_params=_params(("arbitrary",)),
    )(me_idx, gathered, shard)


def _rs_chip_sum(partial, got, chip_idx, name, part=0, n_parts=1, dst=None):
    _, rows, cols = partial.shape
    tr = _tile(rows, 256, 16)
    steps = rows // tr
    n_dst = 0 if dst is None else 1

    def body(p_idx_ref, p_ref, got_ref, *refs):
        refs[n_dst][...] = ((p_ref[0].astype(F32) + got_ref[0].astype(F32)) + got_ref[1].astype(F32)) + got_ref[2].astype(F32)

    grid_spec = pltpu.PrefetchScalarGridSpec(
        num_scalar_prefetch=1, grid=(steps,),
        in_specs=[pl.BlockSpec((1, tr, cols), lambda i, p_ref: (p_ref[0], i, 0)),
                  pl.BlockSpec((3, tr, cols), lambda i, p_ref: (0, i, 0))] + [ANY] * n_dst,
        out_specs=pl.BlockSpec((tr, cols), lambda i, p_ref: (part * steps + i, 0)))
    return _call(
        body, name=name, grid_spec=grid_spec, out_shape=jax.ShapeDtypeStruct((n_parts * rows, cols), F32),
        input_output_aliases={3: 0} if n_dst else {}, compiler_params=_params(("parallel",)),
    )(chip_idx, partial, got, *([] if dst is None else [dst]))


def _rs_begin(gs, tag, split):
    c_idx = jnp.reshape(lax.axis_index("c"), (1,)).astype(jnp.int32)
    gots = _rs_to_sibling(gs, "rs_to_sibling_" + tag)
    partials = [_rs_pair_sum(g, got, c_idx, "rs_pair_sum_%s%d" % (tag, a)) for a, (g, got) in enumerate(zip(gs, gots))]
    if not split:
        return dict(partials=partials, gots=_rs_to_chips(partials, "rs_to_chips_" + tag))
    lands = [jax.ShapeDtypeStruct((3,) + p.shape[1:], p.dtype) for p in partials]
    return _split_start(_to_chips_copies, partials, lands, 3 * len(partials), "rs_to_chips_start_" + tag)


def _rs_finish(begun, tag, after=None, part=0, n_parts=1, dsts=None):
    x, y, _ = _coords()
    chip_idx = jnp.reshape(2 * x + y, (1,)).astype(jnp.int32)
    if "gots" in begun:
        partials, gots = begun["partials"], begun["gots"]
    else:
        partials, gots = _split_wait(_to_chips_copies, begun, after, "rs_to_chips_wait_" + tag)
    return [_rs_chip_sum(p, got, chip_idx, "rs_chip_sum_%s%d" % (tag, a), part, n_parts, None if dsts is None else dsts[a])
            for a, (p, got) in enumerate(zip(partials, gots))]


RUNS = 3
RUN_FIELDS = 5


def _lane_gather_table(src_of):
    n_blocks = src_of.shape[0] // LANES
    tab = np.zeros((n_blocks + 1, RUNS, RUN_FIELDS), np.int32)
    for t in range(n_blocks):
        runs = []
        for lane in range(LANES):
            slab, col = (int(v) for v in src_of[t * LANES + lane])
            if slab < 0:
                continue
            key = (slab, col // LANES, col % LANES - lane)
            if runs and runs[-1][0] == key and runs[-1][2] == lane:
                runs[-1][2] = lane + 1
            else:
                runs.append([key, lane, lane + 1])
        assert len(runs) <= RUNS
        slots = [None] * RUNS
        for key, lo, hi in sorted(runs, key=lambda r: r[0][:2]):
            e = key[1] % 2 if slots[key[1] % 2] is None else slots.index(None)
            slots[e] = (key[0], key[1], key[2], lo, hi)
        for e in range(RUNS):
            tab[t, e] = slots[e] if slots[e] is not None else (tab[t - 1, e, 0], tab[t - 1, e, 1], 0, 0, 0) if t else tab[t, e]
    tab[n_blocks, :, :2] = tab[n_blocks - 1, :, :2]
    return tab.reshape(-1)


def _place_run(tab_ref, t, e, block, under):
    base = (t * RUNS + e) * RUN_FIELDS
    shift, lo, hi = tab_ref[base + 2], tab_ref[base + 3], tab_ref[base + 4]
    lane = lax.broadcasted_iota(jnp.int32, (1, LANES), 1)
    return jnp.where((lane >= lo) & (lane < hi), pltpu.roll(block.astype(F32), (LANES - shift) % LANES, 1), under)


def _lane_gather_cols(src, table, out_slabs, out_width, name):
    _, rows, _ = src.shape
    blocks_per_slab = -(-out_width // LANES)

    def body(tab_ref, *refs):
        t, o_ref = pl.program_id(0), refs[RUNS]
        o_ref[0] = _place_run(tab_ref, t, 1, refs[1][0], _place_run(tab_ref, t, 0, refs[0][0], 0.0)).astype(BF16)
        last = (t * RUNS + RUNS - 1) * RUN_FIELDS

        @pl.when(tab_ref[last + 4] > tab_ref[last + 3])
        def _():
            o_ref[0] = _place_run(tab_ref, t, RUNS - 1, refs[RUNS - 1][0], o_ref[0].astype(F32)).astype(BF16)

    def src_spec(e):
        return pl.BlockSpec((1, rows, LANES), lambda t, tab: (tab[(t * RUNS + e) * RUN_FIELDS], 0, tab[(t * RUNS + e) * RUN_FIELDS + 1]))

    grid_spec = pltpu.PrefetchScalarGridSpec(
        num_scalar_prefetch=1, grid=(out_slabs * blocks_per_slab,), in_specs=[src_spec(e) for e in range(RUNS)],
        out_specs=pl.BlockSpec((1, rows, LANES), lambda t, tab: (t // blocks_per_slab, 0, t % blocks_per_slab)))
    return _call(
        body, name=name, grid_spec=grid_spec, out_shape=jax.ShapeDtypeStruct((out_slabs, rows, out_width), BF16),
        compiler_params=_params(("arbitrary",)),
    )(jnp.asarray(table), src, src, src)


def _all_reduce_small(vec):
    rows, cols = vec.shape

    def body(v_ref, o_ref, buf, send_sems, recv_sems):
        x, y, c = _coords()
        me = 4 * x + 2 * y + c
        buf[me] = v_ref[...]
        copies = []
        for k in range(N_DEV - 1):
            fx, fy, fc = ((k + 1) >> 2) & 1, ((k + 1) >> 1) & 1, (k + 1) & 1
            cp = pltpu.make_async_remote_copy(
                src_ref=v_ref, dst_ref=buf.at[me], send_sem=send_sems.at[k], recv_sem=recv_sems.at[k],
                device_id=(_flip(x, fx), _flip(y, fy), _flip(c, fc)), device_id_type=MESH)
            cp.start()
            copies.append(cp)
        for cp in copies:
            cp.wait()
        total = buf[0]
        for j in range(1, N_DEV):
            total = total + buf[j]
        o_ref[...] = total

    vmem = pl.BlockSpec(memory_space=pltpu.VMEM)
    return _call(
        body, name="all_reduce_small", in_specs=[vmem], out_specs=vmem,
        out_shape=jax.ShapeDtypeStruct((rows, cols), F32),
        scratch_shapes=[pltpu.VMEM((N_DEV, rows, cols), F32), pltpu.SemaphoreType.DMA((N_DEV - 1,)),
                        pltpu.SemaphoreType.DMA((N_DEV - 1,))],
    )(vec)


def _w_in_column_maps(ns, o_ba, n_logit, n_main, n_all):
    own = np.arange(N_DEV * ns)
    work_of_own = np.where(own < o_ba, own, np.where(own < o_ba + n_logit, n_main + own - o_ba, own - n_logit))
    to_work = np.full((n_all, 2), -1, np.int64)
    to_work[work_of_own, 0] = own // ns
    to_work[work_of_own, 1] = own % ns
    slab_width = -(-ns // LANES) * LANES
    to_own = np.full((N_DEV, slab_width, 2), -1, np.int64)
    to_own[:, :ns, 0] = 0
    to_own[:, :ns, 1] = work_of_own.reshape(N_DEV, ns)
    return to_work, to_own.reshape(-1, 2)


def kernel(x, meta_tokens, norm_w, w_in, conv_w, A_log, dt_bias, pool_mix, pool_scale, dn_norm_w, w_pool_out, w_dn_out, w_o, final_norm_w, loss_target, m_meta_tokens, m_norm_w, m_w_in, m_conv_w, m_A_log, m_dt_bias, m_pool_mix, m_pool_scale, m_dn_norm_w, m_w_pool_out, m_w_dn_out, m_w_o, m_final_norm_w, v_meta_tokens, v_norm_w, v_w_in, v_conv_w, v_A_log, v_dt_bias, v_pool_mix, v_pool_scale, v_dn_norm_w, v_w_pool_out, v_w_dn_out, v_w_o, v_final_norm_w):
    seq, d = x.shape[1], x.shape[2]
    n_meta = meta_tokens.shape[0]
    n_heads, hd = A_log.shape[-1], dn_norm_w.shape[-1]
    dn = n_heads * hd
    pw, ng = pool_scale.shape[-1], pool_mix.shape[1]
    pg = pw // ng
    kw = conv_w.shape[1]
    pad = (-n_meta) % CHUNK
    x0 = pad + n_meta
    lp = x0 + seq
    ns = w_in.shape[-1]
    in_cols = N_DEV * ns
    o_q, o_k, o_v, o_zd = 2 * pw, 2 * pw + dn, 2 * pw + 2 * dn, 2 * pw + 3 * dn
    o_ba = 2 * pw + 4 * dn
    o_gp, o_gd = o_ba, o_ba + d
    n_main = o_gd + d
    n_all = n_main + 2 * LANES
    assert lp % CHUNK == 0 and in_cols == n_main + 2 * n_heads and 2 * n_heads <= LANES and hd == LANES
    cs, ms = conv_w.shape[-1], meta_tokens.shape[-1]
    mr = pool_mix.shape[2]
    assert ms == pg and cs % pg == 0
    to_work, to_own = _w_in_column_maps(ns, o_ba, 2 * n_heads, n_main, n_all)
    cols_major = lambda t: jnp.transpose(t, (1, 0, 2)).reshape(t.shape[1], N_DEV * t.shape[2])

    mix_g, conv_g, meta_g = _all_gather([pool_mix[0].reshape(ng * mr, pg).astype(BF16), conv_w[0], meta_tokens])
    win_g = _all_gather_tree(w_in[0].astype(BF16), after=meta_g)
    late_shards = [w_pool_out[0].astype(BF16), w_dn_out[0].astype(BF16), w_o[0].astype(BF16)]
    late_weights = _split_start(_to_all_copies, late_shards, [jax.ShapeDtypeStruct((N_DEV,) + s.shape, BF16) for s in late_shards],
                                (N_DEV - 1) * len(late_shards), "gather_out_proj_start", after=win_g)
    norm_w_in = norm_w + late_weights["token"][0, 0]
    w_all = _lane_gather_cols(win_g, _lane_gather_table(to_work), 1, n_all, "w_in_to_work").reshape(d, n_all)
    mix_f = jnp.transpose(mix_g.reshape(N_DEV, ng, mr, pg), (1, 0, 2, 3)).reshape(ng, pg, pg)
    conv_f = cols_major(conv_g)
    meta_f = cols_major(meta_g)

    h0, xn = _norm_in_fwd(x[0], meta_f, norm_w_in, pad)
    proj = _matmul(xn, w_all, NN, F32, lp, 768, 2048, "proj")
    y_pool = _pool_fwd(proj, mix_f, pool_scale, pad)
    conv_q, conv_k, conv_v = (conv_f[:, i * dn:(i + 1) * dn] for i in range(3))
    qn = _conv_fwd(proj, o_q, conv_q, hd, float(hd) ** -0.5, "conv_q_fwd")
    kn = _conv_fwd(proj, o_k, conv_k, hd, 1.0, "conv_k_fwd")
    vv = _conv_fwd(proj, o_v, conv_v, hd, None, "conv_v_fwd")
    logit_lanes = (n_heads, LANES - 2 * n_heads)
    prm = jnp.pad(A_log, ((0, 7), logit_lanes)) + jnp.pad(dt_bias, ((1, 6), logit_lanes))
    y_dn, hist, tmats = _chunk_fwd(qn, kn, vv, proj, o_zd, n_main, prm, dn_norm_w, n_heads, pad)
    me_idx = jnp.reshape(4 * lax.axis_index("x") + 2 * lax.axis_index("y") + lax.axis_index("c"), (1,)).astype(jnp.int32)
    _, landed = _split_wait(_to_all_copies, late_weights, y_dn, "gather_out_proj_wait")
    wpo_g, wdn_g, wo_g = (_fill_own_block(g, s, me_idx, "own_block_%d" % i) for i, (g, s) in enumerate(zip(landed, late_shards)))
    wpo_f = cols_major(wpo_g)
    wdn_f = wdn_g.reshape(dn, d)
    wo_f = wo_g.reshape(d, d)
    p_out = _matmul(y_pool, wpo_f, NN, F32, 1056, 1024, 1024, "pool_out")
    q_out = _matmul(y_dn, wdn_f, NN, F32, 1056, 1024, 2048, "dn_out")
    merged = _merge_fwd(p_out, q_out, proj, o_gp, o_gd)
    mo = _matmul(merged, wo_f, NN, F32, 1056, 1024, 2048, "w_o_fwd")
    dh1, d_fw, loss_part = _final_loss(h0, mo, final_norm_w.reshape(1, d), loss_target[0], x0)

    d_merged = _matmul(dh1, wo_f, NT, F32, 1056, 1024, 1024, "w_o_bwd_x")
    g_wo = _matmul(merged, dh1, TN, BF16, 1024, 1024, lp, "w_o_bwd_w")
    d_p, d_q, d_gp, d_gd = _merge_bwd(p_out, q_out, proj, o_gp, o_gd, d_merged)
    d_ypool = _matmul(d_p, wpo_f, NT, F32, 1056, 1024, 2048, "pool_out_bwd_x")
    g_wpo = _matmul(y_pool.T, d_p, NN, BF16, 1024, 1024, lp, "pool_out_bwd_w", col_blocks=N_DEV)
    d_ydn = _matmul(d_q, wdn_f, NT, F32, 1056, 1024, 2048, "dn_out_bwd_x")
    g_wdn = _matmul(y_dn, d_q, TN, BF16, 1024, 1024, lp, "dn_out_bwd_w")
    rs_early = _rs_begin([g_wpo, g_wdn.reshape(N_DEV, dn // N_DEV, d), g_wo.reshape(N_DEV, d // N_DEV, d)], "early", split=True)
    started = rs_early["token"][0, 0]
    d_u, d_zp, g_mix, g_pscale = _pool_bwd(proj, mix_f, pool_scale + started, d_ypool, pad)
    d_proj = lax.empty((lp, n_all), BF16)
    d_qn, d_kn, d_vv, d_ba, d_proj, d_prm, g_dnw = _chunk_bwd(qn, kn, vv, proj, o_zd, n_main, prm + started, dn_norm_w, hist, tmats,
                                                              d_ydn, n_heads, pad, d_proj)
    d_proj, g_cq = _conv_bwd(proj, o_q, conv_q, d_qn, hd, float(hd) ** -0.5, pad, "conv_q_bwd", d_proj)
    d_proj, g_ck = _conv_bwd(proj, o_k, conv_k, d_kn, hd, 1.0, pad, "conv_k_bwd", d_proj)
    d_proj, g_cv = _conv_bwd(proj, o_v, conv_v, d_vv, hd, None, pad, "conv_v_bwd", d_proj)
    for off, piece in ((0, d_u), (pw, d_zp), (o_gp, d_gp), (o_gd, d_gd), (n_main, d_ba.astype(BF16)), (n_main + LANES, jnp.zeros((lp, LANES), BF16))):
        d_proj = lax.dynamic_update_slice(d_proj, piece, (0, off))
    xn_t, rs_late, token = xn.T, [], None
    for half in range(2):
        rows = slice(half * (d // 2), (half + 1) * (d // 2))
        g_wall = _matmul(xn_t[rows], d_proj, NN, BF16, 1024, 768, lp, "w_in_bwd_w_%d" % half, after=token)
        g_win = _lane_gather_cols(g_wall.reshape(1, d // 2, n_all), _lane_gather_table(to_own), N_DEV, ns, "w_in_grad_to_own_%d" % half)
        rs_late.append(_rs_begin([g_win], "late%d" % half, split=True))
        token = rs_late[-1]["token"]
    d_xn = _matmul(d_proj, w_all, NT, F32, lp, 512, 2432, "w_in_bwd_x", after=token)
    d_head, grad_x, g_nw = _norm_in_bwd(h0, norm_w, d_xn, dh1, x0)
    grad_x = grad_x[None]

    by_cols = lambda t: jnp.transpose(t.reshape(t.shape[0], N_DEV, t.shape[1] // N_DEV), (1, 0, 2))
    g_conv = by_cols(jnp.concatenate([g_cq, g_ck, g_cv], axis=1)).reshape(N_DEV, kw * cs // pg, pg)
    conv_rows = -(-g_conv.shape[1] // 16) * 16
    g_small = jnp.concatenate(
        [jnp.transpose(g_mix.reshape(ng, N_DEV, mr, pg), (1, 0, 2, 3)).reshape(N_DEV, ng * mr, pg), by_cols(d_head[pad:x0]),
         jnp.pad(g_conv, ((0, 0), (0, conv_rows - g_conv.shape[1]), (0, 0)))], axis=1).astype(BF16)
    r_small, = _rs_finish(_rs_begin([g_small], "small", split=False), "small")
    r_mix, r_meta = r_small[:ng * mr], r_small[ng * mr:ng * mr + n_meta]
    r_conv = r_small[ng * mr + n_meta:ng * mr + n_meta + kw * cs // pg]
    r_wpo, r_wdn, r_wo = _rs_finish(rs_early, "early", after=r_small)

    small = [g_nw[0], d_fw[0], g_pscale[0], g_dnw[0], d_prm[0], d_prm[1], loss_part[0]]
    s_sizes = [t.shape[0] for t in small]
    s_cols = -(-sum(s_sizes) // (8 * LANES)) * LANES
    s_vec = jnp.concatenate(small + [jnp.zeros((8 * s_cols - sum(s_sizes),), F32)]).reshape(8, s_cols)
    s_red = _all_reduce_small(s_vec)
    s_sum = s_red.reshape(-1)
    r_win = None
    for half, begun in enumerate(rs_late):
        r_win = _rs_finish(begun, "late%d" % half, after=s_red, part=half, n_parts=2, dsts=r_win)
    r_win, = r_win
    s_offs = [sum(s_sizes[:i]) for i in range(len(s_sizes))]
    s_take = lambda i, n=None, o=0: s_sum[s_offs[i] + o:s_offs[i] + o + (s_sizes[i] if n is None else n)]

    grads = {
        "meta_tokens": r_meta, "norm_w": s_take(0).reshape(norm_w.shape),
        "w_in": r_win.reshape(w_in.shape), "conv_w": r_conv.reshape(conv_w.shape),
        "A_log": s_take(4, n_heads, n_heads).reshape(A_log.shape), "dt_bias": s_take(5, n_heads, n_heads).reshape(dt_bias.shape),
        "pool_mix": r_mix.reshape(pool_mix.shape), "pool_scale": s_take(2).reshape(pool_scale.shape),
        "dn_norm_w": s_take(3).reshape(dn_norm_w.shape), "w_pool_out": r_wpo.reshape(w_pool_out.shape),
        "w_dn_out": r_wdn.reshape(w_dn_out.shape), "w_o": r_wo.reshape(w_o.shape),
        "final_norm_w": s_take(1).reshape(final_norm_w.shape),
    }
    loss = s_take(6, 1)[0]

    weights = dict(meta_tokens=meta_tokens, norm_w=norm_w, w_in=w_in, conv_w=conv_w, A_log=A_log, dt_bias=dt_bias,
                   pool_mix=pool_mix, pool_scale=pool_scale, dn_norm_w=dn_norm_w, w_pool_out=w_pool_out, w_dn_out=w_dn_out,
                   w_o=w_o, final_norm_w=final_norm_w)
    m_in = dict(meta_tokens=m_meta_tokens, norm_w=m_norm_w, w_in=m_w_in, conv_w=m_conv_w, A_log=m_A_log, dt_bias=m_dt_bias,
                pool_mix=m_pool_mix, pool_scale=m_pool_scale, dn_norm_w=m_dn_norm_w, w_pool_out=m_w_pool_out,
                w_dn_out=m_w_dn_out, w_o=m_w_o, final_norm_w=m_final_norm_w)
    v_in = dict(meta_tokens=v_meta_tokens, norm_w=v_norm_w, w_in=v_w_in, conv_w=v_conv_w, A_log=v_A_log, dt_bias=v_dt_bias,
                pool_mix=v_pool_mix, pool_scale=v_pool_scale, dn_norm_w=v_dn_norm_w, w_pool_out=v_w_pool_out,
                w_dn_out=v_w_dn_out, w_o=v_w_o, final_norm_w=v_final_norm_w)
    names = list(weights)
    upd = {n: _adamw(weights[n], grads[n], m_in[n], v_in[n], "adamw_" + n) for n in names}
    return (loss, grad_x, *[grads[n] for n in names], *[upd[n][0] for n in names], *[upd[n][1] for n in names],
            *[upd[n][2] for n in names])
```

```python
import functools
import math

import jax
import jax.numpy as jnp
import numpy as np
from jax import lax
from jax.experimental import pallas as pl
from jax.experimental.pallas import tpu as pltpu

F32 = jnp.float32
BF16 = jnp.bfloat16
HIGHEST = lax.Precision.HIGHEST
MESH = pl.DeviceIdType.MESH

CHUNK = 64
NORM_EPS = 1e-6
POOL_WINDOWS = (2, 4, 8, 16)
ADAM_LR, ADAM_B1, ADAM_B2, ADAM_EPS, ADAM_WD, ADAM_STEP = 0.001, 0.9, 0.999, 1e-08, 0.01, 10
N_DEV = 8
LANES = 128
VMEM_LIMIT = 48 * 1024 * 1024

NN = (((1,), (0,)), ((), ()))
NT = (((1,), (1,)), ((), ()))
TN = (((0,), (0,)), ((), ()))


def _call(body, **kw):
    return pl.pallas_call(body, **kw)


def _params(sem=None):
    return pltpu.CompilerParams(dimension_semantics=sem, vmem_limit_bytes=VMEM_LIMIT)


def _tile(n, pref, align):
    for d in range(min(pref, n), 0, -1):
        if n % d == 0 and d % align == 0:
            return d
    return n


def _dot(a, b, dims=NN, precision=None):
    return lax.dot_general(a, b, dims, precision=precision, preferred_element_type=F32)


def _sigmoid(x):
    return 0.5 * jnp.tanh(0.5 * x) + 0.5


def _silu(x):
    return x * _sigmoid(x)


def _softplus(x):
    return jnp.maximum(x, 0.0) + jnp.log(1.0 + jnp.exp(-jnp.abs(x)))


def _rmsnorm(x, w):
    return x * lax.rsqrt(jnp.mean(x * x, axis=-1, keepdims=True) + NORM_EPS) * w


def _shift_down(x, j, row):
    if j == 0:
        return x
    return jnp.where(row >= j, pltpu.roll(x, j, 0), 0.0)


def _shift_up(x, j, row):
    if j == 0:
        return x
    n = x.shape[0]
    return jnp.where(row < n - j, pltpu.roll(x, n - j, 0), 0.0)


def _matmul(a, b, dims, out_dtype, tm, tn, tk, name, col_blocks=None, after=None):
    ta = dims == TN
    tb = dims == NT
    m, kdim = (a.shape[1], a.shape[0]) if ta else a.shape
    n = b.shape[0] if tb else b.shape[1]
    if col_blocks:
        tn = n // col_blocks
    tm, tn, tk = _tile(m, tm, 8), _tile(n, tn, LANES), _tile(kdim, tk, LANES if not ta else 16)
    nk = kdim // tk

    n_extra = 0 if after is None else 1

    def body(a_ref, b_ref, *refs):
        o_ref, scratch = refs[n_extra], refs[n_extra + 1:]
        part = _dot(a_ref[...].astype(BF16), b_ref[...].astype(BF16), dims)
        if nk == 1:
            o_ref[...] = part.astype(o_ref.dtype).reshape(o_ref.shape)
            return
        acc_ref, = scratch
        k = pl.program_id(2)

        @pl.when(k == 0)
        def _():
            acc_ref[...] = part

        @pl.when(k > 0)
        def _():
            acc_ref[...] += part

        @pl.when(k == nk - 1)
        def _():
            o_ref[...] = acc_ref[...].astype(o_ref.dtype).reshape(o_ref.shape)

    a_spec = pl.BlockSpec((tk, tm), lambda i, j, k: (k, i)) if ta else pl.BlockSpec((tm, tk), lambda i, j, k: (i, k))
    b_spec = pl.BlockSpec((tn, tk), lambda i, j, k: (j, k)) if tb else pl.BlockSpec((tk, tn), lambda i, j, k: (k, j))
    if col_blocks:
        out_spec = pl.BlockSpec((1, tm, tn), lambda i, j, k: (j, i, 0))
        out_shape = jax.ShapeDtypeStruct((col_blocks, m, tn), out_dtype)
    else:
        out_spec = pl.BlockSpec((tm, tn), lambda i, j, k: (i, j))
        out_shape = jax.ShapeDtypeStruct((m, n), out_dtype)
    return _call(
        body, name=name, grid=(m // tm, n // tn, nk),
        in_specs=[a_spec, b_spec] + [ANY] * n_extra, out_specs=out_spec, out_shape=out_shape,
        scratch_shapes=[] if nk == 1 else [pltpu.VMEM((tm, tn), F32)],
        compiler_params=_params(("parallel", "parallel", "arbitrary")),
    )(a, b, *([] if after is None else [after]))


def _norm_in_fwd(x2d, meta, w, pad):
    seq, d = x2d.shape
    x0 = pad + meta.shape[0]
    assert x0 % 16 == 0
    lp = x0 + seq
    tr = _tile(seq, 512, 16)
    vec = pl.BlockSpec((1, d), lambda i: (0, 0))
    shapes = [jax.ShapeDtypeStruct((lp, d), F32), jax.ShapeDtypeStruct((lp, d), BF16)]

    def body(x_ref, w_ref, h_ref, o_ref):
        h_ref[...] = x_ref[...]
        o_ref[...] = _rmsnorm(x_ref[...], w_ref[...]).astype(BF16)

    def head(m_ref, w_ref, h_in_ref, o_in_ref, h_ref, o_ref):
        h = jnp.concatenate([jnp.zeros((pad, d), F32), m_ref[...]], axis=0) if pad else m_ref[...]
        h_ref[...] = h
        o_ref[...] = _rmsnorm(h, w_ref[...]).astype(BF16)

    rows = _rows_after(x0, tr, d)
    h0, xn = _call(
        body, name="norm_in_fwd", grid=(seq // tr,), in_specs=[pl.BlockSpec((tr, d), lambda i: (i, 0)), vec],
        out_specs=[rows, rows], out_shape=shapes, compiler_params=_params(("parallel",)),
    )(x2d, w)
    first = pl.BlockSpec((x0, d), lambda i: (0, 0))
    return _call(
        head, name="norm_in_fwd_head", grid=(1,), in_specs=[pl.BlockSpec(meta.shape, lambda i: (0, 0)), vec, ANY, ANY],
        out_specs=[first, first], out_shape=shapes, input_output_aliases={2: 0, 3: 1}, compiler_params=_params(("arbitrary",)),
    )(meta, w, h0, xn)


def _rows_after(x0, tr, d):
    step = math.gcd(x0, tr)
    return pl.BlockSpec((pl.Element(tr), pl.Element(d)), lambda i: (pl.multiple_of(x0 + tr * i, step), 0))


def _norm_in_bwd(h0, w, dxn, dh1, x0):
    lp, d = h0.shape
    tr = _tile(lp - x0, 512, 8)
    vec = pl.BlockSpec((1, d), lambda i: (0, 0))

    def make(body_rows, first):
        def body(h_ref, w_ref, da_ref, dh1_ref, dh_ref, dw_ref):
            _, vjp = jax.vjp(_rmsnorm, h_ref[...], w_ref[...])
            dh, dw = vjp(da_ref[...])
            dh_ref[...] = dh + dh1_ref[...]

            @pl.when(pl.program_id(0) == 0)
            def _():
                dw_ref[...] = jnp.zeros_like(dw_ref)

            dw_ref[...] += dw

        rows_in = pl.BlockSpec((x0, d), lambda i: (0, 0)) if first else _rows_after(x0, tr, d)
        return _call(
            body, name="norm_in_bwd_head" if first else "norm_in_bwd", grid=(1 if first else (lp - x0) // tr,),
            in_specs=[rows_in, vec, rows_in, rows_in],
            out_specs=[pl.BlockSpec((body_rows, d), lambda i: (i, 0)), vec],
            out_shape=[jax.ShapeDtypeStruct((x0 if first else lp - x0, d), F32), jax.ShapeDtypeStruct((1, d), F32)],
            compiler_params=_params(("arbitrary",)),
        )(h0, w, dxn, dh1)

    d_head, dw_head = make(x0, True)
    grad_x, dw_rest = make(tr, False)
    return d_head, grad_x, dw_head + dw_rest


def _final_loss(h0, mo, fw, tgt, x0):
    lp, d = h0.shape
    tr = _tile(lp - x0, 512, 8)

    def body(h_ref, mo_ref, fw_ref, t_ref, dh_ref, dw_ref, loss_ref):
        tgt_v = t_ref[...]

        def loss_fn(h1, w):
            err = _rmsnorm(h1, w) - tgt_v
            return 0.5 * jnp.sum(jnp.mean(err * err, axis=-1, keepdims=True), axis=0, keepdims=True)

        loss, vjp = jax.vjp(loss_fn, h_ref[...] + mo_ref[...], fw_ref[...])
        dh, dw = vjp(jnp.ones((1, 1), F32))
        dh_ref[...] = dh

        @pl.when(pl.program_id(0) == 0)
        def _():
            dw_ref[...] = jnp.zeros_like(dw_ref)
            loss_ref[...] = jnp.zeros_like(loss_ref)

        dw_ref[...] += dw
        loss_ref[...] += jnp.broadcast_to(loss, loss_ref.shape)

    def zero_head(dh_in_ref, dh_ref):
        dh_ref[...] = jnp.zeros_like(dh_ref)

    rows = _rows_after(x0, tr, d)
    vec = pl.BlockSpec((1, d), lambda i: (0, 0))
    dh1, dw, loss = _call(
        body, name="final_loss", grid=((lp - x0) // tr,),
        in_specs=[rows, rows, vec, pl.BlockSpec((tr, d), lambda i: (i, 0))],
        out_specs=[rows, vec, pl.BlockSpec((8, LANES), lambda i: (0, 0))],
        out_shape=[jax.ShapeDtypeStruct((lp, d), F32), jax.ShapeDtypeStruct((1, d), F32), jax.ShapeDtypeStruct((8, LANES), F32)],
        compiler_params=_params(("arbitrary",)),
    )(h0, mo, fw, tgt)
    dh1 = _call(
        zero_head, name="final_loss_head", grid=(1,), in_specs=[ANY], out_specs=pl.BlockSpec((x0, d), lambda i: (0, 0)),
        out_shape=jax.ShapeDtypeStruct((lp, d), F32), input_output_aliases={0: 0}, compiler_params=_params(("arbitrary",)),
    )(dh1)
    return dh1, dw, loss


def _pool_select(parts, g):
    out = parts[-1]
    for gi in range(len(parts) - 2, -1, -1):
        out = jnp.where(g == gi, parts[gi], out)
    return out


def _pool_count(row, g, pad):
    win = _pool_select([jnp.full(row.shape, float(w), F32) for w in POOL_WINDOWS], g)
    return jnp.maximum(jnp.minimum((row - pad + 1).astype(F32), win), 1.0)


def _pooled(u, g, row, pad):
    sums, s, span = [], u, 1
    for w in POOL_WINDOWS:
        while span < w:
            s = s + _shift_down(s, span, row)
            span *= 2
        sums.append(s)
    return _pool_select(sums, g) / _pool_count(row, g, pad) - u


def _pooled_adjoint(dp, g, row, pad):
    e = dp / _pool_count(row, g, pad)
    sums, s, span = [], e, 1
    for w in POOL_WINDOWS:
        while span < w:
            s = s + _shift_up(s, span, row)
            span *= 2
        sums.append(s)
    return _pool_select(sums, g) - dp


def _pool_specs(lp, pg, ng, z_off):
    u_spec = pl.BlockSpec((lp, pg), lambda g: (0, g))
    z_spec = pl.BlockSpec((lp, pg), lambda g: (0, z_off + g))
    mix_spec = pl.BlockSpec((1, pg, pg), lambda g: (g, 0, 0))
    vec_spec = pl.BlockSpec((1, pg), lambda g: (0, g))
    return u_spec, z_spec, mix_spec, vec_spec


def _pool_fwd(proj, mix, scale, pad):
    lp = proj.shape[0]
    ng, pg, _ = mix.shape
    pw = ng * pg

    def body(u_ref, z_ref, mix_ref, sc_ref, y_ref):
        g = pl.program_id(0)
        row = lax.broadcasted_iota(jnp.int32, (lp, 1), 0)
        pooled = _pooled(u_ref[...], g, row, pad)
        mixed = _dot(pooled.astype(BF16), mix_ref[0])
        y_ref[...] = (mixed * sc_ref[...] * _silu(z_ref[...])).astype(BF16)

    u_spec, z_spec, mix_spec, vec_spec = _pool_specs(lp, pg, ng, pw // pg)
    return _call(
        body, name="pool_fwd", grid=(ng,), in_specs=[u_spec, z_spec, mix_spec, vec_spec], out_specs=u_spec,
        out_shape=jax.ShapeDtypeStruct((lp, pw), BF16), compiler_params=_params(("parallel",)),
    )(proj, proj, mix, scale)


def _pool_bwd(proj, mix, scale, dy, pad):
    lp = proj.shape[0]
    ng, pg, _ = mix.shape
    pw = ng * pg

    def body(u_ref, z_ref, mix_ref, sc_ref, dy_ref, du_ref, dz_ref, dmix_ref, dsc_ref):
        g = pl.program_id(0)
        row = lax.broadcasted_iota(jnp.int32, (lp, 1), 0)
        real = row >= pad
        z = z_ref[...]
        pooled = _pooled(u_ref[...], g, row, pad).astype(BF16)
        mixed = _dot(pooled, mix_ref[0])
        sig = _sigmoid(z)
        sz = z * sig
        dyv = dy_ref[...]
        dsc_ref[...] = jnp.sum(dyv * mixed * sz, axis=0, keepdims=True)
        d_sz = dyv * mixed * sc_ref[...]
        dz_ref[...] = jnp.where(real, d_sz * (sig + sz * (1.0 - sig)), 0.0).astype(BF16)
        d_mixed = (dyv * sc_ref[...] * sz).astype(BF16)
        dmix_ref[0] = _dot(pooled, d_mixed, TN)
        d_pooled = _dot(d_mixed, mix_ref[0], NT)
        du_ref[...] = jnp.where(real, _pooled_adjoint(d_pooled, g, row, pad), 0.0).astype(BF16)

    u_spec, z_spec, mix_spec, vec_spec = _pool_specs(lp, pg, ng, pw // pg)
    return _call(
        body, name="pool_bwd", grid=(ng,),
        in_specs=[u_spec, z_spec, mix_spec, vec_spec, u_spec], out_specs=[u_spec, u_spec, mix_spec, vec_spec],
        out_shape=[jax.ShapeDtypeStruct((lp, pw), BF16), jax.ShapeDtypeStruct((lp, pw), BF16),
                   jax.ShapeDtypeStruct((ng, pg, pg), F32), jax.ShapeDtypeStruct((1, pw), F32)],
        compiler_params=_params(("parallel",)),
    )(proj, proj, mix, scale, dy)


def _conv_pre(x, w, row):
    kw = w.shape[0]
    y = w[kw - 1:kw, :] * x
    for kk in range(kw - 1):
        y = y + w[kk:kk + 1, :] * _shift_down(x, kw - 1 - kk, row)
    return y


def _conv_post(y, out_scale):
    s = _silu(y)
    if out_scale is None:
        return s
    return s * lax.rsqrt(jnp.sum(s * s, axis=-1, keepdims=True) + NORM_EPS) * out_scale


def _conv_fwd(proj, col_off, w, hd, out_scale, name):
    lp = proj.shape[0]
    kw, width = w.shape
    blk0 = col_off // hd

    def body(x_ref, w_ref, o_ref):
        row = lax.broadcasted_iota(jnp.int32, (lp, 1), 0)
        o_ref[...] = _conv_post(_conv_pre(x_ref[...], w_ref[...], row), out_scale)

    return _call(
        body, name=name, grid=(width // hd,),
        in_specs=[pl.BlockSpec((lp, hd), lambda j: (0, blk0 + j)), pl.BlockSpec((kw, hd), lambda j: (0, j))],
        out_specs=pl.BlockSpec((lp, hd), lambda j: (0, j)),
        out_shape=jax.ShapeDtypeStruct((lp, width), F32), compiler_params=_params(("parallel",)),
    )(proj, w)


def _conv_bwd(proj, col_off, w, d_out, hd, out_scale, pad, name, dst):
    lp = proj.shape[0]
    kw, width = w.shape
    blk0 = col_off // hd

    def body(x_ref, w_ref, do_ref, dst_ref, dx_ref, dw_ref):
        row = lax.broadcasted_iota(jnp.int32, (lp, 1), 0)
        real = row >= pad
        x, wv = x_ref[...], w_ref[...]
        _, vjp = jax.vjp(functools.partial(_conv_post, out_scale=out_scale), _conv_pre(x, wv, row))
        dy = jnp.where(real, vjp(do_ref[...])[0], 0.0)
        dx = wv[kw - 1:kw, :] * dy
        dw_ref[kw - 1:kw, :] = jnp.sum(dy * x, axis=0, keepdims=True)
        for kk in range(kw - 1):
            ahead = _shift_up(dy, kw - 1 - kk, row)
            dx = dx + wv[kk:kk + 1, :] * ahead
            dw_ref[kk:kk + 1, :] = jnp.sum(ahead * x, axis=0, keepdims=True)
        dx_ref[...] = jnp.where(real, dx, 0.0).astype(BF16)

    col = pl.BlockSpec((lp, hd), lambda j: (0, j))
    at_off = pl.BlockSpec((lp, hd), lambda j: (0, blk0 + j))
    wspec = pl.BlockSpec((kw, hd), lambda j: (0, j))
    return _call(
        body, name=name, grid=(width // hd,),
        in_specs=[at_off, wspec, col, ANY], out_specs=[at_off, wspec],
        out_shape=[jax.ShapeDtypeStruct(dst.shape, BF16), jax.ShapeDtypeStruct((kw, width), F32)],
        input_output_aliases={3: 0}, compiler_params=_params(("parallel",)),
    )(proj, w, d_out, dst)


HEADS_PER_STEP = 16
HEADS_PER_STEP_BWD = 8


def _each(fn, *lists):
    return [fn(*args) for args in zip(*lists)]


def _dot3_each(a_list, b_list, dims=NN):
    hi = lambda t: t.astype(BF16)
    lo = lambda t, t_hi: (t - t_hi.astype(F32)).astype(BF16)
    dot = lambda x, y: _dot(x, y, dims)
    a_hi, b_hi = _each(hi, a_list), _each(hi, b_list)
    a_lo, b_lo = _each(lo, a_list, a_hi), _each(lo, b_list, b_hi)
    hh, hl, lh = _each(dot, a_hi, b_hi), _each(dot, a_hi, b_lo), _each(dot, a_lo, b_hi)
    return _each(lambda x, y, w: x + (y + w), hh, hl, lh)


@jax.custom_vjp
def _unit_lower_inverse(lmats):
    c = lmats[0].shape[0]
    eye = lax.broadcasted_iota(jnp.int32, (c, c), 0) == lax.broadcasted_iota(jnp.int32, (c, c), 1)
    a = [-m for m in lmats]
    tmat = [jnp.where(eye, 1.0, 0.0).astype(F32) + m for m in a]
    span = 2
    while span < c:
        a = _dot3_each(a, a)
        tmat = _each(lambda t, u: t + u, tmat, _dot3_each(tmat, a))
        span *= 2
    return tuple(tmat)


def _unit_lower_inverse_fwd(lmats):
    tmats = _unit_lower_inverse(lmats)
    return tmats, tmats


def _unit_lower_inverse_bwd(tmats, cts):
    left = _each(lambda t, ct: _dot(t, ct, TN, HIGHEST), tmats, cts)
    return (tuple(_each(lambda m, t: -_dot(m, t, NT, HIGHEST), left, tmats)),)


_unit_lower_inverse.defvjp(_unit_lower_inverse_fwd, _unit_lower_inverse_bwd)


@jax.custom_vjp
def _known_inverse(lmats, tmats):
    return tmats


def _known_inverse_fwd(lmats, tmats):
    return tmats, tmats


def _known_inverse_bwd(tmats, cts):
    return _unit_lower_inverse_bwd(tmats, cts)[0], tuple(jnp.zeros_like(t) for t in tmats)


_known_inverse.defvjp(_known_inverse_fwd, _known_inverse_bwd)


def _chunk_math(states, q, k, v, ba, z, prm, nw, head0, rowmask, n_heads, tmats=None, keep_tmats=False):
    c = q.shape[0]
    heads = list(range(len(states)))
    hd = q.shape[1] // len(states)
    lane = lax.broadcasted_iota(jnp.int32, ba.shape, 1)
    sub = lax.broadcasted_iota(jnp.int32, (ba.shape[1], c), 0)
    ri = lax.broadcasted_iota(jnp.int32, (c, c), 0)
    ci = lax.broadcasted_iota(jnp.int32, (c, c), 1)
    last = lax.broadcasted_iota(jnp.int32, (c, 1), 0) == c - 1
    causal, strict = ri >= ci, ri > ci
    beta_all = _sigmoid(ba) * rowmask
    g_all = -jnp.exp(prm[0:1, :]) * _softplus(ba + prm[1:2, :]) * rowmask
    gcum_all = _dot(jnp.where(causal, 1.0, 0.0).astype(F32), g_all, precision=HIGHEST)
    gcum_t = gcum_all.T
    split = lambda t: [t[:, j * hd:(j + 1) * hd] for j in heads]
    qs, ks, vs, zs = split(q), split(k), split(v), split(z)
    beta = [jnp.sum(jnp.where(lane == head0 + j, beta_all, 0.0), axis=1, keepdims=True) for j in heads]
    gcum = [jnp.sum(jnp.where(lane == n_heads + head0 + j, gcum_all, 0.0), axis=1, keepdims=True) for j in heads]
    grow = [jnp.sum(jnp.where(sub == n_heads + head0 + j, gcum_t, 0.0), axis=0, keepdims=True) for j in heads]
    glast = _each(lambda gc: jnp.sum(jnp.where(last, gc, 0.0), axis=0, keepdims=True), gcum)
    decay = _each(lambda gc, gr: jnp.where(causal, jnp.exp(jnp.where(causal, gc - gr, 0.0)), 0.0), gcum, grow)
    eg = _each(jnp.exp, gcum)
    k_beta = _each(jnp.multiply, ks, beta)
    kk = _each(lambda a, b: _dot(a, b, NT), k_beta, ks)
    lmats = tuple(_each(lambda m, dc: jnp.where(strict, m * dc, 0.0), kk, decay))
    tmat = list(_unit_lower_inverse(lmats) if tmats is None else _known_inverse(lmats, tuple(tmats)))
    u_c = _each(_dot, tmat, _each(jnp.multiply, vs, beta))
    w_c = _each(_dot, tmat, _each(jnp.multiply, k_beta, eg))
    qk = _each(lambda a, b, dc: jnp.where(causal, _dot(a, b, NT) * dc, 0.0), qs, ks, decay)
    v_new = _each(lambda u, w, s: u - _dot(w, s), u_c, w_c, list(states))
    o = _each(lambda a, e, s, m, vn: _dot(a * e, s) + _dot(m, vn), qs, eg, list(states), qk, v_new)
    k_dec = _each(lambda a, gl, gc: a * jnp.exp(gl - gc), ks, glast, gcum)
    new_states = _each(lambda s, gl, kd, vn: s * jnp.exp(gl) + _dot(kd, vn, TN), list(states), glast, k_dec, v_new)
    ys = _each(lambda oj, zj: _rmsnorm(oj, nw) * _silu(zj), o, zs)
    if keep_tmats:
        return jnp.concatenate(ys, axis=1), tuple(new_states), tuple(tmat)
    return jnp.concatenate(ys, axis=1), tuple(new_states)


def _chunk_specs(nc, hd, n_heads, z_off, ba_off, rev):
    cidx = (lambda c: nc - 1 - c) if rev else (lambda c: c)
    hb = min(HEADS_PER_STEP_BWD if rev else HEADS_PER_STEP, n_heads)
    assert n_heads % hb == 0 and z_off % (hb * hd) == 0 and ba_off % LANES == 0
    blk = lambda off: pl.BlockSpec((CHUNK, hb * hd), lambda c, g: (cidx(c), off + g))
    ba_spec = lambda off: pl.BlockSpec((CHUNK, LANES), lambda c, g: (cidx(c), off // LANES))
    prm_spec = pl.BlockSpec((8, LANES), lambda c, g: (0, 0))
    nw_spec = pl.BlockSpec((1, hd), lambda c, g: (0, 0))
    st_spec = pl.BlockSpec((1, hb, hd, hd), lambda c, g: (cidx(c), g, 0, 0))
    return blk, ba_spec, prm_spec, nw_spec, st_spec, blk(z_off // (hb * hd))


def _rowmask(chunk_idx, pad):
    row = chunk_idx * CHUNK + lax.broadcasted_iota(jnp.int32, (CHUNK, 1), 0)
    return jnp.where(row >= pad, 1.0, 0.0).astype(F32)


def _chunk_fwd(qn, kn, vv, proj, z_off, ba_off, prm, nw, n_heads, pad):
    lp, dn = qn.shape
    hd = dn // n_heads
    nc = lp // CHUNK
    hb = min(HEADS_PER_STEP, n_heads)

    def body(q_ref, k_ref, v_ref, ba_ref, z_ref, prm_ref, nw_ref, y_ref, hist_ref, tm_ref, st_ref):
        c, g = pl.program_id(0), pl.program_id(1)

        @pl.when(c == 0)
        def _():
            for j in range(hb):
                st_ref[g * hb + j] = jnp.zeros((hd, hd), F32)

        states = tuple(st_ref[g * hb + j] for j in range(hb))
        for j in range(hb):
            hist_ref[0, j] = states[j]
        y, new_states, tmats = _chunk_math(states, q_ref[...], k_ref[...], v_ref[...], ba_ref[...], z_ref[...], prm_ref[...],
                                           nw_ref[...], g * hb, _rowmask(c, pad), n_heads, keep_tmats=True)
        y_ref[...] = y.astype(BF16)
        for j in range(hb):
            st_ref[g * hb + j] = new_states[j]
            tm_ref[0, j] = tmats[j]

    blk, ba_spec, prm_spec, nw_spec, st_spec, z_spec = _chunk_specs(nc, hd, n_heads, z_off, ba_off, False)
    tm_spec = pl.BlockSpec((1, hb, CHUNK, CHUNK), lambda c, g: (c, g, 0, 0))
    return _call(
        body, name="chunk_fwd", grid=(nc, n_heads // hb),
        in_specs=[blk(0), blk(0), blk(0), ba_spec(ba_off), z_spec, prm_spec, nw_spec], out_specs=[blk(0), st_spec, tm_spec],
        out_shape=[jax.ShapeDtypeStruct((lp, dn), BF16), jax.ShapeDtypeStruct((nc, n_heads, hd, hd), F32),
                   jax.ShapeDtypeStruct((nc, n_heads, CHUNK, CHUNK), F32)],
        scratch_shapes=[pltpu.VMEM((n_heads, hd, hd), F32)],
        compiler_params=_params(("arbitrary", "arbitrary")),
    )(qn, kn, vv, proj, proj, prm, nw)


def _chunk_bwd(qn, kn, vv, proj, z_off, ba_off, prm, nw, hist, tmats, dy, n_heads, pad, d_proj):
    lp, dn = qn.shape
    hd = dn // n_heads
    nc = lp // CHUNK
    hb = min(HEADS_PER_STEP_BWD, n_heads)

    def body(q_ref, k_ref, v_ref, ba_ref, z_ref, prm_ref, nw_ref, hist_ref, tm_ref, dy_ref, d_proj_ref,
             dq_ref, dk_ref, dv_ref, dba_ref, dz_ref, dprm_ref, dnw_ref, dst_ref):
        step, g = pl.program_id(0), pl.program_id(1)

        @pl.when(step == 0)
        def _():
            for j in range(hb):
                dst_ref[g * hb + j] = jnp.zeros((hd, hd), F32)

        @pl.when((step == 0) & (g == 0))
        def _():
            dprm_ref[...] = jnp.zeros_like(dprm_ref)
            dnw_ref[...] = jnp.zeros_like(dnw_ref)

        @pl.when(g == 0)
        def _():
            dba_ref[...] = jnp.zeros_like(dba_ref)

        def fn(states, q, k, v, ba, z, prm_v, nw_v, known):
            return _chunk_math(states, q, k, v, ba, z, prm_v, nw_v, g * hb, _rowmask(nc - 1 - step, pad), n_heads, tmats=known)

        states = tuple(hist_ref[0, j] for j in range(hb))
        known = tuple(tm_ref[0, j] for j in range(hb))
        _, vjp = jax.vjp(fn, states, q_ref[...], k_ref[...], v_ref[...], ba_ref[...], z_ref[...], prm_ref[...], nw_ref[...], known)
        dst, dq, dk, dv, dba, dz, dprm, dnw, _ = vjp((dy_ref[...], tuple(dst_ref[g * hb + j] for j in range(hb))))
        for j in range(hb):
            dst_ref[g * hb + j] = dst[j]
        dq_ref[...] = dq
        dk_ref[...] = dk
        dv_ref[...] = dv
        dz_ref[...] = dz.astype(BF16)
        dba_ref[...] += dba
        dprm_ref[...] += dprm
        dnw_ref[...] += dnw

    blk, ba_spec, prm_spec, nw_spec, st_spec, z_spec = _chunk_specs(nc, hd, n_heads, z_off, ba_off, True)
    f32_full = jax.ShapeDtypeStruct((lp, dn), F32)
    tm_spec = pl.BlockSpec((1, hb, CHUNK, CHUNK), lambda c, g: (nc - 1 - c, g, 0, 0))
    return _call(
        body, name="chunk_bwd", grid=(nc, n_heads // hb),
        in_specs=[blk(0), blk(0), blk(0), ba_spec(ba_off), z_spec, prm_spec, nw_spec, st_spec, tm_spec, blk(0), ANY],
        out_specs=[blk(0), blk(0), blk(0), ba_spec(0), z_spec, prm_spec, nw_spec],
        out_shape=[f32_full, f32_full, f32_full, jax.ShapeDtypeStruct((lp, LANES), F32), jax.ShapeDtypeStruct(d_proj.shape, BF16),
                   jax.ShapeDtypeStruct((8, LANES), F32), jax.ShapeDtypeStruct((1, hd), F32)],
        scratch_shapes=[pltpu.VMEM((n_heads, hd, hd), F32)],
        input_output_aliases={10: 4}, compiler_params=_params(("arbitrary", "arbitrary")),
    )(qn, kn, vv, proj, proj, prm, nw, hist, tmats, dy, d_proj)


def _merge_math(p, q, gp, gd):
    return _sigmoid(gp) * p + _sigmoid(gd) * q


def _merge_specs(lp, d, gp_off, gd_off):
    tr, tc = _tile(lp, 264, 16), _tile(d, 1024, LANES)
    blk = pl.BlockSpec((tr, tc), lambda i, j: (i, j))
    gp_spec = pl.BlockSpec((tr, tc), lambda i, j: (i, gp_off // tc + j))
    gd_spec = pl.BlockSpec((tr, tc), lambda i, j: (i, gd_off // tc + j))
    return (lp // tr, d // tc), blk, gp_spec, gd_spec


def _merge_fwd(p, q, proj, gp_off, gd_off):
    lp, d = p.shape
    grid, blk, gp_spec, gd_spec = _merge_specs(lp, d, gp_off, gd_off)

    def body(p_ref, q_ref, gp_ref, gd_ref, o_ref):
        o_ref[...] = _merge_math(p_ref[...], q_ref[...], gp_ref[...], gd_ref[...]).astype(BF16)

    return _call(
        body, name="merge_fwd", grid=grid, in_specs=[blk, blk, gp_spec, gd_spec], out_specs=blk,
        out_shape=jax.ShapeDtypeStruct((lp, d), BF16), compiler_params=_params(("parallel", "parallel")),
    )(p, q, proj, proj)


def _merge_bwd(p, q, proj, gp_off, gd_off, dm):
    lp, d = p.shape
    grid, blk, gp_spec, gd_spec = _merge_specs(lp, d, gp_off, gd_off)

    def body(p_ref, q_ref, gp_ref, gd_ref, dm_ref, dp_ref, dq_ref, dgp_ref, dgd_ref):
        _, vjp = jax.vjp(_merge_math, p_ref[...], q_ref[...], gp_ref[...], gd_ref[...])
        for ref, val in zip((dp_ref, dq_ref, dgp_ref, dgd_ref), vjp(dm_ref[...])):
            ref[...] = val.astype(BF16)

    out = jax.ShapeDtypeStruct((lp, d), BF16)
    return _call(
        body, name="merge_bwd", grid=grid, in_specs=[blk, blk, gp_spec, gd_spec, blk], out_specs=[blk] * 4,
        out_shape=[out] * 4, compiler_params=_params(("parallel", "parallel")),
    )(p, q, proj, proj, dm)


def _adamw(w, g, m, v, name):
    shape = w.shape
    w2, g2, m2, v2 = (t.reshape((-1, shape[-1])) for t in (w, g, m, v))
    rows, cols = w2.shape
    tr = _tile(rows, 128, 8)

    def body(w_ref, g_ref, m_ref, v_ref, d_ref, nm_ref, nv_ref):
        gv = g_ref[...]
        nm = ADAM_B1 * m_ref[...] + (1.0 - ADAM_B1) * gv
        nv = ADAM_B2 * v_ref[...] + (1.0 - ADAM_B2) * (gv * gv)
        m_hat = nm / (1.0 - ADAM_B1 ** ADAM_STEP)
        v_hat = nv / (1.0 - ADAM_B2 ** ADAM_STEP)
        d_ref[...] = -ADAM_LR * (m_hat / (jnp.sqrt(v_hat) + ADAM_EPS) + ADAM_WD * w_ref[...])
        nm_ref[...] = nm
        nv_ref[...] = nv

    blk = pl.BlockSpec((tr, cols), lambda i: (i, 0))
    out = jax.ShapeDtypeStruct((rows, cols), F32)
    res = _call(
        body, name=name, grid=(rows // tr,), in_specs=[blk] * 4, out_specs=[blk] * 3, out_shape=[out] * 3,
        compiler_params=_params(("parallel",)),
    )(w2, g2, m2, v2)
    return tuple(t.reshape(shape) for t in res)


def _coords():
    return lax.axis_index("x"), lax.axis_index("y"), lax.axis_index("c")


def _flip(v, bit):
    return 1 - v if bit else v


CHIP_FLIPS = ((1, 0), (0, 1), (1, 1))
ANY = pl.BlockSpec(memory_space=pl.ANY)


def _all_gather(shards):
    n = len(shards)

    def body(*refs):
        x_refs, out_refs = refs[:n], refs[n:2 * n]
        send_sems, recv_sems, local_sems = refs[2 * n:]
        x, y, c = _coords()
        sibling = (x, y, 1 - c)
        chips = [(_flip(x, fx), _flip(y, fy)) for fx, fy in CHIP_FLIPS]

        def copy(a, k, block, to, from_input=False):
            px, py, pc = block
            slot = out_refs[a].at[4 * px + 2 * py + pc]
            return pltpu.make_async_remote_copy(
                src_ref=x_refs[a] if from_input else slot, dst_ref=slot,
                send_sem=send_sems.at[7 * a + k], recv_sem=recv_sems.at[7 * a + k], device_id=to, device_id_type=MESH)

        mine = [pltpu.make_async_copy(x_refs[a], out_refs[a].at[4 * x + 2 * y + c], local_sems.at[a]) for a in range(n)]
        first = []
        for a in range(n):
            mine[a].start()
            first.append(copy(a, 0, (x, y, c), sibling, True))
            first += [copy(a, 1 + j, (x, y, c), (*chip, c), True) for j, chip in enumerate(chips)]
        for cp in first:
            cp.start()
        passed = []
        for j, chip in enumerate(chips):
            for a in range(n):
                copy(a, 1 + j, (*chip, c), (x, y, c)).wait_recv()
                passed.append(copy(a, 4 + j, (*chip, c), sibling))
                passed[-1].start()
        for a in range(n):
            copy(a, 0, (x, y, 1 - c), (x, y, c)).wait_recv()
            for j, chip in enumerate(chips):
                copy(a, 4 + j, (*chip, 1 - c), (x, y, c)).wait_recv()
        for cp in first + passed:
            cp.wait_send()
        for cp in mine:
            cp.wait()

    return _call(
        body, name="all_gather", in_specs=[ANY] * n, out_specs=[ANY] * n,
        out_shape=[jax.ShapeDtypeStruct((N_DEV,) + s.shape, s.dtype) for s in shards],
        scratch_shapes=[pltpu.SemaphoreType.DMA((7 * n,)), pltpu.SemaphoreType.DMA((7 * n,)), pltpu.SemaphoreType.DMA((n,))],
    )(*shards)


def _all_gather_tree(shard, after):
    rows, cols = shard.shape
    half = rows // 2
    assert rows % 32 == 0

    def body(x_ref, after_ref, out_ref, send_sems, recv_sems, local_sem):
        x, y, c = _coords()
        me, sibling = (x, y, c), (x, y, 1 - c)
        x_nbr, y_nbr, diag = (1 - x, y), (x, 1 - y), (1 - x, 1 - y)

        def part(ref, h):
            return ref if h is None else ref.at[pl.ds(h * half, half)]

        def copy(k, block, to, h=None, from_input=False):
            px, py, pc = block
            slot = part(out_ref.at[4 * px + 2 * py + pc], h)
            return pltpu.make_async_remote_copy(
                src_ref=part(x_ref, h) if from_input else slot, dst_ref=slot,
                send_sem=send_sems.at[k], recv_sem=recv_sems.at[k], device_id=to, device_id_type=MESH)

        mine = pltpu.make_async_copy(x_ref, out_ref.at[4 * x + 2 * y + c], local_sem)
        mine.start()
        started = [copy(0, me, sibling, None, True),
                   copy(1, me, (*x_nbr, c), 0, True), copy(2, me, (*x_nbr, c), 1, True),
                   copy(4, me, (*y_nbr, c), 1, True), copy(3, me, (*y_nbr, c), 0, True)]
        for cp in started:
            cp.start()
        copy(1, (*x_nbr, c), me, 0).wait_recv()
        started.append(copy(5, (*x_nbr, c), (*y_nbr, c), 0))
        started[-1].start()
        copy(4, (*y_nbr, c), me, 1).wait_recv()
        started.append(copy(6, (*y_nbr, c), (*x_nbr, c), 1))
        started[-1].start()
        copy(2, (*x_nbr, c), me, 1).wait_recv()
        started.append(copy(7, (*x_nbr, c), sibling))
        started[-1].start()
        copy(3, (*y_nbr, c), me, 0).wait_recv()
        started.append(copy(8, (*y_nbr, c), sibling))
        started[-1].start()
        copy(5, (*diag, c), me, 0).wait_recv()
        copy(6, (*diag, c), me, 1).wait_recv()
        started.append(copy(9, (*diag, c), sibling))
        started[-1].start()
        copy(0, sibling, me).wait_recv()
        for k, chip in ((7, x_nbr), (8, y_nbr), (9, diag)):
            copy(k, (*chip, 1 - c), me).wait_recv()
        for cp in started:
            cp.wait_send()
        mine.wait()

    return _call(
        body, name="all_gather_tree", in_specs=[ANY, ANY], out_specs=ANY,
        out_shape=jax.ShapeDtypeStruct((N_DEV, rows, cols), shard.dtype),
        scratch_shapes=[pltpu.SemaphoreType.DMA((10,)), pltpu.SemaphoreType.DMA((10,)), pltpu.SemaphoreType.DMA],
    )(shard, after)


def _rs_to_sibling(gs, name):
    n = len(gs)

    def body(*refs):
        g_refs, got_refs = refs[:n], refs[n:2 * n]
        send_sems, recv_sems = refs[2 * n:]
        x, y, c = _coords()
        copies = []
        for a in range(n):
            for p in range(4):
                cp = pltpu.make_async_remote_copy(
                    src_ref=g_refs[a].at[2 * p + (1 - c)], dst_ref=got_refs[a].at[p], send_sem=send_sems.at[4 * a + p],
                    recv_sem=recv_sems.at[4 * a + p], device_id=(x, y, 1 - c), device_id_type=MESH)
                cp.start()
                copies.append(cp)
        for cp in copies:
            cp.wait()

    return _call(
        body, name=name, in_specs=[ANY] * n, out_specs=[ANY] * n,
        out_shape=[jax.ShapeDtypeStruct((4,) + g.shape[1:], g.dtype) for g in gs],
        scratch_shapes=[pltpu.SemaphoreType.DMA((4 * n,)), pltpu.SemaphoreType.DMA((4 * n,))],
    )(*gs)


def _rs_pair_sum(g, got, c_idx, name):
    _, rows, cols = g.shape
    tr = _tile(rows, 256, 16)

    def body(c_ref, g_ref, got_ref, o_ref):
        o_ref[...] = (g_ref[...].astype(F32) + got_ref[...].astype(F32)).astype(o_ref.dtype)

    grid_spec = pltpu.PrefetchScalarGridSpec(
        num_scalar_prefetch=1, grid=(4, rows // tr),
        in_specs=[pl.BlockSpec((1, tr, cols), lambda p, i, c_ref: (2 * p + c_ref[0], i, 0)),
                  pl.BlockSpec((1, tr, cols), lambda p, i, c_ref: (p, i, 0))],
        out_specs=pl.BlockSpec((1, tr, cols), lambda p, i, c_ref: (p, i, 0)))
    return _call(
        body, name=name, grid_spec=grid_spec, out_shape=jax.ShapeDtypeStruct((4, rows, cols), g.dtype),
        compiler_params=_params(("parallel", "parallel")),
    )(c_idx, g, got)


def _to_chips_copies(p_refs, got_refs, send_sems, recv_sems):
    x, y, c = _coords()
    copies = []
    for a in range(len(p_refs)):
        for k, (fx, fy) in enumerate(CHIP_FLIPS):
            px, py = _flip(x, fx), _flip(y, fy)
            copies.append(pltpu.make_async_remote_copy(
                src_ref=p_refs[a].at[2 * px + py], dst_ref=got_refs[a].at[k], send_sem=send_sems.at[3 * a + k],
                recv_sem=recv_sems.at[3 * a + k], device_id=(px, py, c), device_id_type=MESH))
    return copies


def _rs_to_chips(partials, name):
    n = len(partials)

    def body(*refs):
        copies = _to_chips_copies(refs[:n], refs[n:2 * n], *refs[2 * n:])
        for cp in copies:
            cp.start()
        for cp in copies:
            cp.wait()

    return _call(
        body, name=name, in_specs=[ANY] * n, out_specs=[ANY] * n,
        out_shape=[jax.ShapeDtypeStruct((3,) + p.shape[1:], p.dtype) for p in partials],
        scratch_shapes=[pltpu.SemaphoreType.DMA((3 * n,)), pltpu.SemaphoreType.DMA((3 * n,))],
    )(*partials)


HBM = pl.BlockSpec(memory_space=pltpu.HBM)
SEM = pl.BlockSpec(memory_space=pltpu.SEMAPHORE)
SIDE_EFFECT = pltpu.CompilerParams(has_side_effects=pltpu.SideEffectType.DATAFLOW_SIDE_EFFECTING)


def _split_start(copies_fn, srcs, land_shapes, n_sems, name, after=None):
    n, m = len(srcs), len(land_shapes)
    extra = [] if after is None else [after]

    def body(*refs):
        outs = refs[n + m + len(extra):]
        send_sems, recv_sems, token = outs[0], outs[1], outs[-1]
        for cp in copies_fn(refs[:n], refs[n:n + m], send_sems, recv_sems):
            cp.start()
        token[...] = jnp.zeros_like(token)

    ins = [pltpu.with_memory_space_constraint(t, pltpu.HBM) for t in list(srcs) + [lax.empty(s.shape, s.dtype) for s in land_shapes]]
    res = _call(
        body, name=name, in_specs=[HBM] * (n + m) + [ANY] * len(extra),
        out_specs=[SEM, SEM] + [HBM] * (n + m) + [pl.BlockSpec(memory_space=pltpu.VMEM)],
        out_shape=[pltpu.SemaphoreType.DMA((n_sems,)), pltpu.SemaphoreType.DMA((n_sems,))]
        + [pltpu.HBM(t.shape, t.dtype) for t in ins] + [jax.ShapeDtypeStruct((8, LANES), F32)],
        input_output_aliases={i: 2 + i for i in range(n + m)}, compiler_params=SIDE_EFFECT,
    )(*ins, *extra)
    return dict(sems=(res[0], res[1]), srcs=res[2:2 + n], lands=res[2 + n:2 + n + m], token=res[-1])


def _split_wait(copies_fn, started, after, name):
    n, m = len(started["srcs"]), len(started["lands"])

    def body(*refs):
        for cp in copies_fn(refs[:n], refs[n:n + m], refs[n + m], refs[n + m + 1]):
            cp.wait_send()
            cp.wait_recv()

    bufs = list(started["srcs"]) + list(started["lands"])
    res = _call(
        body, name=name, in_specs=[HBM] * (n + m) + [SEM, SEM, ANY], out_specs=[HBM] * (n + m),
        out_shape=[pltpu.HBM(t.shape, t.dtype) for t in bufs],
        input_output_aliases={i: i for i in range(n + m)}, compiler_params=SIDE_EFFECT,
    )(*bufs, *started["sems"], after)
    return res[:n], res[n:]


def _to_all_copies(x_refs, out_refs, send_sems, recv_sems):
    x, y, c = _coords()
    copies = []
    for a in range(len(x_refs)):
        for k in range(N_DEV - 1):
            fx, fy, fc = ((k + 1) >> 2) & 1, ((k + 1) >> 1) & 1, (k + 1) & 1
            copies.append(pltpu.make_async_remote_copy(
                src_ref=x_refs[a], dst_ref=out_refs[a].at[4 * x + 2 * y + c], send_sem=send_sems.at[7 * a + k],
                recv_sem=recv_sems.at[7 * a + k], device_id=(_flip(x, fx), _flip(y, fy), _flip(c, fc)), device_id_type=MESH))
    return copies


def _fill_own_block(gathered, shard, me_idx, name):
    rows, cols = shard.shape
    tr = _tile(rows, 512, 16)

    def body(me_ref, g_ref, s_ref, o_ref):
        o_ref[0] = s_ref[...]

    grid_spec = pltpu.PrefetchScalarGridSpec(
        num_scalar_prefetch=1, grid=(rows // tr,),
        in_specs=[ANY, pl.BlockSpec((tr, cols), lambda i, me: (i, 0))],
        out_specs=pl.BlockSpec((1, tr, cols), lambda i, me: (me[0], i, 0)))
    return _call(
        body, name=name, grid_spec=grid_spec, out_shape=jax.ShapeDtypeStruct(gathered.shape, gathered.dtype),
        input_output_aliases={1: 0}, compiler_params=_params(("arbitrary",)),
    )(me_idx, gathered, shard)


def _rs_chip_sum(partial, got, chip_idx, name, part=0, n_parts=1, dst=None):
    _, rows, cols = partial.shape
    tr = _tile(rows, 256, 16)
    steps = rows // tr
    n_dst = 0 if dst is None else 1

    def body(p_idx_ref, p_ref, got_ref, *refs):
        refs[n_dst][...] = ((p_ref[0].astype(F32) + got_ref[0].astype(F32)) + got_ref[1].astype(F32)) + got_ref[2].astype(F32)

    grid_spec = pltpu.PrefetchScalarGridSpec(
        num_scalar_prefetch=1, grid=(steps,),
        in_specs=[pl.BlockSpec((1, tr, cols), lambda i, p_ref: (p_ref[0], i, 0)),
                  pl.BlockSpec((3, tr, cols), lambda i, p_ref: (0, i, 0))] + [ANY] * n_dst,
        out_specs=pl.BlockSpec((tr, cols), lambda i, p_ref: (part * steps + i, 0)))
    return _call(
        body, name=name, grid_spec=grid_spec, out_shape=jax.ShapeDtypeStruct((n_parts * rows, cols), F32),
        input_output_aliases={3: 0} if n_dst else {}, compiler_params=_params(("parallel",)),
    )(chip_idx, partial, got, *([] if dst is None else [dst]))


def _rs_begin(gs, tag, split):
    c_idx = jnp.reshape(lax.axis_index("c"), (1,)).astype(jnp.int32)
    gots = _rs_to_sibling(gs, "rs_to_sibling_" + tag)
    partials = [_rs_pair_sum(g, got, c_idx, "rs_pair_sum_%s%d" % (tag, a)) for a, (g, got) in enumerate(zip(gs, gots))]
    if not split:
        return dict(partials=partials, gots=_rs_to_chips(partials, "rs_to_chips_" + tag))
    lands = [jax.ShapeDtypeStruct((3,) + p.shape[1:], p.dtype) for p in partials]
    return _split_start(_to_chips_copies, partials, lands, 3 * len(partials), "rs_to_chips_start_" + tag)


def _rs_finish(begun, tag, after=None, part=0, n_parts=1, dsts=None):
    x, y, _ = _coords()
    chip_idx = jnp.reshape(2 * x + y, (1,)).astype(jnp.int32)
    if "gots" in begun:
        partials, gots = begun["partials"], begun["gots"]
    else:
        partials, gots = _split_wait(_to_chips_copies, begun, after, "rs_to_chips_wait_" + tag)
    return [_rs_chip_sum(p, got, chip_idx, "rs_chip_sum_%s%d" % (tag, a), part, n_parts, None if dsts is None else dsts[a])
            for a, (p, got) in enumerate(zip(partials, gots))]


RUNS = 3
RUN_FIELDS = 5


def _lane_gather_table(src_of):
    n_blocks = src_of.shape[0] // LANES
    tab = np.zeros((n_blocks + 1, RUNS, RUN_FIELDS), np.int32)
    for t in range(n_blocks):
        runs = []
        for lane in range(LANES):
            slab, col = (int(v) for v in src_of[t * LANES + lane])
            if slab < 0:
                continue
            key = (slab, col // LANES, col % LANES - lane)
            if runs and runs[-1][0] == key and runs[-1][2] == lane:
                runs[-1][2] = lane + 1
            else:
                runs.append([key, lane, lane + 1])
        assert len(runs) <= RUNS
        slots = [None] * RUNS
        for key, lo, hi in sorted(runs, key=lambda r: r[0][:2]):
            e = key[1] % 2 if slots[key[1] % 2] is None else slots.index(None)
            slots[e] = (key[0], key[1], key[2], lo, hi)
        for e in range(RUNS):
            tab[t, e] = slots[e] if slots[e] is not None else (tab[t - 1, e, 0], tab[t - 1, e, 1], 0, 0, 0) if t else tab[t, e]
    tab[n_blocks, :, :2] = tab[n_blocks - 1, :, :2]
    return tab.reshape(-1)


def _place_run(tab_ref, t, e, block, under):
    base = (t * RUNS + e) * RUN_FIELDS
    shift, lo, hi = tab_ref[base + 2], tab_ref[base + 3], tab_ref[base + 4]
    lane = lax.broadcasted_iota(jnp.int32, (1, LANES), 1)
    return jnp.where((lane >= lo) & (lane < hi), pltpu.roll(block.astype(F32), (LANES - shift) % LANES, 1), under)


def _lane_gather_cols(src, table, out_slabs, out_width, name):
    _, rows, _ = src.shape
    blocks_per_slab = -(-out_width // LANES)

    def body(tab_ref, *refs):
        t, o_ref = pl.program_id(0), refs[RUNS]
        o_ref[0] = _place_run(tab_ref, t, 1, refs[1][0], _place_run(tab_ref, t, 0, refs[0][0], 0.0)).astype(BF16)
        last = (t * RUNS + RUNS - 1) * RUN_FIELDS

        @pl.when(tab_ref[last + 4] > tab_ref[last + 3])
        def _():
            o_ref[0] = _place_run(tab_ref, t, RUNS - 1, refs[RUNS - 1][0], o_ref[0].astype(F32)).astype(BF16)

    def src_spec(e):
        return pl.BlockSpec((1, rows, LANES), lambda t, tab: (tab[(t * RUNS + e) * RUN_FIELDS], 0, tab[(t * RUNS + e) * RUN_FIELDS + 1]))

    grid_spec = pltpu.PrefetchScalarGridSpec(
        num_scalar_prefetch=1, grid=(out_slabs * blocks_per_slab,), in_specs=[src_spec(e) for e in range(RUNS)],
        out_specs=pl.BlockSpec((1, rows, LANES), lambda t, tab: (t // blocks_per_slab, 0, t % blocks_per_slab)))
    return _call(
        body, name=name, grid_spec=grid_spec, out_shape=jax.ShapeDtypeStruct((out_slabs, rows, out_width), BF16),
        compiler_params=_params(("arbitrary",)),
    )(jnp.asarray(table), src, src, src)


def _all_reduce_small(vec):
    rows, cols = vec.shape

    def body(v_ref, o_ref, buf, send_sems, recv_sems):
        x, y, c = _coords()
        me = 4 * x + 2 * y + c
        buf[me] = v_ref[...]
        copies = []
        for k in range(N_DEV - 1):
            fx, fy, fc = ((k + 1) >> 2) & 1, ((k + 1) >> 1) & 1, (k + 1) & 1
            cp = pltpu.make_async_remote_copy(
                src_ref=v_ref, dst_ref=buf.at[me], send_sem=send_sems.at[k], recv_sem=recv_sems.at[k],
                device_id=(_flip(x, fx), _flip(y, fy), _flip(c, fc)), device_id_type=MESH)
            cp.start()
            copies.append(cp)
        for cp in copies:
            cp.wait()
        total = buf[0]
        for j in range(1, N_DEV):
            total = total + buf[j]
        o_ref[...] = total

    vmem = pl.BlockSpec(memory_space=pltpu.VMEM)
    return _call(
        body, name="all_reduce_small", in_specs=[vmem], out_specs=vmem,
        out_shape=jax.ShapeDtypeStruct((rows, cols), F32),
        scratch_shapes=[pltpu.VMEM((N_DEV, rows, cols), F32), pltpu.SemaphoreType.DMA((N_DEV - 1,)),
                        pltpu.SemaphoreType.DMA((N_DEV - 1,))],
    )(vec)


def _w_in_column_maps(ns, o_ba, n_logit, n_main, n_all):
    own = np.arange(N_DEV * ns)
    work_of_own = np.where(own < o_ba, own, np.where(own < o_ba + n_logit, n_main + own - o_ba, own - n_logit))
    to_work = np.full((n_all, 2), -1, np.int64)
    to_work[work_of_own, 0] = own // ns
    to_work[work_of_own, 1] = own % ns
    slab_width = -(-ns // LANES) * LANES
    to_own = np.full((N_DEV, slab_width, 2), -1, np.int64)
    to_own[:, :ns, 0] = 0
    to_own[:, :ns, 1] = work_of_own.reshape(N_DEV, ns)
    return to_work, to_own.reshape(-1, 2)


def kernel(x, meta_tokens, norm_w, w_in, conv_w, A_log, dt_bias, pool_mix, pool_scale, dn_norm_w, w_pool_out, w_dn_out, w_o, final_norm_w, loss_target, m_meta_tokens, m_norm_w, m_w_in, m_conv_w, m_A_log, m_dt_bias, m_pool_mix, m_pool_scale, m_dn_norm_w, m_w_pool_out, m_w_dn_out, m_w_o, m_final_norm_w, v_meta_tokens, v_norm_w, v_w_in, v_conv_w, v_A_log, v_dt_bias, v_pool_mix, v_pool_scale, v_dn_norm_w, v_w_pool_out, v_w_dn_out, v_w_o, v_final_norm_w):
    seq, d = x.shape[1], x.shape[2]
    n_meta = meta_tokens.shape[0]
    n_heads, hd = A_log.shape[-1], dn_norm_w.shape[-1]
    dn = n_heads * hd
    pw, ng = pool_scale.shape[-1], pool_mix.shape[1]
    pg = pw // ng
    kw = conv_w.shape[1]
    pad = (-n_meta) % CHUNK
    x0 = pad + n_meta
    lp = x0 + seq
    ns = w_in.shape[-1]
    in_cols = N_DEV * ns
    o_q, o_k, o_v, o_zd = 2 * pw, 2 * pw + dn, 2 * pw + 2 * dn, 2 * pw + 3 * dn
    o_ba = 2 * pw + 4 * dn
    o_gp, o_gd = o_ba, o_ba + d
    n_main = o_gd + d
    n_all = n_main + 2 * LANES
    assert lp % CHUNK == 0 and in_cols == n_main + 2 * n_heads and 2 * n_heads <= LANES and hd == LANES
    cs, ms = conv_w.shape[-1], meta_tokens.shape[-1]
    mr = pool_mix.shape[2]
    assert ms == pg and cs % pg == 0
    to_work, to_own = _w_in_column_maps(ns, o_ba, 2 * n_heads, n_main, n_all)
    cols_major = lambda t: jnp.transpose(t, (1, 0, 2)).reshape(t.shape[1], N_DEV * t.shape[2])

    mix_g, conv_g, meta_g = _all_gather([pool_mix[0].reshape(ng * mr, pg).astype(BF16), conv_w[0], meta_tokens])
    win_g = _all_gather_tree(w_in[0].astype(BF16), after=meta_g)
    late_shards = [w_pool_out[0].astype(BF16), w_dn_out[0].astype(BF16), w_o[0].astype(BF16)]
    late_weights = _split_start(_to_all_copies, late_shards, [jax.ShapeDtypeStruct((N_DEV,) + s.shape, BF16) for s in late_shards],
                                (N_DEV - 1) * len(late_shards), "gather_out_proj_start", after=win_g)
    norm_w_in = norm_w + late_weights["token"][0, 0]
    w_all = _lane_gather_cols(win_g, _lane_gather_table(to_work), 1, n_all, "w_in_to_work").reshape(d, n_all)
    mix_f = jnp.transpose(mix_g.reshape(N_DEV, ng, mr, pg), (1, 0, 2, 3)).reshape(ng, pg, pg)
    conv_f = cols_major(conv_g)
    meta_f = cols_major(meta_g)

    h0, xn = _norm_in_fwd(x[0], meta_f, norm_w_in, pad)
    proj = _matmul(xn, w_all, NN, F32, lp, 768, 2048, "proj")
    y_pool = _pool_fwd(proj, mix_f, pool_scale, pad)
    conv_q, conv_k, conv_v = (conv_f[:, i * dn:(i + 1) * dn] for i in range(3))
    qn = _conv_fwd(proj, o_q, conv_q, hd, float(hd) ** -0.5, "conv_q_fwd")
    kn = _conv_fwd(proj, o_k, conv_k, hd, 1.0, "conv_k_fwd")
    vv = _conv_fwd(proj, o_v, conv_v, hd, None, "conv_v_fwd")
    logit_lanes = (n_heads, LANES - 2 * n_heads)
    prm = jnp.pad(A_log, ((0, 7), logit_lanes)) + jnp.pad(dt_bias, ((1, 6), logit_lanes))
    y_dn, hist, tmats = _chunk_fwd(qn, kn, vv, proj, o_zd, n_main, prm, dn_norm_w, n_heads, pad)
    me_idx = jnp.reshape(4 * lax.axis_index("x") + 2 * lax.axis_index("y") + lax.axis_index("c"), (1,)).astype(jnp.int32)
    _, landed = _split_wait(_to_all_copies, late_weights, y_dn, "gather_out_proj_wait")
    wpo_g, wdn_g, wo_g = (_fill_own_block(g, s, me_idx, "own_block_%d" % i) for i, (g, s) in enumerate(zip(landed, late_shards)))
    wpo_f = cols_major(wpo_g)
    wdn_f = wdn_g.reshape(dn, d)
    wo_f = wo_g.reshape(d, d)
    p_out = _matmul(y_pool, wpo_f, NN, F32, 1056, 1024, 1024, "pool_out")
    q_out = _matmul(y_dn, wdn_f, NN, F32, 1056, 1024, 2048, "dn_out")
    merged = _merge_fwd(p_out, q_out, proj, o_gp, o_gd)
    mo = _matmul(merged, wo_f, NN, F32, 1056, 1024, 2048, "w_o_fwd")
    dh1, d_fw, loss_part = _final_loss(h0, mo, final_norm_w.reshape(1, d), loss_target[0], x0)

    d_merged = _matmul(dh1, wo_f, NT, F32, 1056, 1024, 1024, "w_o_bwd_x")
    g_wo = _matmul(merged, dh1, TN, BF16, 1024, 1024, lp, "w_o_bwd_w")
    d_p, d_q, d_gp, d_gd = _merge_bwd(p_out, q_out, proj, o_gp, o_gd, d_merged)
    d_ypool = _matmul(d_p, wpo_f, NT, F32, 1056, 1024, 2048, "pool_out_bwd_x")
    g_wpo = _matmul(y_pool.T, d_p, NN, BF16, 1024, 1024, lp, "pool_out_bwd_w", col_blocks=N_DEV)
    d_ydn = _matmul(d_q, wdn_f, NT, F32, 1056, 1024, 2048, "dn_out_bwd_x")
    g_wdn = _matmul(y_dn, d_q, TN, BF16, 1024, 1024, lp, "dn_out_bwd_w")
    rs_early = _rs_begin([g_wpo, g_wdn.reshape(N_DEV, dn // N_DEV, d), g_wo.reshape(N_DEV, d // N_DEV, d)], "early", split=True)
    started = rs_early["token"][0, 0]
    d_u, d_zp, g_mix, g_pscale = _pool_bwd(proj, mix_f, pool_scale + started, d_ypool, pad)
    d_proj = lax.empty((lp, n_all), BF16)
    d_qn, d_kn, d_vv, d_ba, d_proj, d_prm, g_dnw = _chunk_bwd(qn, kn, vv, proj, o_zd, n_main, prm + started, dn_norm_w, hist, tmats,
                                                              d_ydn, n_heads, pad, d_proj)
    d_proj, g_cq = _conv_bwd(proj, o_q, conv_q, d_qn, hd, float(hd) ** -0.5, pad, "conv_q_bwd", d_proj)
    d_proj, g_ck = _conv_bwd(proj, o_k, conv_k, d_kn, hd, 1.0, pad, "conv_k_bwd", d_proj)
    d_proj, g_cv = _conv_bwd(proj, o_v, conv_v, d_vv, hd, None, pad, "conv_v_bwd", d_proj)
    for off, piece in ((0, d_u), (pw, d_zp), (o_gp, d_gp), (o_gd, d_gd), (n_main, d_ba.astype(BF16)), (n_main + LANES, jnp.zeros((lp, LANES), BF16))):
        d_proj = lax.dynamic_update_slice(d_proj, piece, (0, off))
    xn_t, rs_late, token = xn.T, [], None
    for half in range(2):
        rows = slice(half * (d // 2), (half + 1) * (d // 2))
        g_wall = _matmul(xn_t[rows], d_proj, NN, BF16, 1024, 768, lp, "w_in_bwd_w_%d" % half, after=token)
        g_win = _lane_gather_cols(g_wall.reshape(1, d // 2, n_all), _lane_gather_table(to_own), N_DEV, ns, "w_in_grad_to_own_%d" % half)
        rs_late.append(_rs_begin([g_win], "late%d" % half, split=True))
        token = rs_late[-1]["token"]
    d_xn = _matmul(d_proj, w_all, NT, F32, lp, 512, 2432, "w_in_bwd_x", after=token)
    d_head, grad_x, g_nw = _norm_in_bwd(h0, norm_w, d_xn, dh1, x0)
    grad_x = grad_x[None]

    by_cols = lambda t: jnp.transpose(t.reshape(t.shape[0], N_DEV, t.shape[1] // N_DEV), (1, 0, 2))
    g_conv = by_cols(jnp.concatenate([g_cq, g_ck, g_cv], axis=1)).reshape(N_DEV, kw * cs // pg, pg)
    conv_rows = -(-g_conv.shape[1] // 16) * 16
    g_small = jnp.concatenate(
        [jnp.transpose(g_mix.reshape(ng, N_DEV, mr, pg), (1, 0, 2, 3)).reshape(N_DEV, ng * mr, pg), by_cols(d_head[pad:x0]),
         jnp.pad(g_conv, ((0, 0), (0, conv_rows - g_conv.shape[1]), (0, 0)))], axis=1).astype(BF16)
    r_small, = _rs_finish(_rs_begin([g_small], "small", split=False), "small")
    r_mix, r_meta = r_small[:ng * mr], r_small[ng * mr:ng * mr + n_meta]
    r_conv = r_small[ng * mr + n_meta:ng * mr + n_meta + kw * cs // pg]
    r_wpo, r_wdn, r_wo = _rs_finish(rs_early, "early", after=r_small)

    small = [g_nw[0], d_fw[0], g_pscale[0], g_dnw[0], d_prm[0], d_prm[1], loss_part[0]]
    s_sizes = [t.shape[0] for t in small]
    s_cols = -(-sum(s_sizes) // (8 * LANES)) * LANES
    s_vec = jnp.concatenate(small + [jnp.zeros((8 * s_cols - sum(s_sizes),), F32)]).reshape(8, s_cols)
    s_red = _all_reduce_small(s_vec)
    s_sum = s_red.reshape(-1)
    r_win = None
    for half, begun in enumerate(rs_late):
        r_win = _rs_finish(begun, "late%d" % half, after=s_red, part=half, n_parts=2, dsts=r_win)
    r_win, = r_win
    s_offs = [sum(s_sizes[:i]) for i in range(len(s_sizes))]
    s_take = lambda i, n=None, o=0: s_sum[s_offs[i] + o:s_offs[i] + o + (s_sizes[i] if n is None else n)]

    grads = {
        "meta_tokens": r_meta, "norm_w": s_take(0).reshape(norm_w.shape),
        "w_in": r_win.reshape(w_in.shape), "conv_w": r_conv.reshape(conv_w.shape),
        "A_log": s_take(4, n_heads, n_heads).reshape(A_log.shape), "dt_bias": s_take(5, n_heads, n_heads).reshape(dt_bias.shape),
        "pool_mix": r_mix.reshape(pool_mix.shape), "pool_scale": s_take(2).reshape(pool_scale.shape),
        "dn_norm_w": s_take(3).reshape(dn_norm_w.shape), "w_pool_out": r_wpo.reshape(w_pool_out.shape),
        "w_dn_out": r_wdn.reshape(w_dn_out.shape), "w_o": r_wo.reshape(w_o.shape),
        "final_norm_w": s_take(1).reshape(final_norm_w.shape),
    }
    loss = s_take(6, 1)[0]

    weights = dict(meta_tokens=meta_tokens, norm_w=norm_w, w_in=w_in, conv_w=conv_w, A_log=A_log, dt_bias=dt_bias,
                   pool_mix=pool_mix, pool_scale=pool_scale, dn_norm_w=dn_norm_w, w_pool_out=w_pool_out, w_dn_out=w_dn_out,
                   w_o=w_o, final_norm_w=final_norm_w)
    m_in = dict(meta_tokens=m_meta_tokens, norm_w=m_norm_w, w_in=m_w_in, conv_w=m_conv_w, A_log=m_A_log, dt_bias=m_dt_bias,
                pool_mix=m_pool_mix, pool_scale=m_pool_scale, dn_norm_w=m_dn_norm_w, w_pool_out=m_w_pool_out,
                w_dn_out=m_w_dn_out, w_o=m_w_o, final_norm_w=m_final_norm_w)
    v_in = dict(meta_tokens=v_meta_tokens, norm_w=v_norm_w, w_in=v_w_in, conv_w=v_conv_w, A_log=v_A_log, dt_bias=v_dt_bias,
                pool_mix=v_pool_mix, pool_scale=v_pool_scale, dn_norm_w=v_dn_norm_w, w_pool_out=v_w_pool_out,
                w_dn_out=v_w_dn_out, w_o=v_w_o, final_norm_w=v_final_norm_w)
    names = list(weights)
    upd = {n: _adamw(weights[n], grads[n], m_in[n], v_in[n], "adamw_" + n) for n in names}
    return (loss, grad_x, *[grads[n] for n in names], *[upd[n][0] for n in names], *[upd[n][1] for n in names],
            *[upd[n][2] for n in names])
```

```python
import functools
import math

import jax
import jax.numpy as jnp
import numpy as np
from jax import lax
from jax.experimental import pallas as pl
from jax.experimental.pallas import tpu as pltpu

F32 = jnp.float32
BF16 = jnp.bfloat16
HIGHEST = lax.Precision.HIGHEST
MESH = pl.DeviceIdType.MESH

CHUNK = 64
NORM_EPS = 1e-6
POOL_WINDOWS = (2, 4, 8, 16)
ADAM_LR, ADAM_B1, ADAM_B2, ADAM_EPS, ADAM_WD, ADAM_STEP = 0.001, 0.9, 0.999, 1e-08, 0.01, 10
N_DEV = 8
LANES = 128
VMEM_LIMIT = 48 * 1024 * 1024

NN = (((1,), (0,)), ((), ()))
NT = (((1,), (1,)), ((), ()))
TN = (((0,), (0,)), ((), ()))


def _call(body, **kw):
    return pl.pallas_call(body, **kw)


def _params(sem=None):
    return pltpu.CompilerParams(dimension_semantics=sem, vmem_limit_bytes=VMEM_LIMIT)


def _tile(n, pref, align):
    for d in range(min(pref, n), 0, -1):
        if n % d == 0 and d % align == 0:
            return d
    return n


def _dot(a, b, dims=NN, precision=None):
    return lax.dot_general(a, b, dims, precision=precision, preferred_element_type=F32)


def _sigmoid(x):
    return 0.5 * jnp.tanh(0.5 * x) + 0.5


def _silu(x):
    return x * _sigmoid(x)


def _softplus(x):
    return jnp.maximum(x, 0.0) + jnp.log(1.0 + jnp.exp(-jnp.abs(x)))


def _rmsnorm(x, w):
    return x * lax.rsqrt(jnp.mean(x * x, axis=-1, keepdims=True) + NORM_EPS) * w


def _shift_down(x, j, row):
    if j == 0:
        return x
    return jnp.where(row >= j, pltpu.roll(x, j, 0), 0.0)


def _shift_up(x, j, row):
    if j == 0:
        return x
    n = x.shape[0]
    return jnp.where(row < n - j, pltpu.roll(x, n - j, 0), 0.0)


def _matmul(a, b, dims, out_dtype, tm, tn, tk, name, col_blocks=None, after=None):
    ta = dims == TN
    tb = dims == NT
    m, kdim = (a.shape[1], a.shape[0]) if ta else a.shape
    n = b.shape[0] if tb else b.shape[1]
    if col_blocks:
        tn = n // col_blocks
    tm, tn, tk = _tile(m, tm, 8), _tile(n, tn, LANES), _tile(kdim, tk, LANES if not ta else 16)
    nk = kdim // tk

    n_extra = 0 if after is None else 1

    def body(a_ref, b_ref, *refs):
        o_ref, scratch = refs[n_extra], refs[n_extra + 1:]
        part = _dot(a_ref[...].astype(BF16), b_ref[...].astype(BF16), dims)
        if nk == 1:
            o_ref[...] = part.astype(o_ref.dtype).reshape(o_ref.shape)
            return
        acc_ref, = scratch
        k = pl.program_id(2)

        @pl.when(k == 0)
        def _():
            acc_ref[...] = part

        @pl.when(k > 0)
        def _():
            acc_ref[...] += part

        @pl.when(k == nk - 1)
        def _():
            o_ref[...] = acc_ref[...].astype(o_ref.dtype).reshape(o_ref.shape)

    a_spec = pl.BlockSpec((tk, tm), lambda i, j, k: (k, i)) if ta else pl.BlockSpec((tm, tk), lambda i, j, k: (i, k))
    b_spec = pl.BlockSpec((tn, tk), lambda i, j, k: (j, k)) if tb else pl.BlockSpec((tk, tn), lambda i, j, k: (k, j))
    if col_blocks:
        out_spec = pl.BlockSpec((1, tm, tn), lambda i, j, k: (j, i, 0))
        out_shape = jax.ShapeDtypeStruct((col_blocks, m, tn), out_dtype)
    else:
        out_spec = pl.BlockSpec((tm, tn), lambda i, j, k: (i, j))
        out_shape = jax.ShapeDtypeStruct((m, n), out_dtype)
    return _call(
        body, name=name, grid=(m // tm, n // tn, nk),
        in_specs=[a_spec, b_spec] + [ANY] * n_extra, out_specs=out_spec, out_shape=out_shape,
        scratch_shapes=[] if nk == 1 else [pltpu.VMEM((tm, tn), F32)],
        compiler_params=_params(("parallel", "parallel", "arbitrary")),
    )(a, b, *([] if after is None else [after]))


def _norm_in_fwd(x2d, meta, w, pad):
    seq, d = x2d.shape
    x0 = pad + meta.shape[0]
    assert x0 % 16 == 0
    lp = x0 + seq
    tr = _tile(seq, 512, 16)
    vec = pl.BlockSpec((1, d), lambda i: (0, 0))
    shapes = [jax.ShapeDtypeStruct((lp, d), F32), jax.ShapeDtypeStruct((lp, d), BF16)]

    def body(x_ref, w_ref, h_ref, o_ref):
        h_ref[...] = x_ref[...]
        o_ref[...] = _rmsnorm(x_ref[...], w_ref[...]).astype(BF16)

    def head(m_ref, w_ref, h_in_ref, o_in_ref, h_ref, o_ref):
        h = jnp.concatenate([jnp.zeros((pad, d), F32), m_ref[...]], axis=0) if pad else m_ref[...]
        h_ref[...] = h
        o_ref[...] = _rmsnorm(h, w_ref[...]).astype(BF16)

    rows = _rows_after(x0, tr, d)
    h0, xn = _call(
        body, name="norm_in_fwd", grid=(seq // tr,), in_specs=[pl.BlockSpec((tr, d), lambda i: (i, 0)), vec],
        out_specs=[rows, rows], out_shape=shapes, compiler_params=_params(("parallel",)),
    )(x2d, w)
    first = pl.BlockSpec((x0, d), lambda i: (0, 0))
    return _call(
        head, name="norm_in_fwd_head", grid=(1,), in_specs=[pl.BlockSpec(meta.shape, lambda i: (0, 0)), vec, ANY, ANY],
        out_specs=[first, first], out_shape=shapes, input_output_aliases={2: 0, 3: 1}, compiler_params=_params(("arbitrary",)),
    )(meta, w, h0, xn)


def _rows_after(x0, tr, d):
    step = math.gcd(x0, tr)
    return pl.BlockSpec((pl.Element(tr), pl.Element(d)), lambda i: (pl.multiple_of(x0 + tr * i, step), 0))


def _norm_in_bwd(h0, w, dxn, dh1, x0):
    lp, d = h0.shape
    tr = _tile(lp - x0, 512, 8)
    vec = pl.BlockSpec((1, d), lambda i: (0, 0))

    def make(body_rows, first):
        def body(h_ref, w_ref, da_ref, dh1_ref, dh_ref, dw_ref):
            _, vjp = jax.vjp(_rmsnorm, h_ref[...], w_ref[...])
            dh, dw = vjp(da_ref[...])
            dh_ref[...] = dh + dh1_ref[...]

            @pl.when(pl.program_id(0) == 0)
            def _():
                dw_ref[...] = jnp.zeros_like(dw_ref)

            dw_ref[...] += dw

        rows_in = pl.BlockSpec((x0, d), lambda i: (0, 0)) if first else _rows_after(x0, tr, d)
        return _call(
            body, name="norm_in_bwd_head" if first else "norm_in_bwd", grid=(1 if first else (lp - x0) // tr,),
            in_specs=[rows_in, vec, rows_in, rows_in],
            out_specs=[pl.BlockSpec((body_rows, d), lambda i: (i, 0)), vec],
            out_shape=[jax.ShapeDtypeStruct((x0 if first else lp - x0, d), F32), jax.ShapeDtypeStruct((1, d), F32)],
            compiler_params=_params(("arbitrary",)),
        )(h0, w, dxn, dh1)

    d_head, dw_head = make(x0, True)
    grad_x, dw_rest = make(tr, False)
    return d_head, grad_x, dw_head + dw_rest


def _final_loss(h0, mo, fw, tgt, x0):
    lp, d = h0.shape
    tr = _tile(lp - x0, 512, 8)

    def body(h_ref, mo_ref, fw_ref, t_ref, dh_ref, dw_ref, loss_ref):
        tgt_v = t_ref[...]

        def loss_fn(h1, w):
            err = _rmsnorm(h1, w) - tgt_v
            return 0.5 * jnp.sum(jnp.mean(err * err, axis=-1, keepdims=True), axis=0, keepdims=True)

        loss, vjp = jax.vjp(loss_fn, h_ref[...] + mo_ref[...], fw_ref[...])
        dh, dw = vjp(jnp.ones((1, 1), F32))
        dh_ref[...] = dh

        @pl.when(pl.program_id(0) == 0)
        def _():
            dw_ref[...] = jnp.zeros_like(dw_ref)
            loss_ref[...] = jnp.zeros_like(loss_ref)

        dw_ref[...] += dw
        loss_ref[...] += jnp.broadcast_to(loss, loss_ref.shape)

    def zero_head(dh_in_ref, dh_ref):
        dh_ref[...] = jnp.zeros_like(dh_ref)

    rows = _rows_after(x0, tr, d)
    vec = pl.BlockSpec((1, d), lambda i: (0, 0))
    dh1, dw, loss = _call(
        body, name="final_loss", grid=((lp - x0) // tr,),
        in_specs=[rows, rows, vec, pl.BlockSpec((tr, d), lambda i: (i, 0))],
        out_specs=[rows, vec, pl.BlockSpec((8, LANES), lambda i: (0, 0))],
        out_shape=[jax.ShapeDtypeStruct((lp, d), F32), jax.ShapeDtypeStruct((1, d), F32), jax.ShapeDtypeStruct((8, LANES), F32)],
        compiler_params=_params(("arbitrary",)),
    )(h0, mo, fw, tgt)
    dh1 = _call(
        zero_head, name="final_loss_head", grid=(1,), in_specs=[ANY], out_specs=pl.BlockSpec((x0, d), lambda i: (0, 0)),
        out_shape=jax.ShapeDtypeStruct((lp, d), F32), input_output_aliases={0: 0}, compiler_params=_params(("arbitrary",)),
    )(dh1)
    return dh1, dw, loss


def _pool_select(parts, g):
    out = parts[-1]
    for gi in range(len(parts) - 2, -1, -1):
        out = jnp.where(g == gi, parts[gi], out)
    return out


def _pool_count(row, g, pad):
    win = _pool_select([jnp.full(row.shape, float(w), F32) for w in POOL_WINDOWS], g)
    return jnp.maximum(jnp.minimum((row - pad + 1).astype(F32), win), 1.0)


def _pooled(u, g, row, pad):
    sums, s, span = [], u, 1
    for w in POOL_WINDOWS:
        while span < w:
            s = s + _shift_down(s, span, row)
            span *= 2
        sums.append(s)
    return _pool_select(sums, g) / _pool_count(row, g, pad) - u


def _pooled_adjoint(dp, g, row, pad):
    e = dp / _pool_count(row, g, pad)
    sums, s, span = [], e, 1
    for w in POOL_WINDOWS:
        while span < w:
            s = s + _shift_up(s, span, row)
            span *= 2
        sums.append(s)
    return _pool_select(sums, g) - dp


def _pool_specs(lp, pg, ng, z_off):
    u_spec = pl.BlockSpec((lp, pg), lambda g: (0, g))
    z_spec = pl.BlockSpec((lp, pg), lambda g: (0, z_off + g))
    mix_spec = pl.BlockSpec((1, pg, pg), lambda g: (g, 0, 0))
    vec_spec = pl.BlockSpec((1, pg), lambda g: (0, g))
    return u_spec, z_spec, mix_spec, vec_spec


def _pool_fwd(proj, mix, scale, pad):
    lp = proj.shape[0]
    ng, pg, _ = mix.shape
    pw = ng * pg

    def body(u_ref, z_ref, mix_ref, sc_ref, y_ref):
        g = pl.program_id(0)
        row = lax.broadcasted_iota(jnp.int32, (lp, 1), 0)
        pooled = _pooled(u_ref[...], g, row, pad)
        mixed = _dot(pooled.astype(BF16), mix_ref[0])
        y_ref[...] = (mixed * sc_ref[...] * _silu(z_ref[...])).astype(BF16)

    u_spec, z_spec, mix_spec, vec_spec = _pool_specs(lp, pg, ng, pw // pg)
    return _call(
        body, name="pool_fwd", grid=(ng,), in_specs=[u_spec, z_spec, mix_spec, vec_spec], out_specs=u_spec,
        out_shape=jax.ShapeDtypeStruct((lp, pw), BF16), compiler_params=_params(("parallel",)),
    )(proj, proj, mix, scale)


def _pool_bwd(proj, mix, scale, dy, pad):
    lp = proj.shape[0]
    ng, pg, _ = mix.shape
    pw = ng * pg

    def body(u_ref, z_ref, mix_ref, sc_ref, dy_ref, du_ref, dz_ref, dmix_ref, dsc_ref):
        g = pl.program_id(0)
        row = lax.broadcasted_iota(jnp.int32, (lp, 1), 0)
        real = row >= pad
        z = z_ref[...]
        pooled = _pooled(u_ref[...], g, row, pad).astype(BF16)
        mixed = _dot(pooled, mix_ref[0])
        sig = _sigmoid(z)
        sz = z * sig
        dyv = dy_ref[...]
        dsc_ref[...] = jnp.sum(dyv * mixed * sz, axis=0, keepdims=True)
        d_sz = dyv * mixed * sc_ref[...]
        dz_ref[...] = jnp.where(real, d_sz * (sig + sz * (1.0 - sig)), 0.0).astype(BF16)
        d_mixed = (dyv * sc_ref[...] * sz).astype(BF16)
        dmix_ref[0] = _dot(pooled, d_mixed, TN)
        d_pooled = _dot(d_mixed, mix_ref[0], NT)
        du_ref[...] = jnp.where(real, _pooled_adjoint(d_pooled, g, row, pad), 0.0).astype(BF16)

    u_spec, z_spec, mix_spec, vec_spec = _pool_specs(lp, pg, ng, pw // pg)
    return _call(
        body, name="pool_bwd", grid=(ng,),
        in_specs=[u_spec, z_spec, mix_spec, vec_spec, u_spec], out_specs=[u_spec, u_spec, mix_spec, vec_spec],
        out_shape=[jax.ShapeDtypeStruct((lp, pw), BF16), jax.ShapeDtypeStruct((lp, pw), BF16),
                   jax.ShapeDtypeStruct((ng, pg, pg), F32), jax.ShapeDtypeStruct((1, pw), F32)],
        compiler_params=_params(("parallel",)),
    )(proj, proj, mix, scale, dy)


def _conv_pre(x, w, row):
    kw = w.shape[0]
    y = w[kw - 1:kw, :] * x
    for kk in range(kw - 1):
        y = y + w[kk:kk + 1, :] * _shift_down(x, kw - 1 - kk, row)
    return y


def _conv_post(y, out_scale):
    s = _silu(y)
    if out_scale is None:
        return s
    return s * lax.rsqrt(jnp.sum(s * s, axis=-1, keepdims=True) + NORM_EPS) * out_scale


def _conv_fwd(proj, col_off, w, hd, out_scale, name):
    lp = proj.shape[0]
    kw, width = w.shape
    blk0 = col_off // hd

    def body(x_ref, w_ref, o_ref):
        row = lax.broadcasted_iota(jnp.int32, (lp, 1), 0)
        o_ref[...] = _conv_post(_conv_pre(x_ref[...], w_ref[...], row), out_scale)

    return _call(
        body, name=name, grid=(width // hd,),
        in_specs=[pl.BlockSpec((lp, hd), lambda j: (0, blk0 + j)), pl.BlockSpec((kw, hd), lambda j: (0, j))],
        out_specs=pl.BlockSpec((lp, hd), lambda j: (0, j)),
        out_shape=jax.ShapeDtypeStruct((lp, width), F32), compiler_params=_params(("parallel",)),
    )(proj, w)


def _conv_bwd(proj, col_off, w, d_out, hd, out_scale, pad, name, dst):
    lp = proj.shape[0]
    kw, width = w.shape
    blk0 = col_off // hd

    def body(x_ref, w_ref, do_ref, dst_ref, dx_ref, dw_ref):
        row = lax.broadcasted_iota(jnp.int32, (lp, 1), 0)
        real = row >= pad
        x, wv = x_ref[...], w_ref[...]
        _, vjp = jax.vjp(functools.partial(_conv_post, out_scale=out_scale), _conv_pre(x, wv, row))
        dy = jnp.where(real, vjp(do_ref[...])[0], 0.0)
        dx = wv[kw - 1:kw, :] * dy
        dw_ref[kw - 1:kw, :] = jnp.sum(dy * x, axis=0, keepdims=True)
        for kk in range(kw - 1):
            ahead = _shift_up(dy, kw - 1 - kk, row)
            dx = dx + wv[kk:kk + 1, :] * ahead
            dw_ref[kk:kk + 1, :] = jnp.sum(ahead * x, axis=0, keepdims=True)
        dx_ref[...] = jnp.where(real, dx, 0.0).astype(BF16)

    col = pl.BlockSpec((lp, hd), lambda j: (0, j))
    at_off = pl.BlockSpec((lp, hd), lambda j: (0, blk0 + j))
    wspec = pl.BlockSpec((kw, hd), lambda j: (0, j))
    return _call(
        body, name=name, grid=(width // hd,),
        in_specs=[at_off, wspec, col, ANY], out_specs=[at_off, wspec],
        out_shape=[jax.ShapeDtypeStruct(dst.shape, BF16), jax.ShapeDtypeStruct((kw, width), F32)],
        input_output_aliases={3: 0}, compiler_params=_params(("parallel",)),
    )(proj, w, d_out, dst)


HEADS_PER_STEP = 16
HEADS_PER_STEP_BWD = 16


def _matmul_with_direct_vjp(dims, da_dims, db_dims, db_swapped):
    @jax.custom_vjp
    def mm(a, b):
        return _dot(a, b, dims)

    def fwd(a, b):
        return _dot(a, b, dims), (a, b)

    def bwd(res, g):
        a, b = res
        return _dot(g, b, da_dims) if not db_swapped[0] else _dot(b, g, da_dims), _dot(a, g, db_dims) if not db_swapped[1] else _dot(g, a, db_dims)

    mm.defvjp(fwd, bwd)
    return mm


_mm_nn = _matmul_with_direct_vjp(NN, NT, TN, (False, False))
_mm_nt = _matmul_with_direct_vjp(NT, NN, TN, (False, True))
_mm_tn = _matmul_with_direct_vjp(TN, NT, NN, (True, False))


def _each(fn, *lists):
    return [fn(*args) for args in zip(*lists)]


def _dot3_each(a_list, b_list, dims=NN):
    hi = lambda t: t.astype(BF16)
    lo = lambda t, t_hi: (t - t_hi.astype(F32)).astype(BF16)
    dot = lambda x, y: _dot(x, y, dims)
    a_hi, b_hi = _each(hi, a_list), _each(hi, b_list)
    a_lo, b_lo = _each(lo, a_list, a_hi), _each(lo, b_list, b_hi)
    hh, hl, lh = _each(dot, a_hi, b_hi), _each(dot, a_hi, b_lo), _each(dot, a_lo, b_hi)
    return _each(lambda x, y, w: x + (y + w), hh, hl, lh)


@jax.custom_vjp
def _unit_lower_inverse(lmats):
    c = lmats[0].shape[0]
    eye = lax.broadcasted_iota(jnp.int32, (c, c), 0) == lax.broadcasted_iota(jnp.int32, (c, c), 1)
    a = [-m for m in lmats]
    tmat = [jnp.where(eye, 1.0, 0.0).astype(F32) + m for m in a]
    span = 2
    while span < c:
        a = _dot3_each(a, a)
        tmat = _each(lambda t, u: t + u, tmat, _dot3_each(tmat, a))
        span *= 2
    return tuple(tmat)


def _unit_lower_inverse_fwd(lmats):
    tmats = _unit_lower_inverse(lmats)
    return tmats, tmats


def _unit_lower_inverse_bwd(tmats, cts):
    left = _each(lambda t, ct: _dot(t, ct, TN, HIGHEST), tmats, cts)
    return (tuple(_each(lambda m, t: -_dot(m, t, NT, HIGHEST), left, tmats)),)


_unit_lower_inverse.defvjp(_unit_lower_inverse_fwd, _unit_lower_inverse_bwd)


@jax.custom_vjp
def _known_inverse(lmats, tmats):
    return tmats


def _known_inverse_fwd(lmats, tmats):
    return tmats, tmats


def _known_inverse_bwd(tmats, cts):
    return _unit_lower_inverse_bwd(tmats, cts)[0], tuple(jnp.zeros_like(t) for t in tmats)


_known_inverse.defvjp(_known_inverse_fwd, _known_inverse_bwd)


def _chunk_math(states, q, k, v, ba, z, prm, nw, head0, rowmask, n_heads, tmats=None, keep_tmats=False):
    c = q.shape[0]
    heads = list(range(len(states)))
    hd = q.shape[1] // len(states)
    lane = lax.broadcasted_iota(jnp.int32, ba.shape, 1)
    sub = lax.broadcasted_iota(jnp.int32, (ba.shape[1], c), 0)
    ri = lax.broadcasted_iota(jnp.int32, (c, c), 0)
    ci = lax.broadcasted_iota(jnp.int32, (c, c), 1)
    last = lax.broadcasted_iota(jnp.int32, (c, 1), 0) == c - 1
    causal, strict = ri >= ci, ri > ci
    beta_all = _sigmoid(ba) * rowmask
    g_all = -jnp.exp(prm[0:1, :]) * _softplus(ba + prm[1:2, :]) * rowmask
    gcum_all = _dot(jnp.where(causal, 1.0, 0.0).astype(F32), g_all, precision=HIGHEST)
    gcum_t = gcum_all.T
    split = lambda t: [t[:, j * hd:(j + 1) * hd] for j in heads]
    qs, ks, vs, zs = split(q), split(k), split(v), split(z)
    beta = [jnp.sum(jnp.where(lane == head0 + j, beta_all, 0.0), axis=1, keepdims=True) for j in heads]
    gcum = [jnp.sum(jnp.where(lane == n_heads + head0 + j, gcum_all, 0.0), axis=1, keepdims=True) for j in heads]
    grow = [jnp.sum(jnp.where(sub == n_heads + head0 + j, gcum_t, 0.0), axis=0, keepdims=True) for j in heads]
    glast = _each(lambda gc: jnp.sum(jnp.where(last, gc, 0.0), axis=0, keepdims=True), gcum)
    decay = _each(lambda gc, gr: jnp.where(causal, jnp.exp(jnp.where(causal, gc - gr, 0.0)), 0.0), gcum, grow)
    eg = _each(jnp.exp, gcum)
    k_beta = _each(jnp.multiply, ks, beta)
    kk = _each(_mm_nt, k_beta, ks)
    lmats = tuple(_each(lambda m, dc: jnp.where(strict, m * dc, 0.0), kk, decay))
    tmat = list(_unit_lower_inverse(lmats) if tmats is None else _known_inverse(lmats, tuple(tmats)))
    u_c = _each(_mm_nn, tmat, _each(jnp.multiply, vs, beta))
    w_c = _each(_mm_nn, tmat, _each(jnp.multiply, k_beta, eg))
    qk = _each(lambda a, b, dc: jnp.where(causal, _mm_nt(a, b) * dc, 0.0), qs, ks, decay)
    v_new = _each(lambda u, w, s: u - _mm_nn(w, s), u_c, w_c, list(states))
    o = _each(lambda a, e, s, m, vn: _mm_nn(a * e, s) + _mm_nn(m, vn), qs, eg, list(states), qk, v_new)
    k_dec = _each(lambda a, gl, gc: a * jnp.exp(gl - gc), ks, glast, gcum)
    new_states = _each(lambda s, gl, kd, vn: s * jnp.exp(gl) + _mm_tn(kd, vn), list(states), glast, k_dec, v_new)
    ys = _each(lambda oj, zj: _rmsnorm(oj, nw) * _silu(zj), o, zs)
    if keep_tmats:
        return jnp.concatenate(ys, axis=1), tuple(new_states), tuple(tmat)
    return jnp.concatenate(ys, axis=1), tuple(new_states)


def _chunk_specs(nc, hd, n_heads, z_off, ba_off, rev):
    cidx = (lambda c: nc - 1 - c) if rev else (lambda c: c)
    hb = min(HEADS_PER_STEP_BWD if rev else HEADS_PER_STEP, n_heads)
    assert n_heads % hb == 0 and z_off % (hb * hd) == 0 and ba_off % LANES == 0
    blk = lambda off: pl.BlockSpec((CHUNK, hb * hd), lambda c, g: (cidx(c), off + g))
    ba_spec = lambda off: pl.BlockSpec((CHUNK, LANES), lambda c, g: (cidx(c), off // LANES))
    prm_spec = pl.BlockSpec((8, LANES), lambda c, g: (0, 0))
    nw_spec = pl.BlockSpec((1, hd), lambda c, g: (0, 0))
    st_spec = pl.BlockSpec((1, hb, hd, hd), lambda c, g: (cidx(c), g, 0, 0))
    return blk, ba_spec, prm_spec, nw_spec, st_spec, blk(z_off // (hb * hd))


def _rowmask(chunk_idx, pad):
    row = chunk_idx * CHUNK + lax.broadcasted_iota(jnp.int32, (CHUNK, 1), 0)
    return jnp.where(row >= pad, 1.0, 0.0).astype(F32)


def _chunk_fwd(qn, kn, vv, proj, z_off, ba_off, prm, nw, n_heads, pad):
    lp, dn = qn.shape
    hd = dn // n_heads
    nc = lp // CHUNK
    hb = min(HEADS_PER_STEP, n_heads)

    def body(q_ref, k_ref, v_ref, ba_ref, z_ref, prm_ref, nw_ref, y_ref, hist_ref, tm_ref, st_ref):
        c, g = pl.program_id(0), pl.program_id(1)

        @pl.when(c == 0)
        def _():
            for j in range(hb):
                st_ref[g * hb + j] = jnp.zeros((hd, hd), F32)

        states = tuple(st_ref[g * hb + j] for j in range(hb))
        for j in range(hb):
            hist_ref[0, j] = states[j]
        y, new_states, tmats = _chunk_math(states, q_ref[...], k_ref[...], v_ref[...], ba_ref[...], z_ref[...], prm_ref[...],
                                           nw_ref[...], g * hb, _rowmask(c, pad), n_heads, keep_tmats=True)
        y_ref[...] = y.astype(BF16)
        for j in range(hb):
            st_ref[g * hb + j] = new_states[j]
            tm_ref[0, j] = tmats[j]

    blk, ba_spec, prm_spec, nw_spec, st_spec, z_spec = _chunk_specs(nc, hd, n_heads, z_off, ba_off, False)
    tm_spec = pl.BlockSpec((1, hb, CHUNK, CHUNK), lambda c, g: (c, g, 0, 0))
    return _call(
        body, name="chunk_fwd", grid=(nc, n_heads // hb),
        in_specs=[blk(0), blk(0), blk(0), ba_spec(ba_off), z_spec, prm_spec, nw_spec], out_specs=[blk(0), st_spec, tm_spec],
        out_shape=[jax.ShapeDtypeStruct((lp, dn), BF16), jax.ShapeDtypeStruct((nc, n_heads, hd, hd), F32),
                   jax.ShapeDtypeStruct((nc, n_heads, CHUNK, CHUNK), F32)],
        scratch_shapes=[pltpu.VMEM((n_heads, hd, hd), F32)],
        compiler_params=_params(("arbitrary", "arbitrary")),
    )(qn, kn, vv, proj, proj, prm, nw)


def _chunk_bwd(qn, kn, vv, proj, z_off, ba_off, prm, nw, hist, tmats, dy, n_heads, pad, d_proj):
    lp, dn = qn.shape
    hd = dn // n_heads
    nc = lp // CHUNK
    hb = min(HEADS_PER_STEP_BWD, n_heads)

    def body(q_ref, k_ref, v_ref, ba_ref, z_ref, prm_ref, nw_ref, hist_ref, tm_ref, dy_ref, d_proj_ref,
             dq_ref, dk_ref, dv_ref, dba_ref, dz_ref, dprm_ref, dnw_ref, dst_ref):
        step, g = pl.program_id(0), pl.program_id(1)

        @pl.when(step == 0)
        def _():
            for j in range(hb):
                dst_ref[g * hb + j] = jnp.zeros((hd, hd), F32)

        @pl.when((step == 0) & (g == 0))
        def _():
            dprm_ref[...] = jnp.zeros_like(dprm_ref)
            dnw_ref[...] = jnp.zeros_like(dnw_ref)

        @pl.when(g == 0)
        def _():
            dba_ref[...] = jnp.zeros_like(dba_ref)

        def fn(states, q, k, v, ba, z, prm_v, nw_v, known):
            return _chunk_math(states, q, k, v, ba, z, prm_v, nw_v, g * hb, _rowmask(nc - 1 - step, pad), n_heads, tmats=known)

        states = tuple(hist_ref[0, j] for j in range(hb))
        known = tuple(tm_ref[0, j] for j in range(hb))
        _, vjp = jax.vjp(fn, states, q_ref[...], k_ref[...], v_ref[...], ba_ref[...], z_ref[...], prm_ref[...], nw_ref[...], known)
        dst, dq, dk, dv, dba, dz, dprm, dnw, _ = vjp((dy_ref[...], tuple(dst_ref[g * hb + j] for j in range(hb))))
        for j in range(hb):
            dst_ref[g * hb + j] = dst[j]
        dq_ref[...] = dq
        dk_ref[...] = dk
        dv_ref[...] = dv
        dz_ref[...] = dz.astype(BF16)
        dba_ref[...] += dba
        dprm_ref[...] += dprm
        dnw_ref[...] += dnw

    blk, ba_spec, prm_spec, nw_spec, st_spec, z_spec = _chunk_specs(nc, hd, n_heads, z_off, ba_off, True)
    f32_full = jax.ShapeDtypeStruct((lp, dn), F32)
    tm_spec = pl.BlockSpec((1, hb, CHUNK, CHUNK), lambda c, g: (nc - 1 - c, g, 0, 0))
    return _call(
        body, name="chunk_bwd", grid=(nc, n_heads // hb),
        in_specs=[blk(0), blk(0), blk(0), ba_spec(ba_off), z_spec, prm_spec, nw_spec, st_spec, tm_spec, blk(0), ANY],
        out_specs=[blk(0), blk(0), blk(0), ba_spec(0), z_spec, prm_spec, nw_spec],
        out_shape=[f32_full, f32_full, f32_full, jax.ShapeDtypeStruct((lp, LANES), F32), jax.ShapeDtypeStruct(d_proj.shape, BF16),
                   jax.ShapeDtypeStruct((8, LANES), F32), jax.ShapeDtypeStruct((1, hd), F32)],
        scratch_shapes=[pltpu.VMEM((n_heads, hd, hd), F32)],
        input_output_aliases={10: 4}, compiler_params=_params(("arbitrary", "arbitrary")),
    )(qn, kn, vv, proj, proj, prm, nw, hist, tmats, dy, d_proj)


def _merge_math(p, q, gp, gd):
    return _sigmoid(gp) * p + _sigmoid(gd) * q


def _merge_specs(lp, d, gp_off, gd_off):
    tr, tc = _tile(lp, 264, 16), _tile(d, 1024, LANES)
    blk = pl.BlockSpec((tr, tc), lambda i, j: (i, j))
    gp_spec = pl.BlockSpec((tr, tc), lambda i, j: (i, gp_off // tc + j))
    gd_spec = pl.BlockSpec((tr, tc), lambda i, j: (i, gd_off // tc + j))
    return (lp // tr, d // tc), blk, gp_spec, gd_spec


def _merge_fwd(p, q, proj, gp_off, gd_off):
    lp, d = p.shape
    grid, blk, gp_spec, gd_spec = _merge_specs(lp, d, gp_off, gd_off)

    def body(p_ref, q_ref, gp_ref, gd_ref, o_ref):
        o_ref[...] = _merge_math(p_ref[...], q_ref[...], gp_ref[...], gd_ref[...]).astype(BF16)

    return _call(
        body, name="merge_fwd", grid=grid, in_specs=[blk, blk, gp_spec, gd_spec], out_specs=blk,
        out_shape=jax.ShapeDtypeStruct((lp, d), BF16), compiler_params=_params(("parallel", "parallel")),
    )(p, q, proj, proj)


def _merge_bwd(p, q, proj, gp_off, gd_off, dm):
    lp, d = p.shape
    grid, blk, gp_spec, gd_spec = _merge_specs(lp, d, gp_off, gd_off)

    def body(p_ref, q_ref, gp_ref, gd_ref, dm_ref, dp_ref, dq_ref, dgp_ref, dgd_ref):
        _, vjp = jax.vjp(_merge_math, p_ref[...], q_ref[...], gp_ref[...], gd_ref[...])
        for ref, val in zip((dp_ref, dq_ref, dgp_ref, dgd_ref), vjp(dm_ref[...])):
            ref[...] = val.astype(BF16)

    out = jax.ShapeDtypeStruct((lp, d), BF16)
    return _call(
        body, name="merge_bwd", grid=grid, in_specs=[blk, blk, gp_spec, gd_spec, blk], out_specs=[blk] * 4,
        out_shape=[out] * 4, compiler_params=_params(("parallel", "parallel")),
    )(p, q, proj, proj, dm)


def _adamw(w, g, m, v, name):
    shape = w.shape
    w2, g2, m2, v2 = (t.reshape((-1, shape[-1])) for t in (w, g, m, v))
    rows, cols = w2.shape
    tr = _tile(rows, 128, 8)

    def body(w_ref, g_ref, m_ref, v_ref, d_ref, nm_ref, nv_ref):
        gv = g_ref[...]
        nm = ADAM_B1 * m_ref[...] + (1.0 - ADAM_B1) * gv
        nv = ADAM_B2 * v_ref[...] + (1.0 - ADAM_B2) * (gv * gv)
        m_hat = nm / (1.0 - ADAM_B1 ** ADAM_STEP)
        v_hat = nv / (1.0 - ADAM_B2 ** ADAM_STEP)
        d_ref[...] = -ADAM_LR * (m_hat / (jnp.sqrt(v_hat) + ADAM_EPS) + ADAM_WD * w_ref[...])
        nm_ref[...] = nm
        nv_ref[...] = nv

    blk = pl.BlockSpec((tr, cols), lambda i: (i, 0))
    out = jax.ShapeDtypeStruct((rows, cols), F32)
    res = _call(
        body, name=name, grid=(rows // tr,), in_specs=[blk] * 4, out_specs=[blk] * 3, out_shape=[out] * 3,
        compiler_params=_params(("parallel",)),
    )(w2, g2, m2, v2)
    return tuple(t.reshape(shape) for t in res)


def _coords():
    return lax.axis_index("x"), lax.axis_index("y"), lax.axis_index("c")


def _flip(v, bit):
    return 1 - v if bit else v


CHIP_FLIPS = ((1, 0), (0, 1), (1, 1))
ANY = pl.BlockSpec(memory_space=pl.ANY)


def _all_gather(shards):
    n = len(shards)

    def body(*refs):
        x_refs, out_refs = refs[:n], refs[n:2 * n]
        send_sems, recv_sems, local_sems = refs[2 * n:]
        x, y, c = _coords()
        sibling = (x, y, 1 - c)
        chips = [(_flip(x, fx), _flip(y, fy)) for fx, fy in CHIP_FLIPS]

        def copy(a, k, block, to, from_input=False):
            px, py, pc = block
            slot = out_refs[a].at[4 * px + 2 * py + pc]
            return pltpu.make_async_remote_copy(
                src_ref=x_refs[a] if from_input else slot, dst_ref=slot,
                send_sem=send_sems.at[7 * a + k], recv_sem=recv_sems.at[7 * a + k], device_id=to, device_id_type=MESH)

        mine = [pltpu.make_async_copy(x_refs[a], out_refs[a].at[4 * x + 2 * y + c], local_sems.at[a]) for a in range(n)]
        first = []
        for a in range(n):
            mine[a].start()
            first.append(copy(a, 0, (x, y, c), sibling, True))
            first += [copy(a, 1 + j, (x, y, c), (*chip, c), True) for j, chip in enumerate(chips)]
        for cp in first:
            cp.start()
        passed = []
        for j, chip in enumerate(chips):
            for a in range(n):
                copy(a, 1 + j, (*chip, c), (x, y, c)).wait_recv()
                passed.append(copy(a, 4 + j, (*chip, c), sibling))
                passed[-1].start()
        for a in range(n):
            copy(a, 0, (x, y, 1 - c), (x, y, c)).wait_recv()
            for j, chip in enumerate(chips):
                copy(a, 4 + j, (*chip, 1 - c), (x, y, c)).wait_recv()
        for cp in first + passed:
            cp.wait_send()
        for cp in mine:
            cp.wait()

    return _call(
        body, name="all_gather", in_specs=[ANY] * n, out_specs=[ANY] * n,
        out_shape=[jax.ShapeDtypeStruct((N_DEV,) + s.shape, s.dtype) for s in shards],
        scratch_shapes=[pltpu.SemaphoreType.DMA((7 * n,)), pltpu.SemaphoreType.DMA((7 * n,)), pltpu.SemaphoreType.DMA((n,))],
    )(*shards)


def _all_gather_tree(shard, after):
    rows, cols = shard.shape
    half = rows // 2
    assert rows % 32 == 0

    def body(x_ref, after_ref, out_ref, send_sems, recv_sems, local_sem):
        x, y, c = _coords()
        me, sibling = (x, y, c), (x, y, 1 - c)
        x_nbr, y_nbr, diag = (1 - x, y), (x, 1 - y), (1 - x, 1 - y)

        def part(ref, h):
            return ref if h is None else ref.at[pl.ds(h * half, half)]

        def copy(k, block, to, h=None, from_input=False):
            px, py, pc = block
            slot = part(out_ref.at[4 * px + 2 * py + pc], h)
            return pltpu.make_async_remote_copy(
                src_ref=part(x_ref, h) if from_input else slot, dst_ref=slot,
                send_sem=send_sems.at[k], recv_sem=recv_sems.at[k], device_id=to, device_id_type=MESH)

        mine = pltpu.make_async_copy(x_ref, out_ref.at[4 * x + 2 * y + c], local_sem)
        mine.start()
        started = [copy(0, me, sibling, None, True),
                   copy(1, me, (*x_nbr, c), 0, True), copy(2, me, (*x_nbr, c), 1, True),
                   copy(4, me, (*y_nbr, c), 1, True), copy(3, me, (*y_nbr, c), 0, True)]
        for cp in started:
            cp.start()
        copy(1, (*x_nbr, c), me, 0).wait_recv()
        started.append(copy(5, (*x_nbr, c), (*y_nbr, c), 0))
        started[-1].start()
        copy(4, (*y_nbr, c), me, 1).wait_recv()
        started.append(copy(6, (*y_nbr, c), (*x_nbr, c), 1))
        started[-1].start()
        copy(2, (*x_nbr, c), me, 1).wait_recv()
        started.append(copy(7, (*x_nbr, c), sibling))
        started[-1].start()
        copy(3, (*y_nbr, c), me, 0).wait_recv()
        started.append(copy(8, (*y_nbr, c), sibling))
        started[-1].start()
        copy(5, (*diag, c), me, 0).wait_recv()
        copy(6, (*diag, c), me, 1).wait_recv()
        started.append(copy(9, (*diag, c), sibling))
        started[-1].start()
        copy(0, sibling, me).wait_recv()
        for k, chip in ((7, x_nbr), (8, y_nbr), (9, diag)):
            copy(k, (*chip, 1 - c), me).wait_recv()
        for cp in started:
            cp.wait_send()
        mine.wait()

    return _call(
        body, name="all_gather_tree", in_specs=[ANY, ANY], out_specs=ANY,
        out_shape=jax.ShapeDtypeStruct((N_DEV, rows, cols), shard.dtype),
        scratch_shapes=[pltpu.SemaphoreType.DMA((10,)), pltpu.SemaphoreType.DMA((10,)), pltpu.SemaphoreType.DMA],
    )(shard, after)


def _rs_to_sibling(gs, name):
    n = len(gs)

    def body(*refs):
        g_refs, got_refs = refs[:n], refs[n:2 * n]
        send_sems, recv_sems = refs[2 * n:]
        x, y, c = _coords()
        copies = []
        for a in range(n):
            for p in range(4):
                cp = pltpu.make_async_remote_copy(
                    src_ref=g_refs[a].at[2 * p + (1 - c)], dst_ref=got_refs[a].at[p], send_sem=send_sems.at[4 * a + p],
                    recv_sem=recv_sems.at[4 * a + p], device_id=(x, y, 1 - c), device_id_type=MESH)
                cp.start()
                copies.append(cp)
        for cp in copies:
            cp.wait()

    return _call(
        body, name=name, in_specs=[ANY] * n, out_specs=[ANY] * n,
        out_shape=[jax.ShapeDtypeStruct((4,) + g.shape[1:], g.dtype) for g in gs],
        scratch_shapes=[pltpu.SemaphoreType.DMA((4 * n,)), pltpu.SemaphoreType.DMA((4 * n,))],
    )(*gs)


def _rs_pair_sum(g, got, c_idx, name):
    _, rows, cols = g.shape
    tr = _tile(rows, 256, 16)

    def body(c_ref, g_ref, got_ref, o_ref):
        o_ref[...] = (g_ref[...].astype(F32) + got_ref[...].astype(F32)).astype(o_ref.dtype)

    grid_spec = pltpu.PrefetchScalarGridSpec(
        num_scalar_prefetch=1, grid=(4, rows // tr),
        in_specs=[pl.BlockSpec((1, tr, cols), lambda p, i, c_ref: (2 * p + c_ref[0], i, 0)),
                  pl.BlockSpec((1, tr, cols), lambda p, i, c_ref: (p, i, 0))],
        out_specs=pl.BlockSpec((1, tr, cols), lambda p, i, c_ref: (p, i, 0)))
    return _call(
        body, name=name, grid_spec=grid_spec, out_shape=jax.ShapeDtypeStruct((4, rows, cols), g.dtype),
        compiler_params=_params(("parallel", "parallel")),
    )(c_idx, g, got)


def _to_chips_copies(p_refs, got_refs, send_sems, recv_sems):
    x, y, c = _coords()
    copies = []
    for a in range(len(p_refs)):
        for k, (fx, fy) in enumerate(CHIP_FLIPS):
            px, py = _flip(x, fx), _flip(y, fy)
            copies.append(pltpu.make_async_remote_copy(
                src_ref=p_refs[a].at[2 * px + py], dst_ref=got_refs[a].at[k], send_sem=send_sems.at[3 * a + k],
                recv_sem=recv_sems.at[3 * a + k], device_id=(px, py, c), device_id_type=MESH))
    return copies


def _rs_to_chips(partials, name):
    n = len(partials)

    def body(*refs):
        copies = _to_chips_copies(refs[:n], refs[n:2 * n], *refs[2 * n:])
        for cp in copies:
            cp.start()
        for cp in copies:
            cp.wait()

    return _call(
        body, name=name, in_specs=[ANY] * n, out_specs=[ANY] * n,
        out_shape=[jax.ShapeDtypeStruct((3,) + p.shape[1:], p.dtype) for p in partials],
        scratch_shapes=[pltpu.SemaphoreType.DMA((3 * n,)), pltpu.SemaphoreType.DMA((3 * n,))],
    )(*partials)


HBM = pl.BlockSpec(memory_space=pltpu.HBM)
SEM = pl.BlockSpec(memory_space=pltpu.SEMAPHORE)
SIDE_EFFECT = pltpu.CompilerParams(has_side_effects=pltpu.SideEffectType.DATAFLOW_SIDE_EFFECTING)


def _split_start(copies_fn, srcs, land_shapes, n_sems, name, after=None):
    n, m = len(srcs), len(land_shapes)
    extra = [] if after is None else [after]

    def body(*refs):
        outs = refs[n + m + len(extra):]
        send_sems, recv_sems, token = outs[0], outs[1], outs[-1]
        for cp in copies_fn(refs[:n], refs[n:n + m], send_sems, recv_sems):
            cp.start()
        token[...] = jnp.zeros_like(token)

    ins = [pltpu.with_memory_space_constraint(t, pltpu.HBM) for t in list(srcs) + [lax.empty(s.shape, s.dtype) for s in land_shapes]]
    res = _call(
        body, name=name, in_specs=[HBM] * (n + m) + [ANY] * len(extra),
        out_specs=[SEM, SEM] + [HBM] * (n + m) + [pl.BlockSpec(memory_space=pltpu.VMEM)],
        out_shape=[pltpu.SemaphoreType.DMA((n_sems,)), pltpu.SemaphoreType.DMA((n_sems,))]
        + [pltpu.HBM(t.shape, t.dtype) for t in ins] + [jax.ShapeDtypeStruct((8, LANES), F32)],
        input_output_aliases={i: 2 + i for i in range(n + m)}, compiler_params=SIDE_EFFECT,
    )(*ins, *extra)
    return dict(sems=(res[0], res[1]), srcs=res[2:2 + n], lands=res[2 + n:2 + n + m], token=res[-1])


def _split_wait(copies_fn, started, after, name):
    n, m = len(started["srcs"]), len(started["lands"])

    def body(*refs):
        for cp in copies_fn(refs[:n], refs[n:n + m], refs[n + m], refs[n + m + 1]):
            cp.wait_send()
            cp.wait_recv()

    bufs = list(started["srcs"]) + list(started["lands"])
    res = _call(
        body, name=name, in_specs=[HBM] * (n + m) + [SEM, SEM, ANY], out_specs=[HBM] * (n + m),
        out_shape=[pltpu.HBM(t.shape, t.dtype) for t in bufs],
        input_output_aliases={i: i for i in range(n + m)}, compiler_params=SIDE_EFFECT,
    )(*bufs, *started["sems"], after)
    return res[:n], res[n:]


def _to_all_copies(x_refs, out_refs, send_sems, recv_sems):
    x, y, c = _coords()
    copies = []
    for a in range(len(x_refs)):
        for k in range(N_DEV - 1):
            fx, fy, fc = ((k + 1) >> 2) & 1, ((k + 1) >> 1) & 1, (k + 1) & 1
            copies.append(pltpu.make_async_remote_copy(
                src_ref=x_refs[a], dst_ref=out_refs[a].at[4 * x + 2 * y + c], send_sem=send_sems.at[7 * a + k],
                recv_sem=recv_sems.at[7 * a + k], device_id=(_flip(x, fx), _flip(y, fy), _flip(c, fc)), device_id_type=MESH))
    return copies


def _fill_own_block(gathered, shard, me_idx, name):
    rows, cols = shard.shape
    tr = _tile(rows, 512, 16)

    def body(me_ref, g_ref, s_ref, o_ref):
        o_ref[0] = s_ref[...]

    grid_spec = pltpu.PrefetchScalarGridSpec(
        num_scalar_prefetch=1, grid=(rows // tr,),
        in_specs=[ANY, pl.BlockSpec((tr, cols), lambda i, me: (i, 0))],
        out_specs=pl.BlockSpec((1, tr, cols), lambda i, me: (me[0], i, 0)))
    return _call(
        body, name=name, grid_spec=grid_spec, out_shape=jax.ShapeDtypeStruct(gathered.shape, gathered.dtype),
        input_output_aliases={1: 0}, compiler_params=_params(("arbitrary",)),
    )(me_idx, gathered, shard)


def _rs_chip_sum(partial, got, chip_idx, name, part=0, n_parts=1, dst=None):
    _, rows, cols = partial.shape
    tr = _tile(rows, 256, 16)
    steps = rows // tr
    n_dst = 0 if dst is None else 1

    def body(p_idx_ref, p_ref, got_ref, *refs):
        refs[n_dst][...] = ((p_ref[0].astype(F32) + got_ref[0].astype(F32)) + got_ref[1].astype(F32)) + got_ref[2].astype(F32)

    grid_spec = pltpu.PrefetchScalarGridSpec(
        num_scalar_prefetch=1, grid=(steps,),
        in_specs=[pl.BlockSpec((1, tr, cols), lambda i, p_ref: (p_ref[0], i, 0)),
                  pl.BlockSpec((3, tr, cols), lambda i, p_ref: (0, i, 0))] + [ANY] * n_dst,
        out_specs=pl.BlockSpec((tr, cols), lambda i, p_ref: (part * steps + i, 0)))
    return _call(
        body, name=name, grid_spec=grid_spec, out_shape=jax.ShapeDtypeStruct((n_parts * rows, cols), F32),
        input_output_aliases={3: 0} if n_dst else {}, compiler_params=_params(("parallel",)),
    )(chip_idx, partial, got, *([] if dst is None else [dst]))


def _rs_begin(gs, tag, split):
    c_idx = jnp.reshape(lax.axis_index("c"), (1,)).astype(jnp.int32)
    gots = _rs_to_sibling(gs, "rs_to_sibling_" + tag)
    partials = [_rs_pair_sum(g, got, c_idx, "rs_pair_sum_%s%d" % (tag, a)) for a, (g, got) in enumerate(zip(gs, gots))]
    if not split:
        return dict(partials=partials, gots=_rs_to_chips(partials, "rs_to_chips_" + tag))
    lands = [jax.ShapeDtypeStruct((3,) + p.shape[1:], p.dtype) for p in partials]
    return _split_start(_to_chips_copies, partials, lands, 3 * len(partials), "rs_to_chips_start_" + tag)


def _rs_finish(begun, tag, after=None, part=0, n_parts=1, dsts=None):
    x, y, _ = _coords()
    chip_idx = jnp.reshape(2 * x + y, (1,)).astype(jnp.int32)
    if "gots" in begun:
        partials, gots = begun["partials"], begun["gots"]
    else:
        partials, gots = _split_wait(_to_chips_copies, begun, after, "rs_to_chips_wait_" + tag)
    return [_rs_chip_sum(p, got, chip_idx, "rs_chip_sum_%s%d" % (tag, a), part, n_parts, None if dsts is None else dsts[a])
            for a, (p, got) in enumerate(zip(partials, gots))]


RUNS = 3
RUN_FIELDS = 5


def _lane_gather_table(src_of):
    n_blocks = src_of.shape[0] // LANES
    tab = np.zeros((n_blocks + 1, RUNS, RUN_FIELDS), np.int32)
    for t in range(n_blocks):
        runs = []
        for lane in range(LANES):
            slab, col = (int(v) for v in src_of[t * LANES + lane])
            if slab < 0:
                continue
            key = (slab, col // LANES, col % LANES - lane)
            if runs and runs[-1][0] == key and runs[-1][2] == lane:
                runs[-1][2] = lane + 1
            else:
                runs.append([key, lane, lane + 1])
        assert len(runs) <= RUNS
        slots = [None] * RUNS
        for key, lo, hi in sorted(runs, key=lambda r: r[0][:2]):
            e = key[1] % 2 if slots[key[1] % 2] is None else slots.index(None)
            slots[e] = (key[0], key[1], key[2], lo, hi)
        for e in range(RUNS):
            tab[t, e] = slots[e] if slots[e] is not None else (tab[t - 1, e, 0], tab[t - 1, e, 1], 0, 0, 0) if t else tab[t, e]
    tab[n_blocks, :, :2] = tab[n_blocks - 1, :, :2]
    return tab.reshape(-1)


def _place_run(tab_ref, t, e, block, under):
    base = (t * RUNS + e) * RUN_FIELDS
    shift, lo, hi = tab_ref[base + 2], tab_ref[base + 3], tab_ref[base + 4]
    lane = lax.broadcasted_iota(jnp.int32, (1, LANES), 1)
    return jnp.where((lane >= lo) & (lane < hi), pltpu.roll(block.astype(F32), (LANES - shift) % LANES, 1), under)


def _lane_gather_cols(src, table, out_slabs, out_width, name):
    _, rows, _ = src.shape
    blocks_per_slab = -(-out_width // LANES)

    def body(tab_ref, *refs):
        t, o_ref = pl.program_id(0), refs[RUNS]
        o_ref[0] = _place_run(tab_ref, t, 1, refs[1][0], _place_run(tab_ref, t, 0, refs[0][0], 0.0)).astype(BF16)
        last = (t * RUNS + RUNS - 1) * RUN_FIELDS

        @pl.when(tab_ref[last + 4] > tab_ref[last + 3])
        def _():
            o_ref[0] = _place_run(tab_ref, t, RUNS - 1, refs[RUNS - 1][0], o_ref[0].astype(F32)).astype(BF16)

    def src_spec(e):
        return pl.BlockSpec((1, rows, LANES), lambda t, tab: (tab[(t * RUNS + e) * RUN_FIELDS], 0, tab[(t * RUNS + e) * RUN_FIELDS + 1]))

    grid_spec = pltpu.PrefetchScalarGridSpec(
        num_scalar_prefetch=1, grid=(out_slabs * blocks_per_slab,), in_specs=[src_spec(e) for e in range(RUNS)],
        out_specs=pl.BlockSpec((1, rows, LANES), lambda t, tab: (t // blocks_per_slab, 0, t % blocks_per_slab)))
    return _call(
        body, name=name, grid_spec=grid_spec, out_shape=jax.ShapeDtypeStruct((out_slabs, rows, out_width), BF16),
        compiler_params=_params(("arbitrary",)),
    )(jnp.asarray(table), src, src, src)


def _all_reduce_small(vec):
    rows, cols = vec.shape

    def body(v_ref, o_ref, buf, send_sems, recv_sems):
        x, y, c = _coords()
        me = 4 * x + 2 * y + c
        buf[me] = v_ref[...]
        copies = []
        for k in range(N_DEV - 1):
            fx, fy, fc = ((k + 1) >> 2) & 1, ((k + 1) >> 1) & 1, (k + 1) & 1
            cp = pltpu.make_async_remote_copy(
                src_ref=v_ref, dst_ref=buf.at[me], send_sem=send_sems.at[k], recv_sem=recv_sems.at[k],
                device_id=(_flip(x, fx), _flip(y, fy), _flip(c, fc)), device_id_type=MESH)
            cp.start()
            copies.append(cp)
        for cp in copies:
            cp.wait()
        total = buf[0]
        for j in range(1, N_DEV):
            total = total + buf[j]
        o_ref[...] = total

    vmem = pl.BlockSpec(memory_space=pltpu.VMEM)
    return _call(
        body, name="all_reduce_small", in_specs=[vmem], out_specs=vmem,
        out_shape=jax.ShapeDtypeStruct((rows, cols), F32),
        scratch_shapes=[pltpu.VMEM((N_DEV, rows, cols), F32), pltpu.SemaphoreType.DMA((N_DEV - 1,)),
                        pltpu.SemaphoreType.DMA((N_DEV - 1,))],
    )(vec)


def _w_in_column_maps(ns, o_ba, n_logit, n_main, n_all):
    own = np.arange(N_DEV * ns)
    work_of_own = np.where(own < o_ba, own, np.where(own < o_ba + n_logit, n_main + own - o_ba, own - n_logit))
    to_work = np.full((n_all, 2), -1, np.int64)
    to_work[work_of_own, 0] = own // ns
    to_work[work_of_own, 1] = own % ns
    slab_width = -(-ns // LANES) * LANES
    to_own = np.full((N_DEV, slab_width, 2), -1, np.int64)
    to_own[:, :ns, 0] = 0
    to_own[:, :ns, 1] = work_of_own.reshape(N_DEV, ns)
    return to_work, to_own.reshape(-1, 2)


def kernel(x, meta_tokens, norm_w, w_in, conv_w, A_log, dt_bias, pool_mix, pool_scale, dn_norm_w, w_pool_out, w_dn_out, w_o, final_norm_w, loss_target, m_meta_tokens, m_norm_w, m_w_in, m_conv_w, m_A_log, m_dt_bias, m_pool_mix, m_pool_scale, m_dn_norm_w, m_w_pool_out, m_w_dn_out, m_w_o, m_final_norm_w, v_meta_tokens, v_norm_w, v_w_in, v_conv_w, v_A_log, v_dt_bias, v_pool_mix, v_pool_scale, v_dn_norm_w, v_w_pool_out, v_w_dn_out, v_w_o, v_final_norm_w):
    seq, d = x.shape[1], x.shape[2]
    n_meta = meta_tokens.shape[0]
    n_heads, hd = A_log.shape[-1], dn_norm_w.shape[-1]
    dn = n_heads * hd
    pw, ng = pool_scale.shape[-1], pool_mix.shape[1]
    pg = pw // ng
    kw = conv_w.shape[1]
    pad = (-n_meta) % CHUNK
    x0 = pad + n_meta
    lp = x0 + seq
    ns = w_in.shape[-1]
    in_cols = N_DEV * ns
    o_q, o_k, o_v, o_zd = 2 * pw, 2 * pw + dn, 2 * pw + 2 * dn, 2 * pw + 3 * dn
    o_ba = 2 * pw + 4 * dn
    o_gp, o_gd = o_ba, o_ba + d
    n_main = o_gd + d
    n_all = n_main + 2 * LANES
    assert lp % CHUNK == 0 and in_cols == n_main + 2 * n_heads and 2 * n_heads <= LANES and hd == LANES
    cs, ms = conv_w.shape[-1], meta_tokens.shape[-1]
    mr = pool_mix.shape[2]
    assert ms == pg and cs % pg == 0
    to_work, to_own = _w_in_column_maps(ns, o_ba, 2 * n_heads, n_main, n_all)
    cols_major = lambda t: jnp.transpose(t, (1, 0, 2)).reshape(t.shape[1], N_DEV * t.shape[2])

    mix_g, conv_g, meta_g = _all_gather([pool_mix[0].reshape(ng * mr, pg).astype(BF16), conv_w[0], meta_tokens])
    win_g = _all_gather_tree(w_in[0].astype(BF16), after=meta_g)
    late_shards = [w_pool_out[0].astype(BF16), w_dn_out[0].astype(BF16), w_o[0].astype(BF16)]
    late_weights = _split_start(_to_all_copies, late_shards, [jax.ShapeDtypeStruct((N_DEV,) + s.shape, BF16) for s in late_shards],
                                (N_DEV - 1) * len(late_shards), "gather_out_proj_start", after=win_g)
    norm_w_in = norm_w + late_weights["token"][0, 0]
    w_all = _lane_gather_cols(win_g, _lane_gather_table(to_work), 1, n_all, "w_in_to_work").reshape(d, n_all)
    mix_f = jnp.transpose(mix_g.reshape(N_DEV, ng, mr, pg), (1, 0, 2, 3)).reshape(ng, pg, pg)
    conv_f = cols_major(conv_g)
    meta_f = cols_major(meta_g)

    h0, xn = _norm_in_fwd(x[0], meta_f, norm_w_in, pad)
    proj = _matmul(xn, w_all, NN, F32, lp, 768, 2048, "proj")
    y_pool = _pool_fwd(proj, mix_f, pool_scale, pad)
    conv_q, conv_k, conv_v = (conv_f[:, i * dn:(i + 1) * dn] for i in range(3))
    qn = _conv_fwd(proj, o_q, conv_q, hd, float(hd) ** -0.5, "conv_q_fwd")
    kn = _conv_fwd(proj, o_k, conv_k, hd, 1.0, "conv_k_fwd")
    vv = _conv_fwd(proj, o_v, conv_v, hd, None, "conv_v_fwd")
    logit_lanes = (n_heads, LANES - 2 * n_heads)
    prm = jnp.pad(A_log, ((0, 7), logit_lanes)) + jnp.pad(dt_bias, ((1, 6), logit_lanes))
    y_dn, hist, tmats = _chunk_fwd(qn, kn, vv, proj, o_zd, n_main, prm, dn_norm_w, n_heads, pad)
    me_idx = jnp.reshape(4 * lax.axis_index("x") + 2 * lax.axis_index("y") + lax.axis_index("c"), (1,)).astype(jnp.int32)
    _, landed = _split_wait(_to_all_copies, late_weights, y_dn, "gather_out_proj_wait")
    wpo_g, wdn_g, wo_g = (_fill_own_block(g, s, me_idx, "own_block_%d" % i) for i, (g, s) in enumerate(zip(landed, late_shards)))
    wpo_f = cols_major(wpo_g)
    wdn_f = wdn_g.reshape(dn, d)
    wo_f = wo_g.reshape(d, d)
    p_out = _matmul(y_pool, wpo_f, NN, F32, 1056, 1024, 1024, "pool_out")
    q_out = _matmul(y_dn, wdn_f, NN, F32, 1056, 1024, 2048, "dn_out")
    merged = _merge_fwd(p_out, q_out, proj, o_gp, o_gd)
    mo = _matmul(merged, wo_f, NN, F32, 1056, 1024, 2048, "w_o_fwd")
    dh1, d_fw, loss_part = _final_loss(h0, mo, final_norm_w.reshape(1, d), loss_target[0], x0)

    d_merged = _matmul(dh1, wo_f, NT, F32, 1056, 1024, 1024, "w_o_bwd_x")
    g_wo = _matmul(merged, dh1, TN, BF16, 1024, 1024, lp, "w_o_bwd_w")
    d_p, d_q, d_gp, d_gd = _merge_bwd(p_out, q_out, proj, o_gp, o_gd, d_merged)
    d_ypool = _matmul(d_p, wpo_f, NT, F32, 1056, 1024, 2048, "pool_out_bwd_x")
    g_wpo = _matmul(y_pool.T, d_p, NN, BF16, 1024, 1024, lp, "pool_out_bwd_w", col_blocks=N_DEV)
    d_ydn = _matmul(d_q, wdn_f, NT, F32, 1056, 1024, 2048, "dn_out_bwd_x")
    g_wdn = _matmul(y_dn, d_q, TN, BF16, 1024, 1024, lp, "dn_out_bwd_w")
    rs_early = _rs_begin([g_wpo, g_wdn.reshape(N_DEV, dn // N_DEV, d), g_wo.reshape(N_DEV, d // N_DEV, d)], "early", split=True)
    started = rs_early["token"][0, 0]
    d_u, d_zp, g_mix, g_pscale = _pool_bwd(proj, mix_f, pool_scale + started, d_ypool, pad)
    d_proj = lax.empty((lp, n_all), BF16)
    d_qn, d_kn, d_vv, d_ba, d_proj, d_prm, g_dnw = _chunk_bwd(qn, kn, vv, proj, o_zd, n_main, prm + started, dn_norm_w, hist, tmats,
                                                              d_ydn, n_heads, pad, d_proj)
    d_proj, g_cq = _conv_bwd(proj, o_q, conv_q, d_qn, hd, float(hd) ** -0.5, pad, "conv_q_bwd", d_proj)
    d_proj, g_ck = _conv_bwd(proj, o_k, conv_k, d_kn, hd, 1.0, pad, "conv_k_bwd", d_proj)
    d_proj, g_cv = _conv_bwd(proj, o_v, conv_v, d_vv, hd, None, pad, "conv_v_bwd", d_proj)
    for off, piece in ((0, d_u), (pw, d_zp), (o_gp, d_gp), (o_gd, d_gd), (n_main, d_ba.astype(BF16)), (n_main + LANES, jnp.zeros((lp, LANES), BF16))):
        d_proj = lax.dynamic_update_slice(d_proj, piece, (0, off))
    xn_t, rs_late, token = xn.T, [], None
    for half in range(2):
        rows = slice(half * (d // 2), (half + 1) * (d // 2))
        g_wall = _matmul(xn_t[rows], d_proj, NN, BF16, 1024, 768, lp, "w_in_bwd_w_%d" % half, after=token)
        g_win = _lane_gather_cols(g_wall.reshape(1, d // 2, n_all), _lane_gather_table(to_own), N_DEV, ns, "w_in_grad_to_own_%d" % half)
        rs_late.append(_rs_begin([g_win], "late%d" % half, split=True))
        token = rs_late[-1]["token"]
    d_xn = _matmul(d_proj, w_all, NT, F32, lp, 512, 2432, "w_in_bwd_x", after=token)
    d_head, grad_x, g_nw = _norm_in_bwd(h0, norm_w, d_xn, dh1, x0)
    grad_x = grad_x[None]

    by_cols = lambda t: jnp.transpose(t.reshape(t.shape[0], N_DEV, t.shape[1] // N_DEV), (1, 0, 2))
    g_conv = by_cols(jnp.concatenate([g_cq, g_ck, g_cv], axis=1)).reshape(N_DEV, kw * cs // pg, pg)
    conv_rows = -(-g_conv.shape[1] // 16) * 16
    g_small = jnp.concatenate(
        [jnp.transpose(g_mix.reshape(ng, N_DEV, mr, pg), (1, 0, 2, 3)).reshape(N_DEV, ng * mr, pg), by_cols(d_head[pad:x0]),
         jnp.pad(g_conv, ((0, 0), (0, conv_rows - g_conv.shape[1]), (0, 0)))], axis=1).astype(BF16)
    r_small, = _rs_finish(_rs_begin([g_small], "small", split=False), "small")
    r_mix, r_meta = r_small[:ng * mr], r_small[ng * mr:ng * mr + n_meta]
    r_conv = r_small[ng * mr + n_meta:ng * mr + n_meta + kw * cs // pg]
    r_wpo, r_wdn, r_wo = _rs_finish(rs_early, "early", after=r_small)

    small = [g_nw[0], d_fw[0], g_pscale[0], g_dnw[0], d_prm[0], d_prm[1], loss_part[0]]
    s_sizes = [t.shape[0] for t in small]
    s_cols = -(-sum(s_sizes) // (8 * LANES)) * LANES
    s_vec = jnp.concatenate(small + [jnp.zeros((8 * s_cols - sum(s_sizes),), F32)]).reshape(8, s_cols)
    s_red = _all_reduce_small(s_vec)
    s_sum = s_red.reshape(-1)
    r_win = None
    for half, begun in enumerate(rs_late):
        r_win = _rs_finish(begun, "late%d" % half, after=s_red, part=half, n_parts=2, dsts=r_win)
    r_win, = r_win
    s_offs = [sum(s_sizes[:i]) for i in range(len(s_sizes))]
    s_take = lambda i, n=None, o=0: s_sum[s_offs[i] + o:s_offs[i] + o + (s_sizes[i] if n is None else n)]

    grads = {
        "meta_tokens": r_meta, "norm_w": s_take(0).reshape(norm_w.shape),
        "w_in": r_win.reshape(w_in.shape), "conv_w": r_conv.reshape(conv_w.shape),
        "A_log": s_take(4, n_heads, n_heads).reshape(A_log.shape), "dt_bias": s_take(5, n_heads, n_heads).reshape(dt_bias.shape),
        "pool_mix": r_mix.reshape(pool_mix.shape), "pool_scale": s_take(2).reshape(pool_scale.shape),
        "dn_norm_w": s_take(3).reshape(dn_norm_w.shape), "w_pool_out": r_wpo.reshape(w_pool_out.shape),
        "w_dn_out": r_wdn.reshape(w_dn_out.shape), "w_o": r_wo.reshape(w_o.shape),
        "final_norm_w": s_take(1).reshape(final_norm_w.shape),
    }
    loss = s_take(6, 1)[0]

    weights = dict(meta_tokens=meta_tokens, norm_w=norm_w, w_in=w_in, conv_w=conv_w, A_log=A_log, dt_bias=dt_bias,
                   pool_mix=pool_mix, pool_scale=pool_scale, dn_norm_w=dn_norm_w, w_pool_out=w_pool_out, w_dn_out=w_dn_out,
                   w_o=w_o, final_norm_w=final_norm_w)
    m_in = dict(meta_tokens=m_meta_tokens, norm_w=m_norm_w, w_in=m_w_in, conv_w=m_conv_w, A_log=m_A_log, dt_bias=m_dt_bias,
                pool_mix=m_pool_mix, pool_scale=m_pool_scale, dn_norm_w=m_dn_norm_w, w_pool_out=m_w_pool_out,
                w_dn_out=m_w_dn_out, w_o=m_w_o, final_norm_w=m_final_norm_w)
    v_in = dict(meta_tokens=v_meta_tokens, norm_w=v_norm_w, w_in=v_w_in, conv_w=v_conv_w, A_log=v_A_log, dt_bias=v_dt_bias,
                pool_mix=v_pool_mix, pool_scale=v_pool_scale, dn_norm_w=v_dn_norm_w, w_pool_out=v_w_pool_out,
                w_dn_out=v_w_dn_out, w_o=v_w_o, final_norm_w=v_final_norm_w)
    names = list(weights)
    upd = {n: _adamw(weights[n], grads[n], m_in[n], v_in[n], "adamw_" + n) for n in names}
    return (loss, grad_x, *[grads[n] for n in names], *[upd[n][0] for n in names], *[upd[n][1] for n in names],
            *[upd[n][2] for n in names])
```

```python
import functools
import math

import jax
import jax.numpy as jnp
import numpy as np
from jax import lax
from jax.experimental import pallas as pl
from jax.experimental.pallas import tpu as pltpu

F32 = jnp.float32
BF16 = jnp.bfloat16
HIGHEST = lax.Precision.HIGHEST
MESH = pl.DeviceIdType.MESH

CHUNK = 64
NORM_EPS = 1e-6
POOL_WINDOWS = (2, 4, 8, 16)
ADAM_LR, ADAM_B1, ADAM_B2, ADAM_EPS, ADAM_WD, ADAM_STEP = 0.001, 0.9, 0.999, 1e-08, 0.01, 10
N_DEV = 8
LANES = 128
VMEM_LIMIT = 48 * 1024 * 1024

NN = (((1,), (0,)), ((), ()))
NT = (((1,), (1,)), ((), ()))
TN = (((0,), (0,)), ((), ()))


def _call(body, **kw):
    return pl.pallas_call(body, **kw)


def _params(sem=None):
    return pltpu.CompilerParams(dimension_semantics=sem, vmem_limit_bytes=VMEM_LIMIT)


def _tile(n, pref, align):
    for d in range(min(pref, n), 0, -1):
        if n % d == 0 and d % align == 0:
            return d
    return n


def _dot(a, b, dims=NN, precision=None):
    return lax.dot_general(a, b, dims, precision=precision, preferred_element_type=F32)


def _sigmoid(x):
    return 0.5 * jnp.tanh(0.5 * x) + 0.5


def _silu(x):
    return x * _sigmoid(x)


def _softplus(x):
    return jnp.maximum(x, 0.0) + jnp.log(1.0 + jnp.exp(-jnp.abs(x)))


def _rmsnorm(x, w):
    return x * lax.rsqrt(jnp.mean(x * x, axis=-1, keepdims=True) + NORM_EPS) * w


def _shift_down(x, j, row):
    if j == 0:
        return x
    return jnp.where(row >= j, pltpu.roll(x, j, 0), 0.0)


def _shift_up(x, j, row):
    if j == 0:
        return x
    n = x.shape[0]
    return jnp.where(row < n - j, pltpu.roll(x, n - j, 0), 0.0)


def _matmul(a, b, dims, out_dtype, tm, tn, tk, name, col_blocks=None, after=None):
    ta = dims == TN
    tb = dims == NT
    m, kdim = (a.shape[1], a.shape[0]) if ta else a.shape
    n = b.shape[0] if tb else b.shape[1]
    if col_blocks:
        tn = n // col_blocks
    tm, tn, tk = _tile(m, tm, 8), _tile(n, tn, LANES), _tile(kdim, tk, LANES if not ta else 16)
    nk = kdim // tk

    n_extra = 0 if after is None else 1

    def body(a_ref, b_ref, *refs):
        o_ref, scratch = refs[n_extra], refs[n_extra + 1:]
        part = _dot(a_ref[...].astype(BF16), b_ref[...].astype(BF16), dims)
        if nk == 1:
            o_ref[...] = part.astype(o_ref.dtype).reshape(o_ref.shape)
            return
        acc_ref, = scratch
        k = pl.program_id(2)

        @pl.when(k == 0)
        def _():
            acc_ref[...] = part

        @pl.when(k > 0)
        def _():
            acc_ref[...] += part

        @pl.when(k == nk - 1)
        def _():
            o_ref[...] = acc_ref[...].astype(o_ref.dtype).reshape(o_ref.shape)

    a_spec = pl.BlockSpec((tk, tm), lambda i, j, k: (k, i)) if ta else pl.BlockSpec((tm, tk), lambda i, j, k: (i, k))
    b_spec = pl.BlockSpec((tn, tk), lambda i, j, k: (j, k)) if tb else pl.BlockSpec((tk, tn), lambda i, j, k: (k, j))
    if col_blocks:
        out_spec = pl.BlockSpec((1, tm, tn), lambda i, j, k: (j, i, 0))
        out_shape = jax.ShapeDtypeStruct((col_blocks, m, tn), out_dtype)
    else:
        out_spec = pl.BlockSpec((tm, tn), lambda i, j, k: (i, j))
        out_shape = jax.ShapeDtypeStruct((m, n), out_dtype)
    return _call(
        body, name=name, grid=(m // tm, n // tn, nk),
        in_specs=[a_spec, b_spec] + [ANY] * n_extra, out_specs=out_spec, out_shape=out_shape,
        scratch_shapes=[] if nk == 1 else [pltpu.VMEM((tm, tn), F32)],
        compiler_params=_params(("parallel", "parallel", "arbitrary")),
    )(a, b, *([] if after is None else [after]))


def _norm_in_fwd(x2d, meta, w, pad):
    seq, d = x2d.shape
    x0 = pad + meta.shape[0]
    assert x0 % 16 == 0
    lp = x0 + seq
    tr = _tile(seq, 512, 16)
    vec = pl.BlockSpec((1, d), lambda i: (0, 0))
    shapes = [jax.ShapeDtypeStruct((lp, d), F32), jax.ShapeDtypeStruct((lp, d), BF16)]

    def body(x_ref, w_ref, h_ref, o_ref):
        h_ref[...] = x_ref[...]
        o_ref[...] = _rmsnorm(x_ref[...], w_ref[...]).astype(BF16)

    def head(m_ref, w_ref, h_in_ref, o_in_ref, h_ref, o_ref):
        h = jnp.concatenate([jnp.zeros((pad, d), F32), m_ref[...]], axis=0) if pad else m_ref[...]
        h_ref[...] = h
        o_ref[...] = _rmsnorm(h, w_ref[...]).astype(BF16)

    rows = _rows_after(x0, tr, d)
    h0, xn = _call(
        body, name="norm_in_fwd", grid=(seq // tr,), in_specs=[pl.BlockSpec((tr, d), lambda i: (i, 0)), vec],
        out_specs=[rows, rows], out_shape=shapes, compiler_params=_params(("parallel",)),
    )(x2d, w)
    first = pl.BlockSpec((x0, d), lambda i: (0, 0))
    return _call(
        head, name="norm_in_fwd_head", grid=(1,), in_specs=[pl.BlockSpec(meta.shape, lambda i: (0, 0)), vec, ANY, ANY],
        out_specs=[first, first], out_shape=shapes, input_output_aliases={2: 0, 3: 1}, compiler_params=_params(("arbitrary",)),
    )(meta, w, h0, xn)


def _rows_after(x0, tr, d):
    step = math.gcd(x0, tr)
    return pl.BlockSpec((pl.Element(tr), pl.Element(d)), lambda i: (pl.multiple_of(x0 + tr * i, step), 0))


def _norm_in_bwd(h0, w, dxn, dh1, x0):
    lp, d = h0.shape
    tr = _tile(lp - x0, 512, 8)
    vec = pl.BlockSpec((1, d), lambda i: (0, 0))

    def make(body_rows, first):
        def body(h_ref, w_ref, da_ref, dh1_ref, dh_ref, dw_ref):
            _, vjp = jax.vjp(_rmsnorm, h_ref[...], w_ref[...])
            dh, dw = vjp(da_ref[...])
            dh_ref[...] = dh + dh1_ref[...]

            @pl.when(pl.program_id(0) == 0)
            def _():
                dw_ref[...] = jnp.zeros_like(dw_ref)

            dw_ref[...] += dw

        rows_in = pl.BlockSpec((x0, d), lambda i: (0, 0)) if first else _rows_after(x0, tr, d)
        return _call(
            body, name="norm_in_bwd_head" if first else "norm_in_bwd", grid=(1 if first else (lp - x0) // tr,),
            in_specs=[rows_in, vec, rows_in, rows_in],
            out_specs=[pl.BlockSpec((body_rows, d), lambda i: (i, 0)), vec],
            out_shape=[jax.ShapeDtypeStruct((x0 if first else lp - x0, d), F32), jax.ShapeDtypeStruct((1, d), F32)],
            compiler_params=_params(("arbitrary",)),
        )(h0, w, dxn, dh1)

    d_head, dw_head = make(x0, True)
    grad_x, dw_rest = make(tr, False)
    return d_head, grad_x, dw_head + dw_rest


def _final_loss(h0, mo, fw, tgt, x0):
    lp, d = h0.shape
    tr = _tile(lp - x0, 512, 8)

    def body(h_ref, mo_ref, fw_ref, t_ref, dh_ref, dw_ref, loss_ref):
        tgt_v = t_ref[...]

        def loss_fn(h1, w):
            err = _rmsnorm(h1, w) - tgt_v
            return 0.5 * jnp.sum(jnp.mean(err * err, axis=-1, keepdims=True), axis=0, keepdims=True)

        loss, vjp = jax.vjp(loss_fn, h_ref[...] + mo_ref[...], fw_ref[...])
        dh, dw = vjp(jnp.ones((1, 1), F32))
        dh_ref[...] = dh

        @pl.when(pl.program_id(0) == 0)
        def _():
            dw_ref[...] = jnp.zeros_like(dw_ref)
            loss_ref[...] = jnp.zeros_like(loss_ref)

        dw_ref[...] += dw
        loss_ref[...] += jnp.broadcast_to(loss, loss_ref.shape)

    def zero_head(dh_in_ref, dh_ref):
        dh_ref[...] = jnp.zeros_like(dh_ref)

    rows = _rows_after(x0, tr, d)
    vec = pl.BlockSpec((1, d), lambda i: (0, 0))
    dh1, dw, loss = _call(
        body, name="final_loss", grid=((lp - x0) // tr,),
        in_specs=[rows, rows, vec, pl.BlockSpec((tr, d), lambda i: (i, 0))],
        out_specs=[rows, vec, pl.BlockSpec((8, LANES), lambda i: (0, 0))],
        out_shape=[jax.ShapeDtypeStruct((lp, d), F32), jax.ShapeDtypeStruct((1, d), F32), jax.ShapeDtypeStruct((8, LANES), F32)],
        compiler_params=_params(("arbitrary",)),
    )(h0, mo, fw, tgt)
    dh1 = _call(
        zero_head, name="final_loss_head", grid=(1,), in_specs=[ANY], out_specs=pl.BlockSpec((x0, d), lambda i: (0, 0)),
        out_shape=jax.ShapeDtypeStruct((lp, d), F32), input_output_aliases={0: 0}, compiler_params=_params(("arbitrary",)),
    )(dh1)
    return dh1, dw, loss


def _pool_select(parts, g):
    out = parts[-1]
    for gi in range(len(parts) - 2, -1, -1):
        out = jnp.where(g == gi, parts[gi], out)
    return out


def _pool_count(row, g, pad):
    win = _pool_select([jnp.full(row.shape, float(w), F32) for w in POOL_WINDOWS], g)
    return jnp.maximum(jnp.minimum((row - pad + 1).astype(F32), win), 1.0)


def _pooled(u, g, row, pad):
    sums, s, span = [], u, 1
    for w in POOL_WINDOWS:
        while span < w:
            s = s + _shift_down(s, span, row)
            span *= 2
        sums.append(s)
    return _pool_select(sums, g) / _pool_count(row, g, pad) - u


def _pooled_adjoint(dp, g, row, pad):
    e = dp / _pool_count(row, g, pad)
    sums, s, span = [], e, 1
    for w in POOL_WINDOWS:
        while span < w:
            s = s + _shift_up(s, span, row)
            span *= 2
        sums.append(s)
    return _pool_select(sums, g) - dp


def _pool_specs(lp, pg, ng, z_off):
    u_spec = pl.BlockSpec((lp, pg), lambda g: (0, g))
    z_spec = pl.BlockSpec((lp, pg), lambda g: (0, z_off + g))
    mix_spec = pl.BlockSpec((1, pg, pg), lambda g: (g, 0, 0))
    vec_spec = pl.BlockSpec((1, pg), lambda g: (0, g))
    return u_spec, z_spec, mix_spec, vec_spec


def _pool_fwd(proj, mix, scale, pad):
    lp = proj.shape[0]
    ng, pg, _ = mix.shape
    pw = ng * pg

    def body(u_ref, z_ref, mix_ref, sc_ref, y_ref):
        g = pl.program_id(0)
        row = lax.broadcasted_iota(jnp.int32, (lp, 1), 0)
        pooled = _pooled(u_ref[...], g, row, pad)
        mixed = _dot(pooled.astype(BF16), mix_ref[0])
        y_ref[...] = (mixed * sc_ref[...] * _silu(z_ref[...])).astype(BF16)

    u_spec, z_spec, mix_spec, vec_spec = _pool_specs(lp, pg, ng, pw // pg)
    return _call(
        body, name="pool_fwd", grid=(ng,), in_specs=[u_spec, z_spec, mix_spec, vec_spec], out_specs=u_spec,
        out_shape=jax.ShapeDtypeStruct((lp, pw), BF16), compiler_params=_params(("parallel",)),
    )(proj, proj, mix, scale)


def _pool_bwd(proj, mix, scale, dy, pad):
    lp = proj.shape[0]
    ng, pg, _ = mix.shape
    pw = ng * pg

    def body(u_ref, z_ref, mix_ref, sc_ref, dy_ref, du_ref, dz_ref, dmix_ref, dsc_ref):
        g = pl.program_id(0)
        row = lax.broadcasted_iota(jnp.int32, (lp, 1), 0)
        real = row >= pad
        z = z_ref[...]
        pooled = _pooled(u_ref[...], g, row, pad).astype(BF16)
        mixed = _dot(pooled, mix_ref[0])
        sig = _sigmoid(z)
        sz = z * sig
        dyv = dy_ref[...]
        dsc_ref[...] = jnp.sum(dyv * mixed * sz, axis=0, keepdims=True)
        d_sz = dyv * mixed * sc_ref[...]
        dz_ref[...] = jnp.where(real, d_sz * (sig + sz * (1.0 - sig)), 0.0).astype(BF16)
        d_mixed = (dyv * sc_ref[...] * sz).astype(BF16)
        dmix_ref[0] = _dot(pooled, d_mixed, TN)
        d_pooled = _dot(d_mixed, mix_ref[0], NT)
        du_ref[...] = jnp.where(real, _pooled_adjoint(d_pooled, g, row, pad), 0.0).astype(BF16)

    u_spec, z_spec, mix_spec, vec_spec = _pool_specs(lp, pg, ng, pw // pg)
    return _call(
        body, name="pool_bwd", grid=(ng,),
        in_specs=[u_spec, z_spec, mix_spec, vec_spec, u_spec], out_specs=[u_spec, u_spec, mix_spec, vec_spec],
        out_shape=[jax.ShapeDtypeStruct((lp, pw), BF16), jax.ShapeDtypeStruct((lp, pw), BF16),
                   jax.ShapeDtypeStruct((ng, pg, pg), F32), jax.ShapeDtypeStruct((1, pw), F32)],
        compiler_params=_params(("parallel",)),
    )(proj, proj, mix, scale, dy)


def _conv_pre(x, w, row):
    kw = w.shape[0]
    y = w[kw - 1:kw, :] * x
    for kk in range(kw - 1):
        y = y + w[kk:kk + 1, :] * _shift_down(x, kw - 1 - kk, row)
    return y


def _conv_post(y, out_scale):
    s = _silu(y)
    if out_scale is None:
        return s
    return s * lax.rsqrt(jnp.sum(s * s, axis=-1, keepdims=True) + NORM_EPS) * out_scale


def _conv_fwd(proj, col_off, w, hd, out_scale, name):
    lp = proj.shape[0]
    kw, width = w.shape
    blk0 = col_off // hd

    def body(x_ref, w_ref, o_ref):
        row = lax.broadcasted_iota(jnp.int32, (lp, 1), 0)
        o_ref[...] = _conv_post(_conv_pre(x_ref[...], w_ref[...], row), out_scale)

    return _call(
        body, name=name, grid=(width // hd,),
        in_specs=[pl.BlockSpec((lp, hd), lambda j: (0, blk0 + j)), pl.BlockSpec((kw, hd), lambda j: (0, j))],
        out_specs=pl.BlockSpec((lp, hd), lambda j: (0, j)),
        out_shape=jax.ShapeDtypeStruct((lp, width), F32), compiler_params=_params(("parallel",)),
    )(proj, w)


def _conv_bwd(proj, col_off, w, d_out, hd, out_scale, pad, name, dst):
    lp = proj.shape[0]
    kw, width = w.shape
    blk0 = col_off // hd

    def body(x_ref, w_ref, do_ref, dst_ref, dx_ref, dw_ref):
        row = lax.broadcasted_iota(jnp.int32, (lp, 1), 0)
        real = row >= pad
        x, wv = x_ref[...], w_ref[...]
        _, vjp = jax.vjp(functools.partial(_conv_post, out_scale=out_scale), _conv_pre(x, wv, row))
        dy = jnp.where(real, vjp(do_ref[...])[0], 0.0)
        dx = wv[kw - 1:kw, :] * dy
        dw_ref[kw - 1:kw, :] = jnp.sum(dy * x, axis=0, keepdims=True)
        for kk in range(kw - 1):
            ahead = _shift_up(dy, kw - 1 - kk, row)
            dx = dx + wv[kk:kk + 1, :] * ahead
            dw_ref[kk:kk + 1, :] = jnp.sum(ahead * x, axis=0, keepdims=True)
        dx_ref[...] = jnp.where(real, dx, 0.0).astype(BF16)

    col = pl.BlockSpec((lp, hd), lambda j: (0, j))
    at_off = pl.BlockSpec((lp, hd), lambda j: (0, blk0 + j))
    wspec = pl.BlockSpec((kw, hd), lambda j: (0, j))
    return _call(
        body, name=name, grid=(width // hd,),
        in_specs=[at_off, wspec, col, ANY], out_specs=[at_off, wspec],
        out_shape=[jax.ShapeDtypeStruct(dst.shape, BF16), jax.ShapeDtypeStruct((kw, width), F32)],
        input_output_aliases={3: 0}, compiler_params=_params(("parallel",)),
    )(proj, w, d_out, dst)


HEADS_PER_STEP = 16
HEADS_PER_STEP_BWD = 16


def _matmul_with_direct_vjp(dims, da_dims, db_dims, db_swapped):
    @jax.custom_vjp
    def mm(a, b):
        return _dot(a, b, dims)

    def fwd(a, b):
        return _dot(a, b, dims), (a, b)

    def bwd(res, g):
        a, b = res
        return _dot(g, b, da_dims) if not db_swapped[0] else _dot(b, g, da_dims), _dot(a, g, db_dims) if not db_swapped[1] else _dot(g, a, db_dims)

    mm.defvjp(fwd, bwd)
    return mm


_mm_nn = _matmul_with_direct_vjp(NN, NT, TN, (False, False))
_mm_nt = _matmul_with_direct_vjp(NT, NN, TN, (False, True))
_mm_tn = _matmul_with_direct_vjp(TN, NT, NN, (True, False))


def _each(fn, *lists):
    return [fn(*args) for args in zip(*lists)]


def _dot3_each(a_list, b_list, dims=NN):
    hi = lambda t: t.astype(BF16)
    lo = lambda t, t_hi: (t - t_hi.astype(F32)).astype(BF16)
    dot = lambda x, y: _dot(x, y, dims)
    a_hi, b_hi = _each(hi, a_list), _each(hi, b_list)
    a_lo, b_lo = _each(lo, a_list, a_hi), _each(lo, b_list, b_hi)
    hh, hl, lh = _each(dot, a_hi, b_hi), _each(dot, a_hi, b_lo), _each(dot, a_lo, b_hi)
    return _each(lambda x, y, w: x + (y + w), hh, hl, lh)


@jax.custom_vjp
def _unit_lower_inverse(lmats):
    c = lmats[0].shape[0]
    eye = lax.broadcasted_iota(jnp.int32, (c, c), 0) == lax.broadcasted_iota(jnp.int32, (c, c), 1)
    a = [-m for m in lmats]
    tmat = [jnp.where(eye, 1.0, 0.0).astype(F32) + m for m in a]
    span = 2
    while span < c:
        a = _dot3_each(a, a)
        tmat = _each(lambda t, u: t + u, tmat, _dot3_each(tmat, a))
        span *= 2
    return tuple(tmat)


def _unit_lower_inverse_fwd(lmats):
    tmats = _unit_lower_inverse(lmats)
    return tmats, tmats


def _unit_lower_inverse_bwd(tmats, cts):
    left = _each(lambda t, ct: _dot(t, ct, TN, HIGHEST), tmats, cts)
    return (tuple(_each(lambda m, t: -_dot(m, t, NT, HIGHEST), left, tmats)),)


_unit_lower_inverse.defvjp(_unit_lower_inverse_fwd, _unit_lower_inverse_bwd)


@jax.custom_vjp
def _known_inverse(lmats, tmats):
    return tmats


def _known_inverse_fwd(lmats, tmats):
    return tmats, tmats


def _known_inverse_bwd(tmats, cts):
    return _unit_lower_inverse_bwd(tmats, cts)[0], tuple(jnp.zeros_like(t) for t in tmats)


_known_inverse.defvjp(_known_inverse_fwd, _known_inverse_bwd)


def _chunk_math(states, q, k, v, ba, z, prm, nw, head0, rowmask, n_heads, tmats=None, keep_tmats=False):
    c = q.shape[0]
    heads = list(range(len(states)))
    hd = q.shape[1] // len(states)
    lane = lax.broadcasted_iota(jnp.int32, ba.shape, 1)
    sub = lax.broadcasted_iota(jnp.int32, (ba.shape[1], c), 0)
    ri = lax.broadcasted_iota(jnp.int32, (c, c), 0)
    ci = lax.broadcasted_iota(jnp.int32, (c, c), 1)
    last = lax.broadcasted_iota(jnp.int32, (c, 1), 0) == c - 1
    causal, strict = ri >= ci, ri > ci
    beta_all = _sigmoid(ba) * rowmask
    g_all = -jnp.exp(prm[0:1, :]) * _softplus(ba + prm[1:2, :]) * rowmask
    gcum_all = _dot(jnp.where(causal, 1.0, 0.0).astype(F32), g_all, precision=HIGHEST)
    gcum_t = gcum_all.T
    split = lambda t: [t[:, j * hd:(j + 1) * hd] for j in heads]
    qs, ks, vs, zs = split(q), split(k), split(v), split(z)
    beta = [jnp.sum(jnp.where(lane == head0 + j, beta_all, 0.0), axis=1, keepdims=True) for j in heads]
    gcum = [jnp.sum(jnp.where(lane == n_heads + head0 + j, gcum_all, 0.0), axis=1, keepdims=True) for j in heads]
    grow = [jnp.sum(jnp.where(sub == n_heads + head0 + j, gcum_t, 0.0), axis=0, keepdims=True) for j in heads]
    glast = _each(lambda gc: jnp.sum(jnp.where(last, gc, 0.0), axis=0, keepdims=True), gcum)
    decay = _each(lambda gc, gr: jnp.where(causal, jnp.exp(jnp.where(causal, gc - gr, 0.0)), 0.0), gcum, grow)
    eg = _each(jnp.exp, gcum)
    k_beta = _each(jnp.multiply, ks, beta)
    kk = _each(_mm_nt, k_beta, ks)
    lmats = tuple(_each(lambda m, dc: jnp.where(strict, m * dc, 0.0), kk, decay))
    tmat = list(_unit_lower_inverse(lmats) if tmats is None else _known_inverse(lmats, tuple(tmats)))
    u_c = _each(_mm_nn, tmat, _each(jnp.multiply, vs, beta))
    w_c = _each(_mm_nn, tmat, _each(jnp.multiply, k_beta, eg))
    qk = _each(lambda a, b, dc: jnp.where(causal, _mm_nt(a, b) * dc, 0.0), qs, ks, decay)
    v_new = _each(lambda u, w, s: u - _mm_nn(w, s), u_c, w_c, list(states))
    o = _each(lambda a, e, s, m, vn: _mm_nn(a * e, s) + _mm_nn(m, vn), qs, eg, list(states), qk, v_new)
    k_dec = _each(lambda a, gl, gc: a * jnp.exp(gl - gc), ks, glast, gcum)
    new_states = _each(lambda s, gl, kd, vn: s * jnp.exp(gl) + _mm_tn(kd, vn), list(states), glast, k_dec, v_new)
    ys = _each(lambda oj, zj: _rmsnorm(oj, nw) * _silu(zj), o, zs)
    if keep_tmats:
        return jnp.concatenate(ys, axis=1), tuple(new_states), tuple(tmat)
    return jnp.concatenate(ys, axis=1), tuple(new_states)


def _chunk_specs(nc, hd, n_heads, z_off, ba_off, rev):
    cidx = (lambda c: nc - 1 - c) if rev else (lambda c: c)
    hb = min(HEADS_PER_STEP_BWD if rev else HEADS_PER_STEP, n_heads)
    assert n_heads % hb == 0 and z_off % (hb * hd) == 0 and ba_off % LANES == 0
    blk = lambda off: pl.BlockSpec((CHUNK, hb * hd), lambda c, g: (cidx(c), off + g))
    ba_spec = lambda off: pl.BlockSpec((CHUNK, LANES), lambda c, g: (cidx(c), off // LANES))
    prm_spec = pl.BlockSpec((8, LANES), lambda c, g: (0, 0))
    nw_spec = pl.BlockSpec((1, hd), lambda c, g: (0, 0))
    st_spec = pl.BlockSpec((1, hb, hd, hd), lambda c, g: (cidx(c), g, 0, 0))
    return blk, ba_spec, prm_spec, nw_spec, st_spec, blk(z_off // (hb * hd))


def _rowmask(chunk_idx, pad):
    row = chunk_idx * CHUNK + lax.broadcasted_iota(jnp.int32, (CHUNK, 1), 0)
    return jnp.where(row >= pad, 1.0, 0.0).astype(F32)


def _chunk_fwd(qn, kn, vv, proj, z_off, ba_off, prm, nw, n_heads, pad):
    lp, dn = qn.shape
    hd = dn // n_heads
    nc = lp // CHUNK
    hb = min(HEADS_PER_STEP, n_heads)

    def body(q_ref, k_ref, v_ref, ba_ref, z_ref, prm_ref, nw_ref, y_ref, hist_ref, tm_ref, st_ref):
        c, g = pl.program_id(0), pl.program_id(1)

        @pl.when(c == 0)
        def _():
            for j in range(hb):
                st_ref[g * hb + j] = jnp.zeros((hd, hd), F32)

        states = tuple(st_ref[g * hb + j] for j in range(hb))
        for j in range(hb):
            hist_ref[0, j] = states[j]
        y, new_states, tmats = _chunk_math(states, q_ref[...], k_ref[...], v_ref[...], ba_ref[...], z_ref[...], prm_ref[...],
                                           nw_ref[...], g * hb, _rowmask(c, pad), n_heads, keep_tmats=True)
        y_ref[...] = y.astype(BF16)
        for j in range(hb):
            st_ref[g * hb + j] = new_states[j]
            tm_ref[0, j] = tmats[j]

    blk, ba_spec, prm_spec, nw_spec, st_spec, z_spec = _chunk_specs(nc, hd, n_heads, z_off, ba_off, False)
    tm_spec = pl.BlockSpec((1, hb, CHUNK, CHUNK), lambda c, g: (c, g, 0, 0))
    return _call(
        body, name="chunk_fwd", grid=(nc, n_heads // hb),
        in_specs=[blk(0), blk(0), blk(0), ba_spec(ba_off), z_spec, prm_spec, nw_spec], out_specs=[blk(0), st_spec, tm_spec],
        out_shape=[jax.ShapeDtypeStruct((lp, dn), BF16), jax.ShapeDtypeStruct((nc, n_heads, hd, hd), F32),
                   jax.ShapeDtypeStruct((nc, n_heads, CHUNK, CHUNK), F32)],
        scratch_shapes=[pltpu.VMEM((n_heads, hd, hd), F32)],
        compiler_params=_params(("arbitrary", "arbitrary")),
    )(qn, kn, vv, proj, proj, prm, nw)


def _chunk_bwd(qn, kn, vv, proj, z_off, ba_off, prm, nw, hist, tmats, dy, n_heads, pad, d_proj):
    lp, dn = qn.shape
    hd = dn // n_heads
    nc = lp // CHUNK
    hb = min(HEADS_PER_STEP_BWD, n_heads)

    def body(q_ref, k_ref, v_ref, ba_ref, z_ref, prm_ref, nw_ref, hist_ref, tm_ref, dy_ref, d_proj_ref,
             dq_ref, dk_ref, dv_ref, dba_ref, dz_ref, dprm_ref, dnw_ref, dst_ref):
        step, g = pl.program_id(0), pl.program_id(1)

        @pl.when(step == 0)
        def _():
            for j in range(hb):
                dst_ref[g * hb + j] = jnp.zeros((hd, hd), F32)

        @pl.when((step == 0) & (g == 0))
        def _():
            dprm_ref[...] = jnp.zeros_like(dprm_ref)
            dnw_ref[...] = jnp.zeros_like(dnw_ref)

        @pl.when(g == 0)
        def _():
            dba_ref[...] = jnp.zeros_like(dba_ref)

        def fn(states, q, k, v, ba, z, prm_v, nw_v, known):
            return _chunk_math(states, q, k, v, ba, z, prm_v, nw_v, g * hb, _rowmask(nc - 1 - step, pad), n_heads, tmats=known)

        states = tuple(hist_ref[0, j] for j in range(hb))
        known = tuple(tm_ref[0, j] for j in range(hb))
        _, vjp = jax.vjp(fn, states, q_ref[...], k_ref[...], v_ref[...], ba_ref[...], z_ref[...], prm_ref[...], nw_ref[...], known)
        dst, dq, dk, dv, dba, dz, dprm, dnw, _ = vjp((dy_ref[...], tuple(dst_ref[g * hb + j] for j in range(hb))))
        for j in range(hb):
            dst_ref[g * hb + j] = dst[j]
        dq_ref[...] = dq
        dk_ref[...] = dk
        dv_ref[...] = dv
        dz_ref[...] = dz.astype(BF16)
        dba_ref[...] += dba
        dprm_ref[...] += dprm
        dnw_ref[...] += dnw

    blk, ba_spec, prm_spec, nw_spec, st_spec, z_spec = _chunk_specs(nc, hd, n_heads, z_off, ba_off, True)
    f32_full = jax.ShapeDtypeStruct((lp, dn), F32)
    tm_spec = pl.BlockSpec((1, hb, CHUNK, CHUNK), lambda c, g: (nc - 1 - c, g, 0, 0))
    return _call(
        body, name="chunk_bwd", grid=(nc, n_heads // hb),
        in_specs=[blk(0), blk(0), blk(0), ba_spec(ba_off), z_spec, prm_spec, nw_spec, st_spec, tm_spec, blk(0), ANY],
        out_specs=[blk(0), blk(0), blk(0), ba_spec(0), z_spec, prm_spec, nw_spec],
        out_shape=[f32_full, f32_full, f32_full, jax.ShapeDtypeStruct((lp, LANES), F32), jax.ShapeDtypeStruct(d_proj.shape, BF16),
                   jax.ShapeDtypeStruct((8, LANES), F32), jax.ShapeDtypeStruct((1, hd), F32)],
        scratch_shapes=[pltpu.VMEM((n_heads, hd, hd), F32)],
        input_output_aliases={10: 4}, compiler_params=_params(("arbitrary", "arbitrary")),
    )(qn, kn, vv, proj, proj, prm, nw, hist, tmats, dy, d_proj)


def _merge_math(p, q, gp, gd):
    return _sigmoid(gp) * p + _sigmoid(gd) * q


def _merge_specs(lp, d, gp_off, gd_off):
    tr, tc = _tile(lp, 264, 16), _tile(d, 1024, LANES)
    blk = pl.BlockSpec((tr, tc), lambda i, j: (i, j))
    gp_spec = pl.BlockSpec((tr, tc), lambda i, j: (i, gp_off // tc + j))
    gd_spec = pl.BlockSpec((tr, tc), lambda i, j: (i, gd_off // tc + j))
    return (lp // tr, d // tc), blk, gp_spec, gd_spec


def _merge_fwd(p, q, proj, gp_off, gd_off):
    lp, d = p.shape
    grid, blk, gp_spec, gd_spec = _merge_specs(lp, d, gp_off, gd_off)

    def body(p_ref, q_ref, gp_ref, gd_ref, o_ref):
        o_ref[...] = _merge_math(p_ref[...], q_ref[...], gp_ref[...], gd_ref[...]).astype(BF16)

    return _call(
        body, name="merge_fwd", grid=grid, in_specs=[blk, blk, gp_spec, gd_spec], out_specs=blk,
        out_shape=jax.ShapeDtypeStruct((lp, d), BF16), compiler_params=_params(("parallel", "parallel")),
    )(p, q, proj, proj)


def _merge_bwd(p, q, proj, gp_off, gd_off, dm):
    lp, d = p.shape
    grid, blk, gp_spec, gd_spec = _merge_specs(lp, d, gp_off, gd_off)

    def body(p_ref, q_ref, gp_ref, gd_ref, dm_ref, dp_ref, dq_ref, dgp_ref, dgd_ref):
        _, vjp = jax.vjp(_merge_math, p_ref[...], q_ref[...], gp_ref[...], gd_ref[...])
        for ref, val in zip((dp_ref, dq_ref, dgp_ref, dgd_ref), vjp(dm_ref[...])):
            ref[...] = val.astype(BF16)

    out = jax.ShapeDtypeStruct((lp, d), BF16)
    return _call(
        body, name="merge_bwd", grid=grid, in_specs=[blk, blk, gp_spec, gd_spec, blk], out_specs=[blk] * 4,
        out_shape=[out] * 4, compiler_params=_params(("parallel", "parallel")),
    )(p, q, proj, proj, dm)


def _adamw(w, g, m, v, name):
    shape = w.shape
    w2, g2, m2, v2 = (t.reshape((-1, shape[-1])) for t in (w, g, m, v))
    rows, cols = w2.shape
    tr = _tile(rows, 128, 8)

    def body(w_ref, g_ref, m_ref, v_ref, d_ref, nm_ref, nv_ref):
        gv = g_ref[...]
        nm = ADAM_B1 * m_ref[...] + (1.0 - ADAM_B1) * gv
        nv = ADAM_B2 * v_ref[...] + (1.0 - ADAM_B2) * (gv * gv)
        m_hat = nm / (1.0 - ADAM_B1 ** ADAM_STEP)
        v_hat = nv / (1.0 - ADAM_B2 ** ADAM_STEP)
        d_ref[...] = -ADAM_LR * (m_hat / (jnp.sqrt(v_hat) + ADAM_EPS) + ADAM_WD * w_ref[...])
        nm_ref[...] = nm
        nv_ref[...] = nv

    blk = pl.BlockSpec((tr, cols), lambda i: (i, 0))
    out = jax.ShapeDtypeStruct((rows, cols), F32)
    res = _call(
        body, name=name, grid=(rows // tr,), in_specs=[blk] * 4, out_specs=[blk] * 3, out_shape=[out] * 3,
        compiler_params=_params(("parallel",)),
    )(w2, g2, m2, v2)
    return tuple(t.reshape(shape) for t in res)


def _coords():
    return lax.axis_index("x"), lax.axis_index("y"), lax.axis_index("c")


def _flip(v, bit):
    return 1 - v if bit else v


CHIP_FLIPS = ((1, 0), (0, 1), (1, 1))
ANY = pl.BlockSpec(memory_space=pl.ANY)


def _all_gather(shards):
    n = len(shards)

    def body(*refs):
        x_refs, out_refs = refs[:n], refs[n:2 * n]
        send_sems, recv_sems, local_sems = refs[2 * n:]
        x, y, c = _coords()
        sibling = (x, y, 1 - c)
        chips = [(_flip(x, fx), _flip(y, fy)) for fx, fy in CHIP_FLIPS]

        def copy(a, k, block, to, from_input=False):
            px, py, pc = block
            slot = out_refs[a].at[4 * px + 2 * py + pc]
            return pltpu.make_async_remote_copy(
                src_ref=x_refs[a] if from_input else slot, dst_ref=slot,
                send_sem=send_sems.at[7 * a + k], recv_sem=recv_sems.at[7 * a + k], device_id=to, device_id_type=MESH)

        mine = [pltpu.make_async_copy(x_refs[a], out_refs[a].at[4 * x + 2 * y + c], local_sems.at[a]) for a in range(n)]
        first = []
        for a in range(n):
            mine[a].start()
            first.append(copy(a, 0, (x, y, c), sibling, True))
            first += [copy(a, 1 + j, (x, y, c), (*chip, c), True) for j, chip in enumerate(chips)]
        for cp in first:
            cp.start()
        passed = []
        for j, chip in enumerate(chips):
            for a in range(n):
                copy(a, 1 + j, (*chip, c), (x, y, c)).wait_recv()
                passed.append(copy(a, 4 + j, (*chip, c), sibling))
                passed[-1].start()
        for a in range(n):
            copy(a, 0, (x, y, 1 - c), (x, y, c)).wait_recv()
            for j, chip in enumerate(chips):
                copy(a, 4 + j, (*chip, 1 - c), (x, y, c)).wait_recv()
        for cp in first + passed:
            cp.wait_send()
        for cp in mine:
            cp.wait()

    return _call(
        body, name="all_gather", in_specs=[ANY] * n, out_specs=[ANY] * n,
        out_shape=[jax.ShapeDtypeStruct((N_DEV,) + s.shape, s.dtype) for s in shards],
        scratch_shapes=[pltpu.SemaphoreType.DMA((7 * n,)), pltpu.SemaphoreType.DMA((7 * n,)), pltpu.SemaphoreType.DMA((n,))],
    )(*shards)


def _all_gather_tree(shard, after):
    rows, cols = shard.shape
    half = rows // 2
    assert rows % 32 == 0

    def body(x_ref, after_ref, out_ref, send_sems, recv_sems, local_sem):
        x, y, c = _coords()
        me, sibling = (x, y, c), (x, y, 1 - c)
        x_nbr, y_nbr, diag = (1 - x, y), (x, 1 - y), (1 - x, 1 - y)

        def part(ref, h):
            return ref if h is None else ref.at[pl.ds(h * half, half)]

        def copy(k, block, to, h=None, from_input=False):
            px, py, pc = block
            slot = part(out_ref.at[4 * px + 2 * py + pc], h)
            return pltpu.make_async_remote_copy(
                src_ref=part(x_ref, h) if from_input else slot, dst_ref=slot,
                send_sem=send_sems.at[k], recv_sem=recv_sems.at[k], device_id=to, device_id_type=MESH)

        mine = pltpu.make_async_copy(x_ref, out_ref.at[4 * x + 2 * y + c], local_sem)
        mine.start()
        started = [copy(0, me, sibling, None, True),
                   copy(1, me, (*x_nbr, c), 0, True), copy(2, me, (*x_nbr, c), 1, True),
                   copy(4, me, (*y_nbr, c), 1, True), copy(3, me, (*y_nbr, c), 0, True)]
        for cp in started:
            cp.start()
        copy(1, (*x_nbr, c), me, 0).wait_recv()
        started.append(copy(5, (*x_nbr, c), (*y_nbr, c), 0))
        started[-1].start()
        copy(4, (*y_nbr, c), me, 1).wait_recv()
        started.append(copy(6, (*y_nbr, c), (*x_nbr, c), 1))
        started[-1].start()
        copy(2, (*x_nbr, c), me, 1).wait_recv()
        started.append(copy(7, (*x_nbr, c), sibling))
        started[-1].start()
        copy(3, (*y_nbr, c), me, 0).wait_recv()
        started.append(copy(8, (*y_nbr, c), sibling))
        started[-1].start()
        copy(5, (*diag, c), me, 0).wait_recv()
        copy(6, (*diag, c), me, 1).wait_recv()
        started.append(copy(9, (*diag, c), sibling))
        started[-1].start()
        copy(0, sibling, me).wait_recv()
        for k, chip in ((7, x_nbr), (8, y_nbr), (9, diag)):
            copy(k, (*chip, 1 - c), me).wait_recv()
        for cp in started:
            cp.wait_send()
        mine.wait()

    return _call(
        body, name="all_gather_tree", in_specs=[ANY, ANY], out_specs=ANY,
        out_shape=jax.ShapeDtypeStruct((N_DEV, rows, cols), shard.dtype),
        scratch_shapes=[pltpu.SemaphoreType.DMA((10,)), pltpu.SemaphoreType.DMA((10,)), pltpu.SemaphoreType.DMA],
    )(shard, after)


def _rs_to_sibling(gs, name):
    n = len(gs)

    def body(*refs):
        g_refs, got_refs = refs[:n], refs[n:2 * n]
        send_sems, recv_sems = refs[2 * n:]
        x, y, c = _coords()
        copies = []
        for a in range(n):
            for p in range(4):
                cp = pltpu.make_async_remote_copy(
                    src_ref=g_refs[a].at[2 * p + (1 - c)], dst_ref=got_refs[a].at[p], send_sem=send_sems.at[4 * a + p],
                    recv_sem=recv_sems.at[4 * a + p], device_id=(x, y, 1 - c), device_id_type=MESH)
                cp.start()
                copies.append(cp)
        for cp in copies:
            cp.wait()

    return _call(
        body, name=name, in_specs=[ANY] * n, out_specs=[ANY] * n,
        out_shape=[jax.ShapeDtypeStruct((4,) + g.shape[1:], g.dtype) for g in gs],
        scratch_shapes=[pltpu.SemaphoreType.DMA((4 * n,)), pltpu.SemaphoreType.DMA((4 * n,))],
    )(*gs)


def _rs_pair_sum(g, got, c_idx, name):
    _, rows, cols = g.shape
    tr = _tile(rows, 256, 16)

    def body(c_ref, g_ref, got_ref, o_ref):
        o_ref[...] = (g_ref[...].astype(F32) + got_ref[...].astype(F32)).astype(o_ref.dtype)

    grid_spec = pltpu.PrefetchScalarGridSpec(
        num_scalar_prefetch=1, grid=(4, rows // tr),
        in_specs=[pl.BlockSpec((1, tr, cols), lambda p, i, c_ref: (2 * p + c_ref[0], i, 0)),
                  pl.BlockSpec((1, tr, cols), lambda p, i, c_ref: (p, i, 0))],
        out_specs=pl.BlockSpec((1, tr, cols), lambda p, i, c_ref: (p, i, 0)))
    return _call(
        body, name=name, grid_spec=grid_spec, out_shape=jax.ShapeDtypeStruct((4, rows, cols), g.dtype),
        compiler_params=_params(("parallel", "parallel")),
    )(c_idx, g, got)


def _to_chips_copies(p_refs, got_refs, send_sems, recv_sems):
    x, y, c = _coords()
    copies = []
    for a in range(len(p_refs)):
        for k, (fx, fy) in enumerate(CHIP_FLIPS):
            px, py = _flip(x, fx), _flip(y, fy)
            copies.append(pltpu.make_async_remote_copy(
                src_ref=p_refs[a].at[2 * px + py], dst_ref=got_refs[a].at[k], send_sem=send_sems.at[3 * a + k],
                recv_sem=recv_sems.at[3 * a + k], device_id=(px, py, c), device_id_type=MESH))
    return copies


def _rs_to_chips(partials, name):
    n = len(partials)

    def body(*refs):
        copies = _to_chips_copies(refs[:n], refs[n:2 * n], *refs[2 * n:])
        for cp in copies:
            cp.start()
        for cp in copies:
            cp.wait()

    return _call(
        body, name=name, in_specs=[ANY] * n, out_specs=[ANY] * n,
        out_shape=[jax.ShapeDtypeStruct((3,) + p.shape[1:], p.dtype) for p in partials],
        scratch_shapes=[pltpu.SemaphoreType.DMA((3 * n,)), pltpu.SemaphoreType.DMA((3 * n,))],
    )(*partials)


HBM = pl.BlockSpec(memory_space=pltpu.HBM)
SEM = pl.BlockSpec(memory_space=pltpu.SEMAPHORE)
SIDE_EFFECT = pltpu.CompilerParams(has_side_effects=pltpu.SideEffectType.DATAFLOW_SIDE_EFFECTING)


def _split_start(copies_fn, srcs, land_shapes, n_sems, name, after=None):
    n, m = len(srcs), len(land_shapes)
    extra = [] if after is None else [after]

    def body(*refs):
        outs = refs[n + m + len(extra):]
        send_sems, recv_sems, token = outs[0], outs[1], outs[-1]
        for cp in copies_fn(refs[:n], refs[n:n + m], send_sems, recv_sems):
            cp.start()
        token[...] = jnp.zeros_like(token)

    ins = [pltpu.with_memory_space_constraint(t, pltpu.HBM) for t in list(srcs) + [lax.empty(s.shape, s.dtype) for s in land_shapes]]
    res = _call(
        body, name=name, in_specs=[HBM] * (n + m) + [ANY] * len(extra),
        out_specs=[SEM, SEM] + [HBM] * (n + m) + [pl.BlockSpec(memory_space=pltpu.VMEM)],
        out_shape=[pltpu.SemaphoreType.DMA((n_sems,)), pltpu.SemaphoreType.DMA((n_sems,))]
        + [pltpu.HBM(t.shape, t.dtype) for t in ins] + [jax.ShapeDtypeStruct((8, LANES), F32)],
        input_output_aliases={i: 2 + i for i in range(n + m)}, compiler_params=SIDE_EFFECT,
    )(*ins, *extra)
    return dict(sems=(res[0], res[1]), srcs=res[2:2 + n], lands=res[2 + n:2 + n + m], token=res[-1])


def _split_wait(copies_fn, started, after, name):
    n, m = len(started["srcs"]), len(started["lands"])

    def body(*refs):
        for cp in copies_fn(refs[:n], refs[n:n + m], refs[n + m], refs[n + m + 1]):
            cp.wait_send()
            cp.wait_recv()

    bufs = list(started["srcs"]) + list(started["lands"])
    res = _call(
        body, name=name, in_specs=[HBM] * (n + m) + [SEM, SEM, ANY], out_specs=[HBM] * (n + m),
        out_shape=[pltpu.HBM(t.shape, t.dtype) for t in bufs],
        input_output_aliases={i: i for i in range(n + m)}, compiler_params=SIDE_EFFECT,
    )(*bufs, *started["sems"], after)
    return res[:n], res[n:]


def _to_all_copies(x_refs, out_refs, send_sems, recv_sems):
    x, y, c = _coords()
    copies = []
    for a in range(len(x_refs)):
        for k in range(N_DEV - 1):
            fx, fy, fc = ((k + 1) >> 2) & 1, ((k + 1) >> 1) & 1, (k + 1) & 1
            copies.append(pltpu.make_async_remote_copy(
                src_ref=x_refs[a], dst_ref=out_refs[a].at[4 * x + 2 * y + c], send_sem=send_sems.at[7 * a + k],
                recv_sem=recv_sems.at[7 * a + k], device_id=(_flip(x, fx), _flip(y, fy), _flip(c, fc)), device_id_type=MESH))
    return copies


def _fill_own_block(gathered, shard, me_idx, name):
    rows, cols = shard.shape
    tr = _tile(rows, 512, 16)

    def body(me_ref, g_ref, s_ref, o_ref):
        o_ref[0] = s_ref[...]

    grid_spec = pltpu.PrefetchScalarGridSpec(
        num_scalar_prefetch=1, grid=(rows // tr,),
        in_specs=[ANY, pl.BlockSpec((tr, cols), lambda i, me: (i, 0))],
        out_specs=pl.BlockSpec((1, tr, cols), lambda i, me: (me[0], i, 0)))
    return _call(
        body, name=name, grid_spec=grid_spec, out_shape=jax.ShapeDtypeStruct(gathered.shape, gathered.dtype),
        input_output_aliases={1: 0}, compiler_params=_params(("arbitrary",)),
    )(me_idx, gathered, shard)


def _rs_chip_sum(partial, got, chip_idx, name, part=0, n_parts=1, dst=None):
    _, rows, cols = partial.shape
    tr = _tile(rows, 256, 16)
    steps = rows // tr
    n_dst = 0 if dst is None else 1

    def body(p_idx_ref, p_ref, got_ref, *refs):
        refs[n_dst][...] = ((p_ref[0].astype(F32) + got_ref[0].astype(F32)) + got_ref[1].astype(F32)) + got_ref[2].astype(F32)

    grid_spec = pltpu.PrefetchScalarGridSpec(
        num_scalar_prefetch=1, grid=(steps,),
        in_specs=[pl.BlockSpec((1, tr, cols), lambda i, p_ref: (p_ref[0], i, 0)),
                  pl.BlockSpec((3, tr, cols), lambda i, p_ref: (0, i, 0))] + [ANY] * n_dst,
        out_specs=pl.BlockSpec((tr, cols), lambda i, p_ref: (part * steps + i, 0)))
    return _call(
        body, name=name, grid_spec=grid_spec, out_shape=jax.ShapeDtypeStruct((n_parts * rows, cols), F32),
        input_output_aliases={3: 0} if n_dst else {}, compiler_params=_params(("parallel",)),
    )(chip_idx, partial, got, *([] if dst is None else [dst]))


def _rs_begin(gs, tag, split):
    c_idx = jnp.reshape(lax.axis_index("c"), (1,)).astype(jnp.int32)
    gots = _rs_to_sibling(gs, "rs_to_sibling_" + tag)
    partials = [_rs_pair_sum(g, got, c_idx, "rs_pair_sum_%s%d" % (tag, a)) for a, (g, got) in enumerate(zip(gs, gots))]
    if not split:
        return dict(partials=partials, gots=_rs_to_chips(partials, "rs_to_chips_" + tag))
    lands = [jax.ShapeDtypeStruct((3,) + p.shape[1:], p.dtype) for p in partials]
    return _split_start(_to_chips_copies, partials, lands, 3 * len(partials), "rs_to_chips_start_" + tag)


def _rs_finish(begun, tag, after=None, part=0, n_parts=1, dsts=None):
    x, y, _ = _coords()
    chip_idx = jnp.reshape(2 * x + y, (1,)).astype(jnp.int32)
    if "gots" in begun:
        partials, gots = begun["partials"], begun["gots"]
    else:
        partials, gots = _split_wait(_to_chips_copies, begun, after, "rs_to_chips_wait_" + tag)
    return [_rs_chip_sum(p, got, chip_idx, "rs_chip_sum_%s%d" % (tag, a), part, n_parts, None if dsts is None else dsts[a])
            for a, (p, got) in enumerate(zip(partials, gots))]


RUNS = 3
RUN_FIELDS = 5


def _lane_gather_table(src_of, ahead):
    n_blocks = src_of.shape[0] // LANES
    tab = np.zeros((n_blocks + 3 * ahead, RUNS, RUN_FIELDS), np.int32)
    for t in range(n_blocks):
        runs = []
        for lane in range(LANES):
            slab, col = (int(v) for v in src_of[t * LANES + lane])
            if slab < 0:
                continue
            key = (slab, col // LANES, col % LANES - lane)
            if runs and runs[-1][0] == key and runs[-1][2] == lane:
                runs[-1][2] = lane + 1
            else:
                runs.append([key, lane, lane + 1])
        assert len(runs) <= RUNS
        slots = [None] * RUNS
        for key, lo, hi in sorted(runs, key=lambda r: r[0][:2]):
            slots[slots.index(None)] = (key[0], key[1], key[2], lo, hi)
        for e in range(RUNS):
            kept = (tab[t - ahead, e, 0], tab[t - ahead, e, 1], 0, 0, 0) if t >= ahead else tab[t, e]
            tab[t, e] = slots[e] if slots[e] is not None else kept
    tab[n_blocks:, :, :2] = np.tile(tab[n_blocks - ahead:n_blocks, :, :2], (3, 1, 1))
    return tab.reshape(-1)


def _place_run(tab_ref, t, e, block, under):
    base = (t * RUNS + e) * RUN_FIELDS
    shift, lo, hi = tab_ref[base + 2], tab_ref[base + 3], tab_ref[base + 4]
    lane = lax.broadcasted_iota(jnp.int32, (1, LANES), 1)
    return jnp.where((lane >= lo) & (lane < hi), pltpu.roll(block.astype(F32), (LANES - shift) % LANES, 1), under)


def _lane_gather_cols(src, table, out_slabs, out_width, name, sub):
    _, rows, _ = src.shape
    steps_per_slab = -(-out_width // (sub * LANES))

    def body(tab_ref, *refs):
        o_ref = refs[sub * RUNS]
        for s in range(sub):
            t = pl.program_id(0) * sub + s
            ops = refs[s * RUNS:(s + 1) * RUNS]
            lanes = slice(s * LANES, (s + 1) * LANES)
            o_ref[0, :, lanes] = _place_run(tab_ref, t, 1, ops[1][0], _place_run(tab_ref, t, 0, ops[0][0], 0.0)).astype(BF16)
            last = (t * RUNS + RUNS - 1) * RUN_FIELDS

            @pl.when(tab_ref[last + 4] > tab_ref[last + 3])
            def _():
                o_ref[0, :, lanes] = _place_run(tab_ref, t, RUNS - 1, ops[RUNS - 1][0], o_ref[0, :, lanes].astype(F32)).astype(BF16)

    def src_spec(s, e):
        at = lambda t: ((t * sub + s) * RUNS + e) * RUN_FIELDS
        return pl.BlockSpec((1, rows, LANES), lambda t, tab: (tab[at(t)], 0, tab[at(t) + 1]))

    grid_spec = pltpu.PrefetchScalarGridSpec(
        num_scalar_prefetch=1, grid=(out_slabs * steps_per_slab,),
        in_specs=[src_spec(s, e) for s in range(sub) for e in range(RUNS)],
        out_specs=pl.BlockSpec((1, rows, sub * LANES), lambda t, tab: (t // steps_per_slab, 0, t % steps_per_slab)))
    return _call(
        body, name=name, grid_spec=grid_spec, out_shape=jax.ShapeDtypeStruct((out_slabs, rows, out_width), BF16),
        compiler_params=_params(("arbitrary",)),
    )(jnp.asarray(table), *([src] * (sub * RUNS)))


def _all_reduce_small(vec):
    rows, cols = vec.shape

    def body(v_ref, o_ref, buf, send_sems, recv_sems):
        x, y, c = _coords()
        me = 4 * x + 2 * y + c
        buf[me] = v_ref[...]
        copies = []
        for k in range(N_DEV - 1):
            fx, fy, fc = ((k + 1) >> 2) & 1, ((k + 1) >> 1) & 1, (k + 1) & 1
            cp = pltpu.make_async_remote_copy(
                src_ref=v_ref, dst_ref=buf.at[me], send_sem=send_sems.at[k], recv_sem=recv_sems.at[k],
                device_id=(_flip(x, fx), _flip(y, fy), _flip(c, fc)), device_id_type=MESH)
            cp.start()
            copies.append(cp)
        for cp in copies:
            cp.wait()
        total = buf[0]
        for j in range(1, N_DEV):
            total = total + buf[j]
        o_ref[...] = total

    vmem = pl.BlockSpec(memory_space=pltpu.VMEM)
    return _call(
        body, name="all_reduce_small", in_specs=[vmem], out_specs=vmem,
        out_shape=jax.ShapeDtypeStruct((rows, cols), F32),
        scratch_shapes=[pltpu.VMEM((N_DEV, rows, cols), F32), pltpu.SemaphoreType.DMA((N_DEV - 1,)),
                        pltpu.SemaphoreType.DMA((N_DEV - 1,))],
    )(vec)


def _w_in_column_maps(ns, o_ba, n_logit, n_main, n_all, own_sub):
    own = np.arange(N_DEV * ns)
    work_of_own = np.where(own < o_ba, own, np.where(own < o_ba + n_logit, n_main + own - o_ba, own - n_logit))
    to_work = np.full((n_all, 2), -1, np.int64)
    to_work[work_of_own, 0] = own // ns
    to_work[work_of_own, 1] = own % ns
    slab_width = -(-ns // (own_sub * LANES)) * own_sub * LANES
    to_own = np.full((N_DEV, slab_width, 2), -1, np.int64)
    to_own[:, :ns, 0] = 0
    to_own[:, :ns, 1] = work_of_own.reshape(N_DEV, ns)
    return to_work, to_own.reshape(-1, 2)


def kernel(x, meta_tokens, norm_w, w_in, conv_w, A_log, dt_bias, pool_mix, pool_scale, dn_norm_w, w_pool_out, w_dn_out, w_o, final_norm_w, loss_target, m_meta_tokens, m_norm_w, m_w_in, m_conv_w, m_A_log, m_dt_bias, m_pool_mix, m_pool_scale, m_dn_norm_w, m_w_pool_out, m_w_dn_out, m_w_o, m_final_norm_w, v_meta_tokens, v_norm_w, v_w_in, v_conv_w, v_A_log, v_dt_bias, v_pool_mix, v_pool_scale, v_dn_norm_w, v_w_pool_out, v_w_dn_out, v_w_o, v_final_norm_w):
    seq, d = x.shape[1], x.shape[2]
    n_meta = meta_tokens.shape[0]
    n_heads, hd = A_log.shape[-1], dn_norm_w.shape[-1]
    dn = n_heads * hd
    pw, ng = pool_scale.shape[-1], pool_mix.shape[1]
    pg = pw // ng
    kw = conv_w.shape[1]
    pad = (-n_meta) % CHUNK
    x0 = pad + n_meta
    lp = x0 + seq
    ns = w_in.shape[-1]
    in_cols = N_DEV * ns
    o_q, o_k, o_v, o_zd = 2 * pw, 2 * pw + dn, 2 * pw + 2 * dn, 2 * pw + 3 * dn
    o_ba = 2 * pw + 4 * dn
    o_gp, o_gd = o_ba, o_ba + d
    n_main = o_gd + d
    n_all = n_main + 2 * LANES
    assert lp % CHUNK == 0 and in_cols == n_main + 2 * n_heads and 2 * n_heads <= LANES and hd == LANES
    cs, ms = conv_w.shape[-1], meta_tokens.shape[-1]
    mr = pool_mix.shape[2]
    assert ms == pg and cs % pg == 0
    work_sub = max(s for s in (6, 3, 2, 1) if (n_all // LANES) % s == 0)
    own_sub = 5
    to_work, to_own = _w_in_column_maps(ns, o_ba, 2 * n_heads, n_main, n_all, own_sub)
    cols_major = lambda t: jnp.transpose(t, (1, 0, 2)).reshape(t.shape[1], N_DEV * t.shape[2])

    mix_g, conv_g, meta_g = _all_gather([pool_mix[0].reshape(ng * mr, pg).astype(BF16), conv_w[0], meta_tokens])
    win_g = _all_gather_tree(w_in[0].astype(BF16), after=meta_g)
    late_shards = [w_pool_out[0].astype(BF16), w_dn_out[0].astype(BF16), w_o[0].astype(BF16)]
    late_weights = _split_start(_to_all_copies, late_shards, [jax.ShapeDtypeStruct((N_DEV,) + s.shape, BF16) for s in late_shards],
                                (N_DEV - 1) * len(late_shards), "gather_out_proj_start", after=win_g)
    norm_w_in = norm_w + late_weights["token"][0, 0]
    w_all = _lane_gather_cols(win_g, _lane_gather_table(to_work, work_sub), 1, n_all, "w_in_to_work", work_sub).reshape(d, n_all)
    mix_f = jnp.transpose(mix_g.reshape(N_DEV, ng, mr, pg), (1, 0, 2, 3)).reshape(ng, pg, pg)
    conv_f = cols_major(conv_g)
    meta_f = cols_major(meta_g)

    h0, xn = _norm_in_fwd(x[0], meta_f, norm_w_in, pad)
    proj = _matmul(xn, w_all, NN, F32, lp, 768, 2048, "proj")
    y_pool = _pool_fwd(proj, mix_f, pool_scale, pad)
    conv_q, conv_k, conv_v = (conv_f[:, i * dn:(i + 1) * dn] for i in range(3))
    qn = _conv_fwd(proj, o_q, conv_q, hd, float(hd) ** -0.5, "conv_q_fwd")
    kn = _conv_fwd(proj, o_k, conv_k, hd, 1.0, "conv_k_fwd")
    vv = _conv_fwd(proj, o_v, conv_v, hd, None, "conv_v_fwd")
    logit_lanes = (n_heads, LANES - 2 * n_heads)
    prm = jnp.pad(A_log, ((0, 7), logit_lanes)) + jnp.pad(dt_bias, ((1, 6), logit_lanes))
    y_dn, hist, tmats = _chunk_fwd(qn, kn, vv, proj, o_zd, n_main, prm, dn_norm_w, n_heads, pad)
    me_idx = jnp.reshape(4 * lax.axis_index("x") + 2 * lax.axis_index("y") + lax.axis_index("c"), (1,)).astype(jnp.int32)
    _, landed = _split_wait(_to_all_copies, late_weights, y_dn, "gather_out_proj_wait")
    wpo_g, wdn_g, wo_g = (_fill_own_block(g, s, me_idx, "own_block_%d" % i) for i, (g, s) in enumerate(zip(landed, late_shards)))
    wpo_f = cols_major(wpo_g)
    wdn_f = wdn_g.reshape(dn, d)
    wo_f = wo_g.reshape(d, d)
    p_out = _matmul(y_pool, wpo_f, NN, F32, 1056, 1024, 1024, "pool_out")
    q_out = _matmul(y_dn, wdn_f, NN, F32, 1056, 1024, 2048, "dn_out")
    merged = _merge_fwd(p_out, q_out, proj, o_gp, o_gd)
    mo = _matmul(merged, wo_f, NN, F32, 1056, 1024, 2048, "w_o_fwd")
    dh1, d_fw, loss_part = _final_loss(h0, mo, final_norm_w.reshape(1, d), loss_target[0], x0)

    d_merged = _matmul(dh1, wo_f, NT, F32, 1056, 1024, 1024, "w_o_bwd_x")
    g_wo = _matmul(merged, dh1, TN, BF16, 1024, 1024, lp, "w_o_bwd_w")
    d_p, d_q, d_gp, d_gd = _merge_bwd(p_out, q_out, proj, o_gp, o_gd, d_merged)
    d_ypool = _matmul(d_p, wpo_f, NT, F32, 1056, 1024, 2048, "pool_out_bwd_x")
    g_wpo = _matmul(y_pool.T, d_p, NN, BF16, 1024, 1024, lp, "pool_out_bwd_w", col_blocks=N_DEV)
    d_ydn = _matmul(d_q, wdn_f, NT, F32, 1056, 1024, 2048, "dn_out_bwd_x")
    g_wdn = _matmul(y_dn, d_q, TN, BF16, 1024, 1024, lp, "dn_out_bwd_w")
    rs_early = _rs_begin([g_wpo, g_wdn.reshape(N_DEV, dn // N_DEV, d), g_wo.reshape(N_DEV, d // N_DEV, d)], "early", split=True)
    started = rs_early["token"][0, 0]
    d_u, d_zp, g_mix, g_pscale = _pool_bwd(proj, mix_f, pool_scale + started, d_ypool, pad)
    d_proj = lax.empty((lp, n_all), BF16)
    d_qn, d_kn, d_vv, d_ba, d_proj, d_prm, g_dnw = _chunk_bwd(qn, kn, vv, proj, o_zd, n_main, prm + started, dn_norm_w, hist, tmats,
                                                              d_ydn, n_heads, pad, d_proj)
    d_proj, g_cq = _conv_bwd(proj, o_q, conv_q, d_qn, hd, float(hd) ** -0.5, pad, "conv_q_bwd", d_proj)
    d_proj, g_ck = _conv_bwd(proj, o_k, conv_k, d_kn, hd, 1.0, pad, "conv_k_bwd", d_proj)
    d_proj, g_cv = _conv_bwd(proj, o_v, conv_v, d_vv, hd, None, pad, "conv_v_bwd", d_proj)
    for off, piece in ((0, d_u), (pw, d_zp), (o_gp, d_gp), (o_gd, d_gd), (n_main, d_ba.astype(BF16)), (n_main + LANES, jnp.zeros((lp, LANES), BF16))):
        d_proj = lax.dynamic_update_slice(d_proj, piece, (0, off))
    xn_t, rs_late, token = xn.T, [], None
    for half in range(2):
        rows = slice(half * (d // 2), (half + 1) * (d // 2))
        g_wall = _matmul(xn_t[rows], d_proj, NN, BF16, 1024, 768, lp, "w_in_bwd_w_%d" % half, after=token)
        g_win = _lane_gather_cols(g_wall.reshape(1, d // 2, n_all), _lane_gather_table(to_own, own_sub), N_DEV, ns,
                                  "w_in_grad_to_own_%d" % half, own_sub)
        rs_late.append(_rs_begin([g_win], "late%d" % half, split=True))
        token = rs_late[-1]["token"]
    d_xn = _matmul(d_proj, w_all, NT, F32, lp, 512, 2432, "w_in_bwd_x", after=token)
    d_head, grad_x, g_nw = _norm_in_bwd(h0, norm_w, d_xn, dh1, x0)
    grad_x = grad_x[None]

    by_cols = lambda t: jnp.transpose(t.reshape(t.shape[0], N_DEV, t.shape[1] // N_DEV), (1, 0, 2))
    g_conv = by_cols(jnp.concatenate([g_cq, g_ck, g_cv], axis=1)).reshape(N_DEV, kw * cs // pg, pg)
    conv_rows = -(-g_conv.shape[1] // 16) * 16
    g_small = jnp.concatenate(
        [jnp.transpose(g_mix.reshape(ng, N_DEV, mr, pg), (1, 0, 2, 3)).reshape(N_DEV, ng * mr, pg), by_cols(d_head[pad:x0]),
         jnp.pad(g_conv, ((0, 0), (0, conv_rows - g_conv.shape[1]), (0, 0)))], axis=1).astype(BF16)
    r_small, = _rs_finish(_rs_begin([g_small], "small", split=False), "small")
    r_mix, r_meta = r_small[:ng * mr], r_small[ng * mr:ng * mr + n_meta]
    r_conv = r_small[ng * mr + n_meta:ng * mr + n_meta + kw * cs // pg]
    r_wpo, r_wdn, r_wo = _rs_finish(rs_early, "early", after=r_small)

    small = [g_nw[0], d_fw[0], g_pscale[0], g_dnw[0], d_prm[0], d_prm[1], loss_part[0]]
    s_sizes = [t.shape[0] for t in small]
    s_cols = -(-sum(s_sizes) // (8 * LANES)) * LANES
    s_vec = jnp.concatenate(small + [jnp.zeros((8 * s_cols - sum(s_sizes),), F32)]).reshape(8, s_cols)
    s_red = _all_reduce_small(s_vec)
    s_sum = s_red.reshape(-1)
    r_win = None
    for half, begun in enumerate(rs_late):
        r_win = _rs_finish(begun, "late%d" % half, after=s_red, part=half, n_parts=2, dsts=r_win)
    r_win, = r_win
    s_offs = [sum(s_sizes[:i]) for i in range(len(s_sizes))]
    s_take = lambda i, n=None, o=0: s_sum[s_offs[i] + o:s_offs[i] + o + (s_sizes[i] if n is None else n)]

    grads = {
        "meta_tokens": r_meta, "norm_w": s_take(0).reshape(norm_w.shape),
        "w_in": r_win.reshape(w_in.shape), "conv_w": r_conv.reshape(conv_w.shape),
        "A_log": s_take(4, n_heads, n_heads).reshape(A_log.shape), "dt_bias": s_take(5, n_heads, n_heads).reshape(dt_bias.shape),
        "pool_mix": r_mix.reshape(pool_mix.shape), "pool_scale": s_take(2).reshape(pool_scale.shape),
        "dn_norm_w": s_take(3).reshape(dn_norm_w.shape), "w_pool_out": r_wpo.reshape(w_pool_out.shape),
        "w_dn_out": r_wdn.reshape(w_dn_out.shape), "w_o": r_wo.reshape(w_o.shape),
        "final_norm_w": s_take(1).reshape(final_norm_w.shape),
    }
    loss = s_take(6, 1)[0]

    weights = dict(meta_tokens=meta_tokens, norm_w=norm_w, w_in=w_in, conv_w=conv_w, A_log=A_log, dt_bias=dt_bias,
                   pool_mix=pool_mix, pool_scale=pool_scale, dn_norm_w=dn_norm_w, w_pool_out=w_pool_out, w_dn_out=w_dn_out,
                   w_o=w_o, final_norm_w=final_norm_w)
    m_in = dict(meta_tokens=m_meta_tokens, norm_w=m_norm_w, w_in=m_w_in, conv_w=m_conv_w, A_log=m_A_log, dt_bias=m_dt_bias,
                pool_mix=m_pool_mix, pool_scale=m_pool_scale, dn_norm_w=m_dn_norm_w, w_pool_out=m_w_pool_out,
                w_dn_out=m_w_dn_out, w_o=m_w_o, final_norm_w=m_final_norm_w)
    v_in = dict(meta_tokens=v_meta_tokens, norm_w=v_norm_w, w_in=v_w_in, conv_w=v_conv_w, A_log=v_A_log, dt_bias=v_dt_bias,
                pool_mix=v_pool_mix, pool_scale=v_pool_scale, dn_norm_w=v_dn_norm_w, w_pool_out=v_w_pool_out,
                w_dn_out=v_w_dn_out, w_o=v_w_o, final_norm_w=v_final_norm_w)
    names = list(weights)
    upd = {n: _adamw(weights[n], grads[n], m_in[n], v_in[n], "adamw_" + n) for n in names}
    return (loss, grad_x, *[grads[n] for n in names], *[upd[n][0] for n in names], *[upd[n][1] for n in names],
            *[upd[n][2] for n in names])
```

```python
import functools
import math

import jax
import jax.numpy as jnp
import numpy as np
from jax import lax
from jax.experimental import pallas as pl
from jax.experimental.pallas import tpu as pltpu

F32 = jnp.float32
BF16 = jnp.bfloat16
HIGHEST = lax.Precision.HIGHEST
MESH = pl.DeviceIdType.MESH

CHUNK = 64
NORM_EPS = 1e-6
POOL_WINDOWS = (2, 4, 8, 16)
ADAM_LR, ADAM_B1, ADAM_B2, ADAM_EPS, ADAM_WD, ADAM_STEP = 0.001, 0.9, 0.999, 1e-08, 0.01, 10
N_DEV = 8
LANES = 128
VMEM_LIMIT = 48 * 1024 * 1024

NN = (((1,), (0,)), ((), ()))
NT = (((1,), (1,)), ((), ()))
TN = (((0,), (0,)), ((), ()))


def _call(body, **kw):
    return pl.pallas_call(body, **kw)


def _params(sem=None):
    return pltpu.CompilerParams(dimension_semantics=sem, vmem_limit_bytes=VMEM_LIMIT)


def _tile(n, pref, align):
    for d in range(min(pref, n), 0, -1):
        if n % d == 0 and d % align == 0:
            return d
    return n


def _dot(a, b, dims=NN, precision=None):
    return lax.dot_general(a, b, dims, precision=precision, preferred_element_type=F32)


def _sigmoid(x):
    return 0.5 * jnp.tanh(0.5 * x) + 0.5


def _silu(x):
    return x * _sigmoid(x)


def _softplus(x):
    return jnp.maximum(x, 0.0) + jnp.log(1.0 + jnp.exp(-jnp.abs(x)))


def _rmsnorm(x, w):
    return x * lax.rsqrt(jnp.mean(x * x, axis=-1, keepdims=True) + NORM_EPS) * w


def _shift_down(x, j, row):
    if j == 0:
        return x
    return jnp.where(row >= j, pltpu.roll(x, j, 0), 0.0)


def _shift_up(x, j, row):
    if j == 0:
        return x
    n = x.shape[0]
    return jnp.where(row < n - j, pltpu.roll(x, n - j, 0), 0.0)


def _matmul(a, b, dims, out_dtype, tm, tn, tk, name, col_blocks=None, after=None):
    ta = dims == TN
    tb = dims == NT
    m, kdim = (a.shape[1], a.shape[0]) if ta else a.shape
    n = b.shape[0] if tb else b.shape[1]
    if col_blocks:
        tn = n // col_blocks
    tm, tn, tk = _tile(m, tm, 8), _tile(n, tn, LANES), _tile(kdim, tk, LANES if not ta else 16)
    nk = kdim // tk

    n_extra = 0 if after is None else 1

    def body(a_ref, b_ref, *refs):
        o_ref, scratch = refs[n_extra], refs[n_extra + 1:]
        part = _dot(a_ref[...].astype(BF16), b_ref[...].astype(BF16), dims)
        if nk == 1:
            o_ref[...] = part.astype(o_ref.dtype).reshape(o_ref.shape)
            return
        acc_ref, = scratch
        k = pl.program_id(2)

        @pl.when(k == 0)
        def _():
            acc_ref[...] = part

        @pl.when(k > 0)
        def _():
            acc_ref[...] += part

        @pl.when(k == nk - 1)
        def _():
            o_ref[...] = acc_ref[...].astype(o_ref.dtype).reshape(o_ref.shape)

    a_spec = pl.BlockSpec((tk, tm), lambda i, j, k: (k, i)) if ta else pl.BlockSpec((tm, tk), lambda i, j, k: (i, k))
    b_spec = pl.BlockSpec((tn, tk), lambda i, j, k: (j, k)) if tb else pl.BlockSpec((tk, tn), lambda i, j, k: (k, j))
    if col_blocks:
        out_spec = pl.BlockSpec((1, tm, tn), lambda i, j, k: (j, i, 0))
        out_shape = jax.ShapeDtypeStruct((col_blocks, m, tn), out_dtype)
    else:
        out_spec = pl.BlockSpec((tm, tn), lambda i, j, k: (i, j))
        out_shape = jax.ShapeDtypeStruct((m, n), out_dtype)
    return _call(
        body, name=name, grid=(m // tm, n // tn, nk),
        in_specs=[a_spec, b_spec] + [ANY] * n_extra, out_specs=out_spec, out_shape=out_shape,
        scratch_shapes=[] if nk == 1 else [pltpu.VMEM((tm, tn), F32)],
        compiler_params=_params(("parallel", "parallel", "arbitrary")),
    )(a, b, *([] if after is None else [after]))


def _norm_in_fwd(x2d, meta, w, pad):
    seq, d = x2d.shape
    x0 = pad + meta.shape[0]
    assert x0 % 16 == 0
    lp = x0 + seq
    tr = _tile(seq, 512, 16)
    vec = pl.BlockSpec((1, d), lambda i: (0, 0))
    shapes = [jax.ShapeDtypeStruct((lp, d), F32), jax.ShapeDtypeStruct((lp, d), BF16)]

    def body(x_ref, w_ref, h_ref, o_ref):
        h_ref[...] = x_ref[...]
        o_ref[...] = _rmsnorm(x_ref[...], w_ref[...]).astype(BF16)

    def head(m_ref, w_ref, h_in_ref, o_in_ref, h_ref, o_ref):
        h = jnp.concatenate([jnp.zeros((pad, d), F32), m_ref[...]], axis=0) if pad else m_ref[...]
        h_ref[...] = h
        o_ref[...] = _rmsnorm(h, w_ref[...]).astype(BF16)

    rows = _rows_after(x0, tr, d)
    h0, xn = _call(
        body, name="norm_in_fwd", grid=(seq // tr,), in_specs=[pl.BlockSpec((tr, d), lambda i: (i, 0)), vec],
        out_specs=[rows, rows], out_shape=shapes, compiler_params=_params(("parallel",)),
    )(x2d, w)
    first = pl.BlockSpec((x0, d), lambda i: (0, 0))
    return _call(
        head, name="norm_in_fwd_head", grid=(1,), in_specs=[pl.BlockSpec(meta.shape, lambda i: (0, 0)), vec, ANY, ANY],
        out_specs=[first, first], out_shape=shapes, input_output_aliases={2: 0, 3: 1}, compiler_params=_params(("arbitrary",)),
    )(meta, w, h0, xn)


def _rows_after(x0, tr, d):
    step = math.gcd(x0, tr)
    return pl.BlockSpec((pl.Element(tr), pl.Element(d)), lambda i: (pl.multiple_of(x0 + tr * i, step), 0))


def _norm_in_bwd(h0, w, dxn, dh1, x0):
    lp, d = h0.shape
    tr = _tile(lp - x0, 512, 8)
    vec = pl.BlockSpec((1, d), lambda i: (0, 0))

    def make(body_rows, first):
        def body(h_ref, w_ref, da_ref, dh1_ref, dh_ref, dw_ref):
            _, vjp = jax.vjp(_rmsnorm, h_ref[...], w_ref[...])
            dh, dw = vjp(da_ref[...])
            dh_ref[...] = dh + dh1_ref[...]

            @pl.when(pl.program_id(0) == 0)
            def _():
                dw_ref[...] = jnp.zeros_like(dw_ref)

            dw_ref[...] += dw

        rows_in = pl.BlockSpec((x0, d), lambda i: (0, 0)) if first else _rows_after(x0, tr, d)
        return _call(
            body, name="norm_in_bwd_head" if first else "norm_in_bwd", grid=(1 if first else (lp - x0) // tr,),
            in_specs=[rows_in, vec, rows_in, rows_in],
            out_specs=[pl.BlockSpec((body_rows, d), lambda i: (i, 0)), vec],
            out_shape=[jax.ShapeDtypeStruct((x0 if first else lp - x0, d), F32), jax.ShapeDtypeStruct((1, d), F32)],
            compiler_params=_params(("arbitrary",)),
        )(h0, w, dxn, dh1)

    d_head, dw_head = make(x0, True)
    grad_x, dw_rest = make(tr, False)
    return d_head, grad_x, dw_head + dw_rest


def _final_loss(h0, mo, fw, tgt, x0):
    lp, d = h0.shape
    tr = _tile(lp - x0, 512, 8)

    def body(h_ref, mo_ref, fw_ref, t_ref, dh_ref, dw_ref, loss_ref):
        tgt_v = t_ref[...]

        def loss_fn(h1, w):
            err = _rmsnorm(h1, w) - tgt_v
            return 0.5 * jnp.sum(jnp.mean(err * err, axis=-1, keepdims=True), axis=0, keepdims=True)

        loss, vjp = jax.vjp(loss_fn, h_ref[...] + mo_ref[...], fw_ref[...])
        dh, dw = vjp(jnp.ones((1, 1), F32))
        dh_ref[...] = dh

        @pl.when(pl.program_id(0) == 0)
        def _():
            dw_ref[...] = jnp.zeros_like(dw_ref)
            loss_ref[...] = jnp.zeros_like(loss_ref)

        dw_ref[...] += dw
        loss_ref[...] += jnp.broadcast_to(loss, loss_ref.shape)

    def zero_head(dh_in_ref, dh_ref):
        dh_ref[...] = jnp.zeros_like(dh_ref)

    rows = _rows_after(x0, tr, d)
    vec = pl.BlockSpec((1, d), lambda i: (0, 0))
    dh1, dw, loss = _call(
        body, name="final_loss", grid=((lp - x0) // tr,),
        in_specs=[rows, rows, vec, pl.BlockSpec((tr, d), lambda i: (i, 0))],
        out_specs=[rows, vec, pl.BlockSpec((8, LANES), lambda i: (0, 0))],
        out_shape=[jax.ShapeDtypeStruct((lp, d), F32), jax.ShapeDtypeStruct((1, d), F32), jax.ShapeDtypeStruct((8, LANES), F32)],
        compiler_params=_params(("arbitrary",)),
    )(h0, mo, fw, tgt)
    dh1 = _call(
        zero_head, name="final_loss_head", grid=(1,), in_specs=[ANY], out_specs=pl.BlockSpec((x0, d), lambda i: (0, 0)),
        out_shape=jax.ShapeDtypeStruct((lp, d), F32), input_output_aliases={0: 0}, compiler_params=_params(("arbitrary",)),
    )(dh1)
    return dh1, dw, loss


def _pool_select(parts, g):
    out = parts[-1]
    for gi in range(len(parts) - 2, -1, -1):
        out = jnp.where(g == gi, parts[gi], out)
    return out


def _pool_count(row, g, pad):
    win = _pool_select([jnp.full(row.shape, float(w), F32) for w in POOL_WINDOWS], g)
    return jnp.maximum(jnp.minimum((row - pad + 1).astype(F32), win), 1.0)


def _pooled(u, g, row, pad):
    sums, s, span = [], u, 1
    for w in POOL_WINDOWS:
        while span < w:
            s = s + _shift_down(s, span, row)
            span *= 2
        sums.append(s)
    return _pool_select(sums, g) / _pool_count(row, g, pad) - u


def _pooled_adjoint(dp, g, row, pad):
    e = dp / _pool_count(row, g, pad)
    sums, s, span = [], e, 1
    for w in POOL_WINDOWS:
        while span < w:
            s = s + _shift_up(s, span, row)
            span *= 2
        sums.append(s)
    return _pool_select(sums, g) - dp


def _pool_specs(lp, pg, ng, z_off):
    u_spec = pl.BlockSpec((lp, pg), lambda g: (0, g))
    z_spec = pl.BlockSpec((lp, pg), lambda g: (0, z_off + g))
    mix_spec = pl.BlockSpec((1, pg, pg), lambda g: (g, 0, 0))
    vec_spec = pl.BlockSpec((1, pg), lambda g: (0, g))
    return u_spec, z_spec, mix_spec, vec_spec


def _pool_fwd(proj, mix, scale, pad):
    lp = proj.shape[0]
    ng, pg, _ = mix.shape
    pw = ng * pg

    def body(u_ref, z_ref, mix_ref, sc_ref, y_ref):
        g = pl.program_id(0)
        row = lax.broadcasted_iota(jnp.int32, (lp, 1), 0)
        pooled = _pooled(u_ref[...], g, row, pad)
        mixed = _dot(pooled.astype(BF16), mix_ref[0])
        y_ref[...] = (mixed * sc_ref[...] * _silu(z_ref[...])).astype(BF16)

    u_spec, z_spec, mix_spec, vec_spec = _pool_specs(lp, pg, ng, pw // pg)
    return _call(
        body, name="pool_fwd", grid=(ng,), in_specs=[u_spec, z_spec, mix_spec, vec_spec], out_specs=u_spec,
        out_shape=jax.ShapeDtypeStruct((lp, pw), BF16), compiler_params=_params(("parallel",)),
    )(proj, proj, mix, scale)


def _pool_bwd(proj, mix, scale, dy, pad):
    lp = proj.shape[0]
    ng, pg, _ = mix.shape
    pw = ng * pg

    def body(u_ref, z_ref, mix_ref, sc_ref, dy_ref, du_ref, dz_ref, dmix_ref, dsc_ref):
        g = pl.program_id(0)
        row = lax.broadcasted_iota(jnp.int32, (lp, 1), 0)
        real = row >= pad
        z = z_ref[...]
        pooled = _pooled(u_ref[...], g, row, pad).astype(BF16)
        mixed = _dot(pooled, mix_ref[0])
        sig = _sigmoid(z)
        sz = z * sig
        dyv = dy_ref[...]
        dsc_ref[...] = jnp.sum(dyv * mixed * sz, axis=0, keepdims=True)
        d_sz = dyv * mixed * sc_ref[...]
        dz_ref[...] = jnp.where(real, d_sz * (sig + sz * (1.0 - sig)), 0.0).astype(BF16)
        d_mixed = (dyv * sc_ref[...] * sz).astype(BF16)
        dmix_ref[0] = _dot(pooled, d_mixed, TN)
        d_pooled = _dot(d_mixed, mix_ref[0], NT)
        du_ref[...] = jnp.where(real, _pooled_adjoint(d_pooled, g, row, pad), 0.0).astype(BF16)

    u_spec, z_spec, mix_spec, vec_spec = _pool_specs(lp, pg, ng, pw // pg)
    return _call(
        body, name="pool_bwd", grid=(ng,),
        in_specs=[u_spec, z_spec, mix_spec, vec_spec, u_spec], out_specs=[u_spec, u_spec, mix_spec, vec_spec],
        out_shape=[jax.ShapeDtypeStruct((lp, pw), BF16), jax.ShapeDtypeStruct((lp, pw), BF16),
                   jax.ShapeDtypeStruct((ng, pg, pg), F32), jax.ShapeDtypeStruct((1, pw), F32)],
        compiler_params=_params(("parallel",)),
    )(proj, proj, mix, scale, dy)


def _conv_pre(x, w, row):
    kw = w.shape[0]
    y = w[kw - 1:kw, :] * x
    for kk in range(kw - 1):
        y = y + w[kk:kk + 1, :] * _shift_down(x, kw - 1 - kk, row)
    return y


def _conv_post(y, out_scale):
    s = _silu(y)
    if out_scale is None:
        return s
    return s * lax.rsqrt(jnp.sum(s * s, axis=-1, keepdims=True) + NORM_EPS) * out_scale


def _conv_fwd(proj, col_off, w, hd, out_scale, name):
    lp = proj.shape[0]
    kw, width = w.shape
    blk0 = col_off // hd

    def body(x_ref, w_ref, o_ref):
        row = lax.broadcasted_iota(jnp.int32, (lp, 1), 0)
        o_ref[...] = _conv_post(_conv_pre(x_ref[...], w_ref[...], row), out_scale)

    return _call(
        body, name=name, grid=(width // hd,),
        in_specs=[pl.BlockSpec((lp, hd), lambda j: (0, blk0 + j)), pl.BlockSpec((kw, hd), lambda j: (0, j))],
        out_specs=pl.BlockSpec((lp, hd), lambda j: (0, j)),
        out_shape=jax.ShapeDtypeStruct((lp, width), F32), compiler_params=_params(("parallel",)),
    )(proj, w)


def _conv_bwd(proj, col_off, w, d_out, hd, out_scale, pad, name, dst):
    lp = proj.shape[0]
    kw, width = w.shape
    blk0 = col_off // hd

    def body(x_ref, w_ref, do_ref, dst_ref, dx_ref, dw_ref):
        row = lax.broadcasted_iota(jnp.int32, (lp, 1), 0)
        real = row >= pad
        x, wv = x_ref[...], w_ref[...]
        _, vjp = jax.vjp(functools.partial(_conv_post, out_scale=out_scale), _conv_pre(x, wv, row))
        dy = jnp.where(real, vjp(do_ref[...])[0], 0.0)
        dx = wv[kw - 1:kw, :] * dy
        dw_ref[kw - 1:kw, :] = jnp.sum(dy * x, axis=0, keepdims=True)
        for kk in range(kw - 1):
            ahead = _shift_up(dy, kw - 1 - kk, row)
            dx = dx + wv[kk:kk + 1, :] * ahead
            dw_ref[kk:kk + 1, :] = jnp.sum(ahead * x, axis=0, keepdims=True)
        dx_ref[...] = jnp.where(real, dx, 0.0).astype(BF16)

    col = pl.BlockSpec((lp, hd), lambda j: (0, j))
    at_off = pl.BlockSpec((lp, hd), lambda j: (0, blk0 + j))
    wspec = pl.BlockSpec((kw, hd), lambda j: (0, j))
    return _call(
        body, name=name, grid=(width // hd,),
        in_specs=[at_off, wspec, col, ANY], out_specs=[at_off, wspec],
        out_shape=[jax.ShapeDtypeStruct(dst.shape, BF16), jax.ShapeDtypeStruct((kw, width), F32)],
        input_output_aliases={3: 0}, compiler_params=_params(("parallel",)),
    )(proj, w, d_out, dst)


HEADS_PER_STEP = 16
HEADS_PER_STEP_BWD = 16


def _matmul_with_direct_vjp(dims, da_dims, db_dims, db_swapped):
    @jax.custom_vjp
    def mm(a, b):
        return _dot(a, b, dims)

    def fwd(a, b):
        return _dot(a, b, dims), (a, b)

    def bwd(res, g):
        a, b = res
        return _dot(g, b, da_dims) if not db_swapped[0] else _dot(b, g, da_dims), _dot(a, g, db_dims) if not db_swapped[1] else _dot(g, a, db_dims)

    mm.defvjp(fwd, bwd)
    return mm


_mm_nn = _matmul_with_direct_vjp(NN, NT, TN, (False, False))
_mm_nt = _matmul_with_direct_vjp(NT, NN, TN, (False, True))
_mm_tn = _matmul_with_direct_vjp(TN, NT, NN, (True, False))


def _each(fn, *lists):
    return [fn(*args) for args in zip(*lists)]


def _dot3_each(a_list, b_list, dims=NN):
    hi = lambda t: t.astype(BF16)
    lo = lambda t, t_hi: (t - t_hi.astype(F32)).astype(BF16)
    dot = lambda x, y: _dot(x, y, dims)
    a_hi, b_hi = _each(hi, a_list), _each(hi, b_list)
    a_lo, b_lo = _each(lo, a_list, a_hi), _each(lo, b_list, b_hi)
    hh, hl, lh = _each(dot, a_hi, b_hi), _each(dot, a_hi, b_lo), _each(dot, a_lo, b_hi)
    return _each(lambda x, y, w: x + (y + w), hh, hl, lh)


@jax.custom_vjp
def _unit_lower_inverse(lmats):
    c = lmats[0].shape[0]
    eye = lax.broadcasted_iota(jnp.int32, (c, c), 0) == lax.broadcasted_iota(jnp.int32, (c, c), 1)
    a = [-m for m in lmats]
    tmat = [jnp.where(eye, 1.0, 0.0).astype(F32) + m for m in a]
    span = 2
    while span < c:
        a = _dot3_each(a, a)
        tmat = _each(lambda t, u: t + u, tmat, _dot3_each(tmat, a))
        span *= 2
    return tuple(tmat)


def _unit_lower_inverse_fwd(lmats):
    tmats = _unit_lower_inverse(lmats)
    return tmats, tmats


def _unit_lower_inverse_bwd(tmats, cts):
    left = _each(lambda t, ct: _dot(t, ct, TN, HIGHEST), tmats, cts)
    return (tuple(_each(lambda m, t: -_dot(m, t, NT, HIGHEST), left, tmats)),)


_unit_lower_inverse.defvjp(_unit_lower_inverse_fwd, _unit_lower_inverse_bwd)


@jax.custom_vjp
def _known_inverse(lmats, tmats):
    return tmats


def _known_inverse_fwd(lmats, tmats):
    return tmats, tmats


def _known_inverse_bwd(tmats, cts):
    return _unit_lower_inverse_bwd(tmats, cts)[0], tuple(jnp.zeros_like(t) for t in tmats)


_known_inverse.defvjp(_known_inverse_fwd, _known_inverse_bwd)


def _chunk_math(states, q, k, v, ba, z, prm, nw, head0, rowmask, n_heads, tmats=None, keep_tmats=False):
    c = q.shape[0]
    heads = list(range(len(states)))
    hd = q.shape[1] // len(states)
    lane = lax.broadcasted_iota(jnp.int32, ba.shape, 1)
    sub = lax.broadcasted_iota(jnp.int32, (ba.shape[1], c), 0)
    ri = lax.broadcasted_iota(jnp.int32, (c, c), 0)
    ci = lax.broadcasted_iota(jnp.int32, (c, c), 1)
    last = lax.broadcasted_iota(jnp.int32, (c, 1), 0) == c - 1
    causal, strict = ri >= ci, ri > ci
    beta_all = _sigmoid(ba) * rowmask
    g_all = -jnp.exp(prm[0:1, :]) * _softplus(ba + prm[1:2, :]) * rowmask
    gcum_all = _dot(jnp.where(causal, 1.0, 0.0).astype(F32), g_all, precision=HIGHEST)
    gcum_t = gcum_all.T
    split = lambda t: [t[:, j * hd:(j + 1) * hd] for j in heads]
    qs, ks, vs, zs = split(q), split(k), split(v), split(z)
    beta = [jnp.sum(jnp.where(lane == head0 + j, beta_all, 0.0), axis=1, keepdims=True) for j in heads]
    gcum = [jnp.sum(jnp.where(lane == n_heads + head0 + j, gcum_all, 0.0), axis=1, keepdims=True) for j in heads]
    grow = [jnp.sum(jnp.where(sub == n_heads + head0 + j, gcum_t, 0.0), axis=0, keepdims=True) for j in heads]
    glast = _each(lambda gc: jnp.sum(jnp.where(last, gc, 0.0), axis=0, keepdims=True), gcum)
    decay = _each(lambda gc, gr: jnp.where(causal, jnp.exp(jnp.where(causal, gc - gr, 0.0)), 0.0), gcum, grow)
    eg = _each(jnp.exp, gcum)
    k_beta = _each(jnp.multiply, ks, beta)
    kk = _each(_mm_nt, k_beta, ks)
    lmats = tuple(_each(lambda m, dc: jnp.where(strict, m * dc, 0.0), kk, decay))
    tmat = list(_unit_lower_inverse(lmats) if tmats is None else _known_inverse(lmats, tuple(tmats)))
    u_c = _each(_mm_nn, tmat, _each(jnp.multiply, vs, beta))
    w_c = _each(_mm_nn, tmat, _each(jnp.multiply, k_beta, eg))
    qk = _each(lambda a, b, dc: jnp.where(causal, _mm_nt(a, b) * dc, 0.0), qs, ks, decay)
    v_new = _each(lambda u, w, s: u - _mm_nn(w, s), u_c, w_c, list(states))
    o = _each(lambda a, e, s, m, vn: _mm_nn(a * e, s) + _mm_nn(m, vn), qs, eg, list(states), qk, v_new)
    k_dec = _each(lambda a, gl, gc: a * jnp.exp(gl - gc), ks, glast, gcum)
    new_states = _each(lambda s, gl, kd, vn: s * jnp.exp(gl) + _mm_tn(kd, vn), list(states), glast, k_dec, v_new)
    ys = _each(lambda oj, zj: _rmsnorm(oj, nw) * _silu(zj), o, zs)
    if keep_tmats:
        return jnp.concatenate(ys, axis=1), tuple(new_states), tuple(tmat)
    return jnp.concatenate(ys, axis=1), tuple(new_states)


def _chunk_specs(nc, hd, n_heads, z_off, ba_off, rev):
    cidx = (lambda c: nc - 1 - c) if rev else (lambda c: c)
    hb = min(HEADS_PER_STEP_BWD if rev else HEADS_PER_STEP, n_heads)
    assert n_heads % hb == 0 and z_off % (hb * hd) == 0 and ba_off % LANES == 0
    blk = lambda off: pl.BlockSpec((CHUNK, hb * hd), lambda c, g: (cidx(c), off + g))
    ba_spec = lambda off: pl.BlockSpec((CHUNK, LANES), lambda c, g: (cidx(c), off // LANES))
    prm_spec = pl.BlockSpec((8, LANES), lambda c, g: (0, 0))
    nw_spec = pl.BlockSpec((1, hd), lambda c, g: (0, 0))
    st_spec = pl.BlockSpec((1, hb, hd, hd), lambda c, g: (cidx(c), g, 0, 0))
    return blk, ba_spec, prm_spec, nw_spec, st_spec, blk(z_off // (hb * hd))


def _rowmask(chunk_idx, pad):
    row = chunk_idx * CHUNK + lax.broadcasted_iota(jnp.int32, (CHUNK, 1), 0)
    return jnp.where(row >= pad, 1.0, 0.0).astype(F32)


def _chunk_fwd(qn, kn, vv, proj, z_off, ba_off, prm, nw, n_heads, pad):
    lp, dn = qn.shape
    hd = dn // n_heads
    nc = lp // CHUNK
    hb = min(HEADS_PER_STEP, n_heads)

    def body(q_ref, k_ref, v_ref, ba_ref, z_ref, prm_ref, nw_ref, y_ref, hist_ref, tm_ref, st_ref):
        c, g = pl.program_id(0), pl.program_id(1)

        @pl.when(c == 0)
        def _():
            for j in range(hb):
                st_ref[g * hb + j] = jnp.zeros((hd, hd), F32)

        states = tuple(st_ref[g * hb + j] for j in range(hb))
        for j in range(hb):
            hist_ref[0, j] = states[j]
        y, new_states, tmats = _chunk_math(states, q_ref[...], k_ref[...], v_ref[...], ba_ref[...], z_ref[...], prm_ref[...],
                                           nw_ref[...], g * hb, _rowmask(c, pad), n_heads, keep_tmats=True)
        y_ref[...] = y.astype(BF16)
        for j in range(hb):
            st_ref[g * hb + j] = new_states[j]
            tm_ref[0, j] = tmats[j]

    blk, ba_spec, prm_spec, nw_spec, st_spec, z_spec = _chunk_specs(nc, hd, n_heads, z_off, ba_off, False)
    tm_spec = pl.BlockSpec((1, hb, CHUNK, CHUNK), lambda c, g: (c, g, 0, 0))
    return _call(
        body, name="chunk_fwd", grid=(nc, n_heads // hb),
        in_specs=[blk(0), blk(0), blk(0), ba_spec(ba_off), z_spec, prm_spec, nw_spec], out_specs=[blk(0), st_spec, tm_spec],
        out_shape=[jax.ShapeDtypeStruct((lp, dn), BF16), jax.ShapeDtypeStruct((nc, n_heads, hd, hd), F32),
                   jax.ShapeDtypeStruct((nc, n_heads, CHUNK, CHUNK), F32)],
        scratch_shapes=[pltpu.VMEM((n_heads, hd, hd), F32)],
        compiler_params=_params(("arbitrary", "arbitrary")),
    )(qn, kn, vv, proj, proj, prm, nw)


def _chunk_bwd(qn, kn, vv, proj, z_off, ba_off, prm, nw, hist, tmats, dy, n_heads, pad, d_proj):
    lp, dn = qn.shape
    hd = dn // n_heads
    nc = lp // CHUNK
    hb = min(HEADS_PER_STEP_BWD, n_heads)

    def body(q_ref, k_ref, v_ref, ba_ref, z_ref, prm_ref, nw_ref, hist_ref, tm_ref, dy_ref, d_proj_ref,
             dq_ref, dk_ref, dv_ref, dba_ref, dz_ref, dprm_ref, dnw_ref, dst_ref):
        step, g = pl.program_id(0), pl.program_id(1)

        @pl.when(step == 0)
        def _():
            for j in range(hb):
                dst_ref[g * hb + j] = jnp.zeros((hd, hd), F32)

        @pl.when((step == 0) & (g == 0))
        def _():
            dprm_ref[...] = jnp.zeros_like(dprm_ref)
            dnw_ref[...] = jnp.zeros_like(dnw_ref)

        @pl.when(g == 0)
        def _():
            dba_ref[...] = jnp.zeros_like(dba_ref)

        def fn(states, q, k, v, ba, z, prm_v, nw_v, known):
            return _chunk_math(states, q, k, v, ba, z, prm_v, nw_v, g * hb, _rowmask(nc - 1 - step, pad), n_heads, tmats=known)

        states = tuple(hist_ref[0, j] for j in range(hb))
        known = tuple(tm_ref[0, j] for j in range(hb))
        _, vjp = jax.vjp(fn, states, q_ref[...], k_ref[...], v_ref[...], ba_ref[...], z_ref[...], prm_ref[...], nw_ref[...], known)
        dst, dq, dk, dv, dba, dz, dprm, dnw, _ = vjp((dy_ref[...], tuple(dst_ref[g * hb + j] for j in range(hb))))
        for j in range(hb):
            dst_ref[g * hb + j] = dst[j]
        dq_ref[...] = dq
        dk_ref[...] = dk
        dv_ref[...] = dv
        dz_ref[...] = dz.astype(BF16)
        dba_ref[...] += dba
        dprm_ref[...] += dprm
        dnw_ref[...] += dnw

    blk, ba_spec, prm_spec, nw_spec, st_spec, z_spec = _chunk_specs(nc, hd, n_heads, z_off, ba_off, True)
    f32_full = jax.ShapeDtypeStruct((lp, dn), F32)
    tm_spec = pl.BlockSpec((1, hb, CHUNK, CHUNK), lambda c, g: (nc - 1 - c, g, 0, 0))
    return _call(
        body, name="chunk_bwd", grid=(nc, n_heads // hb),
        in_specs=[blk(0), blk(0), blk(0), ba_spec(ba_off), z_spec, prm_spec, nw_spec, st_spec, tm_spec, blk(0), ANY],
        out_specs=[blk(0), blk(0), blk(0), ba_spec(0), z_spec, prm_spec, nw_spec],
        out_shape=[f32_full, f32_full, f32_full, jax.ShapeDtypeStruct((lp, LANES), F32), jax.ShapeDtypeStruct(d_proj.shape, BF16),
                   jax.ShapeDtypeStruct((8, LANES), F32), jax.ShapeDtypeStruct((1, hd), F32)],
        scratch_shapes=[pltpu.VMEM((n_heads, hd, hd), F32)],
        input_output_aliases={10: 4}, compiler_params=_params(("arbitrary", "arbitrary")),
    )(qn, kn, vv, proj, proj, prm, nw, hist, tmats, dy, d_proj)


def _merge_math(p, q, gp, gd):
    return _sigmoid(gp) * p + _sigmoid(gd) * q


def _merge_specs(lp, d, gp_off, gd_off):
    tr, tc = _tile(lp, 264, 16), _tile(d, 1024, LANES)
    blk = pl.BlockSpec((tr, tc), lambda i, j: (i, j))
    gp_spec = pl.BlockSpec((tr, tc), lambda i, j: (i, gp_off // tc + j))
    gd_spec = pl.BlockSpec((tr, tc), lambda i, j: (i, gd_off // tc + j))
    return (lp // tr, d // tc), blk, gp_spec, gd_spec


def _merge_fwd(p, q, proj, gp_off, gd_off):
    lp, d = p.shape
    grid, blk, gp_spec, gd_spec = _merge_specs(lp, d, gp_off, gd_off)

    def body(p_ref, q_ref, gp_ref, gd_ref, o_ref):
        o_ref[...] = _merge_math(p_ref[...], q_ref[...], gp_ref[...], gd_ref[...]).astype(BF16)

    return _call(
        body, name="merge_fwd", grid=grid, in_specs=[blk, blk, gp_spec, gd_spec], out_specs=blk,
        out_shape=jax.ShapeDtypeStruct((lp, d), BF16), compiler_params=_params(("parallel", "parallel")),
    )(p, q, proj, proj)


def _merge_bwd(p, q, proj, gp_off, gd_off, dm):
    lp, d = p.shape
    grid, blk, gp_spec, gd_spec = _merge_specs(lp, d, gp_off, gd_off)

    def body(p_ref, q_ref, gp_ref, gd_ref, dm_ref, dp_ref, dq_ref, dgp_ref, dgd_ref):
        _, vjp = jax.vjp(_merge_math, p_ref[...], q_ref[...], gp_ref[...], gd_ref[...])
        for ref, val in zip((dp_ref, dq_ref, dgp_ref, dgd_ref), vjp(dm_ref[...])):
            ref[...] = val.astype(BF16)

    out = jax.ShapeDtypeStruct((lp, d), BF16)
    return _call(
        body, name="merge_bwd", grid=grid, in_specs=[blk, blk, gp_spec, gd_spec, blk], out_specs=[blk] * 4,
        out_shape=[out] * 4, compiler_params=_params(("parallel", "parallel")),
    )(p, q, proj, proj, dm)


def _adamw(w, g, m, v, name):
    shape = w.shape
    w2, g2, m2, v2 = (t.reshape((-1, shape[-1])) for t in (w, g, m, v))
    rows, cols = w2.shape
    tr = _tile(rows, 128, 8)

    def body(w_ref, g_ref, m_ref, v_ref, d_ref, nm_ref, nv_ref):
        gv = g_ref[...]
        nm = ADAM_B1 * m_ref[...] + (1.0 - ADAM_B1) * gv
        nv = ADAM_B2 * v_ref[...] + (1.0 - ADAM_B2) * (gv * gv)
        m_hat = nm / (1.0 - ADAM_B1 ** ADAM_STEP)
        v_hat = nv / (1.0 - ADAM_B2 ** ADAM_STEP)
        d_ref[...] = -ADAM_LR * (m_hat / (jnp.sqrt(v_hat) + ADAM_EPS) + ADAM_WD * w_ref[...])
        nm_ref[...] = nm
        nv_ref[...] = nv

    blk = pl.BlockSpec((tr, cols), lambda i: (i, 0))
    out = jax.ShapeDtypeStruct((rows, cols), F32)
    res = _call(
        body, name=name, grid=(rows // tr,), in_specs=[blk] * 4, out_specs=[blk] * 3, out_shape=[out] * 3,
        compiler_params=_params(("parallel",)),
    )(w2, g2, m2, v2)
    return tuple(t.reshape(shape) for t in res)


def _coords():
    return lax.axis_index("x"), lax.axis_index("y"), lax.axis_index("c")


def _flip(v, bit):
    return 1 - v if bit else v


CHIP_FLIPS = ((1, 0), (0, 1), (1, 1))
ANY = pl.BlockSpec(memory_space=pl.ANY)


def _all_gather(shards):
    n = len(shards)

    def body(*refs):
        x_refs, out_refs = refs[:n], refs[n:2 * n]
        send_sems, recv_sems, local_sems = refs[2 * n:]
        x, y, c = _coords()
        sibling = (x, y, 1 - c)
        chips = [(_flip(x, fx), _flip(y, fy)) for fx, fy in CHIP_FLIPS]

        def copy(a, k, block, to, from_input=False):
            px, py, pc = block
            slot = out_refs[a].at[4 * px + 2 * py + pc]
            return pltpu.make_async_remote_copy(
                src_ref=x_refs[a] if from_input else slot, dst_ref=slot,
                send_sem=send_sems.at[7 * a + k], recv_sem=recv_sems.at[7 * a + k], device_id=to, device_id_type=MESH)

        mine = [pltpu.make_async_copy(x_refs[a], out_refs[a].at[4 * x + 2 * y + c], local_sems.at[a]) for a in range(n)]
        first = []
        for a in range(n):
            mine[a].start()
            first.append(copy(a, 0, (x, y, c), sibling, True))
            first += [copy(a, 1 + j, (x, y, c), (*chip, c), True) for j, chip in enumerate(chips)]
        for cp in first:
            cp.start()
        passed = []
        for j, chip in enumerate(chips):
            for a in range(n):
                copy(a, 1 + j, (*chip, c), (x, y, c)).wait_recv()
                passed.append(copy(a, 4 + j, (*chip, c), sibling))
                passed[-1].start()
        for a in range(n):
            copy(a, 0, (x, y, 1 - c), (x, y, c)).wait_recv()
            for j, chip in enumerate(chips):
                copy(a, 4 + j, (*chip, 1 - c), (x, y, c)).wait_recv()
        for cp in first + passed:
            cp.wait_send()
        for cp in mine:
            cp.wait()

    return _call(
        body, name="all_gather", in_specs=[ANY] * n, out_specs=[ANY] * n,
        out_shape=[jax.ShapeDtypeStruct((N_DEV,) + s.shape, s.dtype) for s in shards],
        scratch_shapes=[pltpu.SemaphoreType.DMA((7 * n,)), pltpu.SemaphoreType.DMA((7 * n,)), pltpu.SemaphoreType.DMA((n,))],
    )(*shards)


def _all_gather_tree(shard, after):
    rows, cols = shard.shape
    half = rows // 2
    assert rows % 32 == 0

    def body(x_ref, after_ref, out_ref, send_sems, recv_sems, local_sem):
        x, y, c = _coords()
        me, sibling = (x, y, c), (x, y, 1 - c)
        x_nbr, y_nbr, diag = (1 - x, y), (x, 1 - y), (1 - x, 1 - y)

        def part(ref, h):
            return ref if h is None else ref.at[pl.ds(h * half, half)]

        def copy(k, block, to, h=None, from_input=False):
            px, py, pc = block
            slot = part(out_ref.at[4 * px + 2 * py + pc], h)
            return pltpu.make_async_remote_copy(
                src_ref=part(x_ref, h) if from_input else slot, dst_ref=slot,
                send_sem=send_sems.at[k], recv_sem=recv_sems.at[k], device_id=to, device_id_type=MESH)

        mine = pltpu.make_async_copy(x_ref, out_ref.at[4 * x + 2 * y + c], local_sem)
        mine.start()
        started = [copy(0, me, sibling, None, True),
                   copy(1, me, (*x_nbr, c), 0, True), copy(2, me, (*x_nbr, c), 1, True),
                   copy(4, me, (*y_nbr, c), 1, True), copy(3, me, (*y_nbr, c), 0, True)]
        for cp in started:
            cp.start()
        copy(1, (*x_nbr, c), me, 0).wait_recv()
        started.append(copy(5, (*x_nbr, c), (*y_nbr, c), 0))
        started[-1].start()
        copy(4, (*y_nbr, c), me, 1).wait_recv()
        started.append(copy(6, (*y_nbr, c), (*x_nbr, c), 1))
        started[-1].start()
        copy(2, (*x_nbr, c), me, 1).wait_recv()
        started.append(copy(7, (*x_nbr, c), sibling))
        started[-1].start()
        copy(3, (*y_nbr, c), me, 0).wait_recv()
        started.append(copy(8, (*y_nbr, c), sibling))
        started[-1].start()
        copy(5, (*diag, c), me, 0).wait_recv()
        copy(6, (*diag, c), me, 1).wait_recv()
        started.append(copy(9, (*diag, c), sibling))
        started[-1].start()
        copy(0, sibling, me).wait_recv()
        for k, chip in ((7, x_nbr), (8, y_nbr), (9, diag)):
            copy(k, (*chip, 1 - c), me).wait_recv()
        for cp in started:
            cp.wait_send()
        mine.wait()

    return _call(
        body, name="all_gather_tree", in_specs=[ANY, ANY], out_specs=ANY,
        out_shape=jax.ShapeDtypeStruct((N_DEV, rows, cols), shard.dtype),
        scratch_shapes=[pltpu.SemaphoreType.DMA((10,)), pltpu.SemaphoreType.DMA((10,)), pltpu.SemaphoreType.DMA],
    )(shard, after)


def _rs_to_sibling(gs, name):
    n = len(gs)

    def body(*refs):
        g_refs, got_refs = refs[:n], refs[n:2 * n]
        send_sems, recv_sems = refs[2 * n:]
        x, y, c = _coords()
        copies = []
        for a in range(n):
            for p in range(4):
                cp = pltpu.make_async_remote_copy(
                    src_ref=g_refs[a].at[2 * p + (1 - c)], dst_ref=got_refs[a].at[p], send_sem=send_sems.at[4 * a + p],
                    recv_sem=recv_sems.at[4 * a + p], device_id=(x, y, 1 - c), device_id_type=MESH)
                cp.start()
                copies.append(cp)
        for cp in copies:
            cp.wait()

    return _call(
        body, name=name, in_specs=[ANY] * n, out_specs=[ANY] * n,
        out_shape=[jax.ShapeDtypeStruct((4,) + g.shape[1:], g.dtype) for g in gs],
        scratch_shapes=[pltpu.SemaphoreType.DMA((4 * n,)), pltpu.SemaphoreType.DMA((4 * n,))],
    )(*gs)


def _rs_pair_sum(g, got, c_idx, name):
    _, rows, cols = g.shape
    tr = _tile(rows, 256, 16)

    def body(c_ref, g_ref, got_ref, o_ref):
        o_ref[...] = (g_ref[...].astype(F32) + got_ref[...].astype(F32)).astype(o_ref.dtype)

    grid_spec = pltpu.PrefetchScalarGridSpec(
        num_scalar_prefetch=1, grid=(4, rows // tr),
        in_specs=[pl.BlockSpec((1, tr, cols), lambda p, i, c_ref: (2 * p + c_ref[0], i, 0)),
                  pl.BlockSpec((1, tr, cols), lambda p, i, c_ref: (p, i, 0))],
        out_specs=pl.BlockSpec((1, tr, cols), lambda p, i, c_ref: (p, i, 0)))
    return _call(
        body, name=name, grid_spec=grid_spec, out_shape=jax.ShapeDtypeStruct((4, rows, cols), g.dtype),
        compiler_params=_params(("parallel", "parallel")),
    )(c_idx, g, got)


def _to_chips_copies(p_refs, got_refs, send_sems, recv_sems):
    x, y, c = _coords()
    copies = []
    for a in range(len(p_refs)):
        for k, (fx, fy) in enumerate(CHIP_FLIPS):
            px, py = _flip(x, fx), _flip(y, fy)
            copies.append(pltpu.make_async_remote_copy(
                src_ref=p_refs[a].at[2 * px + py], dst_ref=got_refs[a].at[k], send_sem=send_sems.at[3 * a + k],
                recv_sem=recv_sems.at[3 * a + k], device_id=(px, py, c), device_id_type=MESH))
    return copies


def _rs_to_chips(partials, name):
    n = len(partials)

    def body(*refs):
        copies = _to_chips_copies(refs[:n], refs[n:2 * n], *refs[2 * n:])
        for cp in copies:
            cp.start()
        for cp in copies:
            cp.wait()

    return _call(
        body, name=name, in_specs=[ANY] * n, out_specs=[ANY] * n,
        out_shape=[jax.ShapeDtypeStruct((3,) + p.shape[1:], p.dtype) for p in partials],
        scratch_shapes=[pltpu.SemaphoreType.DMA((3 * n,)), pltpu.SemaphoreType.DMA((3 * n,))],
    )(*partials)


HBM = pl.BlockSpec(memory_space=pltpu.HBM)
SEM = pl.BlockSpec(memory_space=pltpu.SEMAPHORE)
SIDE_EFFECT = pltpu.CompilerParams(has_side_effects=pltpu.SideEffectType.DATAFLOW_SIDE_EFFECTING)


def _split_start(copies_fn, srcs, land_shapes, n_sems, name, after=None):
    n, m = len(srcs), len(land_shapes)
    extra = [] if after is None else [after]

    def body(*refs):
        outs = refs[n + m + len(extra):]
        send_sems, recv_sems, token = outs[0], outs[1], outs[-1]
        for cp in copies_fn(refs[:n], refs[n:n + m], send_sems, recv_sems):
            cp.start()
        token[...] = jnp.zeros_like(token)

    ins = [pltpu.with_memory_space_constraint(t, pltpu.HBM) for t in list(srcs) + [lax.empty(s.shape, s.dtype) for s in land_shapes]]
    res = _call(
        body, name=name, in_specs=[HBM] * (n + m) + [ANY] * len(extra),
        out_specs=[SEM, SEM] + [HBM] * (n + m) + [pl.BlockSpec(memory_space=pltpu.VMEM)],
        out_shape=[pltpu.SemaphoreType.DMA((n_sems,)), pltpu.SemaphoreType.DMA((n_sems,))]
        + [pltpu.HBM(t.shape, t.dtype) for t in ins] + [jax.ShapeDtypeStruct((8, LANES), F32)],
        input_output_aliases={i: 2 + i for i in range(n + m)}, compiler_params=SIDE_EFFECT,
    )(*ins, *extra)
    return dict(sems=(res[0], res[1]), srcs=res[2:2 + n], lands=res[2 + n:2 + n + m], token=res[-1])


def _split_wait(copies_fn, started, after, name):
    n, m = len(started["srcs"]), len(started["lands"])

    def body(*refs):
        for cp in copies_fn(refs[:n], refs[n:n + m], refs[n + m], refs[n + m + 1]):
            cp.wait_send()
            cp.wait_recv()

    bufs = list(started["srcs"]) + list(started["lands"])
    res = _call(
        body, name=name, in_specs=[HBM] * (n + m) + [SEM, SEM, ANY], out_specs=[HBM] * (n + m),
        out_shape=[pltpu.HBM(t.shape, t.dtype) for t in bufs],
        input_output_aliases={i: i for i in range(n + m)}, compiler_params=SIDE_EFFECT,
    )(*bufs, *started["sems"], after)
    return res[:n], res[n:]


def _to_all_copies(x_refs, out_refs, send_sems, recv_sems):
    x, y, c = _coords()
    copies = []
    for a in range(len(x_refs)):
        for k in range(N_DEV - 1):
            fx, fy, fc = ((k + 1) >> 2) & 1, ((k + 1) >> 1) & 1, (k + 1) & 1
            copies.append(pltpu.make_async_remote_copy(
                src_ref=x_refs[a], dst_ref=out_refs[a].at[4 * x + 2 * y + c], send_sem=send_sems.at[7 * a + k],
                recv_sem=recv_sems.at[7 * a + k], device_id=(_flip(x, fx), _flip(y, fy), _flip(c, fc)), device_id_type=MESH))
    return copies


def _fill_own_block(gathered, shard, me_idx, name):
    rows, cols = shard.shape
    tr = _tile(rows, 512, 16)

    def body(me_ref, g_ref, s_ref, o_ref):
        o_ref[0] = s_ref[...]

    grid_spec = pltpu.PrefetchScalarGridSpec(
        num_scalar_prefetch=1, grid=(rows // tr,),
        in_specs=[ANY, pl.BlockSpec((tr, cols), lambda i, me: (i, 0))],
        out_specs=pl.BlockSpec((1, tr, cols), lambda i, me: (me[0], i, 0)))
    return _call(
        body, name=name, grid_spec=grid_spec, out_shape=jax.ShapeDtypeStruct(gathered.shape, gathered.dtype),
        input_output_aliases={1: 0}, compiler_params=_params(("arbitrary",)),
    )(me_idx, gathered, shard)


def _rs_chip_sum(partial, got, chip_idx, name, part=0, n_parts=1, dst=None):
    _, rows, cols = partial.shape
    tr = _tile(rows, 256, 16)
    steps = rows // tr
    n_dst = 0 if dst is None else 1

    def body(p_idx_ref, p_ref, got_ref, *refs):
        refs[n_dst][...] = ((p_ref[0].astype(F32) + got_ref[0].astype(F32)) + got_ref[1].astype(F32)) + got_ref[2].astype(F32)

    grid_spec = pltpu.PrefetchScalarGridSpec(
        num_scalar_prefetch=1, grid=(steps,),
        in_specs=[pl.BlockSpec((1, tr, cols), lambda i, p_ref: (p_ref[0], i, 0)),
                  pl.BlockSpec((3, tr, cols), lambda i, p_ref: (0, i, 0))] + [ANY] * n_dst,
        out_specs=pl.BlockSpec((tr, cols), lambda i, p_ref: (part * steps + i, 0)))
    return _call(
        body, name=name, grid_spec=grid_spec, out_shape=jax.ShapeDtypeStruct((n_parts * rows, cols), F32),
        input_output_aliases={3: 0} if n_dst else {}, compiler_params=_params(("parallel",)),
    )(chip_idx, partial, got, *([] if dst is None else [dst]))


def _rs_begin(gs, tag, split):
    c_idx = jnp.reshape(lax.axis_index("c"), (1,)).astype(jnp.int32)
    gots = _rs_to_sibling(gs, "rs_to_sibling_" + tag)
    partials = [_rs_pair_sum(g, got, c_idx, "rs_pair_sum_%s%d" % (tag, a)) for a, (g, got) in enumerate(zip(gs, gots))]
    if not split:
        return dict(partials=partials, gots=_rs_to_chips(partials, "rs_to_chips_" + tag))
    lands = [jax.ShapeDtypeStruct((3,) + p.shape[1:], p.dtype) for p in partials]
    return _split_start(_to_chips_copies, partials, lands, 3 * len(partials), "rs_to_chips_start_" + tag)


def _rs_finish(begun, tag, after=None, part=0, n_parts=1, dsts=None):
    x, y, _ = _coords()
    chip_idx = jnp.reshape(2 * x + y, (1,)).astype(jnp.int32)
    if "gots" in begun:
        partials, gots = begun["partials"], begun["gots"]
    else:
        partials, gots = _split_wait(_to_chips_copies, begun, after, "rs_to_chips_wait_" + tag)
    return [_rs_chip_sum(p, got, chip_idx, "rs_chip_sum_%s%d" % (tag, a), part, n_parts, None if dsts is None else dsts[a])
            for a, (p, got) in enumerate(zip(partials, gots))]


RUNS = 3
RUN_FIELDS = 5


def _lane_gather_table(src_of, ahead):
    n_blocks = src_of.shape[0] // LANES
    tab = np.zeros((n_blocks + 3 * ahead, RUNS, RUN_FIELDS), np.int32)
    for t in range(n_blocks):
        runs = []
        for lane in range(LANES):
            slab, col = (int(v) for v in src_of[t * LANES + lane])
            if slab < 0:
                continue
            key = (slab, col // LANES, col % LANES - lane)
            if runs and runs[-1][0] == key and runs[-1][2] == lane:
                runs[-1][2] = lane + 1
            else:
                runs.append([key, lane, lane + 1])
        assert len(runs) <= RUNS
        slots = [None] * RUNS
        for key, lo, hi in sorted(runs, key=lambda r: r[0][:2]):
            slots[slots.index(None)] = (key[0], key[1], key[2], lo, hi)
        for e in range(RUNS):
            kept = (tab[t - ahead, e, 0], tab[t - ahead, e, 1], 0, 0, 0) if t >= ahead else tab[t, e]
            tab[t, e] = slots[e] if slots[e] is not None else kept
    tab[n_blocks:, :, :2] = np.tile(tab[n_blocks - ahead:n_blocks, :, :2], (3, 1, 1))
    return tab.reshape(-1)


def _place_run(tab_ref, t, e, block, under):
    base = (t * RUNS + e) * RUN_FIELDS
    shift, lo, hi = tab_ref[base + 2], tab_ref[base + 3], tab_ref[base + 4]
    lane = lax.broadcasted_iota(jnp.int32, (1, LANES), 1)
    return jnp.where((lane >= lo) & (lane < hi), pltpu.roll(block.astype(F32), (LANES - shift) % LANES, 1), under)


def _lane_gather_cols(src, table, out_slabs, out_width, name, sub):
    _, rows, _ = src.shape
    steps_per_slab = -(-out_width // (sub * LANES))

    def body(tab_ref, *refs):
        o_ref = refs[sub * RUNS]
        for s in range(sub):
            t = pl.program_id(0) * sub + s
            ops = refs[s * RUNS:(s + 1) * RUNS]
            lanes = slice(s * LANES, (s + 1) * LANES)
            o_ref[0, :, lanes] = _place_run(tab_ref, t, 1, ops[1][0], _place_run(tab_ref, t, 0, ops[0][0], 0.0)).astype(BF16)
            last = (t * RUNS + RUNS - 1) * RUN_FIELDS

            @pl.when(tab_ref[last + 4] > tab_ref[last + 3])
            def _():
                o_ref[0, :, lanes] = _place_run(tab_ref, t, RUNS - 1, ops[RUNS - 1][0], o_ref[0, :, lanes].astype(F32)).astype(BF16)

    def src_spec(s, e):
        at = lambda t: ((t * sub + s) * RUNS + e) * RUN_FIELDS
        return pl.BlockSpec((1, rows, LANES), lambda t, tab: (tab[at(t)], 0, tab[at(t) + 1]))

    grid_spec = pltpu.PrefetchScalarGridSpec(
        num_scalar_prefetch=1, grid=(out_slabs * steps_per_slab,),
        in_specs=[src_spec(s, e) for s in range(sub) for e in range(RUNS)],
        out_specs=pl.BlockSpec((1, rows, sub * LANES), lambda t, tab: (t // steps_per_slab, 0, t % steps_per_slab)))
    return _call(
        body, name=name, grid_spec=grid_spec, out_shape=jax.ShapeDtypeStruct((out_slabs, rows, out_width), BF16),
        compiler_params=_params(("arbitrary",)),
    )(jnp.asarray(table), *([src] * (sub * RUNS)))


def _all_reduce_small(vec):
    rows, cols = vec.shape

    def body(v_ref, o_ref, buf, send_sems, recv_sems):
        x, y, c = _coords()
        me = 4 * x + 2 * y + c
        buf[me] = v_ref[...]
        copies = []
        for k in range(N_DEV - 1):
            fx, fy, fc = ((k + 1) >> 2) & 1, ((k + 1) >> 1) & 1, (k + 1) & 1
            cp = pltpu.make_async_remote_copy(
                src_ref=v_ref, dst_ref=buf.at[me], send_sem=send_sems.at[k], recv_sem=recv_sems.at[k],
                device_id=(_flip(x, fx), _flip(y, fy), _flip(c, fc)), device_id_type=MESH)
            cp.start()
            copies.append(cp)
        for cp in copies:
            cp.wait()
        total = buf[0]
        for j in range(1, N_DEV):
            total = total + buf[j]
        o_ref[...] = total

    vmem = pl.BlockSpec(memory_space=pltpu.VMEM)
    return _call(
        body, name="all_reduce_small", in_specs=[vmem], out_specs=vmem,
        out_shape=jax.ShapeDtypeStruct((rows, cols), F32),
        scratch_shapes=[pltpu.VMEM((N_DEV, rows, cols), F32), pltpu.SemaphoreType.DMA((N_DEV - 1,)),
                        pltpu.SemaphoreType.DMA((N_DEV - 1,))],
    )(vec)


def _w_in_column_maps(ns, o_ba, n_logit, n_main, n_all, own_sub):
    own = np.arange(N_DEV * ns)
    work_of_own = np.where(own < o_ba, own, np.where(own < o_ba + n_logit, n_main + own - o_ba, own - n_logit))
    to_work = np.full((n_all, 2), -1, np.int64)
    to_work[work_of_own, 0] = own // ns
    to_work[work_of_own, 1] = own % ns
    slab_width = -(-ns // (own_sub * LANES)) * own_sub * LANES
    to_own = np.full((N_DEV, slab_width, 2), -1, np.int64)
    to_own[:, :ns, 0] = 0
    to_own[:, :ns, 1] = work_of_own.reshape(N_DEV, ns)
    return to_work, to_own.reshape(-1, 2)


def kernel(x, meta_tokens, norm_w, w_in, conv_w, A_log, dt_bias, pool_mix, pool_scale, dn_norm_w, w_pool_out, w_dn_out, w_o, final_norm_w, loss_target, m_meta_tokens, m_norm_w, m_w_in, m_conv_w, m_A_log, m_dt_bias, m_pool_mix, m_pool_scale, m_dn_norm_w, m_w_pool_out, m_w_dn_out, m_w_o, m_final_norm_w, v_meta_tokens, v_norm_w, v_w_in, v_conv_w, v_A_log, v_dt_bias, v_pool_mix, v_pool_scale, v_dn_norm_w, v_w_pool_out, v_w_dn_out, v_w_o, v_final_norm_w):
    seq, d = x.shape[1], x.shape[2]
    n_meta = meta_tokens.shape[0]
    n_heads, hd = A_log.shape[-1], dn_norm_w.shape[-1]
    dn = n_heads * hd
    pw, ng = pool_scale.shape[-1], pool_mix.shape[1]
    pg = pw // ng
    kw = conv_w.shape[1]
    pad = (-n_meta) % CHUNK
    x0 = pad + n_meta
    lp = x0 + seq
    ns = w_in.shape[-1]
    in_cols = N_DEV * ns
    o_q, o_k, o_v, o_zd = 2 * pw, 2 * pw + dn, 2 * pw + 2 * dn, 2 * pw + 3 * dn
    o_ba = 2 * pw + 4 * dn
    o_gp, o_gd = o_ba, o_ba + d
    n_main = o_gd + d
    n_all = n_main + 2 * LANES
    assert lp % CHUNK == 0 and in_cols == n_main + 2 * n_heads and 2 * n_heads <= LANES and hd == LANES
    cs, ms = conv_w.shape[-1], meta_tokens.shape[-1]
    mr = pool_mix.shape[2]
    assert ms == pg and cs % pg == 0
    work_sub = max(s for s in (6, 3, 2, 1) if (n_all // LANES) % s == 0)
    own_sub = 15
    to_work, to_own = _w_in_column_maps(ns, o_ba, 2 * n_heads, n_main, n_all, own_sub)
    cols_major = lambda t: jnp.transpose(t, (1, 0, 2)).reshape(t.shape[1], N_DEV * t.shape[2])

    mix_g, conv_g, meta_g = _all_gather([pool_mix[0].reshape(ng * mr, pg).astype(BF16), conv_w[0], meta_tokens])
    win_g = _all_gather_tree(w_in[0].astype(BF16), after=meta_g)
    late_shards = [w_pool_out[0].astype(BF16), w_dn_out[0].astype(BF16), w_o[0].astype(BF16)]
    late_weights = _split_start(_to_all_copies, late_shards, [jax.ShapeDtypeStruct((N_DEV,) + s.shape, BF16) for s in late_shards],
                                (N_DEV - 1) * len(late_shards), "gather_out_proj_start", after=win_g)
    norm_w_in = norm_w + late_weights["token"][0, 0]
    w_all = _lane_gather_cols(win_g, _lane_gather_table(to_work, work_sub), 1, n_all, "w_in_to_work", work_sub).reshape(d, n_all)
    mix_f = jnp.transpose(mix_g.reshape(N_DEV, ng, mr, pg), (1, 0, 2, 3)).reshape(ng, pg, pg)
    conv_f = cols_major(conv_g)
    meta_f = cols_major(meta_g)

    h0, xn = _norm_in_fwd(x[0], meta_f, norm_w_in, pad)
    proj = _matmul(xn, w_all, NN, F32, lp, 768, 2048, "proj")
    y_pool = _pool_fwd(proj, mix_f, pool_scale, pad)
    conv_q, conv_k, conv_v = (conv_f[:, i * dn:(i + 1) * dn] for i in range(3))
    qn = _conv_fwd(proj, o_q, conv_q, hd, float(hd) ** -0.5, "conv_q_fwd")
    kn = _conv_fwd(proj, o_k, conv_k, hd, 1.0, "conv_k_fwd")
    vv = _conv_fwd(proj, o_v, conv_v, hd, None, "conv_v_fwd")
    logit_lanes = (n_heads, LANES - 2 * n_heads)
    prm = jnp.pad(A_log, ((0, 7), logit_lanes)) + jnp.pad(dt_bias, ((1, 6), logit_lanes))
    y_dn, hist, tmats = _chunk_fwd(qn, kn, vv, proj, o_zd, n_main, prm, dn_norm_w, n_heads, pad)
    me_idx = jnp.reshape(4 * lax.axis_index("x") + 2 * lax.axis_index("y") + lax.axis_index("c"), (1,)).astype(jnp.int32)
    _, landed = _split_wait(_to_all_copies, late_weights, y_dn, "gather_out_proj_wait")
    wpo_g, wdn_g, wo_g = (_fill_own_block(g, s, me_idx, "own_block_%d" % i) for i, (g, s) in enumerate(zip(landed, late_shards)))
    wpo_f = cols_major(wpo_g)
    wdn_f = wdn_g.reshape(dn, d)
    wo_f = wo_g.reshape(d, d)
    p_out = _matmul(y_pool, wpo_f, NN, F32, 1056, 1024, 1024, "pool_out")
    q_out = _matmul(y_dn, wdn_f, NN, F32, 1056, 1024, 2048, "dn_out")
    merged = _merge_fwd(p_out, q_out, proj, o_gp, o_gd)
    mo = _matmul(merged, wo_f, NN, F32, 1056, 1024, 2048, "w_o_fwd")
    dh1, d_fw, loss_part = _final_loss(h0, mo, final_norm_w.reshape(1, d), loss_target[0], x0)

    d_merged = _matmul(dh1, wo_f, NT, F32, 1056, 1024, 2048, "w_o_bwd_x")
    g_wo = _matmul(merged, dh1, TN, BF16, 1024, 1024, lp, "w_o_bwd_w")
    d_p, d_q, d_gp, d_gd = _merge_bwd(p_out, q_out, proj, o_gp, o_gd, d_merged)
    d_ypool = _matmul(d_p, wpo_f, NT, F32, 1056, 1024, 2048, "pool_out_bwd_x")
    g_wpo = _matmul(y_pool.T, d_p, NN, BF16, 1024, 1024, lp, "pool_out_bwd_w", col_blocks=N_DEV)
    d_ydn = _matmul(d_q, wdn_f, NT, F32, 1056, 1024, 2048, "dn_out_bwd_x")
    g_wdn = _matmul(y_dn, d_q, TN, BF16, 1024, 1024, lp, "dn_out_bwd_w")
    rs_early = _rs_begin([g_wpo, g_wdn.reshape(N_DEV, dn // N_DEV, d), g_wo.reshape(N_DEV, d // N_DEV, d)], "early", split=True)
    started = rs_early["token"][0, 0]
    d_u, d_zp, g_mix, g_pscale = _pool_bwd(proj, mix_f, pool_scale + started, d_ypool, pad)
    d_proj = lax.empty((lp, n_all), BF16)
    d_qn, d_kn, d_vv, d_ba, d_proj, d_prm, g_dnw = _chunk_bwd(qn, kn, vv, proj, o_zd, n_main, prm + started, dn_norm_w, hist, tmats,
                                                              d_ydn, n_heads, pad, d_proj)
    d_proj, g_cq = _conv_bwd(proj, o_q, conv_q, d_qn, hd, float(hd) ** -0.5, pad, "conv_q_bwd", d_proj)
    d_proj, g_ck = _conv_bwd(proj, o_k, conv_k, d_kn, hd, 1.0, pad, "conv_k_bwd", d_proj)
    d_proj, g_cv = _conv_bwd(proj, o_v, conv_v, d_vv, hd, None, pad, "conv_v_bwd", d_proj)
    for off, piece in ((0, d_u), (pw, d_zp), (o_gp, d_gp), (o_gd, d_gd), (n_main, d_ba.astype(BF16)), (n_main + LANES, jnp.zeros((lp, LANES), BF16))):
        d_proj = lax.dynamic_update_slice(d_proj, piece, (0, off))
    xn_t, rs_late, token = xn.T, [], None
    for half in range(2):
        rows = slice(half * (d // 2), (half + 1) * (d // 2))
        g_wall = _matmul(xn_t[rows], d_proj, NN, BF16, 1024, 768, lp, "w_in_bwd_w_%d" % half, after=token)
        g_win = _lane_gather_cols(g_wall.reshape(1, d // 2, n_all), _lane_gather_table(to_own, own_sub), N_DEV, ns,
                                  "w_in_grad_to_own_%d" % half, own_sub)
        rs_late.append(_rs_begin([g_win], "late%d" % half, split=True))
        token = rs_late[-1]["token"]
    d_xn = _matmul(d_proj, w_all, NT, F32, lp, 512, 2432, "w_in_bwd_x", after=token)
    d_head, grad_x, g_nw = _norm_in_bwd(h0, norm_w, d_xn, dh1, x0)
    grad_x = grad_x[None]

    by_cols = lambda t: jnp.transpose(t.reshape(t.shape[0], N_DEV, t.shape[1] // N_DEV), (1, 0, 2))
    g_conv = by_cols(jnp.concatenate([g_cq, g_ck, g_cv], axis=1)).reshape(N_DEV, kw * cs // pg, pg)
    conv_rows = -(-g_conv.shape[1] // 16) * 16
    g_small = jnp.concatenate(
        [jnp.transpose(g_mix.reshape(ng, N_DEV, mr, pg), (1, 0, 2, 3)).reshape(N_DEV, ng * mr, pg), by_cols(d_head[pad:x0]),
         jnp.pad(g_conv, ((0, 0), (0, conv_rows - g_conv.shape[1]), (0, 0)))], axis=1).astype(BF16)
    r_small, = _rs_finish(_rs_begin([g_small], "small", split=False), "small")
    r_mix, r_meta = r_small[:ng * mr], r_small[ng * mr:ng * mr + n_meta]
    r_conv = r_small[ng * mr + n_meta:ng * mr + n_meta + kw * cs // pg]
    r_wpo, r_wdn, r_wo = _rs_finish(rs_early, "early", after=r_small)

    small = [g_nw[0], d_fw[0], g_pscale[0], g_dnw[0], d_prm[0], d_prm[1], loss_part[0]]
    s_sizes = [t.shape[0] for t in small]
    s_cols = -(-sum(s_sizes) // (8 * LANES)) * LANES
    s_vec = jnp.concatenate(small + [jnp.zeros((8 * s_cols - sum(s_sizes),), F32)]).reshape(8, s_cols)
    s_red = _all_reduce_small(s_vec)
    s_sum = s_red.reshape(-1)
    r_win = None
    for half, begun in enumerate(rs_late):
        r_win = _rs_finish(begun, "late%d" % half, after=s_red, part=half, n_parts=2, dsts=r_win)
    r_win, = r_win
    s_offs = [sum(s_sizes[:i]) for i in range(len(s_sizes))]
    s_take = lambda i, n=None, o=0: s_sum[s_offs[i] + o:s_offs[i] + o + (s_sizes[i] if n is None else n)]

    grads = {
        "meta_tokens": r_meta, "norm_w": s_take(0).reshape(norm_w.shape),
        "w_in": r_win.reshape(w_in.shape), "conv_w": r_conv.reshape(conv_w.shape),
        "A_log": s_take(4, n_heads, n_heads).reshape(A_log.shape), "dt_bias": s_take(5, n_heads, n_heads).reshape(dt_bias.shape),
        "pool_mix": r_mix.reshape(pool_mix.shape), "pool_scale": s_take(2).reshape(pool_scale.shape),
        "dn_norm_w": s_take(3).reshape(dn_norm_w.shape), "w_pool_out": r_wpo.reshape(w_pool_out.shape),
        "w_dn_out": r_wdn.reshape(w_dn_out.shape), "w_o": r_wo.reshape(w_o.shape),
        "final_norm_w": s_take(1).reshape(final_norm_w.shape),
    }
    loss = s_take(6, 1)[0]

    weights = dict(meta_tokens=meta_tokens, norm_w=norm_w, w_in=w_in, conv_w=conv_w, A_log=A_log, dt_bias=dt_bias,
                   pool_mix=pool_mix, pool_scale=pool_scale, dn_norm_w=dn_norm_w, w_pool_out=w_pool_out, w_dn_out=w_dn_out,
                   w_o=w_o, final_norm_w=final_norm_w)
    m_in = dict(meta_tokens=m_meta_tokens, norm_w=m_norm_w, w_in=m_w_in, conv_w=m_conv_w, A_log=m_A_log, dt_bias=m_dt_bias,
                pool_mix=m_pool_mix, pool_scale=m_pool_scale, dn_norm_w=m_dn_norm_w, w_pool_out=m_w_pool_out,
                w_dn_out=m_w_dn_out, w_o=m_w_o, final_norm_w=m_final_norm_w)
    v_in = dict(meta_tokens=v_meta_tokens, norm_w=v_norm_w, w_in=v_w_in, conv_w=v_conv_w, A_log=v_A_log, dt_bias=v_dt_bias,
                pool_mix=v_pool_mix, pool_scale=v_pool_scale, dn_norm_w=v_dn_norm_w, w_pool_out=v_w_pool_out,
                w_dn_out=v_w_dn_out, w_o=v_w_o, final_norm_w=v_final_norm_w)
    names = list(weights)
    upd = {n: _adamw(weights[n], grads[n], m_in[n], v_in[n], "adamw_" + n) for n in names}
    return (loss, grad_x, *[grads[n] for n in names], *[upd[n][0] for n in names], *[upd[n][1] for n in names],
            *[upd[n][2] for n in names])
```

```python
import functools
import math

import jax
import jax.numpy as jnp
import numpy as np
from jax import lax
from jax.experimental import pallas as pl
from jax.experimental.pallas import tpu as pltpu

F32 = jnp.float32
BF16 = jnp.bfloat16
HIGHEST = lax.Precision.HIGHEST
MESH = pl.DeviceIdType.MESH

CHUNK = 64
NORM_EPS = 1e-6
POOL_WINDOWS = (2, 4, 8, 16)
ADAM_LR, ADAM_B1, ADAM_B2, ADAM_EPS, ADAM_WD, ADAM_STEP = 0.001, 0.9, 0.999, 1e-08, 0.01, 10
N_DEV = 8
LANES = 128
VMEM_LIMIT = 48 * 1024 * 1024

NN = (((1,), (0,)), ((), ()))
NT = (((1,), (1,)), ((), ()))
TN = (((0,), (0,)), ((), ()))


def _call(body, **kw):
    return pl.pallas_call(body, **kw)


def _params(sem=None):
    return pltpu.CompilerParams(dimension_semantics=sem, vmem_limit_bytes=VMEM_LIMIT)


def _tile(n, pref, align):
    for d in range(min(pref, n), 0, -1):
        if n % d == 0 and d % align == 0:
            return d
    return n


def _dot(a, b, dims=NN, precision=None):
    return lax.dot_general(a, b, dims, precision=precision, preferred_element_type=F32)


def _sigmoid(x):
    return 0.5 * jnp.tanh(0.5 * x) + 0.5


def _silu(x):
    return x * _sigmoid(x)


def _softplus(x):
    return jnp.maximum(x, 0.0) + jnp.log(1.0 + jnp.exp(-jnp.abs(x)))


def _rmsnorm(x, w):
    return x * lax.rsqrt(jnp.mean(x * x, axis=-1, keepdims=True) + NORM_EPS) * w


def _shift_down(x, j, row):
    if j == 0:
        return x
    return jnp.where(row >= j, pltpu.roll(x, j, 0), 0.0)


def _shift_up(x, j, row):
    if j == 0:
        return x
    n = x.shape[0]
    return jnp.where(row < n - j, pltpu.roll(x, n - j, 0), 0.0)


def _matmul(a, b, dims, out_dtype, tm, tn, tk, name, col_blocks=None, after=None):
    ta = dims == TN
    tb = dims == NT
    m, kdim = (a.shape[1], a.shape[0]) if ta else a.shape
    n = b.shape[0] if tb else b.shape[1]
    if col_blocks:
        tn = n // col_blocks
    tm, tn, tk = _tile(m, tm, 8), _tile(n, tn, LANES), _tile(kdim, tk, LANES if not ta else 16)
    nk = kdim // tk

    n_extra = 0 if after is None else 1

    def body(a_ref, b_ref, *refs):
        o_ref, scratch = refs[n_extra], refs[n_extra + 1:]
        part = _dot(a_ref[...].astype(BF16), b_ref[...].astype(BF16), dims)
        if nk == 1:
            o_ref[...] = part.astype(o_ref.dtype).reshape(o_ref.shape)
            return
        acc_ref, = scratch
        k = pl.program_id(2)

        @pl.when(k == 0)
        def _():
            acc_ref[...] = part

        @pl.when(k > 0)
        def _():
            acc_ref[...] += part

        @pl.when(k == nk - 1)
        def _():
            o_ref[...] = acc_ref[...].astype(o_ref.dtype).reshape(o_ref.shape)

    a_spec = pl.BlockSpec((tk, tm), lambda i, j, k: (k, i)) if ta else pl.BlockSpec((tm, tk), lambda i, j, k: (i, k))
    b_spec = pl.BlockSpec((tn, tk), lambda i, j, k: (j, k)) if tb else pl.BlockSpec((tk, tn), lambda i, j, k: (k, j))
    if col_blocks:
        out_spec = pl.BlockSpec((1, tm, tn), lambda i, j, k: (j, i, 0))
        out_shape = jax.ShapeDtypeStruct((col_blocks, m, tn), out_dtype)
    else:
        out_spec = pl.BlockSpec((tm, tn), lambda i, j, k: (i, j))
        out_shape = jax.ShapeDtypeStruct((m, n), out_dtype)
    return _call(
        body, name=name, grid=(m // tm, n // tn, nk),
        in_specs=[a_spec, b_spec] + [ANY] * n_extra, out_specs=out_spec, out_shape=out_shape,
        scratch_shapes=[] if nk == 1 else [pltpu.VMEM((tm, tn), F32)],
        compiler_params=_params(("parallel", "parallel", "arbitrary")),
    )(a, b, *([] if after is None else [after]))


def _norm_in_fwd(x2d, meta, w, pad):
    seq, d = x2d.shape
    x0 = pad + meta.shape[0]
    assert x0 % 16 == 0
    lp = x0 + seq
    tr = _tile(seq, 512, 16)
    vec = pl.BlockSpec((1, d), lambda i: (0, 0))
    shapes = [jax.ShapeDtypeStruct((lp, d), F32), jax.ShapeDtypeStruct((lp, d), BF16)]

    def body(x_ref, w_ref, h_ref, o_ref):
        h_ref[...] = x_ref[...]
        o_ref[...] = _rmsnorm(x_ref[...], w_ref[...]).astype(BF16)

    def head(m_ref, w_ref, h_in_ref, o_in_ref, h_ref, o_ref):
        h = jnp.concatenate([jnp.zeros((pad, d), F32), m_ref[...]], axis=0) if pad else m_ref[...]
        h_ref[...] = h
        o_ref[...] = _rmsnorm(h, w_ref[...]).astype(BF16)

    rows = _rows_after(x0, tr, d)
    h0, xn = _call(
        body, name="norm_in_fwd", grid=(seq // tr,), in_specs=[pl.BlockSpec((tr, d), lambda i: (i, 0)), vec],
        out_specs=[rows, rows], out_shape=shapes, compiler_params=_params(("parallel",)),
    )(x2d, w)
    first = pl.BlockSpec((x0, d), lambda i: (0, 0))
    return _call(
        head, name="norm_in_fwd_head", grid=(1,), in_specs=[pl.BlockSpec(meta.shape, lambda i: (0, 0)), vec, ANY, ANY],
        out_specs=[first, first], out_shape=shapes, input_output_aliases={2: 0, 3: 1}, compiler_params=_params(("arbitrary",)),
    )(meta, w, h0, xn)


def _rows_after(x0, tr, d):
    step = math.gcd(x0, tr)
    return pl.BlockSpec((pl.Element(tr), pl.Element(d)), lambda i: (pl.multiple_of(x0 + tr * i, step), 0))


def _norm_in_bwd(h0, w, dxn, dh1, x0):
    lp, d = h0.shape
    tr = _tile(lp - x0, 512, 8)
    vec = pl.BlockSpec((1, d), lambda i: (0, 0))

    def make(body_rows, first):
        def body(h_ref, w_ref, da_ref, dh1_ref, dh_ref, dw_ref):
            _, vjp = jax.vjp(_rmsnorm, h_ref[...], w_ref[...])
            dh, dw = vjp(da_ref[...])
            dh_ref[...] = dh + dh1_ref[...]

            @pl.when(pl.program_id(0) == 0)
            def _():
                dw_ref[...] = jnp.zeros_like(dw_ref)

            dw_ref[...] += dw

        rows_in = pl.BlockSpec((x0, d), lambda i: (0, 0)) if first else _rows_after(x0, tr, d)
        return _call(
            body, name="norm_in_bwd_head" if first else "norm_in_bwd", grid=(1 if first else (lp - x0) // tr,),
            in_specs=[rows_in, vec, rows_in, rows_in],
            out_specs=[pl.BlockSpec((body_rows, d), lambda i: (i, 0)), vec],
            out_shape=[jax.ShapeDtypeStruct((x0 if first else lp - x0, d), F32), jax.ShapeDtypeStruct((1, d), F32)],
            compiler_params=_params(("arbitrary",)),
        )(h0, w, dxn, dh1)

    d_head, dw_head = make(x0, True)
    grad_x, dw_rest = make(tr, False)
    return d_head, grad_x, dw_head + dw_rest


def _final_loss(h0, mo, fw, tgt, x0):
    lp, d = h0.shape
    tr = _tile(lp - x0, 512, 8)

    def body(h_ref, mo_ref, fw_ref, t_ref, dh_ref, dw_ref, loss_ref):
        tgt_v = t_ref[...]

        def loss_fn(h1, w):
            err = _rmsnorm(h1, w) - tgt_v
            return 0.5 * jnp.sum(jnp.mean(err * err, axis=-1, keepdims=True), axis=0, keepdims=True)

        loss, vjp = jax.vjp(loss_fn, h_ref[...] + mo_ref[...], fw_ref[...])
        dh, dw = vjp(jnp.ones((1, 1), F32))
        dh_ref[...] = dh

        @pl.when(pl.program_id(0) == 0)
        def _():
            dw_ref[...] = jnp.zeros_like(dw_ref)
            loss_ref[...] = jnp.zeros_like(loss_ref)

        dw_ref[...] += dw
        loss_ref[...] += jnp.broadcast_to(loss, loss_ref.shape)

    def zero_head(dh_in_ref, dh_ref):
        dh_ref[...] = jnp.zeros_like(dh_ref)

    rows = _rows_after(x0, tr, d)
    vec = pl.BlockSpec((1, d), lambda i: (0, 0))
    dh1, dw, loss = _call(
        body, name="final_loss", grid=((lp - x0) // tr,),
        in_specs=[rows, rows, vec, pl.BlockSpec((tr, d), lambda i: (i, 0))],
        out_specs=[rows, vec, pl.BlockSpec((8, LANES), lambda i: (0, 0))],
        out_shape=[jax.ShapeDtypeStruct((lp, d), F32), jax.ShapeDtypeStruct((1, d), F32), jax.ShapeDtypeStruct((8, LANES), F32)],
        compiler_params=_params(("arbitrary",)),
    )(h0, mo, fw, tgt)
    dh1 = _call(
        zero_head, name="final_loss_head", grid=(1,), in_specs=[ANY], out_specs=pl.BlockSpec((x0, d), lambda i: (0, 0)),
        out_shape=jax.ShapeDtypeStruct((lp, d), F32), input_output_aliases={0: 0}, compiler_params=_params(("arbitrary",)),
    )(dh1)
    return dh1, dw, loss


def _pool_select(parts, g):
    out = parts[-1]
    for gi in range(len(parts) - 2, -1, -1):
        out = jnp.where(g == gi, parts[gi], out)
    return out


def _pool_count(row, g, pad):
    win = _pool_select([jnp.full(row.shape, float(w), F32) for w in POOL_WINDOWS], g)
    return jnp.maximum(jnp.minimum((row - pad + 1).astype(F32), win), 1.0)


def _pooled(u, g, row, pad):
    sums, s, span = [], u, 1
    for w in POOL_WINDOWS:
        while span < w:
            s = s + _shift_down(s, span, row)
            span *= 2
        sums.append(s)
    return _pool_select(sums, g) / _pool_count(row, g, pad) - u


def _pooled_adjoint(dp, g, row, pad):
    e = dp / _pool_count(row, g, pad)
    sums, s, span = [], e, 1
    for w in POOL_WINDOWS:
        while span < w:
            s = s + _shift_up(s, span, row)
            span *= 2
        sums.append(s)
    return _pool_select(sums, g) - dp


def _pool_specs(lp, pg, ng, z_off):
    u_spec = pl.BlockSpec((lp, pg), lambda g: (0, g))
    z_spec = pl.BlockSpec((lp, pg), lambda g: (0, z_off + g))
    mix_spec = pl.BlockSpec((1, pg, pg), lambda g: (g, 0, 0))
    vec_spec = pl.BlockSpec((1, pg), lambda g: (0, g))
    return u_spec, z_spec, mix_spec, vec_spec


def _pool_fwd(proj, mix, scale, pad):
    lp = proj.shape[0]
    ng, pg, _ = mix.shape
    pw = ng * pg

    def body(u_ref, z_ref, mix_ref, sc_ref, y_ref):
        g = pl.program_id(0)
        row = lax.broadcasted_iota(jnp.int32, (lp, 1), 0)
        pooled = _pooled(u_ref[...], g, row, pad)
        mixed = _dot(pooled.astype(BF16), mix_ref[0])
        y_ref[...] = (mixed * sc_ref[...] * _silu(z_ref[...])).astype(BF16)

    u_spec, z_spec, mix_spec, vec_spec = _pool_specs(lp, pg, ng, pw // pg)
    return _call(
        body, name="pool_fwd", grid=(ng,), in_specs=[u_spec, z_spec, mix_spec, vec_spec], out_specs=u_spec,
        out_shape=jax.ShapeDtypeStruct((lp, pw), BF16), compiler_params=_params(("parallel",)),
    )(proj, proj, mix, scale)


def _pool_bwd(proj, mix, scale, dy, pad):
    lp = proj.shape[0]
    ng, pg, _ = mix.shape
    pw = ng * pg

    def body(u_ref, z_ref, mix_ref, sc_ref, dy_ref, du_ref, dz_ref, dmix_ref, dsc_ref):
        g = pl.program_id(0)
        row = lax.broadcasted_iota(jnp.int32, (lp, 1), 0)
        real = row >= pad
        z = z_ref[...]
        pooled = _pooled(u_ref[...], g, row, pad).astype(BF16)
        mixed = _dot(pooled, mix_ref[0])
        sig = _sigmoid(z)
        sz = z * sig
        dyv = dy_ref[...]
        dsc_ref[...] = jnp.sum(dyv * mixed * sz, axis=0, keepdims=True)
        d_sz = dyv * mixed * sc_ref[...]
        dz_ref[...] = jnp.where(real, d_sz * (sig + sz * (1.0 - sig)), 0.0).astype(BF16)
        d_mixed = (dyv * sc_ref[...] * sz).astype(BF16)
        dmix_ref[0] = _dot(pooled, d_mixed, TN)
        d_pooled = _dot(d_mixed, mix_ref[0], NT)
        du_ref[...] = jnp.where(real, _pooled_adjoint(d_pooled, g, row, pad), 0.0).astype(BF16)

    u_spec, z_spec, mix_spec, vec_spec = _pool_specs(lp, pg, ng, pw // pg)
    return _call(
        body, name="pool_bwd", grid=(ng,),
        in_specs=[u_spec, z_spec, mix_spec, vec_spec, u_spec], out_specs=[u_spec, u_spec, mix_spec, vec_spec],
        out_shape=[jax.ShapeDtypeStruct((lp, pw), BF16), jax.ShapeDtypeStruct((lp, pw), BF16),
                   jax.ShapeDtypeStruct((ng, pg, pg), F32), jax.ShapeDtypeStruct((1, pw), F32)],
        compiler_params=_params(("parallel",)),
    )(proj, proj, mix, scale, dy)


def _conv_pre(x, w, row):
    kw = w.shape[0]
    y = w[kw - 1:kw, :] * x
    for kk in range(kw - 1):
        y = y + w[kk:kk + 1, :] * _shift_down(x, kw - 1 - kk, row)
    return y


def _conv_post(y, out_scale):
    s = _silu(y)
    if out_scale is None:
        return s
    return s * lax.rsqrt(jnp.sum(s * s, axis=-1, keepdims=True) + NORM_EPS) * out_scale


def _conv_fwd(proj, col_off, w, hd, out_scale, name):
    lp = proj.shape[0]
    kw, width = w.shape
    blk0 = col_off // hd

    def body(x_ref, w_ref, o_ref):
        row = lax.broadcasted_iota(jnp.int32, (lp, 1), 0)
        o_ref[...] = _conv_post(_conv_pre(x_ref[...], w_ref[...], row), out_scale)

    return _call(
        body, name=name, grid=(width // hd,),
        in_specs=[pl.BlockSpec((lp, hd), lambda j: (0, blk0 + j)), pl.BlockSpec((kw, hd), lambda j: (0, j))],
        out_specs=pl.BlockSpec((lp, hd), lambda j: (0, j)),
        out_shape=jax.ShapeDtypeStruct((lp, width), F32), compiler_params=_params(("parallel",)),
    )(proj, w)


def _conv_bwd(proj, col_off, w, d_out, hd, out_scale, pad, name, dst):
    lp = proj.shape[0]
    kw, width = w.shape
    blk0 = col_off // hd

    def body(x_ref, w_ref, do_ref, dst_ref, dx_ref, dw_ref):
        row = lax.broadcasted_iota(jnp.int32, (lp, 1), 0)
        real = row >= pad
        x, wv = x_ref[...], w_ref[...]
        _, vjp = jax.vjp(functools.partial(_conv_post, out_scale=out_scale), _conv_pre(x, wv, row))
        dy = jnp.where(real, vjp(do_ref[...])[0], 0.0)
        dx = wv[kw - 1:kw, :] * dy
        dw_ref[kw - 1:kw, :] = jnp.sum(dy * x, axis=0, keepdims=True)
        for kk in range(kw - 1):
            ahead = _shift_up(dy, kw - 1 - kk, row)
            dx = dx + wv[kk:kk + 1, :] * ahead
            dw_ref[kk:kk + 1, :] = jnp.sum(ahead * x, axis=0, keepdims=True)
        dx_ref[...] = jnp.where(real, dx, 0.0).astype(BF16)

    col = pl.BlockSpec((lp, hd), lambda j: (0, j))
    at_off = pl.BlockSpec((lp, hd), lambda j: (0, blk0 + j))
    wspec = pl.BlockSpec((kw, hd), lambda j: (0, j))
    return _call(
        body, name=name, grid=(width // hd,),
        in_specs=[at_off, wspec, col, ANY], out_specs=[at_off, wspec],
        out_shape=[jax.ShapeDtypeStruct(dst.shape, BF16), jax.ShapeDtypeStruct((kw, width), F32)],
        input_output_aliases={3: 0}, compiler_params=_params(("parallel",)),
    )(proj, w, d_out, dst)


HEADS_PER_STEP = 16
HEADS_PER_STEP_BWD = 16


def _matmul_with_direct_vjp(dims, da_dims, db_dims, db_swapped):
    @jax.custom_vjp
    def mm(a, b):
        return _dot(a, b, dims)

    def fwd(a, b):
        return _dot(a, b, dims), (a, b)

    def bwd(res, g):
        a, b = res
        return _dot(g, b, da_dims) if not db_swapped[0] else _dot(b, g, da_dims), _dot(a, g, db_dims) if not db_swapped[1] else _dot(g, a, db_dims)

    mm.defvjp(fwd, bwd)
    return mm


_mm_nn = _matmul_with_direct_vjp(NN, NT, TN, (False, False))
_mm_nt = _matmul_with_direct_vjp(NT, NN, TN, (False, True))
_mm_tn = _matmul_with_direct_vjp(TN, NT, NN, (True, False))


def _each(fn, *lists):
    return [fn(*args) for args in zip(*lists)]


def _dot3_each(a_list, b_list, dims=NN):
    hi = lambda t: t.astype(BF16)
    lo = lambda t, t_hi: (t - t_hi.astype(F32)).astype(BF16)
    dot = lambda x, y: _dot(x, y, dims)
    a_hi, b_hi = _each(hi, a_list), _each(hi, b_list)
    a_lo, b_lo = _each(lo, a_list, a_hi), _each(lo, b_list, b_hi)
    hh, hl, lh = _each(dot, a_hi, b_hi), _each(dot, a_hi, b_lo), _each(dot, a_lo, b_hi)
    return _each(lambda x, y, w: x + (y + w), hh, hl, lh)


@jax.custom_vjp
def _unit_lower_inverse(lmats):
    c = lmats[0].shape[0]
    eye = lax.broadcasted_iota(jnp.int32, (c, c), 0) == lax.broadcasted_iota(jnp.int32, (c, c), 1)
    a = [-m for m in lmats]
    tmat = [jnp.where(eye, 1.0, 0.0).astype(F32) + m for m in a]
    span = 2
    while span < c:
        a = _dot3_each(a, a)
        tmat = _each(lambda t, u: t + u, tmat, _dot3_each(tmat, a))
        span *= 2
    return tuple(tmat)


def _unit_lower_inverse_fwd(lmats):
    tmats = _unit_lower_inverse(lmats)
    return tmats, tmats


def _unit_lower_inverse_bwd(tmats, cts):
    left = _each(lambda t, ct: _dot(t, ct, TN, HIGHEST), tmats, cts)
    return (tuple(_each(lambda m, t: -_dot(m, t, NT, HIGHEST), left, tmats)),)


_unit_lower_inverse.defvjp(_unit_lower_inverse_fwd, _unit_lower_inverse_bwd)


@jax.custom_vjp
def _known_inverse(lmats, tmats):
    return tmats


def _known_inverse_fwd(lmats, tmats):
    return tmats, tmats


def _known_inverse_bwd(tmats, cts):
    return _unit_lower_inverse_bwd(tmats, cts)[0], tuple(jnp.zeros_like(t) for t in tmats)


_known_inverse.defvjp(_known_inverse_fwd, _known_inverse_bwd)


def _chunk_math(states, q, k, v, ba, z, prm, nw, head0, rowmask, n_heads, tmats=None, keep_tmats=False):
    c = q.shape[0]
    heads = list(range(len(states)))
    hd = q.shape[1] // len(states)
    lane = lax.broadcasted_iota(jnp.int32, ba.shape, 1)
    sub = lax.broadcasted_iota(jnp.int32, (ba.shape[1], c), 0)
    ri = lax.broadcasted_iota(jnp.int32, (c, c), 0)
    ci = lax.broadcasted_iota(jnp.int32, (c, c), 1)
    last = lax.broadcasted_iota(jnp.int32, (c, 1), 0) == c - 1
    causal, strict = ri >= ci, ri > ci
    beta_all = _sigmoid(ba) * rowmask
    g_all = -jnp.exp(prm[0:1, :]) * _softplus(ba + prm[1:2, :]) * rowmask
    gcum_all = _dot(jnp.where(causal, 1.0, 0.0).astype(F32), g_all, precision=HIGHEST)
    gcum_t = gcum_all.T
    split = lambda t: [t[:, j * hd:(j + 1) * hd] for j in heads]
    qs, ks, vs, zs = split(q), split(k), split(v), split(z)
    beta = [jnp.sum(jnp.where(lane == head0 + j, beta_all, 0.0), axis=1, keepdims=True) for j in heads]
    gcum = [jnp.sum(jnp.where(lane == n_heads + head0 + j, gcum_all, 0.0), axis=1, keepdims=True) for j in heads]
    grow = [jnp.sum(jnp.where(sub == n_heads + head0 + j, gcum_t, 0.0), axis=0, keepdims=True) for j in heads]
    glast = _each(lambda gc: jnp.sum(jnp.where(last, gc, 0.0), axis=0, keepdims=True), gcum)
    decay = _each(lambda gc, gr: jnp.where(causal, jnp.exp(jnp.where(causal, gc - gr, 0.0)), 0.0), gcum, grow)
    eg = _each(jnp.exp, gcum)
    k_beta = _each(jnp.multiply, ks, beta)
    kk = _each(_mm_nt, k_beta, ks)
    lmats = tuple(_each(lambda m, dc: jnp.where(strict, m * dc, 0.0), kk, decay))
    tmat = list(_unit_lower_inverse(lmats) if tmats is None else _known_inverse(lmats, tuple(tmats)))
    uw = _each(lambda t, vj, b, kb, e: _mm_nn(t, jnp.concatenate([vj * b, kb * e], axis=1)), tmat, vs, beta, k_beta, eg)
    u_c, w_c = _each(lambda m: m[:, :hd], uw), _each(lambda m: m[:, hd:], uw)
    qk = _each(lambda a, b, dc: jnp.where(causal, _mm_nt(a, b) * dc, 0.0), qs, ks, decay)
    from_state = _each(lambda w, a, e, s: _mm_nn(jnp.concatenate([w, a * e], axis=0), s), w_c, qs, eg, list(states))
    v_new = _each(lambda u, m: u - m[:c], u_c, from_state)
    o = _each(lambda m, a, vn: m[c:] + _mm_nn(a, vn), from_state, qk, v_new)
    k_dec = _each(lambda a, gl, gc: a * jnp.exp(gl - gc), ks, glast, gcum)
    new_states = _each(lambda s, gl, kd, vn: s * jnp.exp(gl) + _mm_tn(kd, vn), list(states), glast, k_dec, v_new)
    ys = _each(lambda oj, zj: _rmsnorm(oj, nw) * _silu(zj), o, zs)
    if keep_tmats:
        return jnp.concatenate(ys, axis=1), tuple(new_states), tuple(tmat)
    return jnp.concatenate(ys, axis=1), tuple(new_states)


def _chunk_specs(nc, hd, n_heads, z_off, ba_off, rev):
    cidx = (lambda c: nc - 1 - c) if rev else (lambda c: c)
    hb = min(HEADS_PER_STEP_BWD if rev else HEADS_PER_STEP, n_heads)
    assert n_heads % hb == 0 and z_off % (hb * hd) == 0 and ba_off % LANES == 0
    blk = lambda off: pl.BlockSpec((CHUNK, hb * hd), lambda c, g: (cidx(c), off + g))
    ba_spec = lambda off: pl.BlockSpec((CHUNK, LANES), lambda c, g: (cidx(c), off // LANES))
    prm_spec = pl.BlockSpec((8, LANES), lambda c, g: (0, 0))
    nw_spec = pl.BlockSpec((1, hd), lambda c, g: (0, 0))
    st_spec = pl.BlockSpec((1, hb, hd, hd), lambda c, g: (cidx(c), g, 0, 0))
    return blk, ba_spec, prm_spec, nw_spec, st_spec, blk(z_off // (hb * hd))


def _rowmask(chunk_idx, pad):
    row = chunk_idx * CHUNK + lax.broadcasted_iota(jnp.int32, (CHUNK, 1), 0)
    return jnp.where(row >= pad, 1.0, 0.0).astype(F32)


def _chunk_fwd(qn, kn, vv, proj, z_off, ba_off, prm, nw, n_heads, pad):
    lp, dn = qn.shape
    hd = dn // n_heads
    nc = lp // CHUNK
    hb = min(HEADS_PER_STEP, n_heads)

    def body(q_ref, k_ref, v_ref, ba_ref, z_ref, prm_ref, nw_ref, y_ref, hist_ref, tm_ref, st_ref):
        c, g = pl.program_id(0), pl.program_id(1)

        @pl.when(c == 0)
        def _():
            for j in range(hb):
                st_ref[g * hb + j] = jnp.zeros((hd, hd), F32)

        states = tuple(st_ref[g * hb + j] for j in range(hb))
        for j in range(hb):
            hist_ref[0, j] = states[j]
        y, new_states, tmats = _chunk_math(states, q_ref[...], k_ref[...], v_ref[...], ba_ref[...], z_ref[...], prm_ref[...],
                                           nw_ref[...], g * hb, _rowmask(c, pad), n_heads, keep_tmats=True)
        y_ref[...] = y.astype(BF16)
        for j in range(hb):
            st_ref[g * hb + j] = new_states[j]
            tm_ref[0, j] = tmats[j]

    blk, ba_spec, prm_spec, nw_spec, st_spec, z_spec = _chunk_specs(nc, hd, n_heads, z_off, ba_off, False)
    tm_spec = pl.BlockSpec((1, hb, CHUNK, CHUNK), lambda c, g: (c, g, 0, 0))
    return _call(
        body, name="chunk_fwd", grid=(nc, n_heads // hb),
        in_specs=[blk(0), blk(0), blk(0), ba_spec(ba_off), z_spec, prm_spec, nw_spec], out_specs=[blk(0), st_spec, tm_spec],
        out_shape=[jax.ShapeDtypeStruct((lp, dn), BF16), jax.ShapeDtypeStruct((nc, n_heads, hd, hd), F32),
                   jax.ShapeDtypeStruct((nc, n_heads, CHUNK, CHUNK), F32)],
        scratch_shapes=[pltpu.VMEM((n_heads, hd, hd), F32)],
        compiler_params=_params(("arbitrary", "arbitrary")),
    )(qn, kn, vv, proj, proj, prm, nw)


def _chunk_bwd(qn, kn, vv, proj, z_off, ba_off, prm, nw, hist, tmats, dy, n_heads, pad, d_proj):
    lp, dn = qn.shape
    hd = dn // n_heads
    nc = lp // CHUNK
    hb = min(HEADS_PER_STEP_BWD, n_heads)

    def body(q_ref, k_ref, v_ref, ba_ref, z_ref, prm_ref, nw_ref, hist_ref, tm_ref, dy_ref, d_proj_ref,
             dq_ref, dk_ref, dv_ref, dba_ref, dz_ref, dprm_ref, dnw_ref, dst_ref):
        step, g = pl.program_id(0), pl.program_id(1)

        @pl.when(step == 0)
        def _():
            for j in range(hb):
                dst_ref[g * hb + j] = jnp.zeros((hd, hd), F32)

        @pl.when((step == 0) & (g == 0))
        def _():
            dprm_ref[...] = jnp.zeros_like(dprm_ref)
            dnw_ref[...] = jnp.zeros_like(dnw_ref)

        @pl.when(g == 0)
        def _():
            dba_ref[...] = jnp.zeros_like(dba_ref)

        def fn(states, q, k, v, ba, z, prm_v, nw_v, known):
            return _chunk_math(states, q, k, v, ba, z, prm_v, nw_v, g * hb, _rowmask(nc - 1 - step, pad), n_heads, tmats=known)

        states = tuple(hist_ref[0, j] for j in range(hb))
        known = tuple(tm_ref[0, j] for j in range(hb))
        _, vjp = jax.vjp(fn, states, q_ref[...], k_ref[...], v_ref[...], ba_ref[...], z_ref[...], prm_ref[...], nw_ref[...], known)
        dst, dq, dk, dv, dba, dz, dprm, dnw, _ = vjp((dy_ref[...], tuple(dst_ref[g * hb + j] for j in range(hb))))
        for j in range(hb):
            dst_ref[g * hb + j] = dst[j]
        dq_ref[...] = dq
        dk_ref[...] = dk
        dv_ref[...] = dv
        dz_ref[...] = dz.astype(BF16)
        dba_ref[...] += dba
        dprm_ref[...] += dprm
        dnw_ref[...] += dnw

    blk, ba_spec, prm_spec, nw_spec, st_spec, z_spec = _chunk_specs(nc, hd, n_heads, z_off, ba_off, True)
    f32_full = jax.ShapeDtypeStruct((lp, dn), F32)
    tm_spec = pl.BlockSpec((1, hb, CHUNK, CHUNK), lambda c, g: (nc - 1 - c, g, 0, 0))
    return _call(
        body, name="chunk_bwd", grid=(nc, n_heads // hb),
        in_specs=[blk(0), blk(0), blk(0), ba_spec(ba_off), z_spec, prm_spec, nw_spec, st_spec, tm_spec, blk(0), ANY],
        out_specs=[blk(0), blk(0), blk(0), ba_spec(0), z_spec, prm_spec, nw_spec],
        out_shape=[f32_full, f32_full, f32_full, jax.ShapeDtypeStruct((lp, LANES), F32), jax.ShapeDtypeStruct(d_proj.shape, BF16),
                   jax.ShapeDtypeStruct((8, LANES), F32), jax.ShapeDtypeStruct((1, hd), F32)],
        scratch_shapes=[pltpu.VMEM((n_heads, hd, hd), F32)],
        input_output_aliases={10: 4}, compiler_params=_params(("arbitrary", "arbitrary")),
    )(qn, kn, vv, proj, proj, prm, nw, hist, tmats, dy, d_proj)


def _merge_math(p, q, gp, gd):
    return _sigmoid(gp) * p + _sigmoid(gd) * q


def _merge_specs(lp, d, gp_off, gd_off):
    tr, tc = _tile(lp, 264, 16), _tile(d, 1024, LANES)
    blk = pl.BlockSpec((tr, tc), lambda i, j: (i, j))
    gp_spec = pl.BlockSpec((tr, tc), lambda i, j: (i, gp_off // tc + j))
    gd_spec = pl.BlockSpec((tr, tc), lambda i, j: (i, gd_off // tc + j))
    return (lp // tr, d // tc), blk, gp_spec, gd_spec


def _merge_fwd(p, q, proj, gp_off, gd_off):
    lp, d = p.shape
    grid, blk, gp_spec, gd_spec = _merge_specs(lp, d, gp_off, gd_off)

    def body(p_ref, q_ref, gp_ref, gd_ref, o_ref):
        o_ref[...] = _merge_math(p_ref[...], q_ref[...], gp_ref[...], gd_ref[...]).astype(BF16)

    return _call(
        body, name="merge_fwd", grid=grid, in_specs=[blk, blk, gp_spec, gd_spec], out_specs=blk,
        out_shape=jax.ShapeDtypeStruct((lp, d), BF16), compiler_params=_params(("parallel", "parallel")),
    )(p, q, proj, proj)


def _merge_bwd(p, q, proj, gp_off, gd_off, dm):
    lp, d = p.shape
    grid, blk, gp_spec, gd_spec = _merge_specs(lp, d, gp_off, gd_off)

    def body(p_ref, q_ref, gp_ref, gd_ref, dm_ref, dp_ref, dq_ref, dgp_ref, dgd_ref):
        _, vjp = jax.vjp(_merge_math, p_ref[...], q_ref[...], gp_ref[...], gd_ref[...])
        for ref, val in zip((dp_ref, dq_ref, dgp_ref, dgd_ref), vjp(dm_ref[...])):
            ref[...] = val.astype(BF16)

    out = jax.ShapeDtypeStruct((lp, d), BF16)
    return _call(
        body, name="merge_bwd", grid=grid, in_specs=[blk, blk, gp_spec, gd_spec, blk], out_specs=[blk] * 4,
        out_shape=[out] * 4, compiler_params=_params(("parallel", "parallel")),
    )(p, q, proj, proj, dm)


def _adamw(w, g, m, v, name):
    shape = w.shape
    w2, g2, m2, v2 = (t.reshape((-1, shape[-1])) for t in (w, g, m, v))
    rows, cols = w2.shape
    tr = _tile(rows, 128, 8)

    def body(w_ref, g_ref, m_ref, v_ref, d_ref, nm_ref, nv_ref):
        gv = g_ref[...]
        nm = ADAM_B1 * m_ref[...] + (1.0 - ADAM_B1) * gv
        nv = ADAM_B2 * v_ref[...] + (1.0 - ADAM_B2) * (gv * gv)
        m_hat = nm / (1.0 - ADAM_B1 ** ADAM_STEP)
        v_hat = nv / (1.0 - ADAM_B2 ** ADAM_STEP)
        d_ref[...] = -ADAM_LR * (m_hat / (jnp.sqrt(v_hat) + ADAM_EPS) + ADAM_WD * w_ref[...])
        nm_ref[...] = nm
        nv_ref[...] = nv

    blk = pl.BlockSpec((tr, cols), lambda i: (i, 0))
    out = jax.ShapeDtypeStruct((rows, cols), F32)
    res = _call(
        body, name=name, grid=(rows // tr,), in_specs=[blk] * 4, out_specs=[blk] * 3, out_shape=[out] * 3,
        compiler_params=_params(("parallel",)),
    )(w2, g2, m2, v2)
    return tuple(t.reshape(shape) for t in res)


def _coords():
    return lax.axis_index("x"), lax.axis_index("y"), lax.axis_index("c")


def _flip(v, bit):
    return 1 - v if bit else v


CHIP_FLIPS = ((1, 0), (0, 1), (1, 1))
ANY = pl.BlockSpec(memory_space=pl.ANY)


def _all_gather(shards):
    n = len(shards)

    def body(*refs):
        x_refs, out_refs = refs[:n], refs[n:2 * n]
        send_sems, recv_sems, local_sems = refs[2 * n:]
        x, y, c = _coords()
        sibling = (x, y, 1 - c)
        chips = [(_flip(x, fx), _flip(y, fy)) for fx, fy in CHIP_FLIPS]

        def copy(a, k, block, to, from_input=False):
            px, py, pc = block
            slot = out_refs[a].at[4 * px + 2 * py + pc]
            return pltpu.make_async_remote_copy(
                src_ref=x_refs[a] if from_input else slot, dst_ref=slot,
                send_sem=send_sems.at[7 * a + k], recv_sem=recv_sems.at[7 * a + k], device_id=to, device_id_type=MESH)

        mine = [pltpu.make_async_copy(x_refs[a], out_refs[a].at[4 * x + 2 * y + c], local_sems.at[a]) for a in range(n)]
        first = []
        for a in range(n):
            mine[a].start()
            first.append(copy(a, 0, (x, y, c), sibling, True))
            first += [copy(a, 1 + j, (x, y, c), (*chip, c), True) for j, chip in enumerate(chips)]
        for cp in first:
            cp.start()
        passed = []
        for j, chip in enumerate(chips):
            for a in range(n):
                copy(a, 1 + j, (*chip, c), (x, y, c)).wait_recv()
                passed.append(copy(a, 4 + j, (*chip, c), sibling))
                passed[-1].start()
        for a in range(n):
            copy(a, 0, (x, y, 1 - c), (x, y, c)).wait_recv()
            for j, chip in enumerate(chips):
                copy(a, 4 + j, (*chip, 1 - c), (x, y, c)).wait_recv()
        for cp in first + passed:
            cp.wait_send()
        for cp in mine:
            cp.wait()

    return _call(
        body, name="all_gather", in_specs=[ANY] * n, out_specs=[ANY] * n,
        out_shape=[jax.ShapeDtypeStruct((N_DEV,) + s.shape, s.dtype) for s in shards],
        scratch_shapes=[pltpu.SemaphoreType.DMA((7 * n,)), pltpu.SemaphoreType.DMA((7 * n,)), pltpu.SemaphoreType.DMA((n,))],
    )(*shards)


def _all_gather_tree(shard, after):
    rows, cols = shard.shape
    half = rows // 2
    assert rows % 32 == 0

    def body(x_ref, after_ref, out_ref, send_sems, recv_sems, local_sem):
        x, y, c = _coords()
        me, sibling = (x, y, c), (x, y, 1 - c)
        x_nbr, y_nbr, diag = (1 - x, y), (x, 1 - y), (1 - x, 1 - y)

        def part(ref, h):
            return ref if h is None else ref.at[pl.ds(h * half, half)]

        def copy(k, block, to, h=None, from_input=False):
            px, py, pc = block
            slot = part(out_ref.at[4 * px + 2 * py + pc], h)
            return pltpu.make_async_remote_copy(
                src_ref=part(x_ref, h) if from_input else slot, dst_ref=slot,
                send_sem=send_sems.at[k], recv_sem=recv_sems.at[k], device_id=to, device_id_type=MESH)

        mine = pltpu.make_async_copy(x_ref, out_ref.at[4 * x + 2 * y + c], local_sem)
        mine.start()
        started = [copy(0, me, sibling, None, True),
                   copy(1, me, (*x_nbr, c), 0, True), copy(2, me, (*x_nbr, c), 1, True),
                   copy(4, me, (*y_nbr, c), 1, True), copy(3, me, (*y_nbr, c), 0, True)]
        for cp in started:
            cp.start()
        copy(1, (*x_nbr, c), me, 0).wait_recv()
        started.append(copy(5, (*x_nbr, c), (*y_nbr, c), 0))
        started[-1].start()
        copy(4, (*y_nbr, c), me, 1).wait_recv()
        started.append(copy(6, (*y_nbr, c), (*x_nbr, c), 1))
        started[-1].start()
        copy(2, (*x_nbr, c), me, 1).wait_recv()
        started.append(copy(7, (*x_nbr, c), sibling))
        started[-1].start()
        copy(3, (*y_nbr, c), me, 0).wait_recv()
        started.append(copy(8, (*y_nbr, c), sibling))
        started[-1].start()
        copy(5, (*diag, c), me, 0).wait_recv()
        copy(6, (*diag, c), me, 1).wait_recv()
        started.append(copy(9, (*diag, c), sibling))
        started[-1].start()
        copy(0, sibling, me).wait_recv()
        for k, chip in ((7, x_nbr), (8, y_nbr), (9, diag)):
            copy(k, (*chip, 1 - c), me).wait_recv()
        for cp in started:
            cp.wait_send()
        mine.wait()

    return _call(
        body, name="all_gather_tree", in_specs=[ANY, ANY], out_specs=ANY,
        out_shape=jax.ShapeDtypeStruct((N_DEV, rows, cols), shard.dtype),
        scratch_shapes=[pltpu.SemaphoreType.DMA((10,)), pltpu.SemaphoreType.DMA((10,)), pltpu.SemaphoreType.DMA],
    )(shard, after)


def _rs_to_sibling(gs, name):
    n = len(gs)

    def body(*refs):
        g_refs, got_refs = refs[:n], refs[n:2 * n]
        send_sems, recv_sems = refs[2 * n:]
        x, y, c = _coords()
        copies = []
        for a in range(n):
            for p in range(4):
                cp = pltpu.make_async_remote_copy(
                    src_ref=g_refs[a].at[2 * p + (1 - c)], dst_ref=got_refs[a].at[p], send_sem=send_sems.at[4 * a + p],
                    recv_sem=recv_sems.at[4 * a + p], device_id=(x, y, 1 - c), device_id_type=MESH)
                cp.start()
                copies.append(cp)
        for cp in copies:
            cp.wait()

    return _call(
        body, name=name, in_specs=[ANY] * n, out_specs=[ANY] * n,
        out_shape=[jax.ShapeDtypeStruct((4,) + g.shape[1:], g.dtype) for g in gs],
        scratch_shapes=[pltpu.SemaphoreType.DMA((4 * n,)), pltpu.SemaphoreType.DMA((4 * n,))],
    )(*gs)


def _rs_pair_sum(g, got, c_idx, name):
    _, rows, cols = g.shape
    tr = _tile(rows, 256, 16)

    def body(c_ref, g_ref, got_ref, o_ref):
        o_ref[...] = (g_ref[...].astype(F32) + got_ref[...].astype(F32)).astype(o_ref.dtype)

    grid_spec = pltpu.PrefetchScalarGridSpec(
        num_scalar_prefetch=1, grid=(4, rows // tr),
        in_specs=[pl.BlockSpec((1, tr, cols), lambda p, i, c_ref: (2 * p + c_ref[0], i, 0)),
                  pl.BlockSpec((1, tr, cols), lambda p, i, c_ref: (p, i, 0))],
        out_specs=pl.BlockSpec((1, tr, cols), lambda p, i, c_ref: (p, i, 0)))
    return _call(
        body, name=name, grid_spec=grid_spec, out_shape=jax.ShapeDtypeStruct((4, rows, cols), g.dtype),
        compiler_params=_params(("parallel", "parallel")),
    )(c_idx, g, got)


def _to_chips_copies(p_refs, got_refs, send_sems, recv_sems):
    x, y, c = _coords()
    copies = []
    for a in range(len(p_refs)):
        for k, (fx, fy) in enumerate(CHIP_FLIPS):
            px, py = _flip(x, fx), _flip(y, fy)
            copies.append(pltpu.make_async_remote_copy(
                src_ref=p_refs[a].at[2 * px + py], dst_ref=got_refs[a].at[k], send_sem=send_sems.at[3 * a + k],
                recv_sem=recv_sems.at[3 * a + k], device_id=(px, py, c), device_id_type=MESH))
    return copies


def _rs_to_chips(partials, name):
    n = len(partials)

    def body(*refs):
        copies = _to_chips_copies(refs[:n], refs[n:2 * n], *refs[2 * n:])
        for cp in copies:
            cp.start()
        for cp in copies:
            cp.wait()

    return _call(
        body, name=name, in_specs=[ANY] * n, out_specs=[ANY] * n,
        out_shape=[jax.ShapeDtypeStruct((3,) + p.shape[1:], p.dtype) for p in partials],
        scratch_shapes=[pltpu.SemaphoreType.DMA((3 * n,)), pltpu.SemaphoreType.DMA((3 * n,))],
    )(*partials)


HBM = pl.BlockSpec(memory_space=pltpu.HBM)
SEM = pl.BlockSpec(memory_space=pltpu.SEMAPHORE)
SIDE_EFFECT = pltpu.CompilerParams(has_side_effects=pltpu.SideEffectType.DATAFLOW_SIDE_EFFECTING)


def _split_start(copies_fn, srcs, land_shapes, n_sems, name, after=None):
    n, m = len(srcs), len(land_shapes)
    extra = [] if after is None else [after]

    def body(*refs):
        outs = refs[n + m + len(extra):]
        send_sems, recv_sems, token = outs[0], outs[1], outs[-1]
        for cp in copies_fn(refs[:n], refs[n:n + m], send_sems, recv_sems):
            cp.start()
        token[...] = jnp.zeros_like(token)

    ins = [pltpu.with_memory_space_constraint(t, pltpu.HBM) for t in list(srcs) + [lax.empty(s.shape, s.dtype) for s in land_shapes]]
    res = _call(
        body, name=name, in_specs=[HBM] * (n + m) + [ANY] * len(extra),
        out_specs=[SEM, SEM] + [HBM] * (n + m) + [pl.BlockSpec(memory_space=pltpu.VMEM)],
        out_shape=[pltpu.SemaphoreType.DMA((n_sems,)), pltpu.SemaphoreType.DMA((n_sems,))]
        + [pltpu.HBM(t.shape, t.dtype) for t in ins] + [jax.ShapeDtypeStruct((8, LANES), F32)],
        input_output_aliases={i: 2 + i for i in range(n + m)}, compiler_params=SIDE_EFFECT,
    )(*ins, *extra)
    return dict(sems=(res[0], res[1]), srcs=res[2:2 + n], lands=res[2 + n:2 + n + m], token=res[-1])


def _split_wait(copies_fn, started, after, name):
    n, m = len(started["srcs"]), len(started["lands"])

    def body(*refs):
        for cp in copies_fn(refs[:n], refs[n:n + m], refs[n + m], refs[n + m + 1]):
            cp.wait_send()
            cp.wait_recv()

    bufs = list(started["srcs"]) + list(started["lands"])
    res = _call(
        body, name=name, in_specs=[HBM] * (n + m) + [SEM, SEM, ANY], out_specs=[HBM] * (n + m),
        out_shape=[pltpu.HBM(t.shape, t.dtype) for t in bufs],
        input_output_aliases={i: i for i in range(n + m)}, compiler_params=SIDE_EFFECT,
    )(*bufs, *started["sems"], after)
    return res[:n], res[n:]


def _to_all_copies(x_refs, out_refs, send_sems, recv_sems):
    x, y, c = _coords()
    copies = []
    for a in range(len(x_refs)):
        for k in range(N_DEV - 1):
            fx, fy, fc = ((k + 1) >> 2) & 1, ((k + 1) >> 1) & 1, (k + 1) & 1
            copies.append(pltpu.make_async_remote_copy(
                src_ref=x_refs[a], dst_ref=out_refs[a].at[4 * x + 2 * y + c], send_sem=send_sems.at[7 * a + k],
                recv_sem=recv_sems.at[7 * a + k], device_id=(_flip(x, fx), _flip(y, fy), _flip(c, fc)), device_id_type=MESH))
    return copies


def _fill_own_block(gathered, shard, me_idx, name):
    rows, cols = shard.shape
    tr = _tile(rows, 512, 16)

    def body(me_ref, g_ref, s_ref, o_ref):
        o_ref[0] = s_ref[...]

    grid_spec = pltpu.PrefetchScalarGridSpec(
        num_scalar_prefetch=1, grid=(rows // tr,),
        in_specs=[ANY, pl.BlockSpec((tr, cols), lambda i, me: (i, 0))],
        out_specs=pl.BlockSpec((1, tr, cols), lambda i, me: (me[0], i, 0)))
    return _call(
        body, name=name, grid_spec=grid_spec, out_shape=jax.ShapeDtypeStruct(gathered.shape, gathered.dtype),
        input_output_aliases={1: 0}, compiler_params=_params(("arbitrary",)),
    )(me_idx, gathered, shard)


def _rs_chip_sum(partial, got, chip_idx, name, part=0, n_parts=1, dst=None):
    _, rows, cols = partial.shape
    tr = _tile(rows, 256, 16)
    steps = rows // tr
    n_dst = 0 if dst is None else 1

    def body(p_idx_ref, p_ref, got_ref, *refs):
        refs[n_dst][...] = ((p_ref[0].astype(F32) + got_ref[0].astype(F32)) + got_ref[1].astype(F32)) + got_ref[2].astype(F32)

    grid_spec = pltpu.PrefetchScalarGridSpec(
        num_scalar_prefetch=1, grid=(steps,),
        in_specs=[pl.BlockSpec((1, tr, cols), lambda i, p_ref: (p_ref[0], i, 0)),
                  pl.BlockSpec((3, tr, cols), lambda i, p_ref: (0, i, 0))] + [ANY] * n_dst,
        out_specs=pl.BlockSpec((tr, cols), lambda i, p_ref: (part * steps + i, 0)))
    return _call(
        body, name=name, grid_spec=grid_spec, out_shape=jax.ShapeDtypeStruct((n_parts * rows, cols), F32),
        input_output_aliases={3: 0} if n_dst else {}, compiler_params=_params(("parallel",)),
    )(chip_idx, partial, got, *([] if dst is None else [dst]))


def _rs_begin(gs, tag, split):
    c_idx = jnp.reshape(lax.axis_index("c"), (1,)).astype(jnp.int32)
    gots = _rs_to_sibling(gs, "rs_to_sibling_" + tag)
    partials = [_rs_pair_sum(g, got, c_idx, "rs_pair_sum_%s%d" % (tag, a)) for a, (g, got) in enumerate(zip(gs, gots))]
    if not split:
        return dict(partials=partials, gots=_rs_to_chips(partials, "rs_to_chips_" + tag))
    lands = [jax.ShapeDtypeStruct((3,) + p.shape[1:], p.dtype) for p in partials]
    return _split_start(_to_chips_copies, partials, lands, 3 * len(partials), "rs_to_chips_start_" + tag)


def _rs_finish(begun, tag, after=None, part=0, n_parts=1, dsts=None):
    x, y, _ = _coords()
    chip_idx = jnp.reshape(2 * x + y, (1,)).astype(jnp.int32)
    if "gots" in begun:
        partials, gots = begun["partials"], begun["gots"]
    else:
        partials, gots = _split_wait(_to_chips_copies, begun, after, "rs_to_chips_wait_" + tag)
    return [_rs_chip_sum(p, got, chip_idx, "rs_chip_sum_%s%d" % (tag, a), part, n_parts, None if dsts is None else dsts[a])
            for a, (p, got) in enumerate(zip(partials, gots))]


RUNS = 3
RUN_FIELDS = 5


def _lane_gather_table(src_of, ahead):
    n_blocks = src_of.shape[0] // LANES
    tab = np.zeros((n_blocks + 3 * ahead, RUNS, RUN_FIELDS), np.int32)
    for t in range(n_blocks):
        runs = []
        for lane in range(LANES):
            slab, col = (int(v) for v in src_of[t * LANES + lane])
            if slab < 0:
                continue
            key = (slab, col // LANES, col % LANES - lane)
            if runs and runs[-1][0] == key and runs[-1][2] == lane:
                runs[-1][2] = lane + 1
            else:
                runs.append([key, lane, lane + 1])
        assert len(runs) <= RUNS
        slots = [None] * RUNS
        for key, lo, hi in sorted(runs, key=lambda r: r[0][:2]):
            slots[slots.index(None)] = (key[0], key[1], key[2], lo, hi)
        for e in range(RUNS):
            kept = (tab[t - ahead, e, 0], tab[t - ahead, e, 1], 0, 0, 0) if t >= ahead else tab[t, e]
            tab[t, e] = slots[e] if slots[e] is not None else kept
    tab[n_blocks:, :, :2] = np.tile(tab[n_blocks - ahead:n_blocks, :, :2], (3, 1, 1))
    return tab.reshape(-1)


def _place_run(tab_ref, t, e, block, under):
    base = (t * RUNS + e) * RUN_FIELDS
    shift, lo, hi = tab_ref[base + 2], tab_ref[base + 3], tab_ref[base + 4]
    lane = lax.broadcasted_iota(jnp.int32, (1, LANES), 1)
    return jnp.where((lane >= lo) & (lane < hi), pltpu.roll(block.astype(F32), (LANES - shift) % LANES, 1), under)


def _lane_gather_cols(src, table, out_slabs, out_width, name, sub):
    _, rows, _ = src.shape
    steps_per_slab = -(-out_width // (sub * LANES))

    def body(tab_ref, *refs):
        o_ref = refs[sub * RUNS]
        for s in range(sub):
            t = pl.program_id(0) * sub + s
            ops = refs[s * RUNS:(s + 1) * RUNS]
            lanes = slice(s * LANES, (s + 1) * LANES)
            o_ref[0, :, lanes] = _place_run(tab_ref, t, 1, ops[1][0], _place_run(tab_ref, t, 0, ops[0][0], 0.0)).astype(BF16)
            last = (t * RUNS + RUNS - 1) * RUN_FIELDS

            @pl.when(tab_ref[last + 4] > tab_ref[last + 3])
            def _():
                o_ref[0, :, lanes] = _place_run(tab_ref, t, RUNS - 1, ops[RUNS - 1][0], o_ref[0, :, lanes].astype(F32)).astype(BF16)

    def src_spec(s, e):
        at = lambda t: ((t * sub + s) * RUNS + e) * RUN_FIELDS
        return pl.BlockSpec((1, rows, LANES), lambda t, tab: (tab[at(t)], 0, tab[at(t) + 1]))

    grid_spec = pltpu.PrefetchScalarGridSpec(
        num_scalar_prefetch=1, grid=(out_slabs * steps_per_slab,),
        in_specs=[src_spec(s, e) for s in range(sub) for e in range(RUNS)],
        out_specs=pl.BlockSpec((1, rows, sub * LANES), lambda t, tab: (t // steps_per_slab, 0, t % steps_per_slab)))
    return _call(
        body, name=name, grid_spec=grid_spec, out_shape=jax.ShapeDtypeStruct((out_slabs, rows, out_width), BF16),
        compiler_params=_params(("arbitrary",)),
    )(jnp.asarray(table), *([src] * (sub * RUNS)))


def _all_reduce_small(vec):
    rows, cols = vec.shape

    def body(v_ref, o_ref, buf, send_sems, recv_sems):
        x, y, c = _coords()
        me = 4 * x + 2 * y + c
        buf[me] = v_ref[...]
        copies = []
        for k in range(N_DEV - 1):
            fx, fy, fc = ((k + 1) >> 2) & 1, ((k + 1) >> 1) & 1, (k + 1) & 1
            cp = pltpu.make_async_remote_copy(
                src_ref=v_ref, dst_ref=buf.at[me], send_sem=send_sems.at[k], recv_sem=recv_sems.at[k],
                device_id=(_flip(x, fx), _flip(y, fy), _flip(c, fc)), device_id_type=MESH)
            cp.start()
            copies.append(cp)
        for cp in copies:
            cp.wait()
        total = buf[0]
        for j in range(1, N_DEV):
            total = total + buf[j]
        o_ref[...] = total

    vmem = pl.BlockSpec(memory_space=pltpu.VMEM)
    return _call(
        body, name="all_reduce_small", in_specs=[vmem], out_specs=vmem,
        out_shape=jax.ShapeDtypeStruct((rows, cols), F32),
        scratch_shapes=[pltpu.VMEM((N_DEV, rows, cols), F32), pltpu.SemaphoreType.DMA((N_DEV - 1,)),
                        pltpu.SemaphoreType.DMA((N_DEV - 1,))],
    )(vec)


def _w_in_column_maps(ns, o_ba, n_logit, n_main, n_all, own_sub):
    own = np.arange(N_DEV * ns)
    work_of_own = np.where(own < o_ba, own, np.where(own < o_ba + n_logit, n_main + own - o_ba, own - n_logit))
    to_work = np.full((n_all, 2), -1, np.int64)
    to_work[work_of_own, 0] = own // ns
    to_work[work_of_own, 1] = own % ns
    slab_width = -(-ns // (own_sub * LANES)) * own_sub * LANES
    to_own = np.full((N_DEV, slab_width, 2), -1, np.int64)
    to_own[:, :ns, 0] = 0
    to_own[:, :ns, 1] = work_of_own.reshape(N_DEV, ns)
    return to_work, to_own.reshape(-1, 2)


def kernel(x, meta_tokens, norm_w, w_in, conv_w, A_log, dt_bias, pool_mix, pool_scale, dn_norm_w, w_pool_out, w_dn_out, w_o, final_norm_w, loss_target, m_meta_tokens, m_norm_w, m_w_in, m_conv_w, m_A_log, m_dt_bias, m_pool_mix, m_pool_scale, m_dn_norm_w, m_w_pool_out, m_w_dn_out, m_w_o, m_final_norm_w, v_meta_tokens, v_norm_w, v_w_in, v_conv_w, v_A_log, v_dt_bias, v_pool_mix, v_pool_scale, v_dn_norm_w, v_w_pool_out, v_w_dn_out, v_w_o, v_final_norm_w):
    seq, d = x.shape[1], x.shape[2]
    n_meta = meta_tokens.shape[0]
    n_heads, hd = A_log.shape[-1], dn_norm_w.shape[-1]
    dn = n_heads * hd
    pw, ng = pool_scale.shape[-1], pool_mix.shape[1]
    pg = pw // ng
    kw = conv_w.shape[1]
    pad = (-n_meta) % CHUNK
    x0 = pad + n_meta
    lp = x0 + seq
    ns = w_in.shape[-1]
    in_cols = N_DEV * ns
    o_q, o_k, o_v, o_zd = 2 * pw, 2 * pw + dn, 2 * pw + 2 * dn, 2 * pw + 3 * dn
    o_ba = 2 * pw + 4 * dn
    o_gp, o_gd = o_ba, o_ba + d
    n_main = o_gd + d
    n_all = n_main + 2 * LANES
    assert lp % CHUNK == 0 and in_cols == n_main + 2 * n_heads and 2 * n_heads <= LANES and hd == LANES
    cs, ms = conv_w.shape[-1], meta_tokens.shape[-1]
    mr = pool_mix.shape[2]
    assert ms == pg and cs % pg == 0
    work_sub = max(s for s in (6, 3, 2, 1) if (n_all // LANES) % s == 0)
    own_sub = 15
    to_work, to_own = _w_in_column_maps(ns, o_ba, 2 * n_heads, n_main, n_all, own_sub)
    cols_major = lambda t: jnp.transpose(t, (1, 0, 2)).reshape(t.shape[1], N_DEV * t.shape[2])

    mix_g, conv_g, meta_g = _all_gather([pool_mix[0].reshape(ng * mr, pg).astype(BF16), conv_w[0], meta_tokens])
    win_g = _all_gather_tree(w_in[0].astype(BF16), after=meta_g)
    late_shards = [w_pool_out[0].astype(BF16), w_dn_out[0].astype(BF16), w_o[0].astype(BF16)]
    late_weights = _split_start(_to_all_copies, late_shards, [jax.ShapeDtypeStruct((N_DEV,) + s.shape, BF16) for s in late_shards],
                                (N_DEV - 1) * len(late_shards), "gather_out_proj_start", after=win_g)
    norm_w_in = norm_w + late_weights["token"][0, 0]
    w_all = _lane_gather_cols(win_g, _lane_gather_table(to_work, work_sub), 1, n_all, "w_in_to_work", work_sub).reshape(d, n_all)
    mix_f = jnp.transpose(mix_g.reshape(N_DEV, ng, mr, pg), (1, 0, 2, 3)).reshape(ng, pg, pg)
    conv_f = cols_major(conv_g)
    meta_f = cols_major(meta_g)

    h0, xn = _norm_in_fwd(x[0], meta_f, norm_w_in, pad)
    proj = _matmul(xn, w_all, NN, F32, lp, 768, 2048, "proj")
    y_pool = _pool_fwd(proj, mix_f, pool_scale, pad)
    conv_q, conv_k, conv_v = (conv_f[:, i * dn:(i + 1) * dn] for i in range(3))
    qn = _conv_fwd(proj, o_q, conv_q, hd, float(hd) ** -0.5, "conv_q_fwd")
    kn = _conv_fwd(proj, o_k, conv_k, hd, 1.0, "conv_k_fwd")
    vv = _conv_fwd(proj, o_v, conv_v, hd, None, "conv_v_fwd")
    logit_lanes = (n_heads, LANES - 2 * n_heads)
    prm = jnp.pad(A_log, ((0, 7), logit_lanes)) + jnp.pad(dt_bias, ((1, 6), logit_lanes))
    y_dn, hist, tmats = _chunk_fwd(qn, kn, vv, proj, o_zd, n_main, prm, dn_norm_w, n_heads, pad)
    me_idx = jnp.reshape(4 * lax.axis_index("x") + 2 * lax.axis_index("y") + lax.axis_index("c"), (1,)).astype(jnp.int32)
    _, landed = _split_wait(_to_all_copies, late_weights, y_dn, "gather_out_proj_wait")
    wpo_g, wdn_g, wo_g = (_fill_own_block(g, s, me_idx, "own_block_%d" % i) for i, (g, s) in enumerate(zip(landed, late_shards)))
    wpo_f = cols_major(wpo_g)
    wdn_f = wdn_g.reshape(dn, d)
    wo_f = wo_g.reshape(d, d)
    p_out = _matmul(y_pool, wpo_f, NN, F32, 1056, 1024, 1024, "pool_out")
    q_out = _matmul(y_dn, wdn_f, NN, F32, 1056, 1024, 2048, "dn_out")
    merged = _merge_fwd(p_out, q_out, proj, o_gp, o_gd)
    mo = _matmul(merged, wo_f, NN, F32, 1056, 1024, 2048, "w_o_fwd")
    dh1, d_fw, loss_part = _final_loss(h0, mo, final_norm_w.reshape(1, d), loss_target[0], x0)

    d_merged = _matmul(dh1, wo_f, NT, F32, 1056, 1024, 2048, "w_o_bwd_x")
    g_wo = _matmul(merged, dh1, TN, BF16, 1024, 1024, lp, "w_o_bwd_w")
    d_p, d_q, d_gp, d_gd = _merge_bwd(p_out, q_out, proj, o_gp, o_gd, d_merged)
    d_ypool = _matmul(d_p, wpo_f, NT, F32, 1056, 1024, 2048, "pool_out_bwd_x")
    g_wpo = _matmul(y_pool.T, d_p, NN, BF16, 1024, 1024, lp, "pool_out_bwd_w", col_blocks=N_DEV)
    d_ydn = _matmul(d_q, wdn_f, NT, F32, 1056, 1024, 2048, "dn_out_bwd_x")
    g_wdn = _matmul(y_dn, d_q, TN, BF16, 1024, 1024, lp, "dn_out_bwd_w")
    rs_early = _rs_begin([g_wpo, g_wdn.reshape(N_DEV, dn // N_DEV, d), g_wo.reshape(N_DEV, d // N_DEV, d)], "early", split=True)
    started = rs_early["token"][0, 0]
    d_u, d_zp, g_mix, g_pscale = _pool_bwd(proj, mix_f, pool_scale + started, d_ypool, pad)
    d_proj = lax.empty((lp, n_all), BF16)
    d_qn, d_kn, d_vv, d_ba, d_proj, d_prm, g_dnw = _chunk_bwd(qn, kn, vv, proj, o_zd, n_main, prm + started, dn_norm_w, hist, tmats,
                                                              d_ydn, n_heads, pad, d_proj)
    d_proj, g_cq = _conv_bwd(proj, o_q, conv_q, d_qn, hd, float(hd) ** -0.5, pad, "conv_q_bwd", d_proj)
    d_proj, g_ck = _conv_bwd(proj, o_k, conv_k, d_kn, hd, 1.0, pad, "conv_k_bwd", d_proj)
    d_proj, g_cv = _conv_bwd(proj, o_v, conv_v, d_vv, hd, None, pad, "conv_v_bwd", d_proj)
    for off, piece in ((0, d_u), (pw, d_zp), (o_gp, d_gp), (o_gd, d_gd), (n_main, d_ba.astype(BF16)), (n_main + LANES, jnp.zeros((lp, LANES), BF16))):
        d_proj = lax.dynamic_update_slice(d_proj, piece, (0, off))
    xn_t, rs_late, token = xn.T, [], None
    for half in range(2):
        rows = slice(half * (d // 2), (half + 1) * (d // 2))
        g_wall = _matmul(xn_t[rows], d_proj, NN, BF16, 1024, 768, lp, "w_in_bwd_w_%d" % half, after=token)
        g_win = _lane_gather_cols(g_wall.reshape(1, d // 2, n_all), _lane_gather_table(to_own, own_sub), N_DEV, ns,
                                  "w_in_grad_to_own_%d" % half, own_sub)
        rs_late.append(_rs_begin([g_win], "late%d" % half, split=True))
        token = rs_late[-1]["token"]
    d_xn = _matmul(d_proj, w_all, NT, F32, lp, 512, 2432, "w_in_bwd_x", after=token)
    d_head, grad_x, g_nw = _norm_in_bwd(h0, norm_w, d_xn, dh1, x0)
    grad_x = grad_x[None]

    by_cols = lambda t: jnp.transpose(t.reshape(t.shape[0], N_DEV, t.shape[1] // N_DEV), (1, 0, 2))
    g_conv = by_cols(jnp.concatenate([g_cq, g_ck, g_cv], axis=1)).reshape(N_DEV, kw * cs // pg, pg)
    conv_rows = -(-g_conv.shape[1] // 16) * 16
    g_small = jnp.concatenate(
        [jnp.transpose(g_mix.reshape(ng, N_DEV, mr, pg), (1, 0, 2, 3)).reshape(N_DEV, ng * mr, pg), by_cols(d_head[pad:x0]),
         jnp.pad(g_conv, ((0, 0), (0, conv_rows - g_conv.shape[1]), (0, 0)))], axis=1).astype(BF16)
    r_small, = _rs_finish(_rs_begin([g_small], "small", split=False), "small")
    r_mix, r_meta = r_small[:ng * mr], r_small[ng * mr:ng * mr + n_meta]
    r_conv = r_small[ng * mr + n_meta:ng * mr + n_meta + kw * cs // pg]
    r_wpo, r_wdn, r_wo = _rs_finish(rs_early, "early", after=r_small)

    small = [g_nw[0], d_fw[0], g_pscale[0], g_dnw[0], d_prm[0], d_prm[1], loss_part[0]]
    s_sizes = [t.shape[0] for t in small]
    s_cols = -(-sum(s_sizes) // (8 * LANES)) * LANES
    s_vec = jnp.concatenate(small + [jnp.zeros((8 * s_cols - sum(s_sizes),), F32)]).reshape(8, s_cols)
    s_red = _all_reduce_small(s_vec)
    s_sum = s_red.reshape(-1)
    r_win = None
    for half, begun in enumerate(rs_late):
        r_win = _rs_finish(begun, "late%d" % half, after=s_red, part=half, n_parts=2, dsts=r_win)
    r_win, = r_win
    s_offs = [sum(s_sizes[:i]) for i in range(len(s_sizes))]
    s_take = lambda i, n=None, o=0: s_sum[s_offs[i] + o:s_offs[i] + o + (s_sizes[i] if n is None else n)]

    grads = {
        "meta_tokens": r_meta, "norm_w": s_take(0).reshape(norm_w.shape),
        "w_in": r_win.reshape(w_in.shape), "conv_w": r_conv.reshape(conv_w.shape),
        "A_log": s_take(4, n_heads, n_heads).reshape(A_log.shape), "dt_bias": s_take(5, n_heads, n_heads).reshape(dt_bias.shape),
        "pool_mix": r_mix.reshape(pool_mix.shape), "pool_scale": s_take(2).reshape(pool_scale.shape),
        "dn_norm_w": s_take(3).reshape(dn_norm_w.shape), "w_pool_out": r_wpo.reshape(w_pool_out.shape),
        "w_dn_out": r_wdn.reshape(w_dn_out.shape), "w_o": r_wo.reshape(w_o.shape),
        "final_norm_w": s_take(1).reshape(final_norm_w.shape),
    }
    loss = s_take(6, 1)[0]

    weights = dict(meta_tokens=meta_tokens, norm_w=norm_w, w_in=w_in, conv_w=conv_w, A_log=A_log, dt_bias=dt_bias,
                   pool_mix=pool_mix, pool_scale=pool_scale, dn_norm_w=dn_norm_w, w_pool_out=w_pool_out, w_dn_out=w_dn_out,
                   w_o=w_o, final_norm_w=final_norm_w)
    m_in = dict(meta_tokens=m_meta_tokens, norm_w=m_norm_w, w_in=m_w_in, conv_w=m_conv_w, A_log=m_A_log, dt_bias=m_dt_bias,
                pool_mix=m_pool_mix, pool_scale=m_pool_scale, dn_norm_w=m_dn_norm_w, w_pool_out=m_w_pool_out,
                w_dn_out=m_w_dn_out, w_o=m_w_o, final_norm_w=m_final_norm_w)
    v_in = dict(meta_tokens=v_meta_tokens, norm_w=v_norm_w, w_in=v_w_in, conv_w=v_conv_w, A_log=v_A_log, dt_bias=v_dt_bias,
                pool_mix=v_pool_mix, pool_scale=v_pool_scale, dn_norm_w=v_dn_norm_w, w_pool_out=v_w_pool_out,
                w_dn_out=v_w_dn_out, w_o=v_w_o, final_norm_w=v_final_norm_w)
    names = list(weights)
    upd = {n: _adamw(weights[n], grads[n], m_in[n], v_in[n], "adamw_" + n) for n in names}
    return (loss, grad_x, *[grads[n] for n in names], *[upd[n][0] for n in names], *[upd[n][1] for n in names],
            *[upd[n][2] for n in names])
```

```python
import functools
import math

import jax
import jax.numpy as jnp
import numpy as np
from jax import lax
from jax.experimental import pallas as pl
from jax.experimental.pallas import tpu as pltpu

F32 = jnp.float32
BF16 = jnp.bfloat16
HIGHEST = lax.Precision.HIGHEST
MESH = pl.DeviceIdType.MESH

CHUNK = 64
NORM_EPS = 1e-6
POOL_WINDOWS = (2, 4, 8, 16)
ADAM_LR, ADAM_B1, ADAM_B2, ADAM_EPS, ADAM_WD, ADAM_STEP = 0.001, 0.9, 0.999, 1e-08, 0.01, 10
N_DEV = 8
LANES = 128
VMEM_LIMIT = 48 * 1024 * 1024

NN = (((1,), (0,)), ((), ()))
NT = (((1,), (1,)), ((), ()))
TN = (((0,), (0,)), ((), ()))


def _call(body, **kw):
    return pl.pallas_call(body, **kw)


def _params(sem=None):
    return pltpu.CompilerParams(dimension_semantics=sem, vmem_limit_bytes=VMEM_LIMIT)


def _tile(n, pref, align):
    for d in range(min(pref, n), 0, -1):
        if n % d == 0 and d % align == 0:
            return d
    return n


def _dot(a, b, dims=NN, precision=None):
    return lax.dot_general(a, b, dims, precision=precision, preferred_element_type=F32)


def _sigmoid(x):
    return 0.5 * jnp.tanh(0.5 * x) + 0.5


def _silu(x):
    return x * _sigmoid(x)


def _softplus(x):
    return jnp.maximum(x, 0.0) + jnp.log(1.0 + jnp.exp(-jnp.abs(x)))


def _rmsnorm(x, w):
    return x * lax.rsqrt(jnp.mean(x * x, axis=-1, keepdims=True) + NORM_EPS) * w


def _shift_down(x, j, row):
    if j == 0:
        return x
    return jnp.where(row >= j, pltpu.roll(x, j, 0), 0.0)


def _shift_up(x, j, row):
    if j == 0:
        return x
    n = x.shape[0]
    return jnp.where(row < n - j, pltpu.roll(x, n - j, 0), 0.0)


def _matmul(a, b, dims, out_dtype, tm, tn, tk, name, col_blocks=None, after=None):
    ta = dims == TN
    tb = dims == NT
    m, kdim = (a.shape[1], a.shape[0]) if ta else a.shape
    n = b.shape[0] if tb else b.shape[1]
    if col_blocks:
        tn = n // col_blocks
    tm, tn, tk = _tile(m, tm, 8), _tile(n, tn, LANES), _tile(kdim, tk, LANES if not ta else 16)
    nk = kdim // tk

    n_extra = 0 if after is None else 1

    def body(a_ref, b_ref, *refs):
        o_ref, scratch = refs[n_extra], refs[n_extra + 1:]
        part = _dot(a_ref[...].astype(BF16), b_ref[...].astype(BF16), dims)
        if nk == 1:
            o_ref[...] = part.astype(o_ref.dtype).reshape(o_ref.shape)
            return
        acc_ref, = scratch
        k = pl.program_id(2)

        @pl.when(k == 0)
        def _():
            acc_ref[...] = part

        @pl.when(k > 0)
        def _():
            acc_ref[...] += part

        @pl.when(k == nk - 1)
        def _():
            o_ref[...] = acc_ref[...].astype(o_ref.dtype).reshape(o_ref.shape)

    a_spec = pl.BlockSpec((tk, tm), lambda i, j, k: (k, i)) if ta else pl.BlockSpec((tm, tk), lambda i, j, k: (i, k))
    b_spec = pl.BlockSpec((tn, tk), lambda i, j, k: (j, k)) if tb else pl.BlockSpec((tk, tn), lambda i, j, k: (k, j))
    if col_blocks:
        out_spec = pl.BlockSpec((1, tm, tn), lambda i, j, k: (j, i, 0))
        out_shape = jax.ShapeDtypeStruct((col_blocks, m, tn), out_dtype)
    else:
        out_spec = pl.BlockSpec((tm, tn), lambda i, j, k: (i, j))
        out_shape = jax.ShapeDtypeStruct((m, n), out_dtype)
    return _call(
        body, name=name, grid=(m // tm, n // tn, nk),
        in_specs=[a_spec, b_spec] + [ANY] * n_extra, out_specs=out_spec, out_shape=out_shape,
        scratch_shapes=[] if nk == 1 else [pltpu.VMEM((tm, tn), F32)],
        compiler_params=_params(("parallel", "parallel", "arbitrary")),
    )(a, b, *([] if after is None else [after]))


def _norm_in_fwd(x2d, meta, w, pad):
    seq, d = x2d.shape
    x0 = pad + meta.shape[0]
    assert x0 % 16 == 0
    lp = x0 + seq
    tr = _tile(seq, 512, 16)
    vec = pl.BlockSpec((1, d), lambda i: (0, 0))
    shapes = [jax.ShapeDtypeStruct((lp, d), F32), jax.ShapeDtypeStruct((lp, d), BF16)]

    def body(x_ref, w_ref, h_ref, o_ref):
        h_ref[...] = x_ref[...]
        o_ref[...] = _rmsnorm(x_ref[...], w_ref[...]).astype(BF16)

    def head(m_ref, w_ref, h_in_ref, o_in_ref, h_ref, o_ref):
        h = jnp.concatenate([jnp.zeros((pad, d), F32), m_ref[...]], axis=0) if pad else m_ref[...]
        h_ref[...] = h
        o_ref[...] = _rmsnorm(h, w_ref[...]).astype(BF16)

    rows = _rows_after(x0, tr, d)
    h0, xn = _call(
        body, name="norm_in_fwd", grid=(seq // tr,), in_specs=[pl.BlockSpec((tr, d), lambda i: (i, 0)), vec],
        out_specs=[rows, rows], out_shape=shapes, compiler_params=_params(("parallel",)),
    )(x2d, w)
    first = pl.BlockSpec((x0, d), lambda i: (0, 0))
    return _call(
        head, name="norm_in_fwd_head", grid=(1,), in_specs=[pl.BlockSpec(meta.shape, lambda i: (0, 0)), vec, ANY, ANY],
        out_specs=[first, first], out_shape=shapes, input_output_aliases={2: 0, 3: 1}, compiler_params=_params(("arbitrary",)),
    )(meta, w, h0, xn)


def _rows_after(x0, tr, d):
    step = math.gcd(x0, tr)
    return pl.BlockSpec((pl.Element(tr), pl.Element(d)), lambda i: (pl.multiple_of(x0 + tr * i, step), 0))


def _norm_in_bwd(h0, w, dxn, dh1, x0):
    lp, d = h0.shape
    tr = _tile(lp - x0, 512, 8)
    vec = pl.BlockSpec((1, d), lambda i: (0, 0))

    def make(body_rows, first):
        def body(h_ref, w_ref, da_ref, dh1_ref, dh_ref, dw_ref):
            _, vjp = jax.vjp(_rmsnorm, h_ref[...], w_ref[...])
            dh, dw = vjp(da_ref[...])
            dh_ref[...] = dh + dh1_ref[...]

            @pl.when(pl.program_id(0) == 0)
            def _():
                dw_ref[...] = jnp.zeros_like(dw_ref)

            dw_ref[...] += dw

        rows_in = pl.BlockSpec((x0, d), lambda i: (0, 0)) if first else _rows_after(x0, tr, d)
        return _call(
            body, name="norm_in_bwd_head" if first else "norm_in_bwd", grid=(1 if first else (lp - x0) // tr,),
            in_specs=[rows_in, vec, rows_in, rows_in],
            out_specs=[pl.BlockSpec((body_rows, d), lambda i: (i, 0)), vec],
            out_shape=[jax.ShapeDtypeStruct((x0 if first else lp - x0, d), F32), jax.ShapeDtypeStruct((1, d), F32)],
            compiler_params=_params(("arbitrary",)),
        )(h0, w, dxn, dh1)

    d_head, dw_head = make(x0, True)
    grad_x, dw_rest = make(tr, False)
    return d_head, grad_x, dw_head + dw_rest


def _final_loss(h0, mo, fw, tgt, x0):
    lp, d = h0.shape
    tr = _tile(lp - x0, 512, 8)

    def body(h_ref, mo_ref, fw_ref, t_ref, dh_ref, dw_ref, loss_ref):
        tgt_v = t_ref[...]

        def loss_fn(h1, w):
            err = _rmsnorm(h1, w) - tgt_v
            return 0.5 * jnp.sum(jnp.mean(err * err, axis=-1, keepdims=True), axis=0, keepdims=True)

        loss, vjp = jax.vjp(loss_fn, h_ref[...] + mo_ref[...], fw_ref[...])
        dh, dw = vjp(jnp.ones((1, 1), F32))
        dh_ref[...] = dh

        @pl.when(pl.program_id(0) == 0)
        def _():
            dw_ref[...] = jnp.zeros_like(dw_ref)
            loss_ref[...] = jnp.zeros_like(loss_ref)

        dw_ref[...] += dw
        loss_ref[...] += jnp.broadcast_to(loss, loss_ref.shape)

    def zero_head(dh_in_ref, dh_ref):
        dh_ref[...] = jnp.zeros_like(dh_ref)

    rows = _rows_after(x0, tr, d)
    vec = pl.BlockSpec((1, d), lambda i: (0, 0))
    dh1, dw, loss = _call(
        body, name="final_loss", grid=((lp - x0) // tr,),
        in_specs=[rows, rows, vec, pl.BlockSpec((tr, d), lambda i: (i, 0))],
        out_specs=[rows, vec, pl.BlockSpec((8, LANES), lambda i: (0, 0))],
        out_shape=[jax.ShapeDtypeStruct((lp, d), F32), jax.ShapeDtypeStruct((1, d), F32), jax.ShapeDtypeStruct((8, LANES), F32)],
        compiler_params=_params(("arbitrary",)),
    )(h0, mo, fw, tgt)
    dh1 = _call(
        zero_head, name="final_loss_head", grid=(1,), in_specs=[ANY], out_specs=pl.BlockSpec((x0, d), lambda i: (0, 0)),
        out_shape=jax.ShapeDtypeStruct((lp, d), F32), input_output_aliases={0: 0}, compiler_params=_params(("arbitrary",)),
    )(dh1)
    return dh1, dw, loss


def _pool_select(parts, g):
    out = parts[-1]
    for gi in range(len(parts) - 2, -1, -1):
        out = jnp.where(g == gi, parts[gi], out)
    return out


def _pool_count(row, g, pad):
    win = _pool_select([jnp.full(row.shape, float(w), F32) for w in POOL_WINDOWS], g)
    return jnp.maximum(jnp.minimum((row - pad + 1).astype(F32), win), 1.0)


def _pooled(u, g, row, pad):
    sums, s, span = [], u, 1
    for w in POOL_WINDOWS:
        while span < w:
            s = s + _shift_down(s, span, row)
            span *= 2
        sums.append(s)
    return _pool_select(sums, g) / _pool_count(row, g, pad) - u


def _pooled_adjoint(dp, g, row, pad):
    e = dp / _pool_count(row, g, pad)
    sums, s, span = [], e, 1
    for w in POOL_WINDOWS:
        while span < w:
            s = s + _shift_up(s, span, row)
            span *= 2
        sums.append(s)
    return _pool_select(sums, g) - dp


def _pool_specs(lp, pg, ng, z_off):
    u_spec = pl.BlockSpec((lp, pg), lambda g: (0, g))
    z_spec = pl.BlockSpec((lp, pg), lambda g: (0, z_off + g))
    mix_spec = pl.BlockSpec((1, pg, pg), lambda g: (g, 0, 0))
    vec_spec = pl.BlockSpec((1, pg), lambda g: (0, g))
    return u_spec, z_spec, mix_spec, vec_spec


def _pool_fwd(proj, mix, scale, pad):
    lp = proj.shape[0]
    ng, pg, _ = mix.shape
    pw = ng * pg

    def body(u_ref, z_ref, mix_ref, sc_ref, y_ref):
        g = pl.program_id(0)
        row = lax.broadcasted_iota(jnp.int32, (lp, 1), 0)
        pooled = _pooled(u_ref[...], g, row, pad)
        mixed = _dot(pooled.astype(BF16), mix_ref[0])
        y_ref[...] = (mixed * sc_ref[...] * _silu(z_ref[...])).astype(BF16)

    u_spec, z_spec, mix_spec, vec_spec = _pool_specs(lp, pg, ng, pw // pg)
    return _call(
        body, name="pool_fwd", grid=(ng,), in_specs=[u_spec, z_spec, mix_spec, vec_spec], out_specs=u_spec,
        out_shape=jax.ShapeDtypeStruct((lp, pw), BF16), compiler_params=_params(("parallel",)),
    )(proj, proj, mix, scale)


def _pool_bwd(proj, mix, scale, dy, pad):
    lp = proj.shape[0]
    ng, pg, _ = mix.shape
    pw = ng * pg

    def body(u_ref, z_ref, mix_ref, sc_ref, dy_ref, du_ref, dz_ref, dmix_ref, dsc_ref):
        g = pl.program_id(0)
        row = lax.broadcasted_iota(jnp.int32, (lp, 1), 0)
        real = row >= pad
        z = z_ref[...]
        pooled = _pooled(u_ref[...], g, row, pad).astype(BF16)
        mixed = _dot(pooled, mix_ref[0])
        sig = _sigmoid(z)
        sz = z * sig
        dyv = dy_ref[...]
        dsc_ref[...] = jnp.sum(dyv * mixed * sz, axis=0, keepdims=True)
        d_sz = dyv * mixed * sc_ref[...]
        dz_ref[...] = jnp.where(real, d_sz * (sig + sz * (1.0 - sig)), 0.0).astype(BF16)
        d_mixed = (dyv * sc_ref[...] * sz).astype(BF16)
        dmix_ref[0] = _dot(pooled, d_mixed, TN)
        d_pooled = _dot(d_mixed, mix_ref[0], NT)
        du_ref[...] = jnp.where(real, _pooled_adjoint(d_pooled, g, row, pad), 0.0).astype(BF16)

    u_spec, z_spec, mix_spec, vec_spec = _pool_specs(lp, pg, ng, pw // pg)
    return _call(
        body, name="pool_bwd", grid=(ng,),
        in_specs=[u_spec, z_spec, mix_spec, vec_spec, u_spec], out_specs=[u_spec, u_spec, mix_spec, vec_spec],
        out_shape=[jax.ShapeDtypeStruct((lp, pw), BF16), jax.ShapeDtypeStruct((lp, pw), BF16),
                   jax.ShapeDtypeStruct((ng, pg, pg), F32), jax.ShapeDtypeStruct((1, pw), F32)],
        compiler_params=_params(("parallel",)),
    )(proj, proj, mix, scale, dy)


def _conv_pre(x, w, row):
    kw = w.shape[0]
    y = w[kw - 1:kw, :] * x
    for kk in range(kw - 1):
        y = y + w[kk:kk + 1, :] * _shift_down(x, kw - 1 - kk, row)
    return y


def _conv_post(y, out_scale):
    s = _silu(y)
    if out_scale is None:
        return s
    return s * lax.rsqrt(jnp.sum(s * s, axis=-1, keepdims=True) + NORM_EPS) * out_scale


def _conv_fwd(proj, col_off, w, hd, out_scale, name):
    lp = proj.shape[0]
    kw, width = w.shape
    blk0 = col_off // hd

    def body(x_ref, w_ref, o_ref):
        row = lax.broadcasted_iota(jnp.int32, (lp, 1), 0)
        o_ref[...] = _conv_post(_conv_pre(x_ref[...], w_ref[...], row), out_scale)

    return _call(
        body, name=name, grid=(width // hd,),
        in_specs=[pl.BlockSpec((lp, hd), lambda j: (0, blk0 + j)), pl.BlockSpec((kw, hd), lambda j: (0, j))],
        out_specs=pl.BlockSpec((lp, hd), lambda j: (0, j)),
        out_shape=jax.ShapeDtypeStruct((lp, width), F32), compiler_params=_params(("parallel",)),
    )(proj, w)


def _conv_bwd(proj, col_off, w, d_out, hd, out_scale, pad, name, dst):
    lp = proj.shape[0]
    kw, width = w.shape
    blk0 = col_off // hd

    def body(x_ref, w_ref, do_ref, dst_ref, dx_ref, dw_ref):
        row = lax.broadcasted_iota(jnp.int32, (lp, 1), 0)
        real = row >= pad
        x, wv = x_ref[...], w_ref[...]
        _, vjp = jax.vjp(functools.partial(_conv_post, out_scale=out_scale), _conv_pre(x, wv, row))
        dy = jnp.where(real, vjp(do_ref[...])[0], 0.0)
        dx = wv[kw - 1:kw, :] * dy
        dw_ref[kw - 1:kw, :] = jnp.sum(dy * x, axis=0, keepdims=True)
        for kk in range(kw - 1):
            ahead = _shift_up(dy, kw - 1 - kk, row)
            dx = dx + wv[kk:kk + 1, :] * ahead
            dw_ref[kk:kk + 1, :] = jnp.sum(ahead * x, axis=0, keepdims=True)
        dx_ref[...] = jnp.where(real, dx, 0.0).astype(BF16)

    col = pl.BlockSpec((lp, hd), lambda j: (0, j))
    at_off = pl.BlockSpec((lp, hd), lambda j: (0, blk0 + j))
    wspec = pl.BlockSpec((kw, hd), lambda j: (0, j))
    return _call(
        body, name=name, grid=(width // hd,),
        in_specs=[at_off, wspec, col, ANY], out_specs=[at_off, wspec],
        out_shape=[jax.ShapeDtypeStruct(dst.shape, BF16), jax.ShapeDtypeStruct((kw, width), F32)],
        input_output_aliases={3: 0}, compiler_params=_params(("parallel",)),
    )(proj, w, d_out, dst)


HEADS_PER_STEP = 16
HEADS_PER_STEP_BWD = 16


def _matmul_with_direct_vjp(dims, da_dims, db_dims, db_swapped):
    @jax.custom_vjp
    def mm(a, b):
        return _dot(a, b, dims)

    def fwd(a, b):
        return _dot(a, b, dims), (a, b)

    def bwd(res, g):
        a, b = res
        return _dot(g, b, da_dims) if not db_swapped[0] else _dot(b, g, da_dims), _dot(a, g, db_dims) if not db_swapped[1] else _dot(g, a, db_dims)

    mm.defvjp(fwd, bwd)
    return mm


_mm_nn = _matmul_with_direct_vjp(NN, NT, TN, (False, False))
_mm_nt = _matmul_with_direct_vjp(NT, NN, TN, (False, True))
_mm_tn = _matmul_with_direct_vjp(TN, NT, NN, (True, False))


def _each(fn, *lists):
    return [fn(*args) for args in zip(*lists)]


def _dot3_each(a_list, b_list, dims=NN):
    hi = lambda t: t.astype(BF16)
    lo = lambda t, t_hi: (t - t_hi.astype(F32)).astype(BF16)
    dot = lambda x, y: _dot(x, y, dims)
    a_hi, b_hi = _each(hi, a_list), _each(hi, b_list)
    a_lo, b_lo = _each(lo, a_list, a_hi), _each(lo, b_list, b_hi)
    hh, hl, lh = _each(dot, a_hi, b_hi), _each(dot, a_hi, b_lo), _each(dot, a_lo, b_hi)
    return _each(lambda x, y, w: x + (y + w), hh, hl, lh)


@jax.custom_vjp
def _unit_lower_inverse(lmats):
    c = lmats[0].shape[0]
    eye = lax.broadcasted_iota(jnp.int32, (c, c), 0) == lax.broadcasted_iota(jnp.int32, (c, c), 1)
    a = [-m for m in lmats]
    tmat = [jnp.where(eye, 1.0, 0.0).astype(F32) + m for m in a]
    span = 2
    while span < c:
        a = _dot3_each(a, a)
        tmat = _each(lambda t, u: t + u, tmat, _dot3_each(tmat, a))
        span *= 2
    return tuple(tmat)


def _unit_lower_inverse_fwd(lmats):
    tmats = _unit_lower_inverse(lmats)
    return tmats, tmats


def _unit_lower_inverse_bwd(tmats, cts):
    left = _each(lambda t, ct: _dot(t, ct, TN, HIGHEST), tmats, cts)
    return (tuple(_each(lambda m, t: -_dot(m, t, NT, HIGHEST), left, tmats)),)


_unit_lower_inverse.defvjp(_unit_lower_inverse_fwd, _unit_lower_inverse_bwd)


@jax.custom_vjp
def _known_inverse(lmats, tmats):
    return tmats


def _known_inverse_fwd(lmats, tmats):
    return tmats, tmats


def _known_inverse_bwd(tmats, cts):
    return _unit_lower_inverse_bwd(tmats, cts)[0], tuple(jnp.zeros_like(t) for t in tmats)


_known_inverse.defvjp(_known_inverse_fwd, _known_inverse_bwd)


def _chunk_math(states, q, k, v, ba, z, prm, nw, head0, rowmask, n_heads, tmats=None, keep_tmats=False):
    c = q.shape[0]
    heads = list(range(len(states)))
    hd = q.shape[1] // len(states)
    lane = lax.broadcasted_iota(jnp.int32, ba.shape, 1)
    sub = lax.broadcasted_iota(jnp.int32, (ba.shape[1], c), 0)
    ri = lax.broadcasted_iota(jnp.int32, (c, c), 0)
    ci = lax.broadcasted_iota(jnp.int32, (c, c), 1)
    last = lax.broadcasted_iota(jnp.int32, (c, 1), 0) == c - 1
    causal, strict = ri >= ci, ri > ci
    beta_all = _sigmoid(ba) * rowmask
    g_all = -jnp.exp(prm[0:1, :]) * _softplus(ba + prm[1:2, :]) * rowmask
    gcum_all = _dot(jnp.where(causal, 1.0, 0.0).astype(F32), g_all, precision=HIGHEST)
    gcum_t = gcum_all.T
    split = lambda t: [t[:, j * hd:(j + 1) * hd] for j in heads]
    qs, ks, vs, zs = split(q), split(k), split(v), split(z)
    beta = [jnp.sum(jnp.where(lane == head0 + j, beta_all, 0.0), axis=1, keepdims=True) for j in heads]
    gcum = [jnp.sum(jnp.where(lane == n_heads + head0 + j, gcum_all, 0.0), axis=1, keepdims=True) for j in heads]
    grow = [jnp.sum(jnp.where(sub == n_heads + head0 + j, gcum_t, 0.0), axis=0, keepdims=True) for j in heads]
    glast = _each(lambda gc: jnp.sum(jnp.where(last, gc, 0.0), axis=0, keepdims=True), gcum)
    decay = _each(lambda gc, gr: jnp.where(causal, jnp.exp(jnp.where(causal, gc - gr, 0.0)), 0.0), gcum, grow)
    eg = _each(jnp.exp, gcum)
    k_beta = _each(jnp.multiply, ks, beta)
    kk = _each(_mm_nt, k_beta, ks)
    lmats = tuple(_each(lambda m, dc: jnp.where(strict, m * dc, 0.0), kk, decay))
    tmat = list(_unit_lower_inverse(lmats) if tmats is None else _known_inverse(lmats, tuple(tmats)))
    uw = _each(lambda t, vj, b, kb, e: _mm_nn(t, jnp.concatenate([vj * b, kb * e], axis=1)), tmat, vs, beta, k_beta, eg)
    u_c, w_c = _each(lambda m: m[:, :hd], uw), _each(lambda m: m[:, hd:], uw)
    qk = _each(lambda a, b, dc: jnp.where(causal, _mm_nt(a, b) * dc, 0.0), qs, ks, decay)
    from_state = _each(lambda w, a, e, s: _mm_nn(jnp.concatenate([w, a * e], axis=0), s), w_c, qs, eg, list(states))
    v_new = _each(lambda u, m: u - m[:c], u_c, from_state)
    o = _each(lambda m, a, vn: m[c:] + _mm_nn(a, vn), from_state, qk, v_new)
    k_dec = _each(lambda a, gl, gc: a * jnp.exp(gl - gc), ks, glast, gcum)
    new_states = _each(lambda s, gl, kd, vn: s * jnp.exp(gl) + _mm_tn(kd, vn), list(states), glast, k_dec, v_new)
    ys = _each(lambda oj, zj: _rmsnorm(oj, nw) * _silu(zj), o, zs)
    if keep_tmats:
        return jnp.concatenate(ys, axis=1), tuple(new_states), tuple(tmat)
    return jnp.concatenate(ys, axis=1), tuple(new_states)


def _chunk_specs(nc, hd, n_heads, z_off, ba_off, rev):
    cidx = (lambda c: nc - 1 - c) if rev else (lambda c: c)
    hb = min(HEADS_PER_STEP_BWD if rev else HEADS_PER_STEP, n_heads)
    assert n_heads % hb == 0 and z_off % (hb * hd) == 0 and ba_off % LANES == 0
    blk = lambda off: pl.BlockSpec((CHUNK, hb * hd), lambda c, g: (cidx(c), off + g))
    ba_spec = lambda off: pl.BlockSpec((CHUNK, LANES), lambda c, g: (cidx(c), off // LANES))
    prm_spec = pl.BlockSpec((8, LANES), lambda c, g: (0, 0))
    nw_spec = pl.BlockSpec((1, hd), lambda c, g: (0, 0))
    st_spec = pl.BlockSpec((1, hb, hd, hd), lambda c, g: (cidx(c), g, 0, 0))
    return blk, ba_spec, prm_spec, nw_spec, st_spec, blk(z_off // (hb * hd))


def _rowmask(chunk_idx, pad):
    row = chunk_idx * CHUNK + lax.broadcasted_iota(jnp.int32, (CHUNK, 1), 0)
    return jnp.where(row >= pad, 1.0, 0.0).astype(F32)


def _chunk_fwd(qn, kn, vv, proj, z_off, ba_off, prm, nw, n_heads, pad):
    lp, dn = qn.shape
    hd = dn // n_heads
    nc = lp // CHUNK
    hb = min(HEADS_PER_STEP, n_heads)

    def body(q_ref, k_ref, v_ref, ba_ref, z_ref, prm_ref, nw_ref, y_ref, hist_ref, tm_ref, st_ref):
        c, g = pl.program_id(0), pl.program_id(1)

        @pl.when(c == 0)
        def _():
            for j in range(hb):
                st_ref[g * hb + j] = jnp.zeros((hd, hd), F32)

        states = tuple(st_ref[g * hb + j] for j in range(hb))
        for j in range(hb):
            hist_ref[0, j] = states[j]
        y, new_states, tmats = _chunk_math(states, q_ref[...], k_ref[...], v_ref[...], ba_ref[...], z_ref[...], prm_ref[...],
                                           nw_ref[...], g * hb, _rowmask(c, pad), n_heads, keep_tmats=True)
        y_ref[...] = y.astype(BF16)
        for j in range(hb):
            st_ref[g * hb + j] = new_states[j]
            tm_ref[0, j] = tmats[j]

    blk, ba_spec, prm_spec, nw_spec, st_spec, z_spec = _chunk_specs(nc, hd, n_heads, z_off, ba_off, False)
    tm_spec = pl.BlockSpec((1, hb, CHUNK, CHUNK), lambda c, g: (c, g, 0, 0))
    return _call(
        body, name="chunk_fwd", grid=(nc, n_heads // hb),
        in_specs=[blk(0), blk(0), blk(0), ba_spec(ba_off), z_spec, prm_spec, nw_spec], out_specs=[blk(0), st_spec, tm_spec],
        out_shape=[jax.ShapeDtypeStruct((lp, dn), BF16), jax.ShapeDtypeStruct((nc, n_heads, hd, hd), F32),
                   jax.ShapeDtypeStruct((nc, n_heads, CHUNK, CHUNK), F32)],
        scratch_shapes=[pltpu.VMEM((n_heads, hd, hd), F32)],
        compiler_params=_params(("arbitrary", "arbitrary")),
    )(qn, kn, vv, proj, proj, prm, nw)


def _chunk_bwd(qn, kn, vv, proj, z_off, ba_off, prm, nw, hist, tmats, dy, n_heads, pad, d_proj):
    lp, dn = qn.shape
    hd = dn // n_heads
    nc = lp // CHUNK
    hb = min(HEADS_PER_STEP_BWD, n_heads)

    def body(q_ref, k_ref, v_ref, ba_ref, z_ref, prm_ref, nw_ref, hist_ref, tm_ref, dy_ref, d_proj_ref,
             dq_ref, dk_ref, dv_ref, dba_ref, dz_ref, dprm_ref, dnw_ref, dst_ref):
        step, g = pl.program_id(0), pl.program_id(1)

        @pl.when(step == 0)
        def _():
            for j in range(hb):
                dst_ref[g * hb + j] = jnp.zeros((hd, hd), F32)

        @pl.when((step == 0) & (g == 0))
        def _():
            dprm_ref[...] = jnp.zeros_like(dprm_ref)
            dnw_ref[...] = jnp.zeros_like(dnw_ref)

        @pl.when(g == 0)
        def _():
            dba_ref[...] = jnp.zeros_like(dba_ref)

        def fn(states, q, k, v, ba, z, prm_v, nw_v, known):
            return _chunk_math(states, q, k, v, ba, z, prm_v, nw_v, g * hb, _rowmask(nc - 1 - step, pad), n_heads, tmats=known)

        states = tuple(hist_ref[0, j] for j in range(hb))
        known = tuple(tm_ref[0, j] for j in range(hb))
        _, vjp = jax.vjp(fn, states, q_ref[...], k_ref[...], v_ref[...], ba_ref[...], z_ref[...], prm_ref[...], nw_ref[...], known)
        dst, dq, dk, dv, dba, dz, dprm, dnw, _ = vjp((dy_ref[...], tuple(dst_ref[g * hb + j] for j in range(hb))))
        for j in range(hb):
            dst_ref[g * hb + j] = dst[j]
        dq_ref[...] = dq
        dk_ref[...] = dk
        dv_ref[...] = dv
        dz_ref[...] = dz.astype(BF16)
        dba_ref[...] += dba
        dprm_ref[...] += dprm
        dnw_ref[...] += dnw

    blk, ba_spec, prm_spec, nw_spec, st_spec, z_spec = _chunk_specs(nc, hd, n_heads, z_off, ba_off, True)
    f32_full = jax.ShapeDtypeStruct((lp, dn), F32)
    tm_spec = pl.BlockSpec((1, hb, CHUNK, CHUNK), lambda c, g: (nc - 1 - c, g, 0, 0))
    return _call(
        body, name="chunk_bwd", grid=(nc, n_heads // hb),
        in_specs=[blk(0), blk(0), blk(0), ba_spec(ba_off), z_spec, prm_spec, nw_spec, st_spec, tm_spec, blk(0), ANY],
        out_specs=[blk(0), blk(0), blk(0), ba_spec(0), z_spec, prm_spec, nw_spec],
        out_shape=[f32_full, f32_full, f32_full, jax.ShapeDtypeStruct((lp, LANES), F32), jax.ShapeDtypeStruct(d_proj.shape, BF16),
                   jax.ShapeDtypeStruct((8, LANES), F32), jax.ShapeDtypeStruct((1, hd), F32)],
        scratch_shapes=[pltpu.VMEM((n_heads, hd, hd), F32)],
        input_output_aliases={10: 4}, compiler_params=_params(("arbitrary", "arbitrary")),
    )(qn, kn, vv, proj, proj, prm, nw, hist, tmats, dy, d_proj)


def _merge_math(p, q, gp, gd):
    return _sigmoid(gp) * p + _sigmoid(gd) * q


def _merge_specs(lp, d, gp_off, gd_off):
    tr, tc = _tile(lp, 176, 16), _tile(d, 2048, LANES)
    blk = pl.BlockSpec((tr, tc), lambda i, j: (i, j))
    gp_spec = pl.BlockSpec((tr, tc), lambda i, j: (i, gp_off // tc + j))
    gd_spec = pl.BlockSpec((tr, tc), lambda i, j: (i, gd_off // tc + j))
    return (lp // tr, d // tc), blk, gp_spec, gd_spec


def _merge_fwd(p, q, proj, gp_off, gd_off):
    lp, d = p.shape
    grid, blk, gp_spec, gd_spec = _merge_specs(lp, d, gp_off, gd_off)

    def body(p_ref, q_ref, gp_ref, gd_ref, o_ref):
        o_ref[...] = _merge_math(p_ref[...], q_ref[...], gp_ref[...], gd_ref[...]).astype(BF16)

    return _call(
        body, name="merge_fwd", grid=grid, in_specs=[blk, blk, gp_spec, gd_spec], out_specs=blk,
        out_shape=jax.ShapeDtypeStruct((lp, d), BF16), compiler_params=_params(("parallel", "parallel")),
    )(p, q, proj, proj)


def _merge_bwd(p, q, proj, gp_off, gd_off, dm):
    lp, d = p.shape
    grid, blk, gp_spec, gd_spec = _merge_specs(lp, d, gp_off, gd_off)

    def body(p_ref, q_ref, gp_ref, gd_ref, dm_ref, dp_ref, dq_ref, dgp_ref, dgd_ref):
        _, vjp = jax.vjp(_merge_math, p_ref[...], q_ref[...], gp_ref[...], gd_ref[...])
        for ref, val in zip((dp_ref, dq_ref, dgp_ref, dgd_ref), vjp(dm_ref[...])):
            ref[...] = val.astype(BF16)

    out = jax.ShapeDtypeStruct((lp, d), BF16)
    return _call(
        body, name="merge_bwd", grid=grid, in_specs=[blk, blk, gp_spec, gd_spec, blk], out_specs=[blk] * 4,
        out_shape=[out] * 4, compiler_params=_params(("parallel", "parallel")),
    )(p, q, proj, proj, dm)


def _adamw(w, g, m, v, name):
    shape = w.shape
    w2, g2, m2, v2 = (t.reshape((-1, shape[-1])) for t in (w, g, m, v))
    rows, cols = w2.shape
    tr = _tile(rows, 128, 8)

    def body(w_ref, g_ref, m_ref, v_ref, d_ref, nm_ref, nv_ref):
        gv = g_ref[...]
        nm = ADAM_B1 * m_ref[...] + (1.0 - ADAM_B1) * gv
        nv = ADAM_B2 * v_ref[...] + (1.0 - ADAM_B2) * (gv * gv)
        m_hat = nm / (1.0 - ADAM_B1 ** ADAM_STEP)
        v_hat = nv / (1.0 - ADAM_B2 ** ADAM_STEP)
        d_ref[...] = -ADAM_LR * (m_hat / (jnp.sqrt(v_hat) + ADAM_EPS) + ADAM_WD * w_ref[...])
        nm_ref[...] = nm
        nv_ref[...] = nv

    blk = pl.BlockSpec((tr, cols), lambda i: (i, 0))
    out = jax.ShapeDtypeStruct((rows, cols), F32)
    res = _call(
        body, name=name, grid=(rows // tr,), in_specs=[blk] * 4, out_specs=[blk] * 3, out_shape=[out] * 3,
        compiler_params=_params(("parallel",)),
    )(w2, g2, m2, v2)
    return tuple(t.reshape(shape) for t in res)


def _coords():
    return lax.axis_index("x"), lax.axis_index("y"), lax.axis_index("c")


def _flip(v, bit):
    return 1 - v if bit else v


CHIP_FLIPS = ((1, 0), (0, 1), (1, 1))
ANY = pl.BlockSpec(memory_space=pl.ANY)


def _all_gather(shards):
    n = len(shards)

    def body(*refs):
        x_refs, out_refs = refs[:n], refs[n:2 * n]
        send_sems, recv_sems, local_sems = refs[2 * n:]
        x, y, c = _coords()
        sibling = (x, y, 1 - c)
        chips = [(_flip(x, fx), _flip(y, fy)) for fx, fy in CHIP_FLIPS]

        def copy(a, k, block, to, from_input=False):
            px, py, pc = block
            slot = out_refs[a].at[4 * px + 2 * py + pc]
            return pltpu.make_async_remote_copy(
                src_ref=x_refs[a] if from_input else slot, dst_ref=slot,
                send_sem=send_sems.at[7 * a + k], recv_sem=recv_sems.at[7 * a + k], device_id=to, device_id_type=MESH)

        mine = [pltpu.make_async_copy(x_refs[a], out_refs[a].at[4 * x + 2 * y + c], local_sems.at[a]) for a in range(n)]
        first = []
        for a in range(n):
            mine[a].start()
            first.append(copy(a, 0, (x, y, c), sibling, True))
            first += [copy(a, 1 + j, (x, y, c), (*chip, c), True) for j, chip in enumerate(chips)]
        for cp in first:
            cp.start()
        passed = []
        for j, chip in enumerate(chips):
            for a in range(n):
                copy(a, 1 + j, (*chip, c), (x, y, c)).wait_recv()
                passed.append(copy(a, 4 + j, (*chip, c), sibling))
                passed[-1].start()
        for a in range(n):
            copy(a, 0, (x, y, 1 - c), (x, y, c)).wait_recv()
            for j, chip in enumerate(chips):
                copy(a, 4 + j, (*chip, 1 - c), (x, y, c)).wait_recv()
        for cp in first + passed:
            cp.wait_send()
        for cp in mine:
            cp.wait()

    return _call(
        body, name="all_gather", in_specs=[ANY] * n, out_specs=[ANY] * n,
        out_shape=[jax.ShapeDtypeStruct((N_DEV,) + s.shape, s.dtype) for s in shards],
        scratch_shapes=[pltpu.SemaphoreType.DMA((7 * n,)), pltpu.SemaphoreType.DMA((7 * n,)), pltpu.SemaphoreType.DMA((n,))],
    )(*shards)


def _all_gather_tree(shard, after):
    rows, cols = shard.shape
    half = rows // 2
    assert rows % 32 == 0

    def body(x_ref, after_ref, out_ref, send_sems, recv_sems, local_sem):
        x, y, c = _coords()
        me, sibling = (x, y, c), (x, y, 1 - c)
        x_nbr, y_nbr, diag = (1 - x, y), (x, 1 - y), (1 - x, 1 - y)

        def part(ref, h):
            return ref if h is None else ref.at[pl.ds(h * half, half)]

        def copy(k, block, to, h=None, from_input=False):
            px, py, pc = block
            slot = part(out_ref.at[4 * px + 2 * py + pc], h)
            return pltpu.make_async_remote_copy(
                src_ref=part(x_ref, h) if from_input else slot, dst_ref=slot,
                send_sem=send_sems.at[k], recv_sem=recv_sems.at[k], device_id=to, device_id_type=MESH)

        mine = pltpu.make_async_copy(x_ref, out_ref.at[4 * x + 2 * y + c], local_sem)
        mine.start()
        started = [copy(0, me, sibling, None, True),
                   copy(1, me, (*x_nbr, c), 0, True), copy(2, me, (*x_nbr, c), 1, True),
                   copy(4, me, (*y_nbr, c), 1, True), copy(3, me, (*y_nbr, c), 0, True)]
        for cp in started:
            cp.start()
        copy(1, (*x_nbr, c), me, 0).wait_recv()
        started.append(copy(5, (*x_nbr, c), (*y_nbr, c), 0))
        started[-1].start()
        copy(4, (*y_nbr, c), me, 1).wait_recv()
        started.append(copy(6, (*y_nbr, c), (*x_nbr, c), 1))
        started[-1].start()
        copy(2, (*x_nbr, c), me, 1).wait_recv()
        started.append(copy(7, (*x_nbr, c), sibling))
        started[-1].start()
        copy(3, (*y_nbr, c), me, 0).wait_recv()
        started.append(copy(8, (*y_nbr, c), sibling))
        started[-1].start()
        copy(5, (*diag, c), me, 0).wait_recv()
        copy(6, (*diag, c), me, 1).wait_recv()
        started.append(copy(9, (*diag, c), sibling))
        started[-1].start()
        copy(0, sibling, me).wait_recv()
        for k, chip in ((7, x_nbr), (8, y_nbr), (9, diag)):
            copy(k, (*chip, 1 - c), me).wait_recv()
        for cp in started:
            cp.wait_send()
        mine.wait()

    return _call(
        body, name="all_gather_tree", in_specs=[ANY, ANY], out_specs=ANY,
        out_shape=jax.ShapeDtypeStruct((N_DEV, rows, cols), shard.dtype),
        scratch_shapes=[pltpu.SemaphoreType.DMA((10,)), pltpu.SemaphoreType.DMA((10,)), pltpu.SemaphoreType.DMA],
    )(shard, after)


def _rs_to_sibling(gs, name):
    n = len(gs)

    def body(*refs):
        g_refs, got_refs = refs[:n], refs[n:2 * n]
        send_sems, recv_sems = refs[2 * n:]
        x, y, c = _coords()
        copies = []
        for a in range(n):
            for p in range(4):
                cp = pltpu.make_async_remote_copy(
                    src_ref=g_refs[a].at[2 * p + (1 - c)], dst_ref=got_refs[a].at[p], send_sem=send_sems.at[4 * a + p],
                    recv_sem=recv_sems.at[4 * a + p], device_id=(x, y, 1 - c), device_id_type=MESH)
                cp.start()
                copies.append(cp)
        for cp in copies:
            cp.wait()

    return _call(
        body, name=name, in_specs=[ANY] * n, out_specs=[ANY] * n,
        out_shape=[jax.ShapeDtypeStruct((4,) + g.shape[1:], g.dtype) for g in gs],
        scratch_shapes=[pltpu.SemaphoreType.DMA((4 * n,)), pltpu.SemaphoreType.DMA((4 * n,))],
    )(*gs)


def _rs_pair_sum(g, got, c_idx, name):
    _, rows, cols = g.shape
    tr = _tile(rows, 256, 16)

    def body(c_ref, g_ref, got_ref, o_ref):
        o_ref[...] = (g_ref[...].astype(F32) + got_ref[...].astype(F32)).astype(o_ref.dtype)

    grid_spec = pltpu.PrefetchScalarGridSpec(
        num_scalar_prefetch=1, grid=(4, rows // tr),
        in_specs=[pl.BlockSpec((1, tr, cols), lambda p, i, c_ref: (2 * p + c_ref[0], i, 0)),
                  pl.BlockSpec((1, tr, cols), lambda p, i, c_ref: (p, i, 0))],
        out_specs=pl.BlockSpec((1, tr, cols), lambda p, i, c_ref: (p, i, 0)))
    return _call(
        body, name=name, grid_spec=grid_spec, out_shape=jax.ShapeDtypeStruct((4, rows, cols), g.dtype),
        compiler_params=_params(("parallel", "parallel")),
    )(c_idx, g, got)


def _to_chips_copies(p_refs, got_refs, send_sems, recv_sems):
    x, y, c = _coords()
    copies = []
    for a in range(len(p_refs)):
        for k, (fx, fy) in enumerate(CHIP_FLIPS):
            px, py = _flip(x, fx), _flip(y, fy)
            copies.append(pltpu.make_async_remote_copy(
                src_ref=p_refs[a].at[2 * px + py], dst_ref=got_refs[a].at[k], send_sem=send_sems.at[3 * a + k],
                recv_sem=recv_sems.at[3 * a + k], device_id=(px, py, c), device_id_type=MESH))
    return copies


def _rs_to_chips(partials, name):
    n = len(partials)

    def body(*refs):
        copies = _to_chips_copies(refs[:n], refs[n:2 * n], *refs[2 * n:])
        for cp in copies:
            cp.start()
        for cp in copies:
            cp.wait()

    return _call(
        body, name=name, in_specs=[ANY] * n, out_specs=[ANY] * n,
        out_shape=[jax.ShapeDtypeStruct((3,) + p.shape[1:], p.dtype) for p in partials],
        scratch_shapes=[pltpu.SemaphoreType.DMA((3 * n,)), pltpu.SemaphoreType.DMA((3 * n,))],
    )(*partials)


HBM = pl.BlockSpec(memory_space=pltpu.HBM)
SEM = pl.BlockSpec(memory_space=pltpu.SEMAPHORE)
SIDE_EFFECT = pltpu.CompilerParams(has_side_effects=pltpu.SideEffectType.DATAFLOW_SIDE_EFFECTING)


def _split_start(copies_fn, srcs, land_shapes, n_sems, name, after=None):
    n, m = len(srcs), len(land_shapes)
    extra = [] if after is None else [after]

    def body(*refs):
        outs = refs[n + m + len(extra):]
        send_sems, recv_sems, token = outs[0], outs[1], outs[-1]
        for cp in copies_fn(refs[:n], refs[n:n + m], send_sems, recv_sems):
            cp.start()
        token[...] = jnp.zeros_like(token)

    ins = [pltpu.with_memory_space_constraint(t, pltpu.HBM) for t in list(srcs) + [lax.empty(s.shape, s.dtype) for s in land_shapes]]
    res = _call(
        body, name=name, in_specs=[HBM] * (n + m) + [ANY] * len(extra),
        out_specs=[SEM, SEM] + [HBM] * (n + m) + [pl.BlockSpec(memory_space=pltpu.VMEM)],
        out_shape=[pltpu.SemaphoreType.DMA((n_sems,)), pltpu.SemaphoreType.DMA((n_sems,))]
        + [pltpu.HBM(t.shape, t.dtype) for t in ins] + [jax.ShapeDtypeStruct((8, LANES), F32)],
        input_output_aliases={i: 2 + i for i in range(n + m)}, compiler_params=SIDE_EFFECT,
    )(*ins, *extra)
    return dict(sems=(res[0], res[1]), srcs=res[2:2 + n], lands=res[2 + n:2 + n + m], token=res[-1])


def _split_wait(copies_fn, started, after, name):
    n, m = len(started["srcs"]), len(started["lands"])

    def body(*refs):
        for cp in copies_fn(refs[:n], refs[n:n + m], refs[n + m], refs[n + m + 1]):
            cp.wait_send()
            cp.wait_recv()

    bufs = list(started["srcs"]) + list(started["lands"])
    res = _call(
        body, name=name, in_specs=[HBM] * (n + m) + [SEM, SEM, ANY], out_specs=[HBM] * (n + m),
        out_shape=[pltpu.HBM(t.shape, t.dtype) for t in bufs],
        input_output_aliases={i: i for i in range(n + m)}, compiler_params=SIDE_EFFECT,
    )(*bufs, *started["sems"], after)
    return res[:n], res[n:]


def _to_all_copies(x_refs, out_refs, send_sems, recv_sems):
    x, y, c = _coords()
    copies = []
    for a in range(len(x_refs)):
        for k in range(N_DEV - 1):
            fx, fy, fc = ((k + 1) >> 2) & 1, ((k + 1) >> 1) & 1, (k + 1) & 1
            copies.append(pltpu.make_async_remote_copy(
                src_ref=x_refs[a], dst_ref=out_refs[a].at[4 * x + 2 * y + c], send_sem=send_sems.at[7 * a + k],
                recv_sem=recv_sems.at[7 * a + k], device_id=(_flip(x, fx), _flip(y, fy), _flip(c, fc)), device_id_type=MESH))
    return copies


def _fill_own_block(gathered, shard, me_idx, name):
    rows, cols = shard.shape
    tr = _tile(rows, 512, 16)

    def body(me_ref, g_ref, s_ref, o_ref):
        o_ref[0] = s_ref[...]

    grid_spec = pltpu.PrefetchScalarGridSpec(
        num_scalar_prefetch=1, grid=(rows // tr,),
        in_specs=[ANY, pl.BlockSpec((tr, cols), lambda i, me: (i, 0))],
        out_specs=pl.BlockSpec((1, tr, cols), lambda i, me: (me[0], i, 0)))
    return _call(
        body, name=name, grid_spec=grid_spec, out_shape=jax.ShapeDtypeStruct(gathered.shape, gathered.dtype),
        input_output_aliases={1: 0}, compiler_params=_params(("arbitrary",)),
    )(me_idx, gathered, shard)


def _rs_chip_sum(partial, got, chip_idx, name, part=0, n_parts=1, dst=None):
    _, rows, cols = partial.shape
    tr = _tile(rows, 256, 16)
    steps = rows // tr
    n_dst = 0 if dst is None else 1

    def body(p_idx_ref, p_ref, got_ref, *refs):
        refs[n_dst][...] = ((p_ref[0].astype(F32) + got_ref[0].astype(F32)) + got_ref[1].astype(F32)) + got_ref[2].astype(F32)

    grid_spec = pltpu.PrefetchScalarGridSpec(
        num_scalar_prefetch=1, grid=(steps,),
        in_specs=[pl.BlockSpec((1, tr, cols), lambda i, p_ref: (p_ref[0], i, 0)),
                  pl.BlockSpec((3, tr, cols), lambda i, p_ref: (0, i, 0))] + [ANY] * n_dst,
        out_specs=pl.BlockSpec((tr, cols), lambda i, p_ref: (part * steps + i, 0)))
    return _call(
        body, name=name, grid_spec=grid_spec, out_shape=jax.ShapeDtypeStruct((n_parts * rows, cols), F32),
        input_output_aliases={3: 0} if n_dst else {}, compiler_params=_params(("parallel",)),
    )(chip_idx, partial, got, *([] if dst is None else [dst]))


def _rs_begin(gs, tag, split):
    c_idx = jnp.reshape(lax.axis_index("c"), (1,)).astype(jnp.int32)
    gots = _rs_to_sibling(gs, "rs_to_sibling_" + tag)
    partials = [_rs_pair_sum(g, got, c_idx, "rs_pair_sum_%s%d" % (tag, a)) for a, (g, got) in enumerate(zip(gs, gots))]
    if not split:
        return dict(partials=partials, gots=_rs_to_chips(partials, "rs_to_chips_" + tag))
    lands = [jax.ShapeDtypeStruct((3,) + p.shape[1:], p.dtype) for p in partials]
    return _split_start(_to_chips_copies, partials, lands, 3 * len(partials), "rs_to_chips_start_" + tag)


def _rs_finish(begun, tag, after=None, part=0, n_parts=1, dsts=None):
    x, y, _ = _coords()
    chip_idx = jnp.reshape(2 * x + y, (1,)).astype(jnp.int32)
    if "gots" in begun:
        partials, gots = begun["partials"], begun["gots"]
    else:
        partials, gots = _split_wait(_to_chips_copies, begun, after, "rs_to_chips_wait_" + tag)
    return [_rs_chip_sum(p, got, chip_idx, "rs_chip_sum_%s%d" % (tag, a), part, n_parts, None if dsts is None else dsts[a])
            for a, (p, got) in enumerate(zip(partials, gots))]


RUNS = 3
RUN_FIELDS = 5


def _lane_gather_table(src_of, ahead):
    n_blocks = src_of.shape[0] // LANES
    tab = np.zeros((n_blocks + 3 * ahead, RUNS, RUN_FIELDS), np.int32)
    for t in range(n_blocks):
        runs = []
        for lane in range(LANES):
            slab, col = (int(v) for v in src_of[t * LANES + lane])
            if slab < 0:
                continue
            key = (slab, col // LANES, col % LANES - lane)
            if runs and runs[-1][0] == key and runs[-1][2] == lane:
                runs[-1][2] = lane + 1
            else:
                runs.append([key, lane, lane + 1])
        assert len(runs) <= RUNS
        slots = [None] * RUNS
        for key, lo, hi in sorted(runs, key=lambda r: r[0][:2]):
            slots[slots.index(None)] = (key[0], key[1], key[2], lo, hi)
        for e in range(RUNS):
            kept = (tab[t - ahead, e, 0], tab[t - ahead, e, 1], 0, 0, 0) if t >= ahead else tab[t, e]
            tab[t, e] = slots[e] if slots[e] is not None else kept
    tab[n_blocks:, :, :2] = np.tile(tab[n_blocks - ahead:n_blocks, :, :2], (3, 1, 1))
    return tab.reshape(-1)


def _place_run(tab_ref, t, e, block, under):
    base = (t * RUNS + e) * RUN_FIELDS
    shift, lo, hi = tab_ref[base + 2], tab_ref[base + 3], tab_ref[base + 4]
    lane = lax.broadcasted_iota(jnp.int32, (1, LANES), 1)
    return jnp.where((lane >= lo) & (lane < hi), pltpu.roll(block.astype(F32), (LANES - shift) % LANES, 1), under)


def _lane_gather_cols(src, table, out_slabs, out_width, name, sub):
    _, rows, _ = src.shape
    steps_per_slab = -(-out_width // (sub * LANES))

    def body(tab_ref, *refs):
        o_ref = refs[sub * RUNS]
        for s in range(sub):
            t = pl.program_id(0) * sub + s
            ops = refs[s * RUNS:(s + 1) * RUNS]
            lanes = slice(s * LANES, (s + 1) * LANES)
            o_ref[0, :, lanes] = _place_run(tab_ref, t, 1, ops[1][0], _place_run(tab_ref, t, 0, ops[0][0], 0.0)).astype(BF16)
            last = (t * RUNS + RUNS - 1) * RUN_FIELDS

            @pl.when(tab_ref[last + 4] > tab_ref[last + 3])
            def _():
                o_ref[0, :, lanes] = _place_run(tab_ref, t, RUNS - 1, ops[RUNS - 1][0], o_ref[0, :, lanes].astype(F32)).astype(BF16)

    def src_spec(s, e):
        at = lambda t: ((t * sub + s) * RUNS + e) * RUN_FIELDS
        return pl.BlockSpec((1, rows, LANES), lambda t, tab: (tab[at(t)], 0, tab[at(t) + 1]))

    grid_spec = pltpu.PrefetchScalarGridSpec(
        num_scalar_prefetch=1, grid=(out_slabs * steps_per_slab,),
        in_specs=[src_spec(s, e) for s in range(sub) for e in range(RUNS)],
        out_specs=pl.BlockSpec((1, rows, sub * LANES), lambda t, tab: (t // steps_per_slab, 0, t % steps_per_slab)))
    return _call(
        body, name=name, grid_spec=grid_spec, out_shape=jax.ShapeDtypeStruct((out_slabs, rows, out_width), BF16),
        compiler_params=_params(("arbitrary",)),
    )(jnp.asarray(table), *([src] * (sub * RUNS)))


def _all_reduce_small(vec):
    rows, cols = vec.shape

    def body(v_ref, o_ref, buf, send_sems, recv_sems):
        x, y, c = _coords()
        me = 4 * x + 2 * y + c
        buf[me] = v_ref[...]
        copies = []
        for k in range(N_DEV - 1):
            fx, fy, fc = ((k + 1) >> 2) & 1, ((k + 1) >> 1) & 1, (k + 1) & 1
            cp = pltpu.make_async_remote_copy(
                src_ref=v_ref, dst_ref=buf.at[me], send_sem=send_sems.at[k], recv_sem=recv_sems.at[k],
                device_id=(_flip(x, fx), _flip(y, fy), _flip(c, fc)), device_id_type=MESH)
            cp.start()
            copies.append(cp)
        for cp in copies:
            cp.wait()
        total = buf[0]
        for j in range(1, N_DEV):
            total = total + buf[j]
        o_ref[...] = total

    vmem = pl.BlockSpec(memory_space=pltpu.VMEM)
    return _call(
        body, name="all_reduce_small", in_specs=[vmem], out_specs=vmem,
        out_shape=jax.ShapeDtypeStruct((rows, cols), F32),
        scratch_shapes=[pltpu.VMEM((N_DEV, rows, cols), F32), pltpu.SemaphoreType.DMA((N_DEV - 1,)),
                        pltpu.SemaphoreType.DMA((N_DEV - 1,))],
    )(vec)


def _w_in_column_maps(ns, o_ba, n_logit, n_main, n_all, own_sub):
    own = np.arange(N_DEV * ns)
    work_of_own = np.where(own < o_ba, own, np.where(own < o_ba + n_logit, n_main + own - o_ba, own - n_logit))
    to_work = np.full((n_all, 2), -1, np.int64)
    to_work[work_of_own, 0] = own // ns
    to_work[work_of_own, 1] = own % ns
    slab_width = -(-ns // (own_sub * LANES)) * own_sub * LANES
    to_own = np.full((N_DEV, slab_width, 2), -1, np.int64)
    to_own[:, :ns, 0] = 0
    to_own[:, :ns, 1] = work_of_own.reshape(N_DEV, ns)
    return to_work, to_own.reshape(-1, 2)


def kernel(x, meta_tokens, norm_w, w_in, conv_w, A_log, dt_bias, pool_mix, pool_scale, dn_norm_w, w_pool_out, w_dn_out, w_o, final_norm_w, loss_target, m_meta_tokens, m_norm_w, m_w_in, m_conv_w, m_A_log, m_dt_bias, m_pool_mix, m_pool_scale, m_dn_norm_w, m_w_pool_out, m_w_dn_out, m_w_o, m_final_norm_w, v_meta_tokens, v_norm_w, v_w_in, v_conv_w, v_A_log, v_dt_bias, v_pool_mix, v_pool_scale, v_dn_norm_w, v_w_pool_out, v_w_dn_out, v_w_o, v_final_norm_w):
    seq, d = x.shape[1], x.shape[2]
    n_meta = meta_tokens.shape[0]
    n_heads, hd = A_log.shape[-1], dn_norm_w.shape[-1]
    dn = n_heads * hd
    pw, ng = pool_scale.shape[-1], pool_mix.shape[1]
    pg = pw // ng
    kw = conv_w.shape[1]
    pad = (-n_meta) % CHUNK
    x0 = pad + n_meta
    lp = x0 + seq
    ns = w_in.shape[-1]
    in_cols = N_DEV * ns
    o_q, o_k, o_v, o_zd = 2 * pw, 2 * pw + dn, 2 * pw + 2 * dn, 2 * pw + 3 * dn
    o_ba = 2 * pw + 4 * dn
    o_gp, o_gd = o_ba, o_ba + d
    n_main = o_gd + d
    n_all = n_main + 2 * LANES
    assert lp % CHUNK == 0 and in_cols == n_main + 2 * n_heads and 2 * n_heads <= LANES and hd == LANES
    cs, ms = conv_w.shape[-1], meta_tokens.shape[-1]
    mr = pool_mix.shape[2]
    assert ms == pg and cs % pg == 0
    work_sub = max(s for s in (6, 3, 2, 1) if (n_all // LANES) % s == 0)
    own_sub = 15
    to_work, to_own = _w_in_column_maps(ns, o_ba, 2 * n_heads, n_main, n_all, own_sub)
    cols_major = lambda t: jnp.transpose(t, (1, 0, 2)).reshape(t.shape[1], N_DEV * t.shape[2])

    mix_g, conv_g, meta_g = _all_gather([pool_mix[0].reshape(ng * mr, pg).astype(BF16), conv_w[0], meta_tokens])
    win_g = _all_gather_tree(w_in[0].astype(BF16), after=meta_g)
    late_shards = [w_pool_out[0].astype(BF16), w_dn_out[0].astype(BF16), w_o[0].astype(BF16)]
    late_weights = _split_start(_to_all_copies, late_shards, [jax.ShapeDtypeStruct((N_DEV,) + s.shape, BF16) for s in late_shards],
                                (N_DEV - 1) * len(late_shards), "gather_out_proj_start", after=win_g)
    norm_w_in = norm_w + late_weights["token"][0, 0]
    w_all = _lane_gather_cols(win_g, _lane_gather_table(to_work, work_sub), 1, n_all, "w_in_to_work", work_sub).reshape(d, n_all)
    mix_f = jnp.transpose(mix_g.reshape(N_DEV, ng, mr, pg), (1, 0, 2, 3)).reshape(ng, pg, pg)
    conv_f = cols_major(conv_g)
    meta_f = cols_major(meta_g)

    h0, xn = _norm_in_fwd(x[0], meta_f, norm_w_in, pad)
    proj = _matmul(xn, w_all, NN, F32, lp, 768, 2048, "proj")
    y_pool = _pool_fwd(proj, mix_f, pool_scale, pad)
    conv_q, conv_k, conv_v = (conv_f[:, i * dn:(i + 1) * dn] for i in range(3))
    qn = _conv_fwd(proj, o_q, conv_q, hd, float(hd) ** -0.5, "conv_q_fwd")
    kn = _conv_fwd(proj, o_k, conv_k, hd, 1.0, "conv_k_fwd")
    vv = _conv_fwd(proj, o_v, conv_v, hd, None, "conv_v_fwd")
    logit_lanes = (n_heads, LANES - 2 * n_heads)
    prm = jnp.pad(A_log, ((0, 7), logit_lanes)) + jnp.pad(dt_bias, ((1, 6), logit_lanes))
    y_dn, hist, tmats = _chunk_fwd(qn, kn, vv, proj, o_zd, n_main, prm, dn_norm_w, n_heads, pad)
    me_idx = jnp.reshape(4 * lax.axis_index("x") + 2 * lax.axis_index("y") + lax.axis_index("c"), (1,)).astype(jnp.int32)
    _, landed = _split_wait(_to_all_copies, late_weights, y_dn, "gather_out_proj_wait")
    wpo_g, wdn_g, wo_g = (_fill_own_block(g, s, me_idx, "own_block_%d" % i) for i, (g, s) in enumerate(zip(landed, late_shards)))
    wpo_f = cols_major(wpo_g)
    wdn_f = wdn_g.reshape(dn, d)
    wo_f = wo_g.reshape(d, d)
    p_out = _matmul(y_pool, wpo_f, NN, F32, 1056, 1024, 1024, "pool_out")
    q_out = _matmul(y_dn, wdn_f, NN, F32, 1056, 1024, 2048, "dn_out")
    merged = _merge_fwd(p_out, q_out, proj, o_gp, o_gd)
    mo = _matmul(merged, wo_f, NN, F32, 1056, 1024, 2048, "w_o_fwd")
    dh1, d_fw, loss_part = _final_loss(h0, mo, final_norm_w.reshape(1, d), loss_target[0], x0)

    d_merged = _matmul(dh1, wo_f, NT, F32, 1056, 1024, 2048, "w_o_bwd_x")
    g_wo = _matmul(merged, dh1, TN, BF16, 1024, 1024, lp, "w_o_bwd_w")
    d_p, d_q, d_gp, d_gd = _merge_bwd(p_out, q_out, proj, o_gp, o_gd, d_merged)
    d_ypool = _matmul(d_p, wpo_f, NT, F32, 1056, 1024, 2048, "pool_out_bwd_x")
    g_wpo = _matmul(y_pool.T, d_p, NN, BF16, 1024, 1024, lp, "pool_out_bwd_w", col_blocks=N_DEV)
    d_ydn = _matmul(d_q, wdn_f, NT, F32, 1056, 1024, 2048, "dn_out_bwd_x")
    g_wdn = _matmul(y_dn, d_q, TN, BF16, 1024, 1024, lp, "dn_out_bwd_w")
    rs_early = _rs_begin([g_wpo, g_wdn.reshape(N_DEV, dn // N_DEV, d), g_wo.reshape(N_DEV, d // N_DEV, d)], "early", split=True)
    started = rs_early["token"][0, 0]
    d_u, d_zp, g_mix, g_pscale = _pool_bwd(proj, mix_f, pool_scale + started, d_ypool, pad)
    d_proj = lax.empty((lp, n_all), BF16)
    d_qn, d_kn, d_vv, d_ba, d_proj, d_prm, g_dnw = _chunk_bwd(qn, kn, vv, proj, o_zd, n_main, prm + started, dn_norm_w, hist, tmats,
                                                              d_ydn, n_heads, pad, d_proj)
    d_proj, g_cq = _conv_bwd(proj, o_q, conv_q, d_qn, hd, float(hd) ** -0.5, pad, "conv_q_bwd", d_proj)
    d_proj, g_ck = _conv_bwd(proj, o_k, conv_k, d_kn, hd, 1.0, pad, "conv_k_bwd", d_proj)
    d_proj, g_cv = _conv_bwd(proj, o_v, conv_v, d_vv, hd, None, pad, "conv_v_bwd", d_proj)
    for off, piece in ((0, d_u), (pw, d_zp), (o_gp, d_gp), (o_gd, d_gd), (n_main, d_ba.astype(BF16)), (n_main + LANES, jnp.zeros((lp, LANES), BF16))):
        d_proj = lax.dynamic_update_slice(d_proj, piece, (0, off))
    xn_t, rs_late, token = xn.T, [], None
    for half in range(2):
        rows = slice(half * (d // 2), (half + 1) * (d // 2))
        g_wall = _matmul(xn_t[rows], d_proj, NN, BF16, 1024, 768, lp, "w_in_bwd_w_%d" % half, after=token)
        g_win = _lane_gather_cols(g_wall.reshape(1, d // 2, n_all), _lane_gather_table(to_own, own_sub), N_DEV, ns,
                                  "w_in_grad_to_own_%d" % half, own_sub)
        rs_late.append(_rs_begin([g_win], "late%d" % half, split=True))
        token = rs_late[-1]["token"]
    d_xn = _matmul(d_proj, w_all, NT, F32, lp, 512, 2432, "w_in_bwd_x", after=token)
    d_head, grad_x, g_nw = _norm_in_bwd(h0, norm_w, d_xn, dh1, x0)
    grad_x = grad_x[None]

    by_cols = lambda t: jnp.transpose(t.reshape(t.shape[0], N_DEV, t.shape[1] // N_DEV), (1, 0, 2))
    g_conv = by_cols(jnp.concatenate([g_cq, g_ck, g_cv], axis=1)).reshape(N_DEV, kw * cs // pg, pg)
    conv_rows = -(-g_conv.shape[1] // 16) * 16
    g_small = jnp.concatenate(
        [jnp.transpose(g_mix.reshape(ng, N_DEV, mr, pg), (1, 0, 2, 3)).reshape(N_DEV, ng * mr, pg), by_cols(d_head[pad:x0]),
         jnp.pad(g_conv, ((0, 0), (0, conv_rows - g_conv.shape[1]), (0, 0)))], axis=1).astype(BF16)
    r_small, = _rs_finish(_rs_begin([g_small], "small", split=False), "small")
    r_mix, r_meta = r_small[:ng * mr], r_small[ng * mr:ng * mr + n_meta]
    r_conv = r_small[ng * mr + n_meta:ng * mr + n_meta + kw * cs // pg]
    r_wpo, r_wdn, r_wo = _rs_finish(rs_early, "early", after=r_small)

    small = [g_nw[0], d_fw[0], g_pscale[0], g_dnw[0], d_prm[0], d_prm[1], loss_part[0]]
    s_sizes = [t.shape[0] for t in small]
    s_cols = -(-sum(s_sizes) // (8 * LANES)) * LANES
    s_vec = jnp.concatenate(small + [jnp.zeros((8 * s_cols - sum(s_sizes),), F32)]).reshape(8, s_cols)
    s_red = _all_reduce_small(s_vec)
    s_sum = s_red.reshape(-1)
    r_win = None
    for half, begun in enumerate(rs_late):
        r_win = _rs_finish(begun, "late%d" % half, after=s_red, part=half, n_parts=2, dsts=r_win)
    r_win, = r_win
    s_offs = [sum(s_sizes[:i]) for i in range(len(s_sizes))]
    s_take = lambda i, n=None, o=0: s_sum[s_offs[i] + o:s_offs[i] + o + (s_sizes[i] if n is None else n)]

    grads = {
        "meta_tokens": r_meta, "norm_w": s_take(0).reshape(norm_w.shape),
        "w_in": r_win.reshape(w_in.shape), "conv_w": r_conv.reshape(conv_w.shape),
        "A_log": s_take(4, n_heads, n_heads).reshape(A_log.shape), "dt_bias": s_take(5, n_heads, n_heads).reshape(dt_bias.shape),
        "pool_mix": r_mix.reshape(pool_mix.shape), "pool_scale": s_take(2).reshape(pool_scale.shape),
        "dn_norm_w": s_take(3).reshape(dn_norm_w.shape), "w_pool_out": r_wpo.reshape(w_pool_out.shape),
        "w_dn_out": r_wdn.reshape(w_dn_out.shape), "w_o": r_wo.reshape(w_o.shape),
        "final_norm_w": s_take(1).reshape(final_norm_w.shape),
    }
    loss = s_take(6, 1)[0]

    weights = dict(meta_tokens=meta_tokens, norm_w=norm_w, w_in=w_in, conv_w=conv_w, A_log=A_log, dt_bias=dt_bias,
                   pool_mix=pool_mix, pool_scale=pool_scale, dn_norm_w=dn_norm_w, w_pool_out=w_pool_out, w_dn_out=w_dn_out,
                   w_o=w_o, final_norm_w=final_norm_w)
    m_in = dict(meta_tokens=m_meta_tokens, norm_w=m_norm_w, w_in=m_w_in, conv_w=m_conv_w, A_log=m_A_log, dt_bias=m_dt_bias,
                pool_mix=m_pool_mix, pool_scale=m_pool_scale, dn_norm_w=m_dn_norm_w, w_pool_out=m_w_pool_out,
                w_dn_out=m_w_dn_out, w_o=m_w_o, final_norm_w=m_final_norm_w)
    v_in = dict(meta_tokens=v_meta_tokens, norm_w=v_norm_w, w_in=v_w_in, conv_w=v_conv_w, A_log=v_A_log, dt_bias=v_dt_bias,
                pool_mix=v_pool_mix, pool_scale=v_pool_scale, dn_norm_w=v_dn_norm_w, w_pool_out=v_w_pool_out,
                w_dn_out=v_w_dn_out, w_o=v_w_o, final_norm_w=v_final_norm_w)
    names = list(weights)
    upd = {n: _adamw(weights[n], grads[n], m_in[n], v_in[n], "adamw_" + n) for n in names}
    return (loss, grad_x, *[grads[n] for n in names], *[upd[n][0] for n in names], *[upd[n][1] for n in names],
            *[upd[n][2] for n in names])
```

```python
import functools
import math

import jax
import jax.numpy as jnp
import numpy as np
from jax import lax
from jax.experimental import pallas as pl
from jax.experimental.pallas import tpu as pltpu

F32 = jnp.float32
BF16 = jnp.bfloat16
HIGHEST = lax.Precision.HIGHEST
MESH = pl.DeviceIdType.MESH

CHUNK = 64
NORM_EPS = 1e-6
POOL_WINDOWS = (2, 4, 8, 16)
ADAM_LR, ADAM_B1, ADAM_B2, ADAM_EPS, ADAM_WD, ADAM_STEP = 0.001, 0.9, 0.999, 1e-08, 0.01, 10
N_DEV = 8
LANES = 128
VMEM_LIMIT = 48 * 1024 * 1024

NN = (((1,), (0,)), ((), ()))
NT = (((1,), (1,)), ((), ()))
TN = (((0,), (0,)), ((), ()))


def _call(body, **kw):
    return pl.pallas_call(body, **kw)


def _params(sem=None):
    return pltpu.CompilerParams(dimension_semantics=sem, vmem_limit_bytes=VMEM_LIMIT)


def _tile(n, pref, align):
    for d in range(min(pref, n), 0, -1):
        if n % d == 0 and d % align == 0:
            return d
    return n


def _dot(a, b, dims=NN, precision=None):
    return lax.dot_general(a, b, dims, precision=precision, preferred_element_type=F32)


def _sigmoid(x):
    return 0.5 * jnp.tanh(0.5 * x) + 0.5


def _silu(x):
    return x * _sigmoid(x)


def _softplus(x):
    return jnp.maximum(x, 0.0) + jnp.log(1.0 + jnp.exp(-jnp.abs(x)))


def _rmsnorm(x, w):
    return x * lax.rsqrt(jnp.mean(x * x, axis=-1, keepdims=True) + NORM_EPS) * w


def _shift_down(x, j, row):
    if j == 0:
        return x
    return jnp.where(row >= j, pltpu.roll(x, j, 0), 0.0)


def _shift_up(x, j, row):
    if j == 0:
        return x
    n = x.shape[0]
    return jnp.where(row < n - j, pltpu.roll(x, n - j, 0), 0.0)


def _matmul(a, b, dims, out_dtype, tm, tn, tk, name, col_blocks=None, after=None):
    ta = dims == TN
    tb = dims == NT
    m, kdim = (a.shape[1], a.shape[0]) if ta else a.shape
    n = b.shape[0] if tb else b.shape[1]
    if col_blocks:
        tn = n // col_blocks
    tm, tn, tk = _tile(m, tm, 8), _tile(n, tn, LANES), _tile(kdim, tk, LANES if not ta else 16)
    nk = kdim // tk

    n_extra = 0 if after is None else 1

    def body(a_ref, b_ref, *refs):
        o_ref, scratch = refs[n_extra], refs[n_extra + 1:]
        part = _dot(a_ref[...].astype(BF16), b_ref[...].astype(BF16), dims)
        if nk == 1:
            o_ref[...] = part.astype(o_ref.dtype).reshape(o_ref.shape)
            return
        acc_ref, = scratch
        k = pl.program_id(2)

        @pl.when(k == 0)
        def _():
            acc_ref[...] = part

        @pl.when(k > 0)
        def _():
            acc_ref[...] += part

        @pl.when(k == nk - 1)
        def _():
            o_ref[...] = acc_ref[...].astype(o_ref.dtype).reshape(o_ref.shape)

    a_spec = pl.BlockSpec((tk, tm), lambda i, j, k: (k, i)) if ta else pl.BlockSpec((tm, tk), lambda i, j, k: (i, k))
    b_spec = pl.BlockSpec((tn, tk), lambda i, j, k: (j, k)) if tb else pl.BlockSpec((tk, tn), lambda i, j, k: (k, j))
    if col_blocks:
        out_spec = pl.BlockSpec((1, tm, tn), lambda i, j, k: (j, i, 0))
        out_shape = jax.ShapeDtypeStruct((col_blocks, m, tn), out_dtype)
    else:
        out_spec = pl.BlockSpec((tm, tn), lambda i, j, k: (i, j))
        out_shape = jax.ShapeDtypeStruct((m, n), out_dtype)
    return _call(
        body, name=name, grid=(m // tm, n // tn, nk),
        in_specs=[a_spec, b_spec] + [ANY] * n_extra, out_specs=out_spec, out_shape=out_shape,
        scratch_shapes=[] if nk == 1 else [pltpu.VMEM((tm, tn), F32)],
        compiler_params=_params(("parallel", "parallel", "arbitrary")),
    )(a, b, *([] if after is None else [after]))


def _norm_in_fwd(x2d, meta, w, pad):
    seq, d = x2d.shape
    x0 = pad + meta.shape[0]
    assert x0 % 16 == 0
    lp = x0 + seq
    tr = _tile(seq, 512, 16)
    vec = pl.BlockSpec((1, d), lambda i: (0, 0))
    shapes = [jax.ShapeDtypeStruct((lp, d), F32), jax.ShapeDtypeStruct((lp, d), BF16)]

    def body(x_ref, w_ref, h_ref, o_ref):
        h_ref[...] = x_ref[...]
        o_ref[...] = _rmsnorm(x_ref[...], w_ref[...]).astype(BF16)

    def head(m_ref, w_ref, h_in_ref, o_in_ref, h_ref, o_ref):
        h = jnp.concatenate([jnp.zeros((pad, d), F32), m_ref[...]], axis=0) if pad else m_ref[...]
        h_ref[...] = h
        o_ref[...] = _rmsnorm(h, w_ref[...]).astype(BF16)

    rows = _rows_after(x0, tr, d)
    h0, xn = _call(
        body, name="norm_in_fwd", grid=(seq // tr,), in_specs=[pl.BlockSpec((tr, d), lambda i: (i, 0)), vec],
        out_specs=[rows, rows], out_shape=shapes, compiler_params=_params(("parallel",)),
    )(x2d, w)
    first = pl.BlockSpec((x0, d), lambda i: (0, 0))
    return _call(
        head, name="norm_in_fwd_head", grid=(1,), in_specs=[pl.BlockSpec(meta.shape, lambda i: (0, 0)), vec, ANY, ANY],
        out_specs=[first, first], out_shape=shapes, input_output_aliases={2: 0, 3: 1}, compiler_params=_params(("arbitrary",)),
    )(meta, w, h0, xn)


def _rows_after(x0, tr, d):
    step = math.gcd(x0, tr)
    return pl.BlockSpec((pl.Element(tr), pl.Element(d)), lambda i: (pl.multiple_of(x0 + tr * i, step), 0))


def _norm_in_bwd(h0, w, dxn, dh1, x0):
    lp, d = h0.shape
    tr = _tile(lp - x0, 512, 8)
    vec = pl.BlockSpec((1, d), lambda i: (0, 0))

    def make(body_rows, first):
        def body(h_ref, w_ref, da_ref, dh1_ref, dh_ref, dw_ref):
            _, vjp = jax.vjp(_rmsnorm, h_ref[...], w_ref[...])
            dh, dw = vjp(da_ref[...])
            dh_ref[...] = dh + dh1_ref[...]

            @pl.when(pl.program_id(0) == 0)
            def _():
                dw_ref[...] = jnp.zeros_like(dw_ref)

            dw_ref[...] += dw

        rows_in = pl.BlockSpec((x0, d), lambda i: (0, 0)) if first else _rows_after(x0, tr, d)
        return _call(
            body, name="norm_in_bwd_head" if first else "norm_in_bwd", grid=(1 if first else (lp - x0) // tr,),
            in_specs=[rows_in, vec, rows_in, rows_in],
            out_specs=[pl.BlockSpec((body_rows, d), lambda i: (i, 0)), vec],
            out_shape=[jax.ShapeDtypeStruct((x0 if first else lp - x0, d), F32), jax.ShapeDtypeStruct((1, d), F32)],
            compiler_params=_params(("arbitrary",)),
        )(h0, w, dxn, dh1)

    d_head, dw_head = make(x0, True)
    grad_x, dw_rest = make(tr, False)
    return d_head, grad_x, dw_head + dw_rest


def _final_loss(h0, mo, fw, tgt, x0):
    lp, d = h0.shape
    tr = _tile(lp - x0, 512, 8)

    def body(h_ref, mo_ref, fw_ref, t_ref, dh_ref, dw_ref, loss_ref):
        tgt_v = t_ref[...]

        def loss_fn(h1, w):
            err = _rmsnorm(h1, w) - tgt_v
            return 0.5 * jnp.sum(jnp.mean(err * err, axis=-1, keepdims=True), axis=0, keepdims=True)

        loss, vjp = jax.vjp(loss_fn, h_ref[...] + mo_ref[...], fw_ref[...])
        dh, dw = vjp(jnp.ones((1, 1), F32))
        dh_ref[...] = dh

        @pl.when(pl.program_id(0) == 0)
        def _():
            dw_ref[...] = jnp.zeros_like(dw_ref)
            loss_ref[...] = jnp.zeros_like(loss_ref)

        dw_ref[...] += dw
        loss_ref[...] += jnp.broadcast_to(loss, loss_ref.shape)

    def zero_head(dh_in_ref, dh_ref):
        dh_ref[...] = jnp.zeros_like(dh_ref)

    rows = _rows_after(x0, tr, d)
    vec = pl.BlockSpec((1, d), lambda i: (0, 0))
    dh1, dw, loss = _call(
        body, name="final_loss", grid=((lp - x0) // tr,),
        in_specs=[rows, rows, vec, pl.BlockSpec((tr, d), lambda i: (i, 0))],
        out_specs=[rows, vec, pl.BlockSpec((8, LANES), lambda i: (0, 0))],
        out_shape=[jax.ShapeDtypeStruct((lp, d), F32), jax.ShapeDtypeStruct((1, d), F32), jax.ShapeDtypeStruct((8, LANES), F32)],
        compiler_params=_params(("arbitrary",)),
    )(h0, mo, fw, tgt)
    dh1 = _call(
        zero_head, name="final_loss_head", grid=(1,), in_specs=[ANY], out_specs=pl.BlockSpec((x0, d), lambda i: (0, 0)),
        out_shape=jax.ShapeDtypeStruct((lp, d), F32), input_output_aliases={0: 0}, compiler_params=_params(("arbitrary",)),
    )(dh1)
    return dh1, dw, loss


def _pool_select(parts, g):
    out = parts[-1]
    for gi in range(len(parts) - 2, -1, -1):
        out = jnp.where(g == gi, parts[gi], out)
    return out


def _pool_count(row, g, pad):
    win = _pool_select([jnp.full(row.shape, float(w), F32) for w in POOL_WINDOWS], g)
    return jnp.maximum(jnp.minimum((row - pad + 1).astype(F32), win), 1.0)


def _pooled(u, g, row, pad):
    sums, s, span = [], u, 1
    for w in POOL_WINDOWS:
        while span < w:
            s = s + _shift_down(s, span, row)
            span *= 2
        sums.append(s)
    return _pool_select(sums, g) / _pool_count(row, g, pad) - u


def _pooled_adjoint(dp, g, row, pad):
    e = dp / _pool_count(row, g, pad)
    sums, s, span = [], e, 1
    for w in POOL_WINDOWS:
        while span < w:
            s = s + _shift_up(s, span, row)
            span *= 2
        sums.append(s)
    return _pool_select(sums, g) - dp


def _pool_specs(lp, pg, ng, z_off):
    u_spec = pl.BlockSpec((lp, pg), lambda g: (0, g))
    z_spec = pl.BlockSpec((lp, pg), lambda g: (0, z_off + g))
    mix_spec = pl.BlockSpec((1, pg, pg), lambda g: (g, 0, 0))
    vec_spec = pl.BlockSpec((1, pg), lambda g: (0, g))
    return u_spec, z_spec, mix_spec, vec_spec


def _pool_fwd(proj, mix, scale, pad):
    lp = proj.shape[0]
    ng, pg, _ = mix.shape
    pw = ng * pg

    def body(u_ref, z_ref, mix_ref, sc_ref, y_ref):
        g = pl.program_id(0)
        row = lax.broadcasted_iota(jnp.int32, (lp, 1), 0)
        pooled = _pooled(u_ref[...], g, row, pad)
        mixed = _dot(pooled.astype(BF16), mix_ref[0])
        y_ref[...] = (mixed * sc_ref[...] * _silu(z_ref[...])).astype(BF16)

    u_spec, z_spec, mix_spec, vec_spec = _pool_specs(lp, pg, ng, pw // pg)
    return _call(
        body, name="pool_fwd", grid=(ng,), in_specs=[u_spec, z_spec, mix_spec, vec_spec], out_specs=u_spec,
        out_shape=jax.ShapeDtypeStruct((lp, pw), BF16), compiler_params=_params(("parallel",)),
    )(proj, proj, mix, scale)


def _pool_bwd(proj, mix, scale, dy, pad):
    lp = proj.shape[0]
    ng, pg, _ = mix.shape
    pw = ng * pg

    def body(u_ref, z_ref, mix_ref, sc_ref, dy_ref, du_ref, dz_ref, dmix_ref, dsc_ref):
        g = pl.program_id(0)
        row = lax.broadcasted_iota(jnp.int32, (lp, 1), 0)
        real = row >= pad
        z = z_ref[...]
        pooled = _pooled(u_ref[...], g, row, pad).astype(BF16)
        mixed = _dot(pooled, mix_ref[0])
        sig = _sigmoid(z)
        sz = z * sig
        dyv = dy_ref[...]
        dsc_ref[...] = jnp.sum(dyv * mixed * sz, axis=0, keepdims=True)
        d_sz = dyv * mixed * sc_ref[...]
        dz_ref[...] = jnp.where(real, d_sz * (sig + sz * (1.0 - sig)), 0.0).astype(BF16)
        d_mixed = (dyv * sc_ref[...] * sz).astype(BF16)
        dmix_ref[0] = _dot(pooled, d_mixed, TN)
        d_pooled = _dot(d_mixed, mix_ref[0], NT)
        du_ref[...] = jnp.where(real, _pooled_adjoint(d_pooled, g, row, pad), 0.0).astype(BF16)

    u_spec, z_spec, mix_spec, vec_spec = _pool_specs(lp, pg, ng, pw // pg)
    return _call(
        body, name="pool_bwd", grid=(ng,),
        in_specs=[u_spec, z_spec, mix_spec, vec_spec, u_spec], out_specs=[u_spec, u_spec, mix_spec, vec_spec],
        out_shape=[jax.ShapeDtypeStruct((lp, pw), BF16), jax.ShapeDtypeStruct((lp, pw), BF16),
                   jax.ShapeDtypeStruct((ng, pg, pg), F32), jax.ShapeDtypeStruct((1, pw), F32)],
        compiler_params=_params(("parallel",)),
    )(proj, proj, mix, scale, dy)


def _conv_pre(x, w, row):
    kw = w.shape[0]
    y = w[kw - 1:kw, :] * x
    for kk in range(kw - 1):
        y = y + w[kk:kk + 1, :] * _shift_down(x, kw - 1 - kk, row)
    return y


def _conv_post(y, out_scale):
    s = _silu(y)
    if out_scale is None:
        return s
    return s * lax.rsqrt(jnp.sum(s * s, axis=-1, keepdims=True) + NORM_EPS) * out_scale


def _conv_fwd(proj, col_off, w, hd, out_scale, name):
    lp = proj.shape[0]
    kw, width = w.shape
    blk0 = col_off // hd

    def body(x_ref, w_ref, o_ref):
        row = lax.broadcasted_iota(jnp.int32, (lp, 1), 0)
        o_ref[...] = _conv_post(_conv_pre(x_ref[...], w_ref[...], row), out_scale)

    return _call(
        body, name=name, grid=(width // hd,),
        in_specs=[pl.BlockSpec((lp, hd), lambda j: (0, blk0 + j)), pl.BlockSpec((kw, hd), lambda j: (0, j))],
        out_specs=pl.BlockSpec((lp, hd), lambda j: (0, j)),
        out_shape=jax.ShapeDtypeStruct((lp, width), F32), compiler_params=_params(("parallel",)),
    )(proj, w)


def _conv_bwd(proj, col_off, w, d_out, hd, out_scale, pad, name, dst):
    lp = proj.shape[0]
    kw, width = w.shape
    blk0 = col_off // hd

    def body(x_ref, w_ref, do_ref, dst_ref, dx_ref, dw_ref):
        row = lax.broadcasted_iota(jnp.int32, (lp, 1), 0)
        real = row >= pad
        x, wv = x_ref[...], w_ref[...]
        _, vjp = jax.vjp(functools.partial(_conv_post, out_scale=out_scale), _conv_pre(x, wv, row))
        dy = jnp.where(real, vjp(do_ref[...])[0], 0.0)
        dx = wv[kw - 1:kw, :] * dy
        dw_ref[kw - 1:kw, :] = jnp.sum(dy * x, axis=0, keepdims=True)
        for kk in range(kw - 1):
            ahead = _shift_up(dy, kw - 1 - kk, row)
            dx = dx + wv[kk:kk + 1, :] * ahead
            dw_ref[kk:kk + 1, :] = jnp.sum(ahead * x, axis=0, keepdims=True)
        dx_ref[...] = jnp.where(real, dx, 0.0).astype(BF16)

    col = pl.BlockSpec((lp, hd), lambda j: (0, j))
    at_off = pl.BlockSpec((lp, hd), lambda j: (0, blk0 + j))
    wspec = pl.BlockSpec((kw, hd), lambda j: (0, j))
    return _call(
        body, name=name, grid=(width // hd,),
        in_specs=[at_off, wspec, col, ANY], out_specs=[at_off, wspec],
        out_shape=[jax.ShapeDtypeStruct(dst.shape, BF16), jax.ShapeDtypeStruct((kw, width), F32)],
        input_output_aliases={3: 0}, compiler_params=_params(("parallel",)),
    )(proj, w, d_out, dst)


HEADS_PER_STEP = 16
HEADS_PER_STEP_BWD = 16


def _matmul_with_direct_vjp(dims, da_dims, db_dims, db_swapped):
    @jax.custom_vjp
    def mm(a, b):
        return _dot(a, b, dims)

    def fwd(a, b):
        return _dot(a, b, dims), (a, b)

    def bwd(res, g):
        a, b = res
        return _dot(g, b, da_dims) if not db_swapped[0] else _dot(b, g, da_dims), _dot(a, g, db_dims) if not db_swapped[1] else _dot(g, a, db_dims)

    mm.defvjp(fwd, bwd)
    return mm


_mm_nn = _matmul_with_direct_vjp(NN, NT, TN, (False, False))
_mm_nt = _matmul_with_direct_vjp(NT, NN, TN, (False, True))
_mm_tn = _matmul_with_direct_vjp(TN, NT, NN, (True, False))


def _each(fn, *lists):
    return [fn(*args) for args in zip(*lists)]


def _dot3_each(a_list, b_list, dims=NN):
    hi = lambda t: t.astype(BF16)
    lo = lambda t, t_hi: (t - t_hi.astype(F32)).astype(BF16)
    dot = lambda x, y: _dot(x, y, dims)
    a_hi, b_hi = _each(hi, a_list), _each(hi, b_list)
    a_lo, b_lo = _each(lo, a_list, a_hi), _each(lo, b_list, b_hi)
    hh, hl, lh = _each(dot, a_hi, b_hi), _each(dot, a_hi, b_lo), _each(dot, a_lo, b_hi)
    return _each(lambda x, y, w: x + (y + w), hh, hl, lh)


@jax.custom_vjp
def _unit_lower_inverse(lmats):
    c = lmats[0].shape[0]
    eye = lax.broadcasted_iota(jnp.int32, (c, c), 0) == lax.broadcasted_iota(jnp.int32, (c, c), 1)
    a = [-m for m in lmats]
    tmat = [jnp.where(eye, 1.0, 0.0).astype(F32) + m for m in a]
    span = 2
    while span < c:
        a = _dot3_each(a, a)
        tmat = _each(lambda t, u: t + u, tmat, _dot3_each(tmat, a))
        span *= 2
    return tuple(tmat)


def _unit_lower_inverse_fwd(lmats):
    tmats = _unit_lower_inverse(lmats)
    return tmats, tmats


def _unit_lower_inverse_bwd(tmats, cts):
    left = _each(lambda t, ct: _dot(t, ct, TN, HIGHEST), tmats, cts)
    return (tuple(_each(lambda m, t: -_dot(m, t, NT, HIGHEST), left, tmats)),)


_unit_lower_inverse.defvjp(_unit_lower_inverse_fwd, _unit_lower_inverse_bwd)


@jax.custom_vjp
def _known_inverse(lmats, tmats):
    return tmats


def _known_inverse_fwd(lmats, tmats):
    return tmats, tmats


def _known_inverse_bwd(tmats, cts):
    return _unit_lower_inverse_bwd(tmats, cts)[0], tuple(jnp.zeros_like(t) for t in tmats)


_known_inverse.defvjp(_known_inverse_fwd, _known_inverse_bwd)


def _chunk_math(states, q, k, v, ba, z, prm, nw, head0, rowmask, n_heads, tmats=None, keep_tmats=False):
    c = q.shape[0]
    heads = list(range(len(states)))
    hd = q.shape[1] // len(states)
    lane = lax.broadcasted_iota(jnp.int32, ba.shape, 1)
    sub = lax.broadcasted_iota(jnp.int32, (ba.shape[1], c), 0)
    ri = lax.broadcasted_iota(jnp.int32, (c, c), 0)
    ci = lax.broadcasted_iota(jnp.int32, (c, c), 1)
    last = lax.broadcasted_iota(jnp.int32, (c, 1), 0) == c - 1
    causal, strict = ri >= ci, ri > ci
    beta_all = _sigmoid(ba) * rowmask
    g_all = -jnp.exp(prm[0:1, :]) * _softplus(ba + prm[1:2, :]) * rowmask
    gcum_all = _dot(jnp.where(causal, 1.0, 0.0).astype(F32), g_all, precision=HIGHEST)
    gcum_t = gcum_all.T
    split = lambda t: [t[:, j * hd:(j + 1) * hd] for j in heads]
    qs, ks, vs, zs = split(q), split(k), split(v), split(z)
    beta = [jnp.sum(jnp.where(lane == head0 + j, beta_all, 0.0), axis=1, keepdims=True) for j in heads]
    gcum = [jnp.sum(jnp.where(lane == n_heads + head0 + j, gcum_all, 0.0), axis=1, keepdims=True) for j in heads]
    grow = [jnp.sum(jnp.where(sub == n_heads + head0 + j, gcum_t, 0.0), axis=0, keepdims=True) for j in heads]
    glast = _each(lambda gc: jnp.sum(jnp.where(last, gc, 0.0), axis=0, keepdims=True), gcum)
    decay = _each(lambda gc, gr: jnp.where(causal, jnp.exp(jnp.where(causal, gc - gr, 0.0)), 0.0), gcum, grow)
    eg = _each(jnp.exp, gcum)
    k_beta = _each(jnp.multiply, ks, beta)
    kk = _each(_mm_nt, k_beta, ks)
    lmats = tuple(_each(lambda m, dc: jnp.where(strict, m * dc, 0.0), kk, decay))
    tmat = list(_unit_lower_inverse(lmats) if tmats is None else _known_inverse(lmats, tuple(tmats)))
    uw = _each(lambda t, vj, b, kb, e: _mm_nn(t, jnp.concatenate([vj * b, kb * e], axis=1)), tmat, vs, beta, k_beta, eg)
    u_c, w_c = _each(lambda m: m[:, :hd], uw), _each(lambda m: m[:, hd:], uw)
    qk = _each(lambda a, b, dc: jnp.where(causal, _mm_nt(a, b) * dc, 0.0), qs, ks, decay)
    from_state = _each(lambda w, a, e, s: _mm_nn(jnp.concatenate([w, a * e], axis=0), s), w_c, qs, eg, list(states))
    v_new = _each(lambda u, m: u - m[:c], u_c, from_state)
    o = _each(lambda m, a, vn: m[c:] + _mm_nn(a, vn), from_state, qk, v_new)
    k_dec = _each(lambda a, gl, gc: a * jnp.exp(gl - gc), ks, glast, gcum)
    new_states = _each(lambda s, gl, kd, vn: s * jnp.exp(gl) + _mm_tn(kd, vn), list(states), glast, k_dec, v_new)
    ys = _each(lambda oj, zj: _rmsnorm(oj, nw) * _silu(zj), o, zs)
    if keep_tmats:
        return jnp.concatenate(ys, axis=1), tuple(new_states), tuple(tmat)
    return jnp.concatenate(ys, axis=1), tuple(new_states)


def _chunk_specs(nc, hd, n_heads, z_off, ba_off, rev):
    cidx = (lambda c: nc - 1 - c) if rev else (lambda c: c)
    hb = min(HEADS_PER_STEP_BWD if rev else HEADS_PER_STEP, n_heads)
    assert n_heads % hb == 0 and z_off % (hb * hd) == 0 and ba_off % LANES == 0
    blk = lambda off: pl.BlockSpec((CHUNK, hb * hd), lambda c, g: (cidx(c), off + g))
    ba_spec = lambda off: pl.BlockSpec((CHUNK, LANES), lambda c, g: (cidx(c), off // LANES))
    prm_spec = pl.BlockSpec((8, LANES), lambda c, g: (0, 0))
    nw_spec = pl.BlockSpec((1, hd), lambda c, g: (0, 0))
    st_spec = pl.BlockSpec((1, hb, hd, hd), lambda c, g: (cidx(c), g, 0, 0))
    return blk, ba_spec, prm_spec, nw_spec, st_spec, blk(z_off // (hb * hd))


def _rowmask(chunk_idx, pad):
    row = chunk_idx * CHUNK + lax.broadcasted_iota(jnp.int32, (CHUNK, 1), 0)
    return jnp.where(row >= pad, 1.0, 0.0).astype(F32)


def _chunk_fwd(qn, kn, vv, proj, z_off, ba_off, prm, nw, n_heads, pad):
    lp, dn = qn.shape
    hd = dn // n_heads
    nc = lp // CHUNK
    hb = min(HEADS_PER_STEP, n_heads)

    def body(q_ref, k_ref, v_ref, ba_ref, z_ref, prm_ref, nw_ref, y_ref, hist_ref, tm_ref, st_ref):
        c, g = pl.program_id(0), pl.program_id(1)

        @pl.when(c == 0)
        def _():
            for j in range(hb):
                st_ref[g * hb + j] = jnp.zeros((hd, hd), F32)

        states = tuple(st_ref[g * hb + j] for j in range(hb))
        for j in range(hb):
            hist_ref[0, j] = states[j]
        y, new_states, tmats = _chunk_math(states, q_ref[...], k_ref[...], v_ref[...], ba_ref[...], z_ref[...], prm_ref[...],
                                           nw_ref[...], g * hb, _rowmask(c, pad), n_heads, keep_tmats=True)
        y_ref[...] = y.astype(BF16)
        for j in range(hb):
            st_ref[g * hb + j] = new_states[j]
            tm_ref[0, j] = tmats[j]

    blk, ba_spec, prm_spec, nw_spec, st_spec, z_spec = _chunk_specs(nc, hd, n_heads, z_off, ba_off, False)
    tm_spec = pl.BlockSpec((1, hb, CHUNK, CHUNK), lambda c, g: (c, g, 0, 0))
    return _call(
        body, name="chunk_fwd", grid=(nc, n_heads // hb),
        in_specs=[blk(0), blk(0), blk(0), ba_spec(ba_off), z_spec, prm_spec, nw_spec], out_specs=[blk(0), st_spec, tm_spec],
        out_shape=[jax.ShapeDtypeStruct((lp, dn), BF16), jax.ShapeDtypeStruct((nc, n_heads, hd, hd), F32),
                   jax.ShapeDtypeStruct((nc, n_heads, CHUNK, CHUNK), F32)],
        scratch_shapes=[pltpu.VMEM((n_heads, hd, hd), F32)],
        compiler_params=_params(("arbitrary", "arbitrary")),
    )(qn, kn, vv, proj, proj, prm, nw)


def _chunk_bwd(qn, kn, vv, proj, z_off, ba_off, prm, nw, hist, tmats, dy, n_heads, pad, d_proj):
    lp, dn = qn.shape
    hd = dn // n_heads
    nc = lp // CHUNK
    hb = min(HEADS_PER_STEP_BWD, n_heads)

    def body(q_ref, k_ref, v_ref, ba_ref, z_ref, prm_ref, nw_ref, hist_ref, tm_ref, dy_ref, d_proj_ref,
             dq_ref, dk_ref, dv_ref, dba_ref, dz_ref, dprm_ref, dnw_ref, dst_ref):
        step, g = pl.program_id(0), pl.program_id(1)

        @pl.when(step == 0)
        def _():
            for j in range(hb):
                dst_ref[g * hb + j] = jnp.zeros((hd, hd), F32)

        @pl.when((step == 0) & (g == 0))
        def _():
            dprm_ref[...] = jnp.zeros_like(dprm_ref)
            dnw_ref[...] = jnp.zeros_like(dnw_ref)

        @pl.when(g == 0)
        def _():
            dba_ref[...] = jnp.zeros_like(dba_ref)

        def fn(states, q, k, v, ba, z, prm_v, nw_v, known):
            return _chunk_math(states, q, k, v, ba, z, prm_v, nw_v, g * hb, _rowmask(nc - 1 - step, pad), n_heads, tmats=known)

        states = tuple(hist_ref[0, j] for j in range(hb))
        known = tuple(tm_ref[0, j] for j in range(hb))
        _, vjp = jax.vjp(fn, states, q_ref[...], k_ref[...], v_ref[...], ba_ref[...], z_ref[...], prm_ref[...], nw_ref[...], known)
        dst, dq, dk, dv, dba, dz, dprm, dnw, _ = vjp((dy_ref[...], tuple(dst_ref[g * hb + j] for j in range(hb))))
        for j in range(hb):
            dst_ref[g * hb + j] = dst[j]
        dq_ref[...] = dq
        dk_ref[...] = dk
        dv_ref[...] = dv
        dz_ref[...] = dz.astype(BF16)
        dba_ref[...] += dba
        dprm_ref[...] += dprm
        dnw_ref[...] += dnw

    blk, ba_spec, prm_spec, nw_spec, st_spec, z_spec = _chunk_specs(nc, hd, n_heads, z_off, ba_off, True)
    f32_full = jax.ShapeDtypeStruct((lp, dn), F32)
    tm_spec = pl.BlockSpec((1, hb, CHUNK, CHUNK), lambda c, g: (nc - 1 - c, g, 0, 0))
    return _call(
        body, name="chunk_bwd", grid=(nc, n_heads // hb),
        in_specs=[blk(0), blk(0), blk(0), ba_spec(ba_off), z_spec, prm_spec, nw_spec, st_spec, tm_spec, blk(0), ANY],
        out_specs=[blk(0), blk(0), blk(0), ba_spec(0), z_spec, prm_spec, nw_spec],
        out_shape=[f32_full, f32_full, f32_full, jax.ShapeDtypeStruct((lp, LANES), F32), jax.ShapeDtypeStruct(d_proj.shape, BF16),
                   jax.ShapeDtypeStruct((8, LANES), F32), jax.ShapeDtypeStruct((1, hd), F32)],
        scratch_shapes=[pltpu.VMEM((n_heads, hd, hd), F32)],
        input_output_aliases={10: 4}, compiler_params=_params(("arbitrary", "arbitrary")),
    )(qn, kn, vv, proj, proj, prm, nw, hist, tmats, dy, d_proj)


def _merge_math(p, q, gp, gd):
    return _sigmoid(gp) * p + _sigmoid(gd) * q


def _merge_specs(lp, d, gp_off, gd_off):
    tr, tc = _tile(lp, 176, 16), _tile(d, 2048, LANES)
    blk = pl.BlockSpec((tr, tc), lambda i, j: (i, j))
    gp_spec = pl.BlockSpec((tr, tc), lambda i, j: (i, gp_off // tc + j))
    gd_spec = pl.BlockSpec((tr, tc), lambda i, j: (i, gd_off // tc + j))
    return (lp // tr, d // tc), blk, gp_spec, gd_spec


def _merge_fwd(p, q, proj, gp_off, gd_off):
    lp, d = p.shape
    grid, blk, gp_spec, gd_spec = _merge_specs(lp, d, gp_off, gd_off)

    def body(p_ref, q_ref, gp_ref, gd_ref, o_ref):
        o_ref[...] = _merge_math(p_ref[...], q_ref[...], gp_ref[...], gd_ref[...]).astype(BF16)

    return _call(
        body, name="merge_fwd", grid=grid, in_specs=[blk, blk, gp_spec, gd_spec], out_specs=blk,
        out_shape=jax.ShapeDtypeStruct((lp, d), BF16), compiler_params=_params(("parallel", "parallel")),
    )(p, q, proj, proj)


def _merge_bwd(p, q, proj, gp_off, gd_off, dm):
    lp, d = p.shape
    grid, blk, gp_spec, gd_spec = _merge_specs(lp, d, gp_off, gd_off)

    def body(p_ref, q_ref, gp_ref, gd_ref, dm_ref, dp_ref, dq_ref, dgp_ref, dgd_ref):
        _, vjp = jax.vjp(_merge_math, p_ref[...], q_ref[...], gp_ref[...], gd_ref[...])
        for ref, val in zip((dp_ref, dq_ref, dgp_ref, dgd_ref), vjp(dm_ref[...])):
            ref[...] = val.astype(BF16)

    out = jax.ShapeDtypeStruct((lp, d), BF16)
    return _call(
        body, name="merge_bwd", grid=grid, in_specs=[blk, blk, gp_spec, gd_spec, blk], out_specs=[blk] * 4,
        out_shape=[out] * 4, compiler_params=_params(("parallel", "parallel")),
    )(p, q, proj, proj, dm)


def _adamw(w, g, m, v, name):
    shape = w.shape
    w2, g2, m2, v2 = (t.reshape((-1, shape[-1])) for t in (w, g, m, v))
    rows, cols = w2.shape
    tr = _tile(rows, 256, 8)

    def body(w_ref, g_ref, m_ref, v_ref, d_ref, nm_ref, nv_ref):
        gv = g_ref[...]
        nm = ADAM_B1 * m_ref[...] + (1.0 - ADAM_B1) * gv
        nv = ADAM_B2 * v_ref[...] + (1.0 - ADAM_B2) * (gv * gv)
        m_hat = nm / (1.0 - ADAM_B1 ** ADAM_STEP)
        v_hat = nv / (1.0 - ADAM_B2 ** ADAM_STEP)
        d_ref[...] = -ADAM_LR * (m_hat / (jnp.sqrt(v_hat) + ADAM_EPS) + ADAM_WD * w_ref[...])
        nm_ref[...] = nm
        nv_ref[...] = nv

    blk = pl.BlockSpec((tr, cols), lambda i: (i, 0))
    out = jax.ShapeDtypeStruct((rows, cols), F32)
    res = _call(
        body, name=name, grid=(rows // tr,), in_specs=[blk] * 4, out_specs=[blk] * 3, out_shape=[out] * 3,
        compiler_params=_params(("parallel",)),
    )(w2, g2, m2, v2)
    return tuple(t.reshape(shape) for t in res)


def _coords():
    return lax.axis_index("x"), lax.axis_index("y"), lax.axis_index("c")


def _flip(v, bit):
    return 1 - v if bit else v


CHIP_FLIPS = ((1, 0), (0, 1), (1, 1))
ANY = pl.BlockSpec(memory_space=pl.ANY)


def _all_gather(shards):
    n = len(shards)

    def body(*refs):
        x_refs, out_refs = refs[:n], refs[n:2 * n]
        send_sems, recv_sems, local_sems = refs[2 * n:]
        x, y, c = _coords()
        sibling = (x, y, 1 - c)
        chips = [(_flip(x, fx), _flip(y, fy)) for fx, fy in CHIP_FLIPS]

        def copy(a, k, block, to, from_input=False):
            px, py, pc = block
            slot = out_refs[a].at[4 * px + 2 * py + pc]
            return pltpu.make_async_remote_copy(
                src_ref=x_refs[a] if from_input else slot, dst_ref=slot,
                send_sem=send_sems.at[7 * a + k], recv_sem=recv_sems.at[7 * a + k], device_id=to, device_id_type=MESH)

        mine = [pltpu.make_async_copy(x_refs[a], out_refs[a].at[4 * x + 2 * y + c], local_sems.at[a]) for a in range(n)]
        first = []
        for a in range(n):
            mine[a].start()
            first.append(copy(a, 0, (x, y, c), sibling, True))
            first += [copy(a, 1 + j, (x, y, c), (*chip, c), True) for j, chip in enumerate(chips)]
        for cp in first:
            cp.start()
        passed = []
        for j, chip in enumerate(chips):
            for a in range(n):
                copy(a, 1 + j, (*chip, c), (x, y, c)).wait_recv()
                passed.append(copy(a, 4 + j, (*chip, c), sibling))
                passed[-1].start()
        for a in range(n):
            copy(a, 0, (x, y, 1 - c), (x, y, c)).wait_recv()
            for j, chip in enumerate(chips):
                copy(a, 4 + j, (*chip, 1 - c), (x, y, c)).wait_recv()
        for cp in first + passed:
            cp.wait_send()
        for cp in mine:
            cp.wait()

    return _call(
        body, name="all_gather", in_specs=[ANY] * n, out_specs=[ANY] * n,
        out_shape=[jax.ShapeDtypeStruct((N_DEV,) + s.shape, s.dtype) for s in shards],
        scratch_shapes=[pltpu.SemaphoreType.DMA((7 * n,)), pltpu.SemaphoreType.DMA((7 * n,)), pltpu.SemaphoreType.DMA((n,))],
    )(*shards)


def _all_gather_tree(shard, after):
    rows, cols = shard.shape
    half = rows // 2
    assert rows % 32 == 0

    def body(x_ref, after_ref, out_ref, send_sems, recv_sems, local_sem):
        x, y, c = _coords()
        me, sibling = (x, y, c), (x, y, 1 - c)
        x_nbr, y_nbr, diag = (1 - x, y), (x, 1 - y), (1 - x, 1 - y)

        def part(ref, h):
            return ref if h is None else ref.at[pl.ds(h * half, half)]

        def copy(k, block, to, h=None, from_input=False):
            px, py, pc = block
            slot = part(out_ref.at[4 * px + 2 * py + pc], h)
            return pltpu.make_async_remote_copy(
                src_ref=part(x_ref, h) if from_input else slot, dst_ref=slot,
                send_sem=send_sems.at[k], recv_sem=recv_sems.at[k], device_id=to, device_id_type=MESH)

        mine = pltpu.make_async_copy(x_ref, out_ref.at[4 * x + 2 * y + c], local_sem)
        mine.start()
        started = [copy(0, me, sibling, None, True),
                   copy(1, me, (*x_nbr, c), 0, True), copy(2, me, (*x_nbr, c), 1, True),
                   copy(4, me, (*y_nbr, c), 1, True), copy(3, me, (*y_nbr, c), 0, True)]
        for cp in started:
            cp.start()
        copy(1, (*x_nbr, c), me, 0).wait_recv()
        started.append(copy(5, (*x_nbr, c), (*y_nbr, c), 0))
        started[-1].start()
        copy(4, (*y_nbr, c), me, 1).wait_recv()
        started.append(copy(6, (*y_nbr, c), (*x_nbr, c), 1))
        started[-1].start()
        copy(2, (*x_nbr, c), me, 1).wait_recv()
        started.append(copy(7, (*x_nbr, c), sibling))
        started[-1].start()
        copy(3, (*y_nbr, c), me, 0).wait_recv()
        started.append(copy(8, (*y_nbr, c), sibling))
        started[-1].start()
        copy(5, (*diag, c), me, 0).wait_recv()
        copy(6, (*diag, c), me, 1).wait_recv()
        started.append(copy(9, (*diag, c), sibling))
        started[-1].start()
        copy(0, sibling, me).wait_recv()
        for k, chip in ((7, x_nbr), (8, y_nbr), (9, diag)):
            copy(k, (*chip, 1 - c), me).wait_recv()
        for cp in started:
            cp.wait_send()
        mine.wait()

    return _call(
        body, name="all_gather_tree", in_specs=[ANY, ANY], out_specs=ANY,
        out_shape=jax.ShapeDtypeStruct((N_DEV, rows, cols), shard.dtype),
        scratch_shapes=[pltpu.SemaphoreType.DMA((10,)), pltpu.SemaphoreType.DMA((10,)), pltpu.SemaphoreType.DMA],
    )(shard, after)


def _rs_to_sibling(gs, name):
    n = len(gs)

    def body(*refs):
        g_refs, got_refs = refs[:n], refs[n:2 * n]
        send_sems, recv_sems = refs[2 * n:]
        x, y, c = _coords()
        copies = []
        for a in range(n):
            for p in range(4):
                cp = pltpu.make_async_remote_copy(
                    src_ref=g_refs[a].at[2 * p + (1 - c)], dst_ref=got_refs[a].at[p], send_sem=send_sems.at[4 * a + p],
                    recv_sem=recv_sems.at[4 * a + p], device_id=(x, y, 1 - c), device_id_type=MESH)
                cp.start()
                copies.append(cp)
        for cp in copies:
            cp.wait()

    return _call(
        body, name=name, in_specs=[ANY] * n, out_specs=[ANY] * n,
        out_shape=[jax.ShapeDtypeStruct((4,) + g.shape[1:], g.dtype) for g in gs],
        scratch_shapes=[pltpu.SemaphoreType.DMA((4 * n,)), pltpu.SemaphoreType.DMA((4 * n,))],
    )(*gs)


def _rs_pair_sum(g, got, c_idx, name):
    _, rows, cols = g.shape
    tr = _tile(rows, 512, 16)

    def body(c_ref, g_ref, got_ref, o_ref):
        o_ref[...] = (g_ref[...].astype(F32) + got_ref[...].astype(F32)).astype(o_ref.dtype)

    grid_spec = pltpu.PrefetchScalarGridSpec(
        num_scalar_prefetch=1, grid=(4, rows // tr),
        in_specs=[pl.BlockSpec((1, tr, cols), lambda p, i, c_ref: (2 * p + c_ref[0], i, 0)),
                  pl.BlockSpec((1, tr, cols), lambda p, i, c_ref: (p, i, 0))],
        out_specs=pl.BlockSpec((1, tr, cols), lambda p, i, c_ref: (p, i, 0)))
    return _call(
        body, name=name, grid_spec=grid_spec, out_shape=jax.ShapeDtypeStruct((4, rows, cols), g.dtype),
        compiler_params=_params(("parallel", "parallel")),
    )(c_idx, g, got)


def _to_chips_copies(p_refs, got_refs, send_sems, recv_sems):
    x, y, c = _coords()
    copies = []
    for a in range(len(p_refs)):
        for k, (fx, fy) in enumerate(CHIP_FLIPS):
            px, py = _flip(x, fx), _flip(y, fy)
            copies.append(pltpu.make_async_remote_copy(
                src_ref=p_refs[a].at[2 * px + py], dst_ref=got_refs[a].at[k], send_sem=send_sems.at[3 * a + k],
                recv_sem=recv_sems.at[3 * a + k], device_id=(px, py, c), device_id_type=MESH))
    return copies


def _rs_to_chips(partials, name):
    n = len(partials)

    def body(*refs):
        copies = _to_chips_copies(refs[:n], refs[n:2 * n], *refs[2 * n:])
        for cp in copies:
            cp.start()
        for cp in copies:
            cp.wait()

    return _call(
        body, name=name, in_specs=[ANY] * n, out_specs=[ANY] * n,
        out_shape=[jax.ShapeDtypeStruct((3,) + p.shape[1:], p.dtype) for p in partials],
        scratch_shapes=[pltpu.SemaphoreType.DMA((3 * n,)), pltpu.SemaphoreType.DMA((3 * n,))],
    )(*partials)


HBM = pl.BlockSpec(memory_space=pltpu.HBM)
SEM = pl.BlockSpec(memory_space=pltpu.SEMAPHORE)
SIDE_EFFECT = pltpu.CompilerParams(has_side_effects=pltpu.SideEffectType.DATAFLOW_SIDE_EFFECTING)


def _split_start(copies_fn, srcs, land_shapes, n_sems, name, after=None):
    n, m = len(srcs), len(land_shapes)
    extra = [] if after is None else [after]

    def body(*refs):
        outs = refs[n + m + len(extra):]
        send_sems, recv_sems, token = outs[0], outs[1], outs[-1]
        for cp in copies_fn(refs[:n], refs[n:n + m], send_sems, recv_sems):
            cp.start()
        token[...] = jnp.zeros_like(token)

    ins = [pltpu.with_memory_space_constraint(t, pltpu.HBM) for t in list(srcs) + [lax.empty(s.shape, s.dtype) for s in land_shapes]]
    res = _call(
        body, name=name, in_specs=[HBM] * (n + m) + [ANY] * len(extra),
        out_specs=[SEM, SEM] + [HBM] * (n + m) + [pl.BlockSpec(memory_space=pltpu.VMEM)],
        out_shape=[pltpu.SemaphoreType.DMA((n_sems,)), pltpu.SemaphoreType.DMA((n_sems,))]
        + [pltpu.HBM(t.shape, t.dtype) for t in ins] + [jax.ShapeDtypeStruct((8, LANES), F32)],
        input_output_aliases={i: 2 + i for i in range(n + m)}, compiler_params=SIDE_EFFECT,
    )(*ins, *extra)
    return dict(sems=(res[0], res[1]), srcs=res[2:2 + n], lands=res[2 + n:2 + n + m], token=res[-1])


def _split_wait(copies_fn, started, after, name):
    n, m = len(started["srcs"]), len(started["lands"])

    def body(*refs):
        for cp in copies_fn(refs[:n], refs[n:n + m], refs[n + m], refs[n + m + 1]):
            cp.wait_send()
            cp.wait_recv()

    bufs = list(started["srcs"]) + list(started["lands"])
    res = _call(
        body, name=name, in_specs=[HBM] * (n + m) + [SEM, SEM, ANY], out_specs=[HBM] * (n + m),
        out_shape=[pltpu.HBM(t.shape, t.dtype) for t in bufs],
        input_output_aliases={i: i for i in range(n + m)}, compiler_params=SIDE_EFFECT,
    )(*bufs, *started["sems"], after)
    return res[:n], res[n:]


def _to_all_copies(x_refs, out_refs, send_sems, recv_sems):
    x, y, c = _coords()
    copies = []
    for a in range(len(x_refs)):
        for k in range(N_DEV - 1):
            fx, fy, fc = ((k + 1) >> 2) & 1, ((k + 1) >> 1) & 1, (k + 1) & 1
            copies.append(pltpu.make_async_remote_copy(
                src_ref=x_refs[a], dst_ref=out_refs[a].at[4 * x + 2 * y + c], send_sem=send_sems.at[7 * a + k],
                recv_sem=recv_sems.at[7 * a + k], device_id=(_flip(x, fx), _flip(y, fy), _flip(c, fc)), device_id_type=MESH))
    return copies


def _fill_own_block(gathered, shard, me_idx, name):
    rows, cols = shard.shape
    tr = _tile(rows, 512, 16)

    def body(me_ref, g_ref, s_ref, o_ref):
        o_ref[0] = s_ref[...]

    grid_spec = pltpu.PrefetchScalarGridSpec(
        num_scalar_prefetch=1, grid=(rows // tr,),
        in_specs=[ANY, pl.BlockSpec((tr, cols), lambda i, me: (i, 0))],
        out_specs=pl.BlockSpec((1, tr, cols), lambda i, me: (me[0], i, 0)))
    return _call(
        body, name=name, grid_spec=grid_spec, out_shape=jax.ShapeDtypeStruct(gathered.shape, gathered.dtype),
        input_output_aliases={1: 0}, compiler_params=_params(("arbitrary",)),
    )(me_idx, gathered, shard)


def _rs_chip_sum(partial, got, chip_idx, name, part=0, n_parts=1, dst=None):
    _, rows, cols = partial.shape
    tr = _tile(rows, 512, 16)
    steps = rows // tr
    n_dst = 0 if dst is None else 1

    def body(p_idx_ref, p_ref, got_ref, *refs):
        refs[n_dst][...] = ((p_ref[0].astype(F32) + got_ref[0].astype(F32)) + got_ref[1].astype(F32)) + got_ref[2].astype(F32)

    grid_spec = pltpu.PrefetchScalarGridSpec(
        num_scalar_prefetch=1, grid=(steps,),
        in_specs=[pl.BlockSpec((1, tr, cols), lambda i, p_ref: (p_ref[0], i, 0)),
                  pl.BlockSpec((3, tr, cols), lambda i, p_ref: (0, i, 0))] + [ANY] * n_dst,
        out_specs=pl.BlockSpec((tr, cols), lambda i, p_ref: (part * steps + i, 0)))
    return _call(
        body, name=name, grid_spec=grid_spec, out_shape=jax.ShapeDtypeStruct((n_parts * rows, cols), F32),
        input_output_aliases={3: 0} if n_dst else {}, compiler_params=_params(("parallel",)),
    )(chip_idx, partial, got, *([] if dst is None else [dst]))


def _rs_begin(gs, tag, split):
    c_idx = jnp.reshape(lax.axis_index("c"), (1,)).astype(jnp.int32)
    gots = _rs_to_sibling(gs, "rs_to_sibling_" + tag)
    partials = [_rs_pair_sum(g, got, c_idx, "rs_pair_sum_%s%d" % (tag, a)) for a, (g, got) in enumerate(zip(gs, gots))]
    if not split:
        return dict(partials=partials, gots=_rs_to_chips(partials, "rs_to_chips_" + tag))
    lands = [jax.ShapeDtypeStruct((3,) + p.shape[1:], p.dtype) for p in partials]
    return _split_start(_to_chips_copies, partials, lands, 3 * len(partials), "rs_to_chips_start_" + tag)


def _rs_finish(begun, tag, after=None, part=0, n_parts=1, dsts=None):
    x, y, _ = _coords()
    chip_idx = jnp.reshape(2 * x + y, (1,)).astype(jnp.int32)
    if "gots" in begun:
        partials, gots = begun["partials"], begun["gots"]
    else:
        partials, gots = _split_wait(_to_chips_copies, begun, after, "rs_to_chips_wait_" + tag)
    return [_rs_chip_sum(p, got, chip_idx, "rs_chip_sum_%s%d" % (tag, a), part, n_parts, None if dsts is None else dsts[a])
            for a, (p, got) in enumerate(zip(partials, gots))]


RUNS = 3
RUN_FIELDS = 5


def _lane_gather_table(src_of, ahead):
    n_blocks = src_of.shape[0] // LANES
    tab = np.zeros((n_blocks + 3 * ahead, RUNS, RUN_FIELDS), np.int32)
    for t in range(n_blocks):
        runs = []
        for lane in range(LANES):
            slab, col = (int(v) for v in src_of[t * LANES + lane])
            if slab < 0:
                continue
            key = (slab, col // LANES, col % LANES - lane)
            if runs and runs[-1][0] == key and runs[-1][2] == lane:
                runs[-1][2] = lane + 1
            else:
                runs.append([key, lane, lane + 1])
        assert len(runs) <= RUNS
        slots = [None] * RUNS
        for key, lo, hi in sorted(runs, key=lambda r: r[0][:2]):
            slots[slots.index(None)] = (key[0], key[1], key[2], lo, hi)
        for e in range(RUNS):
            kept = (tab[t - ahead, e, 0], tab[t - ahead, e, 1], 0, 0, 0) if t >= ahead else tab[t, e]
            tab[t, e] = slots[e] if slots[e] is not None else kept
    tab[n_blocks:, :, :2] = np.tile(tab[n_blocks - ahead:n_blocks, :, :2], (3, 1, 1))
    return tab.reshape(-1)


def _place_run(tab_ref, t, e, block, under):
    base = (t * RUNS + e) * RUN_FIELDS
    shift, lo, hi = tab_ref[base + 2], tab_ref[base + 3], tab_ref[base + 4]
    lane = lax.broadcasted_iota(jnp.int32, (1, LANES), 1)
    return jnp.where((lane >= lo) & (lane < hi), pltpu.roll(block.astype(F32), (LANES - shift) % LANES, 1), under)


def _lane_gather_cols(src, table, out_slabs, out_width, name, sub):
    _, rows, _ = src.shape
    steps_per_slab = -(-out_width // (sub * LANES))

    def body(tab_ref, *refs):
        o_ref = refs[sub * RUNS]
        for s in range(sub):
            t = pl.program_id(0) * sub + s
            ops = refs[s * RUNS:(s + 1) * RUNS]
            lanes = slice(s * LANES, (s + 1) * LANES)
            o_ref[0, :, lanes] = _place_run(tab_ref, t, 1, ops[1][0], _place_run(tab_ref, t, 0, ops[0][0], 0.0)).astype(BF16)
            last = (t * RUNS + RUNS - 1) * RUN_FIELDS

            @pl.when(tab_ref[last + 4] > tab_ref[last + 3])
            def _():
                o_ref[0, :, lanes] = _place_run(tab_ref, t, RUNS - 1, ops[RUNS - 1][0], o_ref[0, :, lanes].astype(F32)).astype(BF16)

    def src_spec(s, e):
        at = lambda t: ((t * sub + s) * RUNS + e) * RUN_FIELDS
        return pl.BlockSpec((1, rows, LANES), lambda t, tab: (tab[at(t)], 0, tab[at(t) + 1]))

    grid_spec = pltpu.PrefetchScalarGridSpec(
        num_scalar_prefetch=1, grid=(out_slabs * steps_per_slab,),
        in_specs=[src_spec(s, e) for s in range(sub) for e in range(RUNS)],
        out_specs=pl.BlockSpec((1, rows, sub * LANES), lambda t, tab: (t // steps_per_slab, 0, t % steps_per_slab)))
    return _call(
        body, name=name, grid_spec=grid_spec, out_shape=jax.ShapeDtypeStruct((out_slabs, rows, out_width), BF16),
        compiler_params=_params(("arbitrary",)),
    )(jnp.asarray(table), *([src] * (sub * RUNS)))


def _all_reduce_small(vec):
    rows, cols = vec.shape

    def body(v_ref, o_ref, buf, send_sems, recv_sems):
        x, y, c = _coords()
        me = 4 * x + 2 * y + c
        buf[me] = v_ref[...]
        copies = []
        for k in range(N_DEV - 1):
            fx, fy, fc = ((k + 1) >> 2) & 1, ((k + 1) >> 1) & 1, (k + 1) & 1
            cp = pltpu.make_async_remote_copy(
                src_ref=v_ref, dst_ref=buf.at[me], send_sem=send_sems.at[k], recv_sem=recv_sems.at[k],
                device_id=(_flip(x, fx), _flip(y, fy), _flip(c, fc)), device_id_type=MESH)
            cp.start()
            copies.append(cp)
        for cp in copies:
            cp.wait()
        total = buf[0]
        for j in range(1, N_DEV):
            total = total + buf[j]
        o_ref[...] = total

    vmem = pl.BlockSpec(memory_space=pltpu.VMEM)
    return _call(
        body, name="all_reduce_small", in_specs=[vmem], out_specs=vmem,
        out_shape=jax.ShapeDtypeStruct((rows, cols), F32),
        scratch_shapes=[pltpu.VMEM((N_DEV, rows, cols), F32), pltpu.SemaphoreType.DMA((N_DEV - 1,)),
                        pltpu.SemaphoreType.DMA((N_DEV - 1,))],
    )(vec)


def _w_in_column_maps(ns, o_ba, n_logit, n_main, n_all, own_sub):
    own = np.arange(N_DEV * ns)
    work_of_own = np.where(own < o_ba, own, np.where(own < o_ba + n_logit, n_main + own - o_ba, own - n_logit))
    to_work = np.full((n_all, 2), -1, np.int64)
    to_work[work_of_own, 0] = own // ns
    to_work[work_of_own, 1] = own % ns
    slab_width = -(-ns // (own_sub * LANES)) * own_sub * LANES
    to_own = np.full((N_DEV, slab_width, 2), -1, np.int64)
    to_own[:, :ns, 0] = 0
    to_own[:, :ns, 1] = work_of_own.reshape(N_DEV, ns)
    return to_work, to_own.reshape(-1, 2)


def kernel(x, meta_tokens, norm_w, w_in, conv_w, A_log, dt_bias, pool_mix, pool_scale, dn_norm_w, w_pool_out, w_dn_out, w_o, final_norm_w, loss_target, m_meta_tokens, m_norm_w, m_w_in, m_conv_w, m_A_log, m_dt_bias, m_pool_mix, m_pool_scale, m_dn_norm_w, m_w_pool_out, m_w_dn_out, m_w_o, m_final_norm_w, v_meta_tokens, v_norm_w, v_w_in, v_conv_w, v_A_log, v_dt_bias, v_pool_mix, v_pool_scale, v_dn_norm_w, v_w_pool_out, v_w_dn_out, v_w_o, v_final_norm_w):
    seq, d = x.shape[1], x.shape[2]
    n_meta = meta_tokens.shape[0]
    n_heads, hd = A_log.shape[-1], dn_norm_w.shape[-1]
    dn = n_heads * hd
    pw, ng = pool_scale.shape[-1], pool_mix.shape[1]
    pg = pw // ng
    kw = conv_w.shape[1]
    pad = (-n_meta) % CHUNK
    x0 = pad + n_meta
    lp = x0 + seq
    ns = w_in.shape[-1]
    in_cols = N_DEV * ns
    o_q, o_k, o_v, o_zd = 2 * pw, 2 * pw + dn, 2 * pw + 2 * dn, 2 * pw + 3 * dn
    o_ba = 2 * pw + 4 * dn
    o_gp, o_gd = o_ba, o_ba + d
    n_main = o_gd + d
    n_all = n_main + 2 * LANES
    assert lp % CHUNK == 0 and in_cols == n_main + 2 * n_heads and 2 * n_heads <= LANES and hd == LANES
    cs, ms = conv_w.shape[-1], meta_tokens.shape[-1]
    mr = pool_mix.shape[2]
    assert ms == pg and cs % pg == 0
    work_sub = max(s for s in (6, 3, 2, 1) if (n_all // LANES) % s == 0)
    own_sub = 15
    to_work, to_own = _w_in_column_maps(ns, o_ba, 2 * n_heads, n_main, n_all, own_sub)
    cols_major = lambda t: jnp.transpose(t, (1, 0, 2)).reshape(t.shape[1], N_DEV * t.shape[2])

    mix_g, conv_g, meta_g = _all_gather([pool_mix[0].reshape(ng * mr, pg).astype(BF16), conv_w[0], meta_tokens])
    win_g = _all_gather_tree(w_in[0].astype(BF16), after=meta_g)
    late_shards = [w_pool_out[0].astype(BF16), w_dn_out[0].astype(BF16), w_o[0].astype(BF16)]
    late_weights = _split_start(_to_all_copies, late_shards, [jax.ShapeDtypeStruct((N_DEV,) + s.shape, BF16) for s in late_shards],
                                (N_DEV - 1) * len(late_shards), "gather_out_proj_start", after=win_g)
    norm_w_in = norm_w + late_weights["token"][0, 0]
    w_all = _lane_gather_cols(win_g, _lane_gather_table(to_work, work_sub), 1, n_all, "w_in_to_work", work_sub).reshape(d, n_all)
    mix_f = jnp.transpose(mix_g.reshape(N_DEV, ng, mr, pg), (1, 0, 2, 3)).reshape(ng, pg, pg)
    conv_f = cols_major(conv_g)
    meta_f = cols_major(meta_g)

    h0, xn = _norm_in_fwd(x[0], meta_f, norm_w_in, pad)
    proj = _matmul(xn, w_all, NN, F32, lp, 768, 2048, "proj")
    y_pool = _pool_fwd(proj, mix_f, pool_scale, pad)
    conv_q, conv_k, conv_v = (conv_f[:, i * dn:(i + 1) * dn] for i in range(3))
    qn = _conv_fwd(proj, o_q, conv_q, hd, float(hd) ** -0.5, "conv_q_fwd")
    kn = _conv_fwd(proj, o_k, conv_k, hd, 1.0, "conv_k_fwd")
    vv = _conv_fwd(proj, o_v, conv_v, hd, None, "conv_v_fwd")
    logit_lanes = (n_heads, LANES - 2 * n_heads)
    prm = jnp.pad(A_log, ((0, 7), logit_lanes)) + jnp.pad(dt_bias, ((1, 6), logit_lanes))
    y_dn, hist, tmats = _chunk_fwd(qn, kn, vv, proj, o_zd, n_main, prm, dn_norm_w, n_heads, pad)
    me_idx = jnp.reshape(4 * lax.axis_index("x") + 2 * lax.axis_index("y") + lax.axis_index("c"), (1,)).astype(jnp.int32)
    _, landed = _split_wait(_to_all_copies, late_weights, y_dn, "gather_out_proj_wait")
    wpo_g, wdn_g, wo_g = (_fill_own_block(g, s, me_idx, "own_block_%d" % i) for i, (g, s) in enumerate(zip(landed, late_shards)))
    wpo_f = cols_major(wpo_g)
    wdn_f = wdn_g.reshape(dn, d)
    wo_f = wo_g.reshape(d, d)
    p_out = _matmul(y_pool, wpo_f, NN, F32, 1056, 1024, 1024, "pool_out")
    q_out = _matmul(y_dn, wdn_f, NN, F32, 1056, 1024, 2048, "dn_out")
    merged = _merge_fwd(p_out, q_out, proj, o_gp, o_gd)
    mo = _matmul(merged, wo_f, NN, F32, 1056, 1024, 2048, "w_o_fwd")
    dh1, d_fw, loss_part = _final_loss(h0, mo, final_norm_w.reshape(1, d), loss_target[0], x0)

    d_merged = _matmul(dh1, wo_f, NT, F32, 1056, 1024, 2048, "w_o_bwd_x")
    g_wo = _matmul(merged, dh1, TN, BF16, 1024, 1024, lp, "w_o_bwd_w")
    d_p, d_q, d_gp, d_gd = _merge_bwd(p_out, q_out, proj, o_gp, o_gd, d_merged)
    d_ypool = _matmul(d_p, wpo_f, NT, F32, 1056, 1024, 2048, "pool_out_bwd_x")
    g_wpo = _matmul(y_pool.T, d_p, NN, BF16, 1024, 1024, lp, "pool_out_bwd_w", col_blocks=N_DEV)
    d_ydn = _matmul(d_q, wdn_f, NT, F32, 1056, 1024, 2048, "dn_out_bwd_x")
    g_wdn = _matmul(y_dn, d_q, TN, BF16, 1024, 1024, lp, "dn_out_bwd_w")
    rs_early = _rs_begin([g_wpo, g_wdn.reshape(N_DEV, dn // N_DEV, d), g_wo.reshape(N_DEV, d // N_DEV, d)], "early", split=True)
    started = rs_early["token"][0, 0]
    d_u, d_zp, g_mix, g_pscale = _pool_bwd(proj, mix_f, pool_scale + started, d_ypool, pad)
    d_proj = lax.empty((lp, n_all), BF16)
    d_qn, d_kn, d_vv, d_ba, d_proj, d_prm, g_dnw = _chunk_bwd(qn, kn, vv, proj, o_zd, n_main, prm + started, dn_norm_w, hist, tmats,
                                                              d_ydn, n_heads, pad, d_proj)
    d_proj, g_cq = _conv_bwd(proj, o_q, conv_q, d_qn, hd, float(hd) ** -0.5, pad, "conv_q_bwd", d_proj)
    d_proj, g_ck = _conv_bwd(proj, o_k, conv_k, d_kn, hd, 1.0, pad, "conv_k_bwd", d_proj)
    d_proj, g_cv = _conv_bwd(proj, o_v, conv_v, d_vv, hd, None, pad, "conv_v_bwd", d_proj)
    for off, piece in ((0, d_u), (pw, d_zp), (o_gp, d_gp), (o_gd, d_gd), (n_main, d_ba.astype(BF16)), (n_main + LANES, jnp.zeros((lp, LANES), BF16))):
        d_proj = lax.dynamic_update_slice(d_proj, piece, (0, off))
    xn_t, rs_late, token = xn.T, [], None
    for half in range(2):
        rows = slice(half * (d // 2), (half + 1) * (d // 2))
        g_wall = _matmul(xn_t[rows], d_proj, NN, BF16, 1024, 768, lp, "w_in_bwd_w_%d" % half, after=token)
        g_win = _lane_gather_cols(g_wall.reshape(1, d // 2, n_all), _lane_gather_table(to_own, own_sub), N_DEV, ns,
                                  "w_in_grad_to_own_%d" % half, own_sub)
        rs_late.append(_rs_begin([g_win], "late%d" % half, split=True))
        token = rs_late[-1]["token"]
    d_xn = _matmul(d_proj, w_all, NT, F32, lp, 512, 2432, "w_in_bwd_x", after=token)
    d_head, grad_x, g_nw = _norm_in_bwd(h0, norm_w, d_xn, dh1, x0)
    grad_x = grad_x[None]

    by_cols = lambda t: jnp.transpose(t.reshape(t.shape[0], N_DEV, t.shape[1] // N_DEV), (1, 0, 2))
    g_conv = by_cols(jnp.concatenate([g_cq, g_ck, g_cv], axis=1)).reshape(N_DEV, kw * cs // pg, pg)
    conv_rows = -(-g_conv.shape[1] // 16) * 16
    g_small = jnp.concatenate(
        [jnp.transpose(g_mix.reshape(ng, N_DEV, mr, pg), (1, 0, 2, 3)).reshape(N_DEV, ng * mr, pg), by_cols(d_head[pad:x0]),
         jnp.pad(g_conv, ((0, 0), (0, conv_rows - g_conv.shape[1]), (0, 0)))], axis=1).astype(BF16)
    r_small, = _rs_finish(_rs_begin([g_small], "small", split=False), "small")
    r_mix, r_meta = r_small[:ng * mr], r_small[ng * mr:ng * mr + n_meta]
    r_conv = r_small[ng * mr + n_meta:ng * mr + n_meta + kw * cs // pg]
    r_wpo, r_wdn, r_wo = _rs_finish(rs_early, "early", after=r_small)

    small = [g_nw[0], d_fw[0], g_pscale[0], g_dnw[0], d_prm[0], d_prm[1], loss_part[0]]
    s_sizes = [t.shape[0] for t in small]
    s_cols = -(-sum(s_sizes) // (8 * LANES)) * LANES
    s_vec = jnp.concatenate(small + [jnp.zeros((8 * s_cols - sum(s_sizes),), F32)]).reshape(8, s_cols)
    s_red = _all_reduce_small(s_vec)
    s_sum = s_red.reshape(-1)
    r_win = None
    for half, begun in enumerate(rs_late):
        r_win = _rs_finish(begun, "late%d" % half, after=s_red, part=half, n_parts=2, dsts=r_win)
    r_win, = r_win
    s_offs = [sum(s_sizes[:i]) for i in range(len(s_sizes))]
    s_take = lambda i, n=None, o=0: s_sum[s_offs[i] + o:s_offs[i] + o + (s_sizes[i] if n is None else n)]

    grads = {
        "meta_tokens": r_meta, "norm_w": s_take(0).reshape(norm_w.shape),
        "w_in": r_win.reshape(w_in.shape), "conv_w": r_conv.reshape(conv_w.shape),
        "A_log": s_take(4, n_heads, n_heads).reshape(A_log.shape), "dt_bias": s_take(5, n_heads, n_heads).reshape(dt_bias.shape),
        "pool_mix": r_mix.reshape(pool_mix.shape), "pool_scale": s_take(2).reshape(pool_scale.shape),
        "dn_norm_w": s_take(3).reshape(dn_norm_w.shape), "w_pool_out": r_wpo.reshape(w_pool_out.shape),
        "w_dn_out": r_wdn.reshape(w_dn_out.shape), "w_o": r_wo.reshape(w_o.shape),
        "final_norm_w": s_take(1).reshape(final_norm_w.shape),
    }
    loss = s_take(6, 1)[0]

    weights = dict(meta_tokens=meta_tokens, norm_w=norm_w, w_in=w_in, conv_w=conv_w, A_log=A_log, dt_bias=dt_bias,
                   pool_mix=pool_mix, pool_scale=pool_scale, dn_norm_w=dn_norm_w, w_pool_out=w_pool_out, w_dn_out=w_dn_out,
                   w_o=w_o, final_norm_w=final_norm_w)
    m_in = dict(meta_tokens=m_meta_tokens, norm_w=m_norm_w, w_in=m_w_in, conv_w=m_conv_w, A_log=m_A_log, dt_bias=m_dt_bias,
                pool_mix=m_pool_mix, pool_scale=m_pool_scale, dn_norm_w=m_dn_norm_w, w_pool_out=m_w_pool_out,
                w_dn_out=m_w_dn_out, w_o=m_w_o, final_norm_w=m_final_norm_w)
    v_in = dict(meta_tokens=v_meta_tokens, norm_w=v_norm_w, w_in=v_w_in, conv_w=v_conv_w, A_log=v_A_log, dt_bias=v_dt_bias,
                pool_mix=v_pool_mix, pool_scale=v_pool_scale, dn_norm_w=v_dn_norm_w, w_pool_out=v_w_pool_out,
                w_dn_out=v_w_dn_out, w_o=v_w_o, final_norm_w=v_final_norm_w)
    names = list(weights)
    upd = {n: _adamw(weights[n], grads[n], m_in[n], v_in[n], "adamw_" + n) for n in names}
    return (loss, grad_x, *[grads[n] for n in names], *[upd[n][0] for n in names], *[upd[n][1] for n in names],
            *[upd[n][2] for n in names])
```
